```python
import jax, jax.numpy as jnp
from jax import lax
import numpy as np

D_MODEL = 1024
BATCH = 8
SEQ = 16384
DEPTH = 1

GRID_W = 64
NA_W = D_MODEL // 2
NA_HEAD_DIM = 64
NA_HEADS = NA_W // NA_HEAD_DIM
NA_KH = 8
NA_KW = 16
ML_W = D_MODEL // 2
ML_HEADS = 4
ML_HEAD_DIM = ML_W // ML_HEADS
MIX_W = NA_W + ML_W
ML_CHUNK = 64
CONV_W = 5
EPS = 1e-6
IN_SPLITS = (NA_W,) * 4 + (ML_W,) * 5 + (ML_HEADS,) * 4
IN_W = 4 * NA_W + 5 * ML_W + 4 * ML_HEADS

kernel_name = 'hybrid_natten2d_bimlstm_block'


def rmsnorm(x, w):
    x32 = x.astype(jnp.float32)
    return x32 * lax.rsqrt(jnp.mean(x32 * x32, axis=-1, keepdims=True) + EPS) * w


def centred_short_conv(u, w, b):
    T = u.shape[1]
    pad = CONV_W // 2
    up = jnp.pad(u, ((0, 0), (pad, pad), (0, 0)))
    out = up[:, 0:T] * w[0]
    for j in range(1, CONV_W):
        out = out + up[:, j:j + T] * w[j]
    return out + b


def neighbourhood_attention(q, k, v, rpb):
    B, T, _ = q.shape
    rows = T // GRID_W
    kh = min(NA_KH, rows)

    def to_grid(t):
        return t.reshape(B, rows, GRID_W, NA_HEADS, NA_HEAD_DIM).transpose(0, 3, 1, 2, 4)

    qg = to_grid(q) * (NA_HEAD_DIM ** -0.5)
    kgrid, vgrid = to_grid(k), to_grid(v)
    r = jnp.arange(rows)
    row_idx = jnp.clip(r - kh // 2, 0, rows - kh)[:, None] + jnp.arange(kh)[None, :]
    kg = kgrid[:, :, row_idx]
    vg = vgrid[:, :, row_idx]
    col = jnp.arange(GRID_W)
    col_start = jnp.clip(col - NA_KW // 2, 0, GRID_W - NA_KW)
    col_mask = (col[None, :] >= col_start[:, None]) & (col[None, :] < col_start[:, None] + NA_KW)
    dy = row_idx - r[:, None] + NA_KH - 1
    dx = jnp.clip(col[None, :] - col[:, None], -(NA_KW - 1), NA_KW - 1) + NA_KW - 1
    bias = rpb[:, dy[:, None, :, None], dx[None, :, None, :]]
    s = jnp.einsum('bhrcd,bhrjkd->bhrcjk', qg, kg).astype(jnp.float32) + bias[None]
    s = jnp.where(col_mask[:, None, :], s, -jnp.inf)
    p = jax.nn.softmax(s.reshape(B, NA_HEADS, rows, GRID_W, kh * GRID_W), axis=-1).reshape(s.shape)
    o = jnp.einsum('bhrcjk,bhrjkd->bhrcd', p, vg)
    return o.transpose(0, 2, 3, 1, 4).reshape(B, T, NA_W)


def mlstm_chunkwise(q, k, v, i_pre, f_pre):
    B, H, T, d = q.shape
    L = ML_CHUNK
    nc = T // L
    q = q.reshape(B, H, nc, L, d)
    k = (k * (d ** -0.5)).reshape(B, H, nc, L, d)
    v = v.reshape(B, H, nc, L, d)
    i_pre = i_pre.reshape(B, H, nc, L)
    b = jnp.cumsum(jax.nn.log_sigmoid(f_pre).reshape(B, H, nc, L), axis=-1)
    b_last = b[..., -1]
    a = b_last[..., None] - b + i_pre

    def step(carry, inp):
        C, n, m = carry
        k_c, v_c, a_c, bl_c = inp
        m_new = jnp.maximum(bl_c + m, jnp.max(a_c, axis=-1))
        decay = jnp.exp(bl_c + m - m_new)
        w = jnp.exp(a_c - m_new[..., None])
        C_new = decay[..., None, None] * C + jnp.einsum('bhs,bhse,bhsd->bhed', w, v_c, k_c)
        n_new = decay[..., None] * n + jnp.einsum('bhs,bhsd->bhd', w, k_c)
        return (C_new, n_new, m_new), (C, n, m)

    init = (jnp.zeros((B, H, d, d), q.dtype), jnp.zeros((B, H, d), q.dtype), jnp.zeros((B, H), q.dtype))
    xs = (jnp.moveaxis(k, 2, 0), jnp.moveaxis(v, 2, 0), jnp.moveaxis(a, 2, 0), jnp.moveaxis(b_last, 2, 0))
    _, (C_prev, n_prev, m_prev) = lax.scan(step, init, xs)
    C_prev = jnp.moveaxis(C_prev, 0, 2)
    n_prev = jnp.moveaxis(n_prev, 0, 2)
    m_prev = jnp.moveaxis(m_prev, 0, 2)

    lower = jnp.tril(jnp.ones((L, L), dtype=bool))
    Dlog = jnp.where(lower, b[..., :, None] - b[..., None, :] + i_pre[..., None, :], -jnp.inf)
    m_inter = b + m_prev[..., None]
    m_t = jnp.maximum(m_inter, jnp.max(Dlog, axis=-1))
    S = jnp.einsum('bhntd,bhnsd->bhnts', q, k) * jnp.exp(Dlog - m_t[..., None])
    inter = jnp.exp(m_inter - m_t)
    num = jnp.einsum('bhnts,bhnse->bhnte', S, v) + inter[..., None] * jnp.einsum('bhned,bhntd->bhnte', C_prev, q)
    den = jnp.sum(S, axis=-1) + inter * jnp.einsum('bhnd,bhntd->bhnt', n_prev, q)
    h = num / jnp.maximum(jnp.abs(den), jnp.exp(-m_t))[..., None]
    return h.reshape(B, H, T, d)


def hybrid_mixer(h, w_in, b_in, conv_w, conv_b, rpb, ml_norm_w, w_out):
    B, T, _ = h.shape
    proj = h @ w_in + b_in
    split_idx = np.cumsum(IN_SPLITS)[:-1].tolist()
    (na_q, na_k, na_v, na_z, ml_q, ml_k, ml_v, ml_o, ml_z,
     i_f, f_f, i_b, f_b) = jnp.split(proj, split_idx, axis=-1)

    na_out = neighbourhood_attention(na_q, na_k, na_v, rpb) * jax.nn.silu(na_z)

    qk = jax.nn.silu(centred_short_conv(jnp.concatenate([ml_q, ml_k], axis=-1), conv_w, conv_b))
    mq, mk = jnp.split(qk, 2, axis=-1)

    def heads(t):
        return t.reshape(B, T, ML_HEADS, ML_HEAD_DIM).transpose(0, 2, 1, 3)

    mq, mk, mv = heads(mq), heads(mk), heads(ml_v)
    i_f, f_f, i_b, f_b = (g.transpose(0, 2, 1) for g in (i_f, f_f, i_b, f_b))
    h_fwd = mlstm_chunkwise(mq, mk, mv, i_f, f_f)
    flip = lambda t: jnp.flip(t, axis=2)
    h_bwd = flip(mlstm_chunkwise(flip(mq), flip(mk), flip(mv), flip(i_b), flip(f_b)))
    hm = (h_fwd + h_bwd).transpose(0, 2, 1, 3) * jax.nn.sigmoid(ml_o).reshape(B, T, ML_HEADS, ML_HEAD_DIM)
    mu = jnp.mean(hm, axis=-1, keepdims=True)
    var = jnp.mean(jnp.square(hm - mu), axis=-1, keepdims=True)
    hm = ((hm - mu) * lax.rsqrt(var + EPS)).reshape(B, T, ML_W) * ml_norm_w
    ml_out = hm * jax.nn.silu(ml_z)

    return jnp.concatenate([na_out, ml_out], axis=-1) @ w_out


def _fwd_setup_inputs(seed: int = 0) -> dict:
    key = jax.random.key(seed)
    ks = jax.random.split(key, 14)
    D = D_MODEL
    x = jax.random.normal(ks[0], (BATCH, SEQ, D), jnp.float32)
    c = jax.random.normal(ks[1], (BATCH, D), jnp.float32)
    w_ada = jax.random.normal(ks[2], (DEPTH, D, 3 * D), jnp.float32) * (0.5 * D ** -0.5)
    b_ada = jax.random.normal(ks[3], (DEPTH, 3 * D), jnp.float32) * 0.02
    norm_w = 1.0 + 0.1 * jax.random.normal(ks[4], (DEPTH, D), jnp.float32)
    w_in = jax.random.normal(ks[5], (DEPTH, D, IN_W), jnp.float32) * (D ** -0.5)
    n_main = 4 * NA_W + 5 * ML_W
    fgate_base = jnp.linspace(3.0, 6.0, ML_HEADS, dtype=jnp.float32)
    gk = jax.random.split(ks[6], 5)
    b_in = jnp.concatenate([
        0.02 * jax.random.normal(gk[0], (DEPTH, n_main), jnp.float32),
        0.1 * jax.random.normal(gk[1], (DEPTH, ML_HEADS), jnp.float32),
        fgate_base + 0.1 * jax.random.normal(gk[2], (DEPTH, ML_HEADS), jnp.float32),
        0.1 * jax.random.normal(gk[3], (DEPTH, ML_HEADS), jnp.float32),
        fgate_base + 0.1 * jax.random.normal(gk[4], (DEPTH, ML_HEADS), jnp.float32),
    ], axis=-1)
    conv_w = jax.random.normal(ks[7], (DEPTH, CONV_W, 2 * ML_W), jnp.float32) * (CONV_W ** -0.5)
    conv_b = 0.02 * jax.random.normal(ks[8], (DEPTH, 2 * ML_W), jnp.float32)
    rpb = 0.1 * jax.random.normal(ks[9], (DEPTH, NA_HEADS, 2 * NA_KH - 1, 2 * NA_KW - 1), jnp.float32)
    ml_norm_w = 1.0 + 0.1 * jax.random.normal(ks[10], (DEPTH, ML_W), jnp.float32)
    w_out = jax.random.normal(ks[11], (DEPTH, MIX_W, D), jnp.float32) * (MIX_W ** -0.5)
    final_norm_w = 1.0 + 0.1 * jax.random.normal(ks[12], (D,), jnp.float32)
    return {'x': x, 'c': c, 'w_ada': w_ada, 'b_ada': b_ada, 'norm_w': norm_w, 'w_in': w_in,
            'b_in': b_in, 'conv_w': conv_w, 'conv_b': conv_b, 'rpb': rpb, 'ml_norm_w': ml_norm_w,
            'w_out': w_out, 'final_norm_w': final_norm_w}


def _fwd_reference(x, c, w_ada, b_ada, norm_w, w_in, b_in, conv_w, conv_b, rpb, ml_norm_w, w_out, final_norm_w):
    h_res = x.astype(jnp.float32)
    c_act = jax.nn.silu(c.astype(jnp.float32))
    for l in range(DEPTH):
        mod = c_act @ w_ada[l] + b_ada[l]
        shift, scale, gate = jnp.split(mod, 3, axis=-1)
        h = rmsnorm(h_res, norm_w[l]) * (1.0 + scale[:, None, :]) + shift[:, None, :]
        y = hybrid_mixer(h, w_in[l], b_in[l], conv_w[l], conv_b[l], rpb[l], ml_norm_w[l], w_out[l])
        h_res = h_res + gate[:, None, :] * y
    return rmsnorm(h_res, final_norm_w).astype(x.dtype)


import jax as _jax
import jax.numpy as _jnp

TWIN_FORMAT = 'train_step'
FWD_PARAMS = ['x', 'c', 'w_ada', 'b_ada', 'norm_w', 'w_in', 'b_in', 'conv_w', 'conv_b', 'rpb', 'ml_norm_w', 'w_out', 'final_norm_w']
TWIN_WEIGHTS = ['w_ada', 'b_ada', 'norm_w', 'w_in', 'b_in', 'conv_w', 'conv_b', 'rpb', 'ml_norm_w', 'w_out', 'final_norm_w']
TWIN_DIFF_INPUT = 'x'
TWIN_INPUTS = ['x', 'c', 'w_ada', 'b_ada', 'norm_w', 'w_in', 'b_in', 'conv_w', 'conv_b', 'rpb', 'ml_norm_w', 'w_out', 'final_norm_w', 'loss_target', 'm_w_ada', 'm_b_ada', 'm_norm_w', 'm_w_in', 'm_b_in', 'm_conv_w', 'm_conv_b', 'm_rpb', 'm_ml_norm_w', 'm_w_out', 'm_final_norm_w', 'v_w_ada', 'v_b_ada', 'v_norm_w', 'v_w_in', 'v_b_in', 'v_conv_w', 'v_conv_b', 'v_rpb', 'v_ml_norm_w', 'v_w_out', 'v_final_norm_w']
TWIN_OUTPUTS = ['loss', 'grad_x', 'grad_w_ada', 'grad_b_ada', 'grad_norm_w', 'grad_w_in', 'grad_b_in', 'grad_conv_w', 'grad_conv_b', 'grad_rpb', 'grad_ml_norm_w', 'grad_w_out', 'grad_final_norm_w', 'delta_w_ada', 'delta_b_ada', 'delta_norm_w', 'delta_w_in', 'delta_b_in', 'delta_conv_w', 'delta_conv_b', 'delta_rpb', 'delta_ml_norm_w', 'delta_w_out', 'delta_final_norm_w', 'new_m_w_ada', 'new_m_b_ada', 'new_m_norm_w', 'new_m_w_in', 'new_m_b_in', 'new_m_conv_w', 'new_m_conv_b', 'new_m_rpb', 'new_m_ml_norm_w', 'new_m_w_out', 'new_m_final_norm_w', 'new_v_w_ada', 'new_v_b_ada', 'new_v_norm_w', 'new_v_w_in', 'new_v_b_in', 'new_v_conv_w', 'new_v_conv_b', 'new_v_rpb', 'new_v_ml_norm_w', 'new_v_w_out', 'new_v_final_norm_w']
TWIN_LEAF_KINDS = {'loss': 'loss', 'grad_x': 'grad_x', 'grad_w_ada': 'grad_w', 'grad_b_ada': 'grad_w', 'grad_norm_w': 'grad_w', 'grad_w_in': 'grad_w', 'grad_b_in': 'grad_w', 'grad_conv_w': 'grad_w', 'grad_conv_b': 'grad_w', 'grad_rpb': 'grad_w', 'grad_ml_norm_w': 'grad_w', 'grad_w_out': 'grad_w', 'grad_final_norm_w': 'grad_w', 'delta_w_ada': 'delta_w', 'delta_b_ada': 'delta_w', 'delta_norm_w': 'delta_w', 'delta_w_in': 'delta_w', 'delta_b_in': 'delta_w', 'delta_conv_w': 'delta_w', 'delta_conv_b': 'delta_w', 'delta_rpb': 'delta_w', 'delta_ml_norm_w': 'delta_w', 'delta_w_out': 'delta_w', 'delta_final_norm_w': 'delta_w', 'new_m_w_ada': 'new_m', 'new_m_b_ada': 'new_m', 'new_m_norm_w': 'new_m', 'new_m_w_in': 'new_m', 'new_m_b_in': 'new_m', 'new_m_conv_w': 'new_m', 'new_m_conv_b': 'new_m', 'new_m_rpb': 'new_m', 'new_m_ml_norm_w': 'new_m', 'new_m_w_out': 'new_m', 'new_m_final_norm_w': 'new_m', 'new_v_w_ada': 'new_v', 'new_v_b_ada': 'new_v', 'new_v_norm_w': 'new_v', 'new_v_w_in': 'new_v', 'new_v_b_in': 'new_v', 'new_v_conv_w': 'new_v', 'new_v_conv_b': 'new_v', 'new_v_rpb': 'new_v', 'new_v_ml_norm_w': 'new_v', 'new_v_w_out': 'new_v', 'new_v_final_norm_w': 'new_v'}


def _forward(args):
    return _fwd_reference(*[args[k] for k in FWD_PARAMS])


def _output_shape():
    def fwd():
        inp = _fwd_setup_inputs(0)
        return _fwd_reference(*[inp[k] for k in FWD_PARAMS])
    out = _jax.eval_shape(fwd)
    return out.shape, out.dtype

N_MICROBATCH = 1
ADAM_LR = 0.001
ADAM_B1 = 0.9
ADAM_B2 = 0.999
ADAM_EPS = 1e-08
ADAM_WD = 0.01
ADAM_STEP = 10
PER_EXAMPLE_BATCH_AXIS = {'x': 0, 'c': 0, 'loss_target': 0}
SHARED_INPUTS = []
_WEIGHT_DTYPES = {'w_ada': _jnp.float32, 'b_ada': _jnp.float32, 'norm_w': _jnp.float32, 'w_in': _jnp.float32, 'b_in': _jnp.float32, 'conv_w': _jnp.float32, 'conv_b': _jnp.float32, 'rpb': _jnp.float32, 'ml_norm_w': _jnp.float32, 'w_out': _jnp.float32, 'final_norm_w': _jnp.float32}
MOMENT_SCALE = {'w_ada': 1.486208e-01, 'b_ada': 3.027360e-01, 'norm_w': 7.831151e-02, 'w_in': 4.373613e-02, 'b_in': 1.017431e-01, 'conv_w': 2.854766e-02, 'conv_b': 2.518579e-02, 'rpb': 5.371041e-03, 'ml_norm_w': 1.105570e-01, 'w_out': 5.768280e-02, 'final_norm_w': 1.292434e+02}


def _to_microbatches(a, axis):
    t = _jnp.moveaxis(a, axis, 0)
    t = t.reshape((N_MICROBATCH, t.shape[0] // N_MICROBATCH) + t.shape[1:])
    return _jnp.moveaxis(t, 1, axis + 1)


def setup_inputs(seed: int = 0) -> dict:
    inp = _fwd_setup_inputs(seed)
    key = _jax.random.fold_in(_jax.random.key(seed), 7919)
    shape, _ = _output_shape()
    out = dict(inp)
    out["loss_target"] = _jax.random.normal(_jax.random.fold_in(key, 0), shape, _jnp.float32)
    for i, name in enumerate(TWIN_WEIGHTS):
        w = inp[name].astype(_jnp.float32)
        if MOMENT_SCALE is None:
            s = _jnp.sqrt(_jnp.mean(_jnp.square(w)) + 1e-30)
        else:
            s = MOMENT_SCALE[name]
        km, kv = _jax.random.split(_jax.random.fold_in(key, i + 1))
        out[name] = w
        out["m_" + name] = s * _jax.random.normal(km, w.shape, _jnp.float32)
        out["v_" + name] = (s * s) * _jax.random.uniform(kv, w.shape, _jnp.float32, 0.5, 1.5)
    if N_MICROBATCH > 1:
        for name, axis in PER_EXAMPLE_BATCH_AXIS.items():
            out[name] = _to_microbatches(out[name], axis)
    return {'x': out['x'], 'c': out['c'], 'w_ada': out['w_ada'], 'b_ada': out['b_ada'], 'norm_w': out['norm_w'], 'w_in': out['w_in'], 'b_in': out['b_in'], 'conv_w': out['conv_w'], 'conv_b': out['conv_b'], 'rpb': out['rpb'], 'ml_norm_w': out['ml_norm_w'], 'w_out': out['w_out'], 'final_norm_w': out['final_norm_w'], 'loss_target': out['loss_target'], 'm_w_ada': out['m_w_ada'], 'm_b_ada': out['m_b_ada'], 'm_norm_w': out['m_norm_w'], 'm_w_in': out['m_w_in'], 'm_b_in': out['m_b_in'], 'm_conv_w': out['m_conv_w'], 'm_conv_b': out['m_conv_b'], 'm_rpb': out['m_rpb'], 'm_ml_norm_w': out['m_ml_norm_w'], 'm_w_out': out['m_w_out'], 'm_final_norm_w': out['m_final_norm_w'], 'v_w_ada': out['v_w_ada'], 'v_b_ada': out['v_b_ada'], 'v_norm_w': out['v_norm_w'], 'v_w_in': out['v_w_in'], 'v_b_in': out['v_b_in'], 'v_conv_w': out['v_conv_w'], 'v_conv_b': out['v_conv_b'], 'v_rpb': out['v_rpb'], 'v_ml_norm_w': out['v_ml_norm_w'], 'v_w_out': out['v_w_out'], 'v_final_norm_w': out['v_final_norm_w']}


def _loss(weights, diff, rest, loss_target):
    with _jax.named_scope("forward"):
        args = {**rest, TWIN_DIFF_INPUT: diff, **{k: w.astype(_WEIGHT_DTYPES[k]) for k, w in weights.items()}}
        y = _forward(args)
    with _jax.named_scope("loss_head"):
        err = _jnp.square(y.astype(_jnp.float32) - loss_target)
        return 0.5 * _jnp.sum(_jnp.mean(err, axis=-1)) if err.ndim else 0.5 * err


def _adamw(w, g, m, v):
    m = ADAM_B1 * m + (1.0 - ADAM_B1) * g
    v = ADAM_B2 * v + (1.0 - ADAM_B2) * _jnp.square(g)
    m_hat = m / (1.0 - ADAM_B1 ** ADAM_STEP)
    v_hat = v / (1.0 - ADAM_B2 ** ADAM_STEP)
    delta = -ADAM_LR * (m_hat / (_jnp.sqrt(v_hat) + ADAM_EPS) + ADAM_WD * w)
    return delta, m, v


def reference(x, c, w_ada, b_ada, norm_w, w_in, b_in, conv_w, conv_b, rpb, ml_norm_w, w_out, final_norm_w, loss_target, m_w_ada, m_b_ada, m_norm_w, m_w_in, m_b_in, m_conv_w, m_conv_b, m_rpb, m_ml_norm_w, m_w_out, m_final_norm_w, v_w_ada, v_b_ada, v_norm_w, v_w_in, v_b_in, v_conv_w, v_conv_b, v_rpb, v_ml_norm_w, v_w_out, v_final_norm_w):
    given = dict(x=x, c=c, w_ada=w_ada, b_ada=b_ada, norm_w=norm_w, w_in=w_in, b_in=b_in, conv_w=conv_w, conv_b=conv_b, rpb=rpb, ml_norm_w=ml_norm_w, w_out=w_out, final_norm_w=final_norm_w, loss_target=loss_target, m_w_ada=m_w_ada, m_b_ada=m_b_ada, m_norm_w=m_norm_w, m_w_in=m_w_in, m_b_in=m_b_in, m_conv_w=m_conv_w, m_conv_b=m_conv_b, m_rpb=m_rpb, m_ml_norm_w=m_ml_norm_w, m_w_out=m_w_out, m_final_norm_w=m_final_norm_w, v_w_ada=v_w_ada, v_b_ada=v_b_ada, v_norm_w=v_norm_w, v_w_in=v_w_in, v_b_in=v_b_in, v_conv_w=v_conv_w, v_conv_b=v_conv_b, v_rpb=v_rpb, v_ml_norm_w=v_ml_norm_w, v_w_out=v_w_out, v_final_norm_w=v_final_norm_w)
    weights = {n: given[n] for n in TWIN_WEIGHTS}
    shared = {n: given[n] for n in SHARED_INPUTS}
    per_example = {n: given[n] for n in ['x', 'c']}
    grad_fn = _jax.value_and_grad(_loss, argnums=(0, 1))

    def one_microbatch(ex, loss_target):
        ex = dict(ex)
        diff = ex.pop(TWIN_DIFF_INPUT)
        return grad_fn(weights, diff, {**shared, **ex}, loss_target)

    if N_MICROBATCH == 1:
        loss, (grad_w, grad_x) = one_microbatch(per_example, given["loss_target"])
    else:
        def body(carry, xs):
            loss_sum, grad_sum = carry
            l_k, (gw_k, gx_k) = one_microbatch(xs[0], xs[1])
            with _jax.named_scope("update"):
                return (loss_sum + l_k, _jax.tree.map(_jnp.add, grad_sum, gw_k)), gx_k

        init = (_jnp.zeros((), _jnp.float32), _jax.tree.map(_jnp.zeros_like, weights))
        (loss, grad_w), grad_x = _jax.lax.scan(body, init, (per_example, given["loss_target"]))
    with _jax.named_scope("update"):
        delta_w, new_m, new_v = {}, {}, {}
        for n in TWIN_WEIGHTS:
            delta_w[n], new_m[n], new_v[n] = _adamw(weights[n], grad_w[n], given["m_" + n], given["v_" + n])
    return (loss, grad_x, *[grad_w[n] for n in TWIN_WEIGHTS], *[delta_w[n] for n in TWIN_WEIGHTS],
            *[new_m[n] for n in TWIN_WEIGHTS], *[new_v[n] for n in TWIN_WEIGHTS])
```

```python
import functools

import numpy as np
import jax
import jax.numpy as jnp
from jax import lax
from jax.experimental import pallas as pl
from jax.experimental.pallas import tpu as pltpu

N_DEV = 8
D_MODEL = 1024
GRID_W = 64
NA_HEADS = 8
NA_HEAD_DIM = 64
NA_KH = 8
NA_KW = 16
NA_W = 512
ML_HEADS = 4
ML_HEAD_DIM = 128
ML_W = 512
ML_CHUNK = 64
CONV_W = 5
EPS = 1e-6
IN_W = 4624
IN_PAD = 4736
REST_W = IN_PAD - 3 * NA_W
GATE_COL = 3072
NEG = -1e30
NA_RB = 8
NA_WIN = 16
ML_CB = 8
ADAM_LR = 0.001
ADAM_B1 = 0.9
ADAM_B2 = 0.999
ADAM_EPS = 1e-08
ADAM_WD = 0.01
ADAM_STEP = 10
VMEM_LIMIT = 56 * 1024 * 1024

_F32 = jnp.float32
_BF16 = jnp.bfloat16
_HI = lax.Precision.HIGHEST


def _cparams(sem=None):
    return pltpu.CompilerParams(dimension_semantics=sem, vmem_limit_bytes=VMEM_LIMIT)


def _nt(a, b):
    return lax.dot_general(a, b, (((1,), (1,)), ((), ())), preferred_element_type=_F32)


def _tn(a, b):
    return lax.dot_general(a, b, (((0,), (0,)), ((), ())), preferred_element_type=_F32)


def _nn(a, b):
    return jnp.dot(a, b, preferred_element_type=_F32)


def _sigmoid(x):
    return 1.0 / (1.0 + jnp.exp(-x))


def _silu(x):
    return x * _sigmoid(x)


def _dsilu(x):
    s = _sigmoid(x)
    return s * (1.0 + x * (1.0 - s))


def _exchange(arrs, scatter, name):
    n = len(arrs)
    out_shape = []
    for a, sc in zip(arrs, scatter):
        blk = a.shape[1:] if sc else a.shape
        out_shape.append(jax.ShapeDtypeStruct((N_DEV,) + tuple(blk), a.dtype))

    def body(*refs):
        ins = refs[:n]
        outs = refs[n:2 * n]
        send_sems, recv_sems, local_sems = refs[2 * n:]
        x, y, c = lax.axis_index("x"), lax.axis_index("y"), lax.axis_index("c")
        me = 4 * x + 2 * y + c
        local, sends, recvs = [], [], []
        for a in range(n):
            own = ins[a].at[me] if scatter[a] else ins[a]
            cp = pltpu.make_async_copy(own, outs[a].at[me], local_sems.at[a])
            cp.start()
            local.append(cp)
            for k in range(1, N_DEV):
                px = 1 - x if k & 4 else x
                py = 1 - y if k & 2 else y
                pc = 1 - c if k & 1 else c
                p = 4 * px + 2 * py + pc
                src = ins[a].at[p] if scatter[a] else ins[a]
                snd = pltpu.make_async_remote_copy(
                    src_ref=src, dst_ref=outs[a].at[me],
                    send_sem=send_sems.at[a, k - 1], recv_sem=recv_sems.at[a, k - 1],
                    device_id=(px, py, pc), device_id_type=pl.DeviceIdType.MESH)
                snd.start()
                sends.append(snd)
                rcv = pltpu.make_async_remote_copy(
                    src_ref=src, dst_ref=outs[a].at[p],
                    send_sem=send_sems.at[a, k - 1], recv_sem=recv_sems.at[a, k - 1],
                    device_id=(px, py, pc), device_id_type=pl.DeviceIdType.MESH)
                recvs.append(rcv)
        for rcv in recvs:
            rcv.wait_recv()
        for snd in sends:
            snd.wait_send()
        for cp in local:
            cp.wait()

    any_spec = pl.BlockSpec(memory_space=pl.ANY)
    res = pl.pallas_call(
        body, name=name, out_shape=tuple(out_shape),
        in_specs=[any_spec] * n, out_specs=tuple([any_spec] * n),
        scratch_shapes=[pltpu.SemaphoreType.DMA((n, N_DEV - 1)),
                        pltpu.SemaphoreType.DMA((n, N_DEV - 1)),
                        pltpu.SemaphoreType.DMA((n,))],
    )(*arrs)
    return list(res)


def _mod_part(c_all, w_ada, b_my):
    def body(c_ref, w_ref, b_ref, o_ref):
        o_ref[...] = jnp.dot(_silu(c_ref[...]), w_ref[...], precision=_HI,
                             preferred_element_type=_F32) + b_ref[...]

    return pl.pallas_call(
        body, name="mod_part",
        out_shape=jax.ShapeDtypeStruct((N_DEV, w_ada.shape[1]), _F32),
        compiler_params=_cparams(),
    )(c_all, w_ada, b_my)


def _inproj_fwd(x, scale1p, shift, norm_w, w_in_bf, b_in_pad):
    T = x.shape[0]
    tm = 256
    n_q = 3 * NA_W

    def body(x_ref, sc_ref, sh_ref, nw_ref, w_ref, b_ref, qkv_ref, rest_ref, h_ref):
        xv = x_ref[...]
        r = lax.rsqrt(jnp.mean(xv * xv, axis=-1, keepdims=True) + EPS)
        h = xv * r * nw_ref[...] * sc_ref[...] + sh_ref[...]
        hb = h.astype(_BF16)
        h_ref[...] = hb
        for n0 in range(0, IN_PAD, 512):
            wd = min(512, IN_PAD - n0)
            acc = _nn(hb, w_ref[:, n0:n0 + wd]) + b_ref[:, n0:n0 + wd]
            if n0 == 0:
                acc = acc * (NA_HEAD_DIM ** -0.5)
            if n0 < n_q:
                qkv_ref[:, n0:n0 + wd] = acc.astype(_BF16)
            else:
                rest_ref[:, n0 - n_q:n0 - n_q + wd] = acc

    row = lambda w: pl.BlockSpec((1, w), lambda i: (0, 0))
    return pl.pallas_call(
        body, name="inproj_fwd", grid=(T // tm,),
        in_specs=[pl.BlockSpec((tm, D_MODEL), lambda i: (i, 0)), row(D_MODEL), row(D_MODEL), row(D_MODEL),
                  pl.BlockSpec((D_MODEL, IN_PAD), lambda i: (0, 0)), row(IN_PAD)],
        out_specs=(pl.BlockSpec((tm, n_q), lambda i: (i, 0)),
                   pl.BlockSpec((tm, REST_W), lambda i: (i, 0)),
                   pl.BlockSpec((tm, D_MODEL), lambda i: (i, 0))),
        out_shape=(jax.ShapeDtypeStruct((T, n_q), _BF16),
                   jax.ShapeDtypeStruct((T, REST_W), _F32),
                   jax.ShapeDtypeStruct((T, D_MODEL), _BF16)),
        compiler_params=_cparams(("arbitrary",)),
    )(x, scale1p, shift, norm_w, w_in_bf, b_in_pad)


def _na_class_rows(rows):
    nb = rows // NA_RB
    out = []
    for rb in (0, min(1, nb - 1), nb - 1):
        ws = int(np.clip(NA_RB * rb - 4, 0, rows - NA_WIN))
        out.append((NA_RB * rb + np.arange(NA_RB), ws + np.arange(NA_WIN)))
    return out


def _na_pair_index(rows, qrows, krows):
    start = lambda r: np.clip(r - NA_KH // 2, 0, rows - NA_KH)
    col = np.arange(GRID_W)
    cstart = np.clip(col - NA_KW // 2, 0, GRID_W - NA_KW)
    dy = krows[None, :] - qrows[:, None] + NA_KH - 1
    vr = (krows[None, :] >= start(qrows)[:, None]) & (krows[None, :] < start(qrows)[:, None] + NA_KH)
    dx = np.clip(col[None, :] - col[:, None], -(NA_KW - 1), NA_KW - 1) + NA_KW - 1
    vc = (col[None, :] >= cstart[:, None]) & (col[None, :] < cstart[:, None] + NA_KW)
    nq, nk = len(qrows), len(krows)
    dy4 = np.broadcast_to(np.clip(dy, 0, 2 * NA_KH - 2)[:, None, :, None], (nq, GRID_W, nk, GRID_W))
    dx4 = np.broadcast_to(dx[None, :, None, :], (nq, GRID_W, nk, GRID_W))
    valid = vr[:, None, :, None] & vc[None, :, None, :]
    idx = (dy4 * (2 * NA_KW - 1) + dx4).reshape(nq * GRID_W, nk * GRID_W)
    return idx.astype(np.int32), valid.reshape(nq * GRID_W, nk * GRID_W), (dy, vr, dx, vc)


def _na_row_slabs(rpb):
    _, _, (_, _, dx, vc) = _na_pair_index(NA_WIN, np.arange(1), np.arange(1))
    oh = np.zeros((32, GRID_W * GRID_W), np.float32)
    oh[dx.reshape(-1), np.arange(GRID_W * GRID_W)] = 1.0
    col_neg = np.where(vc.reshape(1, -1), 0.0, NEG).astype(np.float32)
    row_neg = np.where(np.arange(NA_HEADS * 16) % 16 == 15, NEG, 0.0).astype(np.float32).reshape(-1, 1)
    rp = jnp.pad(rpb, ((0, 0), (0, 1), (0, 1))).reshape(NA_HEADS * 16, 32)

    def body(r_ref, oh_ref, cn_ref, rn_ref, o_ref):
        o_ref[...] = jnp.dot(r_ref[...], oh_ref[...], precision=_HI,
                             preferred_element_type=_F32) + cn_ref[...] + rn_ref[...]

    slabs = pl.pallas_call(
        body, name="na_row_slabs",
        out_shape=jax.ShapeDtypeStruct((NA_HEADS * 16, GRID_W * GRID_W), _F32),
        compiler_params=_cparams(),
    )(rp, jnp.asarray(oh), jnp.asarray(col_neg), jnp.asarray(row_neg))
    return slabs.reshape(NA_HEADS, 16, GRID_W, GRID_W)


def _na_bias_tables(rpb, rows):
    slabs = _na_row_slabs(rpb)
    slabs_t = slabs.transpose(0, 1, 3, 2)

    def assemble(src, didx):
        a, b = didx.shape
        tab = jnp.stack([src[:, int(d)] for d in didx.reshape(-1)], axis=1)
        tab = tab.reshape(NA_HEADS, a, b, GRID_W, GRID_W).transpose(0, 1, 3, 2, 4)
        return tab.reshape(NA_HEADS, a * GRID_W, b * GRID_W)

    tabs, tabs_t = [], []
    for blk, win in _na_class_rows(rows):
        _, _, (dy, vr, _, _) = _na_pair_index(rows, blk, win)
        tabs.append(assemble(slabs, np.where(vr, dy, 15)))
        _, _, (dy, vr, _, _) = _na_pair_index(rows, win, blk)
        tabs_t.append(assemble(slabs_t, np.where(vr, dy, 15).T))
    return jnp.stack(tabs, axis=1), jnp.stack(tabs_t, axis=1)


def _na_cls(i, nb):
    return jnp.where(i == 0, 0, jnp.where(i == nb - 1, 2, 1))


def _na_fwd(qkv, bias, T):
    rows = T // GRID_W
    nb = rows // NA_RB
    tq = NA_RB * GRID_W
    tw = NA_WIN * GRID_W

    def body(q_ref, k_ref, v_ref, b_ref, o_ref, l_ref):
        rb = pl.program_id(1)
        ws = pl.multiple_of(jnp.clip(NA_RB * rb - 4, 0, rows - NA_WIN) * GRID_W, 256)
        kw = k_ref[pl.ds(ws, tw), :]
        vw = v_ref[pl.ds(ws, tw), :]
        q = q_ref[...]
        lane = lax.broadcasted_iota(jnp.int32, (1, 128), 1)
        outs, lses = [], []
        for hh in range(2):
            msk = (lane < NA_HEAD_DIM) if hh == 0 else (lane >= NA_HEAD_DIM)
            s = _nt(jnp.where(msk, q, jnp.zeros_like(q)), kw) + b_ref[hh, 0]
            m = jnp.max(s, axis=1, keepdims=True)
            p = jnp.exp(s - m)
            l = jnp.sum(p, axis=1, keepdims=True)
            outs.append(_nn(p.astype(_BF16), vw) / l)
            lses.append(m + jnp.log(l))
        first = lane < NA_HEAD_DIM
        o_ref[...] = jnp.where(first, outs[0], outs[1])
        l_ref[...] = jnp.where(first, lses[0], lses[1])

    blk = lambda off: pl.BlockSpec((tq, 128), lambda hp, rb: (rb, off + hp))
    whole = lambda off: pl.BlockSpec((T, 128), lambda hp, rb: (0, off + hp))
    return pl.pallas_call(
        body, name="na_fwd", grid=(NA_HEADS // 2, nb),
        in_specs=[blk(0), whole(4), whole(8),
                  pl.BlockSpec((2, 1, tq, tw), lambda hp, rb: (hp, _na_cls(rb, nb), 0, 0))],
        out_specs=(blk(0), blk(0)),
        out_shape=(jax.ShapeDtypeStruct((T, NA_W), _F32), jax.ShapeDtypeStruct((T, NA_W), _F32)),
        compiler_params=_cparams(("arbitrary", "arbitrary")),
    )(qkv, qkv, qkv, bias)


def _na_bwd_q(qkv, bias, o, d_o, lse, T):
    rows = T // GRID_W
    nb = rows // NA_RB
    tq = NA_RB * GRID_W
    tw = NA_WIN * GRID_W

    def body(q_ref, k_ref, v_ref, b_ref, o_ref, do_ref, l_ref, dq_ref, dl_ref, db_ref):
        rb = pl.program_id(1)
        ws = pl.multiple_of(jnp.clip(NA_RB * rb - 4, 0, rows - NA_WIN) * GRID_W, 256)
        kw = k_ref[pl.ds(ws, tw), :]
        vw = v_ref[pl.ds(ws, tw), :]
        q = q_ref[...]
        d_ov = do_ref[...]
        prod = d_ov.astype(_F32) * o_ref[...]
        lse_v = l_ref[...]
        lane = lax.broadcasted_iota(jnp.int32, (1, 128), 1)
        first_visit = (rb == 0) | (rb == 1) | (rb == nb - 1)
        dqs, dls = [], []
        for hh in range(2):
            msk = (lane < NA_HEAD_DIM) if hh == 0 else (lane >= NA_HEAD_DIM)
            c0 = hh * NA_HEAD_DIM
            s = _nt(jnp.where(msk, q, jnp.zeros_like(q)), kw) + b_ref[hh, 0]
            p = jnp.exp(s - lse_v[:, c0:c0 + 1])
            dp = _nt(jnp.where(msk, d_ov, jnp.zeros_like(d_ov)), vw)
            delta = jnp.sum(jnp.where(msk, prod, 0.0), axis=1, keepdims=True)
            ds = p * (dp - delta)

            @pl.when(first_visit)
            def _():
                db_ref[hh, 0] = ds

            @pl.when(jnp.logical_not(first_visit))
            def _():
                db_ref[hh, 0] += ds

            dqs.append(_nn(ds.astype(_BF16), kw) * (NA_HEAD_DIM ** -0.5))
            dls.append(delta)
        first = lane < NA_HEAD_DIM
        dq_ref[...] = jnp.where(first, dqs[0], dqs[1])
        dl_ref[...] = jnp.where(first, dls[0], dls[1])

    blk = lambda off: pl.BlockSpec((tq, 128), lambda hp, rb: (rb, off + hp))
    whole = lambda off: pl.BlockSpec((T, 128), lambda hp, rb: (0, off + hp))
    tab = pl.BlockSpec((2, 1, tq, tw), lambda hp, rb: (hp, _na_cls(rb, nb), 0, 0))
    return pl.pallas_call(
        body, name="na_bwd_q", grid=(NA_HEADS // 2, nb),
        in_specs=[blk(0), whole(4), whole(8), tab, blk(0), blk(0), blk(0)],
        out_specs=(blk(0), blk(0), tab),
        out_shape=(jax.ShapeDtypeStruct((T, NA_W), _F32), jax.ShapeDtypeStruct((T, NA_W), _F32),
                   jax.ShapeDtypeStruct(bias.shape, _F32)),
        compiler_params=_cparams(("arbitrary", "arbitrary")),
    )(qkv, qkv, qkv, bias, o, d_o, lse)


def _na_bwd_kv(qkv, bias_t, d_o, lse_rows, delta_rows, T):
    rows = T // GRID_W
    nb = rows // NA_RB
    tk = NA_RB * GRID_W
    tw = NA_WIN * GRID_W

    def body(k_ref, v_ref, q_ref, do_ref, b_ref, l_ref, dl_ref, dk_ref, dv_ref):
        hp = pl.program_id(0)
        kb = pl.program_id(1)
        ws = pl.multiple_of(jnp.clip(NA_RB * kb - 4, 0, rows - NA_WIN) * GRID_W, 256)
        qw = q_ref[pl.ds(ws, tw), :]
        dow = do_ref[pl.ds(ws, tw), :]
        w0 = ws // 256
        lw = jnp.concatenate([l_ref[w0 + i] for i in range(tw // 256)], axis=1)
        dw = jnp.concatenate([dl_ref[w0 + i] for i in range(tw // 256)], axis=1)
        k = k_ref[...]
        v = v_ref[...]
        lane = lax.broadcasted_iota(jnp.int32, (1, 128), 1)
        sub = lax.broadcasted_iota(jnp.int32, (NA_HEADS, 1), 0)
        dks, dvs = [], []
        for hh in range(2):
            msk = (lane < NA_HEAD_DIM) if hh == 0 else (lane >= NA_HEAD_DIM)
            pick = sub == 2 * hp + hh
            lrow = jnp.sum(jnp.where(pick, lw, 0.0), axis=0, keepdims=True)
            drow = jnp.sum(jnp.where(pick, dw, 0.0), axis=0, keepdims=True)
            st = _nt(jnp.where(msk, k, jnp.zeros_like(k)), qw) + b_ref[hh, 0]
            pt = jnp.exp(st - lrow)
            dvs.append(_nn(pt.astype(_BF16), dow))
            dpt = _nt(jnp.where(msk, v, jnp.zeros_like(v)), dow)
            dst = pt * (dpt - drow)
            dks.append(_nn(dst.astype(_BF16), qw))
        first = lane < NA_HEAD_DIM
        dk_ref[...] = jnp.where(first, dks[0], dks[1])
        dv_ref[...] = jnp.where(first, dvs[0], dvs[1])

    blk = lambda off: pl.BlockSpec((tk, 128), lambda hp, kb: (kb, off + hp))
    whole = lambda off: pl.BlockSpec((T, 128), lambda hp, kb: (0, off + hp))
    rowspec = pl.BlockSpec(lse_rows.shape, lambda hp, kb: (0, 0, 0))
    return pl.pallas_call(
        body, name="na_bwd_kv", grid=(NA_HEADS // 2, nb),
        in_specs=[blk(4), blk(8), whole(0), whole(0),
                  pl.BlockSpec((2, 1, tk, tw), lambda hp, kb: (hp, _na_cls(kb, nb), 0, 0)),
                  rowspec, rowspec],
        out_specs=(blk(0), blk(0)),
        out_shape=(jax.ShapeDtypeStruct((T, NA_W), _F32), jax.ShapeDtypeStruct((T, NA_W), _F32)),
        compiler_params=_cparams(("arbitrary", "arbitrary")),
    )(qkv, qkv, qkv, d_o, bias_t, lse_rows, delta_rows)


def _rows_layout(col_bcast, T):
    per_head = col_bcast[:, ::NA_HEAD_DIM]
    return per_head.T.reshape(NA_HEADS, T // 256, 256).transpose(1, 0, 2)


def _rpb_grad(dbias, rows):
    ncls = 3
    cls_rows = _na_class_rows(rows)
    nb = rows // NA_RB
    oh_y = np.zeros((16, ncls * NA_RB * NA_WIN), np.float32)
    dxm = None
    for ci, (blk, win) in enumerate(cls_rows):
        if nb > 1 and ci == 1 and nb == 2:
            continue
        _, _, (dy, vr, dx, vc) = _na_pair_index(rows, blk, win)
        for a in range(NA_RB):
            for b in range(NA_WIN):
                if vr[a, b]:
                    oh_y[dy[a, b], (ci * NA_RB + a) * NA_WIN + b] = 1.0
        dxm = (dx, vc)
    dx, vc = dxm
    oh_x = np.zeros((GRID_W * GRID_W, 128), np.float32)
    for a in range(GRID_W):
        for b in range(GRID_W):
            if vc[a, b]:
                oh_x[a * GRID_W + b, dx[a, b]] = 1.0
    xmat = dbias.reshape(NA_HEADS, ncls, NA_RB, GRID_W, NA_WIN, GRID_W).transpose(0, 1, 2, 4, 3, 5)
    xmat = xmat.reshape(NA_HEADS, ncls * NA_RB * NA_WIN, GRID_W * GRID_W)

    def body(x_ref, ox_ref, oy_ref, g_ref):
        y = jnp.dot(x_ref[0], ox_ref[...], precision=_HI, preferred_element_type=_F32)
        g_ref[0] = jnp.dot(oy_ref[...], y, precision=_HI, preferred_element_type=_F32)

    g = pl.pallas_call(
        body, name="rpb_grad", grid=(NA_HEADS,),
        in_specs=[pl.BlockSpec((1,) + xmat.shape[1:], lambda h: (h, 0, 0)),
                  pl.BlockSpec(oh_x.shape, lambda h: (0, 0)),
                  pl.BlockSpec(oh_y.shape, lambda h: (0, 0))],
        out_specs=pl.BlockSpec((1, 16, 128), lambda h: (h, 0, 0)),
        out_shape=jax.ShapeDtypeStruct((NA_HEADS, 16, 128), _F32),
        compiler_params=_cparams(("arbitrary",)),
    )(xmat, jnp.asarray(oh_x), jnp.asarray(oh_y))
    return g[:, :2 * NA_KH - 1, :2 * NA_KW - 1]


def _halo_specs(tm, width, col_of, T, order):
    hb = tm // 8
    last = T // 8 - 1
    if order == "ij":
        cur = pl.BlockSpec((tm, width), lambda i, j: (i, col_of(j)))
        prev = pl.BlockSpec((8, width), lambda i, j: (jnp.maximum(i * hb - 1, 0), col_of(j)))
        nxt = pl.BlockSpec((8, width), lambda i, j: (jnp.minimum((i + 1) * hb, last), col_of(j)))
    else:
        cur = pl.BlockSpec((tm, width), lambda j, i: (i, col_of(j)))
        prev = pl.BlockSpec((8, width), lambda j, i: (jnp.maximum(i * hb - 1, 0), col_of(j)))
        nxt = pl.BlockSpec((8, width), lambda j, i: (jnp.minimum((i + 1) * hb, last), col_of(j)))
    return [prev, cur, nxt]


def _extend(prev_ref, cur_ref, next_ref, i, n_i):
    prev = jnp.where(i > 0, prev_ref[...], 0.0)
    nxt = jnp.where(i < n_i - 1, next_ref[...], 0.0)
    return jnp.concatenate([prev, cur_ref[...], nxt], axis=0)


def _conv_fwd(rest, conv_w, conv_b, T):
    tm = 512
    n_i = T // tm
    n = tm + 16

    def body(p_ref, c_ref, n_ref, w_ref, b_ref, o_ref):
        i = pl.program_id(0)
        ext = _extend(p_ref, c_ref, n_ref, i, n_i)
        acc = jnp.zeros((tm, 512), _F32) + b_ref[...]
        for j in range(CONV_W):
            acc = acc + w_ref[j:j + 1, :] * pltpu.roll(ext, (2 - j) % n, 0)[8:8 + tm]
        o_ref[...] = _silu(acc)

    return pl.pallas_call(
        body, name="conv_fwd", grid=(n_i, 2),
        in_specs=_halo_specs(tm, 512, lambda j: 1 + j, T, "ij")
        + [pl.BlockSpec((8, 512), lambda i, j: (0, j)), pl.BlockSpec((1, 512), lambda i, j: (0, j))],
        out_specs=pl.BlockSpec((tm, 512), lambda i, j: (i, j)),
        out_shape=jax.ShapeDtypeStruct((T, 2 * ML_W), _F32),
        compiler_params=_cparams(("arbitrary", "arbitrary")),
    )(rest, rest, rest, conv_w, conv_b)


def _conv_bwd(rest, conv_w, conv_b, da_f, da_b, T):
    tm = 512
    n_i = T // tm
    n = tm + 16

    def body(up, uc, un, fp, fc, fn, bp, bc, bn, w_ref, b_ref, du_ref, dw_ref):
        i = pl.program_id(1)
        ext_u = _extend(up, uc, un, i, n_i)
        ext_da = _extend(fp, fc, fn, i, n_i) + _extend(bp, bc, bn, i, n_i)
        shifted = [pltpu.roll(ext_u, (2 - j) % n, 0) for j in range(CONV_W)]
        pre = jnp.zeros((n, 512), _F32) + b_ref[...]
        for j in range(CONV_W):
            pre = pre + w_ref[j:j + 1, :] * shifted[j]
        gidx = i * tm - 8 + lax.broadcasted_iota(jnp.int32, (n, 1), 0)
        dpre = jnp.where((gidx >= 0) & (gidx < T), ext_da * _dsilu(pre), 0.0)
        du = jnp.zeros((tm, 512), _F32)
        for j in range(CONV_W):
            du = du + w_ref[j:j + 1, :] * pltpu.roll(dpre, (j - 2) % n, 0)[8:8 + tm]
        du_ref[...] = du
        dpc = dpre[8:8 + tm]
        parts = [jnp.sum(dpc * shifted[j][8:8 + tm], axis=0, keepdims=True) for j in range(CONV_W)]
        parts.append(jnp.sum(dpc, axis=0, keepdims=True))
        parts.append(jnp.zeros((2, 512), _F32))
        upd = jnp.concatenate(parts, axis=0)

        @pl.when(i == 0)
        def _():
            dw_ref[...] = upd

        @pl.when(i > 0)
        def _():
            dw_ref[...] += upd

    return pl.pallas_call(
        body, name="conv_bwd", grid=(2, n_i),
        in_specs=_halo_specs(tm, 512, lambda j: 1 + j, T, "ji")
        + _halo_specs(tm, 512, lambda j: j, T, "ji") + _halo_specs(tm, 512, lambda j: j, T, "ji")
        + [pl.BlockSpec((8, 512), lambda j, i: (0, j)), pl.BlockSpec((1, 512), lambda j, i: (0, j))],
        out_specs=(pl.BlockSpec((tm, 512), lambda j, i: (i, j)), pl.BlockSpec((8, 512), lambda j, i: (0, j))),
        out_shape=(jax.ShapeDtypeStruct((T, 2 * ML_W), _F32), jax.ShapeDtypeStruct((8, 2 * ML_W), _F32)),
        compiler_params=_cparams(("arbitrary", "arbitrary")),
    )(rest, rest, rest, da_f, da_f, da_f, da_b, da_b, da_b, conv_w, conv_b)


def _ml_chunk(q, k, v, icol, fcol, C, n, m, rev):
    L = ML_CHUNK
    ri = lax.broadcasted_iota(jnp.int32, (L, L), 0)
    ci = lax.broadcasted_iota(jnp.int32, (L, L), 1)
    mask = (ci >= ri) if rev else (ci <= ri)
    eye = ri == ci
    to_row = lambda col: jnp.sum(jnp.where(eye, col, 0.0), axis=0, keepdims=True)
    to_col = lambda row: jnp.sum(jnp.where(eye, row, 0.0), axis=1, keepdims=True)
    lf = jnp.minimum(fcol, 0.0) - jnp.log(1.0 + jnp.exp(-jnp.abs(fcol)))
    b_col = jnp.sum(jnp.where(mask, to_row(lf), 0.0), axis=1, keepdims=True)
    b_row = to_row(b_col)
    bl = jnp.sum(lf, axis=0, keepdims=True)
    dlog = jnp.where(mask, b_col - b_row + to_row(icol), NEG)
    m_t = jnp.maximum(b_col + m, jnp.max(dlog, axis=1, keepdims=True))
    dm = jnp.exp(dlog - m_t)
    ks = k * (ML_HEAD_DIM ** -0.5)
    qb, kb, vb = q.astype(_BF16), ks.astype(_BF16), v.astype(_BF16)
    s = _nt(qb, kb) * dm
    g = jnp.exp(b_col + m - m_t)
    qc = _nt(qb, C.astype(_BF16))
    num = _nn(s.astype(_BF16), vb) + g * qc
    qn = jnp.sum(q * n, axis=1, keepdims=True)
    den = jnp.sum(s, axis=1, keepdims=True) + g * qn
    e_m = jnp.exp(-m_t)
    nrm = jnp.maximum(jnp.abs(den), e_m)
    h = num / nrm
    a_col = bl - b_col + icol
    m_new = jnp.maximum(bl + m, jnp.max(a_col, axis=0, keepdims=True))
    decay = jnp.exp(bl + m - m_new)
    w = jnp.exp(a_col - m_new)
    c_new = decay * C + _tn((w * v).astype(_BF16), kb)
    n_new = decay * n + jnp.sum(w * ks, axis=0, keepdims=True)
    aux = dict(mask=mask, to_col=to_col, dm=dm, ks=ks, qb=qb, kb=kb, vb=vb, s=s, g=g, qc=qc, qn=qn,
               den=den, e_m=e_m, nrm=nrm, decay=decay, w=w)
    return h, c_new, n_new, m_new, aux


def _ml_gate_cols(gt, head, rev):
    lane = lax.broadcasted_iota(jnp.int32, (1, 128), 1)
    ci = (8 if rev else 0) + head
    cf = ci + ML_HEADS
    icol = jnp.sum(jnp.where(lane == ci, gt, 0.0), axis=1, keepdims=True)
    fcol = jnp.sum(jnp.where(lane == cf, gt, 0.0), axis=1, keepdims=True)
    return icol, fcol, lane == ci, lane == cf


def _mlstm_fwd(qk_act, rest, T, rev):
    tb = ML_CB * ML_CHUNK
    nblk = T // tb
    nc = T // ML_CHUNK
    bi = (lambda i: nblk - 1 - i) if rev else (lambda i: i)

    def body(q_ref, k_ref, v_ref, g_ref, h_ref, cs_ref, ns_ref, ms_ref, c_scr, n_scr, m_scr):
        @pl.when(pl.program_id(0) == 0)
        def _():
            c_scr[...] = jnp.zeros_like(c_scr)
            n_scr[...] = jnp.zeros_like(n_scr)
            m_scr[...] = jnp.zeros_like(m_scr)

        def step(j, carry):
            c = (ML_CB - 1 - j) if rev else j
            r0 = pl.multiple_of(c * ML_CHUNK, ML_CHUNK)
            gt = g_ref[pl.ds(r0, ML_CHUNK), :]
            for hd in range(ML_HEADS):
                cols = slice(hd * ML_HEAD_DIM, (hd + 1) * ML_HEAD_DIM)
                icol, fcol, _, _ = _ml_gate_cols(gt, hd, rev)
                C = c_scr[hd]
                n = n_scr[hd:hd + 1, :]
                mrow = m_scr[hd:hd + 1, :]
                cs_ref[c, hd] = C
                ns_ref[c, hd:hd + 1, :] = n
                ms_ref[c, hd:hd + 1, :] = mrow
                h, c_new, n_new, m_new, _ = _ml_chunk(
                    q_ref[pl.ds(r0, ML_CHUNK), cols], k_ref[pl.ds(r0, ML_CHUNK), cols],
                    v_ref[pl.ds(r0, ML_CHUNK), cols], icol, fcol, C, n, mrow[:, 0:1], rev)
                h_ref[pl.ds(r0, ML_CHUNK), cols] = h
                c_scr[hd] = c_new
                n_scr[hd:hd + 1, :] = n_new
                m_scr[hd:hd + 1, :] = jnp.broadcast_to(m_new, (1, 128))
            return carry

        lax.fori_loop(0, ML_CB, step, 0)

    return pl.pallas_call(
        body, name="mlstm_fwd_rev" if rev else "mlstm_fwd", grid=(nblk,),
        in_specs=[pl.BlockSpec((tb, ML_W), lambda i: (bi(i), 0)),
                  pl.BlockSpec((tb, ML_W), lambda i: (bi(i), 1)),
                  pl.BlockSpec((tb, ML_W), lambda i: (bi(i), 3)),
                  pl.BlockSpec((tb, 128), lambda i: (bi(i), GATE_COL // 128))],
        out_specs=(pl.BlockSpec((tb, ML_W), lambda i: (bi(i), 0)),
                   pl.BlockSpec((ML_CB, ML_HEADS, 128, 128), lambda i: (bi(i), 0, 0, 0)),
                   pl.BlockSpec((ML_CB, ML_HEADS, 128), lambda i: (bi(i), 0, 0)),
                   pl.BlockSpec((ML_CB, ML_HEADS, 128), lambda i: (bi(i), 0, 0))),
        out_shape=(jax.ShapeDtypeStruct((T, ML_W), _F32),
                   jax.ShapeDtypeStruct((nc, ML_HEADS, 128, 128), _F32),
                   jax.ShapeDtypeStruct((nc, ML_HEADS, 128), _F32),
                   jax.ShapeDtypeStruct((nc, ML_HEADS, 128), _F32)),
        scratch_shapes=[pltpu.VMEM((ML_HEADS, 128, 128), _F32), pltpu.VMEM((8, 128), _F32),
                        pltpu.VMEM((8, 128), _F32)],
        compiler_params=_cparams(("arbitrary",)),
    )(qk_act, qk_act, rest, rest)


def _mlstm_bwd(qk_act, rest, d_h, cs, ns, ms, T, rev):
    tb = ML_CB * ML_CHUNK
    nblk = T // tb
    bi = (lambda i: i) if rev else (lambda i: nblk - 1 - i)

    def body(q_ref, k_ref, v_ref, g_ref, dh_ref, cs_ref, ns_ref, ms_ref,
             dqk_ref, dv_ref, dg_ref, dc_scr, dn_scr):
        @pl.when(pl.program_id(0) == 0)
        def _():
            dc_scr[...] = jnp.zeros_like(dc_scr)
            dn_scr[...] = jnp.zeros_like(dn_scr)

        def step(j, carry):
            c = j if rev else (ML_CB - 1 - j)
            r0 = pl.multiple_of(c * ML_CHUNK, ML_CHUNK)
            gt = g_ref[pl.ds(r0, ML_CHUNK), :]
            dgt = jnp.zeros((ML_CHUNK, 128), _F32)
            for hd in range(ML_HEADS):
                cols = slice(hd * ML_HEAD_DIM, (hd + 1) * ML_HEAD_DIM)
                icol, fcol, is_i, is_f = _ml_gate_cols(gt, hd, rev)
                q = q_ref[pl.ds(r0, ML_CHUNK), cols]
                k = k_ref[pl.ds(r0, ML_CHUNK), cols]
                v = v_ref[pl.ds(r0, ML_CHUNK), cols]
                C = cs_ref[c, hd]
                n = ns_ref[c, hd:hd + 1, :]
                m = ms_ref[c, hd:hd + 1, :][:, 0:1]
                dcn = dc_scr[hd]
                dnn = dn_scr[hd:hd + 1, :]
                h, _, _, _, a = _ml_chunk(q, k, v, icol, fcol, C, n, m, rev)
                d_hv = dh_ref[pl.ds(r0, ML_CHUNK), cols]
                g, s, w, ks = a["g"], a["s"], a["w"], a["ks"]
                qb, kb, vb = a["qb"], a["kb"], a["vb"]
                dnum = d_hv / a["nrm"]
                hdot = jnp.sum(d_hv * h, axis=1, keepdims=True)
                dden = jnp.where(jnp.abs(a["den"]) >= a["e_m"], -hdot / a["nrm"] * jnp.sign(a["den"]), 0.0)
                dnb = dnum.astype(_BF16)
                d_s = _nt(dnb, vb) + dden
                r = d_s * s
                dsqk = (d_s * a["dm"]).astype(_BF16)
                cb = C.astype(_BF16)
                dq = _nn(dsqk, kb) + g * _nn(dnb, cb) + (dden * g) * n
                dk = _tn(dsqk, qb)
                dv = _tn(s.astype(_BF16), dnb)
                dg = jnp.sum(dnum * a["qc"], axis=1, keepdims=True) + dden * a["qn"]
                db_col = jnp.sum(r, axis=1, keepdims=True) + dg * g
                cs_r = a["to_col"](jnp.sum(r, axis=0, keepdims=True))
                dc_chunk = _tn((g * dnum).astype(_BF16), qb)
                dn_chunk = jnp.sum((dden * g) * q, axis=0, keepdims=True)
                dcb = dcn.astype(_BF16)
                vdc = _nn(vb, dcb)
                kdc = _nt(kb, dcb)
                dw = jnp.sum(vdc * ks, axis=1, keepdims=True) + jnp.sum(ks * dnn, axis=1, keepdims=True)
                dv = dv + w * kdc
                dk = dk + w * vdc + w * dnn
                da = dw * w
                ddecay = (jnp.sum(jnp.sum(dcn * C, axis=1, keepdims=True), axis=0, keepdims=True)
                          + jnp.sum(dnn * n, axis=1, keepdims=True))
                dbl = ddecay * a["decay"] + jnp.sum(da, axis=0, keepdims=True)
                db_col = db_col - da - cs_r
                di_col = da + cs_r
                dlf = a["to_col"](jnp.sum(jnp.where(a["mask"], db_col, 0.0), axis=0, keepdims=True)) + dbl
                df_col = dlf * _sigmoid(-fcol)
                dc_scr[hd] = dc_chunk + a["decay"] * dcn
                dn_scr[hd:hd + 1, :] = dn_chunk + a["decay"] * dnn
                dqk_ref[pl.ds(r0, ML_CHUNK), cols] = dq
                dqk_ref[pl.ds(r0, ML_CHUNK), slice(ML_W + hd * 128, ML_W + (hd + 1) * 128)] = dk * (ML_HEAD_DIM ** -0.5)
                dv_ref[pl.ds(r0, ML_CHUNK), cols] = dv
                dgt = dgt + jnp.where(is_i, di_col, 0.0) + jnp.where(is_f, df_col, 0.0)
            dg_ref[pl.ds(r0, ML_CHUNK), :] = dgt
            return carry

        lax.fori_loop(0, ML_CB, step, 0)

    return pl.pallas_call(
        body, name="mlstm_bwd_rev" if rev else "mlstm_bwd", grid=(nblk,),
        in_specs=[pl.BlockSpec((tb, ML_W), lambda i: (bi(i), 0)),
                  pl.BlockSpec((tb, ML_W), lambda i: (bi(i), 1)),
                  pl.BlockSpec((tb, ML_W), lambda i: (bi(i), 3)),
                  pl.BlockSpec((tb, 128), lambda i: (bi(i), GATE_COL // 128)),
                  pl.BlockSpec((tb, ML_W), lambda i: (bi(i), 0)),
                  pl.BlockSpec((ML_CB, ML_HEADS, 128, 128), lambda i: (bi(i), 0, 0, 0)),
                  pl.BlockSpec((ML_CB, ML_HEADS, 128), lambda i: (bi(i), 0, 0)),
                  pl.BlockSpec((ML_CB, ML_HEADS, 128), lambda i: (bi(i), 0, 0))],
        out_specs=(pl.BlockSpec((tb, 2 * ML_W), lambda i: (bi(i), 0)),
                   pl.BlockSpec((tb, ML_W), lambda i: (bi(i), 0)),
                   pl.BlockSpec((tb, 128), lambda i: (bi(i), 0))),
        out_shape=(jax.ShapeDtypeStruct((T, 2 * ML_W), _F32), jax.ShapeDtypeStruct((T, ML_W), _F32),
                   jax.ShapeDtypeStruct((T, 128), _F32)),
        scratch_shapes=[pltpu.VMEM((ML_HEADS, 128, 128), _F32), pltpu.VMEM((8, 128), _F32)],
        compiler_params=_cparams(("arbitrary",)),
    )(qk_act, qk_act, rest, rest, d_h, cs, ns, ms)


def _post(x, target, o_na, rest, h_f, h_b, gate, ml_norm_w, final_w, w_out_bf, T):
    tm = 256
    n_i = T // tm

    def body(x_ref, t_ref, o_ref, zna_ref, hf_ref, hb_ref, mo_ref, mz_ref, gate_ref, mw_ref, fw_ref, w_ref,
             dx1_ref, do_ref, dzna_ref, dh_ref, dmo_ref, dmz_ref, dwo_ref, vec_ref):
        i = pl.program_id(0)
        gate_v = gate_ref[...]
        fw = fw_ref[...]
        zna = zna_ref[...]
        o = o_ref[...]
        na_out = o * _silu(zna)
        hsum = hf_ref[...] + hb_ref[...]
        sg = _sigmoid(mo_ref[...])
        hm = hsum * sg
        mz = mz_ref[...]
        smz = _silu(mz)
        hn_l, rstd_l, ml_l = [], [], []
        for hd in range(ML_HEADS):
            cols = slice(hd * 128, (hd + 1) * 128)
            hh = hm[:, cols]
            mu = jnp.mean(hh, axis=-1, keepdims=True)
            var = jnp.mean(jnp.square(hh - mu), axis=-1, keepdims=True)
            rstd = lax.rsqrt(var + EPS)
            hn = (hh - mu) * rstd
            hn_l.append(hn)
            rstd_l.append(rstd)
            ml_l.append(hn * mw_ref[:, cols] * smz[:, cols])
        mix = jnp.concatenate([na_out] + ml_l, axis=1).astype(_BF16)
        y = _nn(mix, w_ref[...])
        x1 = x_ref[...] + gate_v * y
        r = lax.rsqrt(jnp.mean(x1 * x1, axis=-1, keepdims=True) + EPS)
        xhat = x1 * r
        out = xhat * fw
        err = out - t_ref[...]
        loss = 0.5 * jnp.sum(jnp.sum(err * err, axis=1, keepdims=True), axis=0, keepdims=True) / D_MODEL
        dout = err * (1.0 / D_MODEL)
        dfw = jnp.sum(dout * xhat, axis=0, keepdims=True)
        dxhat = dout * fw
        dx1 = r * (dxhat - xhat * jnp.mean(dxhat * xhat, axis=-1, keepdims=True))
        dx1_ref[...] = dx1
        dgate = jnp.sum(dx1 * y, axis=0, keepdims=True)
        dy = (dx1 * gate_v).astype(_BF16)
        dmix = _nt(dy, w_ref[...])
        dwo = _tn(mix, dy)
        dna = dmix[:, :NA_W]
        do_ref[...] = (dna * _silu(zna)).astype(_BF16)
        dzna_ref[...] = dna * o * _dsilu(zna)
        dmw_l = []
        for hd in range(ML_HEADS):
            cols = slice(hd * 128, (hd + 1) * 128)
            dml = dmix[:, NA_W + hd * 128:NA_W + (hd + 1) * 128]
            hn = hn_l[hd]
            mwv = mw_ref[:, cols]
            dmz_ref[:, cols] = dml * hn * mwv * _dsilu(mz[:, cols])
            dhn = dml * mwv * smz[:, cols]
            dmw_l.append(jnp.sum(dml * hn * smz[:, cols], axis=0, keepdims=True))
            dhm = rstd_l[hd] * (dhn - jnp.mean(dhn, axis=-1, keepdims=True)
                                - hn * jnp.mean(dhn * hn, axis=-1, keepdims=True))
            sgc = sg[:, cols]
            dh_ref[:, cols] = dhm * sgc
            dmo_ref[:, cols] = dhm * hsum[:, cols] * sgc * (1.0 - sgc)
        dmw = jnp.concatenate(dmw_l + [jnp.zeros((1, D_MODEL - ML_W), _F32)], axis=1)
        lane = lax.broadcasted_iota(jnp.int32, (1, D_MODEL), 1)
        vec = jnp.concatenate([dfw, dgate, dmw, jnp.where(lane == 0, loss, 0.0),
                               jnp.zeros((4, D_MODEL), _F32)], axis=0)

        @pl.when(i == 0)
        def _():
            dwo_ref[...] = dwo
            vec_ref[...] = vec

        @pl.when(i > 0)
        def _():
            dwo_ref[...] += dwo
            vec_ref[...] += vec

    tok = lambda w, j: pl.BlockSpec((tm, w), lambda i: (i, j))
    row = lambda w: pl.BlockSpec((1, w), lambda i: (0, 0))
    f32 = lambda w: jax.ShapeDtypeStruct((T, w), _F32)
    return pl.pallas_call(
        body, name="post", grid=(n_i,),
        in_specs=[tok(D_MODEL, 0), tok(D_MODEL, 0), tok(NA_W, 0), tok(NA_W, 0), tok(ML_W, 0), tok(ML_W, 0),
                  tok(ML_W, 4), tok(ML_W, 5), row(D_MODEL), row(ML_W), row(D_MODEL),
                  pl.BlockSpec((D_MODEL, D_MODEL), lambda i: (0, 0))],
        out_specs=(tok(D_MODEL, 0), tok(NA_W, 0), tok(NA_W, 0), tok(ML_W, 0), tok(ML_W, 0), tok(ML_W, 0),
                   pl.BlockSpec((D_MODEL, D_MODEL), lambda i: (0, 0)),
                   pl.BlockSpec((8, D_MODEL), lambda i: (0, 0))),
        out_shape=(f32(D_MODEL), jax.ShapeDtypeStruct((T, NA_W), _BF16), f32(NA_W), f32(ML_W), f32(ML_W),
                   f32(ML_W), jax.ShapeDtypeStruct((D_MODEL, D_MODEL), _F32),
                   jax.ShapeDtypeStruct((8, D_MODEL), _F32)),
        compiler_params=_cparams(("arbitrary",)),
    )(x, target, o_na, rest, h_f, h_b, rest, rest, gate, ml_norm_w, final_w, w_out_bf)


def _section_specs(sections, tm):
    specs, args = [], []
    for _, width, parts in sections:
        for arr, cb in parts:
            specs.append(pl.BlockSpec((tm, width), functools.partial(lambda i, cb: (i, cb), cb=cb)))
            args.append(arr)
    return specs, args


def _section_values(sections, refs):
    vals, at = [], 0
    for _, _, parts in sections:
        v = refs[at][...]
        for r in refs[at + 1:at + len(parts)]:
            v = v + r[...]
        at += len(parts)
        vals.append(v.astype(_BF16))
    return vals


def _inproj_bwd_x(x, dx1, scale1p, norm_w, w_in_bf, sections, T):
    tm = 256
    sspecs, sargs = _section_specs(sections, tm)
    ns = len(sargs)

    def body(*refs):
        x_ref, dx1_ref, sc_ref, nw_ref, w_ref = refs[:5]
        srefs = refs[5:5 + ns]
        gx_ref, vec_ref = refs[5 + ns:]
        i = pl.program_id(0)
        vals = _section_values(sections, srefs)
        dh = jnp.zeros((tm, D_MODEL), _F32)
        for (c0, width, _), val in zip(sections, vals):
            dh = dh + _nt(val, w_ref[:, c0:c0 + width])
        xv = x_ref[...]
        r = lax.rsqrt(jnp.mean(xv * xv, axis=-1, keepdims=True) + EPS)
        xhat = xv * r
        nw = nw_ref[...]
        dshift = jnp.sum(dh, axis=0, keepdims=True)
        dscale = jnp.sum(dh * xhat * nw, axis=0, keepdims=True)
        dhpre = dh * sc_ref[...]
        dnw = jnp.sum(dhpre * xhat, axis=0, keepdims=True)
        dxhat = dhpre * nw
        gx_ref[...] = dx1_ref[...] + r * (dxhat - xhat * jnp.mean(dxhat * xhat, axis=-1, keepdims=True))
        vec = jnp.concatenate([dshift, dscale, dnw, jnp.zeros((5, D_MODEL), _F32)], axis=0)

        @pl.when(i == 0)
        def _():
            vec_ref[...] = vec

        @pl.when(i > 0)
        def _():
            vec_ref[...] += vec

    row = pl.BlockSpec((1, D_MODEL), lambda i: (0, 0))
    tok = pl.BlockSpec((tm, D_MODEL), lambda i: (i, 0))
    return pl.pallas_call(
        body, name="inproj_bwd_x", grid=(T // tm,),
        in_specs=[tok, tok, row, row, pl.BlockSpec((D_MODEL, IN_PAD), lambda i: (0, 0))] + sspecs,
        out_specs=(tok, pl.BlockSpec((8, D_MODEL), lambda i: (0, 0))),
        out_shape=(jax.ShapeDtypeStruct((T, D_MODEL), _F32), jax.ShapeDtypeStruct((8, D_MODEL), _F32)),
        compiler_params=_cparams(("arbitrary",)),
    )(x, dx1, scale1p, norm_w, w_in_bf, *sargs)


def _inproj_bwd_w(h_bf, section, T, name):
    tm = 512
    width = section[1]
    sspecs, sargs = _section_specs([section], tm)
    ns = len(sargs)

    def body(*refs):
        h_ref = refs[0]
        srefs = refs[1:1 + ns]
        dw_ref, db_ref = refs[1 + ns:]
        i = pl.program_id(0)
        v = srefs[0][...]
        for r in srefs[1:]:
            v = v + r[...]
        dw = _tn(h_ref[...], v.astype(_BF16))
        db = jnp.concatenate([jnp.sum(v, axis=0, keepdims=True), jnp.zeros((7, width), _F32)], axis=0)

        @pl.when(i == 0)
        def _():
            dw_ref[...] = dw
            db_ref[...] = db

        @pl.when(i > 0)
        def _():
            dw_ref[...] += dw
            db_ref[...] += db

    return pl.pallas_call(
        body, name=name, grid=(T // tm,),
        in_specs=[pl.BlockSpec((tm, D_MODEL), lambda i: (i, 0))] + sspecs,
        out_specs=(pl.BlockSpec((D_MODEL, width), lambda i: (0, 0)), pl.BlockSpec((8, width), lambda i: (0, 0))),
        out_shape=(jax.ShapeDtypeStruct((D_MODEL, width), _F32), jax.ShapeDtypeStruct((8, width), _F32)),
        compiler_params=_cparams(("arbitrary",)),
    )(h_bf, *sargs)


def _adamw_math(w, g, m, v):
    m = ADAM_B1 * m + (1.0 - ADAM_B1) * g
    v = ADAM_B2 * v + (1.0 - ADAM_B2) * jnp.square(g)
    m_hat = m / (1.0 - ADAM_B1 ** ADAM_STEP)
    v_hat = v / (1.0 - ADAM_B2 ** ADAM_STEP)
    delta = -ADAM_LR * (m_hat / (jnp.sqrt(v_hat) + ADAM_EPS) + ADAM_WD * w)
    return delta, m, v


def _adamw_slots(w, m, v, slots, tr, name):
    R, C = w.shape

    def body(w_ref, m_ref, v_ref, s_ref, g_ref, d_ref, nm_ref, nv_ref):
        g = s_ref[0]
        for k in range(1, N_DEV):
            g = g + s_ref[k]
        g_ref[...] = g
        d_ref[...], nm_ref[...], nv_ref[...] = _adamw_math(w_ref[...], g, m_ref[...], v_ref[...])

    blk = pl.BlockSpec((tr, C), lambda i: (i, 0))
    return pl.pallas_call(
        body, name=name, grid=(R // tr,),
        in_specs=[blk, blk, blk, pl.BlockSpec((N_DEV, tr, C), lambda i: (0, i, 0))],
        out_specs=(blk, blk, blk, blk),
        out_shape=tuple(jax.ShapeDtypeStruct((R, C), _F32) for _ in range(4)),
        compiler_params=_cparams(("arbitrary",)),
    )(w, m, v, slots)


def _w_ada_update(c_all, dmod_my, w, m, v):
    def body(c_ref, d_ref, w_ref, m_ref, v_ref, g_ref, dl_ref, nm_ref, nv_ref):
        g = lax.dot_general(_silu(c_ref[...]), d_ref[...], (((0,), (0,)), ((), ())),
                            precision=_HI, preferred_element_type=_F32)
        g_ref[...] = g
        dl_ref[...], nm_ref[...], nv_ref[...] = _adamw_math(w_ref[...], g, m_ref[...], v_ref[...])

    return pl.pallas_call(
        body, name="w_ada_update",
        out_shape=tuple(jax.ShapeDtypeStruct(w.shape, _F32) for _ in range(4)),
        compiler_params=_cparams(),
    )(c_all, dmod_my, w, m, v)


_PACK = (("b_ada", 3072, 3072), ("norm_w", 1024, 1024), ("b_in", IN_W, IN_PAD), ("conv_w", 5120, 5120),
         ("conv_b", 1024, 1024), ("rpb", 3720, 3840), ("ml_norm_w", 512, 512), ("final_norm_w", 1024, 1024),
         ("loss", 1, 128))
_PACK_OFF = {}
_off = 0
for _name, _len, _pad in _PACK:
    _PACK_OFF[_name] = (_off, _len)
    _off += _pad
_PACK_LEN = _off


def _pack(parts):
    cols = []
    for name, length, pad in _PACK:
        vec = parts[name].reshape(-1).astype(_F32)
        cols.append(jnp.pad(vec, (0, pad - length)))
    return jnp.concatenate(cols).reshape(1, _PACK_LEN)


def _unpack(vec, name, shape):
    off, length = _PACK_OFF[name]
    return vec.reshape(-1)[off:off + length].reshape(shape)


def kernel(x, c, w_ada, b_ada, norm_w, w_in, b_in, conv_w, conv_b, rpb, ml_norm_w, w_out, final_norm_w, loss_target, m_w_ada, m_b_ada, m_norm_w, m_w_in, m_b_in, m_conv_w, m_conv_b, m_rpb, m_ml_norm_w, m_w_out, m_final_norm_w, v_w_ada, v_b_ada, v_norm_w, v_w_in, v_b_in, v_conv_w, v_conv_b, v_rpb, v_ml_norm_w, v_w_out, v_final_norm_w):
    T = x.shape[1]
    rows = T // GRID_W
    me = 4 * lax.axis_index("x") + 2 * lax.axis_index("y") + lax.axis_index("c")
    xs = x.reshape(T, D_MODEL)
    target = loss_target.reshape(T, D_MODEL)
    n_in = w_in.shape[2]
    n_ada = w_ada.shape[2]
    n_cw = conv_w.shape[2]
    n_wo = w_out.shape[1]

    g_w_in, g_w_out, g_conv_w, g_c = _exchange(
        [w_in[0], w_out[0], conv_w[0], c], [False] * 4, "gather_weights")
    w_in_full = g_w_in.transpose(1, 0, 2).reshape(D_MODEL, N_DEV * n_in)
    w_in_bf = jnp.pad(w_in_full, ((0, 0), (0, IN_PAD - IN_W))).astype(_BF16)
    b_in_pad = jnp.pad(b_in, ((0, 0), (0, IN_PAD - IN_W)))
    w_out_bf = g_w_out.reshape(N_DEV * n_wo, D_MODEL).astype(_BF16)
    conv_w_full = jnp.pad(g_conv_w.transpose(1, 0, 2).reshape(CONV_W, N_DEV * n_cw), ((0, 3), (0, 0)))
    c_all = g_c.reshape(N_DEV, D_MODEL)

    b_ada_my = lax.dynamic_slice(b_ada, (0, me * n_ada), (1, n_ada))
    (mod_slots,) = _exchange([_mod_part(c_all, w_ada[0], b_ada_my)], [False], "gather_mod")
    mod = lax.dynamic_index_in_dim(mod_slots, me, axis=1, keepdims=False).reshape(1, 3 * D_MODEL)
    shift, scale, gate = mod[:, :D_MODEL], mod[:, D_MODEL:2 * D_MODEL], mod[:, 2 * D_MODEL:]
    scale1p = 1.0 + scale

    qkv, rest, h_bf = _inproj_fwd(xs, scale1p, shift, norm_w, w_in_bf, b_in_pad)
    bias, bias_t = _na_bias_tables(rpb[0], rows)
    o_na, lse = _na_fwd(qkv, bias, T)
    qk_act = _conv_fwd(rest, conv_w_full, conv_b, T)
    h_f, cs_f, ns_f, ms_f = _mlstm_fwd(qk_act, rest, T, False)
    h_b, cs_b, ns_b, ms_b = _mlstm_fwd(qk_act, rest, T, True)

    dx1, d_o, dz_na, d_h, d_mo, d_mz, dwo, pvec = _post(
        xs, target, o_na, rest, h_f, h_b, gate, ml_norm_w, final_norm_w.reshape(1, D_MODEL), w_out_bf, T)

    dq_na, delta, dbias = _na_bwd_q(qkv, bias, o_na, d_o, lse, T)
    dk_na, dv_na = _na_bwd_kv(qkv, bias_t, d_o, _rows_layout(lse, T), _rows_layout(delta, T), T)
    d_rpb = _rpb_grad(dbias, rows)
    dqk_f, dv_f, dg_f = _mlstm_bwd(qk_act, rest, d_h, cs_f, ns_f, ms_f, T, False)
    dqk_b, dv_b, dg_b = _mlstm_bwd(qk_act, rest, d_h, cs_b, ns_b, ms_b, T, True)
    d_u, dconv = _conv_bwd(rest, conv_w_full, conv_b, dqk_f, dqk_b, T)

    sections = [(0, 512, [(dq_na, 0)]), (512, 512, [(dk_na, 0)]), (1024, 512, [(dv_na, 0)]),
                (1536, 512, [(dz_na, 0)]), (2048, 512, [(d_u, 0)]), (2560, 512, [(d_u, 1)]),
                (3072, 512, [(dv_f, 0), (dv_b, 0)]), (3584, 512, [(d_mo, 0)]), (4096, 512, [(d_mz, 0)]),
                (4608, 128, [(dg_f, 0), (dg_b, 0)])]
    grad_x, xvec = _inproj_bwd_x(xs, dx1, scale1p, norm_w, w_in_bf, sections, T)
    dws, dbs = [], []
    for si, sec in enumerate(sections):
        dw_s, db_s = _inproj_bwd_w(h_bf, sec, T, "inproj_bwd_w%d" % si)
        dws.append(dw_s)
        dbs.append(db_s[0])
    dw_in = jnp.concatenate(dws, axis=1)[:, :IN_W]
    db_in = jnp.concatenate(dbs)[:IN_W]

    small = _pack({
        "b_ada": jnp.concatenate([xvec[0], xvec[1], pvec[1]]),
        "norm_w": xvec[2], "b_in": db_in, "conv_w": dconv[:CONV_W], "conv_b": dconv[CONV_W],
        "rpb": d_rpb, "ml_norm_w": pvec[2, :ML_W], "final_norm_w": pvec[0], "loss": pvec[3, :1]})
    s_small, s_w_in, s_w_out = _exchange(
        [small, dw_in.reshape(D_MODEL, N_DEV, n_in).transpose(1, 0, 2), dwo.reshape(N_DEV, n_wo, D_MODEL)],
        [False, True, True], "exchange_grads")

    g_w_in_s, d_w_in, nm_w_in, nv_w_in = _adamw_slots(w_in[0], m_w_in[0], v_w_in[0], s_w_in, 128, "adamw_w_in")
    g_w_out_s, d_w_out, nm_w_out, nv_w_out = _adamw_slots(w_out[0], m_w_out[0], v_w_out[0], s_w_out, n_wo,
                                                          "adamw_w_out")
    dmod_all = s_small[:, 0, :3 * D_MODEL]
    dmod_my = lax.dynamic_slice(dmod_all, (0, me * n_ada), (N_DEV, n_ada))
    g_w_ada, d_w_ada, nm_w_ada, nv_w_ada = _w_ada_update(c_all, dmod_my, w_ada[0], m_w_ada[0], v_w_ada[0])

    def embed(shard):
        return lax.dynamic_update_slice(jnp.zeros((CONV_W, N_DEV * n_cw), _F32), shard[0], (0, me * n_cw))

    zero1 = jnp.zeros((1,), _F32)
    packed = lambda b_a, n_w, b_i, c_w, c_b, rp, mn, fn: _pack({
        "b_ada": b_a, "norm_w": n_w, "b_in": b_i, "conv_w": embed(c_w), "conv_b": c_b, "rpb": rp,
        "ml_norm_w": mn, "final_norm_w": fn, "loss": zero1})
    pw = packed(b_ada, norm_w, b_in, conv_w, conv_b, rpb, ml_norm_w, final_norm_w)
    pm = packed(m_b_ada, m_norm_w, m_b_in, m_conv_w, m_conv_b, m_rpb, m_ml_norm_w, m_final_norm_w)
    pv = packed(v_b_ada, v_norm_w, v_b_in, v_conv_w, v_conv_b, v_rpb, v_ml_norm_w, v_final_norm_w)
    sg, sd, sm, sv = _adamw_slots(pw, pm, pv, s_small, 1, "adamw_small")

    def small_outs(vec):
        cw = lax.dynamic_slice(_unpack(vec, "conv_w", (CONV_W, N_DEV * n_cw)), (0, me * n_cw), (CONV_W, n_cw))
        return dict(b_ada=_unpack(vec, "b_ada", b_ada.shape), norm_w=_unpack(vec, "norm_w", norm_w.shape),
                    b_in=_unpack(vec, "b_in", b_in.shape), conv_w=cw[None],
                    conv_b=_unpack(vec, "conv_b", conv_b.shape), rpb=_unpack(vec, "rpb", rpb.shape),
                    ml_norm_w=_unpack(vec, "ml_norm_w", ml_norm_w.shape),
                    final_norm_w=_unpack(vec, "final_norm_w", final_norm_w.shape))

    loss = _unpack(sg, "loss", ())
    order = ("w_ada", "b_ada", "norm_w", "w_in", "b_in", "conv_w", "conv_b", "rpb", "ml_norm_w", "w_out",
             "final_norm_w")
    outs = []
    for vec, big in ((sg, (g_w_ada, g_w_in_s, g_w_out_s)), (sd, (d_w_ada, d_w_in, d_w_out)),
                     (sm, (nm_w_ada, nm_w_in, nm_w_out)), (sv, (nv_w_ada, nv_w_in, nv_w_out))):
        group = small_outs(vec)
        group.update(w_ada=big[0][None], w_in=big[1][None], w_out=big[2][None])
        outs.extend(group[name] for name in order)
    return (loss, grad_x.reshape(x.shape), *outs)
```

```python
import functools

import numpy as np
import jax
import jax.numpy as jnp
from jax import lax
from jax.experimental import pallas as pl
from jax.experimental.pallas import tpu as pltpu

N_DEV = 8
D_MODEL = 1024
GRID_W = 64
NA_HEADS = 8
NA_HEAD_DIM = 64
NA_KH = 8
NA_KW = 16
NA_W = 512
ML_HEADS = 4
ML_HEAD_DIM = 128
ML_W = 512
ML_CHUNK = 512
CONV_W = 5
EPS = 1e-6
IN_W = 4624
IN_PAD = 4736
REST_W = IN_PAD - 3 * NA_W
GATE_COL = 3072
NEG = -1e30
NA_RB = 8
NA_WIN = 16
ML_CB = 1
ADAM_LR = 0.001
ADAM_B1 = 0.9
ADAM_B2 = 0.999
ADAM_EPS = 1e-08
ADAM_WD = 0.01
ADAM_STEP = 10
VMEM_LIMIT = 56 * 1024 * 1024

_F32 = jnp.float32
_BF16 = jnp.bfloat16
_HI = lax.Precision.HIGHEST


def _cparams(sem=None):
    return pltpu.CompilerParams(dimension_semantics=sem, vmem_limit_bytes=VMEM_LIMIT)


def _nt(a, b):
    return lax.dot_general(a, b, (((1,), (1,)), ((), ())), preferred_element_type=_F32)


def _tn(a, b):
    return lax.dot_general(a, b, (((0,), (0,)), ((), ())), preferred_element_type=_F32)


def _nn(a, b):
    return jnp.dot(a, b, preferred_element_type=_F32)


def _sigmoid(x):
    return 1.0 / (1.0 + jnp.exp(-x))


def _silu(x):
    return x * _sigmoid(x)


def _dsilu(x):
    s = _sigmoid(x)
    return s * (1.0 + x * (1.0 - s))


def _exchange(arrs, scatter, name):
    n = len(arrs)
    out_shape = []
    for a, sc in zip(arrs, scatter):
        blk = a.shape[1:] if sc else a.shape
        out_shape.append(jax.ShapeDtypeStruct((N_DEV,) + tuple(blk), a.dtype))

    def body(*refs):
        ins = refs[:n]
        outs = refs[n:2 * n]
        send_sems, recv_sems, local_sems = refs[2 * n:]
        x, y, c = lax.axis_index("x"), lax.axis_index("y"), lax.axis_index("c")
        me = 4 * x + 2 * y + c
        local, sends, recvs = [], [], []
        for a in range(n):
            own = ins[a].at[me] if scatter[a] else ins[a]
            cp = pltpu.make_async_copy(own, outs[a].at[me], local_sems.at[a])
            cp.start()
            local.append(cp)
            for k in range(1, N_DEV):
                px = 1 - x if k & 4 else x
                py = 1 - y if k & 2 else y
                pc = 1 - c if k & 1 else c
                p = 4 * px + 2 * py + pc
                src = ins[a].at[p] if scatter[a] else ins[a]
                snd = pltpu.make_async_remote_copy(
                    src_ref=src, dst_ref=outs[a].at[me],
                    send_sem=send_sems.at[a, k - 1], recv_sem=recv_sems.at[a, k - 1],
                    device_id=(px, py, pc), device_id_type=pl.DeviceIdType.MESH)
                snd.start()
                sends.append(snd)
                rcv = pltpu.make_async_remote_copy(
                    src_ref=src, dst_ref=outs[a].at[p],
                    send_sem=send_sems.at[a, k - 1], recv_sem=recv_sems.at[a, k - 1],
                    device_id=(px, py, pc), device_id_type=pl.DeviceIdType.MESH)
                recvs.append(rcv)
        for rcv in recvs:
            rcv.wait_recv()
        for snd in sends:
            snd.wait_send()
        for cp in local:
            cp.wait()

    any_spec = pl.BlockSpec(memory_space=pl.ANY)
    res = pl.pallas_call(
        body, name=name, out_shape=tuple(out_shape),
        in_specs=[any_spec] * n, out_specs=tuple([any_spec] * n),
        scratch_shapes=[pltpu.SemaphoreType.DMA((n, N_DEV - 1)),
                        pltpu.SemaphoreType.DMA((n, N_DEV - 1)),
                        pltpu.SemaphoreType.DMA((n,))],
    )(*arrs)
    return list(res)


def _mod_part(c_all, w_ada, b_my):
    def body(c_ref, w_ref, b_ref, o_ref):
        o_ref[...] = jnp.dot(_silu(c_ref[...]), w_ref[...], precision=_HI,
                             preferred_element_type=_F32) + b_ref[...]

    return pl.pallas_call(
        body, name="mod_part",
        out_shape=jax.ShapeDtypeStruct((N_DEV, w_ada.shape[1]), _F32),
        compiler_params=_cparams(),
    )(c_all, w_ada, b_my)


def _inproj_fwd(x, scale1p, shift, norm_w, w_in_bf, b_in_pad):
    T = x.shape[1]
    tm = 256
    n_q = 3 * NA_W

    def body(x_ref, sc_ref, sh_ref, nw_ref, w_ref, b_ref, qkv_ref, rest_ref, h_ref):
        xv = x_ref[...]
        r = lax.rsqrt(jnp.mean(xv * xv, axis=-1, keepdims=True) + EPS)
        h = xv * r * nw_ref[...] * sc_ref[...] + sh_ref[...]
        hb = h.astype(_BF16)
        h_ref[...] = hb
        for n0 in range(0, IN_PAD, 512):
            wd = min(512, IN_PAD - n0)
            acc = _nn(hb, w_ref[:, n0:n0 + wd]) + b_ref[:, n0:n0 + wd]
            if n0 == 0:
                acc = acc * (NA_HEAD_DIM ** -0.5)
            if n0 < n_q:
                qkv_ref[:, n0:n0 + wd] = acc.astype(_BF16)
            else:
                rest_ref[:, n0 - n_q:n0 - n_q + wd] = acc

    row = lambda w: pl.BlockSpec((1, w), lambda i: (0, 0))
    return pl.pallas_call(
        body, name="inproj_fwd", grid=(T // tm,),
        in_specs=[pl.BlockSpec((None, tm, D_MODEL), lambda i: (0, i, 0)), row(D_MODEL), row(D_MODEL), row(D_MODEL),
                  pl.BlockSpec((D_MODEL, IN_PAD), lambda i: (0, 0)), row(IN_PAD)],
        out_specs=(pl.BlockSpec((tm, n_q), lambda i: (i, 0)),
                   pl.BlockSpec((tm, REST_W), lambda i: (i, 0)),
                   pl.BlockSpec((tm, D_MODEL), lambda i: (i, 0))),
        out_shape=(jax.ShapeDtypeStruct((T, n_q), _BF16),
                   jax.ShapeDtypeStruct((T, REST_W), _F32),
                   jax.ShapeDtypeStruct((T, D_MODEL), _BF16)),
        compiler_params=_cparams(("arbitrary",)),
    )(x, scale1p, shift, norm_w, w_in_bf, b_in_pad)


def _na_class_rows(rows):
    nb = rows // NA_RB
    out = []
    for rb in (0, min(1, nb - 1), nb - 1):
        ws = int(np.clip(NA_RB * rb - 4, 0, rows - NA_WIN))
        out.append((NA_RB * rb + np.arange(NA_RB), ws + np.arange(NA_WIN)))
    return out


def _na_pair_index(rows, qrows, krows):
    start = lambda r: np.clip(r - NA_KH // 2, 0, rows - NA_KH)
    col = np.arange(GRID_W)
    cstart = np.clip(col - NA_KW // 2, 0, GRID_W - NA_KW)
    dy = krows[None, :] - qrows[:, None] + NA_KH - 1
    vr = (krows[None, :] >= start(qrows)[:, None]) & (krows[None, :] < start(qrows)[:, None] + NA_KH)
    dx = np.clip(col[None, :] - col[:, None], -(NA_KW - 1), NA_KW - 1) + NA_KW - 1
    vc = (col[None, :] >= cstart[:, None]) & (col[None, :] < cstart[:, None] + NA_KW)
    nq, nk = len(qrows), len(krows)
    dy4 = np.broadcast_to(np.clip(dy, 0, 2 * NA_KH - 2)[:, None, :, None], (nq, GRID_W, nk, GRID_W))
    dx4 = np.broadcast_to(dx[None, :, None, :], (nq, GRID_W, nk, GRID_W))
    valid = vr[:, None, :, None] & vc[None, :, None, :]
    idx = (dy4 * (2 * NA_KW - 1) + dx4).reshape(nq * GRID_W, nk * GRID_W)
    return idx.astype(np.int32), valid.reshape(nq * GRID_W, nk * GRID_W), (dy, vr, dx, vc)


def _na_half_slabs(rpb):
    _, _, (_, _, dx, vc) = _na_pair_index(NA_WIN, np.arange(1), np.arange(1))
    xs, ys = np.meshgrid(np.arange(GRID_W), np.arange(GRID_W), indexing="ij")
    consts = []
    for trans in (False, True):
        qc, kc = (ys, xs) if trans else (xs, ys)
        for right in (False, True):
            pos = (xs * 128 + (GRID_W if right else 0) + ys).reshape(-1)
            oh = np.zeros((32, GRID_W * 128), np.float32)
            oh[dx[qc, kc].reshape(-1), pos] = 1.0
            col_neg = np.zeros((1, GRID_W * 128), np.float32)
            col_neg[0, pos] = np.where(vc[qc, kc].reshape(-1), 0.0, NEG)
            half = np.zeros((1, GRID_W * 128), np.float32)
            half[0, pos] = 1.0
            consts += [jnp.asarray(oh), jnp.asarray(col_neg), jnp.asarray(half)]
    row_neg = np.where(np.arange(NA_HEADS * 16) % 16 == 15, NEG, 0.0).astype(np.float32).reshape(-1, 1)
    rp = jnp.pad(rpb, ((0, 0), (0, 1), (0, 1))).reshape(NA_HEADS * 16, 32)

    def body(*refs):
        r_ref, rn_ref = refs[0], refs[1]
        for t in range(4):
            oh_ref, cn_ref, half_ref = refs[2 + 3 * t:5 + 3 * t]
            refs[14 + t][...] = (jnp.dot(r_ref[...], oh_ref[...], precision=_HI, preferred_element_type=_F32)
                                 + cn_ref[...] + rn_ref[...] * half_ref[...])

    outs = pl.pallas_call(
        body, name="na_half_slabs",
        out_shape=tuple(jax.ShapeDtypeStruct((NA_HEADS * 16, GRID_W * 128), _F32) for _ in range(4)),
        compiler_params=_cparams(),
    )(rp, jnp.asarray(row_neg), *consts)
    return [o.reshape(NA_HEADS, 16, GRID_W, 128) for o in outs]


def _na_bias_tables(rpb, rows):
    left, right, left_t, right_t = _na_half_slabs(rpb)
    didx, didx_t = [], []
    for blk, win in _na_class_rows(rows):
        _, _, (dy, vr, _, _) = _na_pair_index(rows, blk, win)
        didx.append(np.where(vr, dy, 15))
        _, _, (dy, vr, _, _) = _na_pair_index(rows, win, blk)
        didx_t.append(np.where(vr, dy, 15).T)

    def body(l_ref, r_ref, lt_ref, rt_ref, b_ref, bt_ref):
        for ci in range(3):
            for a in range(NA_RB):
                for j in range(NA_WIN // 2):
                    rs = slice(a * GRID_W, (a + 1) * GRID_W)
                    cs = slice(j * 128, (j + 1) * 128)
                    b_ref[0, ci, rs, cs] = (l_ref[0, int(didx[ci][a, 2 * j])]
                                            + r_ref[0, int(didx[ci][a, 2 * j + 1])])
                    bt_ref[0, ci, rs, cs] = (lt_ref[0, int(didx_t[ci][a, 2 * j])]
                                             + rt_ref[0, int(didx_t[ci][a, 2 * j + 1])])

    slab = pl.BlockSpec((1, 16, GRID_W, 128), lambda h: (h, 0, 0, 0))
    tab = pl.BlockSpec((1, 3, NA_RB * GRID_W, NA_WIN * GRID_W), lambda h: (h, 0, 0, 0))
    shape = jax.ShapeDtypeStruct((NA_HEADS, 3, NA_RB * GRID_W, NA_WIN * GRID_W), _F32)
    return pl.pallas_call(
        body, name="na_tables", grid=(NA_HEADS,),
        in_specs=[slab] * 4, out_specs=(tab, tab), out_shape=(shape, shape),
        compiler_params=_cparams(("arbitrary",)),
    )(left, right, left_t, right_t)


def _na_cls(i, nb):
    return jnp.where(i == 0, 0, jnp.where(i == nb - 1, 2, 1))


def _pair_rows_shape(T):
    return jax.ShapeDtypeStruct((NA_HEADS // 2, T // 256, 2, 256), _F32)


def _pair_rows_spec(tq):
    return pl.BlockSpec((1, tq // 256, 2, 256), lambda hp, rb: (hp, rb, 0, 0))


def _store_pair_rows(ref, col0, col1):
    tq = col0.shape[0]
    lane = lax.broadcasted_iota(jnp.int32, (1, 128), 1)
    tile = jnp.where(lane == 0, col0, jnp.where(lane == 1, col1, 0.0))
    rows = tile.T
    for j in range(tq // 256):
        ref[0, j] = rows[0:2, j * 256:(j + 1) * 256]


def _na_fwd(qkv, bias, T):
    rows = T // GRID_W
    nb = rows // NA_RB
    tq = NA_RB * GRID_W
    tw = NA_WIN * GRID_W

    def body(q_ref, k_ref, v_ref, b_ref, o_ref, l_ref, lr_ref):
        rb = pl.program_id(1)
        ws = pl.multiple_of(jnp.clip(NA_RB * rb - 4, 0, rows - NA_WIN) * GRID_W, 256)
        kw = k_ref[pl.ds(ws, tw), :]
        vw = v_ref[pl.ds(ws, tw), :]
        q = q_ref[...]
        lane = lax.broadcasted_iota(jnp.int32, (1, 128), 1)
        outs, lses = [], []
        for hh in range(2):
            msk = (lane < NA_HEAD_DIM) if hh == 0 else (lane >= NA_HEAD_DIM)
            s = _nt(jnp.where(msk, q, jnp.zeros_like(q)), kw) + b_ref[hh, 0]
            m = jnp.max(s, axis=1, keepdims=True)
            p = jnp.exp(s - m)
            l = jnp.sum(p, axis=1, keepdims=True)
            outs.append(_nn(p.astype(_BF16), vw) / l)
            lses.append(m + jnp.log(l))
        first = lane < NA_HEAD_DIM
        o_ref[...] = jnp.where(first, outs[0], outs[1])
        l_ref[...] = jnp.where(first, lses[0], lses[1])
        _store_pair_rows(lr_ref, lses[0], lses[1])

    blk = lambda off: pl.BlockSpec((tq, 128), lambda hp, rb: (rb, off + hp))
    whole = lambda off: pl.BlockSpec((T, 128), lambda hp, rb: (0, off + hp))
    return pl.pallas_call(
        body, name="na_fwd", grid=(NA_HEADS // 2, nb),
        in_specs=[blk(0), whole(4), whole(8),
                  pl.BlockSpec((2, 1, tq, tw), lambda hp, rb: (hp, _na_cls(rb, nb), 0, 0))],
        out_specs=(blk(0), blk(0), _pair_rows_spec(tq)),
        out_shape=(jax.ShapeDtypeStruct((T, NA_W), _F32), jax.ShapeDtypeStruct((T, NA_W), _F32),
                   _pair_rows_shape(T)),
        compiler_params=_cparams(("arbitrary", "arbitrary")),
    )(qkv, qkv, qkv, bias)


def _na_bwd_q(qkv, bias, o, d_o, lse, T):
    rows = T // GRID_W
    nb = rows // NA_RB
    tq = NA_RB * GRID_W
    tw = NA_WIN * GRID_W

    def body(q_ref, k_ref, v_ref, b_ref, o_ref, do_ref, l_ref, dq_ref, dl_ref, db_ref):
        rb = pl.program_id(1)
        ws = pl.multiple_of(jnp.clip(NA_RB * rb - 4, 0, rows - NA_WIN) * GRID_W, 256)
        kw = k_ref[pl.ds(ws, tw), :]
        vw = v_ref[pl.ds(ws, tw), :]
        q = q_ref[...]
        d_ov = do_ref[...]
        prod = d_ov.astype(_F32) * o_ref[...]
        lse_v = l_ref[...]
        lane = lax.broadcasted_iota(jnp.int32, (1, 128), 1)
        first_visit = (rb == 0) | (rb == 1) | (rb == nb - 1)
        dqs, dls = [], []
        for hh in range(2):
            msk = (lane < NA_HEAD_DIM) if hh == 0 else (lane >= NA_HEAD_DIM)
            c0 = hh * NA_HEAD_DIM
            s = _nt(jnp.where(msk, q, jnp.zeros_like(q)), kw) + b_ref[hh, 0]
            p = jnp.exp(s - lse_v[:, c0:c0 + 1])
            dp = _nt(jnp.where(msk, d_ov, jnp.zeros_like(d_ov)), vw)
            delta = jnp.sum(jnp.where(msk, prod, 0.0), axis=1, keepdims=True)
            ds = p * (dp - delta)

            @pl.when(first_visit)
            def _():
                db_ref[hh, 0] = ds

            @pl.when(jnp.logical_not(first_visit))
            def _():
                db_ref[hh, 0] += ds

            dqs.append(_nn(ds.astype(_BF16), kw) * (NA_HEAD_DIM ** -0.5))
            dls.append(delta)
        first = lane < NA_HEAD_DIM
        dq_ref[...] = jnp.where(first, dqs[0], dqs[1])
        _store_pair_rows(dl_ref, dls[0], dls[1])

    blk = lambda off: pl.BlockSpec((tq, 128), lambda hp, rb: (rb, off + hp))
    whole = lambda off: pl.BlockSpec((T, 128), lambda hp, rb: (0, off + hp))
    tab = pl.BlockSpec((2, 1, tq, tw), lambda hp, rb: (hp, _na_cls(rb, nb), 0, 0))
    return pl.pallas_call(
        body, name="na_bwd_q", grid=(NA_HEADS // 2, nb),
        in_specs=[blk(0), whole(4), whole(8), tab, blk(0), blk(0), blk(0)],
        out_specs=(blk(0), _pair_rows_spec(tq), tab),
        out_shape=(jax.ShapeDtypeStruct((T, NA_W), _F32), _pair_rows_shape(T),
                   jax.ShapeDtypeStruct(bias.shape, _F32)),
        compiler_params=_cparams(("arbitrary", "arbitrary")),
    )(qkv, qkv, qkv, bias, o, d_o, lse)


def _na_bwd_kv(qkv, bias_t, d_o, lse_rows, delta_rows, T):
    rows = T // GRID_W
    nb = rows // NA_RB
    tk = NA_RB * GRID_W
    tw = NA_WIN * GRID_W

    def body(k_ref, v_ref, q_ref, do_ref, b_ref, l_ref, dl_ref, dk_ref, dv_ref):
        kb = pl.program_id(1)
        ws = pl.multiple_of(jnp.clip(NA_RB * kb - 4, 0, rows - NA_WIN) * GRID_W, 256)
        qw = q_ref[pl.ds(ws, tw), :]
        dow = do_ref[pl.ds(ws, tw), :]
        w0 = ws // 256
        lw = jnp.concatenate([l_ref[0, w0 + i] for i in range(tw // 256)], axis=1)
        dw = jnp.concatenate([dl_ref[0, w0 + i] for i in range(tw // 256)], axis=1)
        k = k_ref[...]
        v = v_ref[...]
        lane = lax.broadcasted_iota(jnp.int32, (1, 128), 1)
        dks, dvs = [], []
        for hh in range(2):
            msk = (lane < NA_HEAD_DIM) if hh == 0 else (lane >= NA_HEAD_DIM)
            lrow = lw[hh:hh + 1, :]
            drow = dw[hh:hh + 1, :]
            st =_nt(jnp.where(msk, k, jnp.zeros_like(k)), qw) + b_ref[hh, 0]
            pt = jnp.exp(st - lrow)
            dvs.append(_nn(pt.astype(_BF16), dow))
            dpt = _nt(jnp.where(msk, v, jnp.zeros_like(v)), dow)
            dst = pt * (dpt - drow)
            dks.append(_nn(dst.astype(_BF16), qw))
        first = lane < NA_HEAD_DIM
        dk_ref[...] = jnp.where(first, dks[0], dks[1])
        dv_ref[...] = jnp.where(first, dvs[0], dvs[1])

    blk = lambda off: pl.BlockSpec((tk, 128), lambda hp, kb: (kb, off + hp))
    whole = lambda off: pl.BlockSpec((T, 128), lambda hp, kb: (0, off + hp))
    rowspec = pl.BlockSpec((1,) + lse_rows.shape[1:], lambda hp, kb: (hp, 0, 0, 0))
    return pl.pallas_call(
        body, name="na_bwd_kv", grid=(NA_HEADS // 2, nb),
        in_specs=[blk(4), blk(8), whole(0), whole(0),
                  pl.BlockSpec((2, 1, tk, tw), lambda hp, kb: (hp, _na_cls(kb, nb), 0, 0)),
                  rowspec, rowspec],
        out_specs=(blk(0), blk(0)),
        out_shape=(jax.ShapeDtypeStruct((T, NA_W), _F32), jax.ShapeDtypeStruct((T, NA_W), _F32)),
        compiler_params=_cparams(("arbitrary", "arbitrary")),
    )(qkv, qkv, qkv, d_o, bias_t, lse_rows, delta_rows)


def _rpb_grad(dbias, rows):
    ncls = 3
    cls_rows = _na_class_rows(rows)
    nb = rows // NA_RB
    oh_y = np.zeros((16, ncls * NA_RB * NA_WIN), np.float32)
    dxm = None
    for ci, (blk, win) in enumerate(cls_rows):
        if nb > 1 and ci == 1 and nb == 2:
            continue
        _, _, (dy, vr, dx, vc) = _na_pair_index(rows, blk, win)
        for a in range(NA_RB):
            for b in range(NA_WIN):
                if vr[a, b]:
                    oh_y[dy[a, b], (ci * NA_RB + a) * NA_WIN + b] = 1.0
        dxm = (dx, vc)
    dx, vc = dxm
    oh_x = np.zeros((GRID_W * GRID_W, 128), np.float32)
    for a in range(GRID_W):
        for b in range(GRID_W):
            if vc[a, b]:
                oh_x[a * GRID_W + b, dx[a, b]] = 1.0
    xmat = dbias.reshape(NA_HEADS, ncls, NA_RB, GRID_W, NA_WIN, GRID_W).transpose(0, 1, 2, 4, 3, 5)
    xmat = xmat.reshape(NA_HEADS, ncls * NA_RB * NA_WIN, GRID_W * GRID_W)

    def body(x_ref, ox_ref, oy_ref, g_ref):
        y = jnp.dot(x_ref[0], ox_ref[...], precision=_HI, preferred_element_type=_F32)
        g_ref[0] = jnp.dot(oy_ref[...], y, precision=_HI, preferred_element_type=_F32)

    g = pl.pallas_call(
        body, name="rpb_grad", grid=(NA_HEADS,),
        in_specs=[pl.BlockSpec((1,) + xmat.shape[1:], lambda h: (h, 0, 0)),
                  pl.BlockSpec(oh_x.shape, lambda h: (0, 0)),
                  pl.BlockSpec(oh_y.shape, lambda h: (0, 0))],
        out_specs=pl.BlockSpec((1, 16, 128), lambda h: (h, 0, 0)),
        out_shape=jax.ShapeDtypeStruct((NA_HEADS, 16, 128), _F32),
        compiler_params=_cparams(("arbitrary",)),
    )(xmat, jnp.asarray(oh_x), jnp.asarray(oh_y))
    return g[:, :2 * NA_KH - 1, :2 * NA_KW - 1]


def _halo_specs(tm, width, col_of, T, order):
    hb = tm // 8
    last = T // 8 - 1
    if order == "ij":
        cur = pl.BlockSpec((tm, width), lambda i, j: (i, col_of(j)))
        prev = pl.BlockSpec((8, width), lambda i, j: (jnp.maximum(i * hb - 1, 0), col_of(j)))
        nxt = pl.BlockSpec((8, width), lambda i, j: (jnp.minimum((i + 1) * hb, last), col_of(j)))
    else:
        cur = pl.BlockSpec((tm, width), lambda j, i: (i, col_of(j)))
        prev = pl.BlockSpec((8, width), lambda j, i: (jnp.maximum(i * hb - 1, 0), col_of(j)))
        nxt = pl.BlockSpec((8, width), lambda j, i: (jnp.minimum((i + 1) * hb, last), col_of(j)))
    return [prev, cur, nxt]


def _extend(prev_ref, cur_ref, next_ref, i, n_i):
    prev = jnp.where(i > 0, prev_ref[...], 0.0)
    nxt = jnp.where(i < n_i - 1, next_ref[...], 0.0)
    return jnp.concatenate([prev, cur_ref[...], nxt], axis=0)


def _conv_fwd(rest, conv_w, conv_b, T):
    tm = 512
    n_i = T // tm
    n = tm + 16

    def body(p_ref, c_ref, n_ref, w_ref, b_ref, o_ref):
        i = pl.program_id(0)
        ext = _extend(p_ref, c_ref, n_ref, i, n_i)
        acc = jnp.zeros((tm, 512), _F32) + b_ref[...]
        for j in range(CONV_W):
            acc = acc + w_ref[j:j + 1, :] * pltpu.roll(ext, (2 - j) % n, 0)[8:8 + tm]
        o_ref[...] = _silu(acc)

    return pl.pallas_call(
        body, name="conv_fwd", grid=(n_i, 2),
        in_specs=_halo_specs(tm, 512, lambda j: 1 + j, T, "ij")
        + [pl.BlockSpec((8, 512), lambda i, j: (0, j)), pl.BlockSpec((1, 512), lambda i, j: (0, j))],
        out_specs=pl.BlockSpec((tm, 512), lambda i, j: (i, j)),
        out_shape=jax.ShapeDtypeStruct((T, 2 * ML_W), _F32),
        compiler_params=_cparams(("arbitrary", "arbitrary")),
    )(rest, rest, rest, conv_w, conv_b)


def _conv_bwd(rest, conv_w, conv_b, da_f, da_b, T):
    tm = 512
    n_i = T // tm
    n = tm + 16

    def body(up, uc, un, fp, fc, fn, bp, bc, bn, w_ref, b_ref, du_ref, dw_ref):
        i = pl.program_id(1)
        ext_u = _extend(up, uc, un, i, n_i)
        ext_da = _extend(fp, fc, fn, i, n_i) + _extend(bp, bc, bn, i, n_i)
        shifted = [pltpu.roll(ext_u, (2 - j) % n, 0) for j in range(CONV_W)]
        pre = jnp.zeros((n, 512), _F32) + b_ref[...]
        for j in range(CONV_W):
            pre = pre + w_ref[j:j + 1, :] * shifted[j]
        gidx = i * tm - 8 + lax.broadcasted_iota(jnp.int32, (n, 1), 0)
        dpre = jnp.where((gidx >= 0) & (gidx < T), ext_da * _dsilu(pre), 0.0)
        du = jnp.zeros((tm, 512), _F32)
        for j in range(CONV_W):
            du = du + w_ref[j:j + 1, :] * pltpu.roll(dpre, (j - 2) % n, 0)[8:8 + tm]
        du_ref[...] = du
        dpc = dpre[8:8 + tm]
        parts = [jnp.sum(dpc * shifted[j][8:8 + tm], axis=0, keepdims=True) for j in range(CONV_W)]
        parts.append(jnp.sum(dpc, axis=0, keepdims=True))
        parts.append(jnp.zeros((2, 512), _F32))
        upd = jnp.concatenate(parts, axis=0)

        @pl.when(i == 0)
        def _():
            dw_ref[...] = upd

        @pl.when(i > 0)
        def _():
            dw_ref[...] += upd

    return pl.pallas_call(
        body, name="conv_bwd", grid=(2, n_i),
        in_specs=_halo_specs(tm, 512, lambda j: 1 + j, T, "ji")
        + _halo_specs(tm, 512, lambda j: j, T, "ji") + _halo_specs(tm, 512, lambda j: j, T, "ji")
        + [pl.BlockSpec((8, 512), lambda j, i: (0, j)), pl.BlockSpec((1, 512), lambda j, i: (0, j))],
        out_specs=(pl.BlockSpec((tm, 512), lambda j, i: (i, j)), pl.BlockSpec((8, 512), lambda j, i: (0, j))),
        out_shape=(jax.ShapeDtypeStruct((T, 2 * ML_W), _F32), jax.ShapeDtypeStruct((8, 2 * ML_W), _F32)),
        compiler_params=_cparams(("arbitrary", "arbitrary")),
    )(rest, rest, rest, da_f, da_f, da_f, da_b, da_b, da_b, conv_w, conv_b)


def _ml_chunk(q, k, v, icol, fcol, C, n, m, rev):
    L = ML_CHUNK
    ri = lax.broadcasted_iota(jnp.int32, (L, L), 0)
    ci = lax.broadcasted_iota(jnp.int32, (L, L), 1)
    mask = (ci >= ri) if rev else (ci <= ri)
    eye = ri == ci
    to_row = lambda col: jnp.sum(jnp.where(eye, col, 0.0), axis=0, keepdims=True)
    to_col = lambda row: jnp.sum(jnp.where(eye, row, 0.0), axis=1, keepdims=True)
    lf = jnp.minimum(fcol, 0.0) - jnp.log(1.0 + jnp.exp(-jnp.abs(fcol)))
    b_col = jnp.sum(jnp.where(mask, to_row(lf), 0.0), axis=1, keepdims=True)
    b_row = to_row(b_col)
    bl = jnp.sum(lf, axis=0, keepdims=True)
    dlog = jnp.where(mask, b_col - b_row + to_row(icol), NEG)
    m_t = jnp.maximum(b_col + m, jnp.max(dlog, axis=1, keepdims=True))
    dm = jnp.exp(dlog - m_t)
    ks = k * (ML_HEAD_DIM ** -0.5)
    qb, kb, vb = q.astype(_BF16), ks.astype(_BF16), v.astype(_BF16)
    s = _nt(qb, kb) * dm
    g = jnp.exp(b_col + m - m_t)
    qc = _nt(qb, C.astype(_BF16))
    num = _nn(s.astype(_BF16), vb) + g * qc
    qn = jnp.sum(q * n, axis=1, keepdims=True)
    den = jnp.sum(s, axis=1, keepdims=True) + g * qn
    e_m = jnp.exp(-m_t)
    nrm = jnp.maximum(jnp.abs(den), e_m)
    h = num / nrm
    a_col = bl - b_col + icol
    m_new = jnp.maximum(bl + m, jnp.max(a_col, axis=0, keepdims=True))
    decay = jnp.exp(bl + m - m_new)
    w = jnp.exp(a_col - m_new)
    c_new = decay * C + _tn((w * v).astype(_BF16), kb)
    n_new = decay * n + jnp.sum(w * ks, axis=0, keepdims=True)
    aux = dict(mask=mask, to_col=to_col, dm=dm, ks=ks, qb=qb, kb=kb, vb=vb, s=s, g=g, qc=qc, qn=qn,
               den=den, e_m=e_m, nrm=nrm, decay=decay, w=w)
    return h, c_new, n_new, m_new, aux


def _ml_gate_cols(gt, head, rev):
    lane = lax.broadcasted_iota(jnp.int32, (1, 128), 1)
    ci = (8 if rev else 0) + head
    cf = ci + ML_HEADS
    icol = jnp.sum(jnp.where(lane == ci, gt, 0.0), axis=1, keepdims=True)
    fcol = jnp.sum(jnp.where(lane == cf, gt, 0.0), axis=1, keepdims=True)
    return icol, fcol, lane == ci, lane == cf


def _mlstm_fwd(qk_act, rest, T, rev):
    tb = ML_CB * ML_CHUNK
    nblk = T // tb
    nc = T // ML_CHUNK
    bi = (lambda i: nblk - 1 - i) if rev else (lambda i: i)

    def body(q_ref, k_ref, v_ref, g_ref, h_ref, cs_ref, ns_ref, ms_ref, c_scr, n_scr, m_scr):
        @pl.when(pl.program_id(0) == 0)
        def _():
            c_scr[...] = jnp.zeros_like(c_scr)
            n_scr[...] = jnp.zeros_like(n_scr)
            m_scr[...] = jnp.zeros_like(m_scr)

        def step(j, carry):
            c = (ML_CB - 1 - j) if rev else j
            r0 = pl.multiple_of(c * ML_CHUNK, ML_CHUNK)
            gt = g_ref[pl.ds(r0, ML_CHUNK), :]
            for hd in range(ML_HEADS):
                cols = slice(hd * ML_HEAD_DIM, (hd + 1) * ML_HEAD_DIM)
                icol, fcol, _, _ = _ml_gate_cols(gt, hd, rev)
                C = c_scr[hd]
                n = n_scr[hd:hd + 1, :]
                mrow = m_scr[hd:hd + 1, :]
                cs_ref[c, hd] = C
                ns_ref[c, hd:hd + 1, :] = n
                ms_ref[c, hd:hd + 1, :] = mrow
                h, c_new, n_new, m_new, _ = _ml_chunk(
                    q_ref[pl.ds(r0, ML_CHUNK), cols], k_ref[pl.ds(r0, ML_CHUNK), cols],
                    v_ref[pl.ds(r0, ML_CHUNK), cols], icol, fcol, C, n, mrow[:, 0:1], rev)
                h_ref[pl.ds(r0, ML_CHUNK), cols] = h
                c_scr[hd] = c_new
                n_scr[hd:hd + 1, :] = n_new
                m_scr[hd:hd + 1, :] = jnp.broadcast_to(m_new, (1, 128))
            return carry

        lax.fori_loop(0, ML_CB, step, 0)

    return pl.pallas_call(
        body, name="mlstm_fwd_rev" if rev else "mlstm_fwd", grid=(nblk,),
        in_specs=[pl.BlockSpec((tb, ML_W), lambda i: (bi(i), 0)),
                  pl.BlockSpec((tb, ML_W), lambda i: (bi(i), 1)),
                  pl.BlockSpec((tb, ML_W), lambda i: (bi(i), 3)),
                  pl.BlockSpec((tb, 128), lambda i: (bi(i), GATE_COL // 128))],
        out_specs=(pl.BlockSpec((tb, ML_W), lambda i: (bi(i), 0)),
                   pl.BlockSpec((ML_CB, ML_HEADS, 128, 128), lambda i: (bi(i), 0, 0, 0)),
                   pl.BlockSpec((ML_CB, ML_HEADS, 128), lambda i: (bi(i), 0, 0)),
                   pl.BlockSpec((ML_CB, ML_HEADS, 128), lambda i: (bi(i), 0, 0))),
        out_shape=(jax.ShapeDtypeStruct((T, ML_W), _F32),
                   jax.ShapeDtypeStruct((nc, ML_HEADS, 128, 128), _F32),
                   jax.ShapeDtypeStruct((nc, ML_HEADS, 128), _F32),
                   jax.ShapeDtypeStruct((nc, ML_HEADS, 128), _F32)),
        scratch_shapes=[pltpu.VMEM((ML_HEADS, 128, 128), _F32), pltpu.VMEM((8, 128), _F32),
                        pltpu.VMEM((8, 128), _F32)],
        compiler_params=_cparams(("arbitrary",)),
    )(qk_act, qk_act, rest, rest)


def _mlstm_bwd(qk_act, rest, d_h, cs, ns, ms, T, rev):
    tb = ML_CB * ML_CHUNK
    nblk = T // tb
    bi = (lambda i: i) if rev else (lambda i: nblk - 1 - i)

    def body(q_ref, k_ref, v_ref, g_ref, dh_ref, cs_ref, ns_ref, ms_ref,
             dqk_ref, dv_ref, dg_ref, dc_scr, dn_scr):
        @pl.when(pl.program_id(0) == 0)
        def _():
            dc_scr[...] = jnp.zeros_like(dc_scr)
            dn_scr[...] = jnp.zeros_like(dn_scr)

        def step(j, carry):
            c = j if rev else (ML_CB - 1 - j)
            r0 = pl.multiple_of(c * ML_CHUNK, ML_CHUNK)
            gt = g_ref[pl.ds(r0, ML_CHUNK), :]
            dgt = jnp.zeros((ML_CHUNK, 128), _F32)
            for hd in range(ML_HEADS):
                cols = slice(hd * ML_HEAD_DIM, (hd + 1) * ML_HEAD_DIM)
                icol, fcol, is_i, is_f = _ml_gate_cols(gt, hd, rev)
                q = q_ref[pl.ds(r0, ML_CHUNK), cols]
                k = k_ref[pl.ds(r0, ML_CHUNK), cols]
                v = v_ref[pl.ds(r0, ML_CHUNK), cols]
                C = cs_ref[c, hd]
                n = ns_ref[c, hd:hd + 1, :]
                m = ms_ref[c, hd:hd + 1, :][:, 0:1]
                dcn = dc_scr[hd]
                dnn = dn_scr[hd:hd + 1, :]
                h, _, _, _, a = _ml_chunk(q, k, v, icol, fcol, C, n, m, rev)
                d_hv = dh_ref[pl.ds(r0, ML_CHUNK), cols]
                g, s, w, ks = a["g"], a["s"], a["w"], a["ks"]
                qb, kb, vb = a["qb"], a["kb"], a["vb"]
                dnum = d_hv / a["nrm"]
                hdot = jnp.sum(d_hv * h, axis=1, keepdims=True)
                dden = jnp.where(jnp.abs(a["den"]) >= a["e_m"], -hdot / a["nrm"] * jnp.sign(a["den"]), 0.0)
                dnb = dnum.astype(_BF16)
                d_s = _nt(dnb, vb) + dden
                r = d_s * s
                dsqk = (d_s * a["dm"]).astype(_BF16)
                cb = C.astype(_BF16)
                dq = _nn(dsqk, kb) + g * _nn(dnb, cb) + (dden * g) * n
                dk = _tn(dsqk, qb)
                dv = _tn(s.astype(_BF16), dnb)
                dg = jnp.sum(dnum * a["qc"], axis=1, keepdims=True) + dden * a["qn"]
                db_col = jnp.sum(r, axis=1, keepdims=True) + dg * g
                cs_r = a["to_col"](jnp.sum(r, axis=0, keepdims=True))
                dc_chunk = _tn((g * dnum).astype(_BF16), qb)
                dn_chunk = jnp.sum((dden * g) * q, axis=0, keepdims=True)
                dcb = dcn.astype(_BF16)
                vdc = _nn(vb, dcb)
                kdc = _nt(kb, dcb)
                dw = jnp.sum(vdc * ks, axis=1, keepdims=True) + jnp.sum(ks * dnn, axis=1, keepdims=True)
                dv = dv + w * kdc
                dk = dk + w * vdc + w * dnn
                da = dw * w
                ddecay = (jnp.sum(jnp.sum(dcn * C, axis=1, keepdims=True), axis=0, keepdims=True)
                          + jnp.sum(dnn * n, axis=1, keepdims=True))
                dbl = ddecay * a["decay"] + jnp.sum(da, axis=0, keepdims=True)
                db_col = db_col - da - cs_r
                di_col = da + cs_r
                dlf = a["to_col"](jnp.sum(jnp.where(a["mask"], db_col, 0.0), axis=0, keepdims=True)) + dbl
                df_col = dlf * _sigmoid(-fcol)
                dc_scr[hd] = dc_chunk + a["decay"] * dcn
                dn_scr[hd:hd + 1, :] = dn_chunk + a["decay"] * dnn
                dqk_ref[pl.ds(r0, ML_CHUNK), cols] = dq
                dqk_ref[pl.ds(r0, ML_CHUNK), slice(ML_W + hd * 128, ML_W + (hd + 1) * 128)] = dk * (ML_HEAD_DIM ** -0.5)
                dv_ref[pl.ds(r0, ML_CHUNK), cols] = dv
                dgt = dgt + jnp.where(is_i, di_col, 0.0) + jnp.where(is_f, df_col, 0.0)
            dg_ref[pl.ds(r0, ML_CHUNK), :] = dgt
            return carry

        lax.fori_loop(0, ML_CB, step, 0)

    return pl.pallas_call(
        body, name="mlstm_bwd_rev" if rev else "mlstm_bwd", grid=(nblk,),
        in_specs=[pl.BlockSpec((tb, ML_W), lambda i: (bi(i), 0)),
                  pl.BlockSpec((tb, ML_W), lambda i: (bi(i), 1)),
                  pl.BlockSpec((tb, ML_W), lambda i: (bi(i), 3)),
                  pl.BlockSpec((tb, 128), lambda i: (bi(i), GATE_COL // 128)),
                  pl.BlockSpec((tb, ML_W), lambda i: (bi(i), 0)),
                  pl.BlockSpec((ML_CB, ML_HEADS, 128, 128), lambda i: (bi(i), 0, 0, 0)),
                  pl.BlockSpec((ML_CB, ML_HEADS, 128), lambda i: (bi(i), 0, 0)),
                  pl.BlockSpec((ML_CB, ML_HEADS, 128), lambda i: (bi(i), 0, 0))],
        out_specs=(pl.BlockSpec((tb, 2 * ML_W), lambda i: (bi(i), 0)),
                   pl.BlockSpec((tb, ML_W), lambda i: (bi(i), 0)),
                   pl.BlockSpec((tb, 128), lambda i: (bi(i), 0))),
        out_shape=(jax.ShapeDtypeStruct((T, 2 * ML_W), _F32), jax.ShapeDtypeStruct((T, ML_W), _F32),
                   jax.ShapeDtypeStruct((T, 128), _F32)),
        scratch_shapes=[pltpu.VMEM((ML_HEADS, 128, 128), _F32), pltpu.VMEM((8, 128), _F32)],
        compiler_params=_cparams(("arbitrary",)),
    )(qk_act, qk_act, rest, rest, d_h, cs, ns, ms)


def _post(x, target, o_na, rest, h_f, h_b, gate, ml_norm_w, final_w, w_out_bf, T):
    tm = 256
    n_i = T // tm

    def body(x_ref, t_ref, o_ref, zna_ref, hf_ref, hb_ref, mo_ref, mz_ref, gate_ref, mw_ref, fw_ref, w_ref,
             dx1_ref, do_ref, dzna_ref, dh_ref, dmo_ref, dmz_ref, dwo_ref, vec_ref):
        i = pl.program_id(0)
        gate_v = gate_ref[...]
        fw = fw_ref[...]
        zna = zna_ref[...]
        o = o_ref[...]
        na_out = o * _silu(zna)
        hsum = hf_ref[...] + hb_ref[...]
        sg = _sigmoid(mo_ref[...])
        hm = hsum * sg
        mz = mz_ref[...]
        smz = _silu(mz)
        hn_l, rstd_l, ml_l = [], [], []
        for hd in range(ML_HEADS):
            cols = slice(hd * 128, (hd + 1) * 128)
            hh = hm[:, cols]
            mu = jnp.mean(hh, axis=-1, keepdims=True)
            var = jnp.mean(jnp.square(hh - mu), axis=-1, keepdims=True)
            rstd = lax.rsqrt(var + EPS)
            hn = (hh - mu) * rstd
            hn_l.append(hn)
            rstd_l.append(rstd)
            ml_l.append(hn * mw_ref[:, cols] * smz[:, cols])
        mix = jnp.concatenate([na_out] + ml_l, axis=1).astype(_BF16)
        y = _nn(mix, w_ref[...])
        x1 = x_ref[...] + gate_v * y
        r = lax.rsqrt(jnp.mean(x1 * x1, axis=-1, keepdims=True) + EPS)
        xhat = x1 * r
        out = xhat * fw
        err = out - t_ref[...]
        loss = 0.5 * jnp.sum(jnp.sum(err * err, axis=1, keepdims=True), axis=0, keepdims=True) / D_MODEL
        dout = err * (1.0 / D_MODEL)
        dfw = jnp.sum(dout * xhat, axis=0, keepdims=True)
        dxhat = dout * fw
        dx1 = r * (dxhat - xhat * jnp.mean(dxhat * xhat, axis=-1, keepdims=True))
        dx1_ref[...] = dx1
        dgate = jnp.sum(dx1 * y, axis=0, keepdims=True)
        dy = (dx1 * gate_v).astype(_BF16)
        dmix = _nt(dy, w_ref[...])
        dwo = _tn(mix, dy)
        dna = dmix[:, :NA_W]
        do_ref[...] = (dna * _silu(zna)).astype(_BF16)
        dzna_ref[...] = dna * o * _dsilu(zna)
        dmw_l = []
        for hd in range(ML_HEADS):
            cols = slice(hd * 128, (hd + 1) * 128)
            dml = dmix[:, NA_W + hd * 128:NA_W + (hd + 1) * 128]
            hn = hn_l[hd]
            mwv = mw_ref[:, cols]
            dmz_ref[:, cols] = dml * hn * mwv * _dsilu(mz[:, cols])
            dhn = dml * mwv * smz[:, cols]
            dmw_l.append(jnp.sum(dml * hn * smz[:, cols], axis=0, keepdims=True))
            dhm = rstd_l[hd] * (dhn - jnp.mean(dhn, axis=-1, keepdims=True)
                                - hn * jnp.mean(dhn * hn, axis=-1, keepdims=True))
            sgc = sg[:, cols]
            dh_ref[:, cols] = dhm * sgc
            dmo_ref[:, cols] = dhm * hsum[:, cols] * sgc * (1.0 - sgc)
        dmw = jnp.concatenate(dmw_l + [jnp.zeros((1, D_MODEL - ML_W), _F32)], axis=1)
        lane = lax.broadcasted_iota(jnp.int32, (1, D_MODEL), 1)
        vec = jnp.concatenate([dfw, dgate, dmw, jnp.where(lane == 0, loss, 0.0),
                               jnp.zeros((4, D_MODEL), _F32)], axis=0)

        @pl.when(i == 0)
        def _():
            dwo_ref[...] = dwo
            vec_ref[...] = vec

        @pl.when(i > 0)
        def _():
            dwo_ref[...] += dwo
            vec_ref[...] += vec

    tok = lambda w, j: pl.BlockSpec((tm, w), lambda i: (i, j))
    tok3 = pl.BlockSpec((None, tm, D_MODEL), lambda i: (0, i, 0))
    row = lambda w: pl.BlockSpec((1, w), lambda i: (0, 0))
    f32 = lambda w: jax.ShapeDtypeStruct((T, w), _F32)
    return pl.pallas_call(
        body, name="post", grid=(n_i,),
        in_specs=[tok3, tok3, tok(NA_W, 0), tok(NA_W, 0), tok(ML_W, 0), tok(ML_W, 0),
                  tok(ML_W, 4), tok(ML_W, 5), row(D_MODEL), row(ML_W), row(D_MODEL),
                  pl.BlockSpec((D_MODEL, D_MODEL), lambda i: (0, 0))],
        out_specs=(tok(D_MODEL, 0), tok(NA_W, 0), tok(NA_W, 0), tok(ML_W, 0), tok(ML_W, 0), tok(ML_W, 0),
                   pl.BlockSpec((D_MODEL, D_MODEL), lambda i: (0, 0)),
                   pl.BlockSpec((8, D_MODEL), lambda i: (0, 0))),
        out_shape=(f32(D_MODEL), jax.ShapeDtypeStruct((T, NA_W), _BF16), f32(NA_W), f32(ML_W), f32(ML_W),
                   f32(ML_W), jax.ShapeDtypeStruct((D_MODEL, D_MODEL), _F32),
                   jax.ShapeDtypeStruct((8, D_MODEL), _F32)),
        compiler_params=_cparams(("arbitrary",)),
    )(x, target, o_na, rest, h_f, h_b, rest, rest, gate, ml_norm_w, final_w, w_out_bf)


def _section_specs(sections, tm):
    specs, args = [], []
    for _, width, parts in sections:
        for arr, cb in parts:
            specs.append(pl.BlockSpec((tm, width), functools.partial(lambda i, cb: (i, cb), cb=cb)))
            args.append(arr)
    return specs, args


def _section_values(sections, refs):
    vals, at = [], 0
    for _, _, parts in sections:
        v = refs[at][...]
        for r in refs[at + 1:at + len(parts)]:
            v = v + r[...]
        at += len(parts)
        vals.append(v.astype(_BF16))
    return vals


def _inproj_bwd_x(x, dx1, scale1p, norm_w, w_in_bf, sections, T):
    tm = 256
    sspecs, sargs = _section_specs(sections, tm)
    ns = len(sargs)

    def body(*refs):
        x_ref, dx1_ref, sc_ref, nw_ref, w_ref = refs[:5]
        srefs = refs[5:5 + ns]
        gx_ref, vec_ref = refs[5 + ns:]
        i = pl.program_id(0)
        vals = _section_values(sections, srefs)
        dh = jnp.zeros((tm, D_MODEL), _F32)
        for (c0, width, _), val in zip(sections, vals):
            dh = dh + _nt(val, w_ref[:, c0:c0 + width])
        xv = x_ref[...]
        r = lax.rsqrt(jnp.mean(xv * xv, axis=-1, keepdims=True) + EPS)
        xhat = xv * r
        nw = nw_ref[...]
        dshift = jnp.sum(dh, axis=0, keepdims=True)
        dscale = jnp.sum(dh * xhat * nw, axis=0, keepdims=True)
        dhpre = dh * sc_ref[...]
        dnw = jnp.sum(dhpre * xhat, axis=0, keepdims=True)
        dxhat = dhpre * nw
        gx_ref[...] = dx1_ref[...] + r * (dxhat - xhat * jnp.mean(dxhat * xhat, axis=-1, keepdims=True))
        vec = jnp.concatenate([dshift, dscale, dnw, jnp.zeros((5, D_MODEL), _F32)], axis=0)

        @pl.when(i == 0)
        def _():
            vec_ref[...] = vec

        @pl.when(i > 0)
        def _():
            vec_ref[...] += vec

    row = pl.BlockSpec((1, D_MODEL), lambda i: (0, 0))
    tok = pl.BlockSpec((tm, D_MODEL), lambda i: (i, 0))
    tok3 = pl.BlockSpec((None, tm, D_MODEL), lambda i: (0, i, 0))
    return pl.pallas_call(
        body, name="inproj_bwd_x", grid=(T // tm,),
        in_specs=[tok3, tok, row, row, pl.BlockSpec((D_MODEL, IN_PAD), lambda i: (0, 0))] + sspecs,
        out_specs=(tok3, pl.BlockSpec((8, D_MODEL), lambda i: (0, 0))),
        out_shape=(jax.ShapeDtypeStruct((1, T, D_MODEL), _F32), jax.ShapeDtypeStruct((8, D_MODEL), _F32)),
        compiler_params=_cparams(("arbitrary",)),
    )(x, dx1, scale1p, norm_w, w_in_bf, *sargs)


def _inproj_bwd_w(h_bf, section, T, name):
    tm = 512
    width = section[1]
    sspecs, sargs = _section_specs([section], tm)
    ns = len(sargs)

    def body(*refs):
        h_ref = refs[0]
        srefs = refs[1:1 + ns]
        dw_ref, db_ref = refs[1 + ns:]
        i = pl.program_id(0)
        v = srefs[0][...]
        for r in srefs[1:]:
            v = v + r[...]
        dw = _tn(h_ref[...], v.astype(_BF16))
        db = jnp.concatenate([jnp.sum(v, axis=0, keepdims=True), jnp.zeros((7, width), _F32)], axis=0)

        @pl.when(i == 0)
        def _():
            dw_ref[...] = dw
            db_ref[...] = db

        @pl.when(i > 0)
        def _():
            dw_ref[...] += dw
            db_ref[...] += db

    return pl.pallas_call(
        body, name=name, grid=(T // tm,),
        in_specs=[pl.BlockSpec((tm, D_MODEL), lambda i: (i, 0))] + sspecs,
        out_specs=(pl.BlockSpec((D_MODEL, width), lambda i: (0, 0)), pl.BlockSpec((8, width), lambda i: (0, 0))),
        out_shape=(jax.ShapeDtypeStruct((D_MODEL, width), _F32), jax.ShapeDtypeStruct((8, width), _F32)),
        compiler_params=_cparams(("arbitrary",)),
    )(h_bf, *sargs)


def _adamw_math(w, g, m, v):
    m = ADAM_B1 * m + (1.0 - ADAM_B1) * g
    v = ADAM_B2 * v + (1.0 - ADAM_B2) * jnp.square(g)
    m_hat = m / (1.0 - ADAM_B1 ** ADAM_STEP)
    v_hat = v / (1.0 - ADAM_B2 ** ADAM_STEP)
    delta = -ADAM_LR * (m_hat / (jnp.sqrt(v_hat) + ADAM_EPS) + ADAM_WD * w)
    return delta, m, v


def _adamw_slots(w, m, v, slots, tr, name):
    R, C = w.shape

    def body(w_ref, m_ref, v_ref, s_ref, g_ref, d_ref, nm_ref, nv_ref):
        g = s_ref[0].astype(_F32)
        for k in range(1, N_DEV):
            g = g + s_ref[k].astype(_F32)
        g_ref[...] = g
        d_ref[...], nm_ref[...], nv_ref[...] = _adamw_math(w_ref[...], g, m_ref[...], v_ref[...])

    blk = pl.BlockSpec((tr, C), lambda i: (i, 0))
    return pl.pallas_call(
        body, name=name, grid=(R // tr,),
        in_specs=[blk, blk, blk, pl.BlockSpec((N_DEV, tr, C), lambda i: (0, i, 0))],
        out_specs=(blk, blk, blk, blk),
        out_shape=tuple(jax.ShapeDtypeStruct((R, C), _F32) for _ in range(4)),
        compiler_params=_cparams(("arbitrary",)),
    )(w, m, v, slots)


def _w_ada_update(c_all, dmod_my, w, m, v):
    def body(c_ref, d_ref, w_ref, m_ref, v_ref, g_ref, dl_ref, nm_ref, nv_ref):
        g = lax.dot_general(_silu(c_ref[...]), d_ref[...], (((0,), (0,)), ((), ())),
                            precision=_HI, preferred_element_type=_F32)
        g_ref[...] = g
        dl_ref[...], nm_ref[...], nv_ref[...] = _adamw_math(w_ref[...], g, m_ref[...], v_ref[...])

    return pl.pallas_call(
        body, name="w_ada_update",
        out_shape=tuple(jax.ShapeDtypeStruct(w.shape, _F32) for _ in range(4)),
        compiler_params=_cparams(),
    )(c_all, dmod_my, w, m, v)


_PACK = (("b_ada", 3072, 3072), ("norm_w", 1024, 1024), ("b_in", IN_W, IN_PAD), ("conv_w", 5120, 5120),
         ("conv_b", 1024, 1024), ("rpb", 3720, 3840), ("ml_norm_w", 512, 512), ("final_norm_w", 1024, 1024),
         ("loss", 1, 128))
_PACK_OFF = {}
_off = 0
for _name, _len, _pad in _PACK:
    _PACK_OFF[_name] = (_off, _len)
    _off += _pad
_PACK_LEN = _off


def _pack(parts):
    cols = []
    for name, length, pad in _PACK:
        vec = parts[name].reshape(-1).astype(_F32)
        cols.append(jnp.pad(vec, (0, pad - length)))
    return jnp.concatenate(cols).reshape(1, _PACK_LEN)


def _unpack(vec, name, shape):
    off, length = _PACK_OFF[name]
    return vec.reshape(-1)[off:off + length].reshape(shape)


def kernel(x, c, w_ada, b_ada, norm_w, w_in, b_in, conv_w, conv_b, rpb, ml_norm_w, w_out, final_norm_w, loss_target, m_w_ada, m_b_ada, m_norm_w, m_w_in, m_b_in, m_conv_w, m_conv_b, m_rpb, m_ml_norm_w, m_w_out, m_final_norm_w, v_w_ada, v_b_ada, v_norm_w, v_w_in, v_b_in, v_conv_w, v_conv_b, v_rpb, v_ml_norm_w, v_w_out, v_final_norm_w):
    T = x.shape[1]
    rows = T // GRID_W
    me = 4 * lax.axis_index("x") + 2 * lax.axis_index("y") + lax.axis_index("c")
    n_in = w_in.shape[2]
    n_ada = w_ada.shape[2]
    n_cw = conv_w.shape[2]
    n_wo = w_out.shape[1]

    g_w_in, g_w_out, g_conv_w, g_c = _exchange(
        [w_in[0].astype(_BF16), w_out[0].astype(_BF16), conv_w[0], c], [False] * 4, "gather_weights")
    w_in_full = g_w_in.transpose(1, 0, 2).reshape(D_MODEL, N_DEV * n_in)
    w_in_bf = jnp.pad(w_in_full, ((0, 0), (0, IN_PAD - IN_W)))
    b_in_pad = jnp.pad(b_in, ((0, 0), (0, IN_PAD - IN_W)))
    w_out_bf = g_w_out.reshape(N_DEV * n_wo, D_MODEL)
    conv_w_full = jnp.pad(g_conv_w.transpose(1, 0, 2).reshape(CONV_W, N_DEV * n_cw), ((0, 3), (0, 0)))
    c_all = g_c.reshape(N_DEV, D_MODEL)

    b_ada_my = lax.dynamic_slice(b_ada, (0, me * n_ada), (1, n_ada))
    (mod_slots,) = _exchange([_mod_part(c_all, w_ada[0], b_ada_my)], [False], "gather_mod")
    mod = lax.dynamic_index_in_dim(mod_slots, me, axis=1, keepdims=False).reshape(1, 3 * D_MODEL)
    shift, scale, gate = mod[:, :D_MODEL], mod[:, D_MODEL:2 * D_MODEL], mod[:, 2 * D_MODEL:]
    scale1p = 1.0 + scale

    qkv, rest, h_bf = _inproj_fwd(x, scale1p, shift, norm_w, w_in_bf, b_in_pad)
    bias, bias_t = _na_bias_tables(rpb[0], rows)
    o_na, lse, lse_rows = _na_fwd(qkv, bias, T)
    qk_act = _conv_fwd(rest, conv_w_full, conv_b, T)
    h_f, cs_f, ns_f, ms_f = _mlstm_fwd(qk_act, rest, T, False)
    h_b, cs_b, ns_b, ms_b = _mlstm_fwd(qk_act, rest, T, True)

    dx1, d_o, dz_na, d_h, d_mo, d_mz, dwo, pvec = _post(
        x, loss_target, o_na, rest, h_f, h_b, gate, ml_norm_w, final_norm_w.reshape(1, D_MODEL), w_out_bf, T)

    dq_na, delta_rows, dbias = _na_bwd_q(qkv, bias, o_na, d_o, lse, T)
    dk_na, dv_na = _na_bwd_kv(qkv, bias_t, d_o, lse_rows, delta_rows, T)
    d_rpb = _rpb_grad(dbias, rows)
    dqk_f, dv_f, dg_f = _mlstm_bwd(qk_act, rest, d_h, cs_f, ns_f, ms_f, T, False)
    dqk_b, dv_b, dg_b = _mlstm_bwd(qk_act, rest, d_h, cs_b, ns_b, ms_b, T, True)
    d_u, dconv = _conv_bwd(rest, conv_w_full, conv_b, dqk_f, dqk_b, T)

    sections = [(0, 512, [(dq_na, 0)]), (512, 512, [(dk_na, 0)]), (1024, 512, [(dv_na, 0)]),
                (1536, 512, [(dz_na, 0)]), (2048, 512, [(d_u, 0)]), (2560, 512, [(d_u, 1)]),
                (3072, 512, [(dv_f, 0), (dv_b, 0)]), (3584, 512, [(d_mo, 0)]), (4096, 512, [(d_mz, 0)]),
                (4608, 128, [(dg_f, 0), (dg_b, 0)])]
    grad_x, xvec = _inproj_bwd_x(x, dx1, scale1p, norm_w, w_in_bf, sections, T)
    dws, dbs = [], []
    for si, sec in enumerate(sections):
        dw_s, db_s = _inproj_bwd_w(h_bf, sec, T, "inproj_bwd_w%d" % si)
        dws.append(dw_s)
        dbs.append(db_s[0])
    dw_in = jnp.concatenate(dws, axis=1)[:, :IN_W]
    db_in = jnp.concatenate(dbs)[:IN_W]

    small = _pack({
        "b_ada": jnp.concatenate([xvec[0], xvec[1], pvec[1]]),
        "norm_w": xvec[2], "b_in": db_in, "conv_w": dconv[:CONV_W], "conv_b": dconv[CONV_W],
        "rpb": d_rpb, "ml_norm_w": pvec[2, :ML_W], "final_norm_w": pvec[0], "loss": pvec[3, :1]})
    s_small, s_w_in, s_w_out = _exchange(
        [small, dw_in.astype(_BF16).reshape(D_MODEL, N_DEV, n_in).transpose(1, 0, 2),
         dwo.astype(_BF16).reshape(N_DEV, n_wo, D_MODEL)],
        [False, True, True], "exchange_grads")

    g_w_in_s, d_w_in, nm_w_in, nv_w_in = _adamw_slots(w_in[0], m_w_in[0], v_w_in[0], s_w_in, 128, "adamw_w_in")
    g_w_out_s, d_w_out, nm_w_out, nv_w_out = _adamw_slots(w_out[0], m_w_out[0], v_w_out[0], s_w_out, n_wo,
                                                          "adamw_w_out")
    dmod_all = s_small[:, 0, :3 * D_MODEL]
    dmod_my = lax.dynamic_slice(dmod_all, (0, me * n_ada), (N_DEV, n_ada))
    g_w_ada, d_w_ada, nm_w_ada, nv_w_ada = _w_ada_update(c_all, dmod_my, w_ada[0], m_w_ada[0], v_w_ada[0])

    def embed(shard):
        return lax.dynamic_update_slice(jnp.zeros((CONV_W, N_DEV * n_cw), _F32), shard[0], (0, me * n_cw))

    zero1 = jnp.zeros((1,), _F32)
    packed = lambda b_a, n_w, b_i, c_w, c_b, rp, mn, fn: _pack({
        "b_ada": b_a, "norm_w": n_w, "b_in": b_i, "conv_w": embed(c_w), "conv_b": c_b, "rpb": rp,
        "ml_norm_w": mn, "final_norm_w": fn, "loss": zero1})
    pw = packed(b_ada, norm_w, b_in, conv_w, conv_b, rpb, ml_norm_w, final_norm_w)
    pm = packed(m_b_ada, m_norm_w, m_b_in, m_conv_w, m_conv_b, m_rpb, m_ml_norm_w, m_final_norm_w)
    pv = packed(v_b_ada, v_norm_w, v_b_in, v_conv_w, v_conv_b, v_rpb, v_ml_norm_w, v_final_norm_w)
    sg, sd, sm, sv = _adamw_slots(pw, pm, pv, s_small, 1, "adamw_small")

    def small_outs(vec):
        cw = lax.dynamic_slice(_unpack(vec, "conv_w", (CONV_W, N_DEV * n_cw)), (0, me * n_cw), (CONV_W, n_cw))
        return dict(b_ada=_unpack(vec, "b_ada", b_ada.shape), norm_w=_unpack(vec, "norm_w", norm_w.shape),
                    b_in=_unpack(vec, "b_in", b_in.shape), conv_w=cw[None],
                    conv_b=_unpack(vec, "conv_b", conv_b.shape), rpb=_unpack(vec, "rpb", rpb.shape),
                    ml_norm_w=_unpack(vec, "ml_norm_w", ml_norm_w.shape),
                    final_norm_w=_unpack(vec, "final_norm_w", final_norm_w.shape))

    loss = _unpack(sg, "loss", ())
    order = ("w_ada", "b_ada", "norm_w", "w_in", "b_in", "conv_w", "conv_b", "rpb", "ml_norm_w", "w_out",
             "final_norm_w")
    outs = []
    for vec, big in ((sg, (g_w_ada, g_w_in_s, g_w_out_s)), (sd, (d_w_ada, d_w_in, d_w_out)),
                     (sm, (nm_w_ada, nm_w_in, nm_w_out)), (sv, (nv_w_ada, nv_w_in, nv_w_out))):
        group = small_outs(vec)
        group.update(w_ada=big[0][None], w_in=big[1][None], w_out=big[2][None])
        outs.extend(group[name] for name in order)
    return (loss, grad_x, *outs)
```

```python
import functools

import numpy as np
import jax
import jax.numpy as jnp
from jax import lax
from jax.experimental import pallas as pl
from jax.experimental.pallas import tpu as pltpu

N_DEV = 8
D_MODEL = 1024
GRID_W = 64
NA_HEADS = 8
NA_HEAD_DIM = 64
NA_KH = 8
NA_KW = 16
NA_W = 512
ML_HEADS = 4
ML_HEAD_DIM = 128
ML_W = 512
ML_CHUNK = 512
CONV_W = 5
EPS = 1e-6
IN_W = 4624
IN_PAD = 4736
REST_W = IN_PAD - 3 * NA_W
GATE_COL = 3072
NEG = -1e30
NA_RB = 8
NA_WIN = 16
ML_CB = 1
ADAM_LR = 0.001
ADAM_B1 = 0.9
ADAM_B2 = 0.999
ADAM_EPS = 1e-08
ADAM_WD = 0.01
ADAM_STEP = 10
VMEM_LIMIT = 56 * 1024 * 1024

_F32 = jnp.float32
_BF16 = jnp.bfloat16
_HI = lax.Precision.HIGHEST


def _cparams(sem=None):
    return pltpu.CompilerParams(dimension_semantics=sem, vmem_limit_bytes=VMEM_LIMIT)


def _nt(a, b):
    return lax.dot_general(a, b, (((1,), (1,)), ((), ())), preferred_element_type=_F32)


def _tn(a, b):
    return lax.dot_general(a, b, (((0,), (0,)), ((), ())), preferred_element_type=_F32)


def _nn(a, b):
    return jnp.dot(a, b, preferred_element_type=_F32)


def _sigmoid(x):
    return 1.0 / (1.0 + jnp.exp(-x))


def _silu(x):
    return x * _sigmoid(x)


def _dsilu(x):
    s = _sigmoid(x)
    return s * (1.0 + x * (1.0 - s))


def _exchange(arrs, scatter, name):
    n = len(arrs)
    out_shape = []
    for a, sc in zip(arrs, scatter):
        blk = a.shape[1:] if sc else a.shape
        out_shape.append(jax.ShapeDtypeStruct((N_DEV,) + tuple(blk), a.dtype))

    def body(*refs):
        ins = refs[:n]
        outs = refs[n:2 * n]
        send_sems, recv_sems, local_sems = refs[2 * n:]
        x, y, c = lax.axis_index("x"), lax.axis_index("y"), lax.axis_index("c")
        me = 4 * x + 2 * y + c
        local, sends, recvs = [], [], []
        for a in range(n):
            own = ins[a].at[me] if scatter[a] else ins[a]
            cp = pltpu.make_async_copy(own, outs[a].at[me], local_sems.at[a])
            cp.start()
            local.append(cp)
            for k in range(1, N_DEV):
                px = 1 - x if k & 4 else x
                py = 1 - y if k & 2 else y
                pc = 1 - c if k & 1 else c
                p = 4 * px + 2 * py + pc
                src = ins[a].at[p] if scatter[a] else ins[a]
                snd = pltpu.make_async_remote_copy(
                    src_ref=src, dst_ref=outs[a].at[me],
                    send_sem=send_sems.at[a, k - 1], recv_sem=recv_sems.at[a, k - 1],
                    device_id=(px, py, pc), device_id_type=pl.DeviceIdType.MESH)
                snd.start()
                sends.append(snd)
                rcv = pltpu.make_async_remote_copy(
                    src_ref=src, dst_ref=outs[a].at[p],
                    send_sem=send_sems.at[a, k - 1], recv_sem=recv_sems.at[a, k - 1],
                    device_id=(px, py, pc), device_id_type=pl.DeviceIdType.MESH)
                recvs.append(rcv)
        for rcv in recvs:
            rcv.wait_recv()
        for snd in sends:
            snd.wait_send()
        for cp in local:
            cp.wait()

    any_spec = pl.BlockSpec(memory_space=pl.ANY)
    res = pl.pallas_call(
        body, name=name, out_shape=tuple(out_shape),
        in_specs=[any_spec] * n, out_specs=tuple([any_spec] * n),
        scratch_shapes=[pltpu.SemaphoreType.DMA((n, N_DEV - 1)),
                        pltpu.SemaphoreType.DMA((n, N_DEV - 1)),
                        pltpu.SemaphoreType.DMA((n,))],
    )(*arrs)
    return list(res)


def _mod_part(c_all, w_ada, b_my):
    def body(c_ref, w_ref, b_ref, o_ref):
        o_ref[...] = jnp.dot(_silu(c_ref[...]), w_ref[...], precision=_HI,
                             preferred_element_type=_F32) + b_ref[...]

    return pl.pallas_call(
        body, name="mod_part",
        out_shape=jax.ShapeDtypeStruct((N_DEV, w_ada.shape[1]), _F32),
        compiler_params=_cparams(),
    )(c_all, w_ada, b_my)


def _inproj_fwd(x, scale1p, shift, norm_w, w_in_bf, b_in_pad):
    T = x.shape[1]
    tm = 256
    n_q = 3 * NA_W

    def body(x_ref, sc_ref, sh_ref, nw_ref, w_ref, b_ref, qkv_ref, rest_ref, h_ref):
        xv = x_ref[...]
        r = lax.rsqrt(jnp.mean(xv * xv, axis=-1, keepdims=True) + EPS)
        h = xv * r * nw_ref[...] * sc_ref[...] + sh_ref[...]
        hb = h.astype(_BF16)
        h_ref[...] = h.T.astype(_BF16)
        for n0 in range(0, IN_PAD, 512):
            wd = min(512, IN_PAD - n0)
            acc = _nn(hb, w_ref[:, n0:n0 + wd]) + b_ref[:, n0:n0 + wd]
            if n0 == 0:
                acc = acc * (NA_HEAD_DIM ** -0.5)
            if n0 < n_q:
                qkv_ref[:, n0:n0 + wd] = acc.astype(_BF16)
            else:
                rest_ref[:, n0 - n_q:n0 - n_q + wd] = acc

    row = lambda w: pl.BlockSpec((1, w), lambda i: (0, 0))
    return pl.pallas_call(
        body, name="inproj_fwd", grid=(T // tm,),
        in_specs=[pl.BlockSpec((None, tm, D_MODEL), lambda i: (0, i, 0)), row(D_MODEL), row(D_MODEL), row(D_MODEL),
                  pl.BlockSpec((D_MODEL, IN_PAD), lambda i: (0, 0)), row(IN_PAD)],
        out_specs=(pl.BlockSpec((tm, n_q), lambda i: (i, 0)),
                   pl.BlockSpec((tm, REST_W), lambda i: (i, 0)),
                   pl.BlockSpec((D_MODEL, tm), lambda i: (0, i))),
        out_shape=(jax.ShapeDtypeStruct((T, n_q), _BF16),
                   jax.ShapeDtypeStruct((T, REST_W), _F32),
                   jax.ShapeDtypeStruct((D_MODEL, T), _BF16)),
        compiler_params=_cparams(("arbitrary",)),
    )(x, scale1p, shift, norm_w, w_in_bf, b_in_pad)


def _na_class_rows(rows):
    nb = rows // NA_RB
    out = []
    for rb in (0, min(1, nb - 1), nb - 1):
        ws = int(np.clip(NA_RB * rb - 4, 0, rows - NA_WIN))
        out.append((NA_RB * rb + np.arange(NA_RB), ws + np.arange(NA_WIN)))
    return out


def _na_pair_index(rows, qrows, krows):
    start = lambda r: np.clip(r - NA_KH // 2, 0, rows - NA_KH)
    col = np.arange(GRID_W)
    cstart = np.clip(col - NA_KW // 2, 0, GRID_W - NA_KW)
    dy = krows[None, :] - qrows[:, None] + NA_KH - 1
    vr = (krows[None, :] >= start(qrows)[:, None]) & (krows[None, :] < start(qrows)[:, None] + NA_KH)
    dx = np.clip(col[None, :] - col[:, None], -(NA_KW - 1), NA_KW - 1) + NA_KW - 1
    vc = (col[None, :] >= cstart[:, None]) & (col[None, :] < cstart[:, None] + NA_KW)
    nq, nk = len(qrows), len(krows)
    dy4 = np.broadcast_to(np.clip(dy, 0, 2 * NA_KH - 2)[:, None, :, None], (nq, GRID_W, nk, GRID_W))
    dx4 = np.broadcast_to(dx[None, :, None, :], (nq, GRID_W, nk, GRID_W))
    valid = vr[:, None, :, None] & vc[None, :, None, :]
    idx = (dy4 * (2 * NA_KW - 1) + dx4).reshape(nq * GRID_W, nk * GRID_W)
    return idx.astype(np.int32), valid.reshape(nq * GRID_W, nk * GRID_W), (dy, vr, dx, vc)


def _na_half_slabs(rpb):
    _, _, (_, _, dx, vc) = _na_pair_index(NA_WIN, np.arange(1), np.arange(1))
    xs, ys = np.meshgrid(np.arange(GRID_W), np.arange(GRID_W), indexing="ij")
    consts = []
    for trans in (False, True):
        qc, kc = (ys, xs) if trans else (xs, ys)
        for right in (False, True):
            pos = (xs * 128 + (GRID_W if right else 0) + ys).reshape(-1)
            oh = np.zeros((32, GRID_W * 128), np.float32)
            oh[dx[qc, kc].reshape(-1), pos] = 1.0
            col_neg = np.zeros((1, GRID_W * 128), np.float32)
            col_neg[0, pos] = np.where(vc[qc, kc].reshape(-1), 0.0, NEG)
            half = np.zeros((1, GRID_W * 128), np.float32)
            half[0, pos] = 1.0
            consts += [jnp.asarray(oh), jnp.asarray(col_neg), jnp.asarray(half)]
    row_neg = np.where(np.arange(NA_HEADS * 16) % 16 == 15, NEG, 0.0).astype(np.float32).reshape(-1, 1)
    rp = jnp.pad(rpb, ((0, 0), (0, 1), (0, 1))).reshape(NA_HEADS * 16, 32)

    def body(*refs):
        r_ref, rn_ref = refs[0], refs[1]
        for t in range(4):
            oh_ref, cn_ref, half_ref = refs[2 + 3 * t:5 + 3 * t]
            refs[14 + t][...] = (jnp.dot(r_ref[...], oh_ref[...], precision=_HI, preferred_element_type=_F32)
                                 + cn_ref[...] + rn_ref[...] * half_ref[...])

    outs = pl.pallas_call(
        body, name="na_half_slabs",
        out_shape=tuple(jax.ShapeDtypeStruct((NA_HEADS * 16, GRID_W * 128), _F32) for _ in range(4)),
        compiler_params=_cparams(),
    )(rp, jnp.asarray(row_neg), *consts)
    return [o.reshape(NA_HEADS, 16, GRID_W, 128) for o in outs]


def _na_bias_tables(rpb, rows):
    left, right, left_t, right_t = _na_half_slabs(rpb)
    didx, didx_t = [], []
    for blk, win in _na_class_rows(rows):
        _, _, (dy, vr, _, _) = _na_pair_index(rows, blk, win)
        didx.append(np.where(vr, dy, 15))
        _, _, (dy, vr, _, _) = _na_pair_index(rows, win, blk)
        didx_t.append(np.where(vr, dy, 15).T)

    def body(l_ref, r_ref, lt_ref, rt_ref, b_ref, bt_ref):
        for ci in range(3):
            for a in range(NA_RB):
                for j in range(NA_WIN // 2):
                    rs = slice(a * GRID_W, (a + 1) * GRID_W)
                    cs = slice(j * 128, (j + 1) * 128)
                    b_ref[0, ci, rs, cs] = (l_ref[0, int(didx[ci][a, 2 * j])]
                                            + r_ref[0, int(didx[ci][a, 2 * j + 1])])
                    bt_ref[0, ci, rs, cs] = (lt_ref[0, int(didx_t[ci][a, 2 * j])]
                                             + rt_ref[0, int(didx_t[ci][a, 2 * j + 1])])

    slab = pl.BlockSpec((1, 16, GRID_W, 128), lambda h: (h, 0, 0, 0))
    tab = pl.BlockSpec((1, 3, NA_RB * GRID_W, NA_WIN * GRID_W), lambda h: (h, 0, 0, 0))
    shape = jax.ShapeDtypeStruct((NA_HEADS, 3, NA_RB * GRID_W, NA_WIN * GRID_W), _F32)
    return pl.pallas_call(
        body, name="na_tables", grid=(NA_HEADS,),
        in_specs=[slab] * 4, out_specs=(tab, tab), out_shape=(shape, shape),
        compiler_params=_cparams(("arbitrary",)),
    )(left, right, left_t, right_t)


def _na_cls(i, nb):
    return jnp.where(i == 0, 0, jnp.where(i == nb - 1, 2, 1))


def _pair_rows_shape(T):
    return jax.ShapeDtypeStruct((NA_HEADS // 2, T // 256, 2, 256), _F32)


def _pair_rows_spec(tq):
    return pl.BlockSpec((1, tq // 256, 2, 256), lambda hp, rb: (hp, rb, 0, 0))


def _store_pair_rows(ref, col0, col1):
    tq = col0.shape[0]
    lane = lax.broadcasted_iota(jnp.int32, (1, 128), 1)
    tile = jnp.where(lane == 0, col0, jnp.where(lane == 1, col1, 0.0))
    rows = tile.T
    for j in range(tq // 256):
        ref[0, j] = rows[0:2, j * 256:(j + 1) * 256]


def _na_fwd(qkv, bias, T):
    rows = T // GRID_W
    nb = rows // NA_RB
    tq = NA_RB * GRID_W
    tw = NA_WIN * GRID_W

    def body(q_ref, k_ref, v_ref, b_ref, o_ref, l_ref, lr_ref):
        rb = pl.program_id(1)
        ws = pl.multiple_of(jnp.clip(NA_RB * rb - 4, 0, rows - NA_WIN) * GRID_W, 256)
        kw = k_ref[pl.ds(ws, tw), :]
        vw = v_ref[pl.ds(ws, tw), :]
        q = q_ref[...]
        lane = lax.broadcasted_iota(jnp.int32, (1, 128), 1)
        outs, lses = [], []
        for hh in range(2):
            msk = (lane < NA_HEAD_DIM) if hh == 0 else (lane >= NA_HEAD_DIM)
            s = _nt(jnp.where(msk, q, jnp.zeros_like(q)), kw) + b_ref[hh, 0]
            m = jnp.max(s, axis=1, keepdims=True)
            p = jnp.exp(s - m)
            l = jnp.sum(p, axis=1, keepdims=True)
            outs.append(_nn(p.astype(_BF16), vw) / l)
            lses.append(m + jnp.log(l))
        first = lane < NA_HEAD_DIM
        o_ref[...] = jnp.where(first, outs[0], outs[1])
        l_ref[...] = jnp.where(first, lses[0], lses[1])
        _store_pair_rows(lr_ref, lses[0], lses[1])

    blk = lambda off: pl.BlockSpec((tq, 128), lambda hp, rb: (rb, off + hp))
    whole = lambda off: pl.BlockSpec((T, 128), lambda hp, rb: (0, off + hp))
    return pl.pallas_call(
        body, name="na_fwd", grid=(NA_HEADS // 2, nb),
        in_specs=[blk(0), whole(4), whole(8),
                  pl.BlockSpec((2, 1, tq, tw), lambda hp, rb: (hp, _na_cls(rb, nb), 0, 0))],
        out_specs=(blk(0), blk(0), _pair_rows_spec(tq)),
        out_shape=(jax.ShapeDtypeStruct((T, NA_W), _F32), jax.ShapeDtypeStruct((T, NA_W), _F32),
                   _pair_rows_shape(T)),
        compiler_params=_cparams(("arbitrary", "arbitrary")),
    )(qkv, qkv, qkv, bias)


def _na_bwd_q(qkv, bias, o, d_o, lse, T):
    rows = T // GRID_W
    nb = rows // NA_RB
    tq = NA_RB * GRID_W
    tw = NA_WIN * GRID_W

    def body(q_ref, k_ref, v_ref, b_ref, o_ref, do_ref, l_ref, dq_ref, dl_ref, db_ref):
        rb = pl.program_id(1)
        ws = pl.multiple_of(jnp.clip(NA_RB * rb - 4, 0, rows - NA_WIN) * GRID_W, 256)
        kw = k_ref[pl.ds(ws, tw), :]
        vw = v_ref[pl.ds(ws, tw), :]
        q = q_ref[...]
        d_ov = do_ref[...]
        prod = d_ov.astype(_F32) * o_ref[...]
        lse_v = l_ref[...]
        lane = lax.broadcasted_iota(jnp.int32, (1, 128), 1)
        first_visit = (rb == 0) | (rb == 1) | (rb == nb - 1)
        dqs, dls = [], []
        for hh in range(2):
            msk = (lane < NA_HEAD_DIM) if hh == 0 else (lane >= NA_HEAD_DIM)
            c0 = hh * NA_HEAD_DIM
            s = _nt(jnp.where(msk, q, jnp.zeros_like(q)), kw) + b_ref[hh, 0]
            p = jnp.exp(s - lse_v[:, c0:c0 + 1])
            dp = _nt(jnp.where(msk, d_ov, jnp.zeros_like(d_ov)), vw)
            delta = jnp.sum(jnp.where(msk, prod, 0.0), axis=1, keepdims=True)
            ds = p * (dp - delta)

            @pl.when(first_visit)
            def _():
                db_ref[hh, 0] = ds

            @pl.when(jnp.logical_not(first_visit))
            def _():
                db_ref[hh, 0] += ds

            dqs.append(_nn(ds.astype(_BF16), kw) * (NA_HEAD_DIM ** -0.5))
            dls.append(delta)
        first = lane < NA_HEAD_DIM
        dq_ref[...] = jnp.where(first, dqs[0], dqs[1])
        _store_pair_rows(dl_ref, dls[0], dls[1])

    blk = lambda off: pl.BlockSpec((tq, 128), lambda hp, rb: (rb, off + hp))
    whole = lambda off: pl.BlockSpec((T, 128), lambda hp, rb: (0, off + hp))
    tab = pl.BlockSpec((2, 1, tq, tw), lambda hp, rb: (hp, _na_cls(rb, nb), 0, 0))
    return pl.pallas_call(
        body, name="na_bwd_q", grid=(NA_HEADS // 2, nb),
        in_specs=[blk(0), whole(4), whole(8), tab, blk(0), blk(0), blk(0)],
        out_specs=(blk(0), _pair_rows_spec(tq), tab),
        out_shape=(jax.ShapeDtypeStruct((T, NA_W), _F32), _pair_rows_shape(T),
                   jax.ShapeDtypeStruct(bias.shape, _F32)),
        compiler_params=_cparams(("arbitrary", "arbitrary")),
    )(qkv, qkv, qkv, bias, o, d_o, lse)


def _na_bwd_kv(qkv, bias_t, d_o, lse_rows, delta_rows, T):
    rows = T // GRID_W
    nb = rows // NA_RB
    tk = NA_RB * GRID_W
    tw = NA_WIN * GRID_W

    def body(k_ref, v_ref, q_ref, do_ref, b_ref, l_ref, dl_ref, dk_ref, dv_ref):
        kb = pl.program_id(1)
        ws = pl.multiple_of(jnp.clip(NA_RB * kb - 4, 0, rows - NA_WIN) * GRID_W, 256)
        qw = q_ref[pl.ds(ws, tw), :]
        dow = do_ref[pl.ds(ws, tw), :]
        w0 = ws // 256
        lw = jnp.concatenate([l_ref[0, w0 + i] for i in range(tw // 256)], axis=1)
        dw = jnp.concatenate([dl_ref[0, w0 + i] for i in range(tw // 256)], axis=1)
        k = k_ref[...]
        v = v_ref[...]
        lane = lax.broadcasted_iota(jnp.int32, (1, 128), 1)
        dks, dvs = [], []
        for hh in range(2):
            msk = (lane < NA_HEAD_DIM) if hh == 0 else (lane >= NA_HEAD_DIM)
            lrow = lw[hh:hh + 1, :]
            drow = dw[hh:hh + 1, :]
            st =_nt(jnp.where(msk, k, jnp.zeros_like(k)), qw) + b_ref[hh, 0]
            pt = jnp.exp(st - lrow)
            dvs.append(_nn(pt.astype(_BF16), dow))
            dpt = _nt(jnp.where(msk, v, jnp.zeros_like(v)), dow)
            dst = pt * (dpt - drow)
            dks.append(_nn(dst.astype(_BF16), qw))
        first = lane < NA_HEAD_DIM
        dk_ref[...] = jnp.where(first, dks[0], dks[1])
        dv_ref[...] = jnp.where(first, dvs[0], dvs[1])

    blk = lambda off: pl.BlockSpec((tk, 128), lambda hp, kb: (kb, off + hp))
    whole = lambda off: pl.BlockSpec((T, 128), lambda hp, kb: (0, off + hp))
    rowspec = pl.BlockSpec((1,) + lse_rows.shape[1:], lambda hp, kb: (hp, 0, 0, 0))
    return pl.pallas_call(
        body, name="na_bwd_kv", grid=(NA_HEADS // 2, nb),
        in_specs=[blk(4), blk(8), whole(0), whole(0),
                  pl.BlockSpec((2, 1, tk, tw), lambda hp, kb: (hp, _na_cls(kb, nb), 0, 0)),
                  rowspec, rowspec],
        out_specs=(blk(0), blk(0)),
        out_shape=(jax.ShapeDtypeStruct((T, NA_W), _F32), jax.ShapeDtypeStruct((T, NA_W), _F32)),
        compiler_params=_cparams(("arbitrary", "arbitrary")),
    )(qkv, qkv, qkv, d_o, bias_t, lse_rows, delta_rows)


def _rpb_grad(dbias, rows):
    tw = NA_WIN * GRID_W
    offs = [int(win[0] - blk[0] + NA_KH - 1) for blk, win in _na_class_rows(rows)]

    def body(x_ref, g_ref):
        sub = lax.broadcasted_iota(jnp.int32, (NA_RB, 1), 0)
        qc = lax.broadcasted_iota(jnp.int32, (NA_RB * GRID_W, 1), 0) % GRID_W
        tot = jnp.zeros((NA_RB, tw), _F32)
        for ci in range(3):
            xv = x_ref[0, ci]
            for bit in range(6):
                xv = jnp.where(((qc >> bit) & 1) == 1, pltpu.roll(xv, tw - (1 << bit), 1), xv)
            acc = pltpu.roll(jnp.sum(xv.reshape(NA_RB, GRID_W, tw), axis=1), NA_KW, 1)
            for a in range(NA_RB):
                tot = tot + jnp.where(sub == a, pltpu.roll(acc, (GRID_W * (offs[ci] - a)) % tw, 1), 0.0)
        g_ref[0] = jnp.broadcast_to(jnp.sum(tot, axis=0, keepdims=True), (8, tw))

    g = pl.pallas_call(
        body, name="rpb_grad", grid=(NA_HEADS,),
        in_specs=[pl.BlockSpec((1,) + dbias.shape[1:], lambda h: (h, 0, 0, 0))],
        out_specs=pl.BlockSpec((1, 8, tw), lambda h: (h, 0, 0)),
        out_shape=jax.ShapeDtypeStruct((NA_HEADS, 8, tw), _F32),
        compiler_params=_cparams(("arbitrary",)),
    )(dbias)
    return g[:, 0].reshape(NA_HEADS, NA_WIN, GRID_W)[:, :2 * NA_KH - 1, 1:2 * NA_KW]


def _halo_specs(tm, width, col_of, T, order):
    hb = tm // 8
    last = T // 8 - 1
    if order == "ij":
        cur = pl.BlockSpec((tm, width), lambda i, j: (i, col_of(j)))
        prev = pl.BlockSpec((8, width), lambda i, j: (jnp.maximum(i * hb - 1, 0), col_of(j)))
        nxt = pl.BlockSpec((8, width), lambda i, j: (jnp.minimum((i + 1) * hb, last), col_of(j)))
    else:
        cur = pl.BlockSpec((tm, width), lambda j, i: (i, col_of(j)))
        prev = pl.BlockSpec((8, width), lambda j, i: (jnp.maximum(i * hb - 1, 0), col_of(j)))
        nxt = pl.BlockSpec((8, width), lambda j, i: (jnp.minimum((i + 1) * hb, last), col_of(j)))
    return [prev, cur, nxt]


def _extend(prev_ref, cur_ref, next_ref, i, n_i):
    prev = jnp.where(i > 0, prev_ref[...], 0.0)
    nxt = jnp.where(i < n_i - 1, next_ref[...], 0.0)
    return jnp.concatenate([prev, cur_ref[...], nxt], axis=0)


def _conv_fwd(rest, conv_w, conv_b, T):
    tm = 512
    n_i = T // tm
    n = tm + 16

    def body(p_ref, c_ref, n_ref, w_ref, b_ref, o_ref):
        i = pl.program_id(0)
        ext = _extend(p_ref, c_ref, n_ref, i, n_i)
        acc = jnp.zeros((tm, 512), _F32) + b_ref[...]
        for j in range(CONV_W):
            acc = acc + w_ref[j:j + 1, :] * pltpu.roll(ext, (2 - j) % n, 0)[8:8 + tm]
        o_ref[...] = _silu(acc)

    return pl.pallas_call(
        body, name="conv_fwd", grid=(n_i, 2),
        in_specs=_halo_specs(tm, 512, lambda j: 1 + j, T, "ij")
        + [pl.BlockSpec((8, 512), lambda i, j: (0, j)), pl.BlockSpec((1, 512), lambda i, j: (0, j))],
        out_specs=pl.BlockSpec((tm, 512), lambda i, j: (i, j)),
        out_shape=jax.ShapeDtypeStruct((T, 2 * ML_W), _F32),
        compiler_params=_cparams(("arbitrary", "arbitrary")),
    )(rest, rest, rest, conv_w, conv_b)


def _conv_bwd(rest, conv_w, conv_b, da_f, da_b, T):
    tm = 512
    n_i = T // tm
    n = tm + 16

    def body(up, uc, un, fp, fc, fn, bp, bc, bn, w_ref, b_ref, du_ref, dw_ref):
        i = pl.program_id(1)
        ext_u = _extend(up, uc, un, i, n_i)
        ext_da = _extend(fp, fc, fn, i, n_i) + _extend(bp, bc, bn, i, n_i)
        shifted = [pltpu.roll(ext_u, (2 - j) % n, 0) for j in range(CONV_W)]
        pre = jnp.zeros((n, 512), _F32) + b_ref[...]
        for j in range(CONV_W):
            pre = pre + w_ref[j:j + 1, :] * shifted[j]
        gidx = i * tm - 8 + lax.broadcasted_iota(jnp.int32, (n, 1), 0)
        dpre = jnp.where((gidx >= 0) & (gidx < T), ext_da * _dsilu(pre), 0.0)
        du = jnp.zeros((tm, 512), _F32)
        for j in range(CONV_W):
            du = du + w_ref[j:j + 1, :] * pltpu.roll(dpre, (j - 2) % n, 0)[8:8 + tm]
        du_ref[...] = du
        dpc = dpre[8:8 + tm]
        parts = [jnp.sum(dpc * shifted[j][8:8 + tm], axis=0, keepdims=True) for j in range(CONV_W)]
        parts.append(jnp.sum(dpc, axis=0, keepdims=True))
        parts.append(jnp.zeros((2, 512), _F32))
        upd = jnp.concatenate(parts, axis=0)

        @pl.when(i == 0)
        def _():
            dw_ref[...] = upd

        @pl.when(i > 0)
        def _():
            dw_ref[...] += upd

    return pl.pallas_call(
        body, name="conv_bwd", grid=(2, n_i),
        in_specs=_halo_specs(tm, 512, lambda j: 1 + j, T, "ji")
        + _halo_specs(tm, 512, lambda j: j, T, "ji") + _halo_specs(tm, 512, lambda j: j, T, "ji")
        + [pl.BlockSpec((8, 512), lambda j, i: (0, j)), pl.BlockSpec((1, 512), lambda j, i: (0, j))],
        out_specs=(pl.BlockSpec((tm, 512), lambda j, i: (i, j)), pl.BlockSpec((8, 512), lambda j, i: (0, j))),
        out_shape=(jax.ShapeDtypeStruct((T, 2 * ML_W), _F32), jax.ShapeDtypeStruct((8, 2 * ML_W), _F32)),
        compiler_params=_cparams(("arbitrary", "arbitrary")),
    )(rest, rest, rest, da_f, da_f, da_f, da_b, da_b, da_b, conv_w, conv_b)


def _scan_rows(x, suffix):
    L = x.shape[0]
    row = lax.broadcasted_iota(jnp.int32, (L, 1), 0)
    step = 1
    while step < L:
        if suffix:
            x = x + jnp.where(row < L - step, pltpu.roll(x, L - step, 0), 0.0)
        else:
            x = x + jnp.where(row >= step, pltpu.roll(x, step, 0), 0.0)
        step *= 2
    return x


def _ml_gates(gt, rev):
    L = gt.shape[0]
    ri = lax.broadcasted_iota(jnp.int32, (L, L), 0)
    ci = lax.broadcasted_iota(jnp.int32, (L, L), 1)
    mask = (ci >= ri) if rev else (ci <= ri)
    lf = jnp.minimum(gt, 0.0) - jnp.log(1.0 + jnp.exp(-jnp.abs(gt)))
    b = _scan_rows(lf, suffix=rev)
    return mask, b, b.T, gt.T


def _ml_head_gates(gt, gates, head, rev):
    _, b, b_t, gt_t = gates
    ci = (8 if rev else 0) + head
    cf = ci + ML_HEADS
    last = 0 if rev else gt.shape[0] - 1
    return dict(icol=gt[:, ci:ci + 1], b_col=b[:, cf:cf + 1], b_row=b_t[cf:cf + 1, :],
                i_row=gt_t[ci:ci + 1, :], bl=b[last:last + 1, cf:cf + 1])


def _ml_chunk(q, k, v, hg, mask, C, n, m):
    icol, b_col, b_row, bl = hg["icol"], hg["b_col"], hg["b_row"], hg["bl"]
    dlog = jnp.where(mask, b_col - b_row + hg["i_row"], NEG)
    m_t = jnp.maximum(b_col + m, jnp.max(dlog, axis=1, keepdims=True))
    dm = jnp.exp(dlog - m_t)
    ks = k * (ML_HEAD_DIM ** -0.5)
    qb, kb, vb = q.astype(_BF16), ks.astype(_BF16), v.astype(_BF16)
    s = _nt(qb, kb) * dm
    g = jnp.exp(b_col + m - m_t)
    qc = _nt(qb, C.astype(_BF16))
    num = _nn(s.astype(_BF16), vb) + g * qc
    qn = jnp.sum(q * n, axis=1, keepdims=True)
    den = jnp.sum(s, axis=1, keepdims=True) + g * qn
    e_m = jnp.exp(-m_t)
    nrm = jnp.maximum(jnp.abs(den), e_m)
    h = num / nrm
    a_col = bl - b_col + icol
    m_new = jnp.maximum(bl + m, jnp.max(a_col, axis=0, keepdims=True))
    decay = jnp.exp(bl + m - m_new)
    w = jnp.exp(a_col - m_new)
    c_new = decay * C + _tn((w * v).astype(_BF16), kb)
    n_new = decay * n + jnp.sum(w * ks, axis=0, keepdims=True)
    aux = dict(dm=dm, ks=ks, qb=qb, kb=kb, vb=vb, s=s, g=g, qc=qc, qn=qn,
               den=den, e_m=e_m, nrm=nrm, decay=decay, w=w)
    return h, c_new, n_new, m_new, aux


def _mlstm_fwd(qk_act, rest, T, rev):
    tb = ML_CB * ML_CHUNK
    nblk = T // tb
    nc = T // ML_CHUNK
    bi = (lambda i: nblk - 1 - i) if rev else (lambda i: i)

    def body(q_ref, k_ref, v_ref, g_ref, h_ref, cs_ref, ns_ref, ms_ref, c_scr, n_scr, m_scr):
        @pl.when(pl.program_id(0) == 0)
        def _():
            c_scr[...] = jnp.zeros_like(c_scr)
            n_scr[...] = jnp.zeros_like(n_scr)
            m_scr[...] = jnp.zeros_like(m_scr)

        def step(j, carry):
            c = (ML_CB - 1 - j) if rev else j
            r0 = pl.multiple_of(c * ML_CHUNK, ML_CHUNK)
            gt = g_ref[pl.ds(r0, ML_CHUNK), :]
            gates = _ml_gates(gt, rev)
            for hd in range(ML_HEADS):
                cols = slice(hd * ML_HEAD_DIM, (hd + 1) * ML_HEAD_DIM)
                C = c_scr[hd]
                n = n_scr[hd:hd + 1, :]
                mrow = m_scr[hd:hd + 1, :]
                cs_ref[c, hd] = C
                ns_ref[c, hd:hd + 1, :] = n
                ms_ref[c, hd:hd + 1, :] = mrow
                h, c_new, n_new, m_new, _ = _ml_chunk(
                    q_ref[pl.ds(r0, ML_CHUNK), cols], k_ref[pl.ds(r0, ML_CHUNK), cols],
                    v_ref[pl.ds(r0, ML_CHUNK), cols], _ml_head_gates(gt, gates, hd, rev), gates[0],
                    C, n, mrow[:, 0:1])
                h_ref[pl.ds(r0, ML_CHUNK), cols] = h
                c_scr[hd] = c_new
                n_scr[hd:hd + 1, :] = n_new
                m_scr[hd:hd + 1, :] = jnp.broadcast_to(m_new, (1, 128))
            return carry

        lax.fori_loop(0, ML_CB, step, 0)

    return pl.pallas_call(
        body, name="mlstm_fwd_rev" if rev else "mlstm_fwd", grid=(nblk,),
        in_specs=[pl.BlockSpec((tb, ML_W), lambda i: (bi(i), 0)),
                  pl.BlockSpec((tb, ML_W), lambda i: (bi(i), 1)),
                  pl.BlockSpec((tb, ML_W), lambda i: (bi(i), 3)),
                  pl.BlockSpec((tb, 128), lambda i: (bi(i), GATE_COL // 128))],
        out_specs=(pl.BlockSpec((tb, ML_W), lambda i: (bi(i), 0)),
                   pl.BlockSpec((ML_CB, ML_HEADS, 128, 128), lambda i: (bi(i), 0, 0, 0)),
                   pl.BlockSpec((ML_CB, ML_HEADS, 128), lambda i: (bi(i), 0, 0)),
                   pl.BlockSpec((ML_CB, ML_HEADS, 128), lambda i: (bi(i), 0, 0))),
        out_shape=(jax.ShapeDtypeStruct((T, ML_W), _F32),
                   jax.ShapeDtypeStruct((nc, ML_HEADS, 128, 128), _F32),
                   jax.ShapeDtypeStruct((nc, ML_HEADS, 128), _F32),
                   jax.ShapeDtypeStruct((nc, ML_HEADS, 128), _F32)),
        scratch_shapes=[pltpu.VMEM((ML_HEADS, 128, 128), _F32), pltpu.VMEM((8, 128), _F32),
                        pltpu.VMEM((8, 128), _F32)],
        compiler_params=_cparams(("arbitrary",)),
    )(qk_act, qk_act, rest, rest)


def _mlstm_bwd(qk_act, rest, d_h, cs, ns, ms, T, rev):
    tb = ML_CB * ML_CHUNK
    nblk = T // tb
    bi = (lambda i: i) if rev else (lambda i: nblk - 1 - i)

    def body(q_ref, k_ref, v_ref, g_ref, dh_ref, cs_ref, ns_ref, ms_ref,
             dqk_ref, dv_ref, dg_ref, dc_scr, dn_scr):
        @pl.when(pl.program_id(0) == 0)
        def _():
            dc_scr[...] = jnp.zeros_like(dc_scr)
            dn_scr[...] = jnp.zeros_like(dn_scr)

        def step(j, carry):
            c = j if rev else (ML_CB - 1 - j)
            r0 = pl.multiple_of(c * ML_CHUNK, ML_CHUNK)
            gt = g_ref[pl.ds(r0, ML_CHUNK), :]
            gates = _ml_gates(gt, rev)
            mask = gates[0]
            lane = lax.broadcasted_iota(jnp.int32, (1, 128), 1)
            sub = lax.broadcasted_iota(jnp.int32, (128, 1), 0)
            db_t = jnp.zeros((ML_CHUNK, 128), _F32)
            da_t = jnp.zeros((ML_CHUNK, 128), _F32)
            cs_rows = jnp.zeros((128, ML_CHUNK), _F32)
            dbl_t = jnp.zeros((1, 128), _F32)
            for hd in range(ML_HEADS):
                cols = slice(hd * ML_HEAD_DIM, (hd + 1) * ML_HEAD_DIM)
                ci = (8 if rev else 0) + hd
                cf = ci + ML_HEADS
                q = q_ref[pl.ds(r0, ML_CHUNK), cols]
                k = k_ref[pl.ds(r0, ML_CHUNK), cols]
                v = v_ref[pl.ds(r0, ML_CHUNK), cols]
                C = cs_ref[c, hd]
                n = ns_ref[c, hd:hd + 1, :]
                m = ms_ref[c, hd:hd + 1, :][:, 0:1]
                dcn = dc_scr[hd]
                dnn = dn_scr[hd:hd + 1, :]
                h, _, _, _, a = _ml_chunk(q, k, v, _ml_head_gates(gt, gates, hd, rev), mask, C, n, m)
                d_hv = dh_ref[pl.ds(r0, ML_CHUNK), cols]
                g, s, w, ks = a["g"], a["s"], a["w"], a["ks"]
                qb, kb, vb = a["qb"], a["kb"], a["vb"]
                dnum = d_hv / a["nrm"]
                hdot = jnp.sum(d_hv * h, axis=1, keepdims=True)
                dden = jnp.where(jnp.abs(a["den"]) >= a["e_m"], -hdot / a["nrm"] * jnp.sign(a["den"]), 0.0)
                dnb = dnum.astype(_BF16)
                d_s = _nt(dnb, vb) + dden
                r = d_s * s
                dsqk = (d_s * a["dm"]).astype(_BF16)
                cb = C.astype(_BF16)
                dq = _nn(dsqk, kb) + g * _nn(dnb, cb) + (dden * g) * n
                dk = _tn(dsqk, qb)
                dv = _tn(s.astype(_BF16), dnb)
                dg = jnp.sum(dnum * a["qc"], axis=1, keepdims=True) + dden * a["qn"]
                db_col = jnp.sum(r, axis=1, keepdims=True) + dg * g
                cs_rows = cs_rows + jnp.where((sub == ci) | (sub == cf), jnp.sum(r, axis=0, keepdims=True), 0.0)
                dc_chunk = _tn((g * dnum).astype(_BF16), qb)
                dn_chunk = jnp.sum((dden * g) * q, axis=0, keepdims=True)
                dcb = dcn.astype(_BF16)
                vdc = _nn(vb, dcb)
                kdc = _nt(kb, dcb)
                dw = jnp.sum(vdc * ks, axis=1, keepdims=True) + jnp.sum(ks * dnn, axis=1, keepdims=True)
                dv = dv + w * kdc
                dk = dk + w * vdc + w * dnn
                da = dw * w
                ddecay = (jnp.sum(jnp.sum(dcn * C, axis=1, keepdims=True), axis=0, keepdims=True)
                          + jnp.sum(dnn * n, axis=1, keepdims=True))
                dbl = ddecay * a["decay"] + jnp.sum(da, axis=0, keepdims=True)
                db_t = db_t + jnp.where(lane == cf, db_col - da, 0.0)
                da_t = da_t + jnp.where(lane == ci, da, 0.0)
                dbl_t = dbl_t + jnp.where(lane == cf, dbl, 0.0)
                dc_scr[hd] = dc_chunk + a["decay"] * dcn
                dn_scr[hd:hd + 1, :] = dn_chunk + a["decay"] * dnn
                dqk_ref[pl.ds(r0, ML_CHUNK), cols] = dq
                dqk_ref[pl.ds(r0, ML_CHUNK), slice(ML_W + hd * 128, ML_W + (hd + 1) * 128)] = dk * (ML_HEAD_DIM ** -0.5)
                dv_ref[pl.ds(r0, ML_CHUNK), cols] = dv
            lo = 8 if rev else 0
            is_i = (lane >= lo) & (lane < lo + ML_HEADS)
            is_f = (lane >= lo + ML_HEADS) & (lane < lo + 2 * ML_HEADS)
            cs_t = cs_rows.T
            db_all = db_t - jnp.where(is_f, cs_t, 0.0)
            dlf = _scan_rows(db_all, suffix=not rev) + dbl_t
            dg_ref[pl.ds(r0, ML_CHUNK), :] = (da_t + jnp.where(is_i, cs_t, 0.0)
                                               + jnp.where(is_f, dlf * _sigmoid(-gt), 0.0))
            return carry

        lax.fori_loop(0, ML_CB, step, 0)

    return pl.pallas_call(
        body, name="mlstm_bwd_rev" if rev else "mlstm_bwd", grid=(nblk,),
        in_specs=[pl.BlockSpec((tb, ML_W), lambda i: (bi(i), 0)),
                  pl.BlockSpec((tb, ML_W), lambda i: (bi(i), 1)),
                  pl.BlockSpec((tb, ML_W), lambda i: (bi(i), 3)),
                  pl.BlockSpec((tb, 128), lambda i: (bi(i), GATE_COL // 128)),
                  pl.BlockSpec((tb, ML_W), lambda i: (bi(i), 0)),
                  pl.BlockSpec((ML_CB, ML_HEADS, 128, 128), lambda i: (bi(i), 0, 0, 0)),
                  pl.BlockSpec((ML_CB, ML_HEADS, 128), lambda i: (bi(i), 0, 0)),
                  pl.BlockSpec((ML_CB, ML_HEADS, 128), lambda i: (bi(i), 0, 0))],
        out_specs=(pl.BlockSpec((tb, 2 * ML_W), lambda i: (bi(i), 0)),
                   pl.BlockSpec((tb, ML_W), lambda i: (bi(i), 0)),
                   pl.BlockSpec((tb, 128), lambda i: (bi(i), 0))),
        out_shape=(jax.ShapeDtypeStruct((T, 2 * ML_W), _F32), jax.ShapeDtypeStruct((T, ML_W), _F32),
                   jax.ShapeDtypeStruct((T, 128), _F32)),
        scratch_shapes=[pltpu.VMEM((ML_HEADS, 128, 128), _F32), pltpu.VMEM((8, 128), _F32)],
        compiler_params=_cparams(("arbitrary",)),
    )(qk_act, qk_act, rest, rest, d_h, cs, ns, ms)


def _post(x, target, o_na, rest, h_f, h_b, gate, ml_norm_w, final_w, w_out_bf, T):
    tm = 256
    n_i = T // tm

    def body(x_ref, t_ref, o_ref, zna_ref, hf_ref, hb_ref, mo_ref, mz_ref, gate_ref, mw_ref, fw_ref, w_ref,
             dx1_ref, do_ref, dzna_ref, dh_ref, dmo_ref, dmz_ref, dwo_ref, vec_ref):
        i = pl.program_id(0)
        gate_v = gate_ref[...]
        fw = fw_ref[...]
        zna = zna_ref[...]
        o = o_ref[...]
        na_out = o * _silu(zna)
        hsum = hf_ref[...] + hb_ref[...]
        sg = _sigmoid(mo_ref[...])
        hm = hsum * sg
        mz = mz_ref[...]
        smz = _silu(mz)
        hn_l, rstd_l, ml_l = [], [], []
        for hd in range(ML_HEADS):
            cols = slice(hd * 128, (hd + 1) * 128)
            hh = hm[:, cols]
            mu = jnp.mean(hh, axis=-1, keepdims=True)
            var = jnp.mean(jnp.square(hh - mu), axis=-1, keepdims=True)
            rstd = lax.rsqrt(var + EPS)
            hn = (hh - mu) * rstd
            hn_l.append(hn)
            rstd_l.append(rstd)
            ml_l.append(hn * mw_ref[:, cols] * smz[:, cols])
        mix = jnp.concatenate([na_out] + ml_l, axis=1).astype(_BF16)
        y = _nn(mix, w_ref[...])
        x1 = x_ref[...] + gate_v * y
        r = lax.rsqrt(jnp.mean(x1 * x1, axis=-1, keepdims=True) + EPS)
        xhat = x1 * r
        out = xhat * fw
        err = out - t_ref[...]
        loss = 0.5 * jnp.sum(jnp.sum(err * err, axis=1, keepdims=True), axis=0, keepdims=True) / D_MODEL
        dout = err * (1.0 / D_MODEL)
        dfw = jnp.sum(dout * xhat, axis=0, keepdims=True)
        dxhat = dout * fw
        dx1 = r * (dxhat - xhat * jnp.mean(dxhat * xhat, axis=-1, keepdims=True))
        dx1_ref[...] = dx1
        dgate = jnp.sum(dx1 * y, axis=0, keepdims=True)
        dy = (dx1 * gate_v).astype(_BF16)
        dmix = _nt(dy, w_ref[...])
        dwo = _tn(mix, dy)
        dna = dmix[:, :NA_W]
        do_ref[...] = (dna * _silu(zna)).astype(_BF16)
        dzna_ref[...] = dna * o * _dsilu(zna)
        dmw_l = []
        for hd in range(ML_HEADS):
            cols = slice(hd * 128, (hd + 1) * 128)
            dml = dmix[:, NA_W + hd * 128:NA_W + (hd + 1) * 128]
            hn = hn_l[hd]
            mwv = mw_ref[:, cols]
            dmz_ref[:, cols] = dml * hn * mwv * _dsilu(mz[:, cols])
            dhn = dml * mwv * smz[:, cols]
            dmw_l.append(jnp.sum(dml * hn * smz[:, cols], axis=0, keepdims=True))
            dhm = rstd_l[hd] * (dhn - jnp.mean(dhn, axis=-1, keepdims=True)
                                - hn * jnp.mean(dhn * hn, axis=-1, keepdims=True))
            sgc = sg[:, cols]
            dh_ref[:, cols] = dhm * sgc
            dmo_ref[:, cols] = dhm * hsum[:, cols] * sgc * (1.0 - sgc)
        dmw = jnp.concatenate(dmw_l + [jnp.zeros((1, D_MODEL - ML_W), _F32)], axis=1)
        lane = lax.broadcasted_iota(jnp.int32, (1, D_MODEL), 1)
        vec = jnp.concatenate([dfw, dgate, dmw, jnp.where(lane == 0, loss, 0.0),
                               jnp.zeros((4, D_MODEL), _F32)], axis=0)

        @pl.when(i == 0)
        def _():
            dwo_ref[...] = dwo
            vec_ref[...] = vec

        @pl.when(i > 0)
        def _():
            dwo_ref[...] += dwo
            vec_ref[...] += vec

    tok = lambda w, j: pl.BlockSpec((tm, w), lambda i: (i, j))
    tok3 = pl.BlockSpec((None, tm, D_MODEL), lambda i: (0, i, 0))
    row = lambda w: pl.BlockSpec((1, w), lambda i: (0, 0))
    f32 = lambda w: jax.ShapeDtypeStruct((T, w), _F32)
    return pl.pallas_call(
        body, name="post", grid=(n_i,),
        in_specs=[tok3, tok3, tok(NA_W, 0), tok(NA_W, 0), tok(ML_W, 0), tok(ML_W, 0),
                  tok(ML_W, 4), tok(ML_W, 5), row(D_MODEL), row(ML_W), row(D_MODEL),
                  pl.BlockSpec((D_MODEL, D_MODEL), lambda i: (0, 0))],
        out_specs=(tok(D_MODEL, 0), tok(NA_W, 0), tok(NA_W, 0), tok(ML_W, 0), tok(ML_W, 0), tok(ML_W, 0),
                   pl.BlockSpec((D_MODEL, D_MODEL), lambda i: (0, 0)),
                   pl.BlockSpec((8, D_MODEL), lambda i: (0, 0))),
        out_shape=(f32(D_MODEL), jax.ShapeDtypeStruct((T, NA_W), _BF16), f32(NA_W), f32(ML_W), f32(ML_W),
                   f32(ML_W), jax.ShapeDtypeStruct((D_MODEL, D_MODEL), _F32),
                   jax.ShapeDtypeStruct((8, D_MODEL), _F32)),
        compiler_params=_cparams(("arbitrary",)),
    )(x, target, o_na, rest, h_f, h_b, rest, rest, gate, ml_norm_w, final_w, w_out_bf)


def _section_specs(sections, tm):
    specs, args = [], []
    for _, width, parts in sections:
        for arr, cb in parts:
            specs.append(pl.BlockSpec((tm, width), functools.partial(lambda i, cb: (i, cb), cb=cb)))
            args.append(arr)
    return specs, args


def _section_values(sections, refs):
    vals, at = [], 0
    for _, _, parts in sections:
        v = refs[at][...]
        for r in refs[at + 1:at + len(parts)]:
            v = v + r[...]
        at += len(parts)
        vals.append(v.astype(_BF16))
    return vals


def _inproj_bwd_x(x, dx1, scale1p, norm_w, w_in_bf, sections, T):
    tm = 256
    sspecs, sargs = _section_specs(sections, tm)
    ns = len(sargs)

    def body(*refs):
        x_ref, dx1_ref, sc_ref, nw_ref, w_ref = refs[:5]
        srefs = refs[5:5 + ns]
        gx_ref, vec_ref = refs[5 + ns:]
        i = pl.program_id(0)
        vals = _section_values(sections, srefs)
        dh = jnp.zeros((tm, D_MODEL), _F32)
        for (c0, width, _), val in zip(sections, vals):
            dh = dh + _nt(val, w_ref[:, c0:c0 + width])
        xv = x_ref[...]
        r = lax.rsqrt(jnp.mean(xv * xv, axis=-1, keepdims=True) + EPS)
        xhat = xv * r
        nw = nw_ref[...]
        dshift = jnp.sum(dh, axis=0, keepdims=True)
        dscale = jnp.sum(dh * xhat * nw, axis=0, keepdims=True)
        dhpre = dh * sc_ref[...]
        dnw = jnp.sum(dhpre * xhat, axis=0, keepdims=True)
        dxhat = dhpre * nw
        gx_ref[...] = dx1_ref[...] + r * (dxhat - xhat * jnp.mean(dxhat * xhat, axis=-1, keepdims=True))
        vec = jnp.concatenate([dshift, dscale, dnw, jnp.zeros((5, D_MODEL), _F32)], axis=0)

        @pl.when(i == 0)
        def _():
            vec_ref[...] = vec

        @pl.when(i > 0)
        def _():
            vec_ref[...] += vec

    row = pl.BlockSpec((1, D_MODEL), lambda i: (0, 0))
    tok = pl.BlockSpec((tm, D_MODEL), lambda i: (i, 0))
    tok3 = pl.BlockSpec((None, tm, D_MODEL), lambda i: (0, i, 0))
    return pl.pallas_call(
        body, name="inproj_bwd_x", grid=(T // tm,),
        in_specs=[tok3, tok, row, row, pl.BlockSpec((D_MODEL, IN_PAD), lambda i: (0, 0))] + sspecs,
        out_specs=(tok3, pl.BlockSpec((8, D_MODEL), lambda i: (0, 0))),
        out_shape=(jax.ShapeDtypeStruct((1, T, D_MODEL), _F32), jax.ShapeDtypeStruct((8, D_MODEL), _F32)),
        compiler_params=_cparams(("arbitrary",)),
    )(x, dx1, scale1p, norm_w, w_in_bf, *sargs)


def _inproj_bwd_w(h_bf, section, T, name):
    tm = 512
    width = section[1]
    sspecs, sargs = _section_specs([section], tm)
    ns = len(sargs)

    def body(*refs):
        h_ref = refs[0]
        srefs = refs[1:1 + ns]
        dw_ref, db_ref = refs[1 + ns:]
        i = pl.program_id(0)
        v = srefs[0][...]
        for r in srefs[1:]:
            v = v + r[...]
        dw = _nn(h_ref[...], v.astype(_BF16))
        db = jnp.concatenate([jnp.sum(v, axis=0, keepdims=True), jnp.zeros((7, width), _F32)], axis=0)

        @pl.when(i == 0)
        def _():
            dw_ref[...] = dw
            db_ref[...] = db

        @pl.when(i > 0)
        def _():
            dw_ref[...] += dw
            db_ref[...] += db

    return pl.pallas_call(
        body, name=name, grid=(T // tm,),
        in_specs=[pl.BlockSpec((D_MODEL, tm), lambda i: (0, i))] + sspecs,
        out_specs=(pl.BlockSpec((D_MODEL, width), lambda i: (0, 0)), pl.BlockSpec((8, width), lambda i: (0, 0))),
        out_shape=(jax.ShapeDtypeStruct((D_MODEL, width), _F32), jax.ShapeDtypeStruct((8, width), _F32)),
        compiler_params=_cparams(("arbitrary",)),
    )(h_bf, *sargs)


def _adamw_math(w, g, m, v):
    m = ADAM_B1 * m + (1.0 - ADAM_B1) * g
    v = ADAM_B2 * v + (1.0 - ADAM_B2) * jnp.square(g)
    m_hat = m / (1.0 - ADAM_B1 ** ADAM_STEP)
    v_hat = v / (1.0 - ADAM_B2 ** ADAM_STEP)
    delta = -ADAM_LR * (m_hat / (jnp.sqrt(v_hat) + ADAM_EPS) + ADAM_WD * w)
    return delta, m, v


def _adamw_slots(w, m, v, slots, tr, name):
    R, C = w.shape

    def body(w_ref, m_ref, v_ref, s_ref, g_ref, d_ref, nm_ref, nv_ref):
        g = s_ref[0].astype(_F32)
        for k in range(1, N_DEV):
            g = g + s_ref[k].astype(_F32)
        g_ref[...] = g
        d_ref[...], nm_ref[...], nv_ref[...] = _adamw_math(w_ref[...], g, m_ref[...], v_ref[...])

    blk = pl.BlockSpec((tr, C), lambda i: (i, 0))
    return pl.pallas_call(
        body, name=name, grid=(R // tr,),
        in_specs=[blk, blk, blk, pl.BlockSpec((N_DEV, tr, C), lambda i: (0, i, 0))],
        out_specs=(blk, blk, blk, blk),
        out_shape=tuple(jax.ShapeDtypeStruct((R, C), _F32) for _ in range(4)),
        compiler_params=_cparams(("arbitrary",)),
    )(w, m, v, slots)


def _w_ada_update(c_all, dmod_my, w, m, v):
    def body(c_ref, d_ref, w_ref, m_ref, v_ref, g_ref, dl_ref, nm_ref, nv_ref):
        g = lax.dot_general(_silu(c_ref[...]), d_ref[...], (((0,), (0,)), ((), ())),
                            precision=_HI, preferred_element_type=_F32)
        g_ref[...] = g
        dl_ref[...], nm_ref[...], nv_ref[...] = _adamw_math(w_ref[...], g, m_ref[...], v_ref[...])

    return pl.pallas_call(
        body, name="w_ada_update",
        out_shape=tuple(jax.ShapeDtypeStruct(w.shape, _F32) for _ in range(4)),
        compiler_params=_cparams(),
    )(c_all, dmod_my, w, m, v)


_PACK = (("b_ada", 3072, 3072), ("norm_w", 1024, 1024), ("b_in", IN_W, IN_PAD), ("conv_w", 5120, 5120),
         ("conv_b", 1024, 1024), ("rpb", 3720, 3840), ("ml_norm_w", 512, 512), ("final_norm_w", 1024, 1024),
         ("loss", 1, 128))
_PACK_OFF = {}
_off = 0
for _name, _len, _pad in _PACK:
    _PACK_OFF[_name] = (_off, _len)
    _off += _pad
_PACK_LEN = _off


def _pack(parts):
    cols = []
    for name, length, pad in _PACK:
        vec = parts[name].reshape(-1).astype(_F32)
        cols.append(jnp.pad(vec, (0, pad - length)))
    return jnp.concatenate(cols).reshape(1, _PACK_LEN)


def _unpack(vec, name, shape):
    off, length = _PACK_OFF[name]
    return vec.reshape(-1)[off:off + length].reshape(shape)


def kernel(x, c, w_ada, b_ada, norm_w, w_in, b_in, conv_w, conv_b, rpb, ml_norm_w, w_out, final_norm_w, loss_target, m_w_ada, m_b_ada, m_norm_w, m_w_in, m_b_in, m_conv_w, m_conv_b, m_rpb, m_ml_norm_w, m_w_out, m_final_norm_w, v_w_ada, v_b_ada, v_norm_w, v_w_in, v_b_in, v_conv_w, v_conv_b, v_rpb, v_ml_norm_w, v_w_out, v_final_norm_w):
    T = x.shape[1]
    rows = T // GRID_W
    me = 4 * lax.axis_index("x") + 2 * lax.axis_index("y") + lax.axis_index("c")
    n_in = w_in.shape[2]
    n_ada = w_ada.shape[2]
    n_cw = conv_w.shape[2]
    n_wo = w_out.shape[1]

    g_w_in, g_w_out, g_conv_w, g_c = _exchange(
        [w_in[0].astype(_BF16), w_out[0].astype(_BF16), conv_w[0], c], [False] * 4, "gather_weights")
    w_in_full = g_w_in.transpose(1, 0, 2).reshape(D_MODEL, N_DEV * n_in)
    w_in_bf = jnp.pad(w_in_full, ((0, 0), (0, IN_PAD - IN_W)))
    b_in_pad = jnp.pad(b_in, ((0, 0), (0, IN_PAD - IN_W)))
    w_out_bf = g_w_out.reshape(N_DEV * n_wo, D_MODEL)
    conv_w_full = jnp.pad(g_conv_w.transpose(1, 0, 2).reshape(CONV_W, N_DEV * n_cw), ((0, 3), (0, 0)))
    c_all = g_c.reshape(N_DEV, D_MODEL)

    b_ada_my = lax.dynamic_slice(b_ada, (0, me * n_ada), (1, n_ada))
    (mod_slots,) = _exchange([_mod_part(c_all, w_ada[0], b_ada_my)], [False], "gather_mod")
    mod = lax.dynamic_index_in_dim(mod_slots, me, axis=1, keepdims=False).reshape(1, 3 * D_MODEL)
    shift, scale, gate = mod[:, :D_MODEL], mod[:, D_MODEL:2 * D_MODEL], mod[:, 2 * D_MODEL:]
    scale1p = 1.0 + scale

    qkv, rest, h_bf = _inproj_fwd(x, scale1p, shift, norm_w, w_in_bf, b_in_pad)
    bias, bias_t = _na_bias_tables(rpb[0], rows)
    o_na, lse, lse_rows = _na_fwd(qkv, bias, T)
    qk_act = _conv_fwd(rest, conv_w_full, conv_b, T)
    h_f, cs_f, ns_f, ms_f = _mlstm_fwd(qk_act, rest, T, False)
    h_b, cs_b, ns_b, ms_b = _mlstm_fwd(qk_act, rest, T, True)

    dx1, d_o, dz_na, d_h, d_mo, d_mz, dwo, pvec = _post(
        x, loss_target, o_na, rest, h_f, h_b, gate, ml_norm_w, final_norm_w.reshape(1, D_MODEL), w_out_bf, T)

    dq_na, delta_rows, dbias = _na_bwd_q(qkv, bias, o_na, d_o, lse, T)
    dk_na, dv_na = _na_bwd_kv(qkv, bias_t, d_o, lse_rows, delta_rows, T)
    d_rpb = _rpb_grad(dbias, rows)
    dqk_f, dv_f, dg_f = _mlstm_bwd(qk_act, rest, d_h, cs_f, ns_f, ms_f, T, False)
    dqk_b, dv_b, dg_b = _mlstm_bwd(qk_act, rest, d_h, cs_b, ns_b, ms_b, T, True)
    d_u, dconv = _conv_bwd(rest, conv_w_full, conv_b, dqk_f, dqk_b, T)

    sections = [(0, 512, [(dq_na, 0)]), (512, 512, [(dk_na, 0)]), (1024, 512, [(dv_na, 0)]),
                (1536, 512, [(dz_na, 0)]), (2048, 512, [(d_u, 0)]), (2560, 512, [(d_u, 1)]),
                (3072, 512, [(dv_f, 0), (dv_b, 0)]), (3584, 512, [(d_mo, 0)]), (4096, 512, [(d_mz, 0)]),
                (4608, 128, [(dg_f, 0), (dg_b, 0)])]
    grad_x, xvec = _inproj_bwd_x(x, dx1, scale1p, norm_w, w_in_bf, sections, T)
    dws, dbs = [], []
    for si, sec in enumerate(sections):
        dw_s, db_s = _inproj_bwd_w(h_bf, sec, T, "inproj_bwd_w%d" % si)
        dws.append(dw_s)
        dbs.append(db_s[0])
    dw_in = jnp.concatenate(dws, axis=1)[:, :IN_W]
    db_in = jnp.concatenate(dbs)[:IN_W]

    small = _pack({
        "b_ada": jnp.concatenate([xvec[0], xvec[1], pvec[1]]),
        "norm_w": xvec[2], "b_in": db_in, "conv_w": dconv[:CONV_W], "conv_b": dconv[CONV_W],
        "rpb": d_rpb, "ml_norm_w": pvec[2, :ML_W], "final_norm_w": pvec[0], "loss": pvec[3, :1]})
    s_small, s_w_in, s_w_out = _exchange(
        [small, dw_in.astype(_BF16).reshape(D_MODEL, N_DEV, n_in).transpose(1, 0, 2),
         dwo.astype(_BF16).reshape(N_DEV, n_wo, D_MODEL)],
        [False, True, True], "exchange_grads")

    g_w_in_s, d_w_in, nm_w_in, nv_w_in = _adamw_slots(w_in[0], m_w_in[0], v_w_in[0], s_w_in, 128, "adamw_w_in")
    g_w_out_s, d_w_out, nm_w_out, nv_w_out = _adamw_slots(w_out[0], m_w_out[0], v_w_out[0], s_w_out, n_wo,
                                                          "adamw_w_out")
    dmod_all = s_small[:, 0, :3 * D_MODEL]
    dmod_my = lax.dynamic_slice(dmod_all, (0, me * n_ada), (N_DEV, n_ada))
    g_w_ada, d_w_ada, nm_w_ada, nv_w_ada = _w_ada_update(c_all, dmod_my, w_ada[0], m_w_ada[0], v_w_ada[0])

    def embed(shard):
        return lax.dynamic_update_slice(jnp.zeros((CONV_W, N_DEV * n_cw), _F32), shard[0], (0, me * n_cw))

    zero1 = jnp.zeros((1,), _F32)
    packed = lambda b_a, n_w, b_i, c_w, c_b, rp, mn, fn: _pack({
        "b_ada": b_a, "norm_w": n_w, "b_in": b_i, "conv_w": embed(c_w), "conv_b": c_b, "rpb": rp,
        "ml_norm_w": mn, "final_norm_w": fn, "loss": zero1})
    pw = packed(b_ada, norm_w, b_in, conv_w, conv_b, rpb, ml_norm_w, final_norm_w)
    pm = packed(m_b_ada, m_norm_w, m_b_in, m_conv_w, m_conv_b, m_rpb, m_ml_norm_w, m_final_norm_w)
    pv = packed(v_b_ada, v_norm_w, v_b_in, v_conv_w, v_conv_b, v_rpb, v_ml_norm_w, v_final_norm_w)
    sg, sd, sm, sv = _adamw_slots(pw, pm, pv, s_small, 1, "adamw_small")

    def small_outs(vec):
        cw = lax.dynamic_slice(_unpack(vec, "conv_w", (CONV_W, N_DEV * n_cw)), (0, me * n_cw), (CONV_W, n_cw))
        return dict(b_ada=_unpack(vec, "b_ada", b_ada.shape), norm_w=_unpack(vec, "norm_w", norm_w.shape),
                    b_in=_unpack(vec, "b_in", b_in.shape), conv_w=cw[None],
                    conv_b=_unpack(vec, "conv_b", conv_b.shape), rpb=_unpack(vec, "rpb", rpb.shape),
                    ml_norm_w=_unpack(vec, "ml_norm_w", ml_norm_w.shape),
                    final_norm_w=_unpack(vec, "final_norm_w", final_norm_w.shape))

    loss = _unpack(sg, "loss", ())
    order = ("w_ada", "b_ada", "norm_w", "w_in", "b_in", "conv_w", "conv_b", "rpb", "ml_norm_w", "w_out",
             "final_norm_w")
    outs = []
    for vec, big in ((sg, (g_w_ada, g_w_in_s, g_w_out_s)), (sd, (d_w_ada, d_w_in, d_w_out)),
                     (sm, (nm_w_ada, nm_w_in, nm_w_out)), (sv, (nv_w_ada, nv_w_in, nv_w_out))):
        group = small_outs(vec)
        group.update(w_ada=big[0][None], w_in=big[1][None], w_out=big[2][None])
        outs.extend(group[name] for name in order)
    return (loss, grad_x, *outs)
```

```python
import functools

import numpy as np
import jax
import jax.numpy as jnp
from jax import lax
from jax.experimental import pallas as pl
from jax.experimental.pallas import tpu as pltpu

N_DEV = 8
D_MODEL = 1024
GRID_W = 64
NA_HEADS = 8
NA_HEAD_DIM = 64
NA_KH = 8
NA_KW = 16
NA_W = 512
ML_HEADS = 4
ML_HEAD_DIM = 128
ML_W = 512
ML_CHUNK = 512
CONV_W = 5
EPS = 1e-6
IN_W = 4624
IN_PAD = 4736
REST_W = IN_PAD - 3 * NA_W
GATE_COL = 3072
NEG = -1e30
NA_RB = 8
NA_WIN = 16
ML_CB = 1
ADAM_LR = 0.001
ADAM_B1 = 0.9
ADAM_B2 = 0.999
ADAM_EPS = 1e-08
ADAM_WD = 0.01
ADAM_STEP = 10
VMEM_LIMIT = 56 * 1024 * 1024

_F32 = jnp.float32
_BF16 = jnp.bfloat16
_HI = lax.Precision.HIGHEST


def _cparams(sem=None):
    return pltpu.CompilerParams(dimension_semantics=sem, vmem_limit_bytes=VMEM_LIMIT)


def _nt(a, b):
    return lax.dot_general(a, b, (((1,), (1,)), ((), ())), preferred_element_type=_F32)


def _tn(a, b):
    return lax.dot_general(a, b, (((0,), (0,)), ((), ())), preferred_element_type=_F32)


def _nn(a, b):
    return jnp.dot(a, b, preferred_element_type=_F32)


def _sigmoid(x):
    return 1.0 / (1.0 + jnp.exp(-x))


def _silu(x):
    return x * _sigmoid(x)


def _dsilu(x):
    s = _sigmoid(x)
    return s * (1.0 + x * (1.0 - s))


def _exchange(arrs, scatter, name):
    n = len(arrs)
    out_shape = []
    for a, sc in zip(arrs, scatter):
        blk = a.shape[1:] if sc else a.shape
        out_shape.append(jax.ShapeDtypeStruct((N_DEV,) + tuple(blk), a.dtype))

    def body(*refs):
        ins = refs[:n]
        outs = refs[n:2 * n]
        send_sems, recv_sems, local_sems = refs[2 * n:]
        x, y, c = lax.axis_index("x"), lax.axis_index("y"), lax.axis_index("c")
        me = 4 * x + 2 * y + c
        local, sends, recvs = [], [], []
        for a in range(n):
            own = ins[a].at[me] if scatter[a] else ins[a]
            cp = pltpu.make_async_copy(own, outs[a].at[me], local_sems.at[a])
            cp.start()
            local.append(cp)
            for k in range(1, N_DEV):
                px = 1 - x if k & 4 else x
                py = 1 - y if k & 2 else y
                pc = 1 - c if k & 1 else c
                p = 4 * px + 2 * py + pc
                src = ins[a].at[p] if scatter[a] else ins[a]
                snd = pltpu.make_async_remote_copy(
                    src_ref=src, dst_ref=outs[a].at[me],
                    send_sem=send_sems.at[a, k - 1], recv_sem=recv_sems.at[a, k - 1],
                    device_id=(px, py, pc), device_id_type=pl.DeviceIdType.MESH)
                snd.start()
                sends.append(snd)
                rcv = pltpu.make_async_remote_copy(
                    src_ref=src, dst_ref=outs[a].at[p],
                    send_sem=send_sems.at[a, k - 1], recv_sem=recv_sems.at[a, k - 1],
                    device_id=(px, py, pc), device_id_type=pl.DeviceIdType.MESH)
                recvs.append(rcv)
        for rcv in recvs:
            rcv.wait_recv()
        for snd in sends:
            snd.wait_send()
        for cp in local:
            cp.wait()

    any_spec = pl.BlockSpec(memory_space=pl.ANY)
    res = pl.pallas_call(
        body, name=name, out_shape=tuple(out_shape),
        in_specs=[any_spec] * n, out_specs=tuple([any_spec] * n),
        scratch_shapes=[pltpu.SemaphoreType.DMA((n, N_DEV - 1)),
                        pltpu.SemaphoreType.DMA((n, N_DEV - 1)),
                        pltpu.SemaphoreType.DMA((n,))],
    )(*arrs)
    return list(res)


def _mod_part(c_all, w_ada, b_my):
    def body(c_ref, w_ref, b_ref, o_ref):
        o_ref[...] = jnp.dot(_silu(c_ref[...]), w_ref[...], precision=_HI,
                             preferred_element_type=_F32) + b_ref[...]

    return pl.pallas_call(
        body, name="mod_part",
        out_shape=jax.ShapeDtypeStruct((N_DEV, w_ada.shape[1]), _F32),
        compiler_params=_cparams(),
    )(c_all, w_ada, b_my)


def _inproj_fwd(x, scale1p, shift, norm_w, w_in_bf, b_in_pad):
    T = x.shape[1]
    tm = 256
    n_q = 3 * NA_W

    def body(x_ref, sc_ref, sh_ref, nw_ref, w_ref, b_ref, qkv_ref, rest_ref, h_ref):
        xv = x_ref[...]
        r = lax.rsqrt(jnp.mean(xv * xv, axis=-1, keepdims=True) + EPS)
        h = xv * r * nw_ref[...] * sc_ref[...] + sh_ref[...]
        hb = h.astype(_BF16)
        h_ref[...] = h.T.astype(_BF16)
        for n0 in range(0, IN_PAD, 512):
            wd = min(512, IN_PAD - n0)
            acc = _nn(hb, w_ref[:, n0:n0 + wd]) + b_ref[:, n0:n0 + wd]
            if n0 == 0:
                acc = acc * (NA_HEAD_DIM ** -0.5)
            if n0 < n_q:
                qkv_ref[:, n0:n0 + wd] = acc.astype(_BF16)
            else:
                rest_ref[:, n0 - n_q:n0 - n_q + wd] = acc

    row = lambda w: pl.BlockSpec((1, w), lambda i: (0, 0))
    return pl.pallas_call(
        body, name="inproj_fwd", grid=(T // tm,),
        in_specs=[pl.BlockSpec((None, tm, D_MODEL), lambda i: (0, i, 0)), row(D_MODEL), row(D_MODEL), row(D_MODEL),
                  pl.BlockSpec((D_MODEL, IN_PAD), lambda i: (0, 0)), row(IN_PAD)],
        out_specs=(pl.BlockSpec((tm, n_q), lambda i: (i, 0)),
                   pl.BlockSpec((tm, REST_W), lambda i: (i, 0)),
                   pl.BlockSpec((D_MODEL, tm), lambda i: (0, i))),
        out_shape=(jax.ShapeDtypeStruct((T, n_q), _BF16),
                   jax.ShapeDtypeStruct((T, REST_W), _F32),
                   jax.ShapeDtypeStruct((D_MODEL, T), _BF16)),
        compiler_params=_cparams(("arbitrary",)),
    )(x, scale1p, shift, norm_w, w_in_bf, b_in_pad)


def _na_class_rows(rows):
    nb = rows // NA_RB
    out = []
    for rb in (0, min(1, nb - 1), nb - 1):
        ws = int(np.clip(NA_RB * rb - 4, 0, rows - NA_WIN))
        out.append((NA_RB * rb + np.arange(NA_RB), ws + np.arange(NA_WIN)))
    return out


def _na_pair_index(rows, qrows, krows):
    start = lambda r: np.clip(r - NA_KH // 2, 0, rows - NA_KH)
    col = np.arange(GRID_W)
    cstart = np.clip(col - NA_KW // 2, 0, GRID_W - NA_KW)
    dy = krows[None, :] - qrows[:, None] + NA_KH - 1
    vr = (krows[None, :] >= start(qrows)[:, None]) & (krows[None, :] < start(qrows)[:, None] + NA_KH)
    dx = np.clip(col[None, :] - col[:, None], -(NA_KW - 1), NA_KW - 1) + NA_KW - 1
    vc = (col[None, :] >= cstart[:, None]) & (col[None, :] < cstart[:, None] + NA_KW)
    nq, nk = len(qrows), len(krows)
    dy4 = np.broadcast_to(np.clip(dy, 0, 2 * NA_KH - 2)[:, None, :, None], (nq, GRID_W, nk, GRID_W))
    dx4 = np.broadcast_to(dx[None, :, None, :], (nq, GRID_W, nk, GRID_W))
    valid = vr[:, None, :, None] & vc[None, :, None, :]
    idx = (dy4 * (2 * NA_KW - 1) + dx4).reshape(nq * GRID_W, nk * GRID_W)
    return idx.astype(np.int32), valid.reshape(nq * GRID_W, nk * GRID_W), (dy, vr, dx, vc)


def _na_half_slabs(rpb):
    _, _, (_, _, dx, vc) = _na_pair_index(NA_WIN, np.arange(1), np.arange(1))
    xs, ys = np.meshgrid(np.arange(GRID_W), np.arange(GRID_W), indexing="ij")
    consts = []
    for trans in (False, True):
        qc, kc = (ys, xs) if trans else (xs, ys)
        for right in (False, True):
            pos = (xs * 128 + (GRID_W if right else 0) + ys).reshape(-1)
            oh = np.zeros((32, GRID_W * 128), np.float32)
            oh[dx[qc, kc].reshape(-1), pos] = 1.0
            col_neg = np.zeros((1, GRID_W * 128), np.float32)
            col_neg[0, pos] = np.where(vc[qc, kc].reshape(-1), 0.0, NEG)
            half = np.zeros((1, GRID_W * 128), np.float32)
            half[0, pos] = 1.0
            consts += [jnp.asarray(oh), jnp.asarray(col_neg), jnp.asarray(half)]
    row_neg = np.where(np.arange(NA_HEADS * 16) % 16 == 15, NEG, 0.0).astype(np.float32).reshape(-1, 1)
    rp = jnp.pad(rpb, ((0, 0), (0, 1), (0, 1))).reshape(NA_HEADS * 16, 32)

    def body(*refs):
        r_ref, rn_ref = refs[0], refs[1]
        for t in range(4):
            oh_ref, cn_ref, half_ref = refs[2 + 3 * t:5 + 3 * t]
            refs[14 + t][...] = (jnp.dot(r_ref[...], oh_ref[...], precision=_HI, preferred_element_type=_F32)
                                 + cn_ref[...] + rn_ref[...] * half_ref[...])

    outs = pl.pallas_call(
        body, name="na_half_slabs",
        out_shape=tuple(jax.ShapeDtypeStruct((NA_HEADS * 16, GRID_W * 128), _F32) for _ in range(4)),
        compiler_params=_cparams(),
    )(rp, jnp.asarray(row_neg), *consts)
    return [o.reshape(NA_HEADS, 16, GRID_W, 128) for o in outs]


def _na_bias_tables(rpb, rows):
    left, right, left_t, right_t = _na_half_slabs(rpb)
    didx, didx_t = [], []
    for blk, win in _na_class_rows(rows):
        _, _, (dy, vr, _, _) = _na_pair_index(rows, blk, win)
        didx.append(np.where(vr, dy, 15))
        _, _, (dy, vr, _, _) = _na_pair_index(rows, win, blk)
        didx_t.append(np.where(vr, dy, 15).T)

    def body(l_ref, r_ref, lt_ref, rt_ref, b_ref, bt_ref):
        for ci in range(3):
            for a in range(NA_RB):
                for j in range(NA_WIN // 2):
                    rs = slice(a * GRID_W, (a + 1) * GRID_W)
                    cs = slice(j * 128, (j + 1) * 128)
                    b_ref[0, ci, rs, cs] = (l_ref[0, int(didx[ci][a, 2 * j])]
                                            + r_ref[0, int(didx[ci][a, 2 * j + 1])])
                    bt_ref[0, ci, rs, cs] = (lt_ref[0, int(didx_t[ci][a, 2 * j])]
                                             + rt_ref[0, int(didx_t[ci][a, 2 * j + 1])])

    slab = pl.BlockSpec((1, 16, GRID_W, 128), lambda h: (h, 0, 0, 0))
    tab = pl.BlockSpec((1, 3, NA_RB * GRID_W, NA_WIN * GRID_W), lambda h: (h, 0, 0, 0))
    shape = jax.ShapeDtypeStruct((NA_HEADS, 3, NA_RB * GRID_W, NA_WIN * GRID_W), _F32)
    return pl.pallas_call(
        body, name="na_tables", grid=(NA_HEADS,),
        in_specs=[slab] * 4, out_specs=(tab, tab), out_shape=(shape, shape),
        compiler_params=_cparams(("arbitrary",)),
    )(left, right, left_t, right_t)


def _na_cls(i, nb):
    return jnp.where(i == 0, 0, jnp.where(i == nb - 1, 2, 1))


def _pair_rows_shape(T):
    return jax.ShapeDtypeStruct((NA_HEADS // 2, T // 256, 2, 256), _F32)


def _pair_rows_spec(tq):
    return pl.BlockSpec((1, tq // 256, 2, 256), lambda hp, rb: (hp, rb, 0, 0))


def _store_pair_rows(ref, col0, col1):
    tq = col0.shape[0]
    lane = lax.broadcasted_iota(jnp.int32, (1, 128), 1)
    tile = jnp.where(lane == 0, col0, jnp.where(lane == 1, col1, 0.0))
    rows = tile.T
    for j in range(tq // 256):
        ref[0, j] = rows[0:2, j * 256:(j + 1) * 256]


def _na_fwd(qkv, bias, T):
    rows = T // GRID_W
    nb = rows // NA_RB
    tq = NA_RB * GRID_W
    tw = NA_WIN * GRID_W

    def body(q_ref, k_ref, v_ref, b_ref, o_ref, l_ref, lr_ref):
        rb = pl.program_id(1)
        ws = pl.multiple_of(jnp.clip(NA_RB * rb - 4, 0, rows - NA_WIN) * GRID_W, 256)
        kw = k_ref[pl.ds(ws, tw), :]
        vw = v_ref[pl.ds(ws, tw), :]
        q = q_ref[...]
        lane = lax.broadcasted_iota(jnp.int32, (1, 128), 1)
        outs, lses = [], []
        for hh in range(2):
            msk = (lane < NA_HEAD_DIM) if hh == 0 else (lane >= NA_HEAD_DIM)
            s = _nt(jnp.where(msk, q, jnp.zeros_like(q)), kw) + b_ref[hh, 0]
            m = jnp.max(s, axis=1, keepdims=True)
            p = jnp.exp(s - m)
            l = jnp.sum(p, axis=1, keepdims=True)
            outs.append(_nn(p.astype(_BF16), vw) / l)
            lses.append(m + jnp.log(l))
        first = lane < NA_HEAD_DIM
        o_ref[...] = jnp.where(first, outs[0], outs[1])
        l_ref[...] = jnp.where(first, lses[0], lses[1])
        _store_pair_rows(lr_ref, lses[0], lses[1])

    blk = lambda off: pl.BlockSpec((tq, 128), lambda hp, rb: (rb, off + hp))
    whole = lambda off: pl.BlockSpec((T, 128), lambda hp, rb: (0, off + hp))
    return pl.pallas_call(
        body, name="na_fwd", grid=(NA_HEADS // 2, nb),
        in_specs=[blk(0), whole(4), whole(8),
                  pl.BlockSpec((2, 1, tq, tw), lambda hp, rb: (hp, _na_cls(rb, nb), 0, 0))],
        out_specs=(blk(0), blk(0), _pair_rows_spec(tq)),
        out_shape=(jax.ShapeDtypeStruct((T, NA_W), _F32), jax.ShapeDtypeStruct((T, NA_W), _F32),
                   _pair_rows_shape(T)),
        compiler_params=_cparams(("arbitrary", "arbitrary")),
    )(qkv, qkv, qkv, bias)


def _na_bwd_q(qkv, bias, o, d_o, lse, T):
    rows = T // GRID_W
    nb = rows // NA_RB
    tq = NA_RB * GRID_W
    tw = NA_WIN * GRID_W

    def body(q_ref, k_ref, v_ref, b_ref, o_ref, do_ref, l_ref, dq_ref, dl_ref, db_ref):
        rb = pl.program_id(1)
        ws = pl.multiple_of(jnp.clip(NA_RB * rb - 4, 0, rows - NA_WIN) * GRID_W, 256)
        kw = k_ref[pl.ds(ws, tw), :]
        vw = v_ref[pl.ds(ws, tw), :]
        q = q_ref[...]
        d_ov = do_ref[...]
        prod = d_ov.astype(_F32) * o_ref[...]
        lse_v = l_ref[...]
        lane = lax.broadcasted_iota(jnp.int32, (1, 128), 1)
        first_visit = (rb == 0) | (rb == 1) | (rb == nb - 1)
        dqs, dls = [], []
        for hh in range(2):
            msk = (lane < NA_HEAD_DIM) if hh == 0 else (lane >= NA_HEAD_DIM)
            c0 = hh * NA_HEAD_DIM
            s = _nt(jnp.where(msk, q, jnp.zeros_like(q)), kw) + b_ref[hh, 0]
            p = jnp.exp(s - lse_v[:, c0:c0 + 1])
            dp = _nt(jnp.where(msk, d_ov, jnp.zeros_like(d_ov)), vw)
            delta = jnp.sum(jnp.where(msk, prod, 0.0), axis=1, keepdims=True)
            ds = p * (dp - delta)

            @pl.when(first_visit)
            def _():
                db_ref[hh, 0] = ds

            @pl.when(jnp.logical_not(first_visit))
            def _():
                db_ref[hh, 0] += ds

            dqs.append(_nn(ds.astype(_BF16), kw) * (NA_HEAD_DIM ** -0.5))
            dls.append(delta)
        first = lane < NA_HEAD_DIM
        dq_ref[...] = jnp.where(first, dqs[0], dqs[1]).astype(_BF16)
        _store_pair_rows(dl_ref, dls[0], dls[1])

    blk = lambda off: pl.BlockSpec((tq, 128), lambda hp, rb: (rb, off + hp))
    whole = lambda off: pl.BlockSpec((T, 128), lambda hp, rb: (0, off + hp))
    tab = pl.BlockSpec((2, 1, tq, tw), lambda hp, rb: (hp, _na_cls(rb, nb), 0, 0))
    return pl.pallas_call(
        body, name="na_bwd_q", grid=(NA_HEADS // 2, nb),
        in_specs=[blk(0), whole(4), whole(8), tab, blk(0), blk(0), blk(0)],
        out_specs=(blk(0), _pair_rows_spec(tq), tab),
        out_shape=(jax.ShapeDtypeStruct((T, NA_W), _BF16), _pair_rows_shape(T),
                   jax.ShapeDtypeStruct(bias.shape, _F32)),
        compiler_params=_cparams(("arbitrary", "arbitrary")),
    )(qkv, qkv, qkv, bias, o, d_o, lse)


def _na_bwd_kv(qkv, bias_t, d_o, lse_rows, delta_rows, T):
    rows = T // GRID_W
    nb = rows // NA_RB
    tk = NA_RB * GRID_W
    tw = NA_WIN * GRID_W

    def body(k_ref, v_ref, q_ref, do_ref, b_ref, l_ref, dl_ref, dk_ref, dv_ref):
        kb = pl.program_id(1)
        ws = pl.multiple_of(jnp.clip(NA_RB * kb - 4, 0, rows - NA_WIN) * GRID_W, 256)
        qw = q_ref[pl.ds(ws, tw), :]
        dow = do_ref[pl.ds(ws, tw), :]
        w0 = ws // 256
        lw = jnp.concatenate([l_ref[0, w0 + i] for i in range(tw // 256)], axis=1)
        dw = jnp.concatenate([dl_ref[0, w0 + i] for i in range(tw // 256)], axis=1)
        k = k_ref[...]
        v = v_ref[...]
        lane = lax.broadcasted_iota(jnp.int32, (1, 128), 1)
        dks, dvs = [], []
        for hh in range(2):
            msk = (lane < NA_HEAD_DIM) if hh == 0 else (lane >= NA_HEAD_DIM)
            lrow = lw[hh:hh + 1, :]
            drow = dw[hh:hh + 1, :]
            st =_nt(jnp.where(msk, k, jnp.zeros_like(k)), qw) + b_ref[hh, 0]
            pt = jnp.exp(st - lrow)
            dvs.append(_nn(pt.astype(_BF16), dow))
            dpt = _nt(jnp.where(msk, v, jnp.zeros_like(v)), dow)
            dst = pt * (dpt - drow)
            dks.append(_nn(dst.astype(_BF16), qw))
        first = lane < NA_HEAD_DIM
        dk_ref[...] = jnp.where(first, dks[0], dks[1]).astype(_BF16)
        dv_ref[...] = jnp.where(first, dvs[0], dvs[1]).astype(_BF16)

    blk = lambda off: pl.BlockSpec((tk, 128), lambda hp, kb: (kb, off + hp))
    whole = lambda off: pl.BlockSpec((T, 128), lambda hp, kb: (0, off + hp))
    rowspec = pl.BlockSpec((1,) + lse_rows.shape[1:], lambda hp, kb: (hp, 0, 0, 0))
    return pl.pallas_call(
        body, name="na_bwd_kv", grid=(NA_HEADS // 2, nb),
        in_specs=[blk(4), blk(8), whole(0), whole(0),
                  pl.BlockSpec((2, 1, tk, tw), lambda hp, kb: (hp, _na_cls(kb, nb), 0, 0)),
                  rowspec, rowspec],
        out_specs=(blk(0), blk(0)),
        out_shape=(jax.ShapeDtypeStruct((T, NA_W), _BF16), jax.ShapeDtypeStruct((T, NA_W), _BF16)),
        compiler_params=_cparams(("arbitrary", "arbitrary")),
    )(qkv, qkv, qkv, d_o, bias_t, lse_rows, delta_rows)


def _rpb_grad(dbias, rows):
    tw = NA_WIN * GRID_W
    offs = [int(win[0] - blk[0] + NA_KH - 1) for blk, win in _na_class_rows(rows)]

    def body(x_ref, g_ref):
        sub = lax.broadcasted_iota(jnp.int32, (NA_RB, 1), 0)
        qc = lax.broadcasted_iota(jnp.int32, (NA_RB * GRID_W, 1), 0) % GRID_W
        tot = jnp.zeros((NA_RB, tw), _F32)
        for ci in range(3):
            xv = x_ref[0, ci]
            for bit in range(6):
                xv = jnp.where(((qc >> bit) & 1) == 1, pltpu.roll(xv, tw - (1 << bit), 1), xv)
            acc = pltpu.roll(jnp.sum(xv.reshape(NA_RB, GRID_W, tw), axis=1), NA_KW, 1)
            for a in range(NA_RB):
                tot = tot + jnp.where(sub == a, pltpu.roll(acc, (GRID_W * (offs[ci] - a)) % tw, 1), 0.0)
        g_ref[0] = jnp.broadcast_to(jnp.sum(tot, axis=0, keepdims=True), (8, tw))

    g = pl.pallas_call(
        body, name="rpb_grad", grid=(NA_HEADS,),
        in_specs=[pl.BlockSpec((1,) + dbias.shape[1:], lambda h: (h, 0, 0, 0))],
        out_specs=pl.BlockSpec((1, 8, tw), lambda h: (h, 0, 0)),
        out_shape=jax.ShapeDtypeStruct((NA_HEADS, 8, tw), _F32),
        compiler_params=_cparams(("arbitrary",)),
    )(dbias)
    return g[:, 0].reshape(NA_HEADS, NA_WIN, GRID_W)[:, :2 * NA_KH - 1, 1:2 * NA_KW]


def _halo_specs(tm, width, col_of, T, order):
    hb = tm // 8
    last = T // 8 - 1
    if order == "ij":
        cur = pl.BlockSpec((tm, width), lambda i, j: (i, col_of(j)))
        prev = pl.BlockSpec((8, width), lambda i, j: (jnp.maximum(i * hb - 1, 0), col_of(j)))
        nxt = pl.BlockSpec((8, width), lambda i, j: (jnp.minimum((i + 1) * hb, last), col_of(j)))
    else:
        cur = pl.BlockSpec((tm, width), lambda j, i: (i, col_of(j)))
        prev = pl.BlockSpec((8, width), lambda j, i: (jnp.maximum(i * hb - 1, 0), col_of(j)))
        nxt = pl.BlockSpec((8, width), lambda j, i: (jnp.minimum((i + 1) * hb, last), col_of(j)))
    return [prev, cur, nxt]


def _extend(prev_ref, cur_ref, next_ref, i, n_i):
    prev = jnp.where(i > 0, prev_ref[...], 0.0)
    nxt = jnp.where(i < n_i - 1, next_ref[...], 0.0)
    return jnp.concatenate([prev, cur_ref[...], nxt], axis=0)


def _conv_fwd(rest, conv_w, conv_b, T):
    tm = 512
    n_i = T // tm
    n = tm + 16

    def body(p_ref, c_ref, n_ref, w_ref, b_ref, o_ref):
        i = pl.program_id(0)
        ext = _extend(p_ref, c_ref, n_ref, i, n_i)
        acc = jnp.zeros((tm, 512), _F32) + b_ref[...]
        for j in range(CONV_W):
            acc = acc + w_ref[j:j + 1, :] * pltpu.roll(ext, (2 - j) % n, 0)[8:8 + tm]
        o_ref[...] = _silu(acc)

    return pl.pallas_call(
        body, name="conv_fwd", grid=(n_i, 2),
        in_specs=_halo_specs(tm, 512, lambda j: 1 + j, T, "ij")
        + [pl.BlockSpec((8, 512), lambda i, j: (0, j)), pl.BlockSpec((1, 512), lambda i, j: (0, j))],
        out_specs=pl.BlockSpec((tm, 512), lambda i, j: (i, j)),
        out_shape=jax.ShapeDtypeStruct((T, 2 * ML_W), _F32),
        compiler_params=_cparams(("arbitrary", "arbitrary")),
    )(rest, rest, rest, conv_w, conv_b)


def _conv_bwd(rest, conv_w, conv_b, da_f, da_b, T):
    tm = 512
    n_i = T // tm
    n = tm + 16

    def body(up, uc, un, fp, fc, fn, bp, bc, bn, w_ref, b_ref, du_ref, dw_ref):
        i = pl.program_id(1)
        ext_u = _extend(up, uc, un, i, n_i)
        ext_da = _extend(fp, fc, fn, i, n_i) + _extend(bp, bc, bn, i, n_i)
        shifted = [pltpu.roll(ext_u, (2 - j) % n, 0) for j in range(CONV_W)]
        pre = jnp.zeros((n, 512), _F32) + b_ref[...]
        for j in range(CONV_W):
            pre = pre + w_ref[j:j + 1, :] * shifted[j]
        gidx = i * tm - 8 + lax.broadcasted_iota(jnp.int32, (n, 1), 0)
        dpre = jnp.where((gidx >= 0) & (gidx < T), ext_da * _dsilu(pre), 0.0)
        du = jnp.zeros((tm, 512), _F32)
        for j in range(CONV_W):
            du = du + w_ref[j:j + 1, :] * pltpu.roll(dpre, (j - 2) % n, 0)[8:8 + tm]
        du_ref[...] = du.astype(_BF16)
        dpc = dpre[8:8 + tm]
        parts = [jnp.sum(dpc * shifted[j][8:8 + tm], axis=0, keepdims=True) for j in range(CONV_W)]
        parts.append(jnp.sum(dpc, axis=0, keepdims=True))
        parts.append(jnp.zeros((2, 512), _F32))
        upd = jnp.concatenate(parts, axis=0)

        @pl.when(i == 0)
        def _():
            dw_ref[...] = upd

        @pl.when(i > 0)
        def _():
            dw_ref[...] += upd

    return pl.pallas_call(
        body, name="conv_bwd", grid=(2, n_i),
        in_specs=_halo_specs(tm, 512, lambda j: 1 + j, T, "ji")
        + _halo_specs(tm, 512, lambda j: j, T, "ji") + _halo_specs(tm, 512, lambda j: j, T, "ji")
        + [pl.BlockSpec((8, 512), lambda j, i: (0, j)), pl.BlockSpec((1, 512), lambda j, i: (0, j))],
        out_specs=(pl.BlockSpec((tm, 512), lambda j, i: (i, j)), pl.BlockSpec((8, 512), lambda j, i: (0, j))),
        out_shape=(jax.ShapeDtypeStruct((T, 2 * ML_W), _BF16), jax.ShapeDtypeStruct((8, 2 * ML_W), _F32)),
        compiler_params=_cparams(("arbitrary", "arbitrary")),
    )(rest, rest, rest, da_f, da_f, da_f, da_b, da_b, da_b, conv_w, conv_b)


def _scan_rows(x, suffix):
    L = x.shape[0]
    row = lax.broadcasted_iota(jnp.int32, (L, 1), 0)
    step = 1
    while step < L:
        if suffix:
            x = x + jnp.where(row < L - step, pltpu.roll(x, L - step, 0), 0.0)
        else:
            x = x + jnp.where(row >= step, pltpu.roll(x, step, 0), 0.0)
        step *= 2
    return x


def _ml_gates(gt, rev):
    L = gt.shape[0]
    ri = lax.broadcasted_iota(jnp.int32, (L, L), 0)
    ci = lax.broadcasted_iota(jnp.int32, (L, L), 1)
    mask = (ci >= ri) if rev else (ci <= ri)
    lf = jnp.minimum(gt, 0.0) - jnp.log(1.0 + jnp.exp(-jnp.abs(gt)))
    b = _scan_rows(lf, suffix=rev)
    return mask, b, b.T, gt.T


def _ml_head_gates(gt, gates, head, rev):
    _, b, b_t, gt_t = gates
    ci = (8 if rev else 0) + head
    cf = ci + ML_HEADS
    last = 0 if rev else gt.shape[0] - 1
    return dict(icol=gt[:, ci:ci + 1], b_col=b[:, cf:cf + 1], b_row=b_t[cf:cf + 1, :],
                i_row=gt_t[ci:ci + 1, :], bl=b[last:last + 1, cf:cf + 1])


def _ml_chunk(q, k, v, hg, mask, C, n, m):
    icol, b_col, b_row, bl = hg["icol"], hg["b_col"], hg["b_row"], hg["bl"]
    dlog = jnp.where(mask, b_col - b_row + hg["i_row"], NEG)
    m_t = jnp.maximum(b_col + m, jnp.max(dlog, axis=1, keepdims=True))
    dm = jnp.exp(dlog - m_t)
    ks = k * (ML_HEAD_DIM ** -0.5)
    qb, kb, vb = q.astype(_BF16), ks.astype(_BF16), v.astype(_BF16)
    s = _nt(qb, kb) * dm
    g = jnp.exp(b_col + m - m_t)
    qc = _nt(qb, C.astype(_BF16))
    num = _nn(s.astype(_BF16), vb) + g * qc
    qn = jnp.sum(q * n, axis=1, keepdims=True)
    den = jnp.sum(s, axis=1, keepdims=True) + g * qn
    e_m = jnp.exp(-m_t)
    nrm = jnp.maximum(jnp.abs(den), e_m)
    h = num / nrm
    a_col = bl - b_col + icol
    m_new = jnp.maximum(bl + m, jnp.max(a_col, axis=0, keepdims=True))
    decay = jnp.exp(bl + m - m_new)
    w = jnp.exp(a_col - m_new)
    c_new = decay * C + _tn((w * v).astype(_BF16), kb)
    n_new = decay * n + jnp.sum(w * ks, axis=0, keepdims=True)
    aux = dict(dm=dm, ks=ks, qb=qb, kb=kb, vb=vb, s=s, g=g, qc=qc, qn=qn,
               den=den, e_m=e_m, nrm=nrm, decay=decay, w=w)
    return h, c_new, n_new, m_new, aux


def _mlstm_fwd(qk_act, rest, T, rev):
    tb = ML_CB * ML_CHUNK
    nblk = T // tb
    nc = T // ML_CHUNK
    bi = (lambda i: nblk - 1 - i) if rev else (lambda i: i)

    def body(q_ref, k_ref, v_ref, g_ref, h_ref, cs_ref, ns_ref, ms_ref, c_scr, n_scr, m_scr):
        @pl.when(pl.program_id(0) == 0)
        def _():
            c_scr[...] = jnp.zeros_like(c_scr)
            n_scr[...] = jnp.zeros_like(n_scr)
            m_scr[...] = jnp.zeros_like(m_scr)

        def step(j, carry):
            c = (ML_CB - 1 - j) if rev else j
            r0 = pl.multiple_of(c * ML_CHUNK, ML_CHUNK)
            gt = g_ref[pl.ds(r0, ML_CHUNK), :]
            gates = _ml_gates(gt, rev)
            for hd in range(ML_HEADS):
                cols = slice(hd * ML_HEAD_DIM, (hd + 1) * ML_HEAD_DIM)
                C = c_scr[hd]
                n = n_scr[hd:hd + 1, :]
                mrow = m_scr[hd:hd + 1, :]
                cs_ref[c, hd] = C
                ns_ref[c, hd:hd + 1, :] = n
                ms_ref[c, hd:hd + 1, :] = mrow
                h, c_new, n_new, m_new, _ = _ml_chunk(
                    q_ref[pl.ds(r0, ML_CHUNK), cols], k_ref[pl.ds(r0, ML_CHUNK), cols],
                    v_ref[pl.ds(r0, ML_CHUNK), cols], _ml_head_gates(gt, gates, hd, rev), gates[0],
                    C, n, mrow[:, 0:1])
                h_ref[pl.ds(r0, ML_CHUNK), cols] = h
                c_scr[hd] = c_new
                n_scr[hd:hd + 1, :] = n_new
                m_scr[hd:hd + 1, :] = jnp.broadcast_to(m_new, (1, 128))
            return carry

        lax.fori_loop(0, ML_CB, step, 0)

    return pl.pallas_call(
        body, name="mlstm_fwd_rev" if rev else "mlstm_fwd", grid=(nblk,),
        in_specs=[pl.BlockSpec((tb, ML_W), lambda i: (bi(i), 0)),
                  pl.BlockSpec((tb, ML_W), lambda i: (bi(i), 1)),
                  pl.BlockSpec((tb, ML_W), lambda i: (bi(i), 3)),
                  pl.BlockSpec((tb, 128), lambda i: (bi(i), GATE_COL // 128))],
        out_specs=(pl.BlockSpec((tb, ML_W), lambda i: (bi(i), 0)),
                   pl.BlockSpec((ML_CB, ML_HEADS, 128, 128), lambda i: (bi(i), 0, 0, 0)),
                   pl.BlockSpec((ML_CB, ML_HEADS, 128), lambda i: (bi(i), 0, 0)),
                   pl.BlockSpec((ML_CB, ML_HEADS, 128), lambda i: (bi(i), 0, 0))),
        out_shape=(jax.ShapeDtypeStruct((T, ML_W), _F32),
                   jax.ShapeDtypeStruct((nc, ML_HEADS, 128, 128), _F32),
                   jax.ShapeDtypeStruct((nc, ML_HEADS, 128), _F32),
                   jax.ShapeDtypeStruct((nc, ML_HEADS, 128), _F32)),
        scratch_shapes=[pltpu.VMEM((ML_HEADS, 128, 128), _F32), pltpu.VMEM((8, 128), _F32),
                        pltpu.VMEM((8, 128), _F32)],
        compiler_params=_cparams(("arbitrary",)),
    )(qk_act, qk_act, rest, rest)


def _mlstm_bwd(qk_act, rest, d_h, cs, ns, ms, T, rev):
    tb = ML_CB * ML_CHUNK
    nblk = T // tb
    bi = (lambda i: i) if rev else (lambda i: nblk - 1 - i)

    def body(q_ref, k_ref, v_ref, g_ref, dh_ref, cs_ref, ns_ref, ms_ref,
             dqk_ref, dv_ref, dg_ref, dc_scr, dn_scr):
        @pl.when(pl.program_id(0) == 0)
        def _():
            dc_scr[...] = jnp.zeros_like(dc_scr)
            dn_scr[...] = jnp.zeros_like(dn_scr)

        def step(j, carry):
            c = j if rev else (ML_CB - 1 - j)
            r0 = pl.multiple_of(c * ML_CHUNK, ML_CHUNK)
            gt = g_ref[pl.ds(r0, ML_CHUNK), :]
            gates = _ml_gates(gt, rev)
            mask = gates[0]
            lane = lax.broadcasted_iota(jnp.int32, (1, 128), 1)
            sub = lax.broadcasted_iota(jnp.int32, (128, 1), 0)
            db_t = jnp.zeros((ML_CHUNK, 128), _F32)
            da_t = jnp.zeros((ML_CHUNK, 128), _F32)
            cs_rows = jnp.zeros((128, ML_CHUNK), _F32)
            dbl_t = jnp.zeros((1, 128), _F32)
            for hd in range(ML_HEADS):
                cols = slice(hd * ML_HEAD_DIM, (hd + 1) * ML_HEAD_DIM)
                ci = (8 if rev else 0) + hd
                cf = ci + ML_HEADS
                q = q_ref[pl.ds(r0, ML_CHUNK), cols]
                k = k_ref[pl.ds(r0, ML_CHUNK), cols]
                v = v_ref[pl.ds(r0, ML_CHUNK), cols]
                C = cs_ref[c, hd]
                n = ns_ref[c, hd:hd + 1, :]
                m = ms_ref[c, hd:hd + 1, :][:, 0:1]
                dcn = dc_scr[hd]
                dnn = dn_scr[hd:hd + 1, :]
                h, _, _, _, a = _ml_chunk(q, k, v, _ml_head_gates(gt, gates, hd, rev), mask, C, n, m)
                d_hv = dh_ref[pl.ds(r0, ML_CHUNK), cols]
                g, s, w, ks = a["g"], a["s"], a["w"], a["ks"]
                qb, kb, vb = a["qb"], a["kb"], a["vb"]
                dnum = d_hv / a["nrm"]
                hdot = jnp.sum(d_hv * h, axis=1, keepdims=True)
                dden = jnp.where(jnp.abs(a["den"]) >= a["e_m"], -hdot / a["nrm"] * jnp.sign(a["den"]), 0.0)
                dnb = dnum.astype(_BF16)
                d_s = _nt(dnb, vb) + dden
                r = d_s * s
                dsqk = (d_s * a["dm"]).astype(_BF16)
                cb = C.astype(_BF16)
                dq = _nn(dsqk, kb) + g * _nn(dnb, cb) + (dden * g) * n
                dk = _tn(dsqk, qb)
                dv = _tn(s.astype(_BF16), dnb)
                dg = jnp.sum(dnum * a["qc"], axis=1, keepdims=True) + dden * a["qn"]
                db_col = jnp.sum(r, axis=1, keepdims=True) + dg * g
                cs_rows = cs_rows + jnp.where((sub == ci) | (sub == cf), jnp.sum(r, axis=0, keepdims=True), 0.0)
                dc_chunk = _tn((g * dnum).astype(_BF16), qb)
                dn_chunk = jnp.sum((dden * g) * q, axis=0, keepdims=True)
                dcb = dcn.astype(_BF16)
                vdc = _nn(vb, dcb)
                kdc = _nt(kb, dcb)
                dw = jnp.sum(vdc * ks, axis=1, keepdims=True) + jnp.sum(ks * dnn, axis=1, keepdims=True)
                dv = dv + w * kdc
                dk = dk + w * vdc + w * dnn
                da = dw * w
                ddecay = (jnp.sum(jnp.sum(dcn * C, axis=1, keepdims=True), axis=0, keepdims=True)
                          + jnp.sum(dnn * n, axis=1, keepdims=True))
                dbl = ddecay * a["decay"] + jnp.sum(da, axis=0, keepdims=True)
                db_t = db_t + jnp.where(lane == cf, db_col - da, 0.0)
                da_t = da_t + jnp.where(lane == ci, da, 0.0)
                dbl_t = dbl_t + jnp.where(lane == cf, dbl, 0.0)
                dc_scr[hd] = dc_chunk + a["decay"] * dcn
                dn_scr[hd:hd + 1, :] = dn_chunk + a["decay"] * dnn
                dqk_ref[pl.ds(r0, ML_CHUNK), cols] = dq
                dqk_ref[pl.ds(r0, ML_CHUNK), slice(ML_W + hd * 128, ML_W + (hd + 1) * 128)] = dk * (ML_HEAD_DIM ** -0.5)
                dv_ref[pl.ds(r0, ML_CHUNK), cols] = dv.astype(_BF16)
            lo = 8 if rev else 0
            is_i = (lane >= lo) & (lane < lo + ML_HEADS)
            is_f = (lane >= lo + ML_HEADS) & (lane < lo + 2 * ML_HEADS)
            cs_t = cs_rows.T
            db_all = db_t - jnp.where(is_f, cs_t, 0.0)
            dlf = _scan_rows(db_all, suffix=not rev) + dbl_t
            dg_ref[pl.ds(r0, ML_CHUNK), :] = (da_t + jnp.where(is_i, cs_t, 0.0)
                                               + jnp.where(is_f, dlf * _sigmoid(-gt), 0.0))
            return carry

        lax.fori_loop(0, ML_CB, step, 0)

    return pl.pallas_call(
        body, name="mlstm_bwd_rev" if rev else "mlstm_bwd", grid=(nblk,),
        in_specs=[pl.BlockSpec((tb, ML_W), lambda i: (bi(i), 0)),
                  pl.BlockSpec((tb, ML_W), lambda i: (bi(i), 1)),
                  pl.BlockSpec((tb, ML_W), lambda i: (bi(i), 3)),
                  pl.BlockSpec((tb, 128), lambda i: (bi(i), GATE_COL // 128)),
                  pl.BlockSpec((tb, ML_W), lambda i: (bi(i), 0)),
                  pl.BlockSpec((ML_CB, ML_HEADS, 128, 128), lambda i: (bi(i), 0, 0, 0)),
                  pl.BlockSpec((ML_CB, ML_HEADS, 128), lambda i: (bi(i), 0, 0)),
                  pl.BlockSpec((ML_CB, ML_HEADS, 128), lambda i: (bi(i), 0, 0))],
        out_specs=(pl.BlockSpec((tb, 2 * ML_W), lambda i: (bi(i), 0)),
                   pl.BlockSpec((tb, ML_W), lambda i: (bi(i), 0)),
                   pl.BlockSpec((tb, 128), lambda i: (bi(i), 0))),
        out_shape=(jax.ShapeDtypeStruct((T, 2 * ML_W), _F32), jax.ShapeDtypeStruct((T, ML_W), _BF16),
                   jax.ShapeDtypeStruct((T, 128), _F32)),
        scratch_shapes=[pltpu.VMEM((ML_HEADS, 128, 128), _F32), pltpu.VMEM((8, 128), _F32)],
        compiler_params=_cparams(("arbitrary",)),
    )(qk_act, qk_act, rest, rest, d_h, cs, ns, ms)


def _post(x, target, o_na, rest, h_f, h_b, gate, ml_norm_w, final_w, w_out_bf, T):
    tm = 256
    n_i = T // tm

    def body(x_ref, t_ref, o_ref, zna_ref, hf_ref, hb_ref, mo_ref, mz_ref, gate_ref, mw_ref, fw_ref, w_ref,
             dx1_ref, do_ref, dzna_ref, dh_ref, dmo_ref, dmz_ref, dwo_ref, vec_ref):
        i = pl.program_id(0)
        gate_v = gate_ref[...]
        fw = fw_ref[...]
        zna = zna_ref[...]
        o = o_ref[...]
        na_out = o * _silu(zna)
        hsum = hf_ref[...] + hb_ref[...]
        sg = _sigmoid(mo_ref[...])
        hm = hsum * sg
        mz = mz_ref[...]
        smz = _silu(mz)
        hn_l, rstd_l, ml_l = [], [], []
        for hd in range(ML_HEADS):
            cols = slice(hd * 128, (hd + 1) * 128)
            hh = hm[:, cols]
            mu = jnp.mean(hh, axis=-1, keepdims=True)
            var = jnp.mean(jnp.square(hh - mu), axis=-1, keepdims=True)
            rstd = lax.rsqrt(var + EPS)
            hn = (hh - mu) * rstd
            hn_l.append(hn)
            rstd_l.append(rstd)
            ml_l.append(hn * mw_ref[:, cols] * smz[:, cols])
        mix = jnp.concatenate([na_out] + ml_l, axis=1).astype(_BF16)
        y = _nn(mix, w_ref[...])
        x1 = x_ref[...] + gate_v * y
        r = lax.rsqrt(jnp.mean(x1 * x1, axis=-1, keepdims=True) + EPS)
        xhat = x1 * r
        out = xhat * fw
        err = out - t_ref[...]
        loss = 0.5 * jnp.sum(jnp.sum(err * err, axis=1, keepdims=True), axis=0, keepdims=True) / D_MODEL
        dout = err * (1.0 / D_MODEL)
        dfw = jnp.sum(dout * xhat, axis=0, keepdims=True)
        dxhat = dout * fw
        dx1 = r * (dxhat - xhat * jnp.mean(dxhat * xhat, axis=-1, keepdims=True))
        dx1_ref[...] = dx1
        dgate = jnp.sum(dx1 * y, axis=0, keepdims=True)
        dy = (dx1 * gate_v).astype(_BF16)
        dmix = _nt(dy, w_ref[...])
        dwo = _tn(mix, dy)
        dna = dmix[:, :NA_W]
        do_ref[...] = (dna * _silu(zna)).astype(_BF16)
        dzna_ref[...] = (dna * o * _dsilu(zna)).astype(_BF16)
        dmw_l = []
        for hd in range(ML_HEADS):
            cols = slice(hd * 128, (hd + 1) * 128)
            dml = dmix[:, NA_W + hd * 128:NA_W + (hd + 1) * 128]
            hn = hn_l[hd]
            mwv = mw_ref[:, cols]
            dmz_ref[:, cols] = (dml * hn * mwv * _dsilu(mz[:, cols])).astype(_BF16)
            dhn = dml * mwv * smz[:, cols]
            dmw_l.append(jnp.sum(dml * hn * smz[:, cols], axis=0, keepdims=True))
            dhm = rstd_l[hd] * (dhn - jnp.mean(dhn, axis=-1, keepdims=True)
                                - hn * jnp.mean(dhn * hn, axis=-1, keepdims=True))
            sgc = sg[:, cols]
            dh_ref[:, cols] = dhm * sgc
            dmo_ref[:, cols] = (dhm * hsum[:, cols] * sgc * (1.0 - sgc)).astype(_BF16)
        dmw = jnp.concatenate(dmw_l + [jnp.zeros((1, D_MODEL - ML_W), _F32)], axis=1)
        lane = lax.broadcasted_iota(jnp.int32, (1, D_MODEL), 1)
        vec = jnp.concatenate([dfw, dgate, dmw, jnp.where(lane == 0, loss, 0.0),
                               jnp.zeros((4, D_MODEL), _F32)], axis=0)

        @pl.when(i == 0)
        def _():
            dwo_ref[...] = dwo
            vec_ref[...] = vec

        @pl.when(i > 0)
        def _():
            dwo_ref[...] += dwo
            vec_ref[...] += vec

    tok = lambda w, j: pl.BlockSpec((tm, w), lambda i: (i, j))
    tok3 = pl.BlockSpec((None, tm, D_MODEL), lambda i: (0, i, 0))
    row = lambda w: pl.BlockSpec((1, w), lambda i: (0, 0))
    f32 = lambda w: jax.ShapeDtypeStruct((T, w), _F32)
    bf16 = lambda w: jax.ShapeDtypeStruct((T, w), _BF16)
    return pl.pallas_call(
        body, name="post", grid=(n_i,),
        in_specs=[tok3, tok3, tok(NA_W, 0), tok(NA_W, 0), tok(ML_W, 0), tok(ML_W, 0),
                  tok(ML_W, 4), tok(ML_W, 5), row(D_MODEL), row(ML_W), row(D_MODEL),
                  pl.BlockSpec((D_MODEL, D_MODEL), lambda i: (0, 0))],
        out_specs=(tok(D_MODEL, 0), tok(NA_W, 0), tok(NA_W, 0), tok(ML_W, 0), tok(ML_W, 0), tok(ML_W, 0),
                   pl.BlockSpec((D_MODEL, D_MODEL), lambda i: (0, 0)),
                   pl.BlockSpec((8, D_MODEL), lambda i: (0, 0))),
        out_shape=(f32(D_MODEL), bf16(NA_W), bf16(NA_W), f32(ML_W), bf16(ML_W),
                   bf16(ML_W), jax.ShapeDtypeStruct((D_MODEL, D_MODEL), _F32),
                   jax.ShapeDtypeStruct((8, D_MODEL), _F32)),
        compiler_params=_cparams(("arbitrary",)),
    )(x, target, o_na, rest, h_f, h_b, rest, rest, gate, ml_norm_w, final_w, w_out_bf)


def _section_specs(sections, tm):
    specs, args = [], []
    for _, width, parts in sections:
        for arr, cb in parts:
            specs.append(pl.BlockSpec((tm, width), functools.partial(lambda i, cb: (i, cb), cb=cb)))
            args.append(arr)
    return specs, args


def _section_values(sections, refs, dtype):
    vals, at = [], 0
    for _, _, parts in sections:
        v = refs[at][...]
        for r in refs[at + 1:at + len(parts)]:
            v = v.astype(_F32) + r[...].astype(_F32)
        at += len(parts)
        vals.append(v.astype(dtype))
    return vals


def _inproj_bwd_x(x, dx1, scale1p, norm_w, w_in_bf, sections, T):
    tm = 256
    sspecs, sargs = _section_specs(sections, tm)
    ns = len(sargs)

    def body(*refs):
        x_ref, dx1_ref, sc_ref, nw_ref, w_ref = refs[:5]
        srefs = refs[5:5 + ns]
        gx_ref, vec_ref = refs[5 + ns:]
        i = pl.program_id(0)
        vals = _section_values(sections, srefs, _BF16)
        dh = jnp.zeros((tm, D_MODEL), _F32)
        for (c0, width, _), val in zip(sections, vals):
            dh = dh + _nt(val, w_ref[:, c0:c0 + width])
        xv = x_ref[...]
        r = lax.rsqrt(jnp.mean(xv * xv, axis=-1, keepdims=True) + EPS)
        xhat = xv * r
        nw = nw_ref[...]
        dshift = jnp.sum(dh, axis=0, keepdims=True)
        dscale = jnp.sum(dh * xhat * nw, axis=0, keepdims=True)
        dhpre = dh * sc_ref[...]
        dnw = jnp.sum(dhpre * xhat, axis=0, keepdims=True)
        dxhat = dhpre * nw
        gx_ref[...] = dx1_ref[...] + r * (dxhat - xhat * jnp.mean(dxhat * xhat, axis=-1, keepdims=True))
        vec = jnp.concatenate([dshift, dscale, dnw, jnp.zeros((5, D_MODEL), _F32)], axis=0)

        @pl.when(i == 0)
        def _():
            vec_ref[...] = vec

        @pl.when(i > 0)
        def _():
            vec_ref[...] += vec

    row = pl.BlockSpec((1, D_MODEL), lambda i: (0, 0))
    tok = pl.BlockSpec((tm, D_MODEL), lambda i: (i, 0))
    tok3 = pl.BlockSpec((None, tm, D_MODEL), lambda i: (0, i, 0))
    return pl.pallas_call(
        body, name="inproj_bwd_x", grid=(T // tm,),
        in_specs=[tok3, tok, row, row, pl.BlockSpec((D_MODEL, IN_PAD), lambda i: (0, 0))] + sspecs,
        out_specs=(tok3, pl.BlockSpec((8, D_MODEL), lambda i: (0, 0))),
        out_shape=(jax.ShapeDtypeStruct((1, T, D_MODEL), _F32), jax.ShapeDtypeStruct((8, D_MODEL), _F32)),
        compiler_params=_cparams(("arbitrary",)),
    )(x, dx1, scale1p, norm_w, w_in_bf, *sargs)


def _inproj_bwd_w(h_t, sections, T):
    tm = 1024
    n_i = T // tm
    sspecs, sargs = _section_specs(sections, tm)
    ns = len(sargs)

    def body(*refs):
        h_ref = refs[0]
        srefs = refs[1:1 + ns]
        dw_ref, db_ref, acc, sem = refs[1 + ns:]
        i = pl.program_id(0)

        @pl.when(i == 0)
        def _():
            acc[...] = jnp.zeros_like(acc)
            db_ref[...] = jnp.zeros_like(db_ref)

        hv = h_ref[...]
        for (c0, width, _), v in zip(sections, _section_values(sections, srefs, _F32)):
            acc[:, c0:c0 + width] += _nn(hv, v.astype(_BF16))
            db_ref[0:1, c0:c0 + width] += jnp.sum(v, axis=0, keepdims=True)

        @pl.when(i == n_i - 1)
        def _():
            cp = pltpu.make_async_copy(acc, dw_ref, sem)
            cp.start()
            cp.wait()

    return pl.pallas_call(
        body, name="inproj_bwd_w", grid=(n_i,),
        in_specs=[pl.BlockSpec((D_MODEL, tm), lambda i: (0, i))] + sspecs,
        out_specs=(pl.BlockSpec(memory_space=pl.ANY), pl.BlockSpec((8, IN_PAD), lambda i: (0, 0))),
        out_shape=(jax.ShapeDtypeStruct((D_MODEL, IN_PAD), _F32), jax.ShapeDtypeStruct((8, IN_PAD), _F32)),
        scratch_shapes=[pltpu.VMEM((D_MODEL, IN_PAD), _F32), pltpu.SemaphoreType.DMA],
        compiler_params=_cparams(("arbitrary",)),
    )(h_t, *sargs)


def _adamw_math(w, g, m, v):
    m = ADAM_B1 * m + (1.0 - ADAM_B1) * g
    v = ADAM_B2 * v + (1.0 - ADAM_B2) * jnp.square(g)
    m_hat = m / (1.0 - ADAM_B1 ** ADAM_STEP)
    v_hat = v / (1.0 - ADAM_B2 ** ADAM_STEP)
    delta = -ADAM_LR * (m_hat / (jnp.sqrt(v_hat) + ADAM_EPS) + ADAM_WD * w)
    return delta, m, v


def _adamw_slots(w, m, v, slots, tr, name):
    R, C = w.shape

    def body(w_ref, m_ref, v_ref, s_ref, g_ref, d_ref, nm_ref, nv_ref):
        g = s_ref[0].astype(_F32)
        for k in range(1, N_DEV):
            g = g + s_ref[k].astype(_F32)
        g_ref[...] = g
        d_ref[...], nm_ref[...], nv_ref[...] = _adamw_math(w_ref[...], g, m_ref[...], v_ref[...])

    blk = pl.BlockSpec((tr, C), lambda i: (i, 0))
    return pl.pallas_call(
        body, name=name, grid=(R // tr,),
        in_specs=[blk, blk, blk, pl.BlockSpec((N_DEV, tr, C), lambda i: (0, i, 0))],
        out_specs=(blk, blk, blk, blk),
        out_shape=tuple(jax.ShapeDtypeStruct((R, C), _F32) for _ in range(4)),
        compiler_params=_cparams(("arbitrary",)),
    )(w, m, v, slots)


def _w_ada_update(c_all, dmod_my, w, m, v):
    def body(c_ref, d_ref, w_ref, m_ref, v_ref, g_ref, dl_ref, nm_ref, nv_ref):
        g = lax.dot_general(_silu(c_ref[...]), d_ref[...], (((0,), (0,)), ((), ())),
                            precision=_HI, preferred_element_type=_F32)
        g_ref[...] = g
        dl_ref[...], nm_ref[...], nv_ref[...] = _adamw_math(w_ref[...], g, m_ref[...], v_ref[...])

    return pl.pallas_call(
        body, name="w_ada_update",
        out_shape=tuple(jax.ShapeDtypeStruct(w.shape, _F32) for _ in range(4)),
        compiler_params=_cparams(),
    )(c_all, dmod_my, w, m, v)


_PACK = (("b_ada", 3072, 3072), ("norm_w", 1024, 1024), ("b_in", IN_W, IN_PAD), ("conv_w", 5120, 5120),
         ("conv_b", 1024, 1024), ("rpb", 3720, 3840), ("ml_norm_w", 512, 512), ("final_norm_w", 1024, 1024),
         ("loss", 1, 128))
_PACK_OFF = {}
_off = 0
for _name, _len, _pad in _PACK:
    _PACK_OFF[_name] = (_off, _len)
    _off += _pad
_PACK_LEN = _off


def _pack(parts):
    cols = []
    for name, length, pad in _PACK:
        vec = parts[name].reshape(-1).astype(_F32)
        cols.append(jnp.pad(vec, (0, pad - length)))
    return jnp.concatenate(cols).reshape(1, _PACK_LEN)


def _unpack(vec, name, shape):
    off, length = _PACK_OFF[name]
    return vec.reshape(-1)[off:off + length].reshape(shape)


def kernel(x, c, w_ada, b_ada, norm_w, w_in, b_in, conv_w, conv_b, rpb, ml_norm_w, w_out, final_norm_w, loss_target, m_w_ada, m_b_ada, m_norm_w, m_w_in, m_b_in, m_conv_w, m_conv_b, m_rpb, m_ml_norm_w, m_w_out, m_final_norm_w, v_w_ada, v_b_ada, v_norm_w, v_w_in, v_b_in, v_conv_w, v_conv_b, v_rpb, v_ml_norm_w, v_w_out, v_final_norm_w):
    T = x.shape[1]
    rows = T // GRID_W
    me = 4 * lax.axis_index("x") + 2 * lax.axis_index("y") + lax.axis_index("c")
    n_in = w_in.shape[2]
    n_ada = w_ada.shape[2]
    n_cw = conv_w.shape[2]
    n_wo = w_out.shape[1]

    g_w_in, g_w_out, g_conv_w, g_c = _exchange(
        [w_in[0].astype(_BF16), w_out[0].astype(_BF16), conv_w[0], c], [False] * 4, "gather_weights")
    w_in_full = g_w_in.transpose(1, 0, 2).reshape(D_MODEL, N_DEV * n_in)
    w_in_bf = jnp.pad(w_in_full, ((0, 0), (0, IN_PAD - IN_W)))
    b_in_pad = jnp.pad(b_in, ((0, 0), (0, IN_PAD - IN_W)))
    w_out_bf = g_w_out.reshape(N_DEV * n_wo, D_MODEL)
    conv_w_full = jnp.pad(g_conv_w.transpose(1, 0, 2).reshape(CONV_W, N_DEV * n_cw), ((0, 3), (0, 0)))
    c_all = g_c.reshape(N_DEV, D_MODEL)

    b_ada_my = lax.dynamic_slice(b_ada, (0, me * n_ada), (1, n_ada))
    (mod_slots,) = _exchange([_mod_part(c_all, w_ada[0], b_ada_my)], [False], "gather_mod")
    mod = lax.dynamic_index_in_dim(mod_slots, me, axis=1, keepdims=False).reshape(1, 3 * D_MODEL)
    shift, scale, gate = mod[:, :D_MODEL], mod[:, D_MODEL:2 * D_MODEL], mod[:, 2 * D_MODEL:]
    scale1p = 1.0 + scale

    qkv, rest, h_bf = _inproj_fwd(x, scale1p, shift, norm_w, w_in_bf, b_in_pad)
    bias, bias_t = _na_bias_tables(rpb[0], rows)
    o_na, lse, lse_rows = _na_fwd(qkv, bias, T)
    qk_act = _conv_fwd(rest, conv_w_full, conv_b, T)
    h_f, cs_f, ns_f, ms_f = _mlstm_fwd(qk_act, rest, T, False)
    h_b, cs_b, ns_b, ms_b = _mlstm_fwd(qk_act, rest, T, True)

    dx1, d_o, dz_na, d_h, d_mo, d_mz, dwo, pvec = _post(
        x, loss_target, o_na, rest, h_f, h_b, gate, ml_norm_w, final_norm_w.reshape(1, D_MODEL), w_out_bf, T)

    dq_na, delta_rows, dbias = _na_bwd_q(qkv, bias, o_na, d_o, lse, T)
    dk_na, dv_na = _na_bwd_kv(qkv, bias_t, d_o, lse_rows, delta_rows, T)
    d_rpb = _rpb_grad(dbias, rows)
    dqk_f, dv_f, dg_f = _mlstm_bwd(qk_act, rest, d_h, cs_f, ns_f, ms_f, T, False)
    dqk_b, dv_b, dg_b = _mlstm_bwd(qk_act, rest, d_h, cs_b, ns_b, ms_b, T, True)
    d_u, dconv = _conv_bwd(rest, conv_w_full, conv_b, dqk_f, dqk_b, T)

    sections = [(0, 512, [(dq_na, 0)]), (512, 512, [(dk_na, 0)]), (1024, 512, [(dv_na, 0)]),
                (1536, 512, [(dz_na, 0)]), (2048, 512, [(d_u, 0)]), (2560, 512, [(d_u, 1)]),
                (3072, 512, [(dv_f, 0), (dv_b, 0)]), (3584, 512, [(d_mo, 0)]), (4096, 512, [(d_mz, 0)]),
                (4608, 128, [(dg_f, 0), (dg_b, 0)])]
    grad_x, xvec = _inproj_bwd_x(x, dx1, scale1p, norm_w, w_in_bf, sections, T)
    dw_pad, db_pad = _inproj_bwd_w(h_bf, sections, T)
    dw_in = dw_pad[:, :IN_W]
    db_in = db_pad[0, :IN_W]

    small = _pack({
        "b_ada": jnp.concatenate([xvec[0], xvec[1], pvec[1]]),
        "norm_w": xvec[2], "b_in": db_in, "conv_w": dconv[:CONV_W], "conv_b": dconv[CONV_W],
        "rpb": d_rpb, "ml_norm_w": pvec[2, :ML_W], "final_norm_w": pvec[0], "loss": pvec[3, :1]})
    s_small, s_w_in, s_w_out = _exchange(
        [small, dw_in.astype(_BF16).reshape(D_MODEL, N_DEV, n_in).transpose(1, 0, 2),
         dwo.astype(_BF16).reshape(N_DEV, n_wo, D_MODEL)],
        [False, True, True], "exchange_grads")

    g_w_in_s, d_w_in, nm_w_in, nv_w_in = _adamw_slots(w_in[0], m_w_in[0], v_w_in[0], s_w_in, 128, "adamw_w_in")
    g_w_out_s, d_w_out, nm_w_out, nv_w_out = _adamw_slots(w_out[0], m_w_out[0], v_w_out[0], s_w_out, n_wo,
                                                          "adamw_w_out")
    dmod_all = s_small[:, 0, :3 * D_MODEL]
    dmod_my = lax.dynamic_slice(dmod_all, (0, me * n_ada), (N_DEV, n_ada))
    g_w_ada, d_w_ada, nm_w_ada, nv_w_ada = _w_ada_update(c_all, dmod_my, w_ada[0], m_w_ada[0], v_w_ada[0])

    def embed(shard):
        return lax.dynamic_update_slice(jnp.zeros((CONV_W, N_DEV * n_cw), _F32), shard[0], (0, me * n_cw))

    zero1 = jnp.zeros((1,), _F32)
    packed = lambda b_a, n_w, b_i, c_w, c_b, rp, mn, fn: _pack({
        "b_ada": b_a, "norm_w": n_w, "b_in": b_i, "conv_w": embed(c_w), "conv_b": c_b, "rpb": rp,
        "ml_norm_w": mn, "final_norm_w": fn, "loss": zero1})
    pw = packed(b_ada, norm_w, b_in, conv_w, conv_b, rpb, ml_norm_w, final_norm_w)
    pm = packed(m_b_ada, m_norm_w, m_b_in, m_conv_w, m_conv_b, m_rpb, m_ml_norm_w, m_final_norm_w)
    pv = packed(v_b_ada, v_norm_w, v_b_in, v_conv_w, v_conv_b, v_rpb, v_ml_norm_w, v_final_norm_w)
    sg, sd, sm, sv = _adamw_slots(pw, pm, pv, s_small, 1, "adamw_small")

    def small_outs(vec):
        cw = lax.dynamic_slice(_unpack(vec, "conv_w", (CONV_W, N_DEV * n_cw)), (0, me * n_cw), (CONV_W, n_cw))
        return dict(b_ada=_unpack(vec, "b_ada", b_ada.shape), norm_w=_unpack(vec, "norm_w", norm_w.shape),
                    b_in=_unpack(vec, "b_in", b_in.shape), conv_w=cw[None],
                    conv_b=_unpack(vec, "conv_b", conv_b.shape), rpb=_unpack(vec, "rpb", rpb.shape),
                    ml_norm_w=_unpack(vec, "ml_norm_w", ml_norm_w.shape),
                    final_norm_w=_unpack(vec, "final_norm_w", final_norm_w.shape))

    loss = _unpack(sg, "loss", ())
    order = ("w_ada", "b_ada", "norm_w", "w_in", "b_in", "conv_w", "conv_b", "rpb", "ml_norm_w", "w_out",
             "final_norm_w")
    outs = []
    for vec, big in ((sg, (g_w_ada, g_w_in_s, g_w_out_s)), (sd, (d_w_ada, d_w_in, d_w_out)),
                     (sm, (nm_w_ada, nm_w_in, nm_w_out)), (sv, (nv_w_ada, nv_w_in, nv_w_out))):
        group = small_outs(vec)
        group.update(w_ada=big[0][None], w_in=big[1][None], w_out=big[2][None])
        outs.extend(group[name] for name in order)
    return (loss, grad_x, *outs)
```

```python
import functools

import numpy as np
import jax
import jax.numpy as jnp
from jax import lax
from jax.experimental import pallas as pl
from jax.experimental.pallas import tpu as pltpu

N_DEV = 8
D_MODEL = 1024
GRID_W = 64
NA_HEADS = 8
NA_HEAD_DIM = 64
NA_KH = 8
NA_KW = 16
NA_W = 512
ML_HEADS = 4
ML_HEAD_DIM = 128
ML_W = 512
ML_CHUNK = 512
CONV_W = 5
EPS = 1e-6
IN_W = 4624
IN_PAD = 4736
REST_W = IN_PAD - 3 * NA_W
GATE_COL = 3072
NEG = -1e30
NA_RB = 4
NA_WIN = 12
NA_SUB = 2
ML_CB = 1
ADAM_LR = 0.001
ADAM_B1 = 0.9
ADAM_B2 = 0.999
ADAM_EPS = 1e-08
ADAM_WD = 0.01
ADAM_STEP = 10
VMEM_LIMIT = 56 * 1024 * 1024

_F32 = jnp.float32
_BF16 = jnp.bfloat16
_HI = lax.Precision.HIGHEST


def _cparams(sem=None):
    return pltpu.CompilerParams(dimension_semantics=sem, vmem_limit_bytes=VMEM_LIMIT)


def _nt(a, b):
    return lax.dot_general(a, b, (((1,), (1,)), ((), ())), preferred_element_type=_F32)


def _tn(a, b):
    return lax.dot_general(a, b, (((0,), (0,)), ((), ())), preferred_element_type=_F32)


def _nn(a, b):
    return jnp.dot(a, b, preferred_element_type=_F32)


def _sigmoid(x):
    return 1.0 / (1.0 + jnp.exp(-x))


def _silu(x):
    return x * _sigmoid(x)


def _dsilu(x):
    s = _sigmoid(x)
    return s * (1.0 + x * (1.0 - s))


def _exchange(arrs, scatter, name):
    n = len(arrs)
    out_shape = []
    for a, sc in zip(arrs, scatter):
        blk = a.shape[1:] if sc else a.shape
        out_shape.append(jax.ShapeDtypeStruct((N_DEV,) + tuple(blk), a.dtype))

    def body(*refs):
        ins = refs[:n]
        outs = refs[n:2 * n]
        send_sems, recv_sems, local_sems = refs[2 * n:]
        x, y, c = lax.axis_index("x"), lax.axis_index("y"), lax.axis_index("c")
        me = 4 * x + 2 * y + c
        local, sends, recvs = [], [], []
        for a in range(n):
            own = ins[a].at[me] if scatter[a] else ins[a]
            cp = pltpu.make_async_copy(own, outs[a].at[me], local_sems.at[a])
            cp.start()
            local.append(cp)
            for k in range(1, N_DEV):
                px = 1 - x if k & 4 else x
                py = 1 - y if k & 2 else y
                pc = 1 - c if k & 1 else c
                p = 4 * px + 2 * py + pc
                src = ins[a].at[p] if scatter[a] else ins[a]
                snd = pltpu.make_async_remote_copy(
                    src_ref=src, dst_ref=outs[a].at[me],
                    send_sem=send_sems.at[a, k - 1], recv_sem=recv_sems.at[a, k - 1],
                    device_id=(px, py, pc), device_id_type=pl.DeviceIdType.MESH)
                snd.start()
                sends.append(snd)
                rcv = pltpu.make_async_remote_copy(
                    src_ref=src, dst_ref=outs[a].at[p],
                    send_sem=send_sems.at[a, k - 1], recv_sem=recv_sems.at[a, k - 1],
                    device_id=(px, py, pc), device_id_type=pl.DeviceIdType.MESH)
                recvs.append(rcv)
        for rcv in recvs:
            rcv.wait_recv()
        for snd in sends:
            snd.wait_send()
        for cp in local:
            cp.wait()

    any_spec = pl.BlockSpec(memory_space=pl.ANY)
    res = pl.pallas_call(
        body, name=name, out_shape=tuple(out_shape),
        in_specs=[any_spec] * n, out_specs=tuple([any_spec] * n),
        scratch_shapes=[pltpu.SemaphoreType.DMA((n, N_DEV - 1)),
                        pltpu.SemaphoreType.DMA((n, N_DEV - 1)),
                        pltpu.SemaphoreType.DMA((n,))],
    )(*arrs)
    return list(res)


def _mod_part(c_all, w_ada, b_my):
    def body(c_ref, w_ref, b_ref, o_ref):
        o_ref[...] = jnp.dot(_silu(c_ref[...]), w_ref[...], precision=_HI,
                             preferred_element_type=_F32) + b_ref[...]

    return pl.pallas_call(
        body, name="mod_part",
        out_shape=jax.ShapeDtypeStruct((N_DEV, w_ada.shape[1]), _F32),
        compiler_params=_cparams(),
    )(c_all, w_ada, b_my)


def _inproj_fwd(x, scale1p, shift, norm_w, w_in_bf, b_in_pad):
    T = x.shape[1]
    tm = 256
    n_q = 3 * NA_W

    def body(x_ref, sc_ref, sh_ref, nw_ref, w_ref, b_ref, qkv_ref, rest_ref, h_ref):
        xv = x_ref[...]
        r = lax.rsqrt(jnp.mean(xv * xv, axis=-1, keepdims=True) + EPS)
        h = xv * r * nw_ref[...] * sc_ref[...] + sh_ref[...]
        hb = h.astype(_BF16)
        h_ref[...] = h.T.astype(_BF16)
        for n0 in range(0, IN_PAD, 512):
            wd = min(512, IN_PAD - n0)
            acc = _nn(hb, w_ref[:, n0:n0 + wd]) + b_ref[:, n0:n0 + wd]
            if n0 == 0:
                acc = acc * (NA_HEAD_DIM ** -0.5)
            if n0 < n_q:
                qkv_ref[:, n0:n0 + wd] = acc.astype(_BF16)
            else:
                rest_ref[:, n0 - n_q:n0 - n_q + wd] = acc

    row = lambda w: pl.BlockSpec((1, w), lambda i: (0, 0))
    return pl.pallas_call(
        body, name="inproj_fwd", grid=(T // tm,),
        in_specs=[pl.BlockSpec((None, tm, D_MODEL), lambda i: (0, i, 0)), row(D_MODEL), row(D_MODEL), row(D_MODEL),
                  pl.BlockSpec((D_MODEL, IN_PAD), lambda i: (0, 0)), row(IN_PAD)],
        out_specs=(pl.BlockSpec((tm, n_q), lambda i: (i, 0)),
                   pl.BlockSpec((tm, REST_W), lambda i: (i, 0)),
                   pl.BlockSpec((D_MODEL, tm), lambda i: (0, i))),
        out_shape=(jax.ShapeDtypeStruct((T, n_q), _BF16),
                   jax.ShapeDtypeStruct((T, REST_W), _F32),
                   jax.ShapeDtypeStruct((D_MODEL, T), _BF16)),
        compiler_params=_cparams(("arbitrary",)),
    )(x, scale1p, shift, norm_w, w_in_bf, b_in_pad)


def _na_class_rows(rows, kv=False):
    nb = rows // NA_RB
    reps = (0, 1, 2, nb - 2, nb - 1) if kv else (0, min(1, nb - 1), nb - 1)
    out = []
    for rb in reps:
        ws = int(np.clip(NA_RB * rb - 4, 0, rows - NA_WIN))
        out.append((NA_RB * rb + np.arange(NA_RB), ws + np.arange(NA_WIN)))
    return out


def _na_pair_index(rows, qrows, krows):
    start = lambda r: np.clip(r - NA_KH // 2, 0, rows - NA_KH)
    col = np.arange(GRID_W)
    cstart = np.clip(col - NA_KW // 2, 0, GRID_W - NA_KW)
    dy = krows[None, :] - qrows[:, None] + NA_KH - 1
    vr = (krows[None, :] >= start(qrows)[:, None]) & (krows[None, :] < start(qrows)[:, None] + NA_KH)
    dx = np.clip(col[None, :] - col[:, None], -(NA_KW - 1), NA_KW - 1) + NA_KW - 1
    vc = (col[None, :] >= cstart[:, None]) & (col[None, :] < cstart[:, None] + NA_KW)
    nq, nk = len(qrows), len(krows)
    dy4 = np.broadcast_to(np.clip(dy, 0, 2 * NA_KH - 2)[:, None, :, None], (nq, GRID_W, nk, GRID_W))
    dx4 = np.broadcast_to(dx[None, :, None, :], (nq, GRID_W, nk, GRID_W))
    valid = vr[:, None, :, None] & vc[None, :, None, :]
    idx = (dy4 * (2 * NA_KW - 1) + dx4).reshape(nq * GRID_W, nk * GRID_W)
    return idx.astype(np.int32), valid.reshape(nq * GRID_W, nk * GRID_W), (dy, vr, dx, vc)


def _na_half_slabs(rpb):
    _, _, (_, _, dx, vc) = _na_pair_index(NA_WIN, np.arange(1), np.arange(1))
    xs, ys = np.meshgrid(np.arange(GRID_W), np.arange(GRID_W), indexing="ij")
    consts = []
    for trans in (False, True):
        qc, kc = (ys, xs) if trans else (xs, ys)
        for right in (False, True):
            pos = (xs * 128 + (GRID_W if right else 0) + ys).reshape(-1)
            oh = np.zeros((32, GRID_W * 128), np.float32)
            oh[dx[qc, kc].reshape(-1), pos] = 1.0
            col_neg = np.zeros((1, GRID_W * 128), np.float32)
            col_neg[0, pos] = np.where(vc[qc, kc].reshape(-1), 0.0, NEG)
            half = np.zeros((1, GRID_W * 128), np.float32)
            half[0, pos] = 1.0
            consts += [jnp.asarray(oh), jnp.asarray(col_neg), jnp.asarray(half)]
    row_neg = np.where(np.arange(NA_HEADS * 16) % 16 == 15, NEG, 0.0).astype(np.float32).reshape(-1, 1)
    rp = jnp.pad(rpb, ((0, 0), (0, 1), (0, 1))).reshape(NA_HEADS * 16, 32)

    def body(*refs):
        r_ref, rn_ref = refs[0], refs[1]
        for t in range(4):
            oh_ref, cn_ref, half_ref = refs[2 + 3 * t:5 + 3 * t]
            refs[14 + t][...] = (jnp.dot(r_ref[...], oh_ref[...], precision=_HI, preferred_element_type=_F32)
                                 + cn_ref[...] + rn_ref[...] * half_ref[...])

    outs = pl.pallas_call(
        body, name="na_half_slabs",
        out_shape=tuple(jax.ShapeDtypeStruct((NA_HEADS * 16, GRID_W * 128), _F32) for _ in range(4)),
        compiler_params=_cparams(),
    )(rp, jnp.asarray(row_neg), *consts)
    return [o.reshape(NA_HEADS, 16, GRID_W, 128) for o in outs]


def _na_bias_tables(rpb, rows):
    left, right, left_t, right_t = _na_half_slabs(rpb)
    didx, didx_t = [], []
    for blk, win in _na_class_rows(rows):
        _, _, (dy, vr, _, _) = _na_pair_index(rows, blk, win)
        didx.append(np.where(vr, dy, 15))
    for blk, win in _na_class_rows(rows, kv=True):
        _, _, (dy, vr, _, _) = _na_pair_index(rows, win, blk)
        didx_t.append(np.where(vr, dy, 15).T)

    def fill(ref, left_ref, right_ref, tabs):
        for ci, tab in enumerate(tabs):
            for a in range(NA_RB):
                for j in range(NA_WIN // 2):
                    ref[0, ci, a * GRID_W:(a + 1) * GRID_W, j * 128:(j + 1) * 128] = (
                        left_ref[0, int(tab[a, 2 * j])] + right_ref[0, int(tab[a, 2 * j + 1])])

    def body(l_ref, r_ref, lt_ref, rt_ref, b_ref, bt_ref):
        fill(b_ref, l_ref, r_ref, didx)
        fill(bt_ref, lt_ref, rt_ref, didx_t)

    slab = pl.BlockSpec((1, 16, GRID_W, 128), lambda h: (h, 0, 0, 0))
    tab = lambda n: pl.BlockSpec((1, n, NA_RB * GRID_W, NA_WIN * GRID_W), lambda h: (h, 0, 0, 0))
    shape = lambda n: jax.ShapeDtypeStruct((NA_HEADS, n, NA_RB * GRID_W, NA_WIN * GRID_W), _F32)
    return pl.pallas_call(
        body, name="na_tables", grid=(NA_HEADS,),
        in_specs=[slab] * 4, out_specs=(tab(3), tab(5)), out_shape=(shape(3), shape(5)),
        compiler_params=_cparams(("arbitrary",)),
    )(left, right, left_t, right_t)


def _na_cls(i, nb, kv):
    if kv:
        return jnp.where(i < 2, i, jnp.where(i >= nb - 2, i - (nb - 5), 2))
    return jnp.where(i == 0, 0, jnp.where(i == nb - 1, 2, 1))


def _na_sub(rb, u, rows, kv=False):
    sb = NA_SUB * rb + u
    cls = _na_cls(sb, rows // NA_RB, kv)
    ws = pl.multiple_of(jnp.clip(NA_RB * sb - 4, 0, rows - NA_WIN) * GRID_W, 256)
    return cls, ws


def _pair_rows_shape(T):
    return jax.ShapeDtypeStruct((NA_HEADS // 2, T // 256, 2, 256), _F32)


def _pair_rows_spec(ts):
    return pl.BlockSpec((1, ts // 256, 2, 256), lambda hp, rb: (hp, rb, 0, 0))


def _store_pair_rows(ref, base, col0, col1):
    tq = col0.shape[0]
    lane = lax.broadcasted_iota(jnp.int32, (1, 128), 1)
    tile = jnp.where(lane == 0, col0, jnp.where(lane == 1, col1, 0.0))
    rows = tile.T
    for j in range(tq // 256):
        ref[0, base + j] = rows[0:2, j * 256:(j + 1) * 256]


def _na_fwd(qkv, bias, T):
    rows = T // GRID_W
    tq = NA_RB * GRID_W
    tw = NA_WIN * GRID_W
    ts = NA_SUB * tq

    def body(q_ref, k_ref, v_ref, b_ref, o_ref, l_ref, lr_ref):
        rb = pl.program_id(1)
        lane = lax.broadcasted_iota(jnp.int32, (1, 128), 1)
        first = lane < NA_HEAD_DIM
        for u in range(NA_SUB):
            cls, ws = _na_sub(rb, u, rows)
            kw = k_ref[pl.ds(ws, tw), :]
            vw = v_ref[pl.ds(ws, tw), :]
            q = q_ref[u * tq:(u + 1) * tq, :]
            outs, lses = [], []
            for hh in range(2):
                msk = first if hh == 0 else jnp.logical_not(first)
                s = _nt(jnp.where(msk, q, jnp.zeros_like(q)), kw) + b_ref[hh, cls]
                m = jnp.max(s, axis=1, keepdims=True)
                p = jnp.exp(s - m)
                l = jnp.sum(p, axis=1, keepdims=True)
                outs.append(_nn(p.astype(_BF16), vw) / l)
                lses.append(m + jnp.log(l))
            o_ref[u * tq:(u + 1) * tq, :] = jnp.where(first, outs[0], outs[1])
            l_ref[u * tq:(u + 1) * tq, :] = jnp.where(first, lses[0], lses[1])
            _store_pair_rows(lr_ref, u * (tq // 256), lses[0], lses[1])

    blk = lambda off: pl.BlockSpec((ts, 128), lambda hp, rb: (rb, off + hp))
    whole = lambda off: pl.BlockSpec((T, 128), lambda hp, rb: (0, off + hp))
    return pl.pallas_call(
        body, name="na_fwd", grid=(NA_HEADS // 2, T // ts),
        in_specs=[blk(0), whole(4), whole(8),
                  pl.BlockSpec((2, 3, tq, tw), lambda hp, rb: (hp, 0, 0, 0))],
        out_specs=(blk(0), blk(0), _pair_rows_spec(ts)),
        out_shape=(jax.ShapeDtypeStruct((T, NA_W), _F32), jax.ShapeDtypeStruct((T, NA_W), _F32),
                   _pair_rows_shape(T)),
        compiler_params=_cparams(("arbitrary", "arbitrary")),
    )(qkv, qkv, qkv, bias)


def _na_bwd_q(qkv, bias, o, d_o, lse, T):
    rows = T // GRID_W
    tq = NA_RB * GRID_W
    tw = NA_WIN * GRID_W
    ts = NA_SUB * tq

    def body(q_ref, k_ref, v_ref, b_ref, o_ref, do_ref, l_ref, dq_ref, dl_ref, db_ref):
        rb = pl.program_id(1)
        lane = lax.broadcasted_iota(jnp.int32, (1, 128), 1)
        first = lane < NA_HEAD_DIM

        @pl.when(rb == 0)
        def _():
            db_ref[...] = jnp.zeros_like(db_ref)

        for u in range(NA_SUB):
            cls, ws = _na_sub(rb, u, rows)
            kw = k_ref[pl.ds(ws, tw), :]
            vw = v_ref[pl.ds(ws, tw), :]
            sl = slice(u * tq, (u + 1) * tq)
            q = q_ref[sl, :]
            d_ov = do_ref[sl, :]
            prod = d_ov.astype(_F32) * o_ref[sl, :]
            lse_v = l_ref[sl, :]
            dqs, dls = [], []
            for hh in range(2):
                msk = first if hh == 0 else jnp.logical_not(first)
                c0 = hh * NA_HEAD_DIM
                s = _nt(jnp.where(msk, q, jnp.zeros_like(q)), kw) + b_ref[hh, cls]
                p = jnp.exp(s - lse_v[:, c0:c0 + 1])
                dp = _nt(jnp.where(msk, d_ov, jnp.zeros_like(d_ov)), vw)
                delta = jnp.sum(jnp.where(msk, prod, 0.0), axis=1, keepdims=True)
                ds = p * (dp - delta)
                db_ref[hh, cls] += ds
                dqs.append(_nn(ds.astype(_BF16), kw) * (NA_HEAD_DIM ** -0.5))
                dls.append(delta)
            dq_ref[sl, :] = jnp.where(first, dqs[0], dqs[1]).astype(_BF16)
            _store_pair_rows(dl_ref, u * (tq // 256), dls[0], dls[1])

    blk = lambda off: pl.BlockSpec((ts, 128), lambda hp, rb: (rb, off + hp))
    whole = lambda off: pl.BlockSpec((T, 128), lambda hp, rb: (0, off + hp))
    tab = pl.BlockSpec((2, 3, tq, tw), lambda hp, rb: (hp, 0, 0, 0))
    return pl.pallas_call(
        body, name="na_bwd_q", grid=(NA_HEADS // 2, T // ts),
        in_specs=[blk(0), whole(4), whole(8), tab, blk(0), blk(0), blk(0)],
        out_specs=(blk(0), _pair_rows_spec(ts), tab),
        out_shape=(jax.ShapeDtypeStruct((T, NA_W), _BF16), _pair_rows_shape(T),
                   jax.ShapeDtypeStruct(bias.shape, _F32)),
        compiler_params=_cparams(("arbitrary", "arbitrary")),
    )(qkv, qkv, qkv, bias, o, d_o, lse)


def _na_bwd_kv(qkv, bias_t, d_o, lse_rows, delta_rows, T):
    rows = T // GRID_W
    tk = NA_RB * GRID_W
    tw = NA_WIN * GRID_W
    ts = NA_SUB * tk

    def body(k_ref, v_ref, q_ref, do_ref, b_ref, l_ref, dl_ref, dk_ref, dv_ref):
        kb = pl.program_id(1)
        lane = lax.broadcasted_iota(jnp.int32, (1, 128), 1)
        first = lane < NA_HEAD_DIM
        for u in range(NA_SUB):
            cls, ws = _na_sub(kb, u, rows, kv=True)
            qw = q_ref[pl.ds(ws, tw), :]
            dow = do_ref[pl.ds(ws, tw), :]
            w0 = ws // 256
            lw = jnp.concatenate([l_ref[0, w0 + i] for i in range(tw // 256)], axis=1)
            dw = jnp.concatenate([dl_ref[0, w0 + i] for i in range(tw // 256)], axis=1)
            sl = slice(u * tk, (u + 1) * tk)
            k = k_ref[sl, :]
            v = v_ref[sl, :]
            dks, dvs = [], []
            for hh in range(2):
                msk = first if hh == 0 else jnp.logical_not(first)
                st = _nt(jnp.where(msk, k, jnp.zeros_like(k)), qw) + b_ref[hh, cls]
                pt = jnp.exp(st - lw[hh:hh + 1, :])
                dvs.append(_nn(pt.astype(_BF16), dow))
                dpt = _nt(jnp.where(msk, v, jnp.zeros_like(v)), dow)
                dst = pt * (dpt - dw[hh:hh + 1, :])
                dks.append(_nn(dst.astype(_BF16), qw))
            dk_ref[sl, :] = jnp.where(first, dks[0], dks[1]).astype(_BF16)
            dv_ref[sl, :] = jnp.where(first, dvs[0], dvs[1]).astype(_BF16)

    blk = lambda off: pl.BlockSpec((ts, 128), lambda hp, kb: (kb, off + hp))
    whole = lambda off: pl.BlockSpec((T, 128), lambda hp, kb: (0, off + hp))
    rowspec = pl.BlockSpec((1,) + lse_rows.shape[1:], lambda hp, kb: (hp, 0, 0, 0))
    return pl.pallas_call(
        body, name="na_bwd_kv", grid=(NA_HEADS // 2, T // ts),
        in_specs=[blk(4), blk(8), whole(0), whole(0),
                  pl.BlockSpec((2, 5, tk, tw), lambda hp, kb: (hp, 0, 0, 0)),
                  rowspec, rowspec],
        out_specs=(blk(0), blk(0)),
        out_shape=(jax.ShapeDtypeStruct((T, NA_W), _BF16), jax.ShapeDtypeStruct((T, NA_W), _BF16)),
        compiler_params=_cparams(("arbitrary", "arbitrary")),
    )(qkv, qkv, qkv, d_o, bias_t, lse_rows, delta_rows)


def _rpb_grad(dbias, rows):
    tw = NA_WIN * GRID_W
    lanes = 16 * GRID_W
    offs = [int(win[0] - blk[0] + NA_KH - 1) for blk, win in _na_class_rows(rows)]

    def body(x_ref, g_ref):
        sub = lax.broadcasted_iota(jnp.int32, (NA_RB, 1), 0)
        qc = lax.broadcasted_iota(jnp.int32, (NA_RB * GRID_W, 1), 0) % GRID_W
        tot = jnp.zeros((NA_RB, lanes), _F32)
        for ci in range(3):
            xv = x_ref[0, ci]
            for bit in range(6):
                xv = jnp.where(((qc >> bit) & 1) == 1, pltpu.roll(xv, tw - (1 << bit), 1), xv)
            acc = pltpu.roll(jnp.sum(xv.reshape(NA_RB, GRID_W, tw), axis=1), NA_KW, 1)
            acc = jnp.concatenate([acc, jnp.zeros((NA_RB, lanes - tw), _F32)], axis=1)
            for a in range(NA_RB):
                tot = tot + jnp.where(sub == a, pltpu.roll(acc, (GRID_W * (offs[ci] - a)) % lanes, 1), 0.0)
        g_ref[0] = jnp.broadcast_to(jnp.sum(tot, axis=0, keepdims=True), (8, lanes))

    g = pl.pallas_call(
        body, name="rpb_grad", grid=(NA_HEADS,),
        in_specs=[pl.BlockSpec((1,) + dbias.shape[1:], lambda h: (h, 0, 0, 0))],
        out_specs=pl.BlockSpec((1, 8, lanes), lambda h: (h, 0, 0)),
        out_shape=jax.ShapeDtypeStruct((NA_HEADS, 8, lanes), _F32),
        compiler_params=_cparams(("arbitrary",)),
    )(dbias)
    return g[:, 0].reshape(NA_HEADS, 16, GRID_W)[:, :2 * NA_KH - 1, 1:2 * NA_KW]


def _halo_specs(tm, width, col_of, T, order):
    hb = tm // 8
    last = T // 8 - 1
    if order == "ij":
        cur = pl.BlockSpec((tm, width), lambda i, j: (i, col_of(j)))
        prev = pl.BlockSpec((8, width), lambda i, j: (jnp.maximum(i * hb - 1, 0), col_of(j)))
        nxt = pl.BlockSpec((8, width), lambda i, j: (jnp.minimum((i + 1) * hb, last), col_of(j)))
    else:
        cur = pl.BlockSpec((tm, width), lambda j, i: (i, col_of(j)))
        prev = pl.BlockSpec((8, width), lambda j, i: (jnp.maximum(i * hb - 1, 0), col_of(j)))
        nxt = pl.BlockSpec((8, width), lambda j, i: (jnp.minimum((i + 1) * hb, last), col_of(j)))
    return [prev, cur, nxt]


def _extend(prev_ref, cur_ref, next_ref, i, n_i):
    prev = jnp.where(i > 0, prev_ref[...], 0.0)
    nxt = jnp.where(i < n_i - 1, next_ref[...], 0.0)
    return jnp.concatenate([prev, cur_ref[...], nxt], axis=0)


def _conv_fwd(rest, conv_w, conv_b, T):
    tm = 512
    n_i = T // tm
    n = tm + 16

    def body(p_ref, c_ref, n_ref, w_ref, b_ref, o_ref):
        i = pl.program_id(0)
        ext = _extend(p_ref, c_ref, n_ref, i, n_i)
        acc = jnp.zeros((tm, 512), _F32) + b_ref[...]
        for j in range(CONV_W):
            acc = acc + w_ref[j:j + 1, :] * pltpu.roll(ext, (2 - j) % n, 0)[8:8 + tm]
        o_ref[...] = _silu(acc)

    return pl.pallas_call(
        body, name="conv_fwd", grid=(n_i, 2),
        in_specs=_halo_specs(tm, 512, lambda j: 1 + j, T, "ij")
        + [pl.BlockSpec((8, 512), lambda i, j: (0, j)), pl.BlockSpec((1, 512), lambda i, j: (0, j))],
        out_specs=pl.BlockSpec((tm, 512), lambda i, j: (i, j)),
        out_shape=jax.ShapeDtypeStruct((T, 2 * ML_W), _F32),
        compiler_params=_cparams(("arbitrary", "arbitrary")),
    )(rest, rest, rest, conv_w, conv_b)


def _conv_bwd(rest, conv_w, conv_b, da_f, da_b, T):
    tm = 512
    n_i = T // tm
    n = tm + 16

    def body(up, uc, un, fp, fc, fn, bp, bc, bn, w_ref, b_ref, du_ref, dw_ref):
        i = pl.program_id(1)
        ext_u = _extend(up, uc, un, i, n_i)
        ext_da = _extend(fp, fc, fn, i, n_i) + _extend(bp, bc, bn, i, n_i)
        shifted = [pltpu.roll(ext_u, (2 - j) % n, 0) for j in range(CONV_W)]
        pre = jnp.zeros((n, 512), _F32) + b_ref[...]
        for j in range(CONV_W):
            pre = pre + w_ref[j:j + 1, :] * shifted[j]
        gidx = i * tm - 8 + lax.broadcasted_iota(jnp.int32, (n, 1), 0)
        dpre = jnp.where((gidx >= 0) & (gidx < T), ext_da * _dsilu(pre), 0.0)
        du = jnp.zeros((tm, 512), _F32)
        for j in range(CONV_W):
            du = du + w_ref[j:j + 1, :] * pltpu.roll(dpre, (j - 2) % n, 0)[8:8 + tm]
        du_ref[...] = du.astype(_BF16)
        dpc = dpre[8:8 + tm]
        parts = [jnp.sum(dpc * shifted[j][8:8 + tm], axis=0, keepdims=True) for j in range(CONV_W)]
        parts.append(jnp.sum(dpc, axis=0, keepdims=True))
        parts.append(jnp.zeros((2, 512), _F32))
        upd = jnp.concatenate(parts, axis=0)

        @pl.when(i == 0)
        def _():
            dw_ref[...] = upd

        @pl.when(i > 0)
        def _():
            dw_ref[...] += upd

    return pl.pallas_call(
        body, name="conv_bwd", grid=(2, n_i),
        in_specs=_halo_specs(tm, 512, lambda j: 1 + j, T, "ji")
        + _halo_specs(tm, 512, lambda j: j, T, "ji") + _halo_specs(tm, 512, lambda j: j, T, "ji")
        + [pl.BlockSpec((8, 512), lambda j, i: (0, j)), pl.BlockSpec((1, 512), lambda j, i: (0, j))],
        out_specs=(pl.BlockSpec((tm, 512), lambda j, i: (i, j)), pl.BlockSpec((8, 512), lambda j, i: (0, j))),
        out_shape=(jax.ShapeDtypeStruct((T, 2 * ML_W), _BF16), jax.ShapeDtypeStruct((8, 2 * ML_W), _F32)),
        compiler_params=_cparams(("arbitrary", "arbitrary")),
    )(rest, rest, rest, da_f, da_f, da_f, da_b, da_b, da_b, conv_w, conv_b)


def _scan_rows(x, suffix):
    L = x.shape[0]
    row = lax.broadcasted_iota(jnp.int32, (L, 1), 0)
    step = 1
    while step < L:
        if suffix:
            x = x + jnp.where(row < L - step, pltpu.roll(x, L - step, 0), 0.0)
        else:
            x = x + jnp.where(row >= step, pltpu.roll(x, step, 0), 0.0)
        step *= 2
    return x


def _ml_gates(gt, rev):
    L = gt.shape[0]
    ri = lax.broadcasted_iota(jnp.int32, (L, L), 0)
    ci = lax.broadcasted_iota(jnp.int32, (L, L), 1)
    mask = (ci >= ri) if rev else (ci <= ri)
    lf = jnp.minimum(gt, 0.0) - jnp.log(1.0 + jnp.exp(-jnp.abs(gt)))
    b = _scan_rows(lf, suffix=rev)
    return mask, b, b.T, gt.T


def _ml_head_gates(gt, gates, head, rev):
    _, b, b_t, gt_t = gates
    ci = (8 if rev else 0) + head
    cf = ci + ML_HEADS
    last = 0 if rev else gt.shape[0] - 1
    return dict(icol=gt[:, ci:ci + 1], b_col=b[:, cf:cf + 1], b_row=b_t[cf:cf + 1, :],
                i_row=gt_t[ci:ci + 1, :], bl=b[last:last + 1, cf:cf + 1])


def _ml_chunk(q, k, v, hg, mask, C, n, m):
    icol, b_col, b_row, bl = hg["icol"], hg["b_col"], hg["b_row"], hg["bl"]
    dlog = jnp.where(mask, b_col - b_row + hg["i_row"], NEG)
    m_t = jnp.maximum(b_col + m, jnp.max(dlog, axis=1, keepdims=True))
    dm = jnp.exp(dlog - m_t)
    ks = k * (ML_HEAD_DIM ** -0.5)
    qb, kb, vb = q.astype(_BF16), ks.astype(_BF16), v.astype(_BF16)
    s = _nt(qb, kb) * dm
    g = jnp.exp(b_col + m - m_t)
    qc = _nt(qb, C.astype(_BF16))
    num = _nn(s.astype(_BF16), vb) + g * qc
    qn = jnp.sum(q * n, axis=1, keepdims=True)
    den = jnp.sum(s, axis=1, keepdims=True) + g * qn
    e_m = jnp.exp(-m_t)
    nrm = jnp.maximum(jnp.abs(den), e_m)
    h = num / nrm
    a_col = bl - b_col + icol
    m_new = jnp.maximum(bl + m, jnp.max(a_col, axis=0, keepdims=True))
    decay = jnp.exp(bl + m - m_new)
    w = jnp.exp(a_col - m_new)
    c_new = decay * C + _tn((w * v).astype(_BF16), kb)
    n_new = decay * n + jnp.sum(w * ks, axis=0, keepdims=True)
    aux = dict(dm=dm, ks=ks, qb=qb, kb=kb, vb=vb, s=s, g=g, qc=qc, qn=qn,
               den=den, e_m=e_m, nrm=nrm, decay=decay, w=w)
    return h, c_new, n_new, m_new, aux


def _mlstm_fwd(qk_act, rest, T, rev):
    tb = ML_CB * ML_CHUNK
    nblk = T // tb
    nc = T // ML_CHUNK
    bi = (lambda i: nblk - 1 - i) if rev else (lambda i: i)

    def body(q_ref, k_ref, v_ref, g_ref, h_ref, cs_ref, ns_ref, ms_ref, c_scr, n_scr, m_scr):
        @pl.when(pl.program_id(0) == 0)
        def _():
            c_scr[...] = jnp.zeros_like(c_scr)
            n_scr[...] = jnp.zeros_like(n_scr)
            m_scr[...] = jnp.zeros_like(m_scr)

        def step(j, carry):
            c = (ML_CB - 1 - j) if rev else j
            r0 = pl.multiple_of(c * ML_CHUNK, ML_CHUNK)
            gt = g_ref[pl.ds(r0, ML_CHUNK), :]
            gates = _ml_gates(gt, rev)
            for hd in range(ML_HEADS):
                cols = slice(hd * ML_HEAD_DIM, (hd + 1) * ML_HEAD_DIM)
                C = c_scr[hd]
                n = n_scr[hd:hd + 1, :]
                mrow = m_scr[hd:hd + 1, :]
                cs_ref[c, hd] = C
                ns_ref[c, hd:hd + 1, :] = n
                ms_ref[c, hd:hd + 1, :] = mrow
                h, c_new, n_new, m_new, _ = _ml_chunk(
                    q_ref[pl.ds(r0, ML_CHUNK), cols], k_ref[pl.ds(r0, ML_CHUNK), cols],
                    v_ref[pl.ds(r0, ML_CHUNK), cols], _ml_head_gates(gt, gates, hd, rev), gates[0],
                    C, n, mrow[:, 0:1])
                h_ref[pl.ds(r0, ML_CHUNK), cols] = h
                c_scr[hd] = c_new
                n_scr[hd:hd + 1, :] = n_new
                m_scr[hd:hd + 1, :] = jnp.broadcast_to(m_new, (1, 128))
            return carry

        lax.fori_loop(0, ML_CB, step, 0)

    return pl.pallas_call(
        body, name="mlstm_fwd_rev" if rev else "mlstm_fwd", grid=(nblk,),
        in_specs=[pl.BlockSpec((tb, ML_W), lambda i: (bi(i), 0)),
                  pl.BlockSpec((tb, ML_W), lambda i: (bi(i), 1)),
                  pl.BlockSpec((tb, ML_W), lambda i: (bi(i), 3)),
                  pl.BlockSpec((tb, 128), lambda i: (bi(i), GATE_COL // 128))],
        out_specs=(pl.BlockSpec((tb, ML_W), lambda i: (bi(i), 0)),
                   pl.BlockSpec((ML_CB, ML_HEADS, 128, 128), lambda i: (bi(i), 0, 0, 0)),
                   pl.BlockSpec((ML_CB, ML_HEADS, 128), lambda i: (bi(i), 0, 0)),
                   pl.BlockSpec((ML_CB, ML_HEADS, 128), lambda i: (bi(i), 0, 0))),
        out_shape=(jax.ShapeDtypeStruct((T, ML_W), _F32),
                   jax.ShapeDtypeStruct((nc, ML_HEADS, 128, 128), _F32),
                   jax.ShapeDtypeStruct((nc, ML_HEADS, 128), _F32),
                   jax.ShapeDtypeStruct((nc, ML_HEADS, 128), _F32)),
        scratch_shapes=[pltpu.VMEM((ML_HEADS, 128, 128), _F32), pltpu.VMEM((8, 128), _F32),
                        pltpu.VMEM((8, 128), _F32)],
        compiler_params=_cparams(("arbitrary",)),
    )(qk_act, qk_act, rest, rest)


def _mlstm_bwd(qk_act, rest, d_h, cs, ns, ms, T, rev):
    tb = ML_CB * ML_CHUNK
    nblk = T // tb
    bi = (lambda i: i) if rev else (lambda i: nblk - 1 - i)

    def body(q_ref, k_ref, v_ref, g_ref, dh_ref, cs_ref, ns_ref, ms_ref,
             dqk_ref, dv_ref, dg_ref, dc_scr, dn_scr):
        @pl.when(pl.program_id(0) == 0)
        def _():
            dc_scr[...] = jnp.zeros_like(dc_scr)
            dn_scr[...] = jnp.zeros_like(dn_scr)

        def step(j, carry):
            c = j if rev else (ML_CB - 1 - j)
            r0 = pl.multiple_of(c * ML_CHUNK, ML_CHUNK)
            gt = g_ref[pl.ds(r0, ML_CHUNK), :]
            gates = _ml_gates(gt, rev)
            mask = gates[0]
            lane = lax.broadcasted_iota(jnp.int32, (1, 128), 1)
            sub = lax.broadcasted_iota(jnp.int32, (128, 1), 0)
            db_t = jnp.zeros((ML_CHUNK, 128), _F32)
            da_t = jnp.zeros((ML_CHUNK, 128), _F32)
            cs_rows = jnp.zeros((128, ML_CHUNK), _F32)
            dbl_t = jnp.zeros((1, 128), _F32)
            for hd in range(ML_HEADS):
                cols = slice(hd * ML_HEAD_DIM, (hd + 1) * ML_HEAD_DIM)
                ci = (8 if rev else 0) + hd
                cf = ci + ML_HEADS
                q = q_ref[pl.ds(r0, ML_CHUNK), cols]
                k = k_ref[pl.ds(r0, ML_CHUNK), cols]
                v = v_ref[pl.ds(r0, ML_CHUNK), cols]
                C = cs_ref[c, hd]
                n = ns_ref[c, hd:hd + 1, :]
                m = ms_ref[c, hd:hd + 1, :][:, 0:1]
                dcn = dc_scr[hd]
                dnn = dn_scr[hd:hd + 1, :]
                h, _, _, _, a = _ml_chunk(q, k, v, _ml_head_gates(gt, gates, hd, rev), mask, C, n, m)
                d_hv = dh_ref[pl.ds(r0, ML_CHUNK), cols]
                g, s, w, ks = a["g"], a["s"], a["w"], a["ks"]
                qb, kb, vb = a["qb"], a["kb"], a["vb"]
                dnum = d_hv / a["nrm"]
                hdot = jnp.sum(d_hv * h, axis=1, keepdims=True)
                dden = jnp.where(jnp.abs(a["den"]) >= a["e_m"], -hdot / a["nrm"] * jnp.sign(a["den"]), 0.0)
                dnb = dnum.astype(_BF16)
                d_s = _nt(dnb, vb) + dden
                r = d_s * s
                dsqk = (d_s * a["dm"]).astype(_BF16)
                cb = C.astype(_BF16)
                dq = _nn(dsqk, kb) + g * _nn(dnb, cb) + (dden * g) * n
                dk = _tn(dsqk, qb)
                dv = _tn(s.astype(_BF16), dnb)
                dg = jnp.sum(dnum * a["qc"], axis=1, keepdims=True) + dden * a["qn"]
                db_col = jnp.sum(r, axis=1, keepdims=True) + dg * g
                cs_rows = cs_rows + jnp.where((sub == ci) | (sub == cf), jnp.sum(r, axis=0, keepdims=True), 0.0)
                dc_chunk = _tn((g * dnum).astype(_BF16), qb)
                dn_chunk = jnp.sum((dden * g) * q, axis=0, keepdims=True)
                dcb = dcn.astype(_BF16)
                vdc = _nn(vb, dcb)
                kdc = _nt(kb, dcb)
                dw = jnp.sum(vdc * ks, axis=1, keepdims=True) + jnp.sum(ks * dnn, axis=1, keepdims=True)
                dv = dv + w * kdc
                dk = dk + w * vdc + w * dnn
                da = dw * w
                ddecay = (jnp.sum(jnp.sum(dcn * C, axis=1, keepdims=True), axis=0, keepdims=True)
                          + jnp.sum(dnn * n, axis=1, keepdims=True))
                dbl = ddecay * a["decay"] + jnp.sum(da, axis=0, keepdims=True)
                db_t = db_t + jnp.where(lane == cf, db_col - da, 0.0)
                da_t = da_t + jnp.where(lane == ci, da, 0.0)
                dbl_t = dbl_t + jnp.where(lane == cf, dbl, 0.0)
                dc_scr[hd] = dc_chunk + a["decay"] * dcn
                dn_scr[hd:hd + 1, :] = dn_chunk + a["decay"] * dnn
                dqk_ref[pl.ds(r0, ML_CHUNK), cols] = dq
                dqk_ref[pl.ds(r0, ML_CHUNK), slice(ML_W + hd * 128, ML_W + (hd + 1) * 128)] = dk * (ML_HEAD_DIM ** -0.5)
                dv_ref[pl.ds(r0, ML_CHUNK), cols] = dv.astype(_BF16)
            lo = 8 if rev else 0
            is_i = (lane >= lo) & (lane < lo + ML_HEADS)
            is_f = (lane >= lo + ML_HEADS) & (lane < lo + 2 * ML_HEADS)
            cs_t = cs_rows.T
            db_all = db_t - jnp.where(is_f, cs_t, 0.0)
            dlf = _scan_rows(db_all, suffix=not rev) + dbl_t
            dg_ref[pl.ds(r0, ML_CHUNK), :] = (da_t + jnp.where(is_i, cs_t, 0.0)
                                               + jnp.where(is_f, dlf * _sigmoid(-gt), 0.0))
            return carry

        lax.fori_loop(0, ML_CB, step, 0)

    return pl.pallas_call(
        body, name="mlstm_bwd_rev" if rev else "mlstm_bwd", grid=(nblk,),
        in_specs=[pl.BlockSpec((tb, ML_W), lambda i: (bi(i), 0)),
                  pl.BlockSpec((tb, ML_W), lambda i: (bi(i), 1)),
                  pl.BlockSpec((tb, ML_W), lambda i: (bi(i), 3)),
                  pl.BlockSpec((tb, 128), lambda i: (bi(i), GATE_COL // 128)),
                  pl.BlockSpec((tb, ML_W), lambda i: (bi(i), 0)),
                  pl.BlockSpec((ML_CB, ML_HEADS, 128, 128), lambda i: (bi(i), 0, 0, 0)),
                  pl.BlockSpec((ML_CB, ML_HEADS, 128), lambda i: (bi(i), 0, 0)),
                  pl.BlockSpec((ML_CB, ML_HEADS, 128), lambda i: (bi(i), 0, 0))],
        out_specs=(pl.BlockSpec((tb, 2 * ML_W), lambda i: (bi(i), 0)),
                   pl.BlockSpec((tb, ML_W), lambda i: (bi(i), 0)),
                   pl.BlockSpec((tb, 128), lambda i: (bi(i), 0))),
        out_shape=(jax.ShapeDtypeStruct((T, 2 * ML_W), _F32), jax.ShapeDtypeStruct((T, ML_W), _BF16),
                   jax.ShapeDtypeStruct((T, 128), _F32)),
        scratch_shapes=[pltpu.VMEM((ML_HEADS, 128, 128), _F32), pltpu.VMEM((8, 128), _F32)],
        compiler_params=_cparams(("arbitrary",)),
    )(qk_act, qk_act, rest, rest, d_h, cs, ns, ms)


def _post(x, target, o_na, rest, h_f, h_b, gate, ml_norm_w, final_w, w_out_bf, T):
    tm = 256
    n_i = T // tm

    def body(x_ref, t_ref, o_ref, zna_ref, hf_ref, hb_ref, mo_ref, mz_ref, gate_ref, mw_ref, fw_ref, w_ref,
             dx1_ref, do_ref, dzna_ref, dh_ref, dmo_ref, dmz_ref, dwo_ref, vec_ref):
        i = pl.program_id(0)
        gate_v = gate_ref[...]
        fw = fw_ref[...]
        zna = zna_ref[...]
        o = o_ref[...]
        na_out = o * _silu(zna)
        hsum = hf_ref[...] + hb_ref[...]
        sg = _sigmoid(mo_ref[...])
        hm = hsum * sg
        mz = mz_ref[...]
        smz = _silu(mz)
        hn_l, rstd_l, ml_l = [], [], []
        for hd in range(ML_HEADS):
            cols = slice(hd * 128, (hd + 1) * 128)
            hh = hm[:, cols]
            mu = jnp.mean(hh, axis=-1, keepdims=True)
            var = jnp.mean(jnp.square(hh - mu), axis=-1, keepdims=True)
            rstd = lax.rsqrt(var + EPS)
            hn = (hh - mu) * rstd
            hn_l.append(hn)
            rstd_l.append(rstd)
            ml_l.append(hn * mw_ref[:, cols] * smz[:, cols])
        mix = jnp.concatenate([na_out] + ml_l, axis=1).astype(_BF16)
        y = _nn(mix, w_ref[...])
        x1 = x_ref[...] + gate_v * y
        r = lax.rsqrt(jnp.mean(x1 * x1, axis=-1, keepdims=True) + EPS)
        xhat = x1 * r
        out = xhat * fw
        err = out - t_ref[...]
        loss = 0.5 * jnp.sum(jnp.sum(err * err, axis=1, keepdims=True), axis=0, keepdims=True) / D_MODEL
        dout = err * (1.0 / D_MODEL)
        dfw = jnp.sum(dout * xhat, axis=0, keepdims=True)
        dxhat = dout * fw
        dx1 = r * (dxhat - xhat * jnp.mean(dxhat * xhat, axis=-1, keepdims=True))
        dx1_ref[...] = dx1
        dgate = jnp.sum(dx1 * y, axis=0, keepdims=True)
        dy = (dx1 * gate_v).astype(_BF16)
        dmix = _nt(dy, w_ref[...])
        dwo = _tn(mix, dy)
        dna = dmix[:, :NA_W]
        do_ref[...] = (dna * _silu(zna)).astype(_BF16)
        dzna_ref[...] = (dna * o * _dsilu(zna)).astype(_BF16)
        dmw_l = []
        for hd in range(ML_HEADS):
            cols = slice(hd * 128, (hd + 1) * 128)
            dml = dmix[:, NA_W + hd * 128:NA_W + (hd + 1) * 128]
            hn = hn_l[hd]
            mwv = mw_ref[:, cols]
            dmz_ref[:, cols] = (dml * hn * mwv * _dsilu(mz[:, cols])).astype(_BF16)
            dhn = dml * mwv * smz[:, cols]
            dmw_l.append(jnp.sum(dml * hn * smz[:, cols], axis=0, keepdims=True))
            dhm = rstd_l[hd] * (dhn - jnp.mean(dhn, axis=-1, keepdims=True)
                                - hn * jnp.mean(dhn * hn, axis=-1, keepdims=True))
            sgc = sg[:, cols]
            dh_ref[:, cols] = dhm * sgc
            dmo_ref[:, cols] = (dhm * hsum[:, cols] * sgc * (1.0 - sgc)).astype(_BF16)
        dmw = jnp.concatenate(dmw_l + [jnp.zeros((1, D_MODEL - ML_W), _F32)], axis=1)
        lane = lax.broadcasted_iota(jnp.int32, (1, D_MODEL), 1)
        vec = jnp.concatenate([dfw, dgate, dmw, jnp.where(lane == 0, loss, 0.0),
                               jnp.zeros((4, D_MODEL), _F32)], axis=0)

        @pl.when(i == 0)
        def _():
            dwo_ref[...] = dwo
            vec_ref[...] = vec

        @pl.when(i > 0)
        def _():
            dwo_ref[...] += dwo
            vec_ref[...] += vec

    tok = lambda w, j: pl.BlockSpec((tm, w), lambda i: (i, j))
    tok3 = pl.BlockSpec((None, tm, D_MODEL), lambda i: (0, i, 0))
    row = lambda w: pl.BlockSpec((1, w), lambda i: (0, 0))
    f32 = lambda w: jax.ShapeDtypeStruct((T, w), _F32)
    bf16 = lambda w: jax.ShapeDtypeStruct((T, w), _BF16)
    return pl.pallas_call(
        body, name="post", grid=(n_i,),
        in_specs=[tok3, tok3, tok(NA_W, 0), tok(NA_W, 0), tok(ML_W, 0), tok(ML_W, 0),
                  tok(ML_W, 4), tok(ML_W, 5), row(D_MODEL), row(ML_W), row(D_MODEL),
                  pl.BlockSpec((D_MODEL, D_MODEL), lambda i: (0, 0))],
        out_specs=(tok(D_MODEL, 0), tok(NA_W, 0), tok(NA_W, 0), tok(ML_W, 0), tok(ML_W, 0), tok(ML_W, 0),
                   pl.BlockSpec((D_MODEL, D_MODEL), lambda i: (0, 0)),
                   pl.BlockSpec((8, D_MODEL), lambda i: (0, 0))),
        out_shape=(f32(D_MODEL), bf16(NA_W), bf16(NA_W), f32(ML_W), bf16(ML_W),
                   bf16(ML_W), jax.ShapeDtypeStruct((D_MODEL, D_MODEL), _F32),
                   jax.ShapeDtypeStruct((8, D_MODEL), _F32)),
        compiler_params=_cparams(("arbitrary",)),
    )(x, target, o_na, rest, h_f, h_b, rest, rest, gate, ml_norm_w, final_w, w_out_bf)


def _section_specs(sections, tm):
    specs, args = [], []
    for _, width, parts in sections:
        for arr, cb in parts:
            specs.append(pl.BlockSpec((tm, width), functools.partial(lambda i, cb: (i, cb), cb=cb)))
            args.append(arr)
    return specs, args


def _section_values(sections, refs, dtype):
    vals, at = [], 0
    for _, _, parts in sections:
        v = refs[at][...]
        for r in refs[at + 1:at + len(parts)]:
            v = v.astype(_F32) + r[...].astype(_F32)
        at += len(parts)
        vals.append(v.astype(dtype))
    return vals


def _inproj_bwd_x(x, dx1, scale1p, norm_w, w_in_bf, sections, T):
    tm = 256
    sspecs, sargs = _section_specs(sections, tm)
    ns = len(sargs)

    def body(*refs):
        x_ref, dx1_ref, sc_ref, nw_ref, w_ref = refs[:5]
        srefs = refs[5:5 + ns]
        gx_ref, vec_ref = refs[5 + ns:]
        i = pl.program_id(0)
        vals = _section_values(sections, srefs, _BF16)
        dh = jnp.zeros((tm, D_MODEL), _F32)
        for (c0, width, _), val in zip(sections, vals):
            dh = dh + _nt(val, w_ref[:, c0:c0 + width])
        xv = x_ref[...]
        r = lax.rsqrt(jnp.mean(xv * xv, axis=-1, keepdims=True) + EPS)
        xhat = xv * r
        nw = nw_ref[...]
        dshift = jnp.sum(dh, axis=0, keepdims=True)
        dscale = jnp.sum(dh * xhat * nw, axis=0, keepdims=True)
        dhpre = dh * sc_ref[...]
        dnw = jnp.sum(dhpre * xhat, axis=0, keepdims=True)
        dxhat = dhpre * nw
        gx_ref[...] = dx1_ref[...] + r * (dxhat - xhat * jnp.mean(dxhat * xhat, axis=-1, keepdims=True))
        vec = jnp.concatenate([dshift, dscale, dnw, jnp.zeros((5, D_MODEL), _F32)], axis=0)

        @pl.when(i == 0)
        def _():
            vec_ref[...] = vec

        @pl.when(i > 0)
        def _():
            vec_ref[...] += vec

    row = pl.BlockSpec((1, D_MODEL), lambda i: (0, 0))
    tok = pl.BlockSpec((tm, D_MODEL), lambda i: (i, 0))
    tok3 = pl.BlockSpec((None, tm, D_MODEL), lambda i: (0, i, 0))
    return pl.pallas_call(
        body, name="inproj_bwd_x", grid=(T // tm,),
        in_specs=[tok3, tok, row, row, pl.BlockSpec((D_MODEL, IN_PAD), lambda i: (0, 0))] + sspecs,
        out_specs=(tok3, pl.BlockSpec((8, D_MODEL), lambda i: (0, 0))),
        out_shape=(jax.ShapeDtypeStruct((1, T, D_MODEL), _F32), jax.ShapeDtypeStruct((8, D_MODEL), _F32)),
        compiler_params=_cparams(("arbitrary",)),
    )(x, dx1, scale1p, norm_w, w_in_bf, *sargs)


def _inproj_bwd_w(h_t, sections, T):
    tm = 1024
    n_i = T // tm
    sspecs, sargs = _section_specs(sections, tm)
    ns = len(sargs)

    def body(*refs):
        h_ref = refs[0]
        srefs = refs[1:1 + ns]
        dw_ref, db_ref, acc, sem = refs[1 + ns:]
        i = pl.program_id(0)

        @pl.when(i == 0)
        def _():
            acc[...] = jnp.zeros_like(acc)
            db_ref[...] = jnp.zeros_like(db_ref)

        hv = h_ref[...]
        for (c0, width, _), v in zip(sections, _section_values(sections, srefs, _F32)):
            acc[:, c0:c0 + width] += _nn(hv, v.astype(_BF16))
            db_ref[0:1, c0:c0 + width] += jnp.sum(v, axis=0, keepdims=True)

        @pl.when(i == n_i - 1)
        def _():
            cp = pltpu.make_async_copy(acc, dw_ref, sem)
            cp.start()
            cp.wait()

    return pl.pallas_call(
        body, name="inproj_bwd_w", grid=(n_i,),
        in_specs=[pl.BlockSpec((D_MODEL, tm), lambda i: (0, i))] + sspecs,
        out_specs=(pl.BlockSpec(memory_space=pl.ANY), pl.BlockSpec((8, IN_PAD), lambda i: (0, 0))),
        out_shape=(jax.ShapeDtypeStruct((D_MODEL, IN_PAD), _F32), jax.ShapeDtypeStruct((8, IN_PAD), _F32)),
        scratch_shapes=[pltpu.VMEM((D_MODEL, IN_PAD), _F32), pltpu.SemaphoreType.DMA],
        compiler_params=_cparams(("arbitrary",)),
    )(h_t, *sargs)


def _adamw_math(w, g, m, v):
    m = ADAM_B1 * m + (1.0 - ADAM_B1) * g
    v = ADAM_B2 * v + (1.0 - ADAM_B2) * jnp.square(g)
    m_hat = m / (1.0 - ADAM_B1 ** ADAM_STEP)
    v_hat = v / (1.0 - ADAM_B2 ** ADAM_STEP)
    delta = -ADAM_LR * (m_hat / (jnp.sqrt(v_hat) + ADAM_EPS) + ADAM_WD * w)
    return delta, m, v


def _adamw_slots(w, m, v, slots, tr, name):
    R, C = w.shape

    def body(w_ref, m_ref, v_ref, s_ref, g_ref, d_ref, nm_ref, nv_ref):
        g = s_ref[0].astype(_F32)
        for k in range(1, N_DEV):
            g = g + s_ref[k].astype(_F32)
        g_ref[...] = g
        d_ref[...], nm_ref[...], nv_ref[...] = _adamw_math(w_ref[...], g, m_ref[...], v_ref[...])

    blk = pl.BlockSpec((tr, C), lambda i: (i, 0))
    return pl.pallas_call(
        body, name=name, grid=(R // tr,),
        in_specs=[blk, blk, blk, pl.BlockSpec((N_DEV, tr, C), lambda i: (0, i, 0))],
        out_specs=(blk, blk, blk, blk),
        out_shape=tuple(jax.ShapeDtypeStruct((R, C), _F32) for _ in range(4)),
        compiler_params=_cparams(("arbitrary",)),
    )(w, m, v, slots)


def _w_ada_update(c_all, dmod_my, w, m, v):
    def body(c_ref, d_ref, w_ref, m_ref, v_ref, g_ref, dl_ref, nm_ref, nv_ref):
        g = lax.dot_general(_silu(c_ref[...]), d_ref[...], (((0,), (0,)), ((), ())),
                            precision=_HI, preferred_element_type=_F32)
        g_ref[...] = g
        dl_ref[...], nm_ref[...], nv_ref[...] = _adamw_math(w_ref[...], g, m_ref[...], v_ref[...])

    return pl.pallas_call(
        body, name="w_ada_update",
        out_shape=tuple(jax.ShapeDtypeStruct(w.shape, _F32) for _ in range(4)),
        compiler_params=_cparams(),
    )(c_all, dmod_my, w, m, v)


_PACK = (("b_ada", 3072, 3072), ("norm_w", 1024, 1024), ("b_in", IN_W, IN_PAD), ("conv_w", 5120, 5120),
         ("conv_b", 1024, 1024), ("rpb", 3720, 3840), ("ml_norm_w", 512, 512), ("final_norm_w", 1024, 1024),
         ("loss", 1, 128))
_PACK_OFF = {}
_off = 0
for _name, _len, _pad in _PACK:
    _PACK_OFF[_name] = (_off, _len)
    _off += _pad
_PACK_LEN = _off


def _pack(parts):
    cols = []
    for name, length, pad in _PACK:
        vec = parts[name].reshape(-1).astype(_F32)
        cols.append(jnp.pad(vec, (0, pad - length)))
    return jnp.concatenate(cols).reshape(1, _PACK_LEN)


def _unpack(vec, name, shape):
    off, length = _PACK_OFF[name]
    return vec.reshape(-1)[off:off + length].reshape(shape)


def kernel(x, c, w_ada, b_ada, norm_w, w_in, b_in, conv_w, conv_b, rpb, ml_norm_w, w_out, final_norm_w, loss_target, m_w_ada, m_b_ada, m_norm_w, m_w_in, m_b_in, m_conv_w, m_conv_b, m_rpb, m_ml_norm_w, m_w_out, m_final_norm_w, v_w_ada, v_b_ada, v_norm_w, v_w_in, v_b_in, v_conv_w, v_conv_b, v_rpb, v_ml_norm_w, v_w_out, v_final_norm_w):
    T = x.shape[1]
    rows = T // GRID_W
    me = 4 * lax.axis_index("x") + 2 * lax.axis_index("y") + lax.axis_index("c")
    n_in = w_in.shape[2]
    n_ada = w_ada.shape[2]
    n_cw = conv_w.shape[2]
    n_wo = w_out.shape[1]

    g_w_in, g_w_out, g_conv_w, g_c = _exchange(
        [w_in[0].astype(_BF16), w_out[0].astype(_BF16), conv_w[0], c], [False] * 4, "gather_weights")
    w_in_full = g_w_in.transpose(1, 0, 2).reshape(D_MODEL, N_DEV * n_in)
    w_in_bf = jnp.pad(w_in_full, ((0, 0), (0, IN_PAD - IN_W)))
    b_in_pad = jnp.pad(b_in, ((0, 0), (0, IN_PAD - IN_W)))
    w_out_bf = g_w_out.reshape(N_DEV * n_wo, D_MODEL)
    conv_w_full = jnp.pad(g_conv_w.transpose(1, 0, 2).reshape(CONV_W, N_DEV * n_cw), ((0, 3), (0, 0)))
    c_all = g_c.reshape(N_DEV, D_MODEL)

    b_ada_my = lax.dynamic_slice(b_ada, (0, me * n_ada), (1, n_ada))
    (mod_slots,) = _exchange([_mod_part(c_all, w_ada[0], b_ada_my)], [False], "gather_mod")
    mod = lax.dynamic_index_in_dim(mod_slots, me, axis=1, keepdims=False).reshape(1, 3 * D_MODEL)
    shift, scale, gate = mod[:, :D_MODEL], mod[:, D_MODEL:2 * D_MODEL], mod[:, 2 * D_MODEL:]
    scale1p = 1.0 + scale

    qkv, rest, h_bf = _inproj_fwd(x, scale1p, shift, norm_w, w_in_bf, b_in_pad)
    bias, bias_t = _na_bias_tables(rpb[0], rows)
    o_na, lse, lse_rows = _na_fwd(qkv, bias, T)
    qk_act = _conv_fwd(rest, conv_w_full, conv_b, T)
    h_f, cs_f, ns_f, ms_f = _mlstm_fwd(qk_act, rest, T, False)
    h_b, cs_b, ns_b, ms_b = _mlstm_fwd(qk_act, rest, T, True)

    dx1, d_o, dz_na, d_h, d_mo, d_mz, dwo, pvec = _post(
        x, loss_target, o_na, rest, h_f, h_b, gate, ml_norm_w, final_norm_w.reshape(1, D_MODEL), w_out_bf, T)

    dq_na, delta_rows, dbias = _na_bwd_q(qkv, bias, o_na, d_o, lse, T)
    dk_na, dv_na = _na_bwd_kv(qkv, bias_t, d_o, lse_rows, delta_rows, T)
    d_rpb = _rpb_grad(dbias, rows)
    dqk_f, dv_f, dg_f = _mlstm_bwd(qk_act, rest, d_h, cs_f, ns_f, ms_f, T, False)
    dqk_b, dv_b, dg_b = _mlstm_bwd(qk_act, rest, d_h, cs_b, ns_b, ms_b, T, True)
    d_u, dconv = _conv_bwd(rest, conv_w_full, conv_b, dqk_f, dqk_b, T)

    sections = [(0, 512, [(dq_na, 0)]), (512, 512, [(dk_na, 0)]), (1024, 512, [(dv_na, 0)]),
                (1536, 512, [(dz_na, 0)]), (2048, 512, [(d_u, 0)]), (2560, 512, [(d_u, 1)]),
                (3072, 512, [(dv_f, 0), (dv_b, 0)]), (3584, 512, [(d_mo, 0)]), (4096, 512, [(d_mz, 0)]),
                (4608, 128, [(dg_f, 0), (dg_b, 0)])]
    grad_x, xvec = _inproj_bwd_x(x, dx1, scale1p, norm_w, w_in_bf, sections, T)
    dw_pad, db_pad = _inproj_bwd_w(h_bf, sections, T)
    dw_in = dw_pad[:, :IN_W]
    db_in = db_pad[0, :IN_W]

    small = _pack({
        "b_ada": jnp.concatenate([xvec[0], xvec[1], pvec[1]]),
        "norm_w": xvec[2], "b_in": db_in, "conv_w": dconv[:CONV_W], "conv_b": dconv[CONV_W],
        "rpb": d_rpb, "ml_norm_w": pvec[2, :ML_W], "final_norm_w": pvec[0], "loss": pvec[3, :1]})
    s_small, s_w_in, s_w_out = _exchange(
        [small, dw_in.astype(_BF16).reshape(D_MODEL, N_DEV, n_in).transpose(1, 0, 2),
         dwo.astype(_BF16).reshape(N_DEV, n_wo, D_MODEL)],
        [False, True, True], "exchange_grads")

    g_w_in_s, d_w_in, nm_w_in, nv_w_in = _adamw_slots(w_in[0], m_w_in[0], v_w_in[0], s_w_in, 128, "adamw_w_in")
    g_w_out_s, d_w_out, nm_w_out, nv_w_out = _adamw_slots(w_out[0], m_w_out[0], v_w_out[0], s_w_out, n_wo,
                                                          "adamw_w_out")
    dmod_all = s_small[:, 0, :3 * D_MODEL]
    dmod_my = lax.dynamic_slice(dmod_all, (0, me * n_ada), (N_DEV, n_ada))
    g_w_ada, d_w_ada, nm_w_ada, nv_w_ada = _w_ada_update(c_all, dmod_my, w_ada[0], m_w_ada[0], v_w_ada[0])

    def embed(shard):
        return lax.dynamic_update_slice(jnp.zeros((CONV_W, N_DEV * n_cw), _F32), shard[0], (0, me * n_cw))

    zero1 = jnp.zeros((1,), _F32)
    packed = lambda b_a, n_w, b_i, c_w, c_b, rp, mn, fn: _pack({
        "b_ada": b_a, "norm_w": n_w, "b_in": b_i, "conv_w": embed(c_w), "conv_b": c_b, "rpb": rp,
        "ml_norm_w": mn, "final_norm_w": fn, "loss": zero1})
    pw = packed(b_ada, norm_w, b_in, conv_w, conv_b, rpb, ml_norm_w, final_norm_w)
    pm = packed(m_b_ada, m_norm_w, m_b_in, m_conv_w, m_conv_b, m_rpb, m_ml_norm_w, m_final_norm_w)
    pv = packed(v_b_ada, v_norm_w, v_b_in, v_conv_w, v_conv_b, v_rpb, v_ml_norm_w, v_final_norm_w)
    sg, sd, sm, sv = _adamw_slots(pw, pm, pv, s_small, 1, "adamw_small")

    def small_outs(vec):
        cw = lax.dynamic_slice(_unpack(vec, "conv_w", (CONV_W, N_DEV * n_cw)), (0, me * n_cw), (CONV_W, n_cw))
        return dict(b_ada=_unpack(vec, "b_ada", b_ada.shape), norm_w=_unpack(vec, "norm_w", norm_w.shape),
                    b_in=_unpack(vec, "b_in", b_in.shape), conv_w=cw[None],
                    conv_b=_unpack(vec, "conv_b", conv_b.shape), rpb=_unpack(vec, "rpb", rpb.shape),
                    ml_norm_w=_unpack(vec, "ml_norm_w", ml_norm_w.shape),
                    final_norm_w=_unpack(vec, "final_norm_w", final_norm_w.shape))

    loss = _unpack(sg, "loss", ())
    order = ("w_ada", "b_ada", "norm_w", "w_in", "b_in", "conv_w", "conv_b", "rpb", "ml_norm_w", "w_out",
             "final_norm_w")
    outs = []
    for vec, big in ((sg, (g_w_ada, g_w_in_s, g_w_out_s)), (sd, (d_w_ada, d_w_in, d_w_out)),
                     (sm, (nm_w_ada, nm_w_in, nm_w_out)), (sv, (nv_w_ada, nv_w_in, nv_w_out))):
        group = small_outs(vec)
        group.update(w_ada=big[0][None], w_in=big[1][None], w_out=big[2][None])
        outs.extend(group[name] for name in order)
    return (loss, grad_x, *outs)
```

```python
import functools

import numpy as np
import jax
import jax.numpy as jnp
from jax import lax
from jax.experimental import pallas as pl
from jax.experimental.pallas import tpu as pltpu

N_DEV = 8
D_MODEL = 1024
GRID_W = 64
NA_HEADS = 8
NA_HEAD_DIM = 64
NA_KH = 8
NA_KW = 16
NA_W = 512
ML_HEADS = 4
ML_HEAD_DIM = 128
ML_W = 512
ML_CHUNK = 512
CONV_W = 5
EPS = 1e-6
IN_W = 4624
IN_PAD = 4736
REST_W = IN_PAD - 3 * NA_W
GATE_COL = 3072
NEG = -1e30
NA_RB = 4
NA_WIN = 12
NA_SUB = 2
ML_CB = 1
ADAM_LR = 0.001
ADAM_B1 = 0.9
ADAM_B2 = 0.999
ADAM_EPS = 1e-08
ADAM_WD = 0.01
ADAM_STEP = 10
VMEM_LIMIT = 56 * 1024 * 1024

_F32 = jnp.float32
_BF16 = jnp.bfloat16
_HI = lax.Precision.HIGHEST


def _cparams(sem=None):
    return pltpu.CompilerParams(dimension_semantics=sem, vmem_limit_bytes=VMEM_LIMIT)


def _nt(a, b):
    return lax.dot_general(a, b, (((1,), (1,)), ((), ())), preferred_element_type=_F32)


def _tn(a, b):
    return lax.dot_general(a, b, (((0,), (0,)), ((), ())), preferred_element_type=_F32)


def _nn(a, b):
    return jnp.dot(a, b, preferred_element_type=_F32)


def _sigmoid(x):
    return 1.0 / (1.0 + jnp.exp(-x))


def _silu(x):
    return x * _sigmoid(x)


def _dsilu(x):
    s = _sigmoid(x)
    return s * (1.0 + x * (1.0 - s))


def _exchange(arrs, scatter, name):
    n = len(arrs)
    out_shape = []
    for a, sc in zip(arrs, scatter):
        blk = a.shape[1:] if sc else a.shape
        out_shape.append(jax.ShapeDtypeStruct((N_DEV,) + tuple(blk), a.dtype))

    def body(*refs):
        ins = refs[:n]
        outs = refs[n:2 * n]
        send_sems, recv_sems, local_sems = refs[2 * n:]
        x, y, c = lax.axis_index("x"), lax.axis_index("y"), lax.axis_index("c")
        me = 4 * x + 2 * y + c
        local, sends, recvs = [], [], []
        for a in range(n):
            own = ins[a].at[me] if scatter[a] else ins[a]
            cp = pltpu.make_async_copy(own, outs[a].at[me], local_sems.at[a])
            cp.start()
            local.append(cp)
            for k in range(1, N_DEV):
                px = 1 - x if k & 4 else x
                py = 1 - y if k & 2 else y
                pc = 1 - c if k & 1 else c
                p = 4 * px + 2 * py + pc
                src = ins[a].at[p] if scatter[a] else ins[a]
                snd = pltpu.make_async_remote_copy(
                    src_ref=src, dst_ref=outs[a].at[me],
                    send_sem=send_sems.at[a, k - 1], recv_sem=recv_sems.at[a, k - 1],
                    device_id=(px, py, pc), device_id_type=pl.DeviceIdType.MESH)
                snd.start()
                sends.append(snd)
                rcv = pltpu.make_async_remote_copy(
                    src_ref=src, dst_ref=outs[a].at[p],
                    send_sem=send_sems.at[a, k - 1], recv_sem=recv_sems.at[a, k - 1],
                    device_id=(px, py, pc), device_id_type=pl.DeviceIdType.MESH)
                recvs.append(rcv)
        for rcv in recvs:
            rcv.wait_recv()
        for snd in sends:
            snd.wait_send()
        for cp in local:
            cp.wait()

    any_spec = pl.BlockSpec(memory_space=pl.ANY)
    res = pl.pallas_call(
        body, name=name, out_shape=tuple(out_shape),
        in_specs=[any_spec] * n, out_specs=tuple([any_spec] * n),
        scratch_shapes=[pltpu.SemaphoreType.DMA((n, N_DEV - 1)),
                        pltpu.SemaphoreType.DMA((n, N_DEV - 1)),
                        pltpu.SemaphoreType.DMA((n,))],
    )(*arrs)
    return list(res)


def _peer(k):
    x, y, c = lax.axis_index("x"), lax.axis_index("y"), lax.axis_index("c")
    px = 1 - x if k & 4 else x
    py = 1 - y if k & 2 else y
    pc = 1 - c if k & 1 else c
    return (px, py, pc), 4 * px + 2 * py + pc, 4 * x + 2 * y + c


def _scatter_copy(srcs, lands, send_sems, recv_sems, a, k, receive):
    dev, p, me = _peer(k)
    return pltpu.make_async_remote_copy(
        src_ref=srcs[a].at[p], dst_ref=lands[a].at[p if receive else me],
        send_sem=send_sems[a * (N_DEV - 1) + k - 1], recv_sem=recv_sems[a * (N_DEV - 1) + k - 1],
        device_id=dev, device_id_type=pl.DeviceIdType.MESH)


def _scatter_start(arrs, name):
    n = len(arrs)
    ns = n * (N_DEV - 1)
    hbm = pl.BlockSpec(memory_space=pltpu.HBM)
    sem = pl.BlockSpec(memory_space=pltpu.SEMAPHORE)

    def body(*refs):
        srcs, lands = refs[:n], refs[n:2 * n]
        send_sems, recv_sems = refs[2 * n:2 * n + ns], refs[2 * n + ns:2 * n + 2 * ns]
        token = refs[-1]
        for a in range(n):
            for k in range(1, N_DEV):
                _scatter_copy(srcs, lands, send_sems, recv_sems, a, k, False).start()
        token[...] = jnp.zeros_like(token)

    buffers = [pltpu.HBM(a.shape, a.dtype) for a in arrs]
    sems = [pltpu.SemaphoreType.DMA(()) for _ in range(2 * ns)]
    res = pl.pallas_call(
        body, name=name,
        out_shape=(*sems, *buffers, *buffers, jax.ShapeDtypeStruct((8, 128), _F32)),
        in_specs=[hbm] * (2 * n),
        out_specs=(*([sem] * (2 * ns)), *([hbm] * (2 * n)), pl.BlockSpec(memory_space=pltpu.VMEM)),
        input_output_aliases={i: 2 * ns + i for i in range(2 * n)},
        compiler_params=pltpu.CompilerParams(has_side_effects=pltpu.SideEffectType.DATAFLOW_SIDE_EFFECTING),
    )(*[pltpu.with_memory_space_constraint(a, pltpu.HBM) for a in arrs],
      *[pltpu.with_memory_space_constraint(jnp.zeros(a.shape, a.dtype), pltpu.HBM) for a in arrs])
    res = list(res)
    return (res[:ns], res[ns:2 * ns], res[2 * ns:2 * ns + n], res[2 * ns + n:2 * ns + 2 * n], res[-1])


def _scatter_wait(started, after, name):
    send_sems, recv_sems, srcs, lands, _ = started
    n = len(srcs)
    ns = len(send_sems)
    hbm = pl.BlockSpec(memory_space=pltpu.HBM)
    sem = pl.BlockSpec(memory_space=pltpu.SEMAPHORE)

    def body(*refs):
        src_refs, land_refs = refs[:n], refs[n:2 * n]
        s_sems, r_sems = refs[2 * n:2 * n + ns], refs[2 * n + ns:2 * n + 2 * ns]
        for a in range(n):
            for k in range(1, N_DEV):
                _scatter_copy(src_refs, land_refs, s_sems, r_sems, a, k, False).wait_send()
                _scatter_copy(src_refs, land_refs, s_sems, r_sems, a, k, True).wait_recv()

    buffers = [pltpu.HBM(a.shape, a.dtype) for a in srcs]
    res = pl.pallas_call(
        body, name=name, out_shape=(*buffers, *buffers),
        in_specs=[hbm] * (2 * n) + [sem] * (2 * ns) + [pl.BlockSpec(memory_space=pl.ANY)],
        out_specs=tuple([hbm] * (2 * n)),
        input_output_aliases={i: i for i in range(2 * n)},
        compiler_params=pltpu.CompilerParams(has_side_effects=pltpu.SideEffectType.DATAFLOW_SIDE_EFFECTING),
    )(*srcs, *lands, *send_sems, *recv_sems, after)
    return list(res[:n]), list(res[n:])


def _mod_part(c_all, w_ada, b_my):
    def body(c_ref, w_ref, b_ref, o_ref):
        o_ref[...] = jnp.dot(_silu(c_ref[...]), w_ref[...], precision=_HI,
                             preferred_element_type=_F32) + b_ref[...]

    return pl.pallas_call(
        body, name="mod_part",
        out_shape=jax.ShapeDtypeStruct((N_DEV, w_ada.shape[1]), _F32),
        compiler_params=_cparams(),
    )(c_all, w_ada, b_my)


def _inproj_fwd(x, scale1p, shift, norm_w, w_in_bf, b_in_pad):
    T = x.shape[1]
    tm = 256
    n_q = 3 * NA_W

    def body(x_ref, sc_ref, sh_ref, nw_ref, w_ref, b_ref, qkv_ref, rest_ref, h_ref):
        xv = x_ref[...]
        r = lax.rsqrt(jnp.mean(xv * xv, axis=-1, keepdims=True) + EPS)
        h = xv * r * nw_ref[...] * sc_ref[...] + sh_ref[...]
        hb = h.astype(_BF16)
        h_ref[...] = h.T.astype(_BF16)
        for n0 in range(0, IN_PAD, 512):
            wd = min(512, IN_PAD - n0)
            acc = _nn(hb, w_ref[:, n0:n0 + wd]) + b_ref[:, n0:n0 + wd]
            if n0 == 0:
                acc = acc * (NA_HEAD_DIM ** -0.5)
            if n0 < n_q:
                qkv_ref[:, n0:n0 + wd] = acc.astype(_BF16)
            else:
                rest_ref[:, n0 - n_q:n0 - n_q + wd] = acc

    row = lambda w: pl.BlockSpec((1, w), lambda i: (0, 0))
    return pl.pallas_call(
        body, name="inproj_fwd", grid=(T // tm,),
        in_specs=[pl.BlockSpec((None, tm, D_MODEL), lambda i: (0, i, 0)), row(D_MODEL), row(D_MODEL), row(D_MODEL),
                  pl.BlockSpec((D_MODEL, IN_PAD), lambda i: (0, 0)), row(IN_PAD)],
        out_specs=(pl.BlockSpec((tm, n_q), lambda i: (i, 0)),
                   pl.BlockSpec((tm, REST_W), lambda i: (i, 0)),
                   pl.BlockSpec((D_MODEL, tm), lambda i: (0, i))),
        out_shape=(jax.ShapeDtypeStruct((T, n_q), _BF16),
                   jax.ShapeDtypeStruct((T, REST_W), _F32),
                   jax.ShapeDtypeStruct((D_MODEL, T), _BF16)),
        compiler_params=_cparams(("arbitrary",)),
    )(x, scale1p, shift, norm_w, w_in_bf, b_in_pad)


def _na_class_rows(rows, kv=False):
    nb = rows // NA_RB
    reps = (0, 1, 2, nb - 2, nb - 1) if kv else (0, min(1, nb - 1), nb - 1)
    out = []
    for rb in reps:
        ws = int(np.clip(NA_RB * rb - 4, 0, rows - NA_WIN))
        out.append((NA_RB * rb + np.arange(NA_RB), ws + np.arange(NA_WIN)))
    return out


def _na_pair_index(rows, qrows, krows):
    start = lambda r: np.clip(r - NA_KH // 2, 0, rows - NA_KH)
    col = np.arange(GRID_W)
    cstart = np.clip(col - NA_KW // 2, 0, GRID_W - NA_KW)
    dy = krows[None, :] - qrows[:, None] + NA_KH - 1
    vr = (krows[None, :] >= start(qrows)[:, None]) & (krows[None, :] < start(qrows)[:, None] + NA_KH)
    dx = np.clip(col[None, :] - col[:, None], -(NA_KW - 1), NA_KW - 1) + NA_KW - 1
    vc = (col[None, :] >= cstart[:, None]) & (col[None, :] < cstart[:, None] + NA_KW)
    nq, nk = len(qrows), len(krows)
    dy4 = np.broadcast_to(np.clip(dy, 0, 2 * NA_KH - 2)[:, None, :, None], (nq, GRID_W, nk, GRID_W))
    dx4 = np.broadcast_to(dx[None, :, None, :], (nq, GRID_W, nk, GRID_W))
    valid = vr[:, None, :, None] & vc[None, :, None, :]
    idx = (dy4 * (2 * NA_KW - 1) + dx4).reshape(nq * GRID_W, nk * GRID_W)
    return idx.astype(np.int32), valid.reshape(nq * GRID_W, nk * GRID_W), (dy, vr, dx, vc)


def _na_half_slabs(rpb):
    _, _, (_, _, dx, vc) = _na_pair_index(NA_WIN, np.arange(1), np.arange(1))
    xs, ys = np.meshgrid(np.arange(GRID_W), np.arange(GRID_W), indexing="ij")
    consts = []
    for trans in (False, True):
        qc, kc = (ys, xs) if trans else (xs, ys)
        for right in (False, True):
            pos = (xs * 128 + (GRID_W if right else 0) + ys).reshape(-1)
            oh = np.zeros((32, GRID_W * 128), np.float32)
            oh[dx[qc, kc].reshape(-1), pos] = 1.0
            col_neg = np.zeros((1, GRID_W * 128), np.float32)
            col_neg[0, pos] = np.where(vc[qc, kc].reshape(-1), 0.0, NEG)
            half = np.zeros((1, GRID_W * 128), np.float32)
            half[0, pos] = 1.0
            consts += [jnp.asarray(oh), jnp.asarray(col_neg), jnp.asarray(half)]
    row_neg = np.where(np.arange(NA_HEADS * 16) % 16 == 15, NEG, 0.0).astype(np.float32).reshape(-1, 1)
    rp = jnp.pad(rpb, ((0, 0), (0, 1), (0, 1))).reshape(NA_HEADS * 16, 32)

    def body(*refs):
        r_ref, rn_ref = refs[0], refs[1]
        for t in range(4):
            oh_ref, cn_ref, half_ref = refs[2 + 3 * t:5 + 3 * t]
            refs[14 + t][...] = (jnp.dot(r_ref[...], oh_ref[...], precision=_HI, preferred_element_type=_F32)
                                 + cn_ref[...] + rn_ref[...] * half_ref[...])

    outs = pl.pallas_call(
        body, name="na_half_slabs",
        out_shape=tuple(jax.ShapeDtypeStruct((NA_HEADS * 16, GRID_W * 128), _F32) for _ in range(4)),
        compiler_params=_cparams(),
    )(rp, jnp.asarray(row_neg), *consts)
    return [o.reshape(NA_HEADS, 16, GRID_W, 128) for o in outs]


def _na_bias_tables(rpb, rows):
    left, right, left_t, right_t = _na_half_slabs(rpb)
    didx, didx_t = [], []
    for blk, win in _na_class_rows(rows):
        _, _, (dy, vr, _, _) = _na_pair_index(rows, blk, win)
        didx.append(np.where(vr, dy, 15))
    for blk, win in _na_class_rows(rows, kv=True):
        _, _, (dy, vr, _, _) = _na_pair_index(rows, win, blk)
        didx_t.append(np.where(vr, dy, 15).T)

    def fill(ref, left_ref, right_ref, tabs):
        for ci, tab in enumerate(tabs):
            for a in range(NA_RB):
                for j in range(NA_WIN // 2):
                    ref[0, ci, a * GRID_W:(a + 1) * GRID_W, j * 128:(j + 1) * 128] = (
                        left_ref[0, int(tab[a, 2 * j])] + right_ref[0, int(tab[a, 2 * j + 1])])

    def body(l_ref, r_ref, lt_ref, rt_ref, b_ref, bt_ref):
        fill(b_ref, l_ref, r_ref, didx)
        fill(bt_ref, lt_ref, rt_ref, didx_t)

    slab = pl.BlockSpec((1, 16, GRID_W, 128), lambda h: (h, 0, 0, 0))
    tab = lambda n: pl.BlockSpec((1, n, NA_RB * GRID_W, NA_WIN * GRID_W), lambda h: (h, 0, 0, 0))
    shape = lambda n: jax.ShapeDtypeStruct((NA_HEADS, n, NA_RB * GRID_W, NA_WIN * GRID_W), _F32)
    return pl.pallas_call(
        body, name="na_tables", grid=(NA_HEADS,),
        in_specs=[slab] * 4, out_specs=(tab(3), tab(5)), out_shape=(shape(3), shape(5)),
        compiler_params=_cparams(("arbitrary",)),
    )(left, right, left_t, right_t)


def _na_cls(i, nb, kv):
    if kv:
        return jnp.where(i < 2, i, jnp.where(i >= nb - 2, i - (nb - 5), 2))
    return jnp.where(i == 0, 0, jnp.where(i == nb - 1, 2, 1))


def _na_sub(rb, u, rows, kv=False):
    sb = NA_SUB * rb + u
    cls = _na_cls(sb, rows // NA_RB, kv)
    ws = pl.multiple_of(jnp.clip(NA_RB * sb - 4, 0, rows - NA_WIN) * GRID_W, 256)
    return cls, ws


def _pair_rows_shape(T):
    return jax.ShapeDtypeStruct((NA_HEADS // 2, T // 256, 2, 256), _F32)


def _pair_rows_spec(ts):
    return pl.BlockSpec((1, ts // 256, 2, 256), lambda hp, rb: (hp, rb, 0, 0))


def _store_pair_rows(ref, base, col0, col1):
    tq = col0.shape[0]
    lane = lax.broadcasted_iota(jnp.int32, (1, 128), 1)
    tile = jnp.where(lane == 0, col0, jnp.where(lane == 1, col1, 0.0))
    rows = tile.T
    for j in range(tq // 256):
        ref[0, base + j] = rows[0:2, j * 256:(j + 1) * 256]


def _na_fwd(qkv, bias, T):
    rows = T // GRID_W
    tq = NA_RB * GRID_W
    tw = NA_WIN * GRID_W
    ts = NA_SUB * tq

    def body(q_ref, k_ref, v_ref, b_ref, o_ref, l_ref, lr_ref):
        rb = pl.program_id(1)
        lane = lax.broadcasted_iota(jnp.int32, (1, 128), 1)
        first = lane < NA_HEAD_DIM
        for u in range(NA_SUB):
            cls, ws = _na_sub(rb, u, rows)
            kw = k_ref[pl.ds(ws, tw), :]
            vw = v_ref[pl.ds(ws, tw), :]
            q = q_ref[u * tq:(u + 1) * tq, :]
            outs, lses = [], []
            for hh in range(2):
                msk = first if hh == 0 else jnp.logical_not(first)
                s = _nt(jnp.where(msk, q, jnp.zeros_like(q)), kw) + b_ref[hh, cls]
                m = jnp.max(s, axis=1, keepdims=True)
                p = jnp.exp(s - m)
                l = jnp.sum(p, axis=1, keepdims=True)
                outs.append(_nn(p.astype(_BF16), vw) / l)
                lses.append(m + jnp.log(l))
            o_ref[u * tq:(u + 1) * tq, :] = jnp.where(first, outs[0], outs[1])
            l_ref[u * tq:(u + 1) * tq, :] = jnp.where(first, lses[0], lses[1])
            _store_pair_rows(lr_ref, u * (tq // 256), lses[0], lses[1])

    blk = lambda off: pl.BlockSpec((ts, 128), lambda hp, rb: (rb, off + hp))
    whole = lambda off: pl.BlockSpec((T, 128), lambda hp, rb: (0, off + hp))
    return pl.pallas_call(
        body, name="na_fwd", grid=(NA_HEADS // 2, T // ts),
        in_specs=[blk(0), whole(4), whole(8),
                  pl.BlockSpec((2, 3, tq, tw), lambda hp, rb: (hp, 0, 0, 0))],
        out_specs=(blk(0), blk(0), _pair_rows_spec(ts)),
        out_shape=(jax.ShapeDtypeStruct((T, NA_W), _F32), jax.ShapeDtypeStruct((T, NA_W), _F32),
                   _pair_rows_shape(T)),
        compiler_params=_cparams(("arbitrary", "arbitrary")),
    )(qkv, qkv, qkv, bias)


def _na_bwd_q(qkv, bias, o, d_o, lse, T):
    rows = T // GRID_W
    tq = NA_RB * GRID_W
    tw = NA_WIN * GRID_W
    ts = NA_SUB * tq

    def body(q_ref, k_ref, v_ref, b_ref, o_ref, do_ref, l_ref, dq_ref, dl_ref, db_ref):
        rb = pl.program_id(1)
        lane = lax.broadcasted_iota(jnp.int32, (1, 128), 1)
        first = lane < NA_HEAD_DIM

        @pl.when(rb == 0)
        def _():
            db_ref[...] = jnp.zeros_like(db_ref)

        for u in range(NA_SUB):
            cls, ws = _na_sub(rb, u, rows)
            kw = k_ref[pl.ds(ws, tw), :]
            vw = v_ref[pl.ds(ws, tw), :]
            sl = slice(u * tq, (u + 1) * tq)
            q = q_ref[sl, :]
            d_ov = do_ref[sl, :]
            prod = d_ov.astype(_F32) * o_ref[sl, :]
            lse_v = l_ref[sl, :]
            dqs, dls = [], []
            for hh in range(2):
                msk = first if hh == 0 else jnp.logical_not(first)
                c0 = hh * NA_HEAD_DIM
                s = _nt(jnp.where(msk, q, jnp.zeros_like(q)), kw) + b_ref[hh, cls]
                p = jnp.exp(s - lse_v[:, c0:c0 + 1])
                dp = _nt(jnp.where(msk, d_ov, jnp.zeros_like(d_ov)), vw)
                delta = jnp.sum(jnp.where(msk, prod, 0.0), axis=1, keepdims=True)
                ds = p * (dp - delta)
                db_ref[hh, cls] += ds
                dqs.append(_nn(ds.astype(_BF16), kw) * (NA_HEAD_DIM ** -0.5))
                dls.append(delta)
            dq_ref[sl, :] = jnp.where(first, dqs[0], dqs[1]).astype(_BF16)
            _store_pair_rows(dl_ref, u * (tq // 256), dls[0], dls[1])

    blk = lambda off: pl.BlockSpec((ts, 128), lambda hp, rb: (rb, off + hp))
    whole = lambda off: pl.BlockSpec((T, 128), lambda hp, rb: (0, off + hp))
    tab = pl.BlockSpec((2, 3, tq, tw), lambda hp, rb: (hp, 0, 0, 0))
    return pl.pallas_call(
        body, name="na_bwd_q", grid=(NA_HEADS // 2, T // ts),
        in_specs=[blk(0), whole(4), whole(8), tab, blk(0), blk(0), blk(0)],
        out_specs=(blk(0), _pair_rows_spec(ts), tab),
        out_shape=(jax.ShapeDtypeStruct((T, NA_W), _BF16), _pair_rows_shape(T),
                   jax.ShapeDtypeStruct(bias.shape, _F32)),
        compiler_params=_cparams(("arbitrary", "arbitrary")),
    )(qkv, qkv, qkv, bias, o, d_o, lse)


def _na_bwd_kv(qkv, bias_t, d_o, lse_rows, delta_rows, T):
    rows = T // GRID_W
    tk = NA_RB * GRID_W
    tw = NA_WIN * GRID_W
    ts = NA_SUB * tk

    def body(k_ref, v_ref, q_ref, do_ref, b_ref, l_ref, dl_ref, dk_ref, dv_ref):
        kb = pl.program_id(1)
        lane = lax.broadcasted_iota(jnp.int32, (1, 128), 1)
        first = lane < NA_HEAD_DIM
        for u in range(NA_SUB):
            cls, ws = _na_sub(kb, u, rows, kv=True)
            qw = q_ref[pl.ds(ws, tw), :]
            dow = do_ref[pl.ds(ws, tw), :]
            w0 = ws // 256
            lw = jnp.concatenate([l_ref[0, w0 + i] for i in range(tw // 256)], axis=1)
            dw = jnp.concatenate([dl_ref[0, w0 + i] for i in range(tw // 256)], axis=1)
            sl = slice(u * tk, (u + 1) * tk)
            k = k_ref[sl, :]
            v = v_ref[sl, :]
            dks, dvs = [], []
            for hh in range(2):
                msk = first if hh == 0 else jnp.logical_not(first)
                st = _nt(jnp.where(msk, k, jnp.zeros_like(k)), qw) + b_ref[hh, cls]
                pt = jnp.exp(st - lw[hh:hh + 1, :])
                dvs.append(_nn(pt.astype(_BF16), dow))
                dpt = _nt(jnp.where(msk, v, jnp.zeros_like(v)), dow)
                dst = pt * (dpt - dw[hh:hh + 1, :])
                dks.append(_nn(dst.astype(_BF16), qw))
            dk_ref[sl, :] = jnp.where(first, dks[0], dks[1]).astype(_BF16)
            dv_ref[sl, :] = jnp.where(first, dvs[0], dvs[1]).astype(_BF16)

    blk = lambda off: pl.BlockSpec((ts, 128), lambda hp, kb: (kb, off + hp))
    whole = lambda off: pl.BlockSpec((T, 128), lambda hp, kb: (0, off + hp))
    rowspec = pl.BlockSpec((1,) + lse_rows.shape[1:], lambda hp, kb: (hp, 0, 0, 0))
    return pl.pallas_call(
        body, name="na_bwd_kv", grid=(NA_HEADS // 2, T // ts),
        in_specs=[blk(4), blk(8), whole(0), whole(0),
                  pl.BlockSpec((2, 5, tk, tw), lambda hp, kb: (hp, 0, 0, 0)),
                  rowspec, rowspec],
        out_specs=(blk(0), blk(0)),
        out_shape=(jax.ShapeDtypeStruct((T, NA_W), _BF16), jax.ShapeDtypeStruct((T, NA_W), _BF16)),
        compiler_params=_cparams(("arbitrary", "arbitrary")),
    )(qkv, qkv, qkv, d_o, bias_t, lse_rows, delta_rows)


def _rpb_grad(dbias, rows):
    tw = NA_WIN * GRID_W
    lanes = 16 * GRID_W
    offs = [int(win[0] - blk[0] + NA_KH - 1) for blk, win in _na_class_rows(rows)]

    def body(x_ref, g_ref):
        sub = lax.broadcasted_iota(jnp.int32, (NA_RB, 1), 0)
        qc = lax.broadcasted_iota(jnp.int32, (NA_RB * GRID_W, 1), 0) % GRID_W
        tot = jnp.zeros((NA_RB, lanes), _F32)
        for ci in range(3):
            xv = x_ref[0, ci]
            for bit in range(6):
                xv = jnp.where(((qc >> bit) & 1) == 1, pltpu.roll(xv, tw - (1 << bit), 1), xv)
            acc = pltpu.roll(jnp.sum(xv.reshape(NA_RB, GRID_W, tw), axis=1), NA_KW, 1)
            acc = jnp.concatenate([acc, jnp.zeros((NA_RB, lanes - tw), _F32)], axis=1)
            for a in range(NA_RB):
                tot = tot + jnp.where(sub == a, pltpu.roll(acc, (GRID_W * (offs[ci] - a)) % lanes, 1), 0.0)
        g_ref[0] = jnp.broadcast_to(jnp.sum(tot, axis=0, keepdims=True), (8, lanes))

    g = pl.pallas_call(
        body, name="rpb_grad", grid=(NA_HEADS,),
        in_specs=[pl.BlockSpec((1,) + dbias.shape[1:], lambda h: (h, 0, 0, 0))],
        out_specs=pl.BlockSpec((1, 8, lanes), lambda h: (h, 0, 0)),
        out_shape=jax.ShapeDtypeStruct((NA_HEADS, 8, lanes), _F32),
        compiler_params=_cparams(("arbitrary",)),
    )(dbias)
    return g[:, 0].reshape(NA_HEADS, 16, GRID_W)[:, :2 * NA_KH - 1, 1:2 * NA_KW]


def _halo_specs(tm, width, col_of, T, order):
    hb = tm // 8
    last = T // 8 - 1
    if order == "ij":
        cur = pl.BlockSpec((tm, width), lambda i, j: (i, col_of(j)))
        prev = pl.BlockSpec((8, width), lambda i, j: (jnp.maximum(i * hb - 1, 0), col_of(j)))
        nxt = pl.BlockSpec((8, width), lambda i, j: (jnp.minimum((i + 1) * hb, last), col_of(j)))
    else:
        cur = pl.BlockSpec((tm, width), lambda j, i: (i, col_of(j)))
        prev = pl.BlockSpec((8, width), lambda j, i: (jnp.maximum(i * hb - 1, 0), col_of(j)))
        nxt = pl.BlockSpec((8, width), lambda j, i: (jnp.minimum((i + 1) * hb, last), col_of(j)))
    return [prev, cur, nxt]


def _extend(prev_ref, cur_ref, next_ref, i, n_i):
    prev = jnp.where(i > 0, prev_ref[...], 0.0)
    nxt = jnp.where(i < n_i - 1, next_ref[...], 0.0)
    return jnp.concatenate([prev, cur_ref[...], nxt], axis=0)


def _conv_fwd(rest, conv_w, conv_b, T):
    tm = 512
    n_i = T // tm
    n = tm + 16

    def body(p_ref, c_ref, n_ref, w_ref, b_ref, o_ref):
        i = pl.program_id(0)
        ext = _extend(p_ref, c_ref, n_ref, i, n_i)
        acc = jnp.zeros((tm, 512), _F32) + b_ref[...]
        for j in range(CONV_W):
            acc = acc + w_ref[j:j + 1, :] * pltpu.roll(ext, (2 - j) % n, 0)[8:8 + tm]
        o_ref[...] = _silu(acc)

    return pl.pallas_call(
        body, name="conv_fwd", grid=(n_i, 2),
        in_specs=_halo_specs(tm, 512, lambda j: 1 + j, T, "ij")
        + [pl.BlockSpec((8, 512), lambda i, j: (0, j)), pl.BlockSpec((1, 512), lambda i, j: (0, j))],
        out_specs=pl.BlockSpec((tm, 512), lambda i, j: (i, j)),
        out_shape=jax.ShapeDtypeStruct((T, 2 * ML_W), _F32),
        compiler_params=_cparams(("arbitrary", "arbitrary")),
    )(rest, rest, rest, conv_w, conv_b)


def _conv_bwd(rest, conv_w, conv_b, da_f, da_b, T):
    tm = 512
    n_i = T // tm
    n = tm + 16

    def body(up, uc, un, fp, fc, fn, bp, bc, bn, w_ref, b_ref, du_ref, dw_ref):
        i = pl.program_id(1)
        ext_u = _extend(up, uc, un, i, n_i)
        ext_da = _extend(fp, fc, fn, i, n_i) + _extend(bp, bc, bn, i, n_i)
        shifted = [pltpu.roll(ext_u, (2 - j) % n, 0) for j in range(CONV_W)]
        pre = jnp.zeros((n, 512), _F32) + b_ref[...]
        for j in range(CONV_W):
            pre = pre + w_ref[j:j + 1, :] * shifted[j]
        gidx = i * tm - 8 + lax.broadcasted_iota(jnp.int32, (n, 1), 0)
        dpre = jnp.where((gidx >= 0) & (gidx < T), ext_da * _dsilu(pre), 0.0)
        du = jnp.zeros((tm, 512), _F32)
        for j in range(CONV_W):
            du = du + w_ref[j:j + 1, :] * pltpu.roll(dpre, (j - 2) % n, 0)[8:8 + tm]
        du_ref[...] = du.astype(_BF16)
        dpc = dpre[8:8 + tm]
        parts = [jnp.sum(dpc * shifted[j][8:8 + tm], axis=0, keepdims=True) for j in range(CONV_W)]
        parts.append(jnp.sum(dpc, axis=0, keepdims=True))
        parts.append(jnp.zeros((2, 512), _F32))
        upd = jnp.concatenate(parts, axis=0)

        @pl.when(i == 0)
        def _():
            dw_ref[...] = upd

        @pl.when(i > 0)
        def _():
            dw_ref[...] += upd

    return pl.pallas_call(
        body, name="conv_bwd", grid=(2, n_i),
        in_specs=_halo_specs(tm, 512, lambda j: 1 + j, T, "ji")
        + _halo_specs(tm, 512, lambda j: j, T, "ji") + _halo_specs(tm, 512, lambda j: j, T, "ji")
        + [pl.BlockSpec((8, 512), lambda j, i: (0, j)), pl.BlockSpec((1, 512), lambda j, i: (0, j))],
        out_specs=(pl.BlockSpec((tm, 512), lambda j, i: (i, j)), pl.BlockSpec((8, 512), lambda j, i: (0, j))),
        out_shape=(jax.ShapeDtypeStruct((T, 2 * ML_W), _BF16), jax.ShapeDtypeStruct((8, 2 * ML_W), _F32)),
        compiler_params=_cparams(("arbitrary", "arbitrary")),
    )(rest, rest, rest, da_f, da_f, da_f, da_b, da_b, da_b, conv_w, conv_b)


def _scan_rows(x, suffix):
    L = x.shape[0]
    row = lax.broadcasted_iota(jnp.int32, (L, 1), 0)
    step = 1
    while step < L:
        if suffix:
            x = x + jnp.where(row < L - step, pltpu.roll(x, L - step, 0), 0.0)
        else:
            x = x + jnp.where(row >= step, pltpu.roll(x, step, 0), 0.0)
        step *= 2
    return x


def _ml_gates(gt, rev):
    L = gt.shape[0]
    ri = lax.broadcasted_iota(jnp.int32, (L, L), 0)
    ci = lax.broadcasted_iota(jnp.int32, (L, L), 1)
    mask = (ci >= ri) if rev else (ci <= ri)
    lf = jnp.minimum(gt, 0.0) - jnp.log(1.0 + jnp.exp(-jnp.abs(gt)))
    b = _scan_rows(lf, suffix=rev)
    return mask, b, b.T, gt.T


def _ml_head_gates(gt, gates, head, rev):
    _, b, b_t, gt_t = gates
    ci = (8 if rev else 0) + head
    cf = ci + ML_HEADS
    last = 0 if rev else gt.shape[0] - 1
    return dict(icol=gt[:, ci:ci + 1], b_col=b[:, cf:cf + 1], b_row=b_t[cf:cf + 1, :],
                i_row=gt_t[ci:ci + 1, :], bl=b[last:last + 1, cf:cf + 1])


def _ml_chunk(q, k, v, hg, mask, C, n, m):
    icol, b_col, b_row, bl = hg["icol"], hg["b_col"], hg["b_row"], hg["bl"]
    dlog = jnp.where(mask, b_col - b_row + hg["i_row"], NEG)
    m_t = jnp.maximum(b_col + m, jnp.max(dlog, axis=1, keepdims=True))
    dm = jnp.exp(dlog - m_t)
    ks = k * (ML_HEAD_DIM ** -0.5)
    qb, kb, vb = q.astype(_BF16), ks.astype(_BF16), v.astype(_BF16)
    s = _nt(qb, kb) * dm
    g = jnp.exp(b_col + m - m_t)
    qc = _nt(qb, C.astype(_BF16))
    num = _nn(s.astype(_BF16), vb) + g * qc
    qn = jnp.sum(q * n, axis=1, keepdims=True)
    den = jnp.sum(s, axis=1, keepdims=True) + g * qn
    e_m = jnp.exp(-m_t)
    nrm = jnp.maximum(jnp.abs(den), e_m)
    h = num / nrm
    a_col = bl - b_col + icol
    m_new = jnp.maximum(bl + m, jnp.max(a_col, axis=0, keepdims=True))
    decay = jnp.exp(bl + m - m_new)
    w = jnp.exp(a_col - m_new)
    c_new = decay * C + _tn((w * v).astype(_BF16), kb)
    n_new = decay * n + jnp.sum(w * ks, axis=0, keepdims=True)
    aux = dict(dm=dm, ks=ks, qb=qb, kb=kb, vb=vb, s=s, g=g, qc=qc, qn=qn,
               den=den, e_m=e_m, nrm=nrm, decay=decay, w=w)
    return h, c_new, n_new, m_new, aux


def _mlstm_fwd(qk_act, rest, T, rev):
    tb = ML_CB * ML_CHUNK
    nblk = T // tb
    nc = T // ML_CHUNK
    bi = (lambda i: nblk - 1 - i) if rev else (lambda i: i)

    def body(q_ref, k_ref, v_ref, g_ref, h_ref, cs_ref, ns_ref, ms_ref, c_scr, n_scr, m_scr):
        @pl.when(pl.program_id(0) == 0)
        def _():
            c_scr[...] = jnp.zeros_like(c_scr)
            n_scr[...] = jnp.zeros_like(n_scr)
            m_scr[...] = jnp.zeros_like(m_scr)

        def step(j, carry):
            c = (ML_CB - 1 - j) if rev else j
            r0 = pl.multiple_of(c * ML_CHUNK, ML_CHUNK)
            gt = g_ref[pl.ds(r0, ML_CHUNK), :]
            gates = _ml_gates(gt, rev)
            for hd in range(ML_HEADS):
                cols = slice(hd * ML_HEAD_DIM, (hd + 1) * ML_HEAD_DIM)
                C = c_scr[hd]
                n = n_scr[hd:hd + 1, :]
                mrow = m_scr[hd:hd + 1, :]
                cs_ref[c, hd] = C
                ns_ref[c, hd:hd + 1, :] = n
                ms_ref[c, hd:hd + 1, :] = mrow
                h, c_new, n_new, m_new, _ = _ml_chunk(
                    q_ref[pl.ds(r0, ML_CHUNK), cols], k_ref[pl.ds(r0, ML_CHUNK), cols],
                    v_ref[pl.ds(r0, ML_CHUNK), cols], _ml_head_gates(gt, gates, hd, rev), gates[0],
                    C, n, mrow[:, 0:1])
                h_ref[pl.ds(r0, ML_CHUNK), cols] = h
                c_scr[hd] = c_new
                n_scr[hd:hd + 1, :] = n_new
                m_scr[hd:hd + 1, :] = jnp.broadcast_to(m_new, (1, 128))
            return carry

        lax.fori_loop(0, ML_CB, step, 0)

    return pl.pallas_call(
        body, name="mlstm_fwd_rev" if rev else "mlstm_fwd", grid=(nblk,),
        in_specs=[pl.BlockSpec((tb, ML_W), lambda i: (bi(i), 0)),
                  pl.BlockSpec((tb, ML_W), lambda i: (bi(i), 1)),
                  pl.BlockSpec((tb, ML_W), lambda i: (bi(i), 3)),
                  pl.BlockSpec((tb, 128), lambda i: (bi(i), GATE_COL // 128))],
        out_specs=(pl.BlockSpec((tb, ML_W), lambda i: (bi(i), 0)),
                   pl.BlockSpec((ML_CB, ML_HEADS, 128, 128), lambda i: (bi(i), 0, 0, 0)),
                   pl.BlockSpec((ML_CB, ML_HEADS, 128), lambda i: (bi(i), 0, 0)),
                   pl.BlockSpec((ML_CB, ML_HEADS, 128), lambda i: (bi(i), 0, 0))),
        out_shape=(jax.ShapeDtypeStruct((T, ML_W), _F32),
                   jax.ShapeDtypeStruct((nc, ML_HEADS, 128, 128), _F32),
                   jax.ShapeDtypeStruct((nc, ML_HEADS, 128), _F32),
                   jax.ShapeDtypeStruct((nc, ML_HEADS, 128), _F32)),
        scratch_shapes=[pltpu.VMEM((ML_HEADS, 128, 128), _F32), pltpu.VMEM((8, 128), _F32),
                        pltpu.VMEM((8, 128), _F32)],
        compiler_params=_cparams(("arbitrary",)),
    )(qk_act, qk_act, rest, rest)


def _mlstm_bwd(qk_act, rest, d_h, cs, ns, ms, T, rev):
    tb = ML_CB * ML_CHUNK
    nblk = T // tb
    bi = (lambda i: i) if rev else (lambda i: nblk - 1 - i)

    def body(q_ref, k_ref, v_ref, g_ref, dh_ref, cs_ref, ns_ref, ms_ref,
             dqk_ref, dv_ref, dg_ref, dc_scr, dn_scr):
        @pl.when(pl.program_id(0) == 0)
        def _():
            dc_scr[...] = jnp.zeros_like(dc_scr)
            dn_scr[...] = jnp.zeros_like(dn_scr)

        def step(j, carry):
            c = j if rev else (ML_CB - 1 - j)
            r0 = pl.multiple_of(c * ML_CHUNK, ML_CHUNK)
            gt = g_ref[pl.ds(r0, ML_CHUNK), :]
            gates = _ml_gates(gt, rev)
            mask = gates[0]
            lane = lax.broadcasted_iota(jnp.int32, (1, 128), 1)
            sub = lax.broadcasted_iota(jnp.int32, (128, 1), 0)
            db_t = jnp.zeros((ML_CHUNK, 128), _F32)
            da_t = jnp.zeros((ML_CHUNK, 128), _F32)
            cs_rows = jnp.zeros((128, ML_CHUNK), _F32)
            dbl_t = jnp.zeros((1, 128), _F32)
            for hd in range(ML_HEADS):
                cols = slice(hd * ML_HEAD_DIM, (hd + 1) * ML_HEAD_DIM)
                ci = (8 if rev else 0) + hd
                cf = ci + ML_HEADS
                q = q_ref[pl.ds(r0, ML_CHUNK), cols]
                k = k_ref[pl.ds(r0, ML_CHUNK), cols]
                v = v_ref[pl.ds(r0, ML_CHUNK), cols]
                C = cs_ref[c, hd]
                n = ns_ref[c, hd:hd + 1, :]
                m = ms_ref[c, hd:hd + 1, :][:, 0:1]
                dcn = dc_scr[hd]
                dnn = dn_scr[hd:hd + 1, :]
                h, _, _, _, a = _ml_chunk(q, k, v, _ml_head_gates(gt, gates, hd, rev), mask, C, n, m)
                d_hv = dh_ref[pl.ds(r0, ML_CHUNK), cols]
                g, s, w, ks = a["g"], a["s"], a["w"], a["ks"]
                qb, kb, vb = a["qb"], a["kb"], a["vb"]
                dnum = d_hv / a["nrm"]
                hdot = jnp.sum(d_hv * h, axis=1, keepdims=True)
                dden = jnp.where(jnp.abs(a["den"]) >= a["e_m"], -hdot / a["nrm"] * jnp.sign(a["den"]), 0.0)
                dnb = dnum.astype(_BF16)
                d_s = _nt(dnb, vb) + dden
                r = d_s * s
                dsqk = (d_s * a["dm"]).astype(_BF16)
                cb = C.astype(_BF16)
                dq = _nn(dsqk, kb) + g * _nn(dnb, cb) + (dden * g) * n
                dk = _tn(dsqk, qb)
                dv = _tn(s.astype(_BF16), dnb)
                dg = jnp.sum(dnum * a["qc"], axis=1, keepdims=True) + dden * a["qn"]
                db_col = jnp.sum(r, axis=1, keepdims=True) + dg * g
                cs_rows = cs_rows + jnp.where((sub == ci) | (sub == cf), jnp.sum(r, axis=0, keepdims=True), 0.0)
                dc_chunk = _tn((g * dnum).astype(_BF16), qb)
                dn_chunk = jnp.sum((dden * g) * q, axis=0, keepdims=True)
                dcb = dcn.astype(_BF16)
                vdc = _nn(vb, dcb)
                kdc = _nt(kb, dcb)
                dw = jnp.sum(vdc * ks, axis=1, keepdims=True) + jnp.sum(ks * dnn, axis=1, keepdims=True)
                dv = dv + w * kdc
                dk = dk + w * vdc + w * dnn
                da = dw * w
                ddecay = (jnp.sum(jnp.sum(dcn * C, axis=1, keepdims=True), axis=0, keepdims=True)
                          + jnp.sum(dnn * n, axis=1, keepdims=True))
                dbl = ddecay * a["decay"] + jnp.sum(da, axis=0, keepdims=True)
                db_t = db_t + jnp.where(lane == cf, db_col - da, 0.0)
                da_t = da_t + jnp.where(lane == ci, da, 0.0)
                dbl_t = dbl_t + jnp.where(lane == cf, dbl, 0.0)
                dc_scr[hd] = dc_chunk + a["decay"] * dcn
                dn_scr[hd:hd + 1, :] = dn_chunk + a["decay"] * dnn
                dqk_ref[pl.ds(r0, ML_CHUNK), cols] = dq
                dqk_ref[pl.ds(r0, ML_CHUNK), slice(ML_W + hd * 128, ML_W + (hd + 1) * 128)] = dk * (ML_HEAD_DIM ** -0.5)
                dv_ref[pl.ds(r0, ML_CHUNK), cols] = dv.astype(_BF16)
            lo = 8 if rev else 0
            is_i = (lane >= lo) & (lane < lo + ML_HEADS)
            is_f = (lane >= lo + ML_HEADS) & (lane < lo + 2 * ML_HEADS)
            cs_t = cs_rows.T
            db_all = db_t - jnp.where(is_f, cs_t, 0.0)
            dlf = _scan_rows(db_all, suffix=not rev) + dbl_t
            dg_ref[pl.ds(r0, ML_CHUNK), :] = (da_t + jnp.where(is_i, cs_t, 0.0)
                                               + jnp.where(is_f, dlf * _sigmoid(-gt), 0.0))
            return carry

        lax.fori_loop(0, ML_CB, step, 0)

    return pl.pallas_call(
        body, name="mlstm_bwd_rev" if rev else "mlstm_bwd", grid=(nblk,),
        in_specs=[pl.BlockSpec((tb, ML_W), lambda i: (bi(i), 0)),
                  pl.BlockSpec((tb, ML_W), lambda i: (bi(i), 1)),
                  pl.BlockSpec((tb, ML_W), lambda i: (bi(i), 3)),
                  pl.BlockSpec((tb, 128), lambda i: (bi(i), GATE_COL // 128)),
                  pl.BlockSpec((tb, ML_W), lambda i: (bi(i), 0)),
                  pl.BlockSpec((ML_CB, ML_HEADS, 128, 128), lambda i: (bi(i), 0, 0, 0)),
                  pl.BlockSpec((ML_CB, ML_HEADS, 128), lambda i: (bi(i), 0, 0)),
                  pl.BlockSpec((ML_CB, ML_HEADS, 128), lambda i: (bi(i), 0, 0))],
        out_specs=(pl.BlockSpec((tb, 2 * ML_W), lambda i: (bi(i), 0)),
                   pl.BlockSpec((tb, ML_W), lambda i: (bi(i), 0)),
                   pl.BlockSpec((tb, 128), lambda i: (bi(i), 0))),
        out_shape=(jax.ShapeDtypeStruct((T, 2 * ML_W), _F32), jax.ShapeDtypeStruct((T, ML_W), _BF16),
                   jax.ShapeDtypeStruct((T, 128), _F32)),
        scratch_shapes=[pltpu.VMEM((ML_HEADS, 128, 128), _F32), pltpu.VMEM((8, 128), _F32)],
        compiler_params=_cparams(("arbitrary",)),
    )(qk_act, qk_act, rest, rest, d_h, cs, ns, ms)


def _post(x, target, o_na, rest, h_f, h_b, gate, ml_norm_w, final_w, w_out_bf, T):
    tm = 256
    n_i = T // tm

    def body(x_ref, t_ref, o_ref, zna_ref, hf_ref, hb_ref, mo_ref, mz_ref, gate_ref, mw_ref, fw_ref, w_ref,
             dx1_ref, do_ref, dzna_ref, dh_ref, dmo_ref, dmz_ref, dwo_ref, vec_ref):
        i = pl.program_id(0)
        gate_v = gate_ref[...]
        fw = fw_ref[...]
        zna = zna_ref[...]
        o = o_ref[...]
        na_out = o * _silu(zna)
        hsum = hf_ref[...] + hb_ref[...]
        sg = _sigmoid(mo_ref[...])
        hm = hsum * sg
        mz = mz_ref[...]
        smz = _silu(mz)
        hn_l, rstd_l, ml_l = [], [], []
        for hd in range(ML_HEADS):
            cols = slice(hd * 128, (hd + 1) * 128)
            hh = hm[:, cols]
            mu = jnp.mean(hh, axis=-1, keepdims=True)
            var = jnp.mean(jnp.square(hh - mu), axis=-1, keepdims=True)
            rstd = lax.rsqrt(var + EPS)
            hn = (hh - mu) * rstd
            hn_l.append(hn)
            rstd_l.append(rstd)
            ml_l.append(hn * mw_ref[:, cols] * smz[:, cols])
        mix = jnp.concatenate([na_out] + ml_l, axis=1).astype(_BF16)
        y = _nn(mix, w_ref[...])
        x1 = x_ref[...] + gate_v * y
        r = lax.rsqrt(jnp.mean(x1 * x1, axis=-1, keepdims=True) + EPS)
        xhat = x1 * r
        out = xhat * fw
        err = out - t_ref[...]
        loss = 0.5 * jnp.sum(jnp.sum(err * err, axis=1, keepdims=True), axis=0, keepdims=True) / D_MODEL
        dout = err * (1.0 / D_MODEL)
        dfw = jnp.sum(dout * xhat, axis=0, keepdims=True)
        dxhat = dout * fw
        dx1 = r * (dxhat - xhat * jnp.mean(dxhat * xhat, axis=-1, keepdims=True))
        dx1_ref[...] = dx1
        dgate = jnp.sum(dx1 * y, axis=0, keepdims=True)
        dy = (dx1 * gate_v).astype(_BF16)
        dmix = _nt(dy, w_ref[...])
        dwo = _tn(mix, dy)
        dna = dmix[:, :NA_W]
        do_ref[...] = (dna * _silu(zna)).astype(_BF16)
        dzna_ref[...] = (dna * o * _dsilu(zna)).astype(_BF16)
        dmw_l = []
        for hd in range(ML_HEADS):
            cols = slice(hd * 128, (hd + 1) * 128)
            dml = dmix[:, NA_W + hd * 128:NA_W + (hd + 1) * 128]
            hn = hn_l[hd]
            mwv = mw_ref[:, cols]
            dmz_ref[:, cols] = (dml * hn * mwv * _dsilu(mz[:, cols])).astype(_BF16)
            dhn = dml * mwv * smz[:, cols]
            dmw_l.append(jnp.sum(dml * hn * smz[:, cols], axis=0, keepdims=True))
            dhm = rstd_l[hd] * (dhn - jnp.mean(dhn, axis=-1, keepdims=True)
                                - hn * jnp.mean(dhn * hn, axis=-1, keepdims=True))
            sgc = sg[:, cols]
            dh_ref[:, cols] = dhm * sgc
            dmo_ref[:, cols] = (dhm * hsum[:, cols] * sgc * (1.0 - sgc)).astype(_BF16)
        dmw = jnp.concatenate(dmw_l + [jnp.zeros((1, D_MODEL - ML_W), _F32)], axis=1)
        lane = lax.broadcasted_iota(jnp.int32, (1, D_MODEL), 1)
        vec = jnp.concatenate([dfw, dgate, dmw, jnp.where(lane == 0, loss, 0.0),
                               jnp.zeros((4, D_MODEL), _F32)], axis=0)

        @pl.when(i == 0)
        def _():
            dwo_ref[...] = dwo
            vec_ref[...] = vec

        @pl.when(i > 0)
        def _():
            dwo_ref[...] += dwo
            vec_ref[...] += vec

    tok = lambda w, j: pl.BlockSpec((tm, w), lambda i: (i, j))
    tok3 = pl.BlockSpec((None, tm, D_MODEL), lambda i: (0, i, 0))
    row = lambda w: pl.BlockSpec((1, w), lambda i: (0, 0))
    f32 = lambda w: jax.ShapeDtypeStruct((T, w), _F32)
    bf16 = lambda w: jax.ShapeDtypeStruct((T, w), _BF16)
    return pl.pallas_call(
        body, name="post", grid=(n_i,),
        in_specs=[tok3, tok3, tok(NA_W, 0), tok(NA_W, 0), tok(ML_W, 0), tok(ML_W, 0),
                  tok(ML_W, 4), tok(ML_W, 5), row(D_MODEL), row(ML_W), row(D_MODEL),
                  pl.BlockSpec((D_MODEL, D_MODEL), lambda i: (0, 0))],
        out_specs=(tok(D_MODEL, 0), tok(NA_W, 0), tok(NA_W, 0), tok(ML_W, 0), tok(ML_W, 0), tok(ML_W, 0),
                   pl.BlockSpec((D_MODEL, D_MODEL), lambda i: (0, 0)),
                   pl.BlockSpec((8, D_MODEL), lambda i: (0, 0))),
        out_shape=(f32(D_MODEL), bf16(NA_W), bf16(NA_W), f32(ML_W), bf16(ML_W),
                   bf16(ML_W), jax.ShapeDtypeStruct((D_MODEL, D_MODEL), _F32),
                   jax.ShapeDtypeStruct((8, D_MODEL), _F32)),
        compiler_params=_cparams(("arbitrary",)),
    )(x, target, o_na, rest, h_f, h_b, rest, rest, gate, ml_norm_w, final_w, w_out_bf)


def _section_specs(sections, tm):
    specs, args = [], []
    for _, width, parts in sections:
        for arr, cb in parts:
            specs.append(pl.BlockSpec((tm, width), functools.partial(lambda i, cb: (i, cb), cb=cb)))
            args.append(arr)
    return specs, args


def _section_values(sections, refs, dtype):
    vals, at = [], 0
    for _, _, parts in sections:
        v = refs[at][...]
        for r in refs[at + 1:at + len(parts)]:
            v = v.astype(_F32) + r[...].astype(_F32)
        at += len(parts)
        vals.append(v.astype(dtype))
    return vals


def _inproj_bwd_x(x, dx1, scale1p, norm_w, w_in_bf, sections, T):
    tm = 256
    sspecs, sargs = _section_specs(sections, tm)
    ns = len(sargs)

    def body(*refs):
        x_ref, dx1_ref, sc_ref, nw_ref, w_ref = refs[:5]
        srefs = refs[5:5 + ns]
        gx_ref, vec_ref = refs[5 + ns:]
        i = pl.program_id(0)
        vals = _section_values(sections, srefs, _BF16)
        dh = jnp.zeros((tm, D_MODEL), _F32)
        for (c0, width, _), val in zip(sections, vals):
            dh = dh + _nt(val, w_ref[:, c0:c0 + width])
        xv = x_ref[...]
        r = lax.rsqrt(jnp.mean(xv * xv, axis=-1, keepdims=True) + EPS)
        xhat = xv * r
        nw = nw_ref[...]
        dshift = jnp.sum(dh, axis=0, keepdims=True)
        dscale = jnp.sum(dh * xhat * nw, axis=0, keepdims=True)
        dhpre = dh * sc_ref[...]
        dnw = jnp.sum(dhpre * xhat, axis=0, keepdims=True)
        dxhat = dhpre * nw
        gx_ref[...] = dx1_ref[...] + r * (dxhat - xhat * jnp.mean(dxhat * xhat, axis=-1, keepdims=True))
        vec = jnp.concatenate([dshift, dscale, dnw, jnp.zeros((5, D_MODEL), _F32)], axis=0)

        @pl.when(i == 0)
        def _():
            vec_ref[...] = vec

        @pl.when(i > 0)
        def _():
            vec_ref[...] += vec

    row = pl.BlockSpec((1, D_MODEL), lambda i: (0, 0))
    tok = pl.BlockSpec((tm, D_MODEL), lambda i: (i, 0))
    tok3 = pl.BlockSpec((None, tm, D_MODEL), lambda i: (0, i, 0))
    return pl.pallas_call(
        body, name="inproj_bwd_x", grid=(T // tm,),
        in_specs=[tok3, tok, row, row, pl.BlockSpec((D_MODEL, IN_PAD), lambda i: (0, 0))] + sspecs,
        out_specs=(tok3, pl.BlockSpec((8, D_MODEL), lambda i: (0, 0))),
        out_shape=(jax.ShapeDtypeStruct((1, T, D_MODEL), _F32), jax.ShapeDtypeStruct((8, D_MODEL), _F32)),
        compiler_params=_cparams(("arbitrary",)),
    )(x, dx1, scale1p, norm_w, w_in_bf, *sargs)


def _inproj_bwd_w(h_t, sections, T):
    tm = 1024
    n_i = T // tm
    sspecs, sargs = _section_specs(sections, tm)
    ns = len(sargs)

    def body(*refs):
        h_ref = refs[0]
        srefs = refs[1:1 + ns]
        dw_ref, db_ref, acc, sem = refs[1 + ns:]
        i = pl.program_id(0)

        @pl.when(i == 0)
        def _():
            acc[...] = jnp.zeros_like(acc)
            db_ref[...] = jnp.zeros_like(db_ref)

        hv = h_ref[...]
        for (c0, width, _), v in zip(sections, _section_values(sections, srefs, _F32)):
            acc[:, c0:c0 + width] += _nn(hv, v.astype(_BF16))
            db_ref[0:1, c0:c0 + width] += jnp.sum(v, axis=0, keepdims=True)

        @pl.when(i == n_i - 1)
        def _():
            cp = pltpu.make_async_copy(acc, dw_ref, sem)
            cp.start()
            cp.wait()

    return pl.pallas_call(
        body, name="inproj_bwd_w", grid=(n_i,),
        in_specs=[pl.BlockSpec((D_MODEL, tm), lambda i: (0, i))] + sspecs,
        out_specs=(pl.BlockSpec(memory_space=pl.ANY), pl.BlockSpec((8, IN_PAD), lambda i: (0, 0))),
        out_shape=(jax.ShapeDtypeStruct((D_MODEL, IN_PAD), _F32), jax.ShapeDtypeStruct((8, IN_PAD), _F32)),
        scratch_shapes=[pltpu.VMEM((D_MODEL, IN_PAD), _F32), pltpu.SemaphoreType.DMA],
        compiler_params=_cparams(("arbitrary",)),
    )(h_t, *sargs)


def _adamw_math(w, g, m, v):
    m = ADAM_B1 * m + (1.0 - ADAM_B1) * g
    v = ADAM_B2 * v + (1.0 - ADAM_B2) * jnp.square(g)
    m_hat = m / (1.0 - ADAM_B1 ** ADAM_STEP)
    v_hat = v / (1.0 - ADAM_B2 ** ADAM_STEP)
    delta = -ADAM_LR * (m_hat / (jnp.sqrt(v_hat) + ADAM_EPS) + ADAM_WD * w)
    return delta, m, v


def _adamw_slots(w, m, v, slots, tr, name, own=None):
    R, C = w.shape
    extra = [] if own is None else [own]

    def body(w_ref, m_ref, v_ref, s_ref, *refs):
        g_ref, d_ref, nm_ref, nv_ref = refs[len(extra):]
        g = s_ref[0].astype(_F32)
        for k in range(1, N_DEV):
            g = g + s_ref[k].astype(_F32)
        if extra:
            g = g + refs[0][...].astype(_F32)
        g_ref[...] = g
        d_ref[...], nm_ref[...], nv_ref[...] = _adamw_math(w_ref[...], g, m_ref[...], v_ref[...])

    blk = pl.BlockSpec((tr, C), lambda i: (i, 0))
    return pl.pallas_call(
        body, name=name, grid=(R // tr,),
        in_specs=[blk, blk, blk, pl.BlockSpec((N_DEV, tr, C), lambda i: (0, i, 0))] + [blk] * len(extra),
        out_specs=(blk, blk, blk, blk),
        out_shape=tuple(jax.ShapeDtypeStruct((R, C), _F32) for _ in range(4)),
        compiler_params=_cparams(("arbitrary",)),
    )(w, m, v, slots, *extra)


def _w_ada_update(c_all, dmod_my, w, m, v):
    def body(c_ref, d_ref, w_ref, m_ref, v_ref, g_ref, dl_ref, nm_ref, nv_ref):
        g = lax.dot_general(_silu(c_ref[...]), d_ref[...], (((0,), (0,)), ((), ())),
                            precision=_HI, preferred_element_type=_F32)
        g_ref[...] = g
        dl_ref[...], nm_ref[...], nv_ref[...] = _adamw_math(w_ref[...], g, m_ref[...], v_ref[...])

    return pl.pallas_call(
        body, name="w_ada_update",
        out_shape=tuple(jax.ShapeDtypeStruct(w.shape, _F32) for _ in range(4)),
        compiler_params=_cparams(),
    )(c_all, dmod_my, w, m, v)


_PACK = (("b_ada", 3072, 3072), ("norm_w", 1024, 1024), ("b_in", IN_W, IN_PAD), ("conv_w", 5120, 5120),
         ("conv_b", 1024, 1024), ("rpb", 3720, 3840), ("ml_norm_w", 512, 512), ("final_norm_w", 1024, 1024),
         ("loss", 1, 128))
_PACK_OFF = {}
_off = 0
for _name, _len, _pad in _PACK:
    _PACK_OFF[_name] = (_off, _len)
    _off += _pad
_PACK_LEN = _off


def _pack(parts):
    cols = []
    for name, length, pad in _PACK:
        vec = parts[name].reshape(-1).astype(_F32)
        cols.append(jnp.pad(vec, (0, pad - length)))
    return jnp.concatenate(cols).reshape(1, _PACK_LEN)


def _unpack(vec, name, shape):
    off, length = _PACK_OFF[name]
    return vec.reshape(-1)[off:off + length].reshape(shape)


def kernel(x, c, w_ada, b_ada, norm_w, w_in, b_in, conv_w, conv_b, rpb, ml_norm_w, w_out, final_norm_w, loss_target, m_w_ada, m_b_ada, m_norm_w, m_w_in, m_b_in, m_conv_w, m_conv_b, m_rpb, m_ml_norm_w, m_w_out, m_final_norm_w, v_w_ada, v_b_ada, v_norm_w, v_w_in, v_b_in, v_conv_w, v_conv_b, v_rpb, v_ml_norm_w, v_w_out, v_final_norm_w):
    T = x.shape[1]
    rows = T // GRID_W
    me = 4 * lax.axis_index("x") + 2 * lax.axis_index("y") + lax.axis_index("c")
    n_in = w_in.shape[2]
    n_ada = w_ada.shape[2]
    n_cw = conv_w.shape[2]
    n_wo = w_out.shape[1]

    g_w_in, g_w_out, g_conv_w, g_c = _exchange(
        [w_in[0].astype(_BF16), w_out[0].astype(_BF16), conv_w[0], c], [False] * 4, "gather_weights")
    w_in_full = g_w_in.transpose(1, 0, 2).reshape(D_MODEL, N_DEV * n_in)
    w_in_bf = jnp.pad(w_in_full, ((0, 0), (0, IN_PAD - IN_W)))
    b_in_pad = jnp.pad(b_in, ((0, 0), (0, IN_PAD - IN_W)))
    w_out_bf = g_w_out.reshape(N_DEV * n_wo, D_MODEL)
    conv_w_full = jnp.pad(g_conv_w.transpose(1, 0, 2).reshape(CONV_W, N_DEV * n_cw), ((0, 3), (0, 0)))
    c_all = g_c.reshape(N_DEV, D_MODEL)

    b_ada_my = lax.dynamic_slice(b_ada, (0, me * n_ada), (1, n_ada))
    (mod_slots,) = _exchange([_mod_part(c_all, w_ada[0], b_ada_my)], [False], "gather_mod")
    mod = lax.dynamic_index_in_dim(mod_slots, me, axis=1, keepdims=False).reshape(1, 3 * D_MODEL)
    shift, scale, gate = mod[:, :D_MODEL], mod[:, D_MODEL:2 * D_MODEL], mod[:, 2 * D_MODEL:]
    scale1p = 1.0 + scale

    qkv, rest, h_bf = _inproj_fwd(x, scale1p, shift, norm_w, w_in_bf, b_in_pad)
    bias, bias_t = _na_bias_tables(rpb[0], rows)
    o_na, lse, lse_rows = _na_fwd(qkv, bias, T)
    qk_act = _conv_fwd(rest, conv_w_full, conv_b, T)
    h_f, cs_f, ns_f, ms_f = _mlstm_fwd(qk_act, rest, T, False)
    h_b, cs_b, ns_b, ms_b = _mlstm_fwd(qk_act, rest, T, True)

    dx1, d_o, dz_na, d_h, d_mo, d_mz, dwo, pvec = _post(
        x, loss_target, o_na, rest, h_f, h_b, gate, ml_norm_w, final_norm_w.reshape(1, D_MODEL), w_out_bf, T)

    dq_na, delta_rows, dbias = _na_bwd_q(qkv, bias, o_na, d_o, lse, T)
    dk_na, dv_na = _na_bwd_kv(qkv, bias_t, d_o, lse_rows, delta_rows, T)
    d_rpb = _rpb_grad(dbias, rows)
    dqk_f, dv_f, dg_f = _mlstm_bwd(qk_act, rest, d_h, cs_f, ns_f, ms_f, T, False)
    dqk_b, dv_b, dg_b = _mlstm_bwd(qk_act, rest, d_h, cs_b, ns_b, ms_b, T, True)
    d_u, dconv = _conv_bwd(rest, conv_w_full, conv_b, dqk_f, dqk_b, T)

    sections = [(0, 512, [(dq_na, 0)]), (512, 512, [(dk_na, 0)]), (1024, 512, [(dv_na, 0)]),
                (1536, 512, [(dz_na, 0)]), (2048, 512, [(d_u, 0)]), (2560, 512, [(d_u, 1)]),
                (3072, 512, [(dv_f, 0), (dv_b, 0)]), (3584, 512, [(d_mo, 0)]), (4096, 512, [(d_mz, 0)]),
                (4608, 128, [(dg_f, 0), (dg_b, 0)])]
    dw_pad, db_pad = _inproj_bwd_w(h_bf, sections, T)
    db_in = db_pad[0, :IN_W]

    dw_blocks = dw_pad[:, :IN_W].astype(_BF16).reshape(D_MODEL, N_DEV, n_in).transpose(1, 0, 2)
    dwo_blocks = dwo.astype(_BF16).reshape(N_DEV, n_wo, D_MODEL)
    started = _scatter_start([dw_blocks, dwo_blocks], "grads_start")
    grad_x, xvec = _inproj_bwd_x(x, dx1, scale1p + started[-1][0:1, 0:1], norm_w, w_in_bf, sections, T)
    (dw_blocks, dwo_blocks), (s_w_in, s_w_out) = _scatter_wait(started, xvec, "grads_wait")

    small = _pack({
        "b_ada": jnp.concatenate([xvec[0], xvec[1], pvec[1]]),
        "norm_w": xvec[2], "b_in": db_in, "conv_w": dconv[:CONV_W], "conv_b": dconv[CONV_W],
        "rpb": d_rpb, "ml_norm_w": pvec[2, :ML_W], "final_norm_w": pvec[0], "loss": pvec[3, :1]})
    (s_small,) = _exchange([small], [False], "exchange_small")

    own = lambda blocks: lax.dynamic_index_in_dim(blocks, me, axis=0, keepdims=False)
    g_w_in_s, d_w_in, nm_w_in, nv_w_in = _adamw_slots(
        w_in[0], m_w_in[0], v_w_in[0], s_w_in, 128, "adamw_w_in", own=own(dw_blocks))
    g_w_out_s, d_w_out, nm_w_out, nv_w_out = _adamw_slots(
        w_out[0], m_w_out[0], v_w_out[0], s_w_out, n_wo, "adamw_w_out", own=own(dwo_blocks))
    dmod_all = s_small[:, 0, :3 * D_MODEL]
    dmod_my = lax.dynamic_slice(dmod_all, (0, me * n_ada), (N_DEV, n_ada))
    g_w_ada, d_w_ada, nm_w_ada, nv_w_ada = _w_ada_update(c_all, dmod_my, w_ada[0], m_w_ada[0], v_w_ada[0])

    def embed(shard):
        return lax.dynamic_update_slice(jnp.zeros((CONV_W, N_DEV * n_cw), _F32), shard[0], (0, me * n_cw))

    zero1 = jnp.zeros((1,), _F32)
    packed = lambda b_a, n_w, b_i, c_w, c_b, rp, mn, fn: _pack({
        "b_ada": b_a, "norm_w": n_w, "b_in": b_i, "conv_w": embed(c_w), "conv_b": c_b, "rpb": rp,
        "ml_norm_w": mn, "final_norm_w": fn, "loss": zero1})
    pw = packed(b_ada, norm_w, b_in, conv_w, conv_b, rpb, ml_norm_w, final_norm_w)
    pm = packed(m_b_ada, m_norm_w, m_b_in, m_conv_w, m_conv_b, m_rpb, m_ml_norm_w, m_final_norm_w)
    pv = packed(v_b_ada, v_norm_w, v_b_in, v_conv_w, v_conv_b, v_rpb, v_ml_norm_w, v_final_norm_w)
    sg, sd, sm, sv = _adamw_slots(pw, pm, pv, s_small, 1, "adamw_small")

    def small_outs(vec):
        cw = lax.dynamic_slice(_unpack(vec, "conv_w", (CONV_W, N_DEV * n_cw)), (0, me * n_cw), (CONV_W, n_cw))
        return dict(b_ada=_unpack(vec, "b_ada", b_ada.shape), norm_w=_unpack(vec, "norm_w", norm_w.shape),
                    b_in=_unpack(vec, "b_in", b_in.shape), conv_w=cw[None],
                    conv_b=_unpack(vec, "conv_b", conv_b.shape), rpb=_unpack(vec, "rpb", rpb.shape),
                    ml_norm_w=_unpack(vec, "ml_norm_w", ml_norm_w.shape),
                    final_norm_w=_unpack(vec, "final_norm_w", final_norm_w.shape))

    loss = _unpack(sg, "loss", ())
    order = ("w_ada", "b_ada", "norm_w", "w_in", "b_in", "conv_w", "conv_b", "rpb", "ml_norm_w", "w_out",
             "final_norm_w")
    outs = []
    for vec, big in ((sg, (g_w_ada, g_w_in_s, g_w_out_s)), (sd, (d_w_ada, d_w_in, d_w_out)),
                     (sm, (nm_w_ada, nm_w_in, nm_w_out)), (sv, (nv_w_ada, nv_w_in, nv_w_out))):
        group = small_outs(vec)
        group.update(w_ada=big[0][None], w_in=big[1][None], w_out=big[2][None])
        outs.extend(group[name] for name in order)
    return (loss, grad_x, *outs)
```

```python
import functools

import numpy as np
import jax
import jax.numpy as jnp
from jax import lax
from jax.experimental import pallas as pl
from jax.experimental.pallas import tpu as pltpu

N_DEV = 8
D_MODEL = 1024
GRID_W = 64
NA_HEADS = 8
NA_HEAD_DIM = 64
NA_KH = 8
NA_KW = 16
NA_W = 512
ML_HEADS = 4
ML_HEAD_DIM = 128
ML_W = 512
ML_CHUNK = 512
CONV_W = 5
EPS = 1e-6
IN_W = 4624
IN_PAD = 4736
REST_W = IN_PAD - 3 * NA_W
GATE_COL = 3072
NEG = -1e30
NA_RB = 4
NA_WIN = 12
NA_SUB = 2
ML_CB = 1
ADAM_LR = 0.001
ADAM_B1 = 0.9
ADAM_B2 = 0.999
ADAM_EPS = 1e-08
ADAM_WD = 0.01
ADAM_STEP = 10
VMEM_LIMIT = 56 * 1024 * 1024

_F32 = jnp.float32
_BF16 = jnp.bfloat16
_HI = lax.Precision.HIGHEST


def _cparams(sem=None):
    return pltpu.CompilerParams(dimension_semantics=sem, vmem_limit_bytes=VMEM_LIMIT)


def _nt(a, b):
    return lax.dot_general(a, b, (((1,), (1,)), ((), ())), preferred_element_type=_F32)


def _tn(a, b):
    return lax.dot_general(a, b, (((0,), (0,)), ((), ())), preferred_element_type=_F32)


def _nn(a, b):
    return jnp.dot(a, b, preferred_element_type=_F32)


def _sigmoid(x):
    return 1.0 / (1.0 + jnp.exp(-x))


def _silu(x):
    return x * _sigmoid(x)


def _dsilu(x):
    s = _sigmoid(x)
    return s * (1.0 + x * (1.0 - s))


def _exchange(arrs, scatter, name):
    n = len(arrs)
    out_shape = []
    for a, sc in zip(arrs, scatter):
        blk = a.shape[1:] if sc else a.shape
        out_shape.append(jax.ShapeDtypeStruct((N_DEV,) + tuple(blk), a.dtype))

    def body(*refs):
        ins = refs[:n]
        outs = refs[n:2 * n]
        send_sems, recv_sems, local_sems = refs[2 * n:]
        x, y, c = lax.axis_index("x"), lax.axis_index("y"), lax.axis_index("c")
        me = 4 * x + 2 * y + c
        local, sends, recvs = [], [], []
        for a in range(n):
            own = ins[a].at[me] if scatter[a] else ins[a]
            cp = pltpu.make_async_copy(own, outs[a].at[me], local_sems.at[a])
            cp.start()
            local.append(cp)
            for k in range(1, N_DEV):
                px = 1 - x if k & 4 else x
                py = 1 - y if k & 2 else y
                pc = 1 - c if k & 1 else c
                p = 4 * px + 2 * py + pc
                src = ins[a].at[p] if scatter[a] else ins[a]
                snd = pltpu.make_async_remote_copy(
                    src_ref=src, dst_ref=outs[a].at[me],
                    send_sem=send_sems.at[a, k - 1], recv_sem=recv_sems.at[a, k - 1],
                    device_id=(px, py, pc), device_id_type=pl.DeviceIdType.MESH)
                snd.start()
                sends.append(snd)
                rcv = pltpu.make_async_remote_copy(
                    src_ref=src, dst_ref=outs[a].at[p],
                    send_sem=send_sems.at[a, k - 1], recv_sem=recv_sems.at[a, k - 1],
                    device_id=(px, py, pc), device_id_type=pl.DeviceIdType.MESH)
                recvs.append(rcv)
        for rcv in recvs:
            rcv.wait_recv()
        for snd in sends:
            snd.wait_send()
        for cp in local:
            cp.wait()

    any_spec = pl.BlockSpec(memory_space=pl.ANY)
    res = pl.pallas_call(
        body, name=name, out_shape=tuple(out_shape),
        in_specs=[any_spec] * n, out_specs=tuple([any_spec] * n),
        scratch_shapes=[pltpu.SemaphoreType.DMA((n, N_DEV - 1)),
                        pltpu.SemaphoreType.DMA((n, N_DEV - 1)),
                        pltpu.SemaphoreType.DMA((n,))],
    )(*arrs)
    return list(res)


def _peer(k):
    x, y, c = lax.axis_index("x"), lax.axis_index("y"), lax.axis_index("c")
    px = 1 - x if k & 4 else x
    py = 1 - y if k & 2 else y
    pc = 1 - c if k & 1 else c
    return (px, py, pc), 4 * px + 2 * py + pc, 4 * x + 2 * y + c


def _scatter_copy(srcs, lands, send_sems, recv_sems, a, k, receive, scatter):
    dev, p, me = _peer(k)
    return pltpu.make_async_remote_copy(
        src_ref=srcs[a].at[p] if scatter else srcs[a], dst_ref=lands[a].at[p if receive else me],
        send_sem=send_sems[a * (N_DEV - 1) + k - 1], recv_sem=recv_sems[a * (N_DEV - 1) + k - 1],
        device_id=dev, device_id_type=pl.DeviceIdType.MESH)


def _scatter_start(arrs, name, scatter=True):
    n = len(arrs)
    ns = n * (N_DEV - 1)
    hbm = pl.BlockSpec(memory_space=pltpu.HBM)
    sem = pl.BlockSpec(memory_space=pltpu.SEMAPHORE)

    def body(*refs):
        srcs, lands = refs[:n], refs[n:2 * n]
        send_sems, recv_sems = refs[2 * n:2 * n + ns], refs[2 * n + ns:2 * n + 2 * ns]
        token = refs[-1]
        for a in range(n):
            for k in range(1, N_DEV):
                _scatter_copy(srcs, lands, send_sems, recv_sems, a, k, False, scatter).start()
        token[...] = jnp.zeros_like(token)

    land_shapes = [a.shape if scatter else (N_DEV,) + a.shape for a in arrs]
    buffers = [pltpu.HBM(a.shape, a.dtype) for a in arrs]
    land_buffers = [pltpu.HBM(s, a.dtype) for s, a in zip(land_shapes, arrs)]
    sems = [pltpu.SemaphoreType.DMA(()) for _ in range(2 * ns)]
    res = pl.pallas_call(
        body, name=name,
        out_shape=(*sems, *buffers, *land_buffers, jax.ShapeDtypeStruct((8, 128), _F32)),
        in_specs=[hbm] * (2 * n),
        out_specs=(*([sem] * (2 * ns)), *([hbm] * (2 * n)), pl.BlockSpec(memory_space=pltpu.VMEM)),
        input_output_aliases={i: 2 * ns + i for i in range(2 * n)},
        compiler_params=pltpu.CompilerParams(has_side_effects=pltpu.SideEffectType.DATAFLOW_SIDE_EFFECTING),
    )(*[pltpu.with_memory_space_constraint(a, pltpu.HBM) for a in arrs],
      *[pltpu.with_memory_space_constraint(jnp.zeros(s, a.dtype), pltpu.HBM) for s, a in zip(land_shapes, arrs)])
    res = list(res)
    return (res[:ns], res[ns:2 * ns], res[2 * ns:2 * ns + n], res[2 * ns + n:2 * ns + 2 * n], res[-1])


def _scatter_wait(started, after, name, scatter=True):
    send_sems, recv_sems, srcs, lands, _ = started
    n = len(srcs)
    ns = len(send_sems)
    hbm = pl.BlockSpec(memory_space=pltpu.HBM)
    sem = pl.BlockSpec(memory_space=pltpu.SEMAPHORE)

    def body(*refs):
        src_refs, land_refs = refs[:n], refs[n:2 * n]
        s_sems, r_sems = refs[2 * n:2 * n + ns], refs[2 * n + ns:2 * n + 2 * ns]
        for a in range(n):
            for k in range(1, N_DEV):
                _scatter_copy(src_refs, land_refs, s_sems, r_sems, a, k, False, scatter).wait_send()
                _scatter_copy(src_refs, land_refs, s_sems, r_sems, a, k, True, scatter).wait_recv()

    buffers = [pltpu.HBM(a.shape, a.dtype) for a in list(srcs) + list(lands)]
    res = pl.pallas_call(
        body, name=name, out_shape=tuple(buffers),
        in_specs=[hbm] * (2 * n) + [sem] * (2 * ns) + [pl.BlockSpec(memory_space=pl.ANY)],
        out_specs=tuple([hbm] * (2 * n)),
        input_output_aliases={i: i for i in range(2 * n)},
        compiler_params=pltpu.CompilerParams(has_side_effects=pltpu.SideEffectType.DATAFLOW_SIDE_EFFECTING),
    )(*srcs, *lands, *send_sems, *recv_sems, after)
    return list(res[:n]), list(res[n:])


def _mod_part(c_all, w_ada, b_my):
    def body(c_ref, w_ref, b_ref, o_ref):
        o_ref[...] = jnp.dot(_silu(c_ref[...]), w_ref[...], precision=_HI,
                             preferred_element_type=_F32) + b_ref[...]

    return pl.pallas_call(
        body, name="mod_part",
        out_shape=jax.ShapeDtypeStruct((N_DEV, w_ada.shape[1]), _F32),
        compiler_params=_cparams(),
    )(c_all, w_ada, b_my)


def _inproj_fwd(x, scale1p, shift, norm_w, w_in_bf, b_in_pad):
    T = x.shape[1]
    tm = 256
    n_q = 3 * NA_W

    def body(x_ref, sc_ref, sh_ref, nw_ref, w_ref, b_ref, qkv_ref, rest_ref, h_ref):
        xv = x_ref[...]
        r = lax.rsqrt(jnp.mean(xv * xv, axis=-1, keepdims=True) + EPS)
        h = xv * r * nw_ref[...] * sc_ref[...] + sh_ref[...]
        hb = h.astype(_BF16)
        h_ref[...] = h.T.astype(_BF16)
        for n0 in range(0, IN_PAD, 512):
            wd = min(512, IN_PAD - n0)
            acc = _nn(hb, w_ref[:, n0:n0 + wd]) + b_ref[:, n0:n0 + wd]
            if n0 == 0:
                acc = acc * (NA_HEAD_DIM ** -0.5)
            if n0 < n_q:
                qkv_ref[:, n0:n0 + wd] = acc.astype(_BF16)
            else:
                rest_ref[:, n0 - n_q:n0 - n_q + wd] = acc

    row = lambda w: pl.BlockSpec((1, w), lambda i: (0, 0))
    return pl.pallas_call(
        body, name="inproj_fwd", grid=(T // tm,),
        in_specs=[pl.BlockSpec((None, tm, D_MODEL), lambda i: (0, i, 0)), row(D_MODEL), row(D_MODEL), row(D_MODEL),
                  pl.BlockSpec((D_MODEL, IN_PAD), lambda i: (0, 0)), row(IN_PAD)],
        out_specs=(pl.BlockSpec((tm, n_q), lambda i: (i, 0)),
                   pl.BlockSpec((tm, REST_W), lambda i: (i, 0)),
                   pl.BlockSpec((D_MODEL, tm), lambda i: (0, i))),
        out_shape=(jax.ShapeDtypeStruct((T, n_q), _BF16),
                   jax.ShapeDtypeStruct((T, REST_W), _F32),
                   jax.ShapeDtypeStruct((D_MODEL, T), _BF16)),
        compiler_params=_cparams(("arbitrary",)),
    )(x, scale1p, shift, norm_w, w_in_bf, b_in_pad)


def _na_class_rows(rows):
    nb = rows // NA_RB
    out = []
    for rb in (0, min(1, nb - 1), nb - 1):
        ws = int(np.clip(NA_RB * rb - 4, 0, rows - NA_WIN))
        out.append((NA_RB * rb + np.arange(NA_RB), ws + np.arange(NA_WIN)))
    return out


def _na_pair_index(rows, qrows, krows):
    start = lambda r: np.clip(r - NA_KH // 2, 0, rows - NA_KH)
    col = np.arange(GRID_W)
    cstart = np.clip(col - NA_KW // 2, 0, GRID_W - NA_KW)
    dy = krows[None, :] - qrows[:, None] + NA_KH - 1
    vr = (krows[None, :] >= start(qrows)[:, None]) & (krows[None, :] < start(qrows)[:, None] + NA_KH)
    dx = np.clip(col[None, :] - col[:, None], -(NA_KW - 1), NA_KW - 1) + NA_KW - 1
    vc = (col[None, :] >= cstart[:, None]) & (col[None, :] < cstart[:, None] + NA_KW)
    nq, nk = len(qrows), len(krows)
    dy4 = np.broadcast_to(np.clip(dy, 0, 2 * NA_KH - 2)[:, None, :, None], (nq, GRID_W, nk, GRID_W))
    dx4 = np.broadcast_to(dx[None, :, None, :], (nq, GRID_W, nk, GRID_W))
    valid = vr[:, None, :, None] & vc[None, :, None, :]
    idx = (dy4 * (2 * NA_KW - 1) + dx4).reshape(nq * GRID_W, nk * GRID_W)
    return idx.astype(np.int32), valid.reshape(nq * GRID_W, nk * GRID_W), (dy, vr, dx, vc)


def _na_half_slabs(rpb):
    _, _, (_, _, dx, vc) = _na_pair_index(NA_WIN, np.arange(1), np.arange(1))
    qc, kc = np.meshgrid(np.arange(GRID_W), np.arange(GRID_W), indexing="ij")
    consts = []
    for right in (False, True):
        pos = (qc * 128 + (GRID_W if right else 0) + kc).reshape(-1)
        oh = np.zeros((32, GRID_W * 128), np.float32)
        oh[dx[qc, kc].reshape(-1), pos] = 1.0
        col_neg = np.zeros((1, GRID_W * 128), np.float32)
        col_neg[0, pos] = np.where(vc[qc, kc].reshape(-1), 0.0, NEG)
        half = np.zeros((1, GRID_W * 128), np.float32)
        half[0, pos] = 1.0
        consts += [jnp.asarray(oh), jnp.asarray(col_neg), jnp.asarray(half)]
    row_neg = np.where(np.arange(NA_HEADS * 16) % 16 == 15, NEG, 0.0).astype(np.float32).reshape(-1, 1)
    rp = jnp.pad(rpb, ((0, 0), (0, 1), (0, 1))).reshape(NA_HEADS * 16, 32)

    def body(*refs):
        r_ref, rn_ref = refs[0], refs[1]
        for t in range(2):
            oh_ref, cn_ref, half_ref = refs[2 + 3 * t:5 + 3 * t]
            refs[8 + t][...] = (jnp.dot(r_ref[...], oh_ref[...], precision=_HI, preferred_element_type=_F32)
                                + cn_ref[...] + rn_ref[...] * half_ref[...])

    outs = pl.pallas_call(
        body, name="na_half_slabs",
        out_shape=tuple(jax.ShapeDtypeStruct((NA_HEADS * 16, GRID_W * 128), _F32) for _ in range(2)),
        compiler_params=_cparams(),
    )(rp, jnp.asarray(row_neg), *consts)
    return [o.reshape(NA_HEADS, 16, GRID_W, 128) for o in outs]


def _na_bias_tables(rpb, rows):
    left, right = _na_half_slabs(rpb)
    didx = []
    for blk, win in _na_class_rows(rows):
        _, _, (dy, vr, _, _) = _na_pair_index(rows, blk, win)
        didx.append(np.where(vr, dy, 15))

    def body(l_ref, r_ref, b_ref):
        for ci, tab in enumerate(didx):
            for a in range(NA_RB):
                for j in range(NA_WIN // 2):
                    b_ref[ci, 0, a * GRID_W:(a + 1) * GRID_W, j * 128:(j + 1) * 128] = (
                        l_ref[0, int(tab[a, 2 * j])] + r_ref[0, int(tab[a, 2 * j + 1])])

    slab = pl.BlockSpec((1, 16, GRID_W, 128), lambda h: (h, 0, 0, 0))
    return pl.pallas_call(
        body, name="na_tables", grid=(NA_HEADS,),
        in_specs=[slab] * 2,
        out_specs=pl.BlockSpec((3, 1, NA_RB * GRID_W, NA_WIN * GRID_W), lambda h: (0, h, 0, 0)),
        out_shape=jax.ShapeDtypeStruct((3, NA_HEADS, NA_RB * GRID_W, NA_WIN * GRID_W), _F32),
        compiler_params=_cparams(("arbitrary",)),
    )(left, right)


def _stack_heads(x, first):
    zero = jnp.zeros_like(x)
    return jnp.concatenate([jnp.where(first, x, zero), jnp.where(first, zero, x)], axis=0)


def _na_sub(rb, u, rows):
    sb = NA_SUB * rb + u
    nb = rows // NA_RB
    cls = jnp.where(sb == 0, 0, jnp.where(sb == nb - 1, 2, 1))
    ws = pl.multiple_of(jnp.clip(NA_RB * sb - 4, 0, rows - NA_WIN) * GRID_W, 256)
    return cls, ws


def _na_fwd(qkv, bias, T):
    rows = T // GRID_W
    tq = NA_RB * GRID_W
    tw = NA_WIN * GRID_W
    ts = NA_SUB * tq

    def body(q_ref, k_ref, v_ref, b_ref, o_ref, l_ref):
        rb = pl.program_id(1)
        lane = lax.broadcasted_iota(jnp.int32, (1, 128), 1)
        first = lane < NA_HEAD_DIM
        for u in range(NA_SUB):
            cls, ws = _na_sub(rb, u, rows)
            kw = k_ref[pl.ds(ws, tw), :]
            vw = v_ref[pl.ds(ws, tw), :]
            q2 = _stack_heads(q_ref[u * tq:(u + 1) * tq, :], first)
            s = _nt(q2, kw) + b_ref[cls].reshape(2 * tq, tw)
            m = jnp.max(s, axis=1, keepdims=True)
            p = jnp.exp(s - m)
            l = jnp.sum(p, axis=1, keepdims=True)
            o2 = _nn(p.astype(_BF16), vw) / l
            lse2 = m + jnp.log(l)
            o_ref[u * tq:(u + 1) * tq, :] = jnp.where(first, o2[:tq], o2[tq:])
            l_ref[u * tq:(u + 1) * tq, :] = jnp.where(first, lse2[:tq], lse2[tq:])

    blk = lambda off: pl.BlockSpec((ts, 128), lambda hp, rb: (rb, off + hp))
    whole = lambda off: pl.BlockSpec((T, 128), lambda hp, rb: (0, off + hp))
    return pl.pallas_call(
        body, name="na_fwd", grid=(NA_HEADS // 2, T // ts),
        in_specs=[blk(0), whole(4), whole(8),
                  pl.BlockSpec((3, 2, tq, tw), lambda hp, rb: (0, hp, 0, 0))],
        out_specs=(blk(0), blk(0)),
        out_shape=(jax.ShapeDtypeStruct((T, NA_W), _F32), jax.ShapeDtypeStruct((T, NA_W), _F32)),
        compiler_params=_cparams(("arbitrary", "arbitrary")),
    )(qkv, qkv, qkv, bias)


def _na_bwd(qkv, bias, o, d_o, lse, T):
    rows = T // GRID_W
    tq = NA_RB * GRID_W
    tw = NA_WIN * GRID_W
    ts = NA_SUB * tq

    def body(q_ref, k_ref, v_ref, b_ref, o_ref, do_ref, l_ref, dq_ref, dk_ref, dv_ref, db_ref):
        rb = pl.program_id(1)
        lane = lax.broadcasted_iota(jnp.int32, (1, 128), 1)
        first = lane < NA_HEAD_DIM

        @pl.when(rb == 0)
        def _():
            db_ref[...] = jnp.zeros_like(db_ref)
            dk_ref[...] = jnp.zeros_like(dk_ref)
            dv_ref[...] = jnp.zeros_like(dv_ref)

        for u in range(NA_SUB):
            cls, ws = _na_sub(rb, u, rows)
            kw = k_ref[pl.ds(ws, tw), :]
            vw = v_ref[pl.ds(ws, tw), :]
            sl = slice(u * tq, (u + 1) * tq)
            q = q_ref[sl, :]
            d_ov = do_ref[sl, :]
            prod = d_ov.astype(_F32) * o_ref[sl, :]
            lse_v = l_ref[sl, :]
            dqs = []
            dk_win = jnp.zeros((tw, 128), _F32)
            dv_win = jnp.zeros((tw, 128), _F32)
            for hh in range(2):
                msk = first if hh == 0 else jnp.logical_not(first)
                c0 = hh * NA_HEAD_DIM
                qm = jnp.where(msk, q, jnp.zeros_like(q))
                dom = jnp.where(msk, d_ov, jnp.zeros_like(d_ov))
                s = _nt(qm, kw) + b_ref[cls, hh]
                p = jnp.exp(s - lse_v[:, c0:c0 + 1])
                dp = _nt(dom, vw)
                delta = jnp.sum(jnp.where(msk, prod, 0.0), axis=1, keepdims=True)
                ds = p * (dp - delta)
                db_ref[cls, hh] += ds
                dsb = ds.astype(_BF16)
                dqs.append(_nn(dsb, kw) * (NA_HEAD_DIM ** -0.5))
                dk_win = dk_win + _tn(dsb, qm)
                dv_win = dv_win + _tn(p.astype(_BF16), dom)
            dq_ref[sl, :] = jnp.where(first, dqs[0], dqs[1]).astype(_BF16)
            dk_ref[pl.ds(ws, tw), :] += dk_win
            dv_ref[pl.ds(ws, tw), :] += dv_win

    once = pl.Buffered(1)
    blk = lambda off: pl.BlockSpec((ts, 128), lambda hp, rb: (rb, off + hp))
    whole = lambda off: pl.BlockSpec((T, 128), lambda hp, rb: (0, off + hp), pipeline_mode=once)
    tab = pl.BlockSpec((3, 2, tq, tw), lambda hp, rb: (0, hp, 0, 0), pipeline_mode=once)
    return pl.pallas_call(
        body, name="na_bwd", grid=(NA_HEADS // 2, T // ts),
        in_specs=[blk(0), whole(4), whole(8), tab, blk(0), blk(0), blk(0)],
        out_specs=(blk(0), whole(0), whole(0), tab),
        out_shape=(jax.ShapeDtypeStruct((T, NA_W), _BF16), jax.ShapeDtypeStruct((T, NA_W), _F32),
                   jax.ShapeDtypeStruct((T, NA_W), _F32), jax.ShapeDtypeStruct(bias.shape, _F32)),
        compiler_params=_cparams(("arbitrary", "arbitrary")),
    )(qkv, qkv, qkv, bias, o, d_o, lse)


def _rpb_grad(dbias, rows):
    tw = NA_WIN * GRID_W
    lanes = 16 * GRID_W
    offs = [int(win[0] - blk[0] + NA_KH - 1) for blk, win in _na_class_rows(rows)]

    def body(x_ref, g_ref):
        sub = lax.broadcasted_iota(jnp.int32, (NA_RB, 1), 0)
        qc = lax.broadcasted_iota(jnp.int32, (NA_RB * GRID_W, 1), 0) % GRID_W
        tot = jnp.zeros((NA_RB, lanes), _F32)
        for ci in range(3):
            xv = x_ref[ci, 0]
            for bit in range(6):
                xv = jnp.where(((qc >> bit) & 1) == 1, pltpu.roll(xv, tw - (1 << bit), 1), xv)
            acc = pltpu.roll(jnp.sum(xv.reshape(NA_RB, GRID_W, tw), axis=1), NA_KW, 1)
            acc = jnp.concatenate([acc, jnp.zeros((NA_RB, lanes - tw), _F32)], axis=1)
            for a in range(NA_RB):
                tot = tot + jnp.where(sub == a, pltpu.roll(acc, (GRID_W * (offs[ci] - a)) % lanes, 1), 0.0)
        g_ref[0] = jnp.broadcast_to(jnp.sum(tot, axis=0, keepdims=True), (8, lanes))

    g = pl.pallas_call(
        body, name="rpb_grad", grid=(NA_HEADS,),
        in_specs=[pl.BlockSpec((3, 1) + dbias.shape[2:], lambda h: (0, h, 0, 0))],
        out_specs=pl.BlockSpec((1, 8, lanes), lambda h: (h, 0, 0)),
        out_shape=jax.ShapeDtypeStruct((NA_HEADS, 8, lanes), _F32),
        compiler_params=_cparams(("arbitrary",)),
    )(dbias)
    return g[:, 0].reshape(NA_HEADS, 16, GRID_W)[:, :2 * NA_KH - 1, 1:2 * NA_KW]


def _halo_specs(tm, width, col_of, T, order):
    hb = tm // 8
    last = T // 8 - 1
    if order == "ij":
        cur = pl.BlockSpec((tm, width), lambda i, j: (i, col_of(j)))
        prev = pl.BlockSpec((8, width), lambda i, j: (jnp.maximum(i * hb - 1, 0), col_of(j)))
        nxt = pl.BlockSpec((8, width), lambda i, j: (jnp.minimum((i + 1) * hb, last), col_of(j)))
    else:
        cur = pl.BlockSpec((tm, width), lambda j, i: (i, col_of(j)))
        prev = pl.BlockSpec((8, width), lambda j, i: (jnp.maximum(i * hb - 1, 0), col_of(j)))
        nxt = pl.BlockSpec((8, width), lambda j, i: (jnp.minimum((i + 1) * hb, last), col_of(j)))
    return [prev, cur, nxt]


def _extend(prev_ref, cur_ref, next_ref, i, n_i):
    prev = jnp.where(i > 0, prev_ref[...], 0.0)
    nxt = jnp.where(i < n_i - 1, next_ref[...], 0.0)
    return jnp.concatenate([prev, cur_ref[...], nxt], axis=0)


def _conv_fwd(rest, conv_w, conv_b, T):
    tm = 512
    n_i = T // tm
    n = tm + 16

    def body(p_ref, c_ref, n_ref, w_ref, b_ref, o_ref):
        i = pl.program_id(0)
        ext = _extend(p_ref, c_ref, n_ref, i, n_i)
        acc = jnp.zeros((tm, 512), _F32) + b_ref[...]
        for j in range(CONV_W):
            acc = acc + w_ref[j:j + 1, :] * pltpu.roll(ext, (2 - j) % n, 0)[8:8 + tm]
        o_ref[...] = _silu(acc)

    return pl.pallas_call(
        body, name="conv_fwd", grid=(n_i, 2),
        in_specs=_halo_specs(tm, 512, lambda j: 1 + j, T, "ij")
        + [pl.BlockSpec((8, 512), lambda i, j: (0, j)), pl.BlockSpec((1, 512), lambda i, j: (0, j))],
        out_specs=pl.BlockSpec((tm, 512), lambda i, j: (i, j)),
        out_shape=jax.ShapeDtypeStruct((T, 2 * ML_W), _F32),
        compiler_params=_cparams(("arbitrary", "arbitrary")),
    )(rest, rest, rest, conv_w, conv_b)


def _conv_bwd(rest, conv_w, conv_b, da_f, da_b, T):
    tm = 512
    n_i = T // tm
    n = tm + 16

    def body(up, uc, un, fp, fc, fn, bp, bc, bn, w_ref, b_ref, du_ref, dw_ref):
        i = pl.program_id(1)
        ext_u = _extend(up, uc, un, i, n_i)
        ext_da = _extend(fp, fc, fn, i, n_i) + _extend(bp, bc, bn, i, n_i)
        shifted = [pltpu.roll(ext_u, (2 - j) % n, 0) for j in range(CONV_W)]
        pre = jnp.zeros((n, 512), _F32) + b_ref[...]
        for j in range(CONV_W):
            pre = pre + w_ref[j:j + 1, :] * shifted[j]
        gidx = i * tm - 8 + lax.broadcasted_iota(jnp.int32, (n, 1), 0)
        dpre = jnp.where((gidx >= 0) & (gidx < T), ext_da * _dsilu(pre), 0.0)
        du = jnp.zeros((tm, 512), _F32)
        for j in range(CONV_W):
            du = du + w_ref[j:j + 1, :] * pltpu.roll(dpre, (j - 2) % n, 0)[8:8 + tm]
        du_ref[...] = du.astype(_BF16)
        dpc = dpre[8:8 + tm]
        parts = [jnp.sum(dpc * shifted[j][8:8 + tm], axis=0, keepdims=True) for j in range(CONV_W)]
        parts.append(jnp.sum(dpc, axis=0, keepdims=True))
        parts.append(jnp.zeros((2, 512), _F32))
        upd = jnp.concatenate(parts, axis=0)

        @pl.when(i == 0)
        def _():
            dw_ref[...] = upd

        @pl.when(i > 0)
        def _():
            dw_ref[...] += upd

    return pl.pallas_call(
        body, name="conv_bwd", grid=(2, n_i),
        in_specs=_halo_specs(tm, 512, lambda j: 1 + j, T, "ji")
        + _halo_specs(tm, 512, lambda j: j, T, "ji") + _halo_specs(tm, 512, lambda j: j, T, "ji")
        + [pl.BlockSpec((8, 512), lambda j, i: (0, j)), pl.BlockSpec((1, 512), lambda j, i: (0, j))],
        out_specs=(pl.BlockSpec((tm, 512), lambda j, i: (i, j)), pl.BlockSpec((8, 512), lambda j, i: (0, j))),
        out_shape=(jax.ShapeDtypeStruct((T, 2 * ML_W), _BF16), jax.ShapeDtypeStruct((8, 2 * ML_W), _F32)),
        compiler_params=_cparams(("arbitrary", "arbitrary")),
    )(rest, rest, rest, da_f, da_f, da_f, da_b, da_b, da_b, conv_w, conv_b)


def _scan_rows(x, suffix):
    L = x.shape[0]
    row = lax.broadcasted_iota(jnp.int32, (L, 1), 0)
    step = 1
    while step < L:
        if suffix:
            x = x + jnp.where(row < L - step, pltpu.roll(x, L - step, 0), 0.0)
        else:
            x = x + jnp.where(row >= step, pltpu.roll(x, step, 0), 0.0)
        step *= 2
    return x


def _ml_gates(gt, rev):
    L = gt.shape[0]
    ri = lax.broadcasted_iota(jnp.int32, (L, L), 0)
    ci = lax.broadcasted_iota(jnp.int32, (L, L), 1)
    mask = (ci >= ri) if rev else (ci <= ri)
    lf = jnp.minimum(gt, 0.0) - jnp.log(1.0 + jnp.exp(-jnp.abs(gt)))
    b = _scan_rows(lf, suffix=rev)
    return mask, b, b.T, gt.T


def _ml_head_gates(gt, gates, head, rev):
    _, b, b_t, gt_t = gates
    ci = (8 if rev else 0) + head
    cf = ci + ML_HEADS
    last = 0 if rev else gt.shape[0] - 1
    return dict(icol=gt[:, ci:ci + 1], b_col=b[:, cf:cf + 1], b_row=b_t[cf:cf + 1, :],
                i_row=gt_t[ci:ci + 1, :], bl=b[last:last + 1, cf:cf + 1])


def _ml_chunk(q, k, v, hg, mask, C, n, m):
    icol, b_col, b_row, bl = hg["icol"], hg["b_col"], hg["b_row"], hg["bl"]
    dlog = jnp.where(mask, b_col - b_row + hg["i_row"], NEG)
    m_t = jnp.maximum(b_col + m, jnp.max(dlog, axis=1, keepdims=True))
    dm = jnp.exp(dlog - m_t)
    ks = k * (ML_HEAD_DIM ** -0.5)
    qb, kb, vb = q.astype(_BF16), ks.astype(_BF16), v.astype(_BF16)
    s = _nt(qb, kb) * dm
    g = jnp.exp(b_col + m - m_t)
    qc = _nt(qb, C.astype(_BF16))
    num = _nn(s.astype(_BF16), vb) + g * qc
    qn = jnp.sum(q * n, axis=1, keepdims=True)
    den = jnp.sum(s, axis=1, keepdims=True) + g * qn
    e_m = jnp.exp(-m_t)
    nrm = jnp.maximum(jnp.abs(den), e_m)
    h = num / nrm
    a_col = bl - b_col + icol
    m_new = jnp.maximum(bl + m, jnp.max(a_col, axis=0, keepdims=True))
    decay = jnp.exp(bl + m - m_new)
    w = jnp.exp(a_col - m_new)
    c_new = decay * C + _tn((w * v).astype(_BF16), kb)
    n_new = decay * n + jnp.sum(w * ks, axis=0, keepdims=True)
    aux = dict(dm=dm, ks=ks, qb=qb, kb=kb, vb=vb, s=s, g=g, qc=qc, qn=qn,
               den=den, e_m=e_m, nrm=nrm, decay=decay, w=w)
    return h, c_new, n_new, m_new, aux


def _mlstm_fwd(qk_act, rest, T, rev):
    tb = ML_CB * ML_CHUNK
    nblk = T // tb
    nc = T // ML_CHUNK
    bi = (lambda i: nblk - 1 - i) if rev else (lambda i: i)

    def body(q_ref, k_ref, v_ref, g_ref, h_ref, cs_ref, ns_ref, ms_ref, c_scr, n_scr, m_scr):
        @pl.when(pl.program_id(0) == 0)
        def _():
            c_scr[...] = jnp.zeros_like(c_scr)
            n_scr[...] = jnp.zeros_like(n_scr)
            m_scr[...] = jnp.zeros_like(m_scr)

        def step(j, carry):
            c = (ML_CB - 1 - j) if rev else j
            r0 = pl.multiple_of(c * ML_CHUNK, ML_CHUNK)
            gt = g_ref[pl.ds(r0, ML_CHUNK), :]
            gates = _ml_gates(gt, rev)
            for hd in range(ML_HEADS):
                cols = slice(hd * ML_HEAD_DIM, (hd + 1) * ML_HEAD_DIM)
                C = c_scr[hd]
                n = n_scr[hd:hd + 1, :]
                mrow = m_scr[hd:hd + 1, :]
                cs_ref[c, hd] = C
                ns_ref[c, hd:hd + 1, :] = n
                ms_ref[c, hd:hd + 1, :] = mrow
                h, c_new, n_new, m_new, _ = _ml_chunk(
                    q_ref[pl.ds(r0, ML_CHUNK), cols], k_ref[pl.ds(r0, ML_CHUNK), cols],
                    v_ref[pl.ds(r0, ML_CHUNK), cols], _ml_head_gates(gt, gates, hd, rev), gates[0],
                    C, n, mrow[:, 0:1])
                h_ref[pl.ds(r0, ML_CHUNK), cols] = h
                c_scr[hd] = c_new
                n_scr[hd:hd + 1, :] = n_new
                m_scr[hd:hd + 1, :] = jnp.broadcast_to(m_new, (1, 128))
            return carry

        lax.fori_loop(0, ML_CB, step, 0)

    return pl.pallas_call(
        body, name="mlstm_fwd_rev" if rev else "mlstm_fwd", grid=(nblk,),
        in_specs=[pl.BlockSpec((tb, ML_W), lambda i: (bi(i), 0)),
                  pl.BlockSpec((tb, ML_W), lambda i: (bi(i), 1)),
                  pl.BlockSpec((tb, ML_W), lambda i: (bi(i), 3)),
                  pl.BlockSpec((tb, 128), lambda i: (bi(i), GATE_COL // 128))],
        out_specs=(pl.BlockSpec((tb, ML_W), lambda i: (bi(i), 0)),
                   pl.BlockSpec((ML_CB, ML_HEADS, 128, 128), lambda i: (bi(i), 0, 0, 0)),
                   pl.BlockSpec((ML_CB, ML_HEADS, 128), lambda i: (bi(i), 0, 0)),
                   pl.BlockSpec((ML_CB, ML_HEADS, 128), lambda i: (bi(i), 0, 0))),
        out_shape=(jax.ShapeDtypeStruct((T, ML_W), _F32),
                   jax.ShapeDtypeStruct((nc, ML_HEADS, 128, 128), _F32),
                   jax.ShapeDtypeStruct((nc, ML_HEADS, 128), _F32),
                   jax.ShapeDtypeStruct((nc, ML_HEADS, 128), _F32)),
        scratch_shapes=[pltpu.VMEM((ML_HEADS, 128, 128), _F32), pltpu.VMEM((8, 128), _F32),
                        pltpu.VMEM((8, 128), _F32)],
        compiler_params=_cparams(("arbitrary",)),
    )(qk_act, qk_act, rest, rest)


def _mlstm_bwd(qk_act, rest, d_h, cs, ns, ms, T, rev):
    tb = ML_CB * ML_CHUNK
    nblk = T // tb
    bi = (lambda i: i) if rev else (lambda i: nblk - 1 - i)

    def body(q_ref, k_ref, v_ref, g_ref, dh_ref, cs_ref, ns_ref, ms_ref,
             dqk_ref, dv_ref, dg_ref, dc_scr, dn_scr):
        @pl.when(pl.program_id(0) == 0)
        def _():
            dc_scr[...] = jnp.zeros_like(dc_scr)
            dn_scr[...] = jnp.zeros_like(dn_scr)

        def step(j, carry):
            c = j if rev else (ML_CB - 1 - j)
            r0 = pl.multiple_of(c * ML_CHUNK, ML_CHUNK)
            gt = g_ref[pl.ds(r0, ML_CHUNK), :]
            gates = _ml_gates(gt, rev)
            mask = gates[0]
            lane = lax.broadcasted_iota(jnp.int32, (1, 128), 1)
            sub = lax.broadcasted_iota(jnp.int32, (128, 1), 0)
            db_t = jnp.zeros((ML_CHUNK, 128), _F32)
            da_t = jnp.zeros((ML_CHUNK, 128), _F32)
            cs_rows = jnp.zeros((128, ML_CHUNK), _F32)
            dbl_t = jnp.zeros((1, 128), _F32)
            for hd in range(ML_HEADS):
                cols = slice(hd * ML_HEAD_DIM, (hd + 1) * ML_HEAD_DIM)
                ci = (8 if rev else 0) + hd
                cf = ci + ML_HEADS
                q = q_ref[pl.ds(r0, ML_CHUNK), cols]
                k = k_ref[pl.ds(r0, ML_CHUNK), cols]
                v = v_ref[pl.ds(r0, ML_CHUNK), cols]
                C = cs_ref[c, hd]
                n = ns_ref[c, hd:hd + 1, :]
                m = ms_ref[c, hd:hd + 1, :][:, 0:1]
                dcn = dc_scr[hd]
                dnn = dn_scr[hd:hd + 1, :]
                h, _, _, _, a = _ml_chunk(q, k, v, _ml_head_gates(gt, gates, hd, rev), mask, C, n, m)
                d_hv = dh_ref[pl.ds(r0, ML_CHUNK), cols]
                g, s, w, ks = a["g"], a["s"], a["w"], a["ks"]
                qb, kb, vb = a["qb"], a["kb"], a["vb"]
                dnum = d_hv / a["nrm"]
                hdot = jnp.sum(d_hv * h, axis=1, keepdims=True)
                dden = jnp.where(jnp.abs(a["den"]) >= a["e_m"], -hdot / a["nrm"] * jnp.sign(a["den"]), 0.0)
                dnb = dnum.astype(_BF16)
                d_s = _nt(dnb, vb) + dden
                r = d_s * s
                dsqk = (d_s * a["dm"]).astype(_BF16)
                cb = C.astype(_BF16)
                dq = _nn(dsqk, kb) + g * _nn(dnb, cb) + (dden * g) * n
                dk = _tn(dsqk, qb)
                dv = _tn(s.astype(_BF16), dnb)
                dg = jnp.sum(dnum * a["qc"], axis=1, keepdims=True) + dden * a["qn"]
                db_col = jnp.sum(r, axis=1, keepdims=True) + dg * g
                cs_rows = cs_rows + jnp.where((sub == ci) | (sub == cf), jnp.sum(r, axis=0, keepdims=True), 0.0)
                dc_chunk = _tn((g * dnum).astype(_BF16), qb)
                dn_chunk = jnp.sum((dden * g) * q, axis=0, keepdims=True)
                dcb = dcn.astype(_BF16)
                vdc = _nn(vb, dcb)
                kdc = _nt(kb, dcb)
                dw = jnp.sum(vdc * ks, axis=1, keepdims=True) + jnp.sum(ks * dnn, axis=1, keepdims=True)
                dv = dv + w * kdc
                dk = dk + w * vdc + w * dnn
                da = dw * w
                ddecay = (jnp.sum(jnp.sum(dcn * C, axis=1, keepdims=True), axis=0, keepdims=True)
                          + jnp.sum(dnn * n, axis=1, keepdims=True))
                dbl = ddecay * a["decay"] + jnp.sum(da, axis=0, keepdims=True)
                db_t = db_t + jnp.where(lane == cf, db_col - da, 0.0)
                da_t = da_t + jnp.where(lane == ci, da, 0.0)
                dbl_t = dbl_t + jnp.where(lane == cf, dbl, 0.0)
                dc_scr[hd] = dc_chunk + a["decay"] * dcn
                dn_scr[hd:hd + 1, :] = dn_chunk + a["decay"] * dnn
                dqk_ref[pl.ds(r0, ML_CHUNK), cols] = dq
                dqk_ref[pl.ds(r0, ML_CHUNK), slice(ML_W + hd * 128, ML_W + (hd + 1) * 128)] = dk * (ML_HEAD_DIM ** -0.5)
                dv_ref[pl.ds(r0, ML_CHUNK), cols] = dv.astype(_BF16)
            lo = 8 if rev else 0
            is_i = (lane >= lo) & (lane < lo + ML_HEADS)
            is_f = (lane >= lo + ML_HEADS) & (lane < lo + 2 * ML_HEADS)
            cs_t = cs_rows.T
            db_all = db_t - jnp.where(is_f, cs_t, 0.0)
            dlf = _scan_rows(db_all, suffix=not rev) + dbl_t
            dg_ref[pl.ds(r0, ML_CHUNK), :] = (da_t + jnp.where(is_i, cs_t, 0.0)
                                               + jnp.where(is_f, dlf * _sigmoid(-gt), 0.0))
            return carry

        lax.fori_loop(0, ML_CB, step, 0)

    return pl.pallas_call(
        body, name="mlstm_bwd_rev" if rev else "mlstm_bwd", grid=(nblk,),
        in_specs=[pl.BlockSpec((tb, ML_W), lambda i: (bi(i), 0)),
                  pl.BlockSpec((tb, ML_W), lambda i: (bi(i), 1)),
                  pl.BlockSpec((tb, ML_W), lambda i: (bi(i), 3)),
                  pl.BlockSpec((tb, 128), lambda i: (bi(i), GATE_COL // 128)),
                  pl.BlockSpec((tb, ML_W), lambda i: (bi(i), 0)),
                  pl.BlockSpec((ML_CB, ML_HEADS, 128, 128), lambda i: (bi(i), 0, 0, 0)),
                  pl.BlockSpec((ML_CB, ML_HEADS, 128), lambda i: (bi(i), 0, 0)),
                  pl.BlockSpec((ML_CB, ML_HEADS, 128), lambda i: (bi(i), 0, 0))],
        out_specs=(pl.BlockSpec((tb, 2 * ML_W), lambda i: (bi(i), 0)),
                   pl.BlockSpec((tb, ML_W), lambda i: (bi(i), 0)),
                   pl.BlockSpec((tb, 128), lambda i: (bi(i), 0))),
        out_shape=(jax.ShapeDtypeStruct((T, 2 * ML_W), _F32), jax.ShapeDtypeStruct((T, ML_W), _BF16),
                   jax.ShapeDtypeStruct((T, 128), _F32)),
        scratch_shapes=[pltpu.VMEM((ML_HEADS, 128, 128), _F32), pltpu.VMEM((8, 128), _F32)],
        compiler_params=_cparams(("arbitrary",)),
    )(qk_act, qk_act, rest, rest, d_h, cs, ns, ms)


def _post(x, target, o_na, rest, h_f, h_b, gate, ml_norm_w, final_w, w_out_bf, T):
    tm = 256
    n_i = T // tm

    def body(x_ref, t_ref, o_ref, zna_ref, hf_ref, hb_ref, mo_ref, mz_ref, gate_ref, mw_ref, fw_ref, w_ref,
             dx1_ref, do_ref, dzna_ref, dh_ref, dmo_ref, dmz_ref, dwo_ref, vec_ref):
        i = pl.program_id(0)
        gate_v = gate_ref[...]
        fw = fw_ref[...]
        zna = zna_ref[...]
        o = o_ref[...]
        na_out = o * _silu(zna)
        hsum = hf_ref[...] + hb_ref[...]
        sg = _sigmoid(mo_ref[...])
        hm = hsum * sg
        mz = mz_ref[...]
        smz = _silu(mz)
        hn_l, rstd_l, ml_l = [], [], []
        for hd in range(ML_HEADS):
            cols = slice(hd * 128, (hd + 1) * 128)
            hh = hm[:, cols]
            mu = jnp.mean(hh, axis=-1, keepdims=True)
            var = jnp.mean(jnp.square(hh - mu), axis=-1, keepdims=True)
            rstd = lax.rsqrt(var + EPS)
            hn = (hh - mu) * rstd
            hn_l.append(hn)
            rstd_l.append(rstd)
            ml_l.append(hn * mw_ref[:, cols] * smz[:, cols])
        mix = jnp.concatenate([na_out] + ml_l, axis=1).astype(_BF16)
        y = _nn(mix, w_ref[...])
        x1 = x_ref[...] + gate_v * y
        r = lax.rsqrt(jnp.mean(x1 * x1, axis=-1, keepdims=True) + EPS)
        xhat = x1 * r
        out = xhat * fw
        err = out - t_ref[...]
        loss = 0.5 * jnp.sum(jnp.sum(err * err, axis=1, keepdims=True), axis=0, keepdims=True) / D_MODEL
        dout = err * (1.0 / D_MODEL)
        dfw = jnp.sum(dout * xhat, axis=0, keepdims=True)
        dxhat = dout * fw
        dx1 = r * (dxhat - xhat * jnp.mean(dxhat * xhat, axis=-1, keepdims=True))
        dx1_ref[...] = dx1
        dgate = jnp.sum(dx1 * y, axis=0, keepdims=True)
        dy = (dx1 * gate_v).astype(_BF16)
        dmix = _nt(dy, w_ref[...])
        dwo = _tn(mix, dy)
        dna = dmix[:, :NA_W]
        do_ref[...] = (dna * _silu(zna)).astype(_BF16)
        dzna_ref[...] = (dna * o * _dsilu(zna)).astype(_BF16)
        dmw_l = []
        for hd in range(ML_HEADS):
            cols = slice(hd * 128, (hd + 1) * 128)
            dml = dmix[:, NA_W + hd * 128:NA_W + (hd + 1) * 128]
            hn = hn_l[hd]
            mwv = mw_ref[:, cols]
            dmz_ref[:, cols] = (dml * hn * mwv * _dsilu(mz[:, cols])).astype(_BF16)
            dhn = dml * mwv * smz[:, cols]
            dmw_l.append(jnp.sum(dml * hn * smz[:, cols], axis=0, keepdims=True))
            dhm = rstd_l[hd] * (dhn - jnp.mean(dhn, axis=-1, keepdims=True)
                                - hn * jnp.mean(dhn * hn, axis=-1, keepdims=True))
            sgc = sg[:, cols]
            dh_ref[:, cols] = dhm * sgc
            dmo_ref[:, cols] = (dhm * hsum[:, cols] * sgc * (1.0 - sgc)).astype(_BF16)
        dmw = jnp.concatenate(dmw_l + [jnp.zeros((1, D_MODEL - ML_W), _F32)], axis=1)
        lane = lax.broadcasted_iota(jnp.int32, (1, D_MODEL), 1)
        vec = jnp.concatenate([dfw, dgate, dmw, jnp.where(lane == 0, loss, 0.0),
                               jnp.zeros((4, D_MODEL), _F32)], axis=0)

        @pl.when(i == 0)
        def _():
            dwo_ref[...] = dwo
            vec_ref[...] = vec

        @pl.when(i > 0)
        def _():
            dwo_ref[...] += dwo
            vec_ref[...] += vec

    tok = lambda w, j: pl.BlockSpec((tm, w), lambda i: (i, j))
    tok3 = pl.BlockSpec((None, tm, D_MODEL), lambda i: (0, i, 0))
    row = lambda w: pl.BlockSpec((1, w), lambda i: (0, 0))
    f32 = lambda w: jax.ShapeDtypeStruct((T, w), _F32)
    bf16 = lambda w: jax.ShapeDtypeStruct((T, w), _BF16)
    return pl.pallas_call(
        body, name="post", grid=(n_i,),
        in_specs=[tok3, tok3, tok(NA_W, 0), tok(NA_W, 0), tok(ML_W, 0), tok(ML_W, 0),
                  tok(ML_W, 4), tok(ML_W, 5), row(D_MODEL), row(ML_W), row(D_MODEL),
                  pl.BlockSpec((D_MODEL, D_MODEL), lambda i: (0, 0))],
        out_specs=(tok(D_MODEL, 0), tok(NA_W, 0), tok(NA_W, 0), tok(ML_W, 0), tok(ML_W, 0), tok(ML_W, 0),
                   pl.BlockSpec((D_MODEL, D_MODEL), lambda i: (0, 0)),
                   pl.BlockSpec((8, D_MODEL), lambda i: (0, 0))),
        out_shape=(f32(D_MODEL), bf16(NA_W), bf16(NA_W), f32(ML_W), bf16(ML_W),
                   bf16(ML_W), jax.ShapeDtypeStruct((D_MODEL, D_MODEL), _F32),
                   jax.ShapeDtypeStruct((8, D_MODEL), _F32)),
        compiler_params=_cparams(("arbitrary",)),
    )(x, target, o_na, rest, h_f, h_b, rest, rest, gate, ml_norm_w, final_w, w_out_bf)


def _section_specs(sections, tm):
    specs, args = [], []
    for _, width, parts in sections:
        for arr, cb in parts:
            specs.append(pl.BlockSpec((tm, width), functools.partial(lambda i, cb: (i, cb), cb=cb)))
            args.append(arr)
    return specs, args


def _section_values(sections, refs, dtype):
    vals, at = [], 0
    for _, _, parts in sections:
        v = refs[at][...]
        for r in refs[at + 1:at + len(parts)]:
            v = v.astype(_F32) + r[...].astype(_F32)
        at += len(parts)
        vals.append(v.astype(dtype))
    return vals


def _inproj_bwd_x(x, dx1, scale1p, norm_w, w_in_bf, sections, T):
    tm = 256
    sspecs, sargs = _section_specs(sections, tm)
    ns = len(sargs)

    def body(*refs):
        x_ref, dx1_ref, sc_ref, nw_ref, w_ref = refs[:5]
        srefs = refs[5:5 + ns]
        gx_ref, vec_ref = refs[5 + ns:]
        i = pl.program_id(0)
        vals = _section_values(sections, srefs, _BF16)
        dh = jnp.zeros((tm, D_MODEL), _F32)
        for (c0, width, _), val in zip(sections, vals):
            dh = dh + _nt(val, w_ref[:, c0:c0 + width])
        xv = x_ref[...]
        r = lax.rsqrt(jnp.mean(xv * xv, axis=-1, keepdims=True) + EPS)
        xhat = xv * r
        nw = nw_ref[...]
        dshift = jnp.sum(dh, axis=0, keepdims=True)
        dscale = jnp.sum(dh * xhat * nw, axis=0, keepdims=True)
        dhpre = dh * sc_ref[...]
        dnw = jnp.sum(dhpre * xhat, axis=0, keepdims=True)
        dxhat = dhpre * nw
        gx_ref[...] = dx1_ref[...] + r * (dxhat - xhat * jnp.mean(dxhat * xhat, axis=-1, keepdims=True))
        vec = jnp.concatenate([dshift, dscale, dnw, jnp.zeros((5, D_MODEL), _F32)], axis=0)

        @pl.when(i == 0)
        def _():
            vec_ref[...] = vec

        @pl.when(i > 0)
        def _():
            vec_ref[...] += vec

    row = pl.BlockSpec((1, D_MODEL), lambda i: (0, 0))
    tok = pl.BlockSpec((tm, D_MODEL), lambda i: (i, 0))
    tok3 = pl.BlockSpec((None, tm, D_MODEL), lambda i: (0, i, 0))
    return pl.pallas_call(
        body, name="inproj_bwd_x", grid=(T // tm,),
        in_specs=[tok3, tok, row, row, pl.BlockSpec((D_MODEL, IN_PAD), lambda i: (0, 0))] + sspecs,
        out_specs=(tok3, pl.BlockSpec((8, D_MODEL), lambda i: (0, 0))),
        out_shape=(jax.ShapeDtypeStruct((1, T, D_MODEL), _F32), jax.ShapeDtypeStruct((8, D_MODEL), _F32)),
        compiler_params=_cparams(("arbitrary",)),
    )(x, dx1, scale1p, norm_w, w_in_bf, *sargs)


def _inproj_bwd_w(h_t, sections, T):
    tm = 1024
    n_i = T // tm
    sspecs, sargs = _section_specs(sections, tm)
    ns = len(sargs)

    def body(*refs):
        h_ref = refs[0]
        srefs = refs[1:1 + ns]
        dw_ref, db_ref, acc, sem = refs[1 + ns:]
        i = pl.program_id(0)

        @pl.when(i == 0)
        def _():
            acc[...] = jnp.zeros_like(acc)
            db_ref[...] = jnp.zeros_like(db_ref)

        hv = h_ref[...]
        for (c0, width, _), v in zip(sections, _section_values(sections, srefs, _F32)):
            acc[:, c0:c0 + width] += _nn(hv, v.astype(_BF16))
            db_ref[0:1, c0:c0 + width] += jnp.sum(v, axis=0, keepdims=True)

        @pl.when(i == n_i - 1)
        def _():
            cp = pltpu.make_async_copy(acc, dw_ref, sem)
            cp.start()
            cp.wait()

    return pl.pallas_call(
        body, name="inproj_bwd_w", grid=(n_i,),
        in_specs=[pl.BlockSpec((D_MODEL, tm), lambda i: (0, i))] + sspecs,
        out_specs=(pl.BlockSpec(memory_space=pl.ANY), pl.BlockSpec((8, IN_PAD), lambda i: (0, 0))),
        out_shape=(jax.ShapeDtypeStruct((D_MODEL, IN_PAD), _F32), jax.ShapeDtypeStruct((8, IN_PAD), _F32)),
        scratch_shapes=[pltpu.VMEM((D_MODEL, IN_PAD), _F32), pltpu.SemaphoreType.DMA],
        compiler_params=_cparams(("arbitrary",)),
    )(h_t, *sargs)


def _adamw_math(w, g, m, v):
    m = ADAM_B1 * m + (1.0 - ADAM_B1) * g
    v = ADAM_B2 * v + (1.0 - ADAM_B2) * jnp.square(g)
    m_hat = m / (1.0 - ADAM_B1 ** ADAM_STEP)
    v_hat = v / (1.0 - ADAM_B2 ** ADAM_STEP)
    delta = -ADAM_LR * (m_hat / (jnp.sqrt(v_hat) + ADAM_EPS) + ADAM_WD * w)
    return delta, m, v


def _adamw_slots(w, m, v, slots, tr, name, own=None):
    R, C = w.shape
    extra = [] if own is None else [own]

    def body(w_ref, m_ref, v_ref, s_ref, *refs):
        g_ref, d_ref, nm_ref, nv_ref = refs[len(extra):]
        g = s_ref[0].astype(_F32)
        for k in range(1, N_DEV):
            g = g + s_ref[k].astype(_F32)
        if extra:
            g = g + refs[0][...].astype(_F32)
        g_ref[...] = g
        d_ref[...], nm_ref[...], nv_ref[...] = _adamw_math(w_ref[...], g, m_ref[...], v_ref[...])

    blk = pl.BlockSpec((tr, C), lambda i: (i, 0))
    return pl.pallas_call(
        body, name=name, grid=(R // tr,),
        in_specs=[blk, blk, blk, pl.BlockSpec((N_DEV, tr, C), lambda i: (0, i, 0))] + [blk] * len(extra),
        out_specs=(blk, blk, blk, blk),
        out_shape=tuple(jax.ShapeDtypeStruct((R, C), _F32) for _ in range(4)),
        compiler_params=_cparams(("arbitrary",)),
    )(w, m, v, slots, *extra)


def _w_ada_update(c_all, dmod_my, w, m, v):
    def body(c_ref, d_ref, w_ref, m_ref, v_ref, g_ref, dl_ref, nm_ref, nv_ref):
        g = lax.dot_general(_silu(c_ref[...]), d_ref[...], (((0,), (0,)), ((), ())),
                            precision=_HI, preferred_element_type=_F32)
        g_ref[...] = g
        dl_ref[...], nm_ref[...], nv_ref[...] = _adamw_math(w_ref[...], g, m_ref[...], v_ref[...])

    return pl.pallas_call(
        body, name="w_ada_update",
        out_shape=tuple(jax.ShapeDtypeStruct(w.shape, _F32) for _ in range(4)),
        compiler_params=_cparams(),
    )(c_all, dmod_my, w, m, v)


_PACK = (("b_ada", 3072, 3072), ("norm_w", 1024, 1024), ("b_in", IN_W, IN_PAD), ("conv_w", 5120, 5120),
         ("conv_b", 1024, 1024), ("rpb", 3720, 3840), ("ml_norm_w", 512, 512), ("final_norm_w", 1024, 1024),
         ("loss", 1, 128))
_PACK_OFF = {}
_off = 0
for _name, _len, _pad in _PACK:
    _PACK_OFF[_name] = (_off, _len)
    _off += _pad
_PACK_LEN = _off


def _pack(parts):
    cols = []
    for name, length, pad in _PACK:
        vec = parts[name].reshape(-1).astype(_F32)
        cols.append(jnp.pad(vec, (0, pad - length)))
    return jnp.concatenate(cols).reshape(1, _PACK_LEN)


def _unpack(vec, name, shape):
    off, length = _PACK_OFF[name]
    return vec.reshape(-1)[off:off + length].reshape(shape)


def kernel(x, c, w_ada, b_ada, norm_w, w_in, b_in, conv_w, conv_b, rpb, ml_norm_w, w_out, final_norm_w, loss_target, m_w_ada, m_b_ada, m_norm_w, m_w_in, m_b_in, m_conv_w, m_conv_b, m_rpb, m_ml_norm_w, m_w_out, m_final_norm_w, v_w_ada, v_b_ada, v_norm_w, v_w_in, v_b_in, v_conv_w, v_conv_b, v_rpb, v_ml_norm_w, v_w_out, v_final_norm_w):
    T = x.shape[1]
    rows = T // GRID_W
    me = 4 * lax.axis_index("x") + 2 * lax.axis_index("y") + lax.axis_index("c")
    n_in = w_in.shape[2]
    n_ada = w_ada.shape[2]
    n_cw = conv_w.shape[2]
    n_wo = w_out.shape[1]

    w_in_my, w_out_my = w_in[0].astype(_BF16), w_out[0].astype(_BF16)
    start_in = _scatter_start([w_in_my], "w_in_start", scatter=False)
    start_out = _scatter_start([w_out_my], "w_out_start", scatter=False)
    tokens = start_in[-1][0:1, 0:1] + start_out[-1][0:1, 0:1]
    g_conv_w, g_c = _exchange([conv_w[0], c + tokens], [False] * 2, "gather_small")
    b_in_pad = jnp.pad(b_in, ((0, 0), (0, IN_PAD - IN_W)))
    conv_w_full = jnp.pad(g_conv_w.transpose(1, 0, 2).reshape(CONV_W, N_DEV * n_cw), ((0, 3), (0, 0)))
    c_all = g_c.reshape(N_DEV, D_MODEL)

    b_ada_my = lax.dynamic_slice(b_ada, (0, me * n_ada), (1, n_ada))
    (mod_slots,) = _exchange([_mod_part(c_all, w_ada[0], b_ada_my)], [False], "gather_mod")
    mod = lax.dynamic_index_in_dim(mod_slots, me, axis=1, keepdims=False).reshape(1, 3 * D_MODEL)
    shift, scale, gate = mod[:, :D_MODEL], mod[:, D_MODEL:2 * D_MODEL], mod[:, 2 * D_MODEL:]
    scale1p = 1.0 + scale
    bias = _na_bias_tables(rpb[0], rows)

    def gathered(started, after, name):
        (own,), (land,) = _scatter_wait(started, after, name, scatter=False)
        return lax.dynamic_update_slice(land, own[None], (me,) + (0,) * own.ndim)

    g_w_in = gathered(start_in, bias[0, 0, :8, :128] + scale1p[:, :128], "w_in_wait")
    w_in_full = g_w_in.transpose(1, 0, 2).reshape(D_MODEL, N_DEV * n_in)
    w_in_bf = jnp.pad(w_in_full, ((0, 0), (0, IN_PAD - IN_W)))

    qkv, rest, h_bf = _inproj_fwd(x, scale1p, shift, norm_w, w_in_bf, b_in_pad)
    o_na, lse = _na_fwd(qkv, bias, T)
    qk_act = _conv_fwd(rest, conv_w_full, conv_b, T)
    h_f, cs_f, ns_f, ms_f = _mlstm_fwd(qk_act, rest, T, False)
    h_b, cs_b, ns_b, ms_b = _mlstm_fwd(qk_act, rest, T, True)

    w_out_bf = gathered(start_out, ms_b, "w_out_wait").reshape(N_DEV * n_wo, D_MODEL)
    dx1, d_o, dz_na, d_h, d_mo, d_mz, dwo, pvec = _post(
        x, loss_target, o_na, rest, h_f, h_b, gate, ml_norm_w, final_norm_w.reshape(1, D_MODEL), w_out_bf, T)

    dq_na, dk_na, dv_na, dbias = _na_bwd(qkv, bias, o_na, d_o, lse, T)
    d_rpb = _rpb_grad(dbias, rows)
    dqk_f, dv_f, dg_f = _mlstm_bwd(qk_act, rest, d_h, cs_f, ns_f, ms_f, T, False)
    dqk_b, dv_b, dg_b = _mlstm_bwd(qk_act, rest, d_h, cs_b, ns_b, ms_b, T, True)
    d_u, dconv = _conv_bwd(rest, conv_w_full, conv_b, dqk_f, dqk_b, T)

    sections = [(0, 512, [(dq_na, 0)]), (512, 512, [(dk_na, 0)]), (1024, 512, [(dv_na, 0)]),
                (1536, 512, [(dz_na, 0)]), (2048, 512, [(d_u, 0)]), (2560, 512, [(d_u, 1)]),
                (3072, 512, [(dv_f, 0), (dv_b, 0)]), (3584, 512, [(d_mo, 0)]), (4096, 512, [(d_mz, 0)]),
                (4608, 128, [(dg_f, 0), (dg_b, 0)])]
    dw_pad, db_pad = _inproj_bwd_w(h_bf, sections, T)
    db_in = db_pad[0, :IN_W]

    dw_blocks = dw_pad[:, :IN_W].astype(_BF16).reshape(D_MODEL, N_DEV, n_in).transpose(1, 0, 2)
    dwo_blocks = dwo.astype(_BF16).reshape(N_DEV, n_wo, D_MODEL)
    started = _scatter_start([dw_blocks, dwo_blocks], "grads_start")
    grad_x, xvec = _inproj_bwd_x(x, dx1, scale1p + started[-1][0:1, 0:1], norm_w, w_in_bf, sections, T)
    (dw_blocks, dwo_blocks), (s_w_in, s_w_out) = _scatter_wait(started, xvec, "grads_wait")

    small = _pack({
        "b_ada": jnp.concatenate([xvec[0], xvec[1], pvec[1]]),
        "norm_w": xvec[2], "b_in": db_in, "conv_w": dconv[:CONV_W], "conv_b": dconv[CONV_W],
        "rpb": d_rpb, "ml_norm_w": pvec[2, :ML_W], "final_norm_w": pvec[0], "loss": pvec[3, :1]})
    (s_small,) = _exchange([small], [False], "exchange_small")

    own = lambda blocks: lax.dynamic_index_in_dim(blocks, me, axis=0, keepdims=False)
    g_w_in_s, d_w_in, nm_w_in, nv_w_in = _adamw_slots(
        w_in[0], m_w_in[0], v_w_in[0], s_w_in, 128, "adamw_w_in", own=own(dw_blocks))
    g_w_out_s, d_w_out, nm_w_out, nv_w_out = _adamw_slots(
        w_out[0], m_w_out[0], v_w_out[0], s_w_out, n_wo, "adamw_w_out", own=own(dwo_blocks))
    dmod_all = s_small[:, 0, :3 * D_MODEL]
    dmod_my = lax.dynamic_slice(dmod_all, (0, me * n_ada), (N_DEV, n_ada))
    g_w_ada, d_w_ada, nm_w_ada, nv_w_ada = _w_ada_update(c_all, dmod_my, w_ada[0], m_w_ada[0], v_w_ada[0])

    def embed(shard):
        return lax.dynamic_update_slice(jnp.zeros((CONV_W, N_DEV * n_cw), _F32), shard[0], (0, me * n_cw))

    zero1 = jnp.zeros((1,), _F32)
    packed = lambda b_a, n_w, b_i, c_w, c_b, rp, mn, fn: _pack({
        "b_ada": b_a, "norm_w": n_w, "b_in": b_i, "conv_w": embed(c_w), "conv_b": c_b, "rpb": rp,
        "ml_norm_w": mn, "final_norm_w": fn, "loss": zero1})
    pw = packed(b_ada, norm_w, b_in, conv_w, conv_b, rpb, ml_norm_w, final_norm_w)
    pm = packed(m_b_ada, m_norm_w, m_b_in, m_conv_w, m_conv_b, m_rpb, m_ml_norm_w, m_final_norm_w)
    pv = packed(v_b_ada, v_norm_w, v_b_in, v_conv_w, v_conv_b, v_rpb, v_ml_norm_w, v_final_norm_w)
    sg, sd, sm, sv = _adamw_slots(pw, pm, pv, s_small, 1, "adamw_small")

    def small_outs(vec):
        cw = lax.dynamic_slice(_unpack(vec, "conv_w", (CONV_W, N_DEV * n_cw)), (0, me * n_cw), (CONV_W, n_cw))
        return dict(b_ada=_unpack(vec, "b_ada", b_ada.shape), norm_w=_unpack(vec, "norm_w", norm_w.shape),
                    b_in=_unpack(vec, "b_in", b_in.shape), conv_w=cw[None],
                    conv_b=_unpack(vec, "conv_b", conv_b.shape), rpb=_unpack(vec, "rpb", rpb.shape),
                    ml_norm_w=_unpack(vec, "ml_norm_w", ml_norm_w.shape),
                    final_norm_w=_unpack(vec, "final_norm_w", final_norm_w.shape))

    loss = _unpack(sg, "loss", ())
    order = ("w_ada", "b_ada", "norm_w", "w_in", "b_in", "conv_w", "conv_b", "rpb", "ml_norm_w", "w_out",
             "final_norm_w")
    outs = []
    for vec, big in ((sg, (g_w_ada, g_w_in_s, g_w_out_s)), (sd, (d_w_ada, d_w_in, d_w_out)),
                     (sm, (nm_w_ada, nm_w_in, nm_w_out)), (sv, (nv_w_ada, nv_w_in, nv_w_out))):
        group = small_outs(vec)
        group.update(w_ada=big[0][None], w_in=big[1][None], w_out=big[2][None])
        outs.extend(group[name] for name in order)
    return (loss, grad_x, *outs)
```

```python
import functools

import numpy as np
import jax
import jax.numpy as jnp
from jax import lax
from jax.experimental import pallas as pl
from jax.experimental.pallas import tpu as pltpu

N_DEV = 8
D_MODEL = 1024
GRID_W = 64
NA_HEADS = 8
NA_HEAD_DIM = 64
NA_KH = 8
NA_KW = 16
NA_W = 512
ML_HEADS = 4
ML_HEAD_DIM = 128
ML_W = 512
ML_CHUNK = 512
CONV_W = 5
EPS = 1e-6
IN_W = 4624
IN_PAD = 4736
REST_W = IN_PAD - 3 * NA_W
GATE_COL = 3072
NEG = -1e30
NA_RB = 4
NA_WIN = 12
NA_SUB = 2
ML_CB = 1
ADAM_LR = 0.001
ADAM_B1 = 0.9
ADAM_B2 = 0.999
ADAM_EPS = 1e-08
ADAM_WD = 0.01
ADAM_STEP = 10
VMEM_LIMIT = 56 * 1024 * 1024

_F32 = jnp.float32
_BF16 = jnp.bfloat16
_HI = lax.Precision.HIGHEST


def _cparams(sem=None):
    return pltpu.CompilerParams(dimension_semantics=sem, vmem_limit_bytes=VMEM_LIMIT)


def _nt(a, b):
    return lax.dot_general(a, b, (((1,), (1,)), ((), ())), preferred_element_type=_F32)


def _tn(a, b):
    return lax.dot_general(a, b, (((0,), (0,)), ((), ())), preferred_element_type=_F32)


def _nn(a, b):
    return jnp.dot(a, b, preferred_element_type=_F32)


def _sigmoid(x):
    return 1.0 / (1.0 + jnp.exp(-x))


def _silu(x):
    return x * _sigmoid(x)


def _dsilu(x):
    s = _sigmoid(x)
    return s * (1.0 + x * (1.0 - s))


def _exchange(arrs, scatter, name):
    n = len(arrs)
    out_shape = []
    for a, sc in zip(arrs, scatter):
        blk = a.shape[1:] if sc else a.shape
        out_shape.append(jax.ShapeDtypeStruct((N_DEV,) + tuple(blk), a.dtype))

    def body(*refs):
        ins = refs[:n]
        outs = refs[n:2 * n]
        send_sems, recv_sems, local_sems = refs[2 * n:]
        x, y, c = lax.axis_index("x"), lax.axis_index("y"), lax.axis_index("c")
        me = 4 * x + 2 * y + c
        local, sends, recvs = [], [], []
        for a in range(n):
            own = ins[a].at[me] if scatter[a] else ins[a]
            cp = pltpu.make_async_copy(own, outs[a].at[me], local_sems.at[a])
            cp.start()
            local.append(cp)
            for k in range(1, N_DEV):
                px = 1 - x if k & 4 else x
                py = 1 - y if k & 2 else y
                pc = 1 - c if k & 1 else c
                p = 4 * px + 2 * py + pc
                src = ins[a].at[p] if scatter[a] else ins[a]
                snd = pltpu.make_async_remote_copy(
                    src_ref=src, dst_ref=outs[a].at[me],
                    send_sem=send_sems.at[a, k - 1], recv_sem=recv_sems.at[a, k - 1],
                    device_id=(px, py, pc), device_id_type=pl.DeviceIdType.MESH)
                snd.start()
                sends.append(snd)
                rcv = pltpu.make_async_remote_copy(
                    src_ref=src, dst_ref=outs[a].at[p],
                    send_sem=send_sems.at[a, k - 1], recv_sem=recv_sems.at[a, k - 1],
                    device_id=(px, py, pc), device_id_type=pl.DeviceIdType.MESH)
                recvs.append(rcv)
        for rcv in recvs:
            rcv.wait_recv()
        for snd in sends:
            snd.wait_send()
        for cp in local:
            cp.wait()

    any_spec = pl.BlockSpec(memory_space=pl.ANY)
    res = pl.pallas_call(
        body, name=name, out_shape=tuple(out_shape),
        in_specs=[any_spec] * n, out_specs=tuple([any_spec] * n),
        scratch_shapes=[pltpu.SemaphoreType.DMA((n, N_DEV - 1)),
                        pltpu.SemaphoreType.DMA((n, N_DEV - 1)),
                        pltpu.SemaphoreType.DMA((n,))],
    )(*arrs)
    return list(res)


def _peer(k):
    x, y, c = lax.axis_index("x"), lax.axis_index("y"), lax.axis_index("c")
    px = 1 - x if k & 4 else x
    py = 1 - y if k & 2 else y
    pc = 1 - c if k & 1 else c
    return (px, py, pc), 4 * px + 2 * py + pc, 4 * x + 2 * y + c


def _scatter_copy(srcs, lands, send_sems, recv_sems, a, k, receive, scatter):
    dev, p, me = _peer(k)
    return pltpu.make_async_remote_copy(
        src_ref=srcs[a].at[p] if scatter else srcs[a], dst_ref=lands[a].at[p if receive else me],
        send_sem=send_sems[a * (N_DEV - 1) + k - 1], recv_sem=recv_sems[a * (N_DEV - 1) + k - 1],
        device_id=dev, device_id_type=pl.DeviceIdType.MESH)


def _scatter_start(arrs, name, scatter=True):
    n = len(arrs)
    ns = n * (N_DEV - 1)
    hbm = pl.BlockSpec(memory_space=pltpu.HBM)
    sem = pl.BlockSpec(memory_space=pltpu.SEMAPHORE)

    def body(*refs):
        srcs, lands = refs[:n], refs[n:2 * n]
        send_sems, recv_sems = refs[2 * n:2 * n + ns], refs[2 * n + ns:2 * n + 2 * ns]
        token = refs[-1]
        for a in range(n):
            for k in range(1, N_DEV):
                _scatter_copy(srcs, lands, send_sems, recv_sems, a, k, False, scatter).start()
        token[...] = jnp.zeros_like(token)

    land_shapes = [a.shape if scatter else (N_DEV,) + a.shape for a in arrs]
    buffers = [pltpu.HBM(a.shape, a.dtype) for a in arrs]
    land_buffers = [pltpu.HBM(s, a.dtype) for s, a in zip(land_shapes, arrs)]
    sems = [pltpu.SemaphoreType.DMA(()) for _ in range(2 * ns)]
    res = pl.pallas_call(
        body, name=name,
        out_shape=(*sems, *buffers, *land_buffers, jax.ShapeDtypeStruct((8, 128), _F32)),
        in_specs=[hbm] * (2 * n),
        out_specs=(*([sem] * (2 * ns)), *([hbm] * (2 * n)), pl.BlockSpec(memory_space=pltpu.VMEM)),
        input_output_aliases={i: 2 * ns + i for i in range(2 * n)},
        compiler_params=pltpu.CompilerParams(has_side_effects=pltpu.SideEffectType.DATAFLOW_SIDE_EFFECTING),
    )(*[pltpu.with_memory_space_constraint(a, pltpu.HBM) for a in arrs],
      *[pltpu.with_memory_space_constraint(jnp.zeros(s, a.dtype), pltpu.HBM) for s, a in zip(land_shapes, arrs)])
    res = list(res)
    return (res[:ns], res[ns:2 * ns], res[2 * ns:2 * ns + n], res[2 * ns + n:2 * ns + 2 * n], res[-1])


def _scatter_wait(started, after, name, scatter=True):
    send_sems, recv_sems, srcs, lands, _ = started
    n = len(srcs)
    ns = len(send_sems)
    hbm = pl.BlockSpec(memory_space=pltpu.HBM)
    sem = pl.BlockSpec(memory_space=pltpu.SEMAPHORE)

    def body(*refs):
        src_refs, land_refs = refs[:n], refs[n:2 * n]
        s_sems, r_sems = refs[2 * n:2 * n + ns], refs[2 * n + ns:2 * n + 2 * ns]
        for a in range(n):
            for k in range(1, N_DEV):
                _scatter_copy(src_refs, land_refs, s_sems, r_sems, a, k, False, scatter).wait_send()
                _scatter_copy(src_refs, land_refs, s_sems, r_sems, a, k, True, scatter).wait_recv()

    buffers = [pltpu.HBM(a.shape, a.dtype) for a in list(srcs) + list(lands)]
    res = pl.pallas_call(
        body, name=name, out_shape=tuple(buffers),
        in_specs=[hbm] * (2 * n) + [sem] * (2 * ns) + [pl.BlockSpec(memory_space=pl.ANY)],
        out_specs=tuple([hbm] * (2 * n)),
        input_output_aliases={i: i for i in range(2 * n)},
        compiler_params=pltpu.CompilerParams(has_side_effects=pltpu.SideEffectType.DATAFLOW_SIDE_EFFECTING),
    )(*srcs, *lands, *send_sems, *recv_sems, after)
    return list(res[:n]), list(res[n:])


def _mod_part(c_all, w_ada, b_my):
    def body(c_ref, w_ref, b_ref, o_ref):
        o_ref[...] = jnp.dot(_silu(c_ref[...]), w_ref[...], precision=_HI,
                             preferred_element_type=_F32) + b_ref[...]

    return pl.pallas_call(
        body, name="mod_part",
        out_shape=jax.ShapeDtypeStruct((N_DEV, w_ada.shape[1]), _F32),
        compiler_params=_cparams(),
    )(c_all, w_ada, b_my)


def _inproj_fwd(x, scale1p, shift, norm_w, w_in_bf, b_in_pad):
    T = x.shape[1]
    tm = 512
    n_q = 3 * NA_W

    def body(x_ref, sc_ref, sh_ref, nw_ref, w_ref, b_ref, qkv_ref, rest_ref, h_ref):
        xv = x_ref[...]
        r = lax.rsqrt(jnp.mean(xv * xv, axis=-1, keepdims=True) + EPS)
        h = xv * r * nw_ref[...] * sc_ref[...] + sh_ref[...]
        hb = h.astype(_BF16)
        h_ref[...] = h.T.astype(_BF16)
        for n0 in range(0, IN_PAD, 512):
            wd = min(512, IN_PAD - n0)
            acc = _nn(hb, w_ref[:, n0:n0 + wd]) + b_ref[:, n0:n0 + wd]
            if n0 == 0:
                acc = acc * (NA_HEAD_DIM ** -0.5)
            if n0 < n_q:
                qkv_ref[:, n0:n0 + wd] = acc.astype(_BF16)
            else:
                rest_ref[:, n0 - n_q:n0 - n_q + wd] = acc

    row = lambda w: pl.BlockSpec((1, w), lambda i: (0, 0))
    return pl.pallas_call(
        body, name="inproj_fwd", grid=(T // tm,),
        in_specs=[pl.BlockSpec((None, tm, D_MODEL), lambda i: (0, i, 0)), row(D_MODEL), row(D_MODEL), row(D_MODEL),
                  pl.BlockSpec((D_MODEL, IN_PAD), lambda i: (0, 0), pipeline_mode=pl.Buffered(1)), row(IN_PAD)],
        out_specs=(pl.BlockSpec((tm, n_q), lambda i: (i, 0)),
                   pl.BlockSpec((tm, REST_W), lambda i: (i, 0)),
                   pl.BlockSpec((D_MODEL, tm), lambda i: (0, i))),
        out_shape=(jax.ShapeDtypeStruct((T, n_q), _BF16),
                   jax.ShapeDtypeStruct((T, REST_W), _F32),
                   jax.ShapeDtypeStruct((D_MODEL, T), _BF16)),
        compiler_params=_cparams(("arbitrary",)),
    )(x, scale1p, shift, norm_w, w_in_bf, b_in_pad)


def _na_class_rows(rows):
    nb = rows // NA_RB
    out = []
    for rb in (0, min(1, nb - 1), nb - 1):
        ws = int(np.clip(NA_RB * rb - 4, 0, rows - NA_WIN))
        out.append((NA_RB * rb + np.arange(NA_RB), ws + np.arange(NA_WIN)))
    return out


def _na_pair_index(rows, qrows, krows):
    start = lambda r: np.clip(r - NA_KH // 2, 0, rows - NA_KH)
    col = np.arange(GRID_W)
    cstart = np.clip(col - NA_KW // 2, 0, GRID_W - NA_KW)
    dy = krows[None, :] - qrows[:, None] + NA_KH - 1
    vr = (krows[None, :] >= start(qrows)[:, None]) & (krows[None, :] < start(qrows)[:, None] + NA_KH)
    dx = np.clip(col[None, :] - col[:, None], -(NA_KW - 1), NA_KW - 1) + NA_KW - 1
    vc = (col[None, :] >= cstart[:, None]) & (col[None, :] < cstart[:, None] + NA_KW)
    nq, nk = len(qrows), len(krows)
    dy4 = np.broadcast_to(np.clip(dy, 0, 2 * NA_KH - 2)[:, None, :, None], (nq, GRID_W, nk, GRID_W))
    dx4 = np.broadcast_to(dx[None, :, None, :], (nq, GRID_W, nk, GRID_W))
    valid = vr[:, None, :, None] & vc[None, :, None, :]
    idx = (dy4 * (2 * NA_KW - 1) + dx4).reshape(nq * GRID_W, nk * GRID_W)
    return idx.astype(np.int32), valid.reshape(nq * GRID_W, nk * GRID_W), (dy, vr, dx, vc)


def _na_half_slabs(rpb):
    _, _, (_, _, dx, vc) = _na_pair_index(NA_WIN, np.arange(1), np.arange(1))
    qc, kc = np.meshgrid(np.arange(GRID_W), np.arange(GRID_W), indexing="ij")
    consts = []
    for right in (False, True):
        pos = (qc * 128 + (GRID_W if right else 0) + kc).reshape(-1)
        oh = np.zeros((32, GRID_W * 128), np.float32)
        oh[dx[qc, kc].reshape(-1), pos] = 1.0
        col_neg = np.zeros((1, GRID_W * 128), np.float32)
        col_neg[0, pos] = np.where(vc[qc, kc].reshape(-1), 0.0, NEG)
        half = np.zeros((1, GRID_W * 128), np.float32)
        half[0, pos] = 1.0
        consts += [jnp.asarray(oh), jnp.asarray(col_neg), jnp.asarray(half)]
    row_neg = np.where(np.arange(NA_HEADS * 16) % 16 == 15, NEG, 0.0).astype(np.float32).reshape(-1, 1)
    rp = jnp.pad(rpb, ((0, 0), (0, 1), (0, 1))).reshape(NA_HEADS * 16, 32)

    def body(*refs):
        r_ref, rn_ref = refs[0], refs[1]
        for t in range(2):
            oh_ref, cn_ref, half_ref = refs[2 + 3 * t:5 + 3 * t]
            refs[8 + t][...] = (jnp.dot(r_ref[...], oh_ref[...], precision=_HI, preferred_element_type=_F32)
                                + cn_ref[...] + rn_ref[...] * half_ref[...])

    outs = pl.pallas_call(
        body, name="na_half_slabs",
        out_shape=tuple(jax.ShapeDtypeStruct((NA_HEADS * 16, GRID_W * 128), _F32) for _ in range(2)),
        compiler_params=_cparams(),
    )(rp, jnp.asarray(row_neg), *consts)
    return [o.reshape(NA_HEADS, 16, GRID_W, 128) for o in outs]


def _na_bias_tables(rpb, rows):
    left, right = _na_half_slabs(rpb)
    didx = []
    for blk, win in _na_class_rows(rows):
        _, _, (dy, vr, _, _) = _na_pair_index(rows, blk, win)
        didx.append(np.where(vr, dy, 15))

    def body(l_ref, r_ref, b_ref):
        for ci, tab in enumerate(didx):
            for a in range(NA_RB):
                for j in range(NA_WIN // 2):
                    b_ref[ci, 0, a * GRID_W:(a + 1) * GRID_W, j * 128:(j + 1) * 128] = (
                        l_ref[0, int(tab[a, 2 * j])] + r_ref[0, int(tab[a, 2 * j + 1])])

    slab = pl.BlockSpec((1, 16, GRID_W, 128), lambda h: (h, 0, 0, 0))
    return pl.pallas_call(
        body, name="na_tables", grid=(NA_HEADS,),
        in_specs=[slab] * 2,
        out_specs=pl.BlockSpec((3, 1, NA_RB * GRID_W, NA_WIN * GRID_W), lambda h: (0, h, 0, 0)),
        out_shape=jax.ShapeDtypeStruct((3, NA_HEADS, NA_RB * GRID_W, NA_WIN * GRID_W), _F32),
        compiler_params=_cparams(("arbitrary",)),
    )(left, right)


def _stack_heads(x, first):
    zero = jnp.zeros_like(x)
    return jnp.concatenate([jnp.where(first, x, zero), jnp.where(first, zero, x)], axis=0)


def _na_sub(rb, u, rows):
    sb = NA_SUB * rb + u
    nb = rows // NA_RB
    cls = jnp.where(sb == 0, 0, jnp.where(sb == nb - 1, 2, 1))
    ws = pl.multiple_of(jnp.clip(NA_RB * sb - 4, 0, rows - NA_WIN) * GRID_W, 256)
    return cls, ws


def _na_fwd(qkv, bias, T):
    rows = T // GRID_W
    tq = NA_RB * GRID_W
    tw = NA_WIN * GRID_W
    ts = NA_SUB * tq

    def body(q_ref, k_ref, v_ref, b_ref, o_ref, l_ref):
        rb = pl.program_id(1)
        lane = lax.broadcasted_iota(jnp.int32, (1, 128), 1)
        first = lane < NA_HEAD_DIM
        for u in range(NA_SUB):
            cls, ws = _na_sub(rb, u, rows)
            kw = k_ref[pl.ds(ws, tw), :]
            vw = v_ref[pl.ds(ws, tw), :]
            q2 = _stack_heads(q_ref[u * tq:(u + 1) * tq, :], first)
            s = _nt(q2, kw) + b_ref[cls].reshape(2 * tq, tw)
            m = jnp.max(s, axis=1, keepdims=True)
            p = jnp.exp(s - m)
            l = jnp.sum(p, axis=1, keepdims=True)
            o2 = _nn(p.astype(_BF16), vw) / l
            lse2 = m + jnp.log(l)
            o_ref[u * tq:(u + 1) * tq, :] = jnp.where(first, o2[:tq], o2[tq:])
            l_ref[u * tq:(u + 1) * tq, :] = jnp.where(first, lse2[:tq], lse2[tq:])

    blk = lambda off: pl.BlockSpec((ts, 128), lambda hp, rb: (rb, off + hp))
    whole = lambda off: pl.BlockSpec((T, 128), lambda hp, rb: (0, off + hp))
    return pl.pallas_call(
        body, name="na_fwd", grid=(NA_HEADS // 2, T // ts),
        in_specs=[blk(0), whole(4), whole(8),
                  pl.BlockSpec((3, 2, tq, tw), lambda hp, rb: (0, hp, 0, 0))],
        out_specs=(blk(0), blk(0)),
        out_shape=(jax.ShapeDtypeStruct((T, NA_W), _F32), jax.ShapeDtypeStruct((T, NA_W), _F32)),
        compiler_params=_cparams(("arbitrary", "arbitrary")),
    )(qkv, qkv, qkv, bias)


def _na_bwd(qkv, bias, o, d_o, lse, T):
    rows = T // GRID_W
    tq = NA_RB * GRID_W
    tw = NA_WIN * GRID_W
    ts = NA_SUB * tq

    def body(q_ref, k_ref, v_ref, b_ref, o_ref, do_ref, l_ref, dq_ref, dk_ref, dv_ref, db_ref):
        rb = pl.program_id(1)
        lane = lax.broadcasted_iota(jnp.int32, (1, 128), 1)
        first = lane < NA_HEAD_DIM

        @pl.when(rb == 0)
        def _():
            db_ref[...] = jnp.zeros_like(db_ref)
            dk_ref[...] = jnp.zeros_like(dk_ref)
            dv_ref[...] = jnp.zeros_like(dv_ref)

        for u in range(NA_SUB):
            cls, ws = _na_sub(rb, u, rows)
            kw = k_ref[pl.ds(ws, tw), :]
            vw = v_ref[pl.ds(ws, tw), :]
            sl = slice(u * tq, (u + 1) * tq)
            q = q_ref[sl, :]
            d_ov = do_ref[sl, :]
            prod = d_ov.astype(_F32) * o_ref[sl, :]
            lse_v = l_ref[sl, :]
            dqs = []
            dk_win = jnp.zeros((tw, 128), _F32)
            dv_win = jnp.zeros((tw, 128), _F32)
            for hh in range(2):
                msk = first if hh == 0 else jnp.logical_not(first)
                c0 = hh * NA_HEAD_DIM
                qm = jnp.where(msk, q, jnp.zeros_like(q))
                dom = jnp.where(msk, d_ov, jnp.zeros_like(d_ov))
                s = _nt(qm, kw) + b_ref[cls, hh]
                p = jnp.exp(s - lse_v[:, c0:c0 + 1])
                dp = _nt(dom, vw)
                delta = jnp.sum(jnp.where(msk, prod, 0.0), axis=1, keepdims=True)
                ds = p * (dp - delta)
                db_ref[cls, hh] += ds
                dsb = ds.astype(_BF16)
                dqs.append(_nn(dsb, kw) * (NA_HEAD_DIM ** -0.5))
                dk_win = dk_win + _tn(dsb, qm)
                dv_win = dv_win + _tn(p.astype(_BF16), dom)
            dq_ref[sl, :] = jnp.where(first, dqs[0], dqs[1]).astype(_BF16)
            dk_ref[pl.ds(ws, tw), :] += dk_win
            dv_ref[pl.ds(ws, tw), :] += dv_win

    once = pl.Buffered(1)
    blk = lambda off: pl.BlockSpec((ts, 128), lambda hp, rb: (rb, off + hp))
    whole = lambda off: pl.BlockSpec((T, 128), lambda hp, rb: (0, off + hp), pipeline_mode=once)
    tab = pl.BlockSpec((3, 2, tq, tw), lambda hp, rb: (0, hp, 0, 0), pipeline_mode=once)
    return pl.pallas_call(
        body, name="na_bwd", grid=(NA_HEADS // 2, T // ts),
        in_specs=[blk(0), whole(4), whole(8), tab, blk(0), blk(0), blk(0)],
        out_specs=(blk(0), whole(0), whole(0), tab),
        out_shape=(jax.ShapeDtypeStruct((T, NA_W), _BF16), jax.ShapeDtypeStruct((T, NA_W), _F32),
                   jax.ShapeDtypeStruct((T, NA_W), _F32), jax.ShapeDtypeStruct(bias.shape, _F32)),
        compiler_params=_cparams(("arbitrary", "arbitrary")),
    )(qkv, qkv, qkv, bias, o, d_o, lse)


def _rpb_grad(dbias, rows):
    tw = NA_WIN * GRID_W
    lanes = 16 * GRID_W
    offs = [int(win[0] - blk[0] + NA_KH - 1) for blk, win in _na_class_rows(rows)]

    def body(x_ref, g_ref):
        sub = lax.broadcasted_iota(jnp.int32, (NA_RB, 1), 0)
        qc = lax.broadcasted_iota(jnp.int32, (NA_RB * GRID_W, 1), 0) % GRID_W
        tot = jnp.zeros((NA_RB, lanes), _F32)
        for ci in range(3):
            xv = x_ref[ci, 0]
            for bit in range(6):
                xv = jnp.where(((qc >> bit) & 1) == 1, pltpu.roll(xv, tw - (1 << bit), 1), xv)
            acc = pltpu.roll(jnp.sum(xv.reshape(NA_RB, GRID_W, tw), axis=1), NA_KW, 1)
            acc = jnp.concatenate([acc, jnp.zeros((NA_RB, lanes - tw), _F32)], axis=1)
            for a in range(NA_RB):
                tot = tot + jnp.where(sub == a, pltpu.roll(acc, (GRID_W * (offs[ci] - a)) % lanes, 1), 0.0)
        g_ref[0] = jnp.broadcast_to(jnp.sum(tot, axis=0, keepdims=True), (8, lanes))

    g = pl.pallas_call(
        body, name="rpb_grad", grid=(NA_HEADS,),
        in_specs=[pl.BlockSpec((3, 1) + dbias.shape[2:], lambda h: (0, h, 0, 0))],
        out_specs=pl.BlockSpec((1, 8, lanes), lambda h: (h, 0, 0)),
        out_shape=jax.ShapeDtypeStruct((NA_HEADS, 8, lanes), _F32),
        compiler_params=_cparams(("arbitrary",)),
    )(dbias)
    return g[:, 0].reshape(NA_HEADS, 16, GRID_W)[:, :2 * NA_KH - 1, 1:2 * NA_KW]


def _halo_specs(tm, width, col_of, T, order):
    hb = tm // 8
    last = T // 8 - 1
    if order == "ij":
        cur = pl.BlockSpec((tm, width), lambda i, j: (i, col_of(j)))
        prev = pl.BlockSpec((8, width), lambda i, j: (jnp.maximum(i * hb - 1, 0), col_of(j)))
        nxt = pl.BlockSpec((8, width), lambda i, j: (jnp.minimum((i + 1) * hb, last), col_of(j)))
    else:
        cur = pl.BlockSpec((tm, width), lambda j, i: (i, col_of(j)))
        prev = pl.BlockSpec((8, width), lambda j, i: (jnp.maximum(i * hb - 1, 0), col_of(j)))
        nxt = pl.BlockSpec((8, width), lambda j, i: (jnp.minimum((i + 1) * hb, last), col_of(j)))
    return [prev, cur, nxt]


def _extend(prev_ref, cur_ref, next_ref, i, n_i):
    prev = jnp.where(i > 0, prev_ref[...], 0.0)
    nxt = jnp.where(i < n_i - 1, next_ref[...], 0.0)
    return jnp.concatenate([prev, cur_ref[...], nxt], axis=0)


def _conv_fwd(rest, conv_w, conv_b, T):
    tm = 512
    n_i = T // tm
    n = tm + 16

    def body(p_ref, c_ref, n_ref, w_ref, b_ref, o_ref):
        i = pl.program_id(0)
        ext = _extend(p_ref, c_ref, n_ref, i, n_i)
        acc = jnp.zeros((tm, 512), _F32) + b_ref[...]
        for j in range(CONV_W):
            acc = acc + w_ref[j:j + 1, :] * pltpu.roll(ext, (2 - j) % n, 0)[8:8 + tm]
        o_ref[...] = _silu(acc)

    return pl.pallas_call(
        body, name="conv_fwd", grid=(n_i, 2),
        in_specs=_halo_specs(tm, 512, lambda j: 1 + j, T, "ij")
        + [pl.BlockSpec((8, 512), lambda i, j: (0, j)), pl.BlockSpec((1, 512), lambda i, j: (0, j))],
        out_specs=pl.BlockSpec((tm, 512), lambda i, j: (i, j)),
        out_shape=jax.ShapeDtypeStruct((T, 2 * ML_W), _F32),
        compiler_params=_cparams(("arbitrary", "arbitrary")),
    )(rest, rest, rest, conv_w, conv_b)


def _conv_bwd(rest, conv_w, conv_b, da_f, da_b, T):
    tm = 512
    n_i = T // tm
    n = tm + 16

    def body(up, uc, un, fp, fc, fn, bp, bc, bn, w_ref, b_ref, du_ref, dw_ref):
        i = pl.program_id(1)
        ext_u = _extend(up, uc, un, i, n_i)
        ext_da = _extend(fp, fc, fn, i, n_i) + _extend(bp, bc, bn, i, n_i)
        shifted = [pltpu.roll(ext_u, (2 - j) % n, 0) for j in range(CONV_W)]
        pre = jnp.zeros((n, 512), _F32) + b_ref[...]
        for j in range(CONV_W):
            pre = pre + w_ref[j:j + 1, :] * shifted[j]
        gidx = i * tm - 8 + lax.broadcasted_iota(jnp.int32, (n, 1), 0)
        dpre = jnp.where((gidx >= 0) & (gidx < T), ext_da * _dsilu(pre), 0.0)
        du = jnp.zeros((tm, 512), _F32)
        for j in range(CONV_W):
            du = du + w_ref[j:j + 1, :] * pltpu.roll(dpre, (j - 2) % n, 0)[8:8 + tm]
        du_ref[...] = du.astype(_BF16)
        dpc = dpre[8:8 + tm]
        parts = [jnp.sum(dpc * shifted[j][8:8 + tm], axis=0, keepdims=True) for j in range(CONV_W)]
        parts.append(jnp.sum(dpc, axis=0, keepdims=True))
        parts.append(jnp.zeros((2, 512), _F32))
        upd = jnp.concatenate(parts, axis=0)

        @pl.when(i == 0)
        def _():
            dw_ref[...] = upd

        @pl.when(i > 0)
        def _():
            dw_ref[...] += upd

    return pl.pallas_call(
        body, name="conv_bwd", grid=(2, n_i),
        in_specs=_halo_specs(tm, 512, lambda j: 1 + j, T, "ji")
        + _halo_specs(tm, 512, lambda j: j, T, "ji") + _halo_specs(tm, 512, lambda j: j, T, "ji")
        + [pl.BlockSpec((8, 512), lambda j, i: (0, j)), pl.BlockSpec((1, 512), lambda j, i: (0, j))],
        out_specs=(pl.BlockSpec((tm, 512), lambda j, i: (i, j)), pl.BlockSpec((8, 512), lambda j, i: (0, j))),
        out_shape=(jax.ShapeDtypeStruct((T, 2 * ML_W), _BF16), jax.ShapeDtypeStruct((8, 2 * ML_W), _F32)),
        compiler_params=_cparams(("arbitrary", "arbitrary")),
    )(rest, rest, rest, da_f, da_f, da_f, da_b, da_b, da_b, conv_w, conv_b)


def _scan_rows(x, suffix):
    L = x.shape[0]
    row = lax.broadcasted_iota(jnp.int32, (L, 1), 0)
    step = 1
    while step < L:
        if suffix:
            x = x + jnp.where(row < L - step, pltpu.roll(x, L - step, 0), 0.0)
        else:
            x = x + jnp.where(row >= step, pltpu.roll(x, step, 0), 0.0)
        step *= 2
    return x


def _ml_gates(gt, rev):
    L = gt.shape[0]
    ri = lax.broadcasted_iota(jnp.int32, (L, L), 0)
    ci = lax.broadcasted_iota(jnp.int32, (L, L), 1)
    mask = (ci >= ri) if rev else (ci <= ri)
    lf = jnp.minimum(gt, 0.0) - jnp.log(1.0 + jnp.exp(-jnp.abs(gt)))
    b = _scan_rows(lf, suffix=rev)
    return mask, b, b.T, gt.T


def _ml_head_gates(gt, gates, head, rev):
    _, b, b_t, gt_t = gates
    ci = (8 if rev else 0) + head
    cf = ci + ML_HEADS
    last = 0 if rev else gt.shape[0] - 1
    return dict(icol=gt[:, ci:ci + 1], b_col=b[:, cf:cf + 1], b_row=b_t[cf:cf + 1, :],
                i_row=gt_t[ci:ci + 1, :], bl=b[last:last + 1, cf:cf + 1])


def _ml_chunk(q, k, v, hg, mask, C, n, m):
    icol, b_col, b_row, bl = hg["icol"], hg["b_col"], hg["b_row"], hg["bl"]
    dlog = jnp.where(mask, b_col - b_row + hg["i_row"], NEG)
    m_t = jnp.maximum(b_col + m, jnp.max(dlog, axis=1, keepdims=True))
    dm = jnp.exp(dlog - m_t)
    ks = k * (ML_HEAD_DIM ** -0.5)
    qb, kb, vb = q.astype(_BF16), ks.astype(_BF16), v.astype(_BF16)
    s = _nt(qb, kb) * dm
    g = jnp.exp(b_col + m - m_t)
    qc = _nt(qb, C.astype(_BF16))
    num = _nn(s.astype(_BF16), vb) + g * qc
    qn = jnp.sum(q * n, axis=1, keepdims=True)
    den = jnp.sum(s, axis=1, keepdims=True) + g * qn
    e_m = jnp.exp(-m_t)
    nrm = jnp.maximum(jnp.abs(den), e_m)
    h = num / nrm
    a_col = bl - b_col + icol
    m_new = jnp.maximum(bl + m, jnp.max(a_col, axis=0, keepdims=True))
    decay = jnp.exp(bl + m - m_new)
    w = jnp.exp(a_col - m_new)
    c_new = decay * C + _tn((w * v).astype(_BF16), kb)
    n_new = decay * n + jnp.sum(w * ks, axis=0, keepdims=True)
    aux = dict(dm=dm, ks=ks, qb=qb, kb=kb, vb=vb, s=s, g=g, qc=qc, qn=qn,
               den=den, e_m=e_m, nrm=nrm, decay=decay, w=w)
    return h, c_new, n_new, m_new, aux


def _mlstm_fwd(qk_act, rest, T, rev):
    tb = ML_CB * ML_CHUNK
    nblk = T // tb
    nc = T // ML_CHUNK
    bi = (lambda i: nblk - 1 - i) if rev else (lambda i: i)

    def body(q_ref, k_ref, v_ref, g_ref, h_ref, cs_ref, ns_ref, ms_ref, c_scr, n_scr, m_scr):
        @pl.when(pl.program_id(0) == 0)
        def _():
            c_scr[...] = jnp.zeros_like(c_scr)
            n_scr[...] = jnp.zeros_like(n_scr)
            m_scr[...] = jnp.zeros_like(m_scr)

        def step(j, carry):
            c = (ML_CB - 1 - j) if rev else j
            r0 = pl.multiple_of(c * ML_CHUNK, ML_CHUNK)
            gt = g_ref[pl.ds(r0, ML_CHUNK), :]
            gates = _ml_gates(gt, rev)
            for hd in range(ML_HEADS):
                cols = slice(hd * ML_HEAD_DIM, (hd + 1) * ML_HEAD_DIM)
                C = c_scr[hd]
                n = n_scr[hd:hd + 1, :]
                mrow = m_scr[hd:hd + 1, :]
                cs_ref[c, hd] = C
                ns_ref[c, hd:hd + 1, :] = n
                ms_ref[c, hd:hd + 1, :] = mrow
                h, c_new, n_new, m_new, _ = _ml_chunk(
                    q_ref[pl.ds(r0, ML_CHUNK), cols], k_ref[pl.ds(r0, ML_CHUNK), cols],
                    v_ref[pl.ds(r0, ML_CHUNK), cols], _ml_head_gates(gt, gates, hd, rev), gates[0],
                    C, n, mrow[:, 0:1])
                h_ref[pl.ds(r0, ML_CHUNK), cols] = h
                c_scr[hd] = c_new
                n_scr[hd:hd + 1, :] = n_new
                m_scr[hd:hd + 1, :] = jnp.broadcast_to(m_new, (1, 128))
            return carry

        lax.fori_loop(0, ML_CB, step, 0)

    return pl.pallas_call(
        body, name="mlstm_fwd_rev" if rev else "mlstm_fwd", grid=(nblk,),
        in_specs=[pl.BlockSpec((tb, ML_W), lambda i: (bi(i), 0)),
                  pl.BlockSpec((tb, ML_W), lambda i: (bi(i), 1)),
                  pl.BlockSpec((tb, ML_W), lambda i: (bi(i), 3)),
                  pl.BlockSpec((tb, 128), lambda i: (bi(i), GATE_COL // 128))],
        out_specs=(pl.BlockSpec((tb, ML_W), lambda i: (bi(i), 0)),
                   pl.BlockSpec((ML_CB, ML_HEADS, 128, 128), lambda i: (bi(i), 0, 0, 0)),
                   pl.BlockSpec((ML_CB, ML_HEADS, 128), lambda i: (bi(i), 0, 0)),
                   pl.BlockSpec((ML_CB, ML_HEADS, 128), lambda i: (bi(i), 0, 0))),
        out_shape=(jax.ShapeDtypeStruct((T, ML_W), _F32),
                   jax.ShapeDtypeStruct((nc, ML_HEADS, 128, 128), _F32),
                   jax.ShapeDtypeStruct((nc, ML_HEADS, 128), _F32),
                   jax.ShapeDtypeStruct((nc, ML_HEADS, 128), _F32)),
        scratch_shapes=[pltpu.VMEM((ML_HEADS, 128, 128), _F32), pltpu.VMEM((8, 128), _F32),
                        pltpu.VMEM((8, 128), _F32)],
        compiler_params=_cparams(("arbitrary",)),
    )(qk_act, qk_act, rest, rest)


def _mlstm_bwd(qk_act, rest, d_h, cs, ns, ms, T, rev):
    tb = ML_CB * ML_CHUNK
    nblk = T // tb
    bi = (lambda i: i) if rev else (lambda i: nblk - 1 - i)

    def body(q_ref, k_ref, v_ref, g_ref, dh_ref, cs_ref, ns_ref, ms_ref,
             dqk_ref, dv_ref, dg_ref, dc_scr, dn_scr):
        @pl.when(pl.program_id(0) == 0)
        def _():
            dc_scr[...] = jnp.zeros_like(dc_scr)
            dn_scr[...] = jnp.zeros_like(dn_scr)

        def step(j, carry):
            c = j if rev else (ML_CB - 1 - j)
            r0 = pl.multiple_of(c * ML_CHUNK, ML_CHUNK)
            gt = g_ref[pl.ds(r0, ML_CHUNK), :]
            gates = _ml_gates(gt, rev)
            mask = gates[0]
            lane = lax.broadcasted_iota(jnp.int32, (1, 128), 1)
            sub = lax.broadcasted_iota(jnp.int32, (128, 1), 0)
            db_t = jnp.zeros((ML_CHUNK, 128), _F32)
            da_t = jnp.zeros((ML_CHUNK, 128), _F32)
            cs_rows = jnp.zeros((128, ML_CHUNK), _F32)
            dbl_t = jnp.zeros((1, 128), _F32)
            for hd in range(ML_HEADS):
                cols = slice(hd * ML_HEAD_DIM, (hd + 1) * ML_HEAD_DIM)
                ci = (8 if rev else 0) + hd
                cf = ci + ML_HEADS
                q = q_ref[pl.ds(r0, ML_CHUNK), cols]
                k = k_ref[pl.ds(r0, ML_CHUNK), cols]
                v = v_ref[pl.ds(r0, ML_CHUNK), cols]
                C = cs_ref[c, hd]
                n = ns_ref[c, hd:hd + 1, :]
                m = ms_ref[c, hd:hd + 1, :][:, 0:1]
                dcn = dc_scr[hd]
                dnn = dn_scr[hd:hd + 1, :]
                h, _, _, _, a = _ml_chunk(q, k, v, _ml_head_gates(gt, gates, hd, rev), mask, C, n, m)
                d_hv = dh_ref[pl.ds(r0, ML_CHUNK), cols]
                g, s, w, ks = a["g"], a["s"], a["w"], a["ks"]
                qb, kb, vb = a["qb"], a["kb"], a["vb"]
                dnum = d_hv / a["nrm"]
                hdot = jnp.sum(d_hv * h, axis=1, keepdims=True)
                dden = jnp.where(jnp.abs(a["den"]) >= a["e_m"], -hdot / a["nrm"] * jnp.sign(a["den"]), 0.0)
                dnb = dnum.astype(_BF16)
                d_s = _nt(dnb, vb) + dden
                r = d_s * s
                dsqk = (d_s * a["dm"]).astype(_BF16)
                cb = C.astype(_BF16)
                dq = _nn(dsqk, kb) + g * _nn(dnb, cb) + (dden * g) * n
                dk = _tn(dsqk, qb)
                dv = _tn(s.astype(_BF16), dnb)
                dg = jnp.sum(dnum * a["qc"], axis=1, keepdims=True) + dden * a["qn"]
                db_col = jnp.sum(r, axis=1, keepdims=True) + dg * g
                cs_rows = cs_rows + jnp.where((sub == ci) | (sub == cf), jnp.sum(r, axis=0, keepdims=True), 0.0)
                dc_chunk = _tn((g * dnum).astype(_BF16), qb)
                dn_chunk = jnp.sum((dden * g) * q, axis=0, keepdims=True)
                dcb = dcn.astype(_BF16)
                vdc = _nn(vb, dcb)
                kdc = _nt(kb, dcb)
                dw = jnp.sum(vdc * ks, axis=1, keepdims=True) + jnp.sum(ks * dnn, axis=1, keepdims=True)
                dv = dv + w * kdc
                dk = dk + w * vdc + w * dnn
                da = dw * w
                ddecay = (jnp.sum(jnp.sum(dcn * C, axis=1, keepdims=True), axis=0, keepdims=True)
                          + jnp.sum(dnn * n, axis=1, keepdims=True))
                dbl = ddecay * a["decay"] + jnp.sum(da, axis=0, keepdims=True)
                db_t = db_t + jnp.where(lane == cf, db_col - da, 0.0)
                da_t = da_t + jnp.where(lane == ci, da, 0.0)
                dbl_t = dbl_t + jnp.where(lane == cf, dbl, 0.0)
                dc_scr[hd] = dc_chunk + a["decay"] * dcn
                dn_scr[hd:hd + 1, :] = dn_chunk + a["decay"] * dnn
                dqk_ref[pl.ds(r0, ML_CHUNK), cols] = dq
                dqk_ref[pl.ds(r0, ML_CHUNK), slice(ML_W + hd * 128, ML_W + (hd + 1) * 128)] = dk * (ML_HEAD_DIM ** -0.5)
                dv_ref[pl.ds(r0, ML_CHUNK), cols] = dv.astype(_BF16)
            lo = 8 if rev else 0
            is_i = (lane >= lo) & (lane < lo + ML_HEADS)
            is_f = (lane >= lo + ML_HEADS) & (lane < lo + 2 * ML_HEADS)
            cs_t = cs_rows.T
            db_all = db_t - jnp.where(is_f, cs_t, 0.0)
            dlf = _scan_rows(db_all, suffix=not rev) + dbl_t
            dg_ref[pl.ds(r0, ML_CHUNK), :] = (da_t + jnp.where(is_i, cs_t, 0.0)
                                               + jnp.where(is_f, dlf * _sigmoid(-gt), 0.0))
            return carry

        lax.fori_loop(0, ML_CB, step, 0)

    return pl.pallas_call(
        body, name="mlstm_bwd_rev" if rev else "mlstm_bwd", grid=(nblk,),
        in_specs=[pl.BlockSpec((tb, ML_W), lambda i: (bi(i), 0)),
                  pl.BlockSpec((tb, ML_W), lambda i: (bi(i), 1)),
                  pl.BlockSpec((tb, ML_W), lambda i: (bi(i), 3)),
                  pl.BlockSpec((tb, 128), lambda i: (bi(i), GATE_COL // 128)),
                  pl.BlockSpec((tb, ML_W), lambda i: (bi(i), 0)),
                  pl.BlockSpec((ML_CB, ML_HEADS, 128, 128), lambda i: (bi(i), 0, 0, 0)),
                  pl.BlockSpec((ML_CB, ML_HEADS, 128), lambda i: (bi(i), 0, 0)),
                  pl.BlockSpec((ML_CB, ML_HEADS, 128), lambda i: (bi(i), 0, 0))],
        out_specs=(pl.BlockSpec((tb, 2 * ML_W), lambda i: (bi(i), 0)),
                   pl.BlockSpec((tb, ML_W), lambda i: (bi(i), 0)),
                   pl.BlockSpec((tb, 128), lambda i: (bi(i), 0))),
        out_shape=(jax.ShapeDtypeStruct((T, 2 * ML_W), _F32), jax.ShapeDtypeStruct((T, ML_W), _BF16),
                   jax.ShapeDtypeStruct((T, 128), _F32)),
        scratch_shapes=[pltpu.VMEM((ML_HEADS, 128, 128), _F32), pltpu.VMEM((8, 128), _F32)],
        compiler_params=_cparams(("arbitrary",)),
    )(qk_act, qk_act, rest, rest, d_h, cs, ns, ms)


def _post(x, target, o_na, rest, h_f, h_b, gate, ml_norm_w, final_w, w_out_bf, T):
    tm = 256
    n_i = T // tm

    def body(x_ref, t_ref, o_ref, zna_ref, hf_ref, hb_ref, mo_ref, mz_ref, gate_ref, mw_ref, fw_ref, w_ref,
             dx1_ref, do_ref, dzna_ref, dh_ref, dmo_ref, dmz_ref, dwo_ref, vec_ref):
        i = pl.program_id(0)
        gate_v = gate_ref[...]
        fw = fw_ref[...]
        zna = zna_ref[...]
        o = o_ref[...]
        sig_zna = _sigmoid(zna)
        silu_zna = zna * sig_zna
        na_out = o * silu_zna
        hsum = hf_ref[...] + hb_ref[...]
        sg = _sigmoid(mo_ref[...])
        hm = hsum * sg
        mz = mz_ref[...]
        sig_mz = _sigmoid(mz)
        smz = mz * sig_mz
        dsilu_mz = sig_mz * (1.0 + mz * (1.0 - sig_mz))
        hn_l, rstd_l, ml_l = [], [], []
        for hd in range(ML_HEADS):
            cols = slice(hd * 128, (hd + 1) * 128)
            hh = hm[:, cols]
            mu = jnp.mean(hh, axis=-1, keepdims=True)
            var = jnp.mean(jnp.square(hh - mu), axis=-1, keepdims=True)
            rstd = lax.rsqrt(var + EPS)
            hn = (hh - mu) * rstd
            hn_l.append(hn)
            rstd_l.append(rstd)
            ml_l.append(hn * mw_ref[:, cols] * smz[:, cols])
        mix = jnp.concatenate([na_out] + ml_l, axis=1).astype(_BF16)
        y = _nn(mix, w_ref[...])
        x1 = x_ref[...] + gate_v * y
        r = lax.rsqrt(jnp.mean(x1 * x1, axis=-1, keepdims=True) + EPS)
        xhat = x1 * r
        out = xhat * fw
        err = out - t_ref[...]
        loss = 0.5 * jnp.sum(jnp.sum(err * err, axis=1, keepdims=True), axis=0, keepdims=True) / D_MODEL
        dout = err * (1.0 / D_MODEL)
        dfw = jnp.sum(dout * xhat, axis=0, keepdims=True)
        dxhat = dout * fw
        dx1 = r * (dxhat - xhat * jnp.mean(dxhat * xhat, axis=-1, keepdims=True))
        dx1_ref[...] = dx1
        dgate = jnp.sum(dx1 * y, axis=0, keepdims=True)
        dy = (dx1 * gate_v).astype(_BF16)
        dmix = _nt(dy, w_ref[...])
        dwo = _tn(mix, dy)
        dna = dmix[:, :NA_W]
        do_ref[...] = (dna * silu_zna).astype(_BF16)
        dzna_ref[...] = (dna * o * (sig_zna * (1.0 + zna * (1.0 - sig_zna)))).astype(_BF16)
        dmw_l = []
        for hd in range(ML_HEADS):
            cols = slice(hd * 128, (hd + 1) * 128)
            dml = dmix[:, NA_W + hd * 128:NA_W + (hd + 1) * 128]
            hn = hn_l[hd]
            mwv = mw_ref[:, cols]
            dmz_ref[:, cols] = (dml * hn * mwv * dsilu_mz[:, cols]).astype(_BF16)
            dhn = dml * mwv * smz[:, cols]
            dmw_l.append(jnp.sum(dml * hn * smz[:, cols], axis=0, keepdims=True))
            dhm = rstd_l[hd] * (dhn - jnp.mean(dhn, axis=-1, keepdims=True)
                                - hn * jnp.mean(dhn * hn, axis=-1, keepdims=True))
            sgc = sg[:, cols]
            dh_ref[:, cols] = dhm * sgc
            dmo_ref[:, cols] = (dhm * hsum[:, cols] * sgc * (1.0 - sgc)).astype(_BF16)
        dmw = jnp.concatenate(dmw_l + [jnp.zeros((1, D_MODEL - ML_W), _F32)], axis=1)
        lane = lax.broadcasted_iota(jnp.int32, (1, D_MODEL), 1)
        vec = jnp.concatenate([dfw, dgate, dmw, jnp.where(lane == 0, loss, 0.0),
                               jnp.zeros((4, D_MODEL), _F32)], axis=0)

        @pl.when(i == 0)
        def _():
            dwo_ref[...] = dwo
            vec_ref[...] = vec

        @pl.when(i > 0)
        def _():
            dwo_ref[...] += dwo
            vec_ref[...] += vec

    tok = lambda w, j: pl.BlockSpec((tm, w), lambda i: (i, j))
    tok3 = pl.BlockSpec((None, tm, D_MODEL), lambda i: (0, i, 0))
    row = lambda w: pl.BlockSpec((1, w), lambda i: (0, 0))
    f32 = lambda w: jax.ShapeDtypeStruct((T, w), _F32)
    bf16 = lambda w: jax.ShapeDtypeStruct((T, w), _BF16)
    return pl.pallas_call(
        body, name="post", grid=(n_i,),
        in_specs=[tok3, tok3, tok(NA_W, 0), tok(NA_W, 0), tok(ML_W, 0), tok(ML_W, 0),
                  tok(ML_W, 4), tok(ML_W, 5), row(D_MODEL), row(ML_W), row(D_MODEL),
                  pl.BlockSpec((D_MODEL, D_MODEL), lambda i: (0, 0))],
        out_specs=(tok(D_MODEL, 0), tok(NA_W, 0), tok(NA_W, 0), tok(ML_W, 0), tok(ML_W, 0), tok(ML_W, 0),
                   pl.BlockSpec((D_MODEL, D_MODEL), lambda i: (0, 0)),
                   pl.BlockSpec((8, D_MODEL), lambda i: (0, 0))),
        out_shape=(f32(D_MODEL), bf16(NA_W), bf16(NA_W), f32(ML_W), bf16(ML_W),
                   bf16(ML_W), jax.ShapeDtypeStruct((D_MODEL, D_MODEL), _F32),
                   jax.ShapeDtypeStruct((8, D_MODEL), _F32)),
        compiler_params=_cparams(("arbitrary",)),
    )(x, target, o_na, rest, h_f, h_b, rest, rest, gate, ml_norm_w, final_w, w_out_bf)


def _section_specs(sections, tm):
    specs, args = [], []
    for _, width, parts in sections:
        for arr, cb in parts:
            specs.append(pl.BlockSpec((tm, width), functools.partial(lambda i, cb: (i, cb), cb=cb)))
            args.append(arr)
    return specs, args


def _section_values(sections, refs, dtype):
    vals, at = [], 0
    for _, _, parts in sections:
        v = refs[at][...]
        for r in refs[at + 1:at + len(parts)]:
            v = v.astype(_F32) + r[...].astype(_F32)
        at += len(parts)
        vals.append(v.astype(dtype))
    return vals


def _inproj_bwd_x(x, dx1, scale1p, norm_w, w_in_bf, sections, T):
    tm = 512
    sspecs, sargs = _section_specs(sections, tm)
    ns = len(sargs)

    def body(*refs):
        x_ref, dx1_ref, sc_ref, nw_ref, w_ref = refs[:5]
        srefs = refs[5:5 + ns]
        gx_ref, vec_ref = refs[5 + ns:]
        i = pl.program_id(0)
        vals = _section_values(sections, srefs, _BF16)
        dh = jnp.zeros((tm, D_MODEL), _F32)
        for (c0, width, _), val in zip(sections, vals):
            dh = dh + _nt(val, w_ref[:, c0:c0 + width])
        xv = x_ref[...]
        r = lax.rsqrt(jnp.mean(xv * xv, axis=-1, keepdims=True) + EPS)
        xhat = xv * r
        nw = nw_ref[...]
        dshift = jnp.sum(dh, axis=0, keepdims=True)
        dscale = jnp.sum(dh * xhat * nw, axis=0, keepdims=True)
        dhpre = dh * sc_ref[...]
        dnw = jnp.sum(dhpre * xhat, axis=0, keepdims=True)
        dxhat = dhpre * nw
        gx_ref[...] = dx1_ref[...] + r * (dxhat - xhat * jnp.mean(dxhat * xhat, axis=-1, keepdims=True))
        vec = jnp.concatenate([dshift, dscale, dnw, jnp.zeros((5, D_MODEL), _F32)], axis=0)

        @pl.when(i == 0)
        def _():
            vec_ref[...] = vec

        @pl.when(i > 0)
        def _():
            vec_ref[...] += vec

    row = pl.BlockSpec((1, D_MODEL), lambda i: (0, 0))
    tok = pl.BlockSpec((tm, D_MODEL), lambda i: (i, 0))
    tok3 = pl.BlockSpec((None, tm, D_MODEL), lambda i: (0, i, 0))
    return pl.pallas_call(
        body, name="inproj_bwd_x", grid=(T // tm,),
        in_specs=[tok3, tok, row, row,
                  pl.BlockSpec((D_MODEL, IN_PAD), lambda i: (0, 0), pipeline_mode=pl.Buffered(1))] + sspecs,
        out_specs=(tok3, pl.BlockSpec((8, D_MODEL), lambda i: (0, 0))),
        out_shape=(jax.ShapeDtypeStruct((1, T, D_MODEL), _F32), jax.ShapeDtypeStruct((8, D_MODEL), _F32)),
        compiler_params=_cparams(("arbitrary",)),
    )(x, dx1, scale1p, norm_w, w_in_bf, *sargs)


def _inproj_bwd_w(h_t, sections, T):
    tm = 1024
    n_i = T // tm
    sspecs, sargs = _section_specs(sections, tm)
    ns = len(sargs)

    def body(*refs):
        h_ref = refs[0]
        srefs = refs[1:1 + ns]
        dw_ref, db_ref, acc, sem = refs[1 + ns:]
        i = pl.program_id(0)

        @pl.when(i == 0)
        def _():
            acc[...] = jnp.zeros_like(acc)
            db_ref[...] = jnp.zeros_like(db_ref)

        hv = h_ref[...]
        for (c0, width, _), v in zip(sections, _section_values(sections, srefs, _F32)):
            acc[:, c0:c0 + width] += _nn(hv, v.astype(_BF16))
            db_ref[0:1, c0:c0 + width] += jnp.sum(v, axis=0, keepdims=True)

        @pl.when(i == n_i - 1)
        def _():
            cp = pltpu.make_async_copy(acc, dw_ref, sem)
            cp.start()
            cp.wait()

    return pl.pallas_call(
        body, name="inproj_bwd_w", grid=(n_i,),
        in_specs=[pl.BlockSpec((D_MODEL, tm), lambda i: (0, i))] + sspecs,
        out_specs=(pl.BlockSpec(memory_space=pl.ANY), pl.BlockSpec((8, IN_PAD), lambda i: (0, 0))),
        out_shape=(jax.ShapeDtypeStruct((D_MODEL, IN_PAD), _F32), jax.ShapeDtypeStruct((8, IN_PAD), _F32)),
        scratch_shapes=[pltpu.VMEM((D_MODEL, IN_PAD), _F32), pltpu.SemaphoreType.DMA],
        compiler_params=_cparams(("arbitrary",)),
    )(h_t, *sargs)


def _adamw_math(w, g, m, v):
    m = ADAM_B1 * m + (1.0 - ADAM_B1) * g
    v = ADAM_B2 * v + (1.0 - ADAM_B2) * jnp.square(g)
    m_hat = m / (1.0 - ADAM_B1 ** ADAM_STEP)
    v_hat = v / (1.0 - ADAM_B2 ** ADAM_STEP)
    delta = -ADAM_LR * (m_hat / (jnp.sqrt(v_hat) + ADAM_EPS) + ADAM_WD * w)
    return delta, m, v


def _adamw_slots(w, m, v, slots, tr, name, own=None):
    R, C = w.shape
    extra = [] if own is None else [own]

    def body(w_ref, m_ref, v_ref, s_ref, *refs):
        g_ref, d_ref, nm_ref, nv_ref = refs[len(extra):]
        g = s_ref[0].astype(_F32)
        for k in range(1, N_DEV):
            g = g + s_ref[k].astype(_F32)
        if extra:
            g = g + refs[0][...].astype(_F32)
        g_ref[...] = g
        d_ref[...], nm_ref[...], nv_ref[...] = _adamw_math(w_ref[...], g, m_ref[...], v_ref[...])

    blk = pl.BlockSpec((tr, C), lambda i: (i, 0))
    return pl.pallas_call(
        body, name=name, grid=(R // tr,),
        in_specs=[blk, blk, blk, pl.BlockSpec((N_DEV, tr, C), lambda i: (0, i, 0))] + [blk] * len(extra),
        out_specs=(blk, blk, blk, blk),
        out_shape=tuple(jax.ShapeDtypeStruct((R, C), _F32) for _ in range(4)),
        compiler_params=_cparams(("arbitrary",)),
    )(w, m, v, slots, *extra)


def _w_ada_update(c_all, dmod_my, w, m, v):
    def body(c_ref, d_ref, w_ref, m_ref, v_ref, g_ref, dl_ref, nm_ref, nv_ref):
        g = lax.dot_general(_silu(c_ref[...]), d_ref[...], (((0,), (0,)), ((), ())),
                            precision=_HI, preferred_element_type=_F32)
        g_ref[...] = g
        dl_ref[...], nm_ref[...], nv_ref[...] = _adamw_math(w_ref[...], g, m_ref[...], v_ref[...])

    return pl.pallas_call(
        body, name="w_ada_update",
        out_shape=tuple(jax.ShapeDtypeStruct(w.shape, _F32) for _ in range(4)),
        compiler_params=_cparams(),
    )(c_all, dmod_my, w, m, v)


_PACK = (("b_ada", 3072, 3072), ("norm_w", 1024, 1024), ("b_in", IN_W, IN_PAD), ("conv_w", 5120, 5120),
         ("conv_b", 1024, 1024), ("rpb", 3720, 3840), ("ml_norm_w", 512, 512), ("final_norm_w", 1024, 1024),
         ("loss", 1, 128))
_PACK_OFF = {}
_off = 0
for _name, _len, _pad in _PACK:
    _PACK_OFF[_name] = (_off, _len)
    _off += _pad
_PACK_LEN = _off


def _pack(parts):
    cols = []
    for name, length, pad in _PACK:
        vec = parts[name].reshape(-1).astype(_F32)
        cols.append(jnp.pad(vec, (0, pad - length)))
    return jnp.concatenate(cols).reshape(1, _PACK_LEN)


def _unpack(vec, name, shape):
    off, length = _PACK_OFF[name]
    return vec.reshape(-1)[off:off + length].reshape(shape)


def kernel(x, c, w_ada, b_ada, norm_w, w_in, b_in, conv_w, conv_b, rpb, ml_norm_w, w_out, final_norm_w, loss_target, m_w_ada, m_b_ada, m_norm_w, m_w_in, m_b_in, m_conv_w, m_conv_b, m_rpb, m_ml_norm_w, m_w_out, m_final_norm_w, v_w_ada, v_b_ada, v_norm_w, v_w_in, v_b_in, v_conv_w, v_conv_b, v_rpb, v_ml_norm_w, v_w_out, v_final_norm_w):
    T = x.shape[1]
    rows = T // GRID_W
    me = 4 * lax.axis_index("x") + 2 * lax.axis_index("y") + lax.axis_index("c")
    n_in = w_in.shape[2]
    n_ada = w_ada.shape[2]
    n_cw = conv_w.shape[2]
    n_wo = w_out.shape[1]

    w_in_my, w_out_my = w_in[0].astype(_BF16), w_out[0].astype(_BF16)
    start_in = _scatter_start([w_in_my], "w_in_start", scatter=False)
    start_out = _scatter_start([w_out_my], "w_out_start", scatter=False)
    tokens = start_in[-1][0:1, 0:1] + start_out[-1][0:1, 0:1]
    g_conv_w, g_c = _exchange([conv_w[0], c + tokens], [False] * 2, "gather_small")
    b_in_pad = jnp.pad(b_in, ((0, 0), (0, IN_PAD - IN_W)))
    conv_w_full = jnp.pad(g_conv_w.transpose(1, 0, 2).reshape(CONV_W, N_DEV * n_cw), ((0, 3), (0, 0)))
    c_all = g_c.reshape(N_DEV, D_MODEL)

    b_ada_my = lax.dynamic_slice(b_ada, (0, me * n_ada), (1, n_ada))
    (mod_slots,) = _exchange([_mod_part(c_all, w_ada[0], b_ada_my)], [False], "gather_mod")
    mod = lax.dynamic_index_in_dim(mod_slots, me, axis=1, keepdims=False).reshape(1, 3 * D_MODEL)
    shift, scale, gate = mod[:, :D_MODEL], mod[:, D_MODEL:2 * D_MODEL], mod[:, 2 * D_MODEL:]
    scale1p = 1.0 + scale
    bias = _na_bias_tables(rpb[0], rows)

    def gathered(started, after, name):
        (own,), (land,) = _scatter_wait(started, after, name, scatter=False)
        return lax.dynamic_update_slice(land, own[None], (me,) + (0,) * own.ndim)

    g_w_in = gathered(start_in, bias[0, 0, :8, :128] + scale1p[:, :128], "w_in_wait")
    w_in_full = g_w_in.transpose(1, 0, 2).reshape(D_MODEL, N_DEV * n_in)
    w_in_bf = jnp.pad(w_in_full, ((0, 0), (0, IN_PAD - IN_W)))

    qkv, rest, h_bf = _inproj_fwd(x, scale1p, shift, norm_w, w_in_bf, b_in_pad)
    o_na, lse = _na_fwd(qkv, bias, T)
    qk_act = _conv_fwd(rest, conv_w_full, conv_b, T)
    h_f, cs_f, ns_f, ms_f = _mlstm_fwd(qk_act, rest, T, False)
    h_b, cs_b, ns_b, ms_b = _mlstm_fwd(qk_act, rest, T, True)

    w_out_bf = gathered(start_out, ms_b, "w_out_wait").reshape(N_DEV * n_wo, D_MODEL)
    dx1, d_o, dz_na, d_h, d_mo, d_mz, dwo, pvec = _post(
        x, loss_target, o_na, rest, h_f, h_b, gate, ml_norm_w, final_norm_w.reshape(1, D_MODEL), w_out_bf, T)

    dq_na, dk_na, dv_na, dbias = _na_bwd(qkv, bias, o_na, d_o, lse, T)
    d_rpb = _rpb_grad(dbias, rows)
    dqk_f, dv_f, dg_f = _mlstm_bwd(qk_act, rest, d_h, cs_f, ns_f, ms_f, T, False)
    dqk_b, dv_b, dg_b = _mlstm_bwd(qk_act, rest, d_h, cs_b, ns_b, ms_b, T, True)
    d_u, dconv = _conv_bwd(rest, conv_w_full, conv_b, dqk_f, dqk_b, T)

    sections = [(0, 512, [(dq_na, 0)]), (512, 512, [(dk_na, 0)]), (1024, 512, [(dv_na, 0)]),
                (1536, 512, [(dz_na, 0)]), (2048, 512, [(d_u, 0)]), (2560, 512, [(d_u, 1)]),
                (3072, 512, [(dv_f, 0), (dv_b, 0)]), (3584, 512, [(d_mo, 0)]), (4096, 512, [(d_mz, 0)]),
                (4608, 128, [(dg_f, 0), (dg_b, 0)])]
    dw_pad, db_pad = _inproj_bwd_w(h_bf, sections, T)
    db_in = db_pad[0, :IN_W]

    dw_blocks = dw_pad[:, :IN_W].astype(_BF16).reshape(D_MODEL, N_DEV, n_in).transpose(1, 0, 2)
    dwo_blocks = dwo.astype(_BF16).reshape(N_DEV, n_wo, D_MODEL)
    started = _scatter_start([dw_blocks, dwo_blocks], "grads_start")
    grad_x, xvec = _inproj_bwd_x(x, dx1, scale1p + started[-1][0:1, 0:1], norm_w, w_in_bf, sections, T)
    (dw_blocks, dwo_blocks), (s_w_in, s_w_out) = _scatter_wait(started, xvec, "grads_wait")

    small = _pack({
        "b_ada": jnp.concatenate([xvec[0], xvec[1], pvec[1]]),
        "norm_w": xvec[2], "b_in": db_in, "conv_w": dconv[:CONV_W], "conv_b": dconv[CONV_W],
        "rpb": d_rpb, "ml_norm_w": pvec[2, :ML_W], "final_norm_w": pvec[0], "loss": pvec[3, :1]})
    (s_small,) = _exchange([small], [False], "exchange_small")

    own = lambda blocks: lax.dynamic_index_in_dim(blocks, me, axis=0, keepdims=False)
    g_w_in_s, d_w_in, nm_w_in, nv_w_in = _adamw_slots(
        w_in[0], m_w_in[0], v_w_in[0], s_w_in, 128, "adamw_w_in", own=own(dw_blocks))
    g_w_out_s, d_w_out, nm_w_out, nv_w_out = _adamw_slots(
        w_out[0], m_w_out[0], v_w_out[0], s_w_out, n_wo, "adamw_w_out", own=own(dwo_blocks))
    dmod_all = s_small[:, 0, :3 * D_MODEL]
    dmod_my = lax.dynamic_slice(dmod_all, (0, me * n_ada), (N_DEV, n_ada))
    g_w_ada, d_w_ada, nm_w_ada, nv_w_ada = _w_ada_update(c_all, dmod_my, w_ada[0], m_w_ada[0], v_w_ada[0])

    def embed(shard):
        return lax.dynamic_update_slice(jnp.zeros((CONV_W, N_DEV * n_cw), _F32), shard[0], (0, me * n_cw))

    zero1 = jnp.zeros((1,), _F32)
    packed = lambda b_a, n_w, b_i, c_w, c_b, rp, mn, fn: _pack({
        "b_ada": b_a, "norm_w": n_w, "b_in": b_i, "conv_w": embed(c_w), "conv_b": c_b, "rpb": rp,
        "ml_norm_w": mn, "final_norm_w": fn, "loss": zero1})
    pw = packed(b_ada, norm_w, b_in, conv_w, conv_b, rpb, ml_norm_w, final_norm_w)
    pm = packed(m_b_ada, m_norm_w, m_b_in, m_conv_w, m_conv_b, m_rpb, m_ml_norm_w, m_final_norm_w)
    pv = packed(v_b_ada, v_norm_w, v_b_in, v_conv_w, v_conv_b, v_rpb, v_ml_norm_w, v_final_norm_w)
    sg, sd, sm, sv = _adamw_slots(pw, pm, pv, s_small, 1, "adamw_small")

    def small_outs(vec):
        cw = lax.dynamic_slice(_unpack(vec, "conv_w", (CONV_W, N_DEV * n_cw)), (0, me * n_cw), (CONV_W, n_cw))
        return dict(b_ada=_unpack(vec, "b_ada", b_ada.shape), norm_w=_unpack(vec, "norm_w", norm_w.shape),
                    b_in=_unpack(vec, "b_in", b_in.shape), conv_w=cw[None],
                    conv_b=_unpack(vec, "conv_b", conv_b.shape), rpb=_unpack(vec, "rpb", rpb.shape),
                    ml_norm_w=_unpack(vec, "ml_norm_w", ml_norm_w.shape),
                    final_norm_w=_unpack(vec, "final_norm_w", final_norm_w.shape))

    loss = _unpack(sg, "loss", ())
    order = ("w_ada", "b_ada", "norm_w", "w_in", "b_in", "conv_w", "conv_b", "rpb", "ml_norm_w", "w_out",
             "final_norm_w")
    outs = []
    for vec, big in ((sg, (g_w_ada, g_w_in_s, g_w_out_s)), (sd, (d_w_ada, d_w_in, d_w_out)),
                     (sm, (nm_w_ada, nm_w_in, nm_w_out)), (sv, (nv_w_ada, nv_w_in, nv_w_out))):
        group = small_outs(vec)
        group.update(w_ada=big[0][None], w_in=big[1][None], w_out=big[2][None])
        outs.extend(group[name] for name in order)
    return (loss, grad_x, *outs)
```

```python
import functools

import numpy as np
import jax
import jax.numpy as jnp
from jax import lax
from jax.experimental import pallas as pl
from jax.experimental.pallas import tpu as pltpu

N_DEV = 8
D_MODEL = 1024
GRID_W = 64
NA_HEADS = 8
NA_HEAD_DIM = 64
NA_KH = 8
NA_KW = 16
NA_W = 512
ML_HEADS = 4
ML_HEAD_DIM = 128
ML_W = 512
ML_CHUNK = 512
CONV_W = 5
EPS = 1e-6
IN_W = 4624
IN_PAD = 4736
REST_W = IN_PAD - 3 * NA_W
GATE_COL = 3072
NEG = -1e30
NA_RB = 4
NA_WIN = 12
NA_SUB = 2
ML_CB = 1
ADAM_LR = 0.001
ADAM_B1 = 0.9
ADAM_B2 = 0.999
ADAM_EPS = 1e-08
ADAM_WD = 0.01
ADAM_STEP = 10
VMEM_LIMIT = 56 * 1024 * 1024

_F32 = jnp.float32
_BF16 = jnp.bfloat16
_HI = lax.Precision.HIGHEST


def _cparams(sem=None):
    return pltpu.CompilerParams(dimension_semantics=sem, vmem_limit_bytes=VMEM_LIMIT)


def _nt(a, b):
    return lax.dot_general(a, b, (((1,), (1,)), ((), ())), preferred_element_type=_F32)


def _tn(a, b):
    return lax.dot_general(a, b, (((0,), (0,)), ((), ())), preferred_element_type=_F32)


def _nn(a, b):
    return jnp.dot(a, b, preferred_element_type=_F32)


def _sigmoid(x):
    return 1.0 / (1.0 + jnp.exp(-x))


def _silu(x):
    return x * _sigmoid(x)


def _dsilu(x):
    s = _sigmoid(x)
    return s * (1.0 + x * (1.0 - s))


def _exchange(arrs, scatter, name):
    n = len(arrs)
    out_shape = []
    for a, sc in zip(arrs, scatter):
        blk = a.shape[1:] if sc else a.shape
        out_shape.append(jax.ShapeDtypeStruct((N_DEV,) + tuple(blk), a.dtype))

    def body(*refs):
        ins = refs[:n]
        outs = refs[n:2 * n]
        send_sems, recv_sems, local_sems = refs[2 * n:]
        x, y, c = lax.axis_index("x"), lax.axis_index("y"), lax.axis_index("c")
        me = 4 * x + 2 * y + c
        local, sends, recvs = [], [], []
        for a in range(n):
            own = ins[a].at[me] if scatter[a] else ins[a]
            cp = pltpu.make_async_copy(own, outs[a].at[me], local_sems.at[a])
            cp.start()
            local.append(cp)
            for k in range(1, N_DEV):
                px = 1 - x if k & 4 else x
                py = 1 - y if k & 2 else y
                pc = 1 - c if k & 1 else c
                p = 4 * px + 2 * py + pc
                src = ins[a].at[p] if scatter[a] else ins[a]
                snd = pltpu.make_async_remote_copy(
                    src_ref=src, dst_ref=outs[a].at[me],
                    send_sem=send_sems.at[a, k - 1], recv_sem=recv_sems.at[a, k - 1],
                    device_id=(px, py, pc), device_id_type=pl.DeviceIdType.MESH)
                snd.start()
                sends.append(snd)
                rcv = pltpu.make_async_remote_copy(
                    src_ref=src, dst_ref=outs[a].at[p],
                    send_sem=send_sems.at[a, k - 1], recv_sem=recv_sems.at[a, k - 1],
                    device_id=(px, py, pc), device_id_type=pl.DeviceIdType.MESH)
                recvs.append(rcv)
        for rcv in recvs:
            rcv.wait_recv()
        for snd in sends:
            snd.wait_send()
        for cp in local:
            cp.wait()

    any_spec = pl.BlockSpec(memory_space=pl.ANY)
    res = pl.pallas_call(
        body, name=name, out_shape=tuple(out_shape),
        in_specs=[any_spec] * n, out_specs=tuple([any_spec] * n),
        scratch_shapes=[pltpu.SemaphoreType.DMA((n, N_DEV - 1)),
                        pltpu.SemaphoreType.DMA((n, N_DEV - 1)),
                        pltpu.SemaphoreType.DMA((n,))],
    )(*arrs)
    return list(res)


def _peer(k):
    x, y, c = lax.axis_index("x"), lax.axis_index("y"), lax.axis_index("c")
    px = 1 - x if k & 4 else x
    py = 1 - y if k & 2 else y
    pc = 1 - c if k & 1 else c
    return (px, py, pc), 4 * px + 2 * py + pc, 4 * x + 2 * y + c


def _scatter_copy(srcs, lands, send_sems, recv_sems, a, k, receive, scatter):
    dev, p, me = _peer(k)
    return pltpu.make_async_remote_copy(
        src_ref=srcs[a].at[p] if scatter else srcs[a], dst_ref=lands[a].at[p if receive else me],
        send_sem=send_sems[a * (N_DEV - 1) + k - 1], recv_sem=recv_sems[a * (N_DEV - 1) + k - 1],
        device_id=dev, device_id_type=pl.DeviceIdType.MESH)


def _scatter_start(arrs, name, scatter=True):
    n = len(arrs)
    ns = n * (N_DEV - 1)
    hbm = pl.BlockSpec(memory_space=pltpu.HBM)
    sem = pl.BlockSpec(memory_space=pltpu.SEMAPHORE)

    def body(*refs):
        srcs, lands = refs[:n], refs[n:2 * n]
        send_sems, recv_sems = refs[2 * n:2 * n + ns], refs[2 * n + ns:2 * n + 2 * ns]
        token = refs[-1]
        for a in range(n):
            for k in range(1, N_DEV):
                _scatter_copy(srcs, lands, send_sems, recv_sems, a, k, False, scatter).start()
        token[...] = jnp.zeros_like(token)

    land_shapes = [a.shape if scatter else (N_DEV,) + a.shape for a in arrs]
    buffers = [pltpu.HBM(a.shape, a.dtype) for a in arrs]
    land_buffers = [pltpu.HBM(s, a.dtype) for s, a in zip(land_shapes, arrs)]
    sems = [pltpu.SemaphoreType.DMA(()) for _ in range(2 * ns)]
    res = pl.pallas_call(
        body, name=name,
        out_shape=(*sems, *buffers, *land_buffers, jax.ShapeDtypeStruct((8, 128), _F32)),
        in_specs=[hbm] * (2 * n),
        out_specs=(*([sem] * (2 * ns)), *([hbm] * (2 * n)), pl.BlockSpec(memory_space=pltpu.VMEM)),
        input_output_aliases={i: 2 * ns + i for i in range(2 * n)},
        compiler_params=pltpu.CompilerParams(has_side_effects=pltpu.SideEffectType.DATAFLOW_SIDE_EFFECTING),
    )(*[pltpu.with_memory_space_constraint(a, pltpu.HBM) for a in arrs],
      *[pltpu.with_memory_space_constraint(jnp.zeros(s, a.dtype), pltpu.HBM) for s, a in zip(land_shapes, arrs)])
    res = list(res)
    return (res[:ns], res[ns:2 * ns], res[2 * ns:2 * ns + n], res[2 * ns + n:2 * ns + 2 * n], res[-1])


def _scatter_wait(started, after, name, scatter=True):
    send_sems, recv_sems, srcs, lands, _ = started
    n = len(srcs)
    ns = len(send_sems)
    hbm = pl.BlockSpec(memory_space=pltpu.HBM)
    sem = pl.BlockSpec(memory_space=pltpu.SEMAPHORE)

    def body(*refs):
        src_refs, land_refs = refs[:n], refs[n:2 * n]
        s_sems, r_sems = refs[2 * n:2 * n + ns], refs[2 * n + ns:2 * n + 2 * ns]
        for a in range(n):
            for k in range(1, N_DEV):
                _scatter_copy(src_refs, land_refs, s_sems, r_sems, a, k, False, scatter).wait_send()
                _scatter_copy(src_refs, land_refs, s_sems, r_sems, a, k, True, scatter).wait_recv()

    buffers = [pltpu.HBM(a.shape, a.dtype) for a in list(srcs) + list(lands)]
    res = pl.pallas_call(
        body, name=name, out_shape=tuple(buffers),
        in_specs=[hbm] * (2 * n) + [sem] * (2 * ns) + [pl.BlockSpec(memory_space=pl.ANY)],
        out_specs=tuple([hbm] * (2 * n)),
        input_output_aliases={i: i for i in range(2 * n)},
        compiler_params=pltpu.CompilerParams(has_side_effects=pltpu.SideEffectType.DATAFLOW_SIDE_EFFECTING),
    )(*srcs, *lands, *send_sems, *recv_sems, after)
    return list(res[:n]), list(res[n:])


def _mod_part(c_all, w_ada, b_my):
    def body(c_ref, w_ref, b_ref, o_ref):
        o_ref[...] = jnp.dot(_silu(c_ref[...]), w_ref[...], precision=_HI,
                             preferred_element_type=_F32) + b_ref[...]

    return pl.pallas_call(
        body, name="mod_part",
        out_shape=jax.ShapeDtypeStruct((N_DEV, w_ada.shape[1]), _F32),
        compiler_params=_cparams(),
    )(c_all, w_ada, b_my)


def _inproj_fwd(x, scale1p, shift, norm_w, w_in_bf, b_in_pad):
    T = x.shape[1]
    tm = 512
    n_q = 3 * NA_W

    def body(x_ref, sc_ref, sh_ref, nw_ref, w_ref, b_ref, qkv_ref, rest_ref, h_ref):
        xv = x_ref[...]
        r = lax.rsqrt(jnp.mean(xv * xv, axis=-1, keepdims=True) + EPS)
        h = xv * r * nw_ref[...] * sc_ref[...] + sh_ref[...]
        hb = h.astype(_BF16)
        h_ref[...] = h.T.astype(_BF16)
        for n0 in range(0, IN_PAD, 512):
            wd = min(512, IN_PAD - n0)
            acc = _nn(hb, w_ref[:, n0:n0 + wd]) + b_ref[:, n0:n0 + wd]
            if n0 == 0:
                acc = acc * (NA_HEAD_DIM ** -0.5)
            if n0 < n_q:
                qkv_ref[:, n0:n0 + wd] = acc.astype(_BF16)
            else:
                rest_ref[:, n0 - n_q:n0 - n_q + wd] = acc

    row = lambda w: pl.BlockSpec((1, w), lambda i: (0, 0))
    return pl.pallas_call(
        body, name="inproj_fwd", grid=(T // tm,),
        in_specs=[pl.BlockSpec((None, tm, D_MODEL), lambda i: (0, i, 0)), row(D_MODEL), row(D_MODEL), row(D_MODEL),
                  pl.BlockSpec((D_MODEL, IN_PAD), lambda i: (0, 0), pipeline_mode=pl.Buffered(1)), row(IN_PAD)],
        out_specs=(pl.BlockSpec((tm, n_q), lambda i: (i, 0)),
                   pl.BlockSpec((tm, REST_W), lambda i: (i, 0)),
                   pl.BlockSpec((D_MODEL, tm), lambda i: (0, i))),
        out_shape=(jax.ShapeDtypeStruct((T, n_q), _BF16),
                   jax.ShapeDtypeStruct((T, REST_W), _F32),
                   jax.ShapeDtypeStruct((D_MODEL, T), _BF16)),
        compiler_params=_cparams(("arbitrary",)),
    )(x, scale1p, shift, norm_w, w_in_bf, b_in_pad)


def _na_class_rows(rows):
    nb = rows // NA_RB
    out = []
    for rb in (0, min(1, nb - 1), nb - 1):
        ws = int(np.clip(NA_RB * rb - 4, 0, rows - NA_WIN))
        out.append((NA_RB * rb + np.arange(NA_RB), ws + np.arange(NA_WIN)))
    return out


def _na_pair_index(rows, qrows, krows):
    start = lambda r: np.clip(r - NA_KH // 2, 0, rows - NA_KH)
    col = np.arange(GRID_W)
    cstart = np.clip(col - NA_KW // 2, 0, GRID_W - NA_KW)
    dy = krows[None, :] - qrows[:, None] + NA_KH - 1
    vr = (krows[None, :] >= start(qrows)[:, None]) & (krows[None, :] < start(qrows)[:, None] + NA_KH)
    dx = np.clip(col[None, :] - col[:, None], -(NA_KW - 1), NA_KW - 1) + NA_KW - 1
    vc = (col[None, :] >= cstart[:, None]) & (col[None, :] < cstart[:, None] + NA_KW)
    nq, nk = len(qrows), len(krows)
    dy4 = np.broadcast_to(np.clip(dy, 0, 2 * NA_KH - 2)[:, None, :, None], (nq, GRID_W, nk, GRID_W))
    dx4 = np.broadcast_to(dx[None, :, None, :], (nq, GRID_W, nk, GRID_W))
    valid = vr[:, None, :, None] & vc[None, :, None, :]
    idx = (dy4 * (2 * NA_KW - 1) + dx4).reshape(nq * GRID_W, nk * GRID_W)
    return idx.astype(np.int32), valid.reshape(nq * GRID_W, nk * GRID_W), (dy, vr, dx, vc)


def _na_half_slabs(rpb):
    _, _, (_, _, dx, vc) = _na_pair_index(NA_WIN, np.arange(1), np.arange(1))
    qc, kc = np.meshgrid(np.arange(GRID_W), np.arange(GRID_W), indexing="ij")
    consts = []
    for right in (False, True):
        pos = (qc * 128 + (GRID_W if right else 0) + kc).reshape(-1)
        oh = np.zeros((32, GRID_W * 128), np.float32)
        oh[dx[qc, kc].reshape(-1), pos] = 1.0
        col_neg = np.zeros((1, GRID_W * 128), np.float32)
        col_neg[0, pos] = np.where(vc[qc, kc].reshape(-1), 0.0, NEG)
        half = np.zeros((1, GRID_W * 128), np.float32)
        half[0, pos] = 1.0
        consts += [jnp.asarray(oh), jnp.asarray(col_neg), jnp.asarray(half)]
    row_neg = np.where(np.arange(NA_HEADS * 16) % 16 == 15, NEG, 0.0).astype(np.float32).reshape(-1, 1)
    rp = jnp.pad(rpb, ((0, 0), (0, 1), (0, 1))).reshape(NA_HEADS * 16, 32)

    def body(*refs):
        r_ref, rn_ref = refs[0], refs[1]
        for t in range(2):
            oh_ref, cn_ref, half_ref = refs[2 + 3 * t:5 + 3 * t]
            refs[8 + t][...] = (jnp.dot(r_ref[...], oh_ref[...], precision=_HI, preferred_element_type=_F32)
                                + cn_ref[...] + rn_ref[...] * half_ref[...])

    outs = pl.pallas_call(
        body, name="na_half_slabs",
        out_shape=tuple(jax.ShapeDtypeStruct((NA_HEADS * 16, GRID_W * 128), _F32) for _ in range(2)),
        compiler_params=_cparams(),
    )(rp, jnp.asarray(row_neg), *consts)
    return [o.reshape(NA_HEADS, 16, GRID_W, 128) for o in outs]


def _na_bias_tables(rpb, rows):
    left, right = _na_half_slabs(rpb)
    didx = []
    for blk, win in _na_class_rows(rows):
        _, _, (dy, vr, _, _) = _na_pair_index(rows, blk, win)
        didx.append(np.where(vr, dy, 15))

    def body(l_ref, r_ref, b_ref):
        for ci, tab in enumerate(didx):
            for a in range(NA_RB):
                for j in range(NA_WIN // 2):
                    b_ref[ci, 0, a * GRID_W:(a + 1) * GRID_W, j * 128:(j + 1) * 128] = (
                        l_ref[0, int(tab[a, 2 * j])] + r_ref[0, int(tab[a, 2 * j + 1])])

    slab = pl.BlockSpec((1, 16, GRID_W, 128), lambda h: (h, 0, 0, 0))
    return pl.pallas_call(
        body, name="na_tables", grid=(NA_HEADS,),
        in_specs=[slab] * 2,
        out_specs=pl.BlockSpec((3, 1, NA_RB * GRID_W, NA_WIN * GRID_W), lambda h: (0, h, 0, 0)),
        out_shape=jax.ShapeDtypeStruct((3, NA_HEADS, NA_RB * GRID_W, NA_WIN * GRID_W), _F32),
        compiler_params=_cparams(("arbitrary",)),
    )(left, right)


def _stack_heads(x, first):
    zero = jnp.zeros_like(x)
    return jnp.concatenate([jnp.where(first, x, zero), jnp.where(first, zero, x)], axis=0)


def _na_sub(rb, u, rows):
    sb = NA_SUB * rb + u
    nb = rows // NA_RB
    cls = jnp.where(sb == 0, 0, jnp.where(sb == nb - 1, 2, 1))
    ws = pl.multiple_of(jnp.clip(NA_RB * sb - 4, 0, rows - NA_WIN) * GRID_W, 256)
    return cls, ws


def _na_fwd(qkv, bias, T):
    rows = T // GRID_W
    tq = NA_RB * GRID_W
    tw = NA_WIN * GRID_W
    ts = NA_SUB * tq

    def body(q_ref, k_ref, v_ref, b_ref, o_ref, l_ref):
        rb = pl.program_id(1)
        lane = lax.broadcasted_iota(jnp.int32, (1, 128), 1)
        first = lane < NA_HEAD_DIM
        for u in range(NA_SUB):
            cls, ws = _na_sub(rb, u, rows)
            kw = k_ref[pl.ds(ws, tw), :]
            vw = v_ref[pl.ds(ws, tw), :]
            q2 = _stack_heads(q_ref[u * tq:(u + 1) * tq, :], first)
            s = _nt(q2, kw) + b_ref[cls].reshape(2 * tq, tw)
            m = jnp.max(s, axis=1, keepdims=True)
            p = jnp.exp(s - m)
            l = jnp.sum(p, axis=1, keepdims=True)
            o2 = _nn(p.astype(_BF16), vw) / l
            lse2 = m + jnp.log(l)
            o_ref[u * tq:(u + 1) * tq, :] = jnp.where(first, o2[:tq], o2[tq:])
            l_ref[u * tq:(u + 1) * tq, :] = jnp.where(first, lse2[:tq], lse2[tq:])

    blk = lambda off: pl.BlockSpec((ts, 128), lambda hp, rb: (rb, off + hp))
    whole = lambda off: pl.BlockSpec((T, 128), lambda hp, rb: (0, off + hp))
    return pl.pallas_call(
        body, name="na_fwd", grid=(NA_HEADS // 2, T // ts),
        in_specs=[blk(0), whole(4), whole(8),
                  pl.BlockSpec((3, 2, tq, tw), lambda hp, rb: (0, hp, 0, 0))],
        out_specs=(blk(0), blk(0)),
        out_shape=(jax.ShapeDtypeStruct((T, NA_W), _F32), jax.ShapeDtypeStruct((T, NA_W), _F32)),
        compiler_params=_cparams(("arbitrary", "arbitrary")),
    )(qkv, qkv, qkv, bias)


def _na_bwd(qkv, bias, o, d_o, lse, T):
    rows = T // GRID_W
    tq = NA_RB * GRID_W
    tw = NA_WIN * GRID_W
    ts = NA_SUB * tq

    def body(q_ref, k_ref, v_ref, b_ref, o_ref, do_ref, l_ref, dq_ref, dk_ref, dv_ref, db_ref):
        rb = pl.program_id(1)
        lane = lax.broadcasted_iota(jnp.int32, (1, 128), 1)
        first = lane < NA_HEAD_DIM

        @pl.when(rb == 0)
        def _():
            db_ref[...] = jnp.zeros_like(db_ref)
            dk_ref[...] = jnp.zeros_like(dk_ref)
            dv_ref[...] = jnp.zeros_like(dv_ref)

        for u in range(NA_SUB):
            cls, ws = _na_sub(rb, u, rows)
            kw = k_ref[pl.ds(ws, tw), :]
            vw = v_ref[pl.ds(ws, tw), :]
            sl = slice(u * tq, (u + 1) * tq)
            q = q_ref[sl, :]
            d_ov = do_ref[sl, :]
            prod = d_ov.astype(_F32) * o_ref[sl, :]
            lse_v = l_ref[sl, :]
            dqs = []
            dk_win = jnp.zeros((tw, 128), _F32)
            dv_win = jnp.zeros((tw, 128), _F32)
            for hh in range(2):
                msk = first if hh == 0 else jnp.logical_not(first)
                c0 = hh * NA_HEAD_DIM
                qm = jnp.where(msk, q, jnp.zeros_like(q))
                dom = jnp.where(msk, d_ov, jnp.zeros_like(d_ov))
                s = _nt(qm, kw) + b_ref[cls, hh]
                p = jnp.exp(s - lse_v[:, c0:c0 + 1])
                dp = _nt(dom, vw)
                delta = jnp.sum(jnp.where(msk, prod, 0.0), axis=1, keepdims=True)
                ds = p * (dp - delta)
                db_ref[cls, hh] += ds
                dsb = ds.astype(_BF16)
                dqs.append(_nn(dsb, kw) * (NA_HEAD_DIM ** -0.5))
                dk_win = dk_win + _tn(dsb, qm)
                dv_win = dv_win + _tn(p.astype(_BF16), dom)
            dq_ref[sl, :] = jnp.where(first, dqs[0], dqs[1]).astype(_BF16)
            dk_ref[pl.ds(ws, tw), :] += dk_win
            dv_ref[pl.ds(ws, tw), :] += dv_win

    once = pl.Buffered(1)
    blk = lambda off: pl.BlockSpec((ts, 128), lambda hp, rb: (rb, off + hp))
    whole = lambda off: pl.BlockSpec((T, 128), lambda hp, rb: (0, off + hp), pipeline_mode=once)
    tab = pl.BlockSpec((3, 2, tq, tw), lambda hp, rb: (0, hp, 0, 0), pipeline_mode=once)
    return pl.pallas_call(
        body, name="na_bwd", grid=(NA_HEADS // 2, T // ts),
        in_specs=[blk(0), whole(4), whole(8), tab, blk(0), blk(0), blk(0)],
        out_specs=(blk(0), whole(0), whole(0), tab),
        out_shape=(jax.ShapeDtypeStruct((T, NA_W), _BF16), jax.ShapeDtypeStruct((T, NA_W), _F32),
                   jax.ShapeDtypeStruct((T, NA_W), _F32), jax.ShapeDtypeStruct(bias.shape, _F32)),
        compiler_params=_cparams(("arbitrary", "arbitrary")),
    )(qkv, qkv, qkv, bias, o, d_o, lse)


def _rpb_grad(dbias, rows):
    tw = NA_WIN * GRID_W
    lanes = 16 * GRID_W
    offs = [int(win[0] - blk[0] + NA_KH - 1) for blk, win in _na_class_rows(rows)]

    def body(x_ref, g_ref):
        sub = lax.broadcasted_iota(jnp.int32, (NA_RB, 1), 0)
        qc = lax.broadcasted_iota(jnp.int32, (NA_RB * GRID_W, 1), 0) % GRID_W
        tot = jnp.zeros((NA_RB, lanes), _F32)
        for ci in range(3):
            xv = x_ref[ci, 0]
            for bit in range(6):
                xv = jnp.where(((qc >> bit) & 1) == 1, pltpu.roll(xv, tw - (1 << bit), 1), xv)
            acc = pltpu.roll(jnp.sum(xv.reshape(NA_RB, GRID_W, tw), axis=1), NA_KW, 1)
            acc = jnp.concatenate([acc, jnp.zeros((NA_RB, lanes - tw), _F32)], axis=1)
            for a in range(NA_RB):
                tot = tot + jnp.where(sub == a, pltpu.roll(acc, (GRID_W * (offs[ci] - a)) % lanes, 1), 0.0)
        g_ref[0] = jnp.broadcast_to(jnp.sum(tot, axis=0, keepdims=True), (8, lanes))

    g = pl.pallas_call(
        body, name="rpb_grad", grid=(NA_HEADS,),
        in_specs=[pl.BlockSpec((3, 1) + dbias.shape[2:], lambda h: (0, h, 0, 0))],
        out_specs=pl.BlockSpec((1, 8, lanes), lambda h: (h, 0, 0)),
        out_shape=jax.ShapeDtypeStruct((NA_HEADS, 8, lanes), _F32),
        compiler_params=_cparams(("arbitrary",)),
    )(dbias)
    return g[:, 0].reshape(NA_HEADS, 16, GRID_W)[:, :2 * NA_KH - 1, 1:2 * NA_KW]


def _halo_specs(tm, width, col_of, T, order):
    hb = tm // 8
    last = T // 8 - 1
    if order == "ij":
        cur = pl.BlockSpec((tm, width), lambda i, j: (i, col_of(j)))
        prev = pl.BlockSpec((8, width), lambda i, j: (jnp.maximum(i * hb - 1, 0), col_of(j)))
        nxt = pl.BlockSpec((8, width), lambda i, j: (jnp.minimum((i + 1) * hb, last), col_of(j)))
    else:
        cur = pl.BlockSpec((tm, width), lambda j, i: (i, col_of(j)))
        prev = pl.BlockSpec((8, width), lambda j, i: (jnp.maximum(i * hb - 1, 0), col_of(j)))
        nxt = pl.BlockSpec((8, width), lambda j, i: (jnp.minimum((i + 1) * hb, last), col_of(j)))
    return [prev, cur, nxt]


def _extend(prev_ref, cur_ref, next_ref, i, n_i):
    prev = jnp.where(i > 0, prev_ref[...], 0.0)
    nxt = jnp.where(i < n_i - 1, next_ref[...], 0.0)
    return jnp.concatenate([prev, cur_ref[...], nxt], axis=0)


def _conv_fwd(rest, conv_w, conv_b, T):
    tm = 512
    n_i = T // tm
    n = tm + 16

    def body(p_ref, c_ref, n_ref, w_ref, b_ref, o_ref):
        i = pl.program_id(0)
        ext = _extend(p_ref, c_ref, n_ref, i, n_i)
        acc = jnp.zeros((tm, 512), _F32) + b_ref[...]
        for j in range(CONV_W):
            acc = acc + w_ref[j:j + 1, :] * pltpu.roll(ext, (2 - j) % n, 0)[8:8 + tm]
        o_ref[...] = _silu(acc)

    return pl.pallas_call(
        body, name="conv_fwd", grid=(n_i, 2),
        in_specs=_halo_specs(tm, 512, lambda j: 1 + j, T, "ij")
        + [pl.BlockSpec((8, 512), lambda i, j: (0, j)), pl.BlockSpec((1, 512), lambda i, j: (0, j))],
        out_specs=pl.BlockSpec((tm, 512), lambda i, j: (i, j)),
        out_shape=jax.ShapeDtypeStruct((T, 2 * ML_W), _F32),
        compiler_params=_cparams(("arbitrary", "arbitrary")),
    )(rest, rest, rest, conv_w, conv_b)


def _conv_bwd(rest, conv_w, conv_b, da_f, da_b, T):
    tm = 512
    n_i = T // tm
    n = tm + 16

    def body(up, uc, un, fp, fc, fn, bp, bc, bn, w_ref, b_ref, du_ref, dw_ref):
        i = pl.program_id(1)
        ext_u = _extend(up, uc, un, i, n_i)
        ext_da = _extend(fp, fc, fn, i, n_i) + _extend(bp, bc, bn, i, n_i)
        shifted = [pltpu.roll(ext_u, (2 - j) % n, 0) for j in range(CONV_W)]
        pre = jnp.zeros((n, 512), _F32) + b_ref[...]
        for j in range(CONV_W):
            pre = pre + w_ref[j:j + 1, :] * shifted[j]
        gidx = i * tm - 8 + lax.broadcasted_iota(jnp.int32, (n, 1), 0)
        dpre = jnp.where((gidx >= 0) & (gidx < T), ext_da * _dsilu(pre), 0.0)
        du = jnp.zeros((tm, 512), _F32)
        for j in range(CONV_W):
            du = du + w_ref[j:j + 1, :] * pltpu.roll(dpre, (j - 2) % n, 0)[8:8 + tm]
        du_ref[...] = du.astype(_BF16)
        dpc = dpre[8:8 + tm]
        parts = [jnp.sum(dpc * shifted[j][8:8 + tm], axis=0, keepdims=True) for j in range(CONV_W)]
        parts.append(jnp.sum(dpc, axis=0, keepdims=True))
        parts.append(jnp.zeros((2, 512), _F32))
        upd = jnp.concatenate(parts, axis=0)

        @pl.when(i == 0)
        def _():
            dw_ref[...] = upd

        @pl.when(i > 0)
        def _():
            dw_ref[...] += upd

    return pl.pallas_call(
        body, name="conv_bwd", grid=(2, n_i),
        in_specs=_halo_specs(tm, 512, lambda j: 1 + j, T, "ji")
        + _halo_specs(tm, 512, lambda j: j, T, "ji") + _halo_specs(tm, 512, lambda j: j, T, "ji")
        + [pl.BlockSpec((8, 512), lambda j, i: (0, j)), pl.BlockSpec((1, 512), lambda j, i: (0, j))],
        out_specs=(pl.BlockSpec((tm, 512), lambda j, i: (i, j)), pl.BlockSpec((8, 512), lambda j, i: (0, j))),
        out_shape=(jax.ShapeDtypeStruct((T, 2 * ML_W), _BF16), jax.ShapeDtypeStruct((8, 2 * ML_W), _F32)),
        compiler_params=_cparams(("arbitrary", "arbitrary")),
    )(rest, rest, rest, da_f, da_f, da_f, da_b, da_b, da_b, conv_w, conv_b)


def _scan_rows(x, suffix):
    L = x.shape[0]
    row = lax.broadcasted_iota(jnp.int32, (L, 1), 0)
    step = 1
    while step < L:
        if suffix:
            x = x + jnp.where(row < L - step, pltpu.roll(x, L - step, 0), 0.0)
        else:
            x = x + jnp.where(row >= step, pltpu.roll(x, step, 0), 0.0)
        step *= 2
    return x


def _ml_gates(gt, rev):
    L = gt.shape[0]
    ri = lax.broadcasted_iota(jnp.int32, (L, L), 0)
    ci = lax.broadcasted_iota(jnp.int32, (L, L), 1)
    mask = (ci >= ri) if rev else (ci <= ri)
    lf = jnp.minimum(gt, 0.0) - jnp.log(1.0 + jnp.exp(-jnp.abs(gt)))
    b = _scan_rows(lf, suffix=rev)
    return mask, b, b.T, gt.T


def _ml_head_gates(gt, gates, head, rev):
    _, b, b_t, gt_t = gates
    ci = (8 if rev else 0) + head
    cf = ci + ML_HEADS
    last = 0 if rev else gt.shape[0] - 1
    return dict(icol=gt[:, ci:ci + 1], b_col=b[:, cf:cf + 1], b_row=b_t[cf:cf + 1, :],
                i_row=gt_t[ci:ci + 1, :], bl=b[last:last + 1, cf:cf + 1])


def _ml_chunk(q, k, v, hg, mask, C, n, m, saved=None):
    icol, b_col, b_row, bl = hg["icol"], hg["b_col"], hg["b_row"], hg["bl"]
    if saved is None:
        dlog = jnp.where(mask, b_col - b_row + hg["i_row"], NEG)
        m_t = jnp.maximum(b_col + m, jnp.max(dlog, axis=1, keepdims=True))
        dm = jnp.exp(dlog - m_t)
    else:
        dm, m_t = saved[0].astype(_F32), saved[1]
    ks = k * (ML_HEAD_DIM ** -0.5)
    qb, kb, vb = q.astype(_BF16), ks.astype(_BF16), v.astype(_BF16)
    s = _nt(qb, kb) * dm
    g = jnp.exp(b_col + m - m_t)
    qc = _nt(qb, C.astype(_BF16))
    num = _nn(s.astype(_BF16), vb) + g * qc
    qn = jnp.sum(q * n, axis=1, keepdims=True)
    den = jnp.sum(s, axis=1, keepdims=True) + g * qn
    e_m = jnp.exp(-m_t)
    nrm = jnp.maximum(jnp.abs(den), e_m)
    h = num / nrm
    a_col = bl - b_col + icol
    m_new = jnp.maximum(bl + m, jnp.max(a_col, axis=0, keepdims=True))
    decay = jnp.exp(bl + m - m_new)
    w = jnp.exp(a_col - m_new)
    c_new = decay * C + _tn((w * v).astype(_BF16), kb)
    n_new = decay * n + jnp.sum(w * ks, axis=0, keepdims=True)
    aux = dict(dm=dm, m_t=m_t, ks=ks, qb=qb, kb=kb, vb=vb, s=s, g=g, qc=qc, qn=qn,
               den=den, e_m=e_m, nrm=nrm, decay=decay, w=w)
    return h, c_new, n_new, m_new, aux


def _mlstm_fwd(qk_act, rest, T, rev):
    tb = ML_CB * ML_CHUNK
    nblk = T // tb
    nc = T // ML_CHUNK
    bi = (lambda i: nblk - 1 - i) if rev else (lambda i: i)

    def body(q_ref, k_ref, v_ref, g_ref, h_ref, cs_ref, ns_ref, ms_ref, dm_ref, mt_ref, c_scr, n_scr, m_scr):
        @pl.when(pl.program_id(0) == 0)
        def _():
            c_scr[...] = jnp.zeros_like(c_scr)
            n_scr[...] = jnp.zeros_like(n_scr)
            m_scr[...] = jnp.zeros_like(m_scr)

        def step(j, carry):
            c = (ML_CB - 1 - j) if rev else j
            r0 = pl.multiple_of(c * ML_CHUNK, ML_CHUNK)
            gt = g_ref[pl.ds(r0, ML_CHUNK), :]
            gates = _ml_gates(gt, rev)
            lane = lax.broadcasted_iota(jnp.int32, (1, 128), 1)
            mt_tile = jnp.zeros((ML_CHUNK, 128), _F32)
            for hd in range(ML_HEADS):
                cols = slice(hd * ML_HEAD_DIM, (hd + 1) * ML_HEAD_DIM)
                C = c_scr[hd]
                n = n_scr[hd:hd + 1, :]
                mrow = m_scr[hd:hd + 1, :]
                cs_ref[c, hd] = C
                ns_ref[c, hd:hd + 1, :] = n
                ms_ref[c, hd:hd + 1, :] = mrow
                h, c_new, n_new, m_new, a = _ml_chunk(
                    q_ref[pl.ds(r0, ML_CHUNK), cols], k_ref[pl.ds(r0, ML_CHUNK), cols],
                    v_ref[pl.ds(r0, ML_CHUNK), cols], _ml_head_gates(gt, gates, hd, rev), gates[0],
                    C, n, mrow[:, 0:1])
                h_ref[pl.ds(r0, ML_CHUNK), cols] = h
                dm_ref[c, hd] = a["dm"].astype(_BF16)
                mt_tile = jnp.where(lane == hd, a["m_t"], mt_tile)
                c_scr[hd] = c_new
                n_scr[hd:hd + 1, :] = n_new
                m_scr[hd:hd + 1, :] = jnp.broadcast_to(m_new, (1, 128))
            mt_ref[c] = mt_tile
            return carry

        lax.fori_loop(0, ML_CB, step, 0)

    return pl.pallas_call(
        body, name="mlstm_fwd_rev" if rev else "mlstm_fwd", grid=(nblk,),
        in_specs=[pl.BlockSpec((tb, ML_W), lambda i: (bi(i), 0)),
                  pl.BlockSpec((tb, ML_W), lambda i: (bi(i), 1)),
                  pl.BlockSpec((tb, ML_W), lambda i: (bi(i), 3)),
                  pl.BlockSpec((tb, 128), lambda i: (bi(i), GATE_COL // 128))],
        out_specs=(pl.BlockSpec((tb, ML_W), lambda i: (bi(i), 0)),
                   pl.BlockSpec((ML_CB, ML_HEADS, 128, 128), lambda i: (bi(i), 0, 0, 0)),
                   pl.BlockSpec((ML_CB, ML_HEADS, 128), lambda i: (bi(i), 0, 0)),
                   pl.BlockSpec((ML_CB, ML_HEADS, 128), lambda i: (bi(i), 0, 0)),
                   pl.BlockSpec((ML_CB, ML_HEADS, ML_CHUNK, ML_CHUNK), lambda i: (bi(i), 0, 0, 0)),
                   pl.BlockSpec((ML_CB, ML_CHUNK, 128), lambda i: (bi(i), 0, 0))),
        out_shape=(jax.ShapeDtypeStruct((T, ML_W), _F32),
                   jax.ShapeDtypeStruct((nc, ML_HEADS, 128, 128), _F32),
                   jax.ShapeDtypeStruct((nc, ML_HEADS, 128), _F32),
                   jax.ShapeDtypeStruct((nc, ML_HEADS, 128), _F32),
                   jax.ShapeDtypeStruct((nc, ML_HEADS, ML_CHUNK, ML_CHUNK), _BF16),
                   jax.ShapeDtypeStruct((nc, ML_CHUNK, 128), _F32)),
        scratch_shapes=[pltpu.VMEM((ML_HEADS, 128, 128), _F32), pltpu.VMEM((8, 128), _F32),
                        pltpu.VMEM((8, 128), _F32)],
        compiler_params=_cparams(("arbitrary",)),
    )(qk_act, qk_act, rest, rest)


def _mlstm_bwd(qk_act, rest, d_h, saved, T, rev):
    tb = ML_CB * ML_CHUNK
    nblk = T // tb
    bi = (lambda i: i) if rev else (lambda i: nblk - 1 - i)

    def body(q_ref, k_ref, v_ref, g_ref, dh_ref, cs_ref, ns_ref, ms_ref, dm_ref, mt_ref,
             dqk_ref, dv_ref, dg_ref, dc_scr, dn_scr):
        @pl.when(pl.program_id(0) == 0)
        def _():
            dc_scr[...] = jnp.zeros_like(dc_scr)
            dn_scr[...] = jnp.zeros_like(dn_scr)

        def step(j, carry):
            c = j if rev else (ML_CB - 1 - j)
            r0 = pl.multiple_of(c * ML_CHUNK, ML_CHUNK)
            gt = g_ref[pl.ds(r0, ML_CHUNK), :]
            gates = _ml_gates(gt, rev)
            mask = gates[0]
            lane = lax.broadcasted_iota(jnp.int32, (1, 128), 1)
            sub = lax.broadcasted_iota(jnp.int32, (128, 1), 0)
            db_t = jnp.zeros((ML_CHUNK, 128), _F32)
            da_t = jnp.zeros((ML_CHUNK, 128), _F32)
            cs_rows = jnp.zeros((128, ML_CHUNK), _F32)
            dbl_t = jnp.zeros((1, 128), _F32)
            for hd in range(ML_HEADS):
                cols = slice(hd * ML_HEAD_DIM, (hd + 1) * ML_HEAD_DIM)
                ci = (8 if rev else 0) + hd
                cf = ci + ML_HEADS
                q = q_ref[pl.ds(r0, ML_CHUNK), cols]
                k = k_ref[pl.ds(r0, ML_CHUNK), cols]
                v = v_ref[pl.ds(r0, ML_CHUNK), cols]
                C = cs_ref[c, hd]
                n = ns_ref[c, hd:hd + 1, :]
                m = ms_ref[c, hd:hd + 1, :][:, 0:1]
                dcn = dc_scr[hd]
                dnn = dn_scr[hd:hd + 1, :]
                h, _, _, _, a = _ml_chunk(q, k, v, _ml_head_gates(gt, gates, hd, rev), mask, C, n, m,
                                          saved=(dm_ref[c, hd], mt_ref[c][:, hd:hd + 1]))
                d_hv = dh_ref[pl.ds(r0, ML_CHUNK), cols]
                g, s, w, ks = a["g"], a["s"], a["w"], a["ks"]
                qb, kb, vb = a["qb"], a["kb"], a["vb"]
                dnum = d_hv / a["nrm"]
                hdot = jnp.sum(d_hv * h, axis=1, keepdims=True)
                dden = jnp.where(jnp.abs(a["den"]) >= a["e_m"], -hdot / a["nrm"] * jnp.sign(a["den"]), 0.0)
                dnb = dnum.astype(_BF16)
                d_s = _nt(dnb, vb) + dden
                r = d_s * s
                dsqk = (d_s * a["dm"]).astype(_BF16)
                cb = C.astype(_BF16)
                dq = _nn(dsqk, kb) + g * _nn(dnb, cb) + (dden * g) * n
                dk = _tn(dsqk, qb)
                dv = _tn(s.astype(_BF16), dnb)
                dg = jnp.sum(dnum * a["qc"], axis=1, keepdims=True) + dden * a["qn"]
                db_col = jnp.sum(r, axis=1, keepdims=True) + dg * g
                cs_rows = cs_rows + jnp.where((sub == ci) | (sub == cf), jnp.sum(r, axis=0, keepdims=True), 0.0)
                dc_chunk = _tn((g * dnum).astype(_BF16), qb)
                dn_chunk = jnp.sum((dden * g) * q, axis=0, keepdims=True)
                dcb = dcn.astype(_BF16)
                vdc = _nn(vb, dcb)
                kdc = _nt(kb, dcb)
                dw = jnp.sum(vdc * ks, axis=1, keepdims=True) + jnp.sum(ks * dnn, axis=1, keepdims=True)
                dv = dv + w * kdc
                dk = dk + w * vdc + w * dnn
                da = dw * w
                ddecay = (jnp.sum(jnp.sum(dcn * C, axis=1, keepdims=True), axis=0, keepdims=True)
                          + jnp.sum(dnn * n, axis=1, keepdims=True))
                dbl = ddecay * a["decay"] + jnp.sum(da, axis=0, keepdims=True)
                db_t = db_t + jnp.where(lane == cf, db_col - da, 0.0)
                da_t = da_t + jnp.where(lane == ci, da, 0.0)
                dbl_t = dbl_t + jnp.where(lane == cf, dbl, 0.0)
                dc_scr[hd] = dc_chunk + a["decay"] * dcn
                dn_scr[hd:hd + 1, :] = dn_chunk + a["decay"] * dnn
                dqk_ref[pl.ds(r0, ML_CHUNK), cols] = dq
                dqk_ref[pl.ds(r0, ML_CHUNK), slice(ML_W + hd * 128, ML_W + (hd + 1) * 128)] = dk * (ML_HEAD_DIM ** -0.5)
                dv_ref[pl.ds(r0, ML_CHUNK), cols] = dv.astype(_BF16)
            lo = 8 if rev else 0
            is_i = (lane >= lo) & (lane < lo + ML_HEADS)
            is_f = (lane >= lo + ML_HEADS) & (lane < lo + 2 * ML_HEADS)
            cs_t = cs_rows.T
            db_all = db_t - jnp.where(is_f, cs_t, 0.0)
            dlf = _scan_rows(db_all, suffix=not rev) + dbl_t
            dg_ref[pl.ds(r0, ML_CHUNK), :] = (da_t + jnp.where(is_i, cs_t, 0.0)
                                               + jnp.where(is_f, dlf * _sigmoid(-gt), 0.0))
            return carry

        lax.fori_loop(0, ML_CB, step, 0)

    return pl.pallas_call(
        body, name="mlstm_bwd_rev" if rev else "mlstm_bwd", grid=(nblk,),
        in_specs=[pl.BlockSpec((tb, ML_W), lambda i: (bi(i), 0)),
                  pl.BlockSpec((tb, ML_W), lambda i: (bi(i), 1)),
                  pl.BlockSpec((tb, ML_W), lambda i: (bi(i), 3)),
                  pl.BlockSpec((tb, 128), lambda i: (bi(i), GATE_COL // 128)),
                  pl.BlockSpec((tb, ML_W), lambda i: (bi(i), 0)),
                  pl.BlockSpec((ML_CB, ML_HEADS, 128, 128), lambda i: (bi(i), 0, 0, 0)),
                  pl.BlockSpec((ML_CB, ML_HEADS, 128), lambda i: (bi(i), 0, 0)),
                  pl.BlockSpec((ML_CB, ML_HEADS, 128), lambda i: (bi(i), 0, 0)),
                  pl.BlockSpec((ML_CB, ML_HEADS, ML_CHUNK, ML_CHUNK), lambda i: (bi(i), 0, 0, 0)),
                  pl.BlockSpec((ML_CB, ML_CHUNK, 128), lambda i: (bi(i), 0, 0))],
        out_specs=(pl.BlockSpec((tb, 2 * ML_W), lambda i: (bi(i), 0)),
                   pl.BlockSpec((tb, ML_W), lambda i: (bi(i), 0)),
                   pl.BlockSpec((tb, 128), lambda i: (bi(i), 0))),
        out_shape=(jax.ShapeDtypeStruct((T, 2 * ML_W), _F32), jax.ShapeDtypeStruct((T, ML_W), _BF16),
                   jax.ShapeDtypeStruct((T, 128), _F32)),
        scratch_shapes=[pltpu.VMEM((ML_HEADS, 128, 128), _F32), pltpu.VMEM((8, 128), _F32)],
        compiler_params=_cparams(("arbitrary",)),
    )(qk_act, qk_act, rest, rest, d_h, *saved)


def _post(x, target, o_na, rest, h_f, h_b, gate, ml_norm_w, final_w, w_out_bf, T):
    tm = 256
    n_i = T // tm

    def body(x_ref, t_ref, o_ref, zna_ref, hf_ref, hb_ref, mo_ref, mz_ref, gate_ref, mw_ref, fw_ref, w_ref,
             dx1_ref, do_ref, dzna_ref, dh_ref, dmo_ref, dmz_ref, dwo_ref, vec_ref):
        i = pl.program_id(0)
        gate_v = gate_ref[...]
        fw = fw_ref[...]
        zna = zna_ref[...]
        o = o_ref[...]
        sig_zna = _sigmoid(zna)
        silu_zna = zna * sig_zna
        na_out = o * silu_zna
        hsum = hf_ref[...] + hb_ref[...]
        sg = _sigmoid(mo_ref[...])
        hm = hsum * sg
        mz = mz_ref[...]
        sig_mz = _sigmoid(mz)
        smz = mz * sig_mz
        dsilu_mz = sig_mz * (1.0 + mz * (1.0 - sig_mz))
        hn_l, rstd_l, ml_l = [], [], []
        for hd in range(ML_HEADS):
            cols = slice(hd * 128, (hd + 1) * 128)
            hh = hm[:, cols]
            mu = jnp.mean(hh, axis=-1, keepdims=True)
            var = jnp.mean(jnp.square(hh - mu), axis=-1, keepdims=True)
            rstd = lax.rsqrt(var + EPS)
            hn = (hh - mu) * rstd
            hn_l.append(hn)
            rstd_l.append(rstd)
            ml_l.append(hn * mw_ref[:, cols] * smz[:, cols])
        mix = jnp.concatenate([na_out] + ml_l, axis=1).astype(_BF16)
        y = _nn(mix, w_ref[...])
        x1 = x_ref[...] + gate_v * y
        r = lax.rsqrt(jnp.mean(x1 * x1, axis=-1, keepdims=True) + EPS)
        xhat = x1 * r
        out = xhat * fw
        err = out - t_ref[...]
        loss = 0.5 * jnp.sum(jnp.sum(err * err, axis=1, keepdims=True), axis=0, keepdims=True) / D_MODEL
        dout = err * (1.0 / D_MODEL)
        dfw = jnp.sum(dout * xhat, axis=0, keepdims=True)
        dxhat = dout * fw
        dx1 = r * (dxhat - xhat * jnp.mean(dxhat * xhat, axis=-1, keepdims=True))
        dx1_ref[...] = dx1
        dgate = jnp.sum(dx1 * y, axis=0, keepdims=True)
        dy = (dx1 * gate_v).astype(_BF16)
        dmix = _nt(dy, w_ref[...])
        dwo = _tn(mix, dy)
        dna = dmix[:, :NA_W]
        do_ref[...] = (dna * silu_zna).astype(_BF16)
        dzna_ref[...] = (dna * o * (sig_zna * (1.0 + zna * (1.0 - sig_zna)))).astype(_BF16)
        dmw_l = []
        for hd in range(ML_HEADS):
            cols = slice(hd * 128, (hd + 1) * 128)
            dml = dmix[:, NA_W + hd * 128:NA_W + (hd + 1) * 128]
            hn = hn_l[hd]
            mwv = mw_ref[:, cols]
            dmz_ref[:, cols] = (dml * hn * mwv * dsilu_mz[:, cols]).astype(_BF16)
            dhn = dml * mwv * smz[:, cols]
            dmw_l.append(jnp.sum(dml * hn * smz[:, cols], axis=0, keepdims=True))
            dhm = rstd_l[hd] * (dhn - jnp.mean(dhn, axis=-1, keepdims=True)
                                - hn * jnp.mean(dhn * hn, axis=-1, keepdims=True))
            sgc = sg[:, cols]
            dh_ref[:, cols] = dhm * sgc
            dmo_ref[:, cols] = (dhm * hsum[:, cols] * sgc * (1.0 - sgc)).astype(_BF16)
        dmw = jnp.concatenate(dmw_l + [jnp.zeros((1, D_MODEL - ML_W), _F32)], axis=1)
        lane = lax.broadcasted_iota(jnp.int32, (1, D_MODEL), 1)
        vec = jnp.concatenate([dfw, dgate, dmw, jnp.where(lane == 0, loss, 0.0),
                               jnp.zeros((4, D_MODEL), _F32)], axis=0)

        @pl.when(i == 0)
        def _():
            dwo_ref[...] = dwo
            vec_ref[...] = vec

        @pl.when(i > 0)
        def _():
            dwo_ref[...] += dwo
            vec_ref[...] += vec

    tok = lambda w, j: pl.BlockSpec((tm, w), lambda i: (i, j))
    tok3 = pl.BlockSpec((None, tm, D_MODEL), lambda i: (0, i, 0))
    row = lambda w: pl.BlockSpec((1, w), lambda i: (0, 0))
    f32 = lambda w: jax.ShapeDtypeStruct((T, w), _F32)
    bf16 = lambda w: jax.ShapeDtypeStruct((T, w), _BF16)
    return pl.pallas_call(
        body, name="post", grid=(n_i,),
        in_specs=[tok3, tok3, tok(NA_W, 0), tok(NA_W, 0), tok(ML_W, 0), tok(ML_W, 0),
                  tok(ML_W, 4), tok(ML_W, 5), row(D_MODEL), row(ML_W), row(D_MODEL),
                  pl.BlockSpec((D_MODEL, D_MODEL), lambda i: (0, 0))],
        out_specs=(tok(D_MODEL, 0), tok(NA_W, 0), tok(NA_W, 0), tok(ML_W, 0), tok(ML_W, 0), tok(ML_W, 0),
                   pl.BlockSpec((D_MODEL, D_MODEL), lambda i: (0, 0)),
                   pl.BlockSpec((8, D_MODEL), lambda i: (0, 0))),
        out_shape=(f32(D_MODEL), bf16(NA_W), bf16(NA_W), f32(ML_W), bf16(ML_W),
                   bf16(ML_W), jax.ShapeDtypeStruct((D_MODEL, D_MODEL), _F32),
                   jax.ShapeDtypeStruct((8, D_MODEL), _F32)),
        compiler_params=_cparams(("arbitrary",)),
    )(x, target, o_na, rest, h_f, h_b, rest, rest, gate, ml_norm_w, final_w, w_out_bf)


def _section_specs(sections, tm):
    specs, args = [], []
    for _, width, parts in sections:
        for arr, cb in parts:
            specs.append(pl.BlockSpec((tm, width), functools.partial(lambda i, cb: (i, cb), cb=cb)))
            args.append(arr)
    return specs, args


def _section_values(sections, refs, dtype):
    vals, at = [], 0
    for _, _, parts in sections:
        v = refs[at][...]
        for r in refs[at + 1:at + len(parts)]:
            v = v.astype(_F32) + r[...].astype(_F32)
        at += len(parts)
        vals.append(v.astype(dtype))
    return vals


def _inproj_bwd_x(x, dx1, scale1p, norm_w, w_in_bf, sections, T):
    tm = 512
    sspecs, sargs = _section_specs(sections, tm)
    ns = len(sargs)

    def body(*refs):
        x_ref, dx1_ref, sc_ref, nw_ref, w_ref = refs[:5]
        srefs = refs[5:5 + ns]
        gx_ref, vec_ref = refs[5 + ns:]
        i = pl.program_id(0)
        vals = _section_values(sections, srefs, _BF16)
        dh = jnp.zeros((tm, D_MODEL), _F32)
        for (c0, width, _), val in zip(sections, vals):
            dh = dh + _nt(val, w_ref[:, c0:c0 + width])
        xv = x_ref[...]
        r = lax.rsqrt(jnp.mean(xv * xv, axis=-1, keepdims=True) + EPS)
        xhat = xv * r
        nw = nw_ref[...]
        dshift = jnp.sum(dh, axis=0, keepdims=True)
        dscale = jnp.sum(dh * xhat * nw, axis=0, keepdims=True)
        dhpre = dh * sc_ref[...]
        dnw = jnp.sum(dhpre * xhat, axis=0, keepdims=True)
        dxhat = dhpre * nw
        gx_ref[...] = dx1_ref[...] + r * (dxhat - xhat * jnp.mean(dxhat * xhat, axis=-1, keepdims=True))
        vec = jnp.concatenate([dshift, dscale, dnw, jnp.zeros((5, D_MODEL), _F32)], axis=0)

        @pl.when(i == 0)
        def _():
            vec_ref[...] = vec

        @pl.when(i > 0)
        def _():
            vec_ref[...] += vec

    row = pl.BlockSpec((1, D_MODEL), lambda i: (0, 0))
    tok = pl.BlockSpec((tm, D_MODEL), lambda i: (i, 0))
    tok3 = pl.BlockSpec((None, tm, D_MODEL), lambda i: (0, i, 0))
    return pl.pallas_call(
        body, name="inproj_bwd_x", grid=(T // tm,),
        in_specs=[tok3, tok, row, row,
                  pl.BlockSpec((D_MODEL, IN_PAD), lambda i: (0, 0), pipeline_mode=pl.Buffered(1))] + sspecs,
        out_specs=(tok3, pl.BlockSpec((8, D_MODEL), lambda i: (0, 0))),
        out_shape=(jax.ShapeDtypeStruct((1, T, D_MODEL), _F32), jax.ShapeDtypeStruct((8, D_MODEL), _F32)),
        compiler_params=_cparams(("arbitrary",)),
    )(x, dx1, scale1p, norm_w, w_in_bf, *sargs)


def _inproj_bwd_w(h_t, sections, T):
    tm = 1024
    n_i = T // tm
    sspecs, sargs = _section_specs(sections, tm)
    ns = len(sargs)

    def body(*refs):
        h_ref = refs[0]
        srefs = refs[1:1 + ns]
        dw_ref, db_ref, acc, sem = refs[1 + ns:]
        i = pl.program_id(0)

        @pl.when(i == 0)
        def _():
            acc[...] = jnp.zeros_like(acc)
            db_ref[...] = jnp.zeros_like(db_ref)

        hv = h_ref[...]
        for (c0, width, _), v in zip(sections, _section_values(sections, srefs, _F32)):
            acc[:, c0:c0 + width] += _nn(hv, v.astype(_BF16))
            db_ref[0:1, c0:c0 + width] += jnp.sum(v, axis=0, keepdims=True)

        @pl.when(i == n_i - 1)
        def _():
            cp = pltpu.make_async_copy(acc, dw_ref, sem)
            cp.start()
            cp.wait()

    return pl.pallas_call(
        body, name="inproj_bwd_w", grid=(n_i,),
        in_specs=[pl.BlockSpec((D_MODEL, tm), lambda i: (0, i))] + sspecs,
        out_specs=(pl.BlockSpec(memory_space=pl.ANY), pl.BlockSpec((8, IN_PAD), lambda i: (0, 0))),
        out_shape=(jax.ShapeDtypeStruct((D_MODEL, IN_PAD), _F32), jax.ShapeDtypeStruct((8, IN_PAD), _F32)),
        scratch_shapes=[pltpu.VMEM((D_MODEL, IN_PAD), _F32), pltpu.SemaphoreType.DMA],
        compiler_params=_cparams(("arbitrary",)),
    )(h_t, *sargs)


def _adamw_math(w, g, m, v):
    m = ADAM_B1 * m + (1.0 - ADAM_B1) * g
    v = ADAM_B2 * v + (1.0 - ADAM_B2) * jnp.square(g)
    m_hat = m / (1.0 - ADAM_B1 ** ADAM_STEP)
    v_hat = v / (1.0 - ADAM_B2 ** ADAM_STEP)
    delta = -ADAM_LR * (m_hat / (jnp.sqrt(v_hat) + ADAM_EPS) + ADAM_WD * w)
    return delta, m, v


def _adamw_slots(w, m, v, slots, tr, name, own=None):
    R, C = w.shape
    extra = [] if own is None else [own]

    def body(w_ref, m_ref, v_ref, s_ref, *refs):
        g_ref, d_ref, nm_ref, nv_ref = refs[len(extra):]
        g = s_ref[0].astype(_F32)
        for k in range(1, N_DEV):
            g = g + s_ref[k].astype(_F32)
        if extra:
            g = g + refs[0][...].astype(_F32)
        g_ref[...] = g
        d_ref[...], nm_ref[...], nv_ref[...] = _adamw_math(w_ref[...], g, m_ref[...], v_ref[...])

    blk = pl.BlockSpec((tr, C), lambda i: (i, 0))
    return pl.pallas_call(
        body, name=name, grid=(R // tr,),
        in_specs=[blk, blk, blk, pl.BlockSpec((N_DEV, tr, C), lambda i: (0, i, 0))] + [blk] * len(extra),
        out_specs=(blk, blk, blk, blk),
        out_shape=tuple(jax.ShapeDtypeStruct((R, C), _F32) for _ in range(4)),
        compiler_params=_cparams(("arbitrary",)),
    )(w, m, v, slots, *extra)


def _w_ada_update(c_all, dmod_my, w, m, v):
    def body(c_ref, d_ref, w_ref, m_ref, v_ref, g_ref, dl_ref, nm_ref, nv_ref):
        g = lax.dot_general(_silu(c_ref[...]), d_ref[...], (((0,), (0,)), ((), ())),
                            precision=_HI, preferred_element_type=_F32)
        g_ref[...] = g
        dl_ref[...], nm_ref[...], nv_ref[...] = _adamw_math(w_ref[...], g, m_ref[...], v_ref[...])

    return pl.pallas_call(
        body, name="w_ada_update",
        out_shape=tuple(jax.ShapeDtypeStruct(w.shape, _F32) for _ in range(4)),
        compiler_params=_cparams(),
    )(c_all, dmod_my, w, m, v)


_PACK = (("b_ada", 3072, 3072), ("norm_w", 1024, 1024), ("b_in", IN_W, IN_PAD), ("conv_w", 5120, 5120),
         ("conv_b", 1024, 1024), ("rpb", 3720, 3840), ("ml_norm_w", 512, 512), ("final_norm_w", 1024, 1024),
         ("loss", 1, 128))
_PACK_OFF = {}
_off = 0
for _name, _len, _pad in _PACK:
    _PACK_OFF[_name] = (_off, _len)
    _off += _pad
_PACK_LEN = _off


def _pack(parts):
    cols = []
    for name, length, pad in _PACK:
        vec = parts[name].reshape(-1).astype(_F32)
        cols.append(jnp.pad(vec, (0, pad - length)))
    return jnp.concatenate(cols).reshape(1, _PACK_LEN)


def _unpack(vec, name, shape):
    off, length = _PACK_OFF[name]
    return vec.reshape(-1)[off:off + length].reshape(shape)


def kernel(x, c, w_ada, b_ada, norm_w, w_in, b_in, conv_w, conv_b, rpb, ml_norm_w, w_out, final_norm_w, loss_target, m_w_ada, m_b_ada, m_norm_w, m_w_in, m_b_in, m_conv_w, m_conv_b, m_rpb, m_ml_norm_w, m_w_out, m_final_norm_w, v_w_ada, v_b_ada, v_norm_w, v_w_in, v_b_in, v_conv_w, v_conv_b, v_rpb, v_ml_norm_w, v_w_out, v_final_norm_w):
    T = x.shape[1]
    rows = T // GRID_W
    me = 4 * lax.axis_index("x") + 2 * lax.axis_index("y") + lax.axis_index("c")
    n_in = w_in.shape[2]
    n_ada = w_ada.shape[2]
    n_cw = conv_w.shape[2]
    n_wo = w_out.shape[1]

    w_in_my, w_out_my = w_in[0].astype(_BF16), w_out[0].astype(_BF16)
    start_in = _scatter_start([w_in_my], "w_in_start", scatter=False)
    start_out = _scatter_start([w_out_my], "w_out_start", scatter=False)
    tokens = start_in[-1][0:1, 0:1] + start_out[-1][0:1, 0:1]
    g_conv_w, g_c = _exchange([conv_w[0], c + tokens], [False] * 2, "gather_small")
    b_in_pad = jnp.pad(b_in, ((0, 0), (0, IN_PAD - IN_W)))
    conv_w_full = jnp.pad(g_conv_w.transpose(1, 0, 2).reshape(CONV_W, N_DEV * n_cw), ((0, 3), (0, 0)))
    c_all = g_c.reshape(N_DEV, D_MODEL)

    b_ada_my = lax.dynamic_slice(b_ada, (0, me * n_ada), (1, n_ada))
    (mod_slots,) = _exchange([_mod_part(c_all, w_ada[0], b_ada_my)], [False], "gather_mod")
    mod = lax.dynamic_index_in_dim(mod_slots, me, axis=1, keepdims=False).reshape(1, 3 * D_MODEL)
    shift, scale, gate = mod[:, :D_MODEL], mod[:, D_MODEL:2 * D_MODEL], mod[:, 2 * D_MODEL:]
    scale1p = 1.0 + scale
    bias = _na_bias_tables(rpb[0], rows)

    def gathered(started, after, name):
        (own,), (land,) = _scatter_wait(started, after, name, scatter=False)
        return lax.dynamic_update_slice(land, own[None], (me,) + (0,) * own.ndim)

    g_w_in = gathered(start_in, bias[0, 0, :8, :128] + scale1p[:, :128], "w_in_wait")
    w_in_full = g_w_in.transpose(1, 0, 2).reshape(D_MODEL, N_DEV * n_in)
    w_in_bf = jnp.pad(w_in_full, ((0, 0), (0, IN_PAD - IN_W)))

    qkv, rest, h_bf = _inproj_fwd(x, scale1p, shift, norm_w, w_in_bf, b_in_pad)
    o_na, lse = _na_fwd(qkv, bias, T)
    qk_act = _conv_fwd(rest, conv_w_full, conv_b, T)
    h_f, *saved_f = _mlstm_fwd(qk_act, rest, T, False)
    h_b, *saved_b = _mlstm_fwd(qk_act, rest, T, True)

    w_out_bf = gathered(start_out, saved_b[2], "w_out_wait").reshape(N_DEV * n_wo, D_MODEL)
    dx1, d_o, dz_na, d_h, d_mo, d_mz, dwo, pvec = _post(
        x, loss_target, o_na, rest, h_f, h_b, gate, ml_norm_w, final_norm_w.reshape(1, D_MODEL), w_out_bf, T)

    dq_na, dk_na, dv_na, dbias = _na_bwd(qkv, bias, o_na, d_o, lse, T)
    d_rpb = _rpb_grad(dbias, rows)
    dqk_f, dv_f, dg_f = _mlstm_bwd(qk_act, rest, d_h, saved_f, T, False)
    dqk_b, dv_b, dg_b = _mlstm_bwd(qk_act, rest, d_h, saved_b, T, True)
    d_u, dconv = _conv_bwd(rest, conv_w_full, conv_b, dqk_f, dqk_b, T)

    sections = [(0, 512, [(dq_na, 0)]), (512, 512, [(dk_na, 0)]), (1024, 512, [(dv_na, 0)]),
                (1536, 512, [(dz_na, 0)]), (2048, 512, [(d_u, 0)]), (2560, 512, [(d_u, 1)]),
                (3072, 512, [(dv_f, 0), (dv_b, 0)]), (3584, 512, [(d_mo, 0)]), (4096, 512, [(d_mz, 0)]),
                (4608, 128, [(dg_f, 0), (dg_b, 0)])]
    dw_pad, db_pad = _inproj_bwd_w(h_bf, sections, T)
    db_in = db_pad[0, :IN_W]

    dw_blocks = dw_pad[:, :IN_W].astype(_BF16).reshape(D_MODEL, N_DEV, n_in).transpose(1, 0, 2)
    dwo_blocks = dwo.astype(_BF16).reshape(N_DEV, n_wo, D_MODEL)
    started = _scatter_start([dw_blocks, dwo_blocks], "grads_start")
    grad_x, xvec = _inproj_bwd_x(x, dx1, scale1p + started[-1][0:1, 0:1], norm_w, w_in_bf, sections, T)
    (dw_blocks, dwo_blocks), (s_w_in, s_w_out) = _scatter_wait(started, xvec, "grads_wait")

    small = _pack({
        "b_ada": jnp.concatenate([xvec[0], xvec[1], pvec[1]]),
        "norm_w": xvec[2], "b_in": db_in, "conv_w": dconv[:CONV_W], "conv_b": dconv[CONV_W],
        "rpb": d_rpb, "ml_norm_w": pvec[2, :ML_W], "final_norm_w": pvec[0], "loss": pvec[3, :1]})
    (s_small,) = _exchange([small], [False], "exchange_small")

    own = lambda blocks: lax.dynamic_index_in_dim(blocks, me, axis=0, keepdims=False)
    g_w_in_s, d_w_in, nm_w_in, nv_w_in = _adamw_slots(
        w_in[0], m_w_in[0], v_w_in[0], s_w_in, 128, "adamw_w_in", own=own(dw_blocks))
    g_w_out_s, d_w_out, nm_w_out, nv_w_out = _adamw_slots(
        w_out[0], m_w_out[0], v_w_out[0], s_w_out, n_wo, "adamw_w_out", own=own(dwo_blocks))
    dmod_all = s_small[:, 0, :3 * D_MODEL]
    dmod_my = lax.dynamic_slice(dmod_all, (0, me * n_ada), (N_DEV, n_ada))
    g_w_ada, d_w_ada, nm_w_ada, nv_w_ada = _w_ada_update(c_all, dmod_my, w_ada[0], m_w_ada[0], v_w_ada[0])

    def embed(shard):
        return lax.dynamic_update_slice(jnp.zeros((CONV_W, N_DEV * n_cw), _F32), shard[0], (0, me * n_cw))

    zero1 = jnp.zeros((1,), _F32)
    packed = lambda b_a, n_w, b_i, c_w, c_b, rp, mn, fn: _pack({
        "b_ada": b_a, "norm_w": n_w, "b_in": b_i, "conv_w": embed(c_w), "conv_b": c_b, "rpb": rp,
        "ml_norm_w": mn, "final_norm_w": fn, "loss": zero1})
    pw = packed(b_ada, norm_w, b_in, conv_w, conv_b, rpb, ml_norm_w, final_norm_w)
    pm = packed(m_b_ada, m_norm_w, m_b_in, m_conv_w, m_conv_b, m_rpb, m_ml_norm_w, m_final_norm_w)
    pv = packed(v_b_ada, v_norm_w, v_b_in, v_conv_w, v_conv_b, v_rpb, v_ml_norm_w, v_final_norm_w)
    sg, sd, sm, sv = _adamw_slots(pw, pm, pv, s_small, 1, "adamw_small")

    def small_outs(vec):
        cw = lax.dynamic_slice(_unpack(vec, "conv_w", (CONV_W, N_DEV * n_cw)), (0, me * n_cw), (CONV_W, n_cw))
        return dict(b_ada=_unpack(vec, "b_ada", b_ada.shape), norm_w=_unpack(vec, "norm_w", norm_w.shape),
                    b_in=_unpack(vec, "b_in", b_in.shape), conv_w=cw[None],
                    conv_b=_unpack(vec, "conv_b", conv_b.shape), rpb=_unpack(vec, "rpb", rpb.shape),
                    ml_norm_w=_unpack(vec, "ml_norm_w", ml_norm_w.shape),
                    final_norm_w=_unpack(vec, "final_norm_w", final_norm_w.shape))

    loss = _unpack(sg, "loss", ())
    order = ("w_ada", "b_ada", "norm_w", "w_in", "b_in", "conv_w", "conv_b", "rpb", "ml_norm_w", "w_out",
             "final_norm_w")
    outs = []
    for vec, big in ((sg, (g_w_ada, g_w_in_s, g_w_out_s)), (sd, (d_w_ada, d_w_in, d_w_out)),
                     (sm, (nm_w_ada, nm_w_in, nm_w_out)), (sv, (nv_w_ada, nv_w_in, nv_w_out))):
        group = small_outs(vec)
        group.update(w_ada=big[0][None], w_in=big[1][None], w_out=big[2][None])
        outs.extend(group[name] for name in order)
    return (loss, grad_x, *outs)
```

```python
import functools

import numpy as np
import jax
import jax.numpy as jnp
from jax import lax
from jax.experimental import pallas as pl
from jax.experimental.pallas import tpu as pltpu

N_DEV = 8
D_MODEL = 1024
GRID_W = 64
NA_HEADS = 8
NA_HEAD_DIM = 64
NA_KH = 8
NA_KW = 16
NA_W = 512
ML_HEADS = 4
ML_HEAD_DIM = 128
ML_W = 512
ML_CHUNK = 512
CONV_W = 5
EPS = 1e-6
IN_W = 4624
IN_PAD = 4736
REST_W = IN_PAD - 3 * NA_W
GATE_COL = 3072
NEG = -1e30
NA_RB = 4
NA_WIN = 12
NA_SUB = 2
ML_CB = 1
ADAM_LR = 0.001
ADAM_B1 = 0.9
ADAM_B2 = 0.999
ADAM_EPS = 1e-08
ADAM_WD = 0.01
ADAM_STEP = 10
VMEM_LIMIT = 56 * 1024 * 1024

_F32 = jnp.float32
_BF16 = jnp.bfloat16
_HI = lax.Precision.HIGHEST


def _cparams(sem=None):
    return pltpu.CompilerParams(dimension_semantics=sem, vmem_limit_bytes=VMEM_LIMIT)


def _nt(a, b):
    return lax.dot_general(a, b, (((1,), (1,)), ((), ())), preferred_element_type=_F32)


def _tn(a, b):
    return lax.dot_general(a, b, (((0,), (0,)), ((), ())), preferred_element_type=_F32)


def _nn(a, b):
    return jnp.dot(a, b, preferred_element_type=_F32)


def _sigmoid(x):
    return 1.0 / (1.0 + jnp.exp(-x))


def _silu(x):
    return x * _sigmoid(x)


def _dsilu(x):
    s = _sigmoid(x)
    return s * (1.0 + x * (1.0 - s))


def _exchange(arrs, scatter, name):
    n = len(arrs)
    out_shape = []
    for a, sc in zip(arrs, scatter):
        blk = a.shape[1:] if sc else a.shape
        out_shape.append(jax.ShapeDtypeStruct((N_DEV,) + tuple(blk), a.dtype))

    def body(*refs):
        ins = refs[:n]
        outs = refs[n:2 * n]
        send_sems, recv_sems, local_sems = refs[2 * n:]
        x, y, c = lax.axis_index("x"), lax.axis_index("y"), lax.axis_index("c")
        me = 4 * x + 2 * y + c
        local, sends, recvs = [], [], []
        for a in range(n):
            own = ins[a].at[me] if scatter[a] else ins[a]
            cp = pltpu.make_async_copy(own, outs[a].at[me], local_sems.at[a])
            cp.start()
            local.append(cp)
            for k in range(1, N_DEV):
                px = 1 - x if k & 4 else x
                py = 1 - y if k & 2 else y
                pc = 1 - c if k & 1 else c
                p = 4 * px + 2 * py + pc
                src = ins[a].at[p] if scatter[a] else ins[a]
                snd = pltpu.make_async_remote_copy(
                    src_ref=src, dst_ref=outs[a].at[me],
                    send_sem=send_sems.at[a, k - 1], recv_sem=recv_sems.at[a, k - 1],
                    device_id=(px, py, pc), device_id_type=pl.DeviceIdType.MESH)
                snd.start()
                sends.append(snd)
                rcv = pltpu.make_async_remote_copy(
                    src_ref=src, dst_ref=outs[a].at[p],
                    send_sem=send_sems.at[a, k - 1], recv_sem=recv_sems.at[a, k - 1],
                    device_id=(px, py, pc), device_id_type=pl.DeviceIdType.MESH)
                recvs.append(rcv)
        for rcv in recvs:
            rcv.wait_recv()
        for snd in sends:
            snd.wait_send()
        for cp in local:
            cp.wait()

    any_spec = pl.BlockSpec(memory_space=pl.ANY)
    res = pl.pallas_call(
        body, name=name, out_shape=tuple(out_shape),
        in_specs=[any_spec] * n, out_specs=tuple([any_spec] * n),
        scratch_shapes=[pltpu.SemaphoreType.DMA((n, N_DEV - 1)),
                        pltpu.SemaphoreType.DMA((n, N_DEV - 1)),
                        pltpu.SemaphoreType.DMA((n,))],
    )(*arrs)
    return list(res)


def _peer(k):
    x, y, c = lax.axis_index("x"), lax.axis_index("y"), lax.axis_index("c")
    px = 1 - x if k & 4 else x
    py = 1 - y if k & 2 else y
    pc = 1 - c if k & 1 else c
    return (px, py, pc), 4 * px + 2 * py + pc, 4 * x + 2 * y + c


_ALL_PEERS = tuple(range(1, N_DEV))
_CHIP_PEERS = (2, 4, 6)


def _scatter_copy(srcs, lands, send_sems, recv_sems, a, k, receive, scatter, ks=_ALL_PEERS):
    dev, p, me = _peer(k)
    at = a * len(ks) + ks.index(k)
    return pltpu.make_async_remote_copy(
        src_ref=srcs[a].at[p] if scatter else srcs[a], dst_ref=lands[a].at[p if receive else me],
        send_sem=send_sems[at], recv_sem=recv_sems[at],
        device_id=dev, device_id_type=pl.DeviceIdType.MESH)


def _scatter_start(arrs, name, scatter=True, ks=_ALL_PEERS):
    n = len(arrs)
    ns = n * len(ks)
    hbm = pl.BlockSpec(memory_space=pltpu.HBM)
    sem = pl.BlockSpec(memory_space=pltpu.SEMAPHORE)

    def body(*refs):
        srcs, lands = refs[:n], refs[n:2 * n]
        send_sems, recv_sems = refs[2 * n:2 * n + ns], refs[2 * n + ns:2 * n + 2 * ns]
        token = refs[-1]
        for a in range(n):
            for k in ks:
                _scatter_copy(srcs, lands, send_sems, recv_sems, a, k, False, scatter, ks).start()
        token[...] = jnp.zeros_like(token)

    land_shapes = [a.shape if scatter else (N_DEV,) + a.shape for a in arrs]
    buffers = [pltpu.HBM(a.shape, a.dtype) for a in arrs]
    land_buffers = [pltpu.HBM(s, a.dtype) for s, a in zip(land_shapes, arrs)]
    sems = [pltpu.SemaphoreType.DMA(()) for _ in range(2 * ns)]
    res = pl.pallas_call(
        body, name=name,
        out_shape=(*sems, *buffers, *land_buffers, jax.ShapeDtypeStruct((8, 128), _F32)),
        in_specs=[hbm] * (2 * n),
        out_specs=(*([sem] * (2 * ns)), *([hbm] * (2 * n)), pl.BlockSpec(memory_space=pltpu.VMEM)),
        input_output_aliases={i: 2 * ns + i for i in range(2 * n)},
        compiler_params=pltpu.CompilerParams(has_side_effects=pltpu.SideEffectType.DATAFLOW_SIDE_EFFECTING),
    )(*[pltpu.with_memory_space_constraint(a, pltpu.HBM) for a in arrs],
      *[pltpu.with_memory_space_constraint(jnp.zeros(s, a.dtype), pltpu.HBM) for s, a in zip(land_shapes, arrs)])
    res = list(res)
    return (res[:ns], res[ns:2 * ns], res[2 * ns:2 * ns + n], res[2 * ns + n:2 * ns + 2 * n], res[-1])


def _scatter_wait(started, after, name, scatter=True, ks=_ALL_PEERS):
    send_sems, recv_sems, srcs, lands, _ = started
    n = len(srcs)
    ns = len(send_sems)
    hbm = pl.BlockSpec(memory_space=pltpu.HBM)
    sem = pl.BlockSpec(memory_space=pltpu.SEMAPHORE)

    def body(*refs):
        src_refs, land_refs = refs[:n], refs[n:2 * n]
        s_sems, r_sems = refs[2 * n:2 * n + ns], refs[2 * n + ns:2 * n + 2 * ns]
        for a in range(n):
            for k in ks:
                _scatter_copy(src_refs, land_refs, s_sems, r_sems, a, k, False, scatter, ks).wait_send()
                _scatter_copy(src_refs, land_refs, s_sems, r_sems, a, k, True, scatter, ks).wait_recv()

    buffers = [pltpu.HBM(a.shape, a.dtype) for a in list(srcs) + list(lands)]
    res = pl.pallas_call(
        body, name=name, out_shape=tuple(buffers),
        in_specs=[hbm] * (2 * n) + [sem] * (2 * ns) + [pl.BlockSpec(memory_space=pl.ANY)],
        out_specs=tuple([hbm] * (2 * n)),
        input_output_aliases={i: i for i in range(2 * n)},
        compiler_params=pltpu.CompilerParams(has_side_effects=pltpu.SideEffectType.DATAFLOW_SIDE_EFFECTING),
    )(*srcs, *lands, *send_sems, *recv_sems, after)
    return list(res[:n]), list(res[n:])


def _relay_copy(land, send_sems, recv_sems, j, receive):
    k = _CHIP_PEERS[j]
    sibling, _, _ = _peer(1)
    _, slot, _ = _peer(k | 1 if receive else k)
    return pltpu.make_async_remote_copy(
        src_ref=land.at[slot], dst_ref=land.at[slot], send_sem=send_sems[j], recv_sem=recv_sems[j],
        device_id=sibling, device_id_type=pl.DeviceIdType.MESH)


def _relay_start(land, name):
    ns = len(_CHIP_PEERS)
    hbm = pl.BlockSpec(memory_space=pltpu.HBM)
    sem = pl.BlockSpec(memory_space=pltpu.SEMAPHORE)

    def body(*refs):
        land_ref = refs[0]
        send_sems, recv_sems = refs[1:1 + ns], refs[1 + ns:1 + 2 * ns]
        for j in range(ns):
            _relay_copy(land_ref, send_sems, recv_sems, j, False).start()
        refs[-1][...] = jnp.zeros_like(refs[-1])

    sems = [pltpu.SemaphoreType.DMA(()) for _ in range(2 * ns)]
    res = pl.pallas_call(
        body, name=name,
        out_shape=(*sems, pltpu.HBM(land.shape, land.dtype), jax.ShapeDtypeStruct((8, 128), _F32)),
        in_specs=[hbm],
        out_specs=(*([sem] * (2 * ns)), hbm, pl.BlockSpec(memory_space=pltpu.VMEM)),
        input_output_aliases={0: 2 * ns},
        compiler_params=pltpu.CompilerParams(has_side_effects=pltpu.SideEffectType.DATAFLOW_SIDE_EFFECTING),
    )(land)
    res = list(res)
    return res[:ns], res[ns:2 * ns], res[2 * ns], res[-1]


def _relay_wait(started, name):
    send_sems, recv_sems, land, token = started
    ns = len(send_sems)
    hbm = pl.BlockSpec(memory_space=pltpu.HBM)
    sem = pl.BlockSpec(memory_space=pltpu.SEMAPHORE)

    def body(*refs):
        land_ref = refs[0]
        s_sems, r_sems = refs[1:1 + ns], refs[1 + ns:1 + 2 * ns]
        for j in range(ns):
            _relay_copy(land_ref, s_sems, r_sems, j, False).wait_send()
            _relay_copy(land_ref, s_sems, r_sems, j, True).wait_recv()

    return pl.pallas_call(
        body, name=name, out_shape=pltpu.HBM(land.shape, land.dtype),
        in_specs=[hbm] + [sem] * (2 * ns) + [pl.BlockSpec(memory_space=pl.ANY)],
        out_specs=hbm, input_output_aliases={0: 0},
        compiler_params=pltpu.CompilerParams(has_side_effects=pltpu.SideEffectType.DATAFLOW_SIDE_EFFECTING),
    )(land, *send_sems, *recv_sems, token)


def _mod_part(c_all, w_ada, b_my):
    def body(c_ref, w_ref, b_ref, o_ref):
        o_ref[...] = jnp.dot(_silu(c_ref[...]), w_ref[...], precision=_HI,
                             preferred_element_type=_F32) + b_ref[...]

    return pl.pallas_call(
        body, name="mod_part",
        out_shape=jax.ShapeDtypeStruct((N_DEV, w_ada.shape[1]), _F32),
        compiler_params=_cparams(),
    )(c_all, w_ada, b_my)


def _inproj_fwd(x, scale1p, shift, norm_w, w_in_bf, b_in_pad):
    T = x.shape[1]
    tm = 512
    n_q = 3 * NA_W

    def body(x_ref, sc_ref, sh_ref, nw_ref, w_ref, b_ref, qkv_ref, rest_ref, h_ref):
        xv = x_ref[...]
        r = lax.rsqrt(jnp.mean(xv * xv, axis=-1, keepdims=True) + EPS)
        h = xv * r * nw_ref[...] * sc_ref[...] + sh_ref[...]
        hb = h.astype(_BF16)
        h_ref[...] = h.T.astype(_BF16)
        for n0 in range(0, IN_PAD, 512):
            wd = min(512, IN_PAD - n0)
            acc = _nn(hb, w_ref[:, n0:n0 + wd]) + b_ref[:, n0:n0 + wd]
            if n0 == 0:
                acc = acc * (NA_HEAD_DIM ** -0.5)
            if n0 < n_q:
                qkv_ref[:, n0:n0 + wd] = acc.astype(_BF16)
            else:
                rest_ref[:, n0 - n_q:n0 - n_q + wd] = acc

    row = lambda w: pl.BlockSpec((1, w), lambda i: (0, 0))
    return pl.pallas_call(
        body, name="inproj_fwd", grid=(T // tm,),
        in_specs=[pl.BlockSpec((None, tm, D_MODEL), lambda i: (0, i, 0)), row(D_MODEL), row(D_MODEL), row(D_MODEL),
                  pl.BlockSpec((D_MODEL, IN_PAD), lambda i: (0, 0), pipeline_mode=pl.Buffered(1)), row(IN_PAD)],
        out_specs=(pl.BlockSpec((tm, n_q), lambda i: (i, 0)),
                   pl.BlockSpec((tm, REST_W), lambda i: (i, 0)),
                   pl.BlockSpec((D_MODEL, tm), lambda i: (0, i))),
        out_shape=(jax.ShapeDtypeStruct((T, n_q), _BF16),
                   jax.ShapeDtypeStruct((T, REST_W), _F32),
                   jax.ShapeDtypeStruct((D_MODEL, T), _BF16)),
        compiler_params=_cparams(("arbitrary",)),
    )(x, scale1p, shift, norm_w, w_in_bf, b_in_pad)


def _na_class_rows(rows):
    nb = rows // NA_RB
    out = []
    for rb in (0, min(1, nb - 1), nb - 1):
        ws = int(np.clip(NA_RB * rb - 4, 0, rows - NA_WIN))
        out.append((NA_RB * rb + np.arange(NA_RB), ws + np.arange(NA_WIN)))
    return out


def _na_pair_index(rows, qrows, krows):
    start = lambda r: np.clip(r - NA_KH // 2, 0, rows - NA_KH)
    col = np.arange(GRID_W)
    cstart = np.clip(col - NA_KW // 2, 0, GRID_W - NA_KW)
    dy = krows[None, :] - qrows[:, None] + NA_KH - 1
    vr = (krows[None, :] >= start(qrows)[:, None]) & (krows[None, :] < start(qrows)[:, None] + NA_KH)
    dx = np.clip(col[None, :] - col[:, None], -(NA_KW - 1), NA_KW - 1) + NA_KW - 1
    vc = (col[None, :] >= cstart[:, None]) & (col[None, :] < cstart[:, None] + NA_KW)
    nq, nk = len(qrows), len(krows)
    dy4 = np.broadcast_to(np.clip(dy, 0, 2 * NA_KH - 2)[:, None, :, None], (nq, GRID_W, nk, GRID_W))
    dx4 = np.broadcast_to(dx[None, :, None, :], (nq, GRID_W, nk, GRID_W))
    valid = vr[:, None, :, None] & vc[None, :, None, :]
    idx = (dy4 * (2 * NA_KW - 1) + dx4).reshape(nq * GRID_W, nk * GRID_W)
    return idx.astype(np.int32), valid.reshape(nq * GRID_W, nk * GRID_W), (dy, vr, dx, vc)


def _na_half_slabs(rpb):
    _, _, (_, _, dx, vc) = _na_pair_index(NA_WIN, np.arange(1), np.arange(1))
    qc, kc = np.meshgrid(np.arange(GRID_W), np.arange(GRID_W), indexing="ij")
    consts = []
    for right in (False, True):
        pos = (qc * 128 + (GRID_W if right else 0) + kc).reshape(-1)
        oh = np.zeros((32, GRID_W * 128), np.float32)
        oh[dx[qc, kc].reshape(-1), pos] = 1.0
        col_neg = np.zeros((1, GRID_W * 128), np.float32)
        col_neg[0, pos] = np.where(vc[qc, kc].reshape(-1), 0.0, NEG)
        half = np.zeros((1, GRID_W * 128), np.float32)
        half[0, pos] = 1.0
        consts += [jnp.asarray(oh), jnp.asarray(col_neg), jnp.asarray(half)]
    row_neg = np.where(np.arange(NA_HEADS * 16) % 16 == 15, NEG, 0.0).astype(np.float32).reshape(-1, 1)
    rp = jnp.pad(rpb, ((0, 0), (0, 1), (0, 1))).reshape(NA_HEADS * 16, 32)

    def body(*refs):
        r_ref, rn_ref = refs[0], refs[1]
        for t in range(2):
            oh_ref, cn_ref, half_ref = refs[2 + 3 * t:5 + 3 * t]
            refs[8 + t][...] = (jnp.dot(r_ref[...], oh_ref[...], precision=_HI, preferred_element_type=_F32)
                                + cn_ref[...] + rn_ref[...] * half_ref[...])

    outs = pl.pallas_call(
        body, name="na_half_slabs",
        out_shape=tuple(jax.ShapeDtypeStruct((NA_HEADS * 16, GRID_W * 128), _F32) for _ in range(2)),
        compiler_params=_cparams(),
    )(rp, jnp.asarray(row_neg), *consts)
    return [o.reshape(NA_HEADS, 16, GRID_W, 128) for o in outs]


def _na_bias_tables(rpb, rows):
    left, right = _na_half_slabs(rpb)
    didx = []
    for blk, win in _na_class_rows(rows):
        _, _, (dy, vr, _, _) = _na_pair_index(rows, blk, win)
        didx.append(np.where(vr, dy, 15))

    def body(l_ref, r_ref, b_ref):
        for ci, tab in enumerate(didx):
            for a in range(NA_RB):
                for j in range(NA_WIN // 2):
                    b_ref[ci, 0, a * GRID_W:(a + 1) * GRID_W, j * 128:(j + 1) * 128] = (
                        l_ref[0, int(tab[a, 2 * j])] + r_ref[0, int(tab[a, 2 * j + 1])])

    slab = pl.BlockSpec((1, 16, GRID_W, 128), lambda h: (h, 0, 0, 0))
    return pl.pallas_call(
        body, name="na_tables", grid=(NA_HEADS,),
        in_specs=[slab] * 2,
        out_specs=pl.BlockSpec((3, 1, NA_RB * GRID_W, NA_WIN * GRID_W), lambda h: (0, h, 0, 0)),
        out_shape=jax.ShapeDtypeStruct((3, NA_HEADS, NA_RB * GRID_W, NA_WIN * GRID_W), _F32),
        compiler_params=_cparams(("arbitrary",)),
    )(left, right)


def _stack_heads(x, first):
    zero = jnp.zeros_like(x)
    return jnp.concatenate([jnp.where(first, x, zero), jnp.where(first, zero, x)], axis=0)


def _na_sub(rb, u, rows):
    sb = NA_SUB * rb + u
    nb = rows // NA_RB
    cls = jnp.where(sb == 0, 0, jnp.where(sb == nb - 1, 2, 1))
    ws = pl.multiple_of(jnp.clip(NA_RB * sb - 4, 0, rows - NA_WIN) * GRID_W, 256)
    return cls, ws


def _na_fwd(qkv, bias, T):
    rows = T // GRID_W
    tq = NA_RB * GRID_W
    tw = NA_WIN * GRID_W
    ts = NA_SUB * tq

    def body(q_ref, k_ref, v_ref, b_ref, o_ref, l_ref):
        rb = pl.program_id(1)
        lane = lax.broadcasted_iota(jnp.int32, (1, 128), 1)
        first = lane < NA_HEAD_DIM
        for u in range(NA_SUB):
            cls, ws = _na_sub(rb, u, rows)
            kw = k_ref[pl.ds(ws, tw), :]
            vw = v_ref[pl.ds(ws, tw), :]
            q2 = _stack_heads(q_ref[u * tq:(u + 1) * tq, :], first)
            s = _nt(q2, kw) + b_ref[cls].reshape(2 * tq, tw)
            m = jnp.max(s, axis=1, keepdims=True)
            p = jnp.exp(s - m)
            l = jnp.sum(p, axis=1, keepdims=True)
            o2 = _nn(p.astype(_BF16), vw) / l
            lse2 = m + jnp.log(l)
            o_ref[u * tq:(u + 1) * tq, :] = jnp.where(first, o2[:tq], o2[tq:])
            l_ref[u * tq:(u + 1) * tq, :] = jnp.where(first, lse2[:tq], lse2[tq:])

    blk = lambda off: pl.BlockSpec((ts, 128), lambda hp, rb: (rb, off + hp))
    whole = lambda off: pl.BlockSpec((T, 128), lambda hp, rb: (0, off + hp))
    return pl.pallas_call(
        body, name="na_fwd", grid=(NA_HEADS // 2, T // ts),
        in_specs=[blk(0), whole(4), whole(8),
                  pl.BlockSpec((3, 2, tq, tw), lambda hp, rb: (0, hp, 0, 0))],
        out_specs=(blk(0), blk(0)),
        out_shape=(jax.ShapeDtypeStruct((T, NA_W), _F32), jax.ShapeDtypeStruct((T, NA_W), _F32)),
        compiler_params=_cparams(("arbitrary", "arbitrary")),
    )(qkv, qkv, qkv, bias)


def _na_bwd(qkv, bias, o, d_o, lse, T):
    rows = T // GRID_W
    tq = NA_RB * GRID_W
    tw = NA_WIN * GRID_W
    ts = NA_SUB * tq

    def body(q_ref, k_ref, v_ref, b_ref, o_ref, do_ref, l_ref, dq_ref, dk_ref, dv_ref, db_ref):
        rb = pl.program_id(1)
        lane = lax.broadcasted_iota(jnp.int32, (1, 128), 1)
        first = lane < NA_HEAD_DIM

        @pl.when(rb == 0)
        def _():
            db_ref[...] = jnp.zeros_like(db_ref)
            dk_ref[...] = jnp.zeros_like(dk_ref)
            dv_ref[...] = jnp.zeros_like(dv_ref)

        for u in range(NA_SUB):
            cls, ws = _na_sub(rb, u, rows)
            kw = k_ref[pl.ds(ws, tw), :]
            vw = v_ref[pl.ds(ws, tw), :]
            sl = slice(u * tq, (u + 1) * tq)
            q = q_ref[sl, :]
            d_ov = do_ref[sl, :]
            prod = d_ov.astype(_F32) * o_ref[sl, :]
            lse_v = l_ref[sl, :]
            dqs = []
            dk_win = jnp.zeros((tw, 128), _F32)
            dv_win = jnp.zeros((tw, 128), _F32)
            for hh in range(2):
                msk = first if hh == 0 else jnp.logical_not(first)
                c0 = hh * NA_HEAD_DIM
                qm = jnp.where(msk, q, jnp.zeros_like(q))
                dom = jnp.where(msk, d_ov, jnp.zeros_like(d_ov))
                s = _nt(qm, kw) + b_ref[cls, hh]
                p = jnp.exp(s - lse_v[:, c0:c0 + 1])
                dp = _nt(dom, vw)
                delta = jnp.sum(jnp.where(msk, prod, 0.0), axis=1, keepdims=True)
                ds = p * (dp - delta)
                db_ref[cls, hh] += ds
                dsb = ds.astype(_BF16)
                dqs.append(_nn(dsb, kw) * (NA_HEAD_DIM ** -0.5))
                dk_win = dk_win + _tn(dsb, qm)
                dv_win = dv_win + _tn(p.astype(_BF16), dom)
            dq_ref[sl, :] = jnp.where(first, dqs[0], dqs[1]).astype(_BF16)
            dk_ref[pl.ds(ws, tw), :] += dk_win
            dv_ref[pl.ds(ws, tw), :] += dv_win

    once = pl.Buffered(1)
    blk = lambda off: pl.BlockSpec((ts, 128), lambda hp, rb: (rb, off + hp))
    whole = lambda off: pl.BlockSpec((T, 128), lambda hp, rb: (0, off + hp), pipeline_mode=once)
    tab = pl.BlockSpec((3, 2, tq, tw), lambda hp, rb: (0, hp, 0, 0), pipeline_mode=once)
    return pl.pallas_call(
        body, name="na_bwd", grid=(NA_HEADS // 2, T // ts),
        in_specs=[blk(0), whole(4), whole(8), tab, blk(0), blk(0), blk(0)],
        out_specs=(blk(0), whole(0), whole(0), tab),
        out_shape=(jax.ShapeDtypeStruct((T, NA_W), _BF16), jax.ShapeDtypeStruct((T, NA_W), _F32),
                   jax.ShapeDtypeStruct((T, NA_W), _F32), jax.ShapeDtypeStruct(bias.shape, _F32)),
        compiler_params=_cparams(("arbitrary", "arbitrary")),
    )(qkv, qkv, qkv, bias, o, d_o, lse)


def _rpb_grad(dbias, rows):
    tw = NA_WIN * GRID_W
    lanes = 16 * GRID_W
    offs = [int(win[0] - blk[0] + NA_KH - 1) for blk, win in _na_class_rows(rows)]

    def body(x_ref, g_ref):
        sub = lax.broadcasted_iota(jnp.int32, (NA_RB, 1), 0)
        qc = lax.broadcasted_iota(jnp.int32, (NA_RB * GRID_W, 1), 0) % GRID_W
        tot = jnp.zeros((NA_RB, lanes), _F32)
        for ci in range(3):
            xv = x_ref[ci, 0]
            for bit in range(6):
                xv = jnp.where(((qc >> bit) & 1) == 1, pltpu.roll(xv, tw - (1 << bit), 1), xv)
            acc = pltpu.roll(jnp.sum(xv.reshape(NA_RB, GRID_W, tw), axis=1), NA_KW, 1)
            acc = jnp.concatenate([acc, jnp.zeros((NA_RB, lanes - tw), _F32)], axis=1)
            for a in range(NA_RB):
                tot = tot + jnp.where(sub == a, pltpu.roll(acc, (GRID_W * (offs[ci] - a)) % lanes, 1), 0.0)
        g_ref[0] = jnp.broadcast_to(jnp.sum(tot, axis=0, keepdims=True), (8, lanes))

    g = pl.pallas_call(
        body, name="rpb_grad", grid=(NA_HEADS,),
        in_specs=[pl.BlockSpec((3, 1) + dbias.shape[2:], lambda h: (0, h, 0, 0))],
        out_specs=pl.BlockSpec((1, 8, lanes), lambda h: (h, 0, 0)),
        out_shape=jax.ShapeDtypeStruct((NA_HEADS, 8, lanes), _F32),
        compiler_params=_cparams(("arbitrary",)),
    )(dbias)
    return g[:, 0].reshape(NA_HEADS, 16, GRID_W)[:, :2 * NA_KH - 1, 1:2 * NA_KW]


def _halo_specs(tm, width, col_of, T, order):
    hb = tm // 8
    last = T // 8 - 1
    if order == "ij":
        cur = pl.BlockSpec((tm, width), lambda i, j: (i, col_of(j)))
        prev = pl.BlockSpec((8, width), lambda i, j: (jnp.maximum(i * hb - 1, 0), col_of(j)))
        nxt = pl.BlockSpec((8, width), lambda i, j: (jnp.minimum((i + 1) * hb, last), col_of(j)))
    else:
        cur = pl.BlockSpec((tm, width), lambda j, i: (i, col_of(j)))
        prev = pl.BlockSpec((8, width), lambda j, i: (jnp.maximum(i * hb - 1, 0), col_of(j)))
        nxt = pl.BlockSpec((8, width), lambda j, i: (jnp.minimum((i + 1) * hb, last), col_of(j)))
    return [prev, cur, nxt]


def _extend(prev_ref, cur_ref, next_ref, i, n_i):
    prev = jnp.where(i > 0, prev_ref[...], 0.0)
    nxt = jnp.where(i < n_i - 1, next_ref[...], 0.0)
    return jnp.concatenate([prev, cur_ref[...], nxt], axis=0)


def _conv_fwd(rest, conv_w, conv_b, T):
    tm = 512
    n_i = T // tm
    n = tm + 16

    def body(p_ref, c_ref, n_ref, w_ref, b_ref, o_ref):
        i = pl.program_id(0)
        ext = _extend(p_ref, c_ref, n_ref, i, n_i)
        acc = jnp.zeros((tm, 512), _F32) + b_ref[...]
        for j in range(CONV_W):
            acc = acc + w_ref[j:j + 1, :] * pltpu.roll(ext, (2 - j) % n, 0)[8:8 + tm]
        o_ref[...] = _silu(acc)

    return pl.pallas_call(
        body, name="conv_fwd", grid=(n_i, 2),
        in_specs=_halo_specs(tm, 512, lambda j: 1 + j, T, "ij")
        + [pl.BlockSpec((8, 512), lambda i, j: (0, j)), pl.BlockSpec((1, 512), lambda i, j: (0, j))],
        out_specs=pl.BlockSpec((tm, 512), lambda i, j: (i, j)),
        out_shape=jax.ShapeDtypeStruct((T, 2 * ML_W), _F32),
        compiler_params=_cparams(("arbitrary", "arbitrary")),
    )(rest, rest, rest, conv_w, conv_b)


def _conv_bwd(rest, conv_w, conv_b, da_f, da_b, T):
    tm = 512
    n_i = T // tm
    n = tm + 16

    def body(up, uc, un, fp, fc, fn, bp, bc, bn, w_ref, b_ref, du_ref, dw_ref):
        i = pl.program_id(1)
        ext_u = _extend(up, uc, un, i, n_i)
        ext_da = _extend(fp, fc, fn, i, n_i) + _extend(bp, bc, bn, i, n_i)
        shifted = [pltpu.roll(ext_u, (2 - j) % n, 0) for j in range(CONV_W)]
        pre = jnp.zeros((n, 512), _F32) + b_ref[...]
        for j in range(CONV_W):
            pre = pre + w_ref[j:j + 1, :] * shifted[j]
        gidx = i * tm - 8 + lax.broadcasted_iota(jnp.int32, (n, 1), 0)
        dpre = jnp.where((gidx >= 0) & (gidx < T), ext_da * _dsilu(pre), 0.0)
        du = jnp.zeros((tm, 512), _F32)
        for j in range(CONV_W):
            du = du + w_ref[j:j + 1, :] * pltpu.roll(dpre, (j - 2) % n, 0)[8:8 + tm]
        du_ref[...] = du.astype(_BF16)
        dpc = dpre[8:8 + tm]
        parts = [jnp.sum(dpc * shifted[j][8:8 + tm], axis=0, keepdims=True) for j in range(CONV_W)]
        parts.append(jnp.sum(dpc, axis=0, keepdims=True))
        parts.append(jnp.zeros((2, 512), _F32))
        upd = jnp.concatenate(parts, axis=0)

        @pl.when(i == 0)
        def _():
            dw_ref[...] = upd

        @pl.when(i > 0)
        def _():
            dw_ref[...] += upd

    return pl.pallas_call(
        body, name="conv_bwd", grid=(2, n_i),
        in_specs=_halo_specs(tm, 512, lambda j: 1 + j, T, "ji")
        + _halo_specs(tm, 512, lambda j: j, T, "ji") + _halo_specs(tm, 512, lambda j: j, T, "ji")
        + [pl.BlockSpec((8, 512), lambda j, i: (0, j)), pl.BlockSpec((1, 512), lambda j, i: (0, j))],
        out_specs=(pl.BlockSpec((tm, 512), lambda j, i: (i, j)), pl.BlockSpec((8, 512), lambda j, i: (0, j))),
        out_shape=(jax.ShapeDtypeStruct((T, 2 * ML_W), _BF16), jax.ShapeDtypeStruct((8, 2 * ML_W), _F32)),
        compiler_params=_cparams(("arbitrary", "arbitrary")),
    )(rest, rest, rest, da_f, da_f, da_f, da_b, da_b, da_b, conv_w, conv_b)


def _scan_rows(x, suffix):
    L = x.shape[0]
    row = lax.broadcasted_iota(jnp.int32, (L, 1), 0)
    step = 1
    while step < L:
        if suffix:
            x = x + jnp.where(row < L - step, pltpu.roll(x, L - step, 0), 0.0)
        else:
            x = x + jnp.where(row >= step, pltpu.roll(x, step, 0), 0.0)
        step *= 2
    return x


def _ml_gates(gt, rev):
    L = gt.shape[0]
    ri = lax.broadcasted_iota(jnp.int32, (L, L), 0)
    ci = lax.broadcasted_iota(jnp.int32, (L, L), 1)
    mask = (ci >= ri) if rev else (ci <= ri)
    lf = jnp.minimum(gt, 0.0) - jnp.log(1.0 + jnp.exp(-jnp.abs(gt)))
    b = _scan_rows(lf, suffix=rev)
    return mask, b, b.T, gt.T


def _ml_head_gates(gt, gates, head, rev):
    _, b, b_t, gt_t = gates
    ci = (8 if rev else 0) + head
    cf = ci + ML_HEADS
    last = 0 if rev else gt.shape[0] - 1
    return dict(icol=gt[:, ci:ci + 1], b_col=b[:, cf:cf + 1], b_row=b_t[cf:cf + 1, :],
                i_row=gt_t[ci:ci + 1, :], bl=b[last:last + 1, cf:cf + 1])


def _ml_chunk(q, k, v, hg, mask, C, n, m, saved=None):
    icol, b_col, b_row, bl = hg["icol"], hg["b_col"], hg["b_row"], hg["bl"]
    if saved is None:
        dlog = jnp.where(mask, b_col - b_row + hg["i_row"], NEG)
        m_t = jnp.maximum(b_col + m, jnp.max(dlog, axis=1, keepdims=True))
        dm = jnp.exp(dlog - m_t)
    else:
        dm, m_t = saved[0].astype(_F32), saved[1]
    ks = k * (ML_HEAD_DIM ** -0.5)
    qb, kb, vb = q.astype(_BF16), ks.astype(_BF16), v.astype(_BF16)
    s = _nt(qb, kb) * dm
    g = jnp.exp(b_col + m - m_t)
    qc = _nt(qb, C.astype(_BF16))
    num = _nn(s.astype(_BF16), vb) + g * qc
    qn = jnp.sum(q * n, axis=1, keepdims=True)
    den = jnp.sum(s, axis=1, keepdims=True) + g * qn
    e_m = jnp.exp(-m_t)
    nrm = jnp.maximum(jnp.abs(den), e_m)
    h = num / nrm
    a_col = bl - b_col + icol
    m_new = jnp.maximum(bl + m, jnp.max(a_col, axis=0, keepdims=True))
    decay = jnp.exp(bl + m - m_new)
    w = jnp.exp(a_col - m_new)
    c_new = decay * C + _tn((w * v).astype(_BF16), kb)
    n_new = decay * n + jnp.sum(w * ks, axis=0, keepdims=True)
    aux = dict(dm=dm, m_t=m_t, ks=ks, qb=qb, kb=kb, vb=vb, s=s, g=g, qc=qc, qn=qn,
               den=den, e_m=e_m, nrm=nrm, decay=decay, w=w)
    return h, c_new, n_new, m_new, aux


def _mlstm_fwd(qk_act, rest, T, rev):
    tb = ML_CB * ML_CHUNK
    nblk = T // tb
    nc = T // ML_CHUNK
    bi = (lambda i: nblk - 1 - i) if rev else (lambda i: i)

    def body(q_ref, k_ref, v_ref, g_ref, h_ref, cs_ref, ns_ref, ms_ref, dm_ref, mt_ref, c_scr, n_scr, m_scr):
        @pl.when(pl.program_id(0) == 0)
        def _():
            c_scr[...] = jnp.zeros_like(c_scr)
            n_scr[...] = jnp.zeros_like(n_scr)
            m_scr[...] = jnp.zeros_like(m_scr)

        def step(j, carry):
            c = (ML_CB - 1 - j) if rev else j
            r0 = pl.multiple_of(c * ML_CHUNK, ML_CHUNK)
            gt = g_ref[pl.ds(r0, ML_CHUNK), :]
            gates = _ml_gates(gt, rev)
            lane = lax.broadcasted_iota(jnp.int32, (1, 128), 1)
            mt_tile = jnp.zeros((ML_CHUNK, 128), _F32)
            for hd in range(ML_HEADS):
                cols = slice(hd * ML_HEAD_DIM, (hd + 1) * ML_HEAD_DIM)
                C = c_scr[hd]
                n = n_scr[hd:hd + 1, :]
                mrow = m_scr[hd:hd + 1, :]
                cs_ref[c, hd] = C
                ns_ref[c, hd:hd + 1, :] = n
                ms_ref[c, hd:hd + 1, :] = mrow
                h, c_new, n_new, m_new, a = _ml_chunk(
                    q_ref[pl.ds(r0, ML_CHUNK), cols], k_ref[pl.ds(r0, ML_CHUNK), cols],
                    v_ref[pl.ds(r0, ML_CHUNK), cols], _ml_head_gates(gt, gates, hd, rev), gates[0],
                    C, n, mrow[:, 0:1])
                h_ref[pl.ds(r0, ML_CHUNK), cols] = h
                dm_ref[c, hd] = a["dm"].astype(_BF16)
                mt_tile = jnp.where(lane == hd, a["m_t"], mt_tile)
                c_scr[hd] = c_new
                n_scr[hd:hd + 1, :] = n_new
                m_scr[hd:hd + 1, :] = jnp.broadcast_to(m_new, (1, 128))
            mt_ref[c] = mt_tile
            return carry

        lax.fori_loop(0, ML_CB, step, 0)

    return pl.pallas_call(
        body, name="mlstm_fwd_rev" if rev else "mlstm_fwd", grid=(nblk,),
        in_specs=[pl.BlockSpec((tb, ML_W), lambda i: (bi(i), 0)),
                  pl.BlockSpec((tb, ML_W), lambda i: (bi(i), 1)),
                  pl.BlockSpec((tb, ML_W), lambda i: (bi(i), 3)),
                  pl.BlockSpec((tb, 128), lambda i: (bi(i), GATE_COL // 128))],
        out_specs=(pl.BlockSpec((tb, ML_W), lambda i: (bi(i), 0)),
                   pl.BlockSpec((ML_CB, ML_HEADS, 128, 128), lambda i: (bi(i), 0, 0, 0)),
                   pl.BlockSpec((ML_CB, ML_HEADS, 128), lambda i: (bi(i), 0, 0)),
                   pl.BlockSpec((ML_CB, ML_HEADS, 128), lambda i: (bi(i), 0, 0)),
                   pl.BlockSpec((ML_CB, ML_HEADS, ML_CHUNK, ML_CHUNK), lambda i: (bi(i), 0, 0, 0)),
                   pl.BlockSpec((ML_CB, ML_CHUNK, 128), lambda i: (bi(i), 0, 0))),
        out_shape=(jax.ShapeDtypeStruct((T, ML_W), _F32),
                   jax.ShapeDtypeStruct((nc, ML_HEADS, 128, 128), _F32),
                   jax.ShapeDtypeStruct((nc, ML_HEADS, 128), _F32),
                   jax.ShapeDtypeStruct((nc, ML_HEADS, 128), _F32),
                   jax.ShapeDtypeStruct((nc, ML_HEADS, ML_CHUNK, ML_CHUNK), _BF16),
                   jax.ShapeDtypeStruct((nc, ML_CHUNK, 128), _F32)),
        scratch_shapes=[pltpu.VMEM((ML_HEADS, 128, 128), _F32), pltpu.VMEM((8, 128), _F32),
                        pltpu.VMEM((8, 128), _F32)],
        compiler_params=_cparams(("arbitrary",)),
    )(qk_act, qk_act, rest, rest)


def _mlstm_bwd(qk_act, rest, d_h, saved, T, rev):
    tb = ML_CB * ML_CHUNK
    nblk = T // tb
    bi = (lambda i: i) if rev else (lambda i: nblk - 1 - i)

    def body(q_ref, k_ref, v_ref, g_ref, dh_ref, cs_ref, ns_ref, ms_ref, dm_ref, mt_ref,
             dqk_ref, dv_ref, dg_ref, dc_scr, dn_scr):
        @pl.when(pl.program_id(0) == 0)
        def _():
            dc_scr[...] = jnp.zeros_like(dc_scr)
            dn_scr[...] = jnp.zeros_like(dn_scr)

        def step(j, carry):
            c = j if rev else (ML_CB - 1 - j)
            r0 = pl.multiple_of(c * ML_CHUNK, ML_CHUNK)
            gt = g_ref[pl.ds(r0, ML_CHUNK), :]
            gates = _ml_gates(gt, rev)
            mask = gates[0]
            lane = lax.broadcasted_iota(jnp.int32, (1, 128), 1)
            sub = lax.broadcasted_iota(jnp.int32, (128, 1), 0)
            db_t = jnp.zeros((ML_CHUNK, 128), _F32)
            da_t = jnp.zeros((ML_CHUNK, 128), _F32)
            cs_rows = jnp.zeros((128, ML_CHUNK), _F32)
            dbl_t = jnp.zeros((1, 128), _F32)
            for hd in range(ML_HEADS):
                cols = slice(hd * ML_HEAD_DIM, (hd + 1) * ML_HEAD_DIM)
                ci = (8 if rev else 0) + hd
                cf = ci + ML_HEADS
                q = q_ref[pl.ds(r0, ML_CHUNK), cols]
                k = k_ref[pl.ds(r0, ML_CHUNK), cols]
                v = v_ref[pl.ds(r0, ML_CHUNK), cols]
                C = cs_ref[c, hd]
                n = ns_ref[c, hd:hd + 1, :]
                m = ms_ref[c, hd:hd + 1, :][:, 0:1]
                dcn = dc_scr[hd]
                dnn = dn_scr[hd:hd + 1, :]
                h, _, _, _, a = _ml_chunk(q, k, v, _ml_head_gates(gt, gates, hd, rev), mask, C, n, m,
                                          saved=(dm_ref[c, hd], mt_ref[c][:, hd:hd + 1]))
                d_hv = dh_ref[pl.ds(r0, ML_CHUNK), cols]
                g, s, w, ks = a["g"], a["s"], a["w"], a["ks"]
                qb, kb, vb = a["qb"], a["kb"], a["vb"]
                dnum = d_hv / a["nrm"]
                hdot = jnp.sum(d_hv * h, axis=1, keepdims=True)
                dden = jnp.where(jnp.abs(a["den"]) >= a["e_m"], -hdot / a["nrm"] * jnp.sign(a["den"]), 0.0)
                dnb = dnum.astype(_BF16)
                d_s = _nt(dnb, vb) + dden
                r = d_s * s
                dsqk = (d_s * a["dm"]).astype(_BF16)
                cb = C.astype(_BF16)
                dq = _nn(dsqk, kb) + g * _nn(dnb, cb) + (dden * g) * n
                dk = _tn(dsqk, qb)
                dv = _tn(s.astype(_BF16), dnb)
                dg = jnp.sum(dnum * a["qc"], axis=1, keepdims=True) + dden * a["qn"]
                db_col = jnp.sum(r, axis=1, keepdims=True) + dg * g
                cs_rows = cs_rows + jnp.where((sub == ci) | (sub == cf), jnp.sum(r, axis=0, keepdims=True), 0.0)
                dc_chunk = _tn((g * dnum).astype(_BF16), qb)
                dn_chunk = jnp.sum((dden * g) * q, axis=0, keepdims=True)
                dcb = dcn.astype(_BF16)
                vdc = _nn(vb, dcb)
                kdc = _nt(kb, dcb)
                dw = jnp.sum(vdc * ks, axis=1, keepdims=True) + jnp.sum(ks * dnn, axis=1, keepdims=True)
                dv = dv + w * kdc
                dk = dk + w * vdc + w * dnn
                da = dw * w
                ddecay = (jnp.sum(jnp.sum(dcn * C, axis=1, keepdims=True), axis=0, keepdims=True)
                          + jnp.sum(dnn * n, axis=1, keepdims=True))
                dbl = ddecay * a["decay"] + jnp.sum(da, axis=0, keepdims=True)
                db_t = db_t + jnp.where(lane == cf, db_col - da, 0.0)
                da_t = da_t + jnp.where(lane == ci, da, 0.0)
                dbl_t = dbl_t + jnp.where(lane == cf, dbl, 0.0)
                dc_scr[hd] = dc_chunk + a["decay"] * dcn
                dn_scr[hd:hd + 1, :] = dn_chunk + a["decay"] * dnn
                dqk_ref[pl.ds(r0, ML_CHUNK), cols] = dq
                dqk_ref[pl.ds(r0, ML_CHUNK), slice(ML_W + hd * 128, ML_W + (hd + 1) * 128)] = dk * (ML_HEAD_DIM ** -0.5)
                dv_ref[pl.ds(r0, ML_CHUNK), cols] = dv.astype(_BF16)
            lo = 8 if rev else 0
            is_i = (lane >= lo) & (lane < lo + ML_HEADS)
            is_f = (lane >= lo + ML_HEADS) & (lane < lo + 2 * ML_HEADS)
            cs_t = cs_rows.T
            db_all = db_t - jnp.where(is_f, cs_t, 0.0)
            dlf = _scan_rows(db_all, suffix=not rev) + dbl_t
            dg_ref[pl.ds(r0, ML_CHUNK), :] = (da_t + jnp.where(is_i, cs_t, 0.0)
                                               + jnp.where(is_f, dlf * _sigmoid(-gt), 0.0))
            return carry

        lax.fori_loop(0, ML_CB, step, 0)

    return pl.pallas_call(
        body, name="mlstm_bwd_rev" if rev else "mlstm_bwd", grid=(nblk,),
        in_specs=[pl.BlockSpec((tb, ML_W), lambda i: (bi(i), 0)),
                  pl.BlockSpec((tb, ML_W), lambda i: (bi(i), 1)),
                  pl.BlockSpec((tb, ML_W), lambda i: (bi(i), 3)),
                  pl.BlockSpec((tb, 128), lambda i: (bi(i), GATE_COL // 128)),
                  pl.BlockSpec((tb, ML_W), lambda i: (bi(i), 0)),
                  pl.BlockSpec((ML_CB, ML_HEADS, 128, 128), lambda i: (bi(i), 0, 0, 0)),
                  pl.BlockSpec((ML_CB, ML_HEADS, 128), lambda i: (bi(i), 0, 0)),
                  pl.BlockSpec((ML_CB, ML_HEADS, 128), lambda i: (bi(i), 0, 0)),
                  pl.BlockSpec((ML_CB, ML_HEADS, ML_CHUNK, ML_CHUNK), lambda i: (bi(i), 0, 0, 0)),
                  pl.BlockSpec((ML_CB, ML_CHUNK, 128), lambda i: (bi(i), 0, 0))],
        out_specs=(pl.BlockSpec((tb, 2 * ML_W), lambda i: (bi(i), 0)),
                   pl.BlockSpec((tb, ML_W), lambda i: (bi(i), 0)),
                   pl.BlockSpec((tb, 128), lambda i: (bi(i), 0))),
        out_shape=(jax.ShapeDtypeStruct((T, 2 * ML_W), _F32), jax.ShapeDtypeStruct((T, ML_W), _BF16),
                   jax.ShapeDtypeStruct((T, 128), _F32)),
        scratch_shapes=[pltpu.VMEM((ML_HEADS, 128, 128), _F32), pltpu.VMEM((8, 128), _F32)],
        compiler_params=_cparams(("arbitrary",)),
    )(qk_act, qk_act, rest, rest, d_h, *saved)


def _post(x, target, o_na, rest, h_f, h_b, gate, ml_norm_w, final_w, w_out_bf, T):
    tm = 256
    n_i = T // tm

    def body(x_ref, t_ref, o_ref, zna_ref, hf_ref, hb_ref, mo_ref, mz_ref, gate_ref, mw_ref, fw_ref, w_ref,
             dx1_ref, do_ref, dzna_ref, dh_ref, dmo_ref, dmz_ref, dwo_ref, vec_ref):
        i = pl.program_id(0)
        gate_v = gate_ref[...]
        fw = fw_ref[...]
        zna = zna_ref[...]
        o = o_ref[...]
        sig_zna = _sigmoid(zna)
        silu_zna = zna * sig_zna
        na_out = o * silu_zna
        hsum = hf_ref[...] + hb_ref[...]
        sg = _sigmoid(mo_ref[...])
        hm = hsum * sg
        mz = mz_ref[...]
        sig_mz = _sigmoid(mz)
        smz = mz * sig_mz
        dsilu_mz = sig_mz * (1.0 + mz * (1.0 - sig_mz))
        hn_l, rstd_l, ml_l = [], [], []
        for hd in range(ML_HEADS):
            cols = slice(hd * 128, (hd + 1) * 128)
            hh = hm[:, cols]
            mu = jnp.mean(hh, axis=-1, keepdims=True)
            var = jnp.mean(jnp.square(hh - mu), axis=-1, keepdims=True)
            rstd = lax.rsqrt(var + EPS)
            hn = (hh - mu) * rstd
            hn_l.append(hn)
            rstd_l.append(rstd)
            ml_l.append(hn * mw_ref[:, cols] * smz[:, cols])
        mix = jnp.concatenate([na_out] + ml_l, axis=1).astype(_BF16)
        y = _nn(mix, w_ref[...])
        x1 = x_ref[...] + gate_v * y
        r = lax.rsqrt(jnp.mean(x1 * x1, axis=-1, keepdims=True) + EPS)
        xhat = x1 * r
        out = xhat * fw
        err = out - t_ref[...]
        loss = 0.5 * jnp.sum(jnp.sum(err * err, axis=1, keepdims=True), axis=0, keepdims=True) / D_MODEL
        dout = err * (1.0 / D_MODEL)
        dfw = jnp.sum(dout * xhat, axis=0, keepdims=True)
        dxhat = dout * fw
        dx1 = r * (dxhat - xhat * jnp.mean(dxhat * xhat, axis=-1, keepdims=True))
        dx1_ref[...] = dx1
        dgate = jnp.sum(dx1 * y, axis=0, keepdims=True)
        dy = (dx1 * gate_v).astype(_BF16)
        dmix = _nt(dy, w_ref[...])
        dwo = _tn(mix, dy)
        dna = dmix[:, :NA_W]
        do_ref[...] = (dna * silu_zna).astype(_BF16)
        dzna_ref[...] = (dna * o * (sig_zna * (1.0 + zna * (1.0 - sig_zna)))).astype(_BF16)
        dmw_l = []
        for hd in range(ML_HEADS):
            cols = slice(hd * 128, (hd + 1) * 128)
            dml = dmix[:, NA_W + hd * 128:NA_W + (hd + 1) * 128]
            hn = hn_l[hd]
            mwv = mw_ref[:, cols]
            dmz_ref[:, cols] = (dml * hn * mwv * dsilu_mz[:, cols]).astype(_BF16)
            dhn = dml * mwv * smz[:, cols]
            dmw_l.append(jnp.sum(dml * hn * smz[:, cols], axis=0, keepdims=True))
            dhm = rstd_l[hd] * (dhn - jnp.mean(dhn, axis=-1, keepdims=True)
                                - hn * jnp.mean(dhn * hn, axis=-1, keepdims=True))
            sgc = sg[:, cols]
            dh_ref[:, cols] = dhm * sgc
            dmo_ref[:, cols] = (dhm * hsum[:, cols] * sgc * (1.0 - sgc)).astype(_BF16)
        dmw = jnp.concatenate(dmw_l + [jnp.zeros((1, D_MODEL - ML_W), _F32)], axis=1)
        lane = lax.broadcasted_iota(jnp.int32, (1, D_MODEL), 1)
        vec = jnp.concatenate([dfw, dgate, dmw, jnp.where(lane == 0, loss, 0.0),
                               jnp.zeros((4, D_MODEL), _F32)], axis=0)

        @pl.when(i == 0)
        def _():
            dwo_ref[...] = dwo
            vec_ref[...] = vec

        @pl.when(i > 0)
        def _():
            dwo_ref[...] += dwo
            vec_ref[...] += vec

    tok = lambda w, j: pl.BlockSpec((tm, w), lambda i: (i, j))
    tok3 = pl.BlockSpec((None, tm, D_MODEL), lambda i: (0, i, 0))
    row = lambda w: pl.BlockSpec((1, w), lambda i: (0, 0))
    f32 = lambda w: jax.ShapeDtypeStruct((T, w), _F32)
    bf16 = lambda w: jax.ShapeDtypeStruct((T, w), _BF16)
    return pl.pallas_call(
        body, name="post", grid=(n_i,),
        in_specs=[tok3, tok3, tok(NA_W, 0), tok(NA_W, 0), tok(ML_W, 0), tok(ML_W, 0),
                  tok(ML_W, 4), tok(ML_W, 5), row(D_MODEL), row(ML_W), row(D_MODEL),
                  pl.BlockSpec((D_MODEL, D_MODEL), lambda i: (0, 0))],
        out_specs=(tok(D_MODEL, 0), tok(NA_W, 0), tok(NA_W, 0), tok(ML_W, 0), tok(ML_W, 0), tok(ML_W, 0),
                   pl.BlockSpec((D_MODEL, D_MODEL), lambda i: (0, 0)),
                   pl.BlockSpec((8, D_MODEL), lambda i: (0, 0))),
        out_shape=(f32(D_MODEL), bf16(NA_W), bf16(NA_W), f32(ML_W), bf16(ML_W),
                   bf16(ML_W), jax.ShapeDtypeStruct((D_MODEL, D_MODEL), _F32),
                   jax.ShapeDtypeStruct((8, D_MODEL), _F32)),
        compiler_params=_cparams(("arbitrary",)),
    )(x, target, o_na, rest, h_f, h_b, rest, rest, gate, ml_norm_w, final_w, w_out_bf)


def _section_specs(sections, tm):
    specs, args = [], []
    for _, width, parts in sections:
        for arr, cb in parts:
            specs.append(pl.BlockSpec((tm, width), functools.partial(lambda i, cb: (i, cb), cb=cb)))
            args.append(arr)
    return specs, args


def _section_values(sections, refs, dtype):
    vals, at = [], 0
    for _, _, parts in sections:
        v = refs[at][...]
        for r in refs[at + 1:at + len(parts)]:
            v = v.astype(_F32) + r[...].astype(_F32)
        at += len(parts)
        vals.append(v.astype(dtype))
    return vals


def _inproj_bwd_x(x, dx1, scale1p, norm_w, w_in_bf, sections, T):
    tm = 512
    sspecs, sargs = _section_specs(sections, tm)
    ns = len(sargs)

    def body(*refs):
        x_ref, dx1_ref, sc_ref, nw_ref, w_ref = refs[:5]
        srefs = refs[5:5 + ns]
        gx_ref, vec_ref = refs[5 + ns:]
        i = pl.program_id(0)
        vals = _section_values(sections, srefs, _BF16)
        dh = jnp.zeros((tm, D_MODEL), _F32)
        for (c0, width, _), val in zip(sections, vals):
            dh = dh + _nt(val, w_ref[:, c0:c0 + width])
        xv = x_ref[...]
        r = lax.rsqrt(jnp.mean(xv * xv, axis=-1, keepdims=True) + EPS)
        xhat = xv * r
        nw = nw_ref[...]
        dshift = jnp.sum(dh, axis=0, keepdims=True)
        dscale = jnp.sum(dh * xhat * nw, axis=0, keepdims=True)
        dhpre = dh * sc_ref[...]
        dnw = jnp.sum(dhpre * xhat, axis=0, keepdims=True)
        dxhat = dhpre * nw
        gx_ref[...] = dx1_ref[...] + r * (dxhat - xhat * jnp.mean(dxhat * xhat, axis=-1, keepdims=True))
        vec = jnp.concatenate([dshift, dscale, dnw, jnp.zeros((5, D_MODEL), _F32)], axis=0)

        @pl.when(i == 0)
        def _():
            vec_ref[...] = vec

        @pl.when(i > 0)
        def _():
            vec_ref[...] += vec

    row = pl.BlockSpec((1, D_MODEL), lambda i: (0, 0))
    tok = pl.BlockSpec((tm, D_MODEL), lambda i: (i, 0))
    tok3 = pl.BlockSpec((None, tm, D_MODEL), lambda i: (0, i, 0))
    return pl.pallas_call(
        body, name="inproj_bwd_x", grid=(T // tm,),
        in_specs=[tok3, tok, row, row,
                  pl.BlockSpec((D_MODEL, IN_PAD), lambda i: (0, 0), pipeline_mode=pl.Buffered(1))] + sspecs,
        out_specs=(tok3, pl.BlockSpec((8, D_MODEL), lambda i: (0, 0))),
        out_shape=(jax.ShapeDtypeStruct((1, T, D_MODEL), _F32), jax.ShapeDtypeStruct((8, D_MODEL), _F32)),
        compiler_params=_cparams(("arbitrary",)),
    )(x, dx1, scale1p, norm_w, w_in_bf, *sargs)


def _inproj_bwd_w(h_t, sections, T):
    tm = 1024
    n_i = T // tm
    sspecs, sargs = _section_specs(sections, tm)
    ns = len(sargs)

    def body(*refs):
        h_ref = refs[0]
        srefs = refs[1:1 + ns]
        dw_ref, db_ref, acc, sem = refs[1 + ns:]
        i = pl.program_id(0)

        @pl.when(i == 0)
        def _():
            acc[...] = jnp.zeros_like(acc)
            db_ref[...] = jnp.zeros_like(db_ref)

        hv = h_ref[...]
        for (c0, width, _), v in zip(sections, _section_values(sections, srefs, _F32)):
            acc[:, c0:c0 + width] += _nn(hv, v.astype(_BF16))
            db_ref[0:1, c0:c0 + width] += jnp.sum(v, axis=0, keepdims=True)

        @pl.when(i == n_i - 1)
        def _():
            cp = pltpu.make_async_copy(acc, dw_ref, sem)
            cp.start()
            cp.wait()

    return pl.pallas_call(
        body, name="inproj_bwd_w", grid=(n_i,),
        in_specs=[pl.BlockSpec((D_MODEL, tm), lambda i: (0, i))] + sspecs,
        out_specs=(pl.BlockSpec(memory_space=pl.ANY), pl.BlockSpec((8, IN_PAD), lambda i: (0, 0))),
        out_shape=(jax.ShapeDtypeStruct((D_MODEL, IN_PAD), _F32), jax.ShapeDtypeStruct((8, IN_PAD), _F32)),
        scratch_shapes=[pltpu.VMEM((D_MODEL, IN_PAD), _F32), pltpu.SemaphoreType.DMA],
        compiler_params=_cparams(("arbitrary",)),
    )(h_t, *sargs)


def _adamw_math(w, g, m, v):
    m = ADAM_B1 * m + (1.0 - ADAM_B1) * g
    v = ADAM_B2 * v + (1.0 - ADAM_B2) * jnp.square(g)
    m_hat = m / (1.0 - ADAM_B1 ** ADAM_STEP)
    v_hat = v / (1.0 - ADAM_B2 ** ADAM_STEP)
    delta = -ADAM_LR * (m_hat / (jnp.sqrt(v_hat) + ADAM_EPS) + ADAM_WD * w)
    return delta, m, v


def _adamw_slots(w, m, v, slots, tr, name, own=None):
    R, C = w.shape
    extra = [] if own is None else [own]

    def body(w_ref, m_ref, v_ref, s_ref, *refs):
        g_ref, d_ref, nm_ref, nv_ref = refs[len(extra):]
        g = s_ref[0].astype(_F32)
        for k in range(1, N_DEV):
            g = g + s_ref[k].astype(_F32)
        if extra:
            g = g + refs[0][...].astype(_F32)
        g_ref[...] = g
        d_ref[...], nm_ref[...], nv_ref[...] = _adamw_math(w_ref[...], g, m_ref[...], v_ref[...])

    blk = pl.BlockSpec((tr, C), lambda i: (i, 0))
    return pl.pallas_call(
        body, name=name, grid=(R // tr,),
        in_specs=[blk, blk, blk, pl.BlockSpec((N_DEV, tr, C), lambda i: (0, i, 0))] + [blk] * len(extra),
        out_specs=(blk, blk, blk, blk),
        out_shape=tuple(jax.ShapeDtypeStruct((R, C), _F32) for _ in range(4)),
        compiler_params=_cparams(("arbitrary",)),
    )(w, m, v, slots, *extra)


def _w_ada_update(c_all, dmod_my, w, m, v):
    def body(c_ref, d_ref, w_ref, m_ref, v_ref, g_ref, dl_ref, nm_ref, nv_ref):
        g = lax.dot_general(_silu(c_ref[...]), d_ref[...], (((0,), (0,)), ((), ())),
                            precision=_HI, preferred_element_type=_F32)
        g_ref[...] = g
        dl_ref[...], nm_ref[...], nv_ref[...] = _adamw_math(w_ref[...], g, m_ref[...], v_ref[...])

    return pl.pallas_call(
        body, name="w_ada_update",
        out_shape=tuple(jax.ShapeDtypeStruct(w.shape, _F32) for _ in range(4)),
        compiler_params=_cparams(),
    )(c_all, dmod_my, w, m, v)


_PACK = (("b_ada", 3072, 3072), ("norm_w", 1024, 1024), ("b_in", IN_W, IN_PAD), ("conv_w", 5120, 5120),
         ("conv_b", 1024, 1024), ("rpb", 3720, 3840), ("ml_norm_w", 512, 512), ("final_norm_w", 1024, 1024),
         ("loss", 1, 128))
_PACK_OFF = {}
_off = 0
for _name, _len, _pad in _PACK:
    _PACK_OFF[_name] = (_off, _len)
    _off += _pad
_PACK_LEN = _off


def _pack(parts):
    cols = []
    for name, length, pad in _PACK:
        vec = parts[name].reshape(-1).astype(_F32)
        cols.append(jnp.pad(vec, (0, pad - length)))
    return jnp.concatenate(cols).reshape(1, _PACK_LEN)


def _unpack(vec, name, shape):
    off, length = _PACK_OFF[name]
    return vec.reshape(-1)[off:off + length].reshape(shape)


def kernel(x, c, w_ada, b_ada, norm_w, w_in, b_in, conv_w, conv_b, rpb, ml_norm_w, w_out, final_norm_w, loss_target, m_w_ada, m_b_ada, m_norm_w, m_w_in, m_b_in, m_conv_w, m_conv_b, m_rpb, m_ml_norm_w, m_w_out, m_final_norm_w, v_w_ada, v_b_ada, v_norm_w, v_w_in, v_b_in, v_conv_w, v_conv_b, v_rpb, v_ml_norm_w, v_w_out, v_final_norm_w):
    T = x.shape[1]
    rows = T // GRID_W
    me = 4 * lax.axis_index("x") + 2 * lax.axis_index("y") + lax.axis_index("c")
    n_in = w_in.shape[2]
    n_ada = w_ada.shape[2]
    n_cw = conv_w.shape[2]
    n_wo = w_out.shape[1]

    w_in_my, w_out_my = w_in[0].astype(_BF16), w_out[0].astype(_BF16)
    first_leg = (1,) + _CHIP_PEERS
    start_in = _scatter_start([w_in_my], "w_in_start", scatter=False, ks=first_leg)
    start_out = _scatter_start([w_out_my], "w_out_start", scatter=False)
    tokens = start_in[-1][0:1, 0:1] + start_out[-1][0:1, 0:1]
    g_conv_w, g_c = _exchange([conv_w[0], c + tokens], [False] * 2, "gather_small")
    b_in_pad = jnp.pad(b_in, ((0, 0), (0, IN_PAD - IN_W)))
    conv_w_full = jnp.pad(g_conv_w.transpose(1, 0, 2).reshape(CONV_W, N_DEV * n_cw), ((0, 3), (0, 0)))
    c_all = g_c.reshape(N_DEV, D_MODEL)

    b_ada_my = lax.dynamic_slice(b_ada, (0, me * n_ada), (1, n_ada))
    (mod_slots,) = _exchange([_mod_part(c_all, w_ada[0], b_ada_my)], [False], "gather_mod")
    mod = lax.dynamic_index_in_dim(mod_slots, me, axis=1, keepdims=False).reshape(1, 3 * D_MODEL)
    shift, scale, gate = mod[:, :D_MODEL], mod[:, D_MODEL:2 * D_MODEL], mod[:, 2 * D_MODEL:]
    scale1p = 1.0 + scale
    bias = _na_bias_tables(rpb[0], rows)

    def own_slot(land, own):
        return lax.dynamic_update_slice(land, own[None], (me,) + (0,) * own.ndim)

    def gathered(started, after, name):
        (own,), (land,) = _scatter_wait(started, after, name, scatter=False)
        return own_slot(land, own)

    (w_in_own,), (w_in_land,) = _scatter_wait(start_in, bias[0, 0, :8, :128] + scale1p[:, :128], "w_in_wait",
                                              scatter=False, ks=first_leg)
    g_w_in = own_slot(_relay_wait(_relay_start(w_in_land, "w_in_relay_start"), "w_in_relay_wait"), w_in_own)
    w_in_full = g_w_in.transpose(1, 0, 2).reshape(D_MODEL, N_DEV * n_in)
    w_in_bf = jnp.pad(w_in_full, ((0, 0), (0, IN_PAD - IN_W)))

    qkv, rest, h_bf = _inproj_fwd(x, scale1p, shift, norm_w, w_in_bf, b_in_pad)
    o_na, lse = _na_fwd(qkv, bias, T)
    qk_act = _conv_fwd(rest, conv_w_full, conv_b, T)
    h_f, *saved_f = _mlstm_fwd(qk_act, rest, T, False)
    h_b, *saved_b = _mlstm_fwd(qk_act, rest, T, True)

    w_out_bf = gathered(start_out, saved_b[2], "w_out_wait").reshape(N_DEV * n_wo, D_MODEL)
    dx1, d_o, dz_na, d_h, d_mo, d_mz, dwo, pvec = _post(
        x, loss_target, o_na, rest, h_f, h_b, gate, ml_norm_w, final_norm_w.reshape(1, D_MODEL), w_out_bf, T)

    dq_na, dk_na, dv_na, dbias = _na_bwd(qkv, bias, o_na, d_o, lse, T)
    d_rpb = _rpb_grad(dbias, rows)
    dqk_f, dv_f, dg_f = _mlstm_bwd(qk_act, rest, d_h, saved_f, T, False)
    dqk_b, dv_b, dg_b = _mlstm_bwd(qk_act, rest, d_h, saved_b, T, True)
    d_u, dconv = _conv_bwd(rest, conv_w_full, conv_b, dqk_f, dqk_b, T)

    sections = [(0, 512, [(dq_na, 0)]), (512, 512, [(dk_na, 0)]), (1024, 512, [(dv_na, 0)]),
                (1536, 512, [(dz_na, 0)]), (2048, 512, [(d_u, 0)]), (2560, 512, [(d_u, 1)]),
                (3072, 512, [(dv_f, 0), (dv_b, 0)]), (3584, 512, [(d_mo, 0)]), (4096, 512, [(d_mz, 0)]),
                (4608, 128, [(dg_f, 0), (dg_b, 0)])]
    dw_pad, db_pad = _inproj_bwd_w(h_bf, sections, T)
    db_in = db_pad[0, :IN_W]

    dw_blocks = dw_pad[:, :IN_W].astype(_BF16).reshape(D_MODEL, N_DEV, n_in).transpose(1, 0, 2)
    dwo_blocks = dwo.astype(_BF16).reshape(N_DEV, n_wo, D_MODEL)
    started = _scatter_start([dw_blocks, dwo_blocks], "grads_start")
    grad_x, xvec = _inproj_bwd_x(x, dx1, scale1p + started[-1][0:1, 0:1], norm_w, w_in_bf, sections, T)
    (dw_blocks, dwo_blocks), (s_w_in, s_w_out) = _scatter_wait(started, xvec, "grads_wait")

    small = _pack({
        "b_ada": jnp.concatenate([xvec[0], xvec[1], pvec[1]]),
        "norm_w": xvec[2], "b_in": db_in, "conv_w": dconv[:CONV_W], "conv_b": dconv[CONV_W],
        "rpb": d_rpb, "ml_norm_w": pvec[2, :ML_W], "final_norm_w": pvec[0], "loss": pvec[3, :1]})
    (s_small,) = _exchange([small], [False], "exchange_small")

    own = lambda blocks: lax.dynamic_index_in_dim(blocks, me, axis=0, keepdims=False)
    g_w_in_s, d_w_in, nm_w_in, nv_w_in = _adamw_slots(
        w_in[0], m_w_in[0], v_w_in[0], s_w_in, 128, "adamw_w_in", own=own(dw_blocks))
    g_w_out_s, d_w_out, nm_w_out, nv_w_out = _adamw_slots(
        w_out[0], m_w_out[0], v_w_out[0], s_w_out, n_wo, "adamw_w_out", own=own(dwo_blocks))
    dmod_all = s_small[:, 0, :3 * D_MODEL]
    dmod_my = lax.dynamic_slice(dmod_all, (0, me * n_ada), (N_DEV, n_ada))
    g_w_ada, d_w_ada, nm_w_ada, nv_w_ada = _w_ada_update(c_all, dmod_my, w_ada[0], m_w_ada[0], v_w_ada[0])

    def embed(shard):
        return lax.dynamic_update_slice(jnp.zeros((CONV_W, N_DEV * n_cw), _F32), shard[0], (0, me * n_cw))

    zero1 = jnp.zeros((1,), _F32)
    packed = lambda b_a, n_w, b_i, c_w, c_b, rp, mn, fn: _pack({
        "b_ada": b_a, "norm_w": n_w, "b_in": b_i, "conv_w": embed(c_w), "conv_b": c_b, "rpb": rp,
        "ml_norm_w": mn, "final_norm_w": fn, "loss": zero1})
    pw = packed(b_ada, norm_w, b_in, conv_w, conv_b, rpb, ml_norm_w, final_norm_w)
    pm = packed(m_b_ada, m_norm_w, m_b_in, m_conv_w, m_conv_b, m_rpb, m_ml_norm_w, m_final_norm_w)
    pv = packed(v_b_ada, v_norm_w, v_b_in, v_conv_w, v_conv_b, v_rpb, v_ml_norm_w, v_final_norm_w)
    sg, sd, sm, sv = _adamw_slots(pw, pm, pv, s_small, 1, "adamw_small")

    def small_outs(vec):
        cw = lax.dynamic_slice(_unpack(vec, "conv_w", (CONV_W, N_DEV * n_cw)), (0, me * n_cw), (CONV_W, n_cw))
        return dict(b_ada=_unpack(vec, "b_ada", b_ada.shape), norm_w=_unpack(vec, "norm_w", norm_w.shape),
                    b_in=_unpack(vec, "b_in", b_in.shape), conv_w=cw[None],
                    conv_b=_unpack(vec, "conv_b", conv_b.shape), rpb=_unpack(vec, "rpb", rpb.shape),
                    ml_norm_w=_unpack(vec, "ml_norm_w", ml_norm_w.shape),
                    final_norm_w=_unpack(vec, "final_norm_w", final_norm_w.shape))

    loss = _unpack(sg, "loss", ())
    order = ("w_ada", "b_ada", "norm_w", "w_in", "b_in", "conv_w", "conv_b", "rpb", "ml_norm_w", "w_out",
             "final_norm_w")
    outs = []
    for vec, big in ((sg, (g_w_ada, g_w_in_s, g_w_out_s)), (sd, (d_w_ada, d_w_in, d_w_out)),
                     (sm, (nm_w_ada, nm_w_in, nm_w_out)), (sv, (nv_w_ada, nv_w_in, nv_w_out))):
        group = small_outs(vec)
        group.update(w_ada=big[0][None], w_in=big[1][None], w_out=big[2][None])
        outs.extend(group[name] for name in order)
    return (loss, grad_x, *outs)
```

```python
import functools

import numpy as np
import jax
import jax.numpy as jnp
from jax import lax
from jax.experimental import pallas as pl
from jax.experimental.pallas import tpu as pltpu

N_DEV = 8
D_MODEL = 1024
GRID_W = 64
NA_HEADS = 8
NA_HEAD_DIM = 64
NA_KH = 8
NA_KW = 16
NA_W = 512
ML_HEADS = 4
ML_HEAD_DIM = 128
ML_W = 512
ML_CHUNK = 512
CONV_W = 5
EPS = 1e-6
IN_W = 4624
IN_PAD = 4736
REST_W = IN_PAD - 3 * NA_W
GATE_COL = 3072
NEG = -1e30
NA_RB = 4
NA_WIN = 12
NA_SUB = 2
ML_CB = 1
ADAM_LR = 0.001
ADAM_B1 = 0.9
ADAM_B2 = 0.999
ADAM_EPS = 1e-08
ADAM_WD = 0.01
ADAM_STEP = 10
VMEM_LIMIT = 56 * 1024 * 1024

_F32 = jnp.float32
_BF16 = jnp.bfloat16
_HI = lax.Precision.HIGHEST


def _cparams(sem=None):
    return pltpu.CompilerParams(dimension_semantics=sem, vmem_limit_bytes=VMEM_LIMIT)


def _nt(a, b):
    return lax.dot_general(a, b, (((1,), (1,)), ((), ())), preferred_element_type=_F32)


def _tn(a, b):
    return lax.dot_general(a, b, (((0,), (0,)), ((), ())), preferred_element_type=_F32)


def _nn(a, b):
    return jnp.dot(a, b, preferred_element_type=_F32)


def _sigmoid(x):
    return 1.0 / (1.0 + jnp.exp(-x))


def _silu(x):
    return x * _sigmoid(x)


def _dsilu(x):
    s = _sigmoid(x)
    return s * (1.0 + x * (1.0 - s))


def _exchange(arrs, scatter, name):
    n = len(arrs)
    out_shape = []
    for a, sc in zip(arrs, scatter):
        blk = a.shape[1:] if sc else a.shape
        out_shape.append(jax.ShapeDtypeStruct((N_DEV,) + tuple(blk), a.dtype))

    def body(*refs):
        ins = refs[:n]
        outs = refs[n:2 * n]
        send_sems, recv_sems, local_sems = refs[2 * n:]
        x, y, c = lax.axis_index("x"), lax.axis_index("y"), lax.axis_index("c")
        me = 4 * x + 2 * y + c
        local, sends, recvs = [], [], []
        for a in range(n):
            own = ins[a].at[me] if scatter[a] else ins[a]
            cp = pltpu.make_async_copy(own, outs[a].at[me], local_sems.at[a])
            cp.start()
            local.append(cp)
            for k in range(1, N_DEV):
                px = 1 - x if k & 4 else x
                py = 1 - y if k & 2 else y
                pc = 1 - c if k & 1 else c
                p = 4 * px + 2 * py + pc
                src = ins[a].at[p] if scatter[a] else ins[a]
                snd = pltpu.make_async_remote_copy(
                    src_ref=src, dst_ref=outs[a].at[me],
                    send_sem=send_sems.at[a, k - 1], recv_sem=recv_sems.at[a, k - 1],
                    device_id=(px, py, pc), device_id_type=pl.DeviceIdType.MESH)
                snd.start()
                sends.append(snd)
                rcv = pltpu.make_async_remote_copy(
                    src_ref=src, dst_ref=outs[a].at[p],
                    send_sem=send_sems.at[a, k - 1], recv_sem=recv_sems.at[a, k - 1],
                    device_id=(px, py, pc), device_id_type=pl.DeviceIdType.MESH)
                recvs.append(rcv)
        for rcv in recvs:
            rcv.wait_recv()
        for snd in sends:
            snd.wait_send()
        for cp in local:
            cp.wait()

    any_spec = pl.BlockSpec(memory_space=pl.ANY)
    res = pl.pallas_call(
        body, name=name, out_shape=tuple(out_shape),
        in_specs=[any_spec] * n, out_specs=tuple([any_spec] * n),
        scratch_shapes=[pltpu.SemaphoreType.DMA((n, N_DEV - 1)),
                        pltpu.SemaphoreType.DMA((n, N_DEV - 1)),
                        pltpu.SemaphoreType.DMA((n,))],
    )(*arrs)
    return list(res)


def _peer(k):
    x, y, c = lax.axis_index("x"), lax.axis_index("y"), lax.axis_index("c")
    px = 1 - x if k & 4 else x
    py = 1 - y if k & 2 else y
    pc = 1 - c if k & 1 else c
    return (px, py, pc), 4 * px + 2 * py + pc, 4 * x + 2 * y + c


_ALL_PEERS = tuple(range(1, N_DEV))
_CHIP_PEERS = (2, 4, 6)


def _scatter_copy(srcs, lands, send_sems, recv_sems, a, k, receive, scatter, ks=_ALL_PEERS):
    dev, p, me = _peer(k)
    at = a * len(ks) + ks.index(k)
    return pltpu.make_async_remote_copy(
        src_ref=srcs[a].at[p] if scatter else srcs[a], dst_ref=lands[a].at[p if receive else me],
        send_sem=send_sems[at], recv_sem=recv_sems[at],
        device_id=dev, device_id_type=pl.DeviceIdType.MESH)


def _scatter_start(arrs, name, scatter=True, ks=_ALL_PEERS):
    n = len(arrs)
    ns = n * len(ks)
    hbm = pl.BlockSpec(memory_space=pltpu.HBM)
    sem = pl.BlockSpec(memory_space=pltpu.SEMAPHORE)

    def body(*refs):
        srcs, lands = refs[:n], refs[n:2 * n]
        send_sems, recv_sems = refs[2 * n:2 * n + ns], refs[2 * n + ns:2 * n + 2 * ns]
        token = refs[-1]
        for a in range(n):
            for k in ks:
                _scatter_copy(srcs, lands, send_sems, recv_sems, a, k, False, scatter, ks).start()
        token[...] = jnp.zeros_like(token)

    land_shapes = [a.shape if scatter else (N_DEV,) + a.shape for a in arrs]
    buffers = [pltpu.HBM(a.shape, a.dtype) for a in arrs]
    land_buffers = [pltpu.HBM(s, a.dtype) for s, a in zip(land_shapes, arrs)]
    sems = [pltpu.SemaphoreType.DMA(()) for _ in range(2 * ns)]
    res = pl.pallas_call(
        body, name=name,
        out_shape=(*sems, *buffers, *land_buffers, jax.ShapeDtypeStruct((8, 128), _F32)),
        in_specs=[hbm] * (2 * n),
        out_specs=(*([sem] * (2 * ns)), *([hbm] * (2 * n)), pl.BlockSpec(memory_space=pltpu.VMEM)),
        input_output_aliases={i: 2 * ns + i for i in range(2 * n)},
        compiler_params=pltpu.CompilerParams(has_side_effects=pltpu.SideEffectType.DATAFLOW_SIDE_EFFECTING),
    )(*[pltpu.with_memory_space_constraint(a, pltpu.HBM) for a in arrs],
      *[pltpu.with_memory_space_constraint(jnp.zeros(s, a.dtype), pltpu.HBM) for s, a in zip(land_shapes, arrs)])
    res = list(res)
    return (res[:ns], res[ns:2 * ns], res[2 * ns:2 * ns + n], res[2 * ns + n:2 * ns + 2 * n], res[-1])


def _scatter_wait(started, after, name, scatter=True, ks=_ALL_PEERS):
    send_sems, recv_sems, srcs, lands, _ = started
    n = len(srcs)
    ns = len(send_sems)
    hbm = pl.BlockSpec(memory_space=pltpu.HBM)
    sem = pl.BlockSpec(memory_space=pltpu.SEMAPHORE)

    def body(*refs):
        src_refs, land_refs = refs[:n], refs[n:2 * n]
        s_sems, r_sems = refs[2 * n:2 * n + ns], refs[2 * n + ns:2 * n + 2 * ns]
        for a in range(n):
            for k in ks:
                _scatter_copy(src_refs, land_refs, s_sems, r_sems, a, k, False, scatter, ks).wait_send()
                _scatter_copy(src_refs, land_refs, s_sems, r_sems, a, k, True, scatter, ks).wait_recv()

    buffers = [pltpu.HBM(a.shape, a.dtype) for a in list(srcs) + list(lands)]
    res = pl.pallas_call(
        body, name=name, out_shape=tuple(buffers),
        in_specs=[hbm] * (2 * n) + [sem] * (2 * ns) + [pl.BlockSpec(memory_space=pl.ANY)],
        out_specs=tuple([hbm] * (2 * n)),
        input_output_aliases={i: i for i in range(2 * n)},
        compiler_params=pltpu.CompilerParams(has_side_effects=pltpu.SideEffectType.DATAFLOW_SIDE_EFFECTING),
    )(*srcs, *lands, *send_sems, *recv_sems, after)
    return list(res[:n]), list(res[n:])


def _relay_copy(land, send_sems, recv_sems, j, receive):
    k = _CHIP_PEERS[j]
    sibling, _, _ = _peer(1)
    _, slot, _ = _peer(k | 1 if receive else k)
    return pltpu.make_async_remote_copy(
        src_ref=land.at[slot], dst_ref=land.at[slot], send_sem=send_sems[j], recv_sem=recv_sems[j],
        device_id=sibling, device_id_type=pl.DeviceIdType.MESH)


def _relay_start(land, name):
    ns = len(_CHIP_PEERS)
    hbm = pl.BlockSpec(memory_space=pltpu.HBM)
    sem = pl.BlockSpec(memory_space=pltpu.SEMAPHORE)

    def body(*refs):
        land_ref = refs[0]
        send_sems, recv_sems = refs[1:1 + ns], refs[1 + ns:1 + 2 * ns]
        for j in range(ns):
            _relay_copy(land_ref, send_sems, recv_sems, j, False).start()
        refs[-1][...] = jnp.zeros_like(refs[-1])

    sems = [pltpu.SemaphoreType.DMA(()) for _ in range(2 * ns)]
    res = pl.pallas_call(
        body, name=name,
        out_shape=(*sems, pltpu.HBM(land.shape, land.dtype), jax.ShapeDtypeStruct((8, 128), _F32)),
        in_specs=[hbm],
        out_specs=(*([sem] * (2 * ns)), hbm, pl.BlockSpec(memory_space=pltpu.VMEM)),
        input_output_aliases={0: 2 * ns},
        compiler_params=pltpu.CompilerParams(has_side_effects=pltpu.SideEffectType.DATAFLOW_SIDE_EFFECTING),
    )(land)
    res = list(res)
    return res[:ns], res[ns:2 * ns], res[2 * ns], res[-1]


def _relay_wait(started, name):
    send_sems, recv_sems, land, token = started
    ns = len(send_sems)
    hbm = pl.BlockSpec(memory_space=pltpu.HBM)
    sem = pl.BlockSpec(memory_space=pltpu.SEMAPHORE)

    def body(*refs):
        land_ref = refs[0]
        s_sems, r_sems = refs[1:1 + ns], refs[1 + ns:1 + 2 * ns]
        for j in range(ns):
            _relay_copy(land_ref, s_sems, r_sems, j, False).wait_send()
            _relay_copy(land_ref, s_sems, r_sems, j, True).wait_recv()

    return pl.pallas_call(
        body, name=name, out_shape=pltpu.HBM(land.shape, land.dtype),
        in_specs=[hbm] + [sem] * (2 * ns) + [pl.BlockSpec(memory_space=pl.ANY)],
        out_specs=hbm, input_output_aliases={0: 0},
        compiler_params=pltpu.CompilerParams(has_side_effects=pltpu.SideEffectType.DATAFLOW_SIDE_EFFECTING),
    )(land, *send_sems, *recv_sems, token)


def _mod_part(c_all, w_ada, b_my):
    def body(c_ref, w_ref, b_ref, o_ref):
        o_ref[...] = jnp.dot(_silu(c_ref[...]), w_ref[...], precision=_HI,
                             preferred_element_type=_F32) + b_ref[...]

    return pl.pallas_call(
        body, name="mod_part",
        out_shape=jax.ShapeDtypeStruct((N_DEV, w_ada.shape[1]), _F32),
        compiler_params=_cparams(),
    )(c_all, w_ada, b_my)


def _inproj_fwd(x, scale1p, shift, norm_w, w_in_bf, b_in_pad):
    T = x.shape[1]
    tm = 512
    n_q = 3 * NA_W

    def body(x_ref, sc_ref, sh_ref, nw_ref, w_ref, b_ref, qkv_ref, rest_ref, h_ref):
        xv = x_ref[...]
        r = lax.rsqrt(jnp.mean(xv * xv, axis=-1, keepdims=True) + EPS)
        h = xv * r * nw_ref[...] * sc_ref[...] + sh_ref[...]
        hb = h.astype(_BF16)
        h_ref[...] = h.T.astype(_BF16)
        for n0 in range(0, IN_PAD, 512):
            wd = min(512, IN_PAD - n0)
            acc = _nn(hb, w_ref[:, n0:n0 + wd]) + b_ref[:, n0:n0 + wd]
            if n0 == 0:
                acc = acc * (NA_HEAD_DIM ** -0.5)
            if n0 < n_q:
                qkv_ref[:, n0:n0 + wd] = acc.astype(_BF16)
            else:
                rest_ref[:, n0 - n_q:n0 - n_q + wd] = acc

    row = lambda w: pl.BlockSpec((1, w), lambda i: (0, 0))
    return pl.pallas_call(
        body, name="inproj_fwd", grid=(T // tm,),
        in_specs=[pl.BlockSpec((None, tm, D_MODEL), lambda i: (0, i, 0)), row(D_MODEL), row(D_MODEL), row(D_MODEL),
                  pl.BlockSpec((D_MODEL, IN_PAD), lambda i: (0, 0), pipeline_mode=pl.Buffered(1)), row(IN_PAD)],
        out_specs=(pl.BlockSpec((tm, n_q), lambda i: (i, 0)),
                   pl.BlockSpec((tm, REST_W), lambda i: (i, 0)),
                   pl.BlockSpec((D_MODEL, tm), lambda i: (0, i))),
        out_shape=(jax.ShapeDtypeStruct((T, n_q), _BF16),
                   jax.ShapeDtypeStruct((T, REST_W), _F32),
                   jax.ShapeDtypeStruct((D_MODEL, T), _BF16)),
        compiler_params=_cparams(("arbitrary",)),
    )(x, scale1p, shift, norm_w, w_in_bf, b_in_pad)


def _na_class_rows(rows):
    nb = rows // NA_RB
    out = []
    for rb in (0, min(1, nb - 1), nb - 1):
        ws = int(np.clip(NA_RB * rb - 4, 0, rows - NA_WIN))
        out.append((NA_RB * rb + np.arange(NA_RB), ws + np.arange(NA_WIN)))
    return out


def _na_pair_index(rows, qrows, krows):
    start = lambda r: np.clip(r - NA_KH // 2, 0, rows - NA_KH)
    col = np.arange(GRID_W)
    cstart = np.clip(col - NA_KW // 2, 0, GRID_W - NA_KW)
    dy = krows[None, :] - qrows[:, None] + NA_KH - 1
    vr = (krows[None, :] >= start(qrows)[:, None]) & (krows[None, :] < start(qrows)[:, None] + NA_KH)
    dx = np.clip(col[None, :] - col[:, None], -(NA_KW - 1), NA_KW - 1) + NA_KW - 1
    vc = (col[None, :] >= cstart[:, None]) & (col[None, :] < cstart[:, None] + NA_KW)
    nq, nk = len(qrows), len(krows)
    dy4 = np.broadcast_to(np.clip(dy, 0, 2 * NA_KH - 2)[:, None, :, None], (nq, GRID_W, nk, GRID_W))
    dx4 = np.broadcast_to(dx[None, :, None, :], (nq, GRID_W, nk, GRID_W))
    valid = vr[:, None, :, None] & vc[None, :, None, :]
    idx = (dy4 * (2 * NA_KW - 1) + dx4).reshape(nq * GRID_W, nk * GRID_W)
    return idx.astype(np.int32), valid.reshape(nq * GRID_W, nk * GRID_W), (dy, vr, dx, vc)


def _na_half_slabs(rpb):
    _, _, (_, _, dx, vc) = _na_pair_index(NA_WIN, np.arange(1), np.arange(1))
    qc, kc = np.meshgrid(np.arange(GRID_W), np.arange(GRID_W), indexing="ij")
    consts = []
    for right in (False, True):
        pos = (qc * 128 + (GRID_W if right else 0) + kc).reshape(-1)
        oh = np.zeros((32, GRID_W * 128), np.float32)
        oh[dx[qc, kc].reshape(-1), pos] = 1.0
        col_neg = np.zeros((1, GRID_W * 128), np.float32)
        col_neg[0, pos] = np.where(vc[qc, kc].reshape(-1), 0.0, NEG)
        half = np.zeros((1, GRID_W * 128), np.float32)
        half[0, pos] = 1.0
        consts += [jnp.asarray(oh), jnp.asarray(col_neg), jnp.asarray(half)]
    row_neg = np.where(np.arange(NA_HEADS * 16) % 16 == 15, NEG, 0.0).astype(np.float32).reshape(-1, 1)
    rp = jnp.pad(rpb, ((0, 0), (0, 1), (0, 1))).reshape(NA_HEADS * 16, 32)

    def body(*refs):
        r_ref, rn_ref = refs[0], refs[1]
        for t in range(2):
            oh_ref, cn_ref, half_ref = refs[2 + 3 * t:5 + 3 * t]
            refs[8 + t][...] = (jnp.dot(r_ref[...], oh_ref[...], precision=_HI, preferred_element_type=_F32)
                                + cn_ref[...] + rn_ref[...] * half_ref[...])

    outs = pl.pallas_call(
        body, name="na_half_slabs",
        out_shape=tuple(jax.ShapeDtypeStruct((NA_HEADS * 16, GRID_W * 128), _F32) for _ in range(2)),
        compiler_params=_cparams(),
    )(rp, jnp.asarray(row_neg), *consts)
    return [o.reshape(NA_HEADS, 16, GRID_W, 128) for o in outs]


def _na_bias_tables(rpb, rows):
    left, right = _na_half_slabs(rpb)
    didx = []
    for blk, win in _na_class_rows(rows):
        _, _, (dy, vr, _, _) = _na_pair_index(rows, blk, win)
        didx.append(np.where(vr, dy, 15))

    def body(l_ref, r_ref, b_ref):
        for ci, tab in enumerate(didx):
            for a in range(NA_RB):
                for j in range(NA_WIN // 2):
                    b_ref[ci, 0, a * GRID_W:(a + 1) * GRID_W, j * 128:(j + 1) * 128] = (
                        l_ref[0, int(tab[a, 2 * j])] + r_ref[0, int(tab[a, 2 * j + 1])])

    slab = pl.BlockSpec((1, 16, GRID_W, 128), lambda h: (h, 0, 0, 0))
    return pl.pallas_call(
        body, name="na_tables", grid=(NA_HEADS,),
        in_specs=[slab] * 2,
        out_specs=pl.BlockSpec((3, 1, NA_RB * GRID_W, NA_WIN * GRID_W), lambda h: (0, h, 0, 0)),
        out_shape=jax.ShapeDtypeStruct((3, NA_HEADS, NA_RB * GRID_W, NA_WIN * GRID_W), _F32),
        compiler_params=_cparams(("arbitrary",)),
    )(left, right)


def _stack_heads(x, first):
    zero = jnp.zeros_like(x)
    return jnp.concatenate([jnp.where(first, x, zero), jnp.where(first, zero, x)], axis=0)


def _na_sub(rb, u, rows):
    sb = NA_SUB * rb + u
    nb = rows // NA_RB
    cls = jnp.where(sb == 0, 0, jnp.where(sb == nb - 1, 2, 1))
    ws = pl.multiple_of(jnp.clip(NA_RB * sb - 4, 0, rows - NA_WIN) * GRID_W, 256)
    return cls, ws


def _na_fwd(qkv, bias, T):
    rows = T // GRID_W
    tq = NA_RB * GRID_W
    tw = NA_WIN * GRID_W
    ts = NA_SUB * tq

    def body(q_ref, k_ref, v_ref, b_ref, o_ref, l_ref):
        rb = pl.program_id(1)
        lane = lax.broadcasted_iota(jnp.int32, (1, 128), 1)
        first = lane < NA_HEAD_DIM
        for u in range(NA_SUB):
            cls, ws = _na_sub(rb, u, rows)
            kw = k_ref[pl.ds(ws, tw), :]
            vw = v_ref[pl.ds(ws, tw), :]
            q2 = _stack_heads(q_ref[u * tq:(u + 1) * tq, :], first)
            s = _nt(q2, kw) + b_ref[cls].reshape(2 * tq, tw)
            m = jnp.max(s, axis=1, keepdims=True)
            p = jnp.exp(s - m)
            l = jnp.sum(p, axis=1, keepdims=True)
            o2 = _nn(p.astype(_BF16), vw) / l
            lse2 = m + jnp.log(l)
            o_ref[u * tq:(u + 1) * tq, :] = jnp.where(first, o2[:tq], o2[tq:])
            l_ref[u * tq:(u + 1) * tq, :] = jnp.where(first, lse2[:tq], lse2[tq:])

    blk = lambda off: pl.BlockSpec((ts, 128), lambda hp, rb: (rb, off + hp))
    whole = lambda off: pl.BlockSpec((T, 128), lambda hp, rb: (0, off + hp))
    return pl.pallas_call(
        body, name="na_fwd", grid=(NA_HEADS // 2, T // ts),
        in_specs=[blk(0), whole(4), whole(8),
                  pl.BlockSpec((3, 2, tq, tw), lambda hp, rb: (0, hp, 0, 0))],
        out_specs=(blk(0), blk(0)),
        out_shape=(jax.ShapeDtypeStruct((T, NA_W), _F32), jax.ShapeDtypeStruct((T, NA_W), _F32)),
        compiler_params=_cparams(("arbitrary", "arbitrary")),
    )(qkv, qkv, qkv, bias)


def _na_bwd(qkv, bias, o, d_o, lse, T):
    rows = T // GRID_W
    tq = NA_RB * GRID_W
    tw = NA_WIN * GRID_W
    ts = NA_SUB * tq

    def body(q_ref, k_ref, v_ref, b_ref, o_ref, do_ref, l_ref, dq_ref, dk_ref, dv_ref, db_ref):
        rb = pl.program_id(1)
        lane = lax.broadcasted_iota(jnp.int32, (1, 128), 1)
        first = lane < NA_HEAD_DIM

        @pl.when(rb == 0)
        def _():
            db_ref[...] = jnp.zeros_like(db_ref)
            dk_ref[...] = jnp.zeros_like(dk_ref)
            dv_ref[...] = jnp.zeros_like(dv_ref)

        for u in range(NA_SUB):
            cls, ws = _na_sub(rb, u, rows)
            kw = k_ref[pl.ds(ws, tw), :]
            vw = v_ref[pl.ds(ws, tw), :]
            sl = slice(u * tq, (u + 1) * tq)
            q = q_ref[sl, :]
            d_ov = do_ref[sl, :]
            prod = d_ov.astype(_F32) * o_ref[sl, :]
            lse_v = l_ref[sl, :]
            dqs = []
            dk_win = jnp.zeros((tw, 128), _F32)
            dv_win = jnp.zeros((tw, 128), _F32)
            for hh in range(2):
                msk = first if hh == 0 else jnp.logical_not(first)
                c0 = hh * NA_HEAD_DIM
                qm = jnp.where(msk, q, jnp.zeros_like(q))
                dom = jnp.where(msk, d_ov, jnp.zeros_like(d_ov))
                s = _nt(qm, kw) + b_ref[cls, hh]
                p = jnp.exp(s - lse_v[:, c0:c0 + 1])
                dp = _nt(dom, vw)
                delta = jnp.sum(jnp.where(msk, prod, 0.0), axis=1, keepdims=True)
                ds = p * (dp - delta)
                db_ref[cls, hh] += ds
                dsb = ds.astype(_BF16)
                dqs.append(_nn(dsb, kw) * (NA_HEAD_DIM ** -0.5))
                dk_win = dk_win + _tn(dsb, qm)
                dv_win = dv_win + _tn(p.astype(_BF16), dom)
            dq_ref[sl, :] = jnp.where(first, dqs[0], dqs[1]).astype(_BF16)
            dk_ref[pl.ds(ws, tw), :] += dk_win
            dv_ref[pl.ds(ws, tw), :] += dv_win

    once = pl.Buffered(1)
    blk = lambda off: pl.BlockSpec((ts, 128), lambda hp, rb: (rb, off + hp))
    whole = lambda off: pl.BlockSpec((T, 128), lambda hp, rb: (0, off + hp), pipeline_mode=once)
    tab = pl.BlockSpec((3, 2, tq, tw), lambda hp, rb: (0, hp, 0, 0), pipeline_mode=once)
    return pl.pallas_call(
        body, name="na_bwd", grid=(NA_HEADS // 2, T // ts),
        in_specs=[blk(0), whole(4), whole(8), tab, blk(0), blk(0), blk(0)],
        out_specs=(blk(0), whole(0), whole(0), tab),
        out_shape=(jax.ShapeDtypeStruct((T, NA_W), _BF16), jax.ShapeDtypeStruct((T, NA_W), _F32),
                   jax.ShapeDtypeStruct((T, NA_W), _F32), jax.ShapeDtypeStruct(bias.shape, _F32)),
        compiler_params=_cparams(("arbitrary", "arbitrary")),
    )(qkv, qkv, qkv, bias, o, d_o, lse)


def _rpb_grad(dbias, rows):
    tw = NA_WIN * GRID_W
    lanes = 16 * GRID_W
    offs = [int(win[0] - blk[0] + NA_KH - 1) for blk, win in _na_class_rows(rows)]

    def body(x_ref, g_ref):
        sub = lax.broadcasted_iota(jnp.int32, (NA_RB, 1), 0)
        qc = lax.broadcasted_iota(jnp.int32, (NA_RB * GRID_W, 1), 0) % GRID_W
        tot = jnp.zeros((NA_RB, lanes), _F32)
        for ci in range(3):
            xv = x_ref[ci, 0]
            for bit in range(6):
                xv = jnp.where(((qc >> bit) & 1) == 1, pltpu.roll(xv, tw - (1 << bit), 1), xv)
            acc = pltpu.roll(jnp.sum(xv.reshape(NA_RB, GRID_W, tw), axis=1), NA_KW, 1)
            acc = jnp.concatenate([acc, jnp.zeros((NA_RB, lanes - tw), _F32)], axis=1)
            for a in range(NA_RB):
                tot = tot + jnp.where(sub == a, pltpu.roll(acc, (GRID_W * (offs[ci] - a)) % lanes, 1), 0.0)
        g_ref[0] = jnp.broadcast_to(jnp.sum(tot, axis=0, keepdims=True), (8, lanes))

    g = pl.pallas_call(
        body, name="rpb_grad", grid=(NA_HEADS,),
        in_specs=[pl.BlockSpec((3, 1) + dbias.shape[2:], lambda h: (0, h, 0, 0))],
        out_specs=pl.BlockSpec((1, 8, lanes), lambda h: (h, 0, 0)),
        out_shape=jax.ShapeDtypeStruct((NA_HEADS, 8, lanes), _F32),
        compiler_params=_cparams(("arbitrary",)),
    )(dbias)
    return g[:, 0].reshape(NA_HEADS, 16, GRID_W)[:, :2 * NA_KH - 1, 1:2 * NA_KW]


def _halo_specs(tm, width, col_of, T, order):
    hb = tm // 8
    last = T // 8 - 1
    if order == "ij":
        cur = pl.BlockSpec((tm, width), lambda i, j: (i, col_of(j)))
        prev = pl.BlockSpec((8, width), lambda i, j: (jnp.maximum(i * hb - 1, 0), col_of(j)))
        nxt = pl.BlockSpec((8, width), lambda i, j: (jnp.minimum((i + 1) * hb, last), col_of(j)))
    else:
        cur = pl.BlockSpec((tm, width), lambda j, i: (i, col_of(j)))
        prev = pl.BlockSpec((8, width), lambda j, i: (jnp.maximum(i * hb - 1, 0), col_of(j)))
        nxt = pl.BlockSpec((8, width), lambda j, i: (jnp.minimum((i + 1) * hb, last), col_of(j)))
    return [prev, cur, nxt]


def _extend(prev_ref, cur_ref, next_ref, i, n_i):
    prev = jnp.where(i > 0, prev_ref[...], 0.0)
    nxt = jnp.where(i < n_i - 1, next_ref[...], 0.0)
    return jnp.concatenate([prev, cur_ref[...], nxt], axis=0)


def _conv_fwd(rest, conv_w, conv_b, T):
    tm = 512
    n_i = T // tm
    n = tm + 16

    def body(p_ref, c_ref, n_ref, w_ref, b_ref, o_ref, pre_ref):
        i = pl.program_id(0)
        ext = _extend(p_ref, c_ref, n_ref, i, n_i)
        acc = jnp.zeros((tm, 512), _F32) + b_ref[...]
        for j in range(CONV_W):
            acc = acc + w_ref[j:j + 1, :] * pltpu.roll(ext, (2 - j) % n, 0)[8:8 + tm]
        pre_ref[...] = acc
        o_ref[...] = _silu(acc)

    out = pl.BlockSpec((tm, 512), lambda i, j: (i, j))
    return pl.pallas_call(
        body, name="conv_fwd", grid=(n_i, 2),
        in_specs=_halo_specs(tm, 512, lambda j: 1 + j, T, "ij")
        + [pl.BlockSpec((8, 512), lambda i, j: (0, j)), pl.BlockSpec((1, 512), lambda i, j: (0, j))],
        out_specs=(out, out),
        out_shape=(jax.ShapeDtypeStruct((T, 2 * ML_W), _F32), jax.ShapeDtypeStruct((T, 2 * ML_W), _F32)),
        compiler_params=_cparams(("arbitrary", "arbitrary")),
    )(rest, rest, rest, conv_w, conv_b)


def _conv_bwd(rest, pre, conv_w, da_f, da_b, T):
    tm = 512
    n_i = T // tm
    n = tm + 16

    def body(u_ref, pp, pc, pn, fp, fc, fn, bp, bc, bn, w_ref, du_ref, dw_ref):
        i = pl.program_id(1)
        ext_pre = _extend(pp, pc, pn, i, n_i)
        ext_da = _extend(fp, fc, fn, i, n_i) + _extend(bp, bc, bn, i, n_i)
        gidx = i * tm - 8 + lax.broadcasted_iota(jnp.int32, (n, 1), 0)
        dpre = jnp.where((gidx >= 0) & (gidx < T), ext_da * _dsilu(ext_pre), 0.0)
        shifted = [pltpu.roll(dpre, (j - 2) % n, 0)[8:8 + tm] for j in range(CONV_W)]
        du = jnp.zeros((tm, 512), _F32)
        for j in range(CONV_W):
            du = du + w_ref[j:j + 1, :] * shifted[j]
        du_ref[...] = du.astype(_BF16)
        u = u_ref[...]
        parts = [jnp.sum(u * shifted[j], axis=0, keepdims=True) for j in range(CONV_W)]
        parts.append(jnp.sum(dpre[8:8 + tm], axis=0, keepdims=True))
        parts.append(jnp.zeros((2, 512), _F32))
        upd = jnp.concatenate(parts, axis=0)

        @pl.when(i == 0)
        def _():
            dw_ref[...] = upd

        @pl.when(i > 0)
        def _():
            dw_ref[...] += upd

    return pl.pallas_call(
        body, name="conv_bwd", grid=(2, n_i),
        in_specs=[pl.BlockSpec((tm, 512), lambda j, i: (i, 1 + j))]
        + _halo_specs(tm, 512, lambda j: j, T, "ji") + _halo_specs(tm, 512, lambda j: j, T, "ji")
        + _halo_specs(tm, 512, lambda j: j, T, "ji")
        + [pl.BlockSpec((8, 512), lambda j, i: (0, j))],
        out_specs=(pl.BlockSpec((tm, 512), lambda j, i: (i, j)), pl.BlockSpec((8, 512), lambda j, i: (0, j))),
        out_shape=(jax.ShapeDtypeStruct((T, 2 * ML_W), _BF16), jax.ShapeDtypeStruct((8, 2 * ML_W), _F32)),
        compiler_params=_cparams(("arbitrary", "arbitrary")),
    )(rest, pre, pre, pre, da_f, da_f, da_f, da_b, da_b, da_b, conv_w)


def _scan_rows(x, suffix):
    L = x.shape[0]
    row = lax.broadcasted_iota(jnp.int32, (L, 1), 0)
    step = 1
    while step < L:
        if suffix:
            x = x + jnp.where(row < L - step, pltpu.roll(x, L - step, 0), 0.0)
        else:
            x = x + jnp.where(row >= step, pltpu.roll(x, step, 0), 0.0)
        step *= 2
    return x


def _ml_gates(gt, rev):
    L = gt.shape[0]
    ri = lax.broadcasted_iota(jnp.int32, (L, L), 0)
    ci = lax.broadcasted_iota(jnp.int32, (L, L), 1)
    mask = (ci >= ri) if rev else (ci <= ri)
    lf = jnp.minimum(gt, 0.0) - jnp.log(1.0 + jnp.exp(-jnp.abs(gt)))
    b = _scan_rows(lf, suffix=rev)
    return mask, b, b.T, gt.T


def _ml_head_gates(gt, gates, head, rev):
    _, b, b_t, gt_t = gates
    ci = (8 if rev else 0) + head
    cf = ci + ML_HEADS
    last = 0 if rev else gt.shape[0] - 1
    return dict(icol=gt[:, ci:ci + 1], b_col=b[:, cf:cf + 1], b_row=b_t[cf:cf + 1, :],
                i_row=gt_t[ci:ci + 1, :], bl=b[last:last + 1, cf:cf + 1])


def _ml_chunk(q, k, v, hg, mask, C, n, m, saved=None):
    icol, b_col, b_row, bl = hg["icol"], hg["b_col"], hg["b_row"], hg["bl"]
    if saved is None:
        dlog = jnp.where(mask, b_col - b_row + hg["i_row"], NEG)
        m_t = jnp.maximum(b_col + m, jnp.max(dlog, axis=1, keepdims=True))
        dm = jnp.exp(dlog - m_t)
    else:
        dm, m_t = saved[0].astype(_F32), saved[1]
    ks = k * (ML_HEAD_DIM ** -0.5)
    qb, kb, vb = q.astype(_BF16), ks.astype(_BF16), v.astype(_BF16)
    s = _nt(qb, kb) * dm
    g = jnp.exp(b_col + m - m_t)
    qc = _nt(qb, C.astype(_BF16))
    num = _nn(s.astype(_BF16), vb) + g * qc
    qn = jnp.sum(q * n, axis=1, keepdims=True)
    den = jnp.sum(s, axis=1, keepdims=True) + g * qn
    e_m = jnp.exp(-m_t)
    nrm = jnp.maximum(jnp.abs(den), e_m)
    h = num / nrm
    a_col = bl - b_col + icol
    m_new = jnp.maximum(bl + m, jnp.max(a_col, axis=0, keepdims=True))
    decay = jnp.exp(bl + m - m_new)
    w = jnp.exp(a_col - m_new)
    c_new = decay * C + _tn((w * v).astype(_BF16), kb)
    n_new = decay * n + jnp.sum(w * ks, axis=0, keepdims=True)
    aux = dict(dm=dm, m_t=m_t, ks=ks, qb=qb, kb=kb, vb=vb, s=s, g=g, qc=qc, qn=qn,
               den=den, e_m=e_m, nrm=nrm, decay=decay, w=w)
    return h, c_new, n_new, m_new, aux


def _mlstm_fwd(qk_act, rest, T, rev):
    tb = ML_CB * ML_CHUNK
    nblk = T // tb
    nc = T // ML_CHUNK
    bi = (lambda i: nblk - 1 - i) if rev else (lambda i: i)

    def body(q_ref, k_ref, v_ref, g_ref, h_ref, cs_ref, ns_ref, ms_ref, dm_ref, mt_ref, c_scr, n_scr, m_scr):
        @pl.when(pl.program_id(0) == 0)
        def _():
            c_scr[...] = jnp.zeros_like(c_scr)
            n_scr[...] = jnp.zeros_like(n_scr)
            m_scr[...] = jnp.zeros_like(m_scr)

        def step(j, carry):
            c = (ML_CB - 1 - j) if rev else j
            r0 = pl.multiple_of(c * ML_CHUNK, ML_CHUNK)
            gt = g_ref[pl.ds(r0, ML_CHUNK), :]
            gates = _ml_gates(gt, rev)
            lane = lax.broadcasted_iota(jnp.int32, (1, 128), 1)
            mt_tile = jnp.zeros((ML_CHUNK, 128), _F32)
            for hd in range(ML_HEADS):
                cols = slice(hd * ML_HEAD_DIM, (hd + 1) * ML_HEAD_DIM)
                C = c_scr[hd]
                n = n_scr[hd:hd + 1, :]
                mrow = m_scr[hd:hd + 1, :]
                cs_ref[c, hd] = C
                ns_ref[c, hd:hd + 1, :] = n
                ms_ref[c, hd:hd + 1, :] = mrow
                h, c_new, n_new, m_new, a = _ml_chunk(
                    q_ref[pl.ds(r0, ML_CHUNK), cols], k_ref[pl.ds(r0, ML_CHUNK), cols],
                    v_ref[pl.ds(r0, ML_CHUNK), cols], _ml_head_gates(gt, gates, hd, rev), gates[0],
                    C, n, mrow[:, 0:1])
                h_ref[pl.ds(r0, ML_CHUNK), cols] = h
                dm_ref[c, hd] = a["dm"].astype(_BF16)
                mt_tile = jnp.where(lane == hd, a["m_t"], mt_tile)
                c_scr[hd] = c_new
                n_scr[hd:hd + 1, :] = n_new
                m_scr[hd:hd + 1, :] = jnp.broadcast_to(m_new, (1, 128))
            mt_ref[c] = mt_tile
            return carry

        lax.fori_loop(0, ML_CB, step, 0)

    return pl.pallas_call(
        body, name="mlstm_fwd_rev" if rev else "mlstm_fwd", grid=(nblk,),
        in_specs=[pl.BlockSpec((tb, ML_W), lambda i: (bi(i), 0)),
                  pl.BlockSpec((tb, ML_W), lambda i: (bi(i), 1)),
                  pl.BlockSpec((tb, ML_W), lambda i: (bi(i), 3)),
                  pl.BlockSpec((tb, 128), lambda i: (bi(i), GATE_COL // 128))],
        out_specs=(pl.BlockSpec((tb, ML_W), lambda i: (bi(i), 0)),
                   pl.BlockSpec((ML_CB, ML_HEADS, 128, 128), lambda i: (bi(i), 0, 0, 0)),
                   pl.BlockSpec((ML_CB, ML_HEADS, 128), lambda i: (bi(i), 0, 0)),
                   pl.BlockSpec((ML_CB, ML_HEADS, 128), lambda i: (bi(i), 0, 0)),
                   pl.BlockSpec((ML_CB, ML_HEADS, ML_CHUNK, ML_CHUNK), lambda i: (bi(i), 0, 0, 0)),
                   pl.BlockSpec((ML_CB, ML_CHUNK, 128), lambda i: (bi(i), 0, 0))),
        out_shape=(jax.ShapeDtypeStruct((T, ML_W), _F32),
                   jax.ShapeDtypeStruct((nc, ML_HEADS, 128, 128), _F32),
                   jax.ShapeDtypeStruct((nc, ML_HEADS, 128), _F32),
                   jax.ShapeDtypeStruct((nc, ML_HEADS, 128), _F32),
                   jax.ShapeDtypeStruct((nc, ML_HEADS, ML_CHUNK, ML_CHUNK), _BF16),
                   jax.ShapeDtypeStruct((nc, ML_CHUNK, 128), _F32)),
        scratch_shapes=[pltpu.VMEM((ML_HEADS, 128, 128), _F32), pltpu.VMEM((8, 128), _F32),
                        pltpu.VMEM((8, 128), _F32)],
        compiler_params=_cparams(("arbitrary",)),
    )(qk_act, qk_act, rest, rest)


def _mlstm_bwd(qk_act, rest, d_h, saved, T, rev):
    tb = ML_CB * ML_CHUNK
    nblk = T // tb
    bi = (lambda i: i) if rev else (lambda i: nblk - 1 - i)

    def body(q_ref, k_ref, v_ref, g_ref, dh_ref, cs_ref, ns_ref, ms_ref, dm_ref, mt_ref,
             dqk_ref, dv_ref, dg_ref, dc_scr, dn_scr):
        @pl.when(pl.program_id(0) == 0)
        def _():
            dc_scr[...] = jnp.zeros_like(dc_scr)
            dn_scr[...] = jnp.zeros_like(dn_scr)

        def step(j, carry):
            c = j if rev else (ML_CB - 1 - j)
            r0 = pl.multiple_of(c * ML_CHUNK, ML_CHUNK)
            gt = g_ref[pl.ds(r0, ML_CHUNK), :]
            gates = _ml_gates(gt, rev)
            mask = gates[0]
            lane = lax.broadcasted_iota(jnp.int32, (1, 128), 1)
            sub = lax.broadcasted_iota(jnp.int32, (128, 1), 0)
            db_t = jnp.zeros((ML_CHUNK, 128), _F32)
            da_t = jnp.zeros((ML_CHUNK, 128), _F32)
            cs_rows = jnp.zeros((128, ML_CHUNK), _F32)
            dbl_t = jnp.zeros((1, 128), _F32)
            for hd in range(ML_HEADS):
                cols = slice(hd * ML_HEAD_DIM, (hd + 1) * ML_HEAD_DIM)
                ci = (8 if rev else 0) + hd
                cf = ci + ML_HEADS
                q = q_ref[pl.ds(r0, ML_CHUNK), cols]
                k = k_ref[pl.ds(r0, ML_CHUNK), cols]
                v = v_ref[pl.ds(r0, ML_CHUNK), cols]
                C = cs_ref[c, hd]
                n = ns_ref[c, hd:hd + 1, :]
                m = ms_ref[c, hd:hd + 1, :][:, 0:1]
                dcn = dc_scr[hd]
                dnn = dn_scr[hd:hd + 1, :]
                h, _, _, _, a = _ml_chunk(q, k, v, _ml_head_gates(gt, gates, hd, rev), mask, C, n, m,
                                          saved=(dm_ref[c, hd], mt_ref[c][:, hd:hd + 1]))
                d_hv = dh_ref[pl.ds(r0, ML_CHUNK), cols]
                g, s, w, ks = a["g"], a["s"], a["w"], a["ks"]
                qb, kb, vb = a["qb"], a["kb"], a["vb"]
                dnum = d_hv / a["nrm"]
                hdot = jnp.sum(d_hv * h, axis=1, keepdims=True)
                dden = jnp.where(jnp.abs(a["den"]) >= a["e_m"], -hdot / a["nrm"] * jnp.sign(a["den"]), 0.0)
                dnb = dnum.astype(_BF16)
                d_s = _nt(dnb, vb) + dden
                r = d_s * s
                dsqk = (d_s * a["dm"]).astype(_BF16)
                cb = C.astype(_BF16)
                dq = _nn(dsqk, kb) + g * _nn(dnb, cb) + (dden * g) * n
                dk = _tn(dsqk, qb)
                dv = _tn(s.astype(_BF16), dnb)
                dg = jnp.sum(dnum * a["qc"], axis=1, keepdims=True) + dden * a["qn"]
                db_col = jnp.sum(r, axis=1, keepdims=True) + dg * g
                cs_rows = cs_rows + jnp.where((sub == ci) | (sub == cf), jnp.sum(r, axis=0, keepdims=True), 0.0)
                dc_chunk = _tn((g * dnum).astype(_BF16), qb)
                dn_chunk = jnp.sum((dden * g) * q, axis=0, keepdims=True)
                dcb = dcn.astype(_BF16)
                vdc = _nn(vb, dcb)
                kdc = _nt(kb, dcb)
                dw = jnp.sum(vdc * ks, axis=1, keepdims=True) + jnp.sum(ks * dnn, axis=1, keepdims=True)
                dv = dv + w * kdc
                dk = dk + w * vdc + w * dnn
                da = dw * w
                ddecay = (jnp.sum(jnp.sum(dcn * C, axis=1, keepdims=True), axis=0, keepdims=True)
                          + jnp.sum(dnn * n, axis=1, keepdims=True))
                dbl = ddecay * a["decay"] + jnp.sum(da, axis=0, keepdims=True)
                db_t = db_t + jnp.where(lane == cf, db_col - da, 0.0)
                da_t = da_t + jnp.where(lane == ci, da, 0.0)
                dbl_t = dbl_t + jnp.where(lane == cf, dbl, 0.0)
                dc_scr[hd] = dc_chunk + a["decay"] * dcn
                dn_scr[hd:hd + 1, :] = dn_chunk + a["decay"] * dnn
                dqk_ref[pl.ds(r0, ML_CHUNK), cols] = dq
                dqk_ref[pl.ds(r0, ML_CHUNK), slice(ML_W + hd * 128, ML_W + (hd + 1) * 128)] = dk * (ML_HEAD_DIM ** -0.5)
                dv_ref[pl.ds(r0, ML_CHUNK), cols] = dv.astype(_BF16)
            lo = 8 if rev else 0
            is_i = (lane >= lo) & (lane < lo + ML_HEADS)
            is_f = (lane >= lo + ML_HEADS) & (lane < lo + 2 * ML_HEADS)
            cs_t = cs_rows.T
            db_all = db_t - jnp.where(is_f, cs_t, 0.0)
            dlf = _scan_rows(db_all, suffix=not rev) + dbl_t
            dg_ref[pl.ds(r0, ML_CHUNK), :] = (da_t + jnp.where(is_i, cs_t, 0.0)
                                               + jnp.where(is_f, dlf * _sigmoid(-gt), 0.0))
            return carry

        lax.fori_loop(0, ML_CB, step, 0)

    return pl.pallas_call(
        body, name="mlstm_bwd_rev" if rev else "mlstm_bwd", grid=(nblk,),
        in_specs=[pl.BlockSpec((tb, ML_W), lambda i: (bi(i), 0)),
                  pl.BlockSpec((tb, ML_W), lambda i: (bi(i), 1)),
                  pl.BlockSpec((tb, ML_W), lambda i: (bi(i), 3)),
                  pl.BlockSpec((tb, 128), lambda i: (bi(i), GATE_COL // 128)),
                  pl.BlockSpec((tb, ML_W), lambda i: (bi(i), 0)),
                  pl.BlockSpec((ML_CB, ML_HEADS, 128, 128), lambda i: (bi(i), 0, 0, 0)),
                  pl.BlockSpec((ML_CB, ML_HEADS, 128), lambda i: (bi(i), 0, 0)),
                  pl.BlockSpec((ML_CB, ML_HEADS, 128), lambda i: (bi(i), 0, 0)),
                  pl.BlockSpec((ML_CB, ML_HEADS, ML_CHUNK, ML_CHUNK), lambda i: (bi(i), 0, 0, 0)),
                  pl.BlockSpec((ML_CB, ML_CHUNK, 128), lambda i: (bi(i), 0, 0))],
        out_specs=(pl.BlockSpec((tb, 2 * ML_W), lambda i: (bi(i), 0)),
                   pl.BlockSpec((tb, ML_W), lambda i: (bi(i), 0)),
                   pl.BlockSpec((tb, 128), lambda i: (bi(i), 0))),
        out_shape=(jax.ShapeDtypeStruct((T, 2 * ML_W), _F32), jax.ShapeDtypeStruct((T, ML_W), _BF16),
                   jax.ShapeDtypeStruct((T, 128), _F32)),
        scratch_shapes=[pltpu.VMEM((ML_HEADS, 128, 128), _F32), pltpu.VMEM((8, 128), _F32)],
        compiler_params=_cparams(("arbitrary",)),
    )(qk_act, qk_act, rest, rest, d_h, *saved)


def _post(x, target, o_na, rest, h_f, h_b, gate, ml_norm_w, final_w, w_out_bf, T):
    tm = 256
    n_i = T // tm

    def body(x_ref, t_ref, o_ref, zna_ref, hf_ref, hb_ref, mo_ref, mz_ref, gate_ref, mw_ref, fw_ref, w_ref,
             dx1_ref, do_ref, dzna_ref, dh_ref, dmo_ref, dmz_ref, dwo_ref, vec_ref):
        i = pl.program_id(0)
        gate_v = gate_ref[...]
        fw = fw_ref[...]
        zna = zna_ref[...]
        o = o_ref[...]
        sig_zna = _sigmoid(zna)
        silu_zna = zna * sig_zna
        na_out = o * silu_zna
        hsum = hf_ref[...] + hb_ref[...]
        sg = _sigmoid(mo_ref[...])
        hm = hsum * sg
        mz = mz_ref[...]
        sig_mz = _sigmoid(mz)
        smz = mz * sig_mz
        dsilu_mz = sig_mz * (1.0 + mz * (1.0 - sig_mz))
        hn_l, rstd_l, ml_l = [], [], []
        for hd in range(ML_HEADS):
            cols = slice(hd * 128, (hd + 1) * 128)
            hh = hm[:, cols]
            mu = jnp.mean(hh, axis=-1, keepdims=True)
            var = jnp.mean(jnp.square(hh - mu), axis=-1, keepdims=True)
            rstd = lax.rsqrt(var + EPS)
            hn = (hh - mu) * rstd
            hn_l.append(hn)
            rstd_l.append(rstd)
            ml_l.append(hn * mw_ref[:, cols] * smz[:, cols])
        mix = jnp.concatenate([na_out] + ml_l, axis=1).astype(_BF16)
        y = _nn(mix, w_ref[...])
        x1 = x_ref[...] + gate_v * y
        r = lax.rsqrt(jnp.mean(x1 * x1, axis=-1, keepdims=True) + EPS)
        xhat = x1 * r
        out = xhat * fw
        err = out - t_ref[...]
        loss = 0.5 * jnp.sum(jnp.sum(err * err, axis=1, keepdims=True), axis=0, keepdims=True) / D_MODEL
        dout = err * (1.0 / D_MODEL)
        dfw = jnp.sum(dout * xhat, axis=0, keepdims=True)
        dxhat = dout * fw
        dx1 = r * (dxhat - xhat * jnp.mean(dxhat * xhat, axis=-1, keepdims=True))
        dx1_ref[...] = dx1
        dgate = jnp.sum(dx1 * y, axis=0, keepdims=True)
        dy = (dx1 * gate_v).astype(_BF16)
        dmix = _nt(dy, w_ref[...])
        dwo = _tn(mix, dy)
        dna = dmix[:, :NA_W]
        do_ref[...] = (dna * silu_zna).astype(_BF16)
        dzna_ref[...] = (dna * o * (sig_zna * (1.0 + zna * (1.0 - sig_zna)))).astype(_BF16)
        dmw_l = []
        for hd in range(ML_HEADS):
            cols = slice(hd * 128, (hd + 1) * 128)
            dml = dmix[:, NA_W + hd * 128:NA_W + (hd + 1) * 128]
            hn = hn_l[hd]
            mwv = mw_ref[:, cols]
            dmz_ref[:, cols] = (dml * hn * mwv * dsilu_mz[:, cols]).astype(_BF16)
            dhn = dml * mwv * smz[:, cols]
            dmw_l.append(jnp.sum(dml * hn * smz[:, cols], axis=0, keepdims=True))
            dhm = rstd_l[hd] * (dhn - jnp.mean(dhn, axis=-1, keepdims=True)
                                - hn * jnp.mean(dhn * hn, axis=-1, keepdims=True))
            sgc = sg[:, cols]
            dh_ref[:, cols] = dhm * sgc
            dmo_ref[:, cols] = (dhm * hsum[:, cols] * sgc * (1.0 - sgc)).astype(_BF16)
        dmw = jnp.concatenate(dmw_l + [jnp.zeros((1, D_MODEL - ML_W), _F32)], axis=1)
        lane = lax.broadcasted_iota(jnp.int32, (1, D_MODEL), 1)
        vec = jnp.concatenate([dfw, dgate, dmw, jnp.where(lane == 0, loss, 0.0),
                               jnp.zeros((4, D_MODEL), _F32)], axis=0)

        @pl.when(i == 0)
        def _():
            dwo_ref[...] = dwo
            vec_ref[...] = vec

        @pl.when(i > 0)
        def _():
            dwo_ref[...] += dwo
            vec_ref[...] += vec

    tok = lambda w, j: pl.BlockSpec((tm, w), lambda i: (i, j))
    tok3 = pl.BlockSpec((None, tm, D_MODEL), lambda i: (0, i, 0))
    row = lambda w: pl.BlockSpec((1, w), lambda i: (0, 0))
    f32 = lambda w: jax.ShapeDtypeStruct((T, w), _F32)
    bf16 = lambda w: jax.ShapeDtypeStruct((T, w), _BF16)
    return pl.pallas_call(
        body, name="post", grid=(n_i,),
        in_specs=[tok3, tok3, tok(NA_W, 0), tok(NA_W, 0), tok(ML_W, 0), tok(ML_W, 0),
                  tok(ML_W, 4), tok(ML_W, 5), row(D_MODEL), row(ML_W), row(D_MODEL),
                  pl.BlockSpec((D_MODEL, D_MODEL), lambda i: (0, 0))],
        out_specs=(tok(D_MODEL, 0), tok(NA_W, 0), tok(NA_W, 0), tok(ML_W, 0), tok(ML_W, 0), tok(ML_W, 0),
                   pl.BlockSpec((D_MODEL, D_MODEL), lambda i: (0, 0)),
                   pl.BlockSpec((8, D_MODEL), lambda i: (0, 0))),
        out_shape=(f32(D_MODEL), bf16(NA_W), bf16(NA_W), f32(ML_W), bf16(ML_W),
                   bf16(ML_W), jax.ShapeDtypeStruct((D_MODEL, D_MODEL), _F32),
                   jax.ShapeDtypeStruct((8, D_MODEL), _F32)),
        compiler_params=_cparams(("arbitrary",)),
    )(x, target, o_na, rest, h_f, h_b, rest, rest, gate, ml_norm_w, final_w, w_out_bf)


def _section_specs(sections, tm):
    specs, args = [], []
    for _, width, parts in sections:
        for arr, cb in parts:
            specs.append(pl.BlockSpec((tm, width), functools.partial(lambda i, cb: (i, cb), cb=cb)))
            args.append(arr)
    return specs, args


def _section_values(sections, refs, dtype):
    vals, at = [], 0
    for _, _, parts in sections:
        v = refs[at][...]
        for r in refs[at + 1:at + len(parts)]:
            v = v.astype(_F32) + r[...].astype(_F32)
        at += len(parts)
        vals.append(v.astype(dtype))
    return vals


def _inproj_bwd_x(x, dx1, scale1p, norm_w, w_in_bf, sections, T):
    tm = 512
    sspecs, sargs = _section_specs(sections, tm)
    ns = len(sargs)

    def body(*refs):
        x_ref, dx1_ref, sc_ref, nw_ref, w_ref = refs[:5]
        srefs = refs[5:5 + ns]
        gx_ref, vec_ref = refs[5 + ns:]
        i = pl.program_id(0)
        vals = _section_values(sections, srefs, _BF16)
        dh = jnp.zeros((tm, D_MODEL), _F32)
        for (c0, width, _), val in zip(sections, vals):
            dh = dh + _nt(val, w_ref[:, c0:c0 + width])
        xv = x_ref[...]
        r = lax.rsqrt(jnp.mean(xv * xv, axis=-1, keepdims=True) + EPS)
        xhat = xv * r
        nw = nw_ref[...]
        dshift = jnp.sum(dh, axis=0, keepdims=True)
        dscale = jnp.sum(dh * xhat * nw, axis=0, keepdims=True)
        dhpre = dh * sc_ref[...]
        dnw = jnp.sum(dhpre * xhat, axis=0, keepdims=True)
        dxhat = dhpre * nw
        gx_ref[...] = dx1_ref[...] + r * (dxhat - xhat * jnp.mean(dxhat * xhat, axis=-1, keepdims=True))
        vec = jnp.concatenate([dshift, dscale, dnw, jnp.zeros((5, D_MODEL), _F32)], axis=0)

        @pl.when(i == 0)
        def _():
            vec_ref[...] = vec

        @pl.when(i > 0)
        def _():
            vec_ref[...] += vec

    row = pl.BlockSpec((1, D_MODEL), lambda i: (0, 0))
    tok = pl.BlockSpec((tm, D_MODEL), lambda i: (i, 0))
    tok3 = pl.BlockSpec((None, tm, D_MODEL), lambda i: (0, i, 0))
    return pl.pallas_call(
        body, name="inproj_bwd_x", grid=(T // tm,),
        in_specs=[tok3, tok, row, row,
                  pl.BlockSpec((D_MODEL, IN_PAD), lambda i: (0, 0), pipeline_mode=pl.Buffered(1))] + sspecs,
        out_specs=(tok3, pl.BlockSpec((8, D_MODEL), lambda i: (0, 0))),
        out_shape=(jax.ShapeDtypeStruct((1, T, D_MODEL), _F32), jax.ShapeDtypeStruct((8, D_MODEL), _F32)),
        compiler_params=_cparams(("arbitrary",)),
    )(x, dx1, scale1p, norm_w, w_in_bf, *sargs)


def _inproj_bwd_w(h_t, sections, T):
    tm = 1024
    n_i = T // tm
    sspecs, sargs = _section_specs(sections, tm)
    ns = len(sargs)

    def body(*refs):
        h_ref = refs[0]
        srefs = refs[1:1 + ns]
        dw_ref, db_ref, acc, sem = refs[1 + ns:]
        i = pl.program_id(0)

        @pl.when(i == 0)
        def _():
            acc[...] = jnp.zeros_like(acc)
            db_ref[...] = jnp.zeros_like(db_ref)

        hv = h_ref[...]
        for (c0, width, _), v in zip(sections, _section_values(sections, srefs, _F32)):
            acc[:, c0:c0 + width] += _nn(hv, v.astype(_BF16))
            db_ref[0:1, c0:c0 + width] += jnp.sum(v, axis=0, keepdims=True)

        @pl.when(i == n_i - 1)
        def _():
            cp = pltpu.make_async_copy(acc, dw_ref, sem)
            cp.start()
            cp.wait()

    return pl.pallas_call(
        body, name="inproj_bwd_w", grid=(n_i,),
        in_specs=[pl.BlockSpec((D_MODEL, tm), lambda i: (0, i))] + sspecs,
        out_specs=(pl.BlockSpec(memory_space=pl.ANY), pl.BlockSpec((8, IN_PAD), lambda i: (0, 0))),
        out_shape=(jax.ShapeDtypeStruct((D_MODEL, IN_PAD), _F32), jax.ShapeDtypeStruct((8, IN_PAD), _F32)),
        scratch_shapes=[pltpu.VMEM((D_MODEL, IN_PAD), _F32), pltpu.SemaphoreType.DMA],
        compiler_params=_cparams(("arbitrary",)),
    )(h_t, *sargs)


def _adamw_math(w, g, m, v):
    m = ADAM_B1 * m + (1.0 - ADAM_B1) * g
    v = ADAM_B2 * v + (1.0 - ADAM_B2) * jnp.square(g)
    m_hat = m / (1.0 - ADAM_B1 ** ADAM_STEP)
    v_hat = v / (1.0 - ADAM_B2 ** ADAM_STEP)
    delta = -ADAM_LR * (m_hat / (jnp.sqrt(v_hat) + ADAM_EPS) + ADAM_WD * w)
    return delta, m, v


def _adamw_slots(w, m, v, slots, tr, name, own=None):
    R, C = w.shape
    extra = [] if own is None else [own]

    def body(w_ref, m_ref, v_ref, s_ref, *refs):
        g_ref, d_ref, nm_ref, nv_ref = refs[len(extra):]
        g = s_ref[0].astype(_F32)
        for k in range(1, N_DEV):
            g = g + s_ref[k].astype(_F32)
        if extra:
            g = g + refs[0][...].astype(_F32)
        g_ref[...] = g
        d_ref[...], nm_ref[...], nv_ref[...] = _adamw_math(w_ref[...], g, m_ref[...], v_ref[...])

    blk = pl.BlockSpec((tr, C), lambda i: (i, 0))
    return pl.pallas_call(
        body, name=name, grid=(R // tr,),
        in_specs=[blk, blk, blk, pl.BlockSpec((N_DEV, tr, C), lambda i: (0, i, 0))] + [blk] * len(extra),
        out_specs=(blk, blk, blk, blk),
        out_shape=tuple(jax.ShapeDtypeStruct((R, C), _F32) for _ in range(4)),
        compiler_params=_cparams(("arbitrary",)),
    )(w, m, v, slots, *extra)


def _w_ada_update(c_all, dmod_my, w, m, v):
    def body(c_ref, d_ref, w_ref, m_ref, v_ref, g_ref, dl_ref, nm_ref, nv_ref):
        g = lax.dot_general(_silu(c_ref[...]), d_ref[...], (((0,), (0,)), ((), ())),
                            precision=_HI, preferred_element_type=_F32)
        g_ref[...] = g
        dl_ref[...], nm_ref[...], nv_ref[...] = _adamw_math(w_ref[...], g, m_ref[...], v_ref[...])

    return pl.pallas_call(
        body, name="w_ada_update",
        out_shape=tuple(jax.ShapeDtypeStruct(w.shape, _F32) for _ in range(4)),
        compiler_params=_cparams(),
    )(c_all, dmod_my, w, m, v)


_PACK = (("b_ada", 3072, 3072), ("norm_w", 1024, 1024), ("b_in", IN_W, IN_PAD), ("conv_w", 5120, 5120),
         ("conv_b", 1024, 1024), ("rpb", 3720, 3840), ("ml_norm_w", 512, 512), ("final_norm_w", 1024, 1024),
         ("loss", 1, 128))
_PACK_OFF = {}
_off = 0
for _name, _len, _pad in _PACK:
    _PACK_OFF[_name] = (_off, _len)
    _off += _pad
_PACK_LEN = _off


def _pack(parts):
    cols = []
    for name, length, pad in _PACK:
        vec = parts[name].reshape(-1).astype(_F32)
        cols.append(jnp.pad(vec, (0, pad - length)))
    return jnp.concatenate(cols).reshape(1, _PACK_LEN)


def _unpack(vec, name, shape):
    off, length = _PACK_OFF[name]
    return vec.reshape(-1)[off:off + length].reshape(shape)


def kernel(x, c, w_ada, b_ada, norm_w, w_in, b_in, conv_w, conv_b, rpb, ml_norm_w, w_out, final_norm_w, loss_target, m_w_ada, m_b_ada, m_norm_w, m_w_in, m_b_in, m_conv_w, m_conv_b, m_rpb, m_ml_norm_w, m_w_out, m_final_norm_w, v_w_ada, v_b_ada, v_norm_w, v_w_in, v_b_in, v_conv_w, v_conv_b, v_rpb, v_ml_norm_w, v_w_out, v_final_norm_w):
    T = x.shape[1]
    rows = T // GRID_W
    me = 4 * lax.axis_index("x") + 2 * lax.axis_index("y") + lax.axis_index("c")
    n_in = w_in.shape[2]
    n_ada = w_ada.shape[2]
    n_cw = conv_w.shape[2]
    n_wo = w_out.shape[1]

    w_in_my, w_out_my = w_in[0].astype(_BF16), w_out[0].astype(_BF16)
    first_leg = (1,) + _CHIP_PEERS
    start_in = _scatter_start([w_in_my], "w_in_start", scatter=False, ks=first_leg)
    start_out = _scatter_start([w_out_my], "w_out_start", scatter=False)
    tokens = start_in[-1][0:1, 0:1] + start_out[-1][0:1, 0:1]
    g_conv_w, g_c = _exchange([conv_w[0], c + tokens], [False] * 2, "gather_small")
    b_in_pad = jnp.pad(b_in, ((0, 0), (0, IN_PAD - IN_W)))
    conv_w_full = jnp.pad(g_conv_w.transpose(1, 0, 2).reshape(CONV_W, N_DEV * n_cw), ((0, 3), (0, 0)))
    c_all = g_c.reshape(N_DEV, D_MODEL)

    b_ada_my = lax.dynamic_slice(b_ada, (0, me * n_ada), (1, n_ada))
    (mod_slots,) = _exchange([_mod_part(c_all, w_ada[0], b_ada_my)], [False], "gather_mod")
    mod = lax.dynamic_index_in_dim(mod_slots, me, axis=1, keepdims=False).reshape(1, 3 * D_MODEL)
    shift, scale, gate = mod[:, :D_MODEL], mod[:, D_MODEL:2 * D_MODEL], mod[:, 2 * D_MODEL:]
    scale1p = 1.0 + scale
    bias = _na_bias_tables(rpb[0], rows)

    def own_slot(land, own):
        return lax.dynamic_update_slice(land, own[None], (me,) + (0,) * own.ndim)

    def gathered(started, after, name):
        (own,), (land,) = _scatter_wait(started, after, name, scatter=False)
        return own_slot(land, own)

    (w_in_own,), (w_in_land,) = _scatter_wait(start_in, bias[0, 0, :8, :128] + scale1p[:, :128], "w_in_wait",
                                              scatter=False, ks=first_leg)
    g_w_in = own_slot(_relay_wait(_relay_start(w_in_land, "w_in_relay_start"), "w_in_relay_wait"), w_in_own)
    w_in_full = g_w_in.transpose(1, 0, 2).reshape(D_MODEL, N_DEV * n_in)
    w_in_bf = jnp.pad(w_in_full, ((0, 0), (0, IN_PAD - IN_W)))

    qkv, rest, h_bf = _inproj_fwd(x, scale1p, shift, norm_w, w_in_bf, b_in_pad)
    o_na, lse = _na_fwd(qkv, bias, T)
    qk_act, qk_pre = _conv_fwd(rest, conv_w_full, conv_b, T)
    h_f, *saved_f = _mlstm_fwd(qk_act, rest, T, False)
    h_b, *saved_b = _mlstm_fwd(qk_act, rest, T, True)

    w_out_bf = gathered(start_out, saved_b[2], "w_out_wait").reshape(N_DEV * n_wo, D_MODEL)
    dx1, d_o, dz_na, d_h, d_mo, d_mz, dwo, pvec = _post(
        x, loss_target, o_na, rest, h_f, h_b, gate, ml_norm_w, final_norm_w.reshape(1, D_MODEL), w_out_bf, T)

    dq_na, dk_na, dv_na, dbias = _na_bwd(qkv, bias, o_na, d_o, lse, T)
    d_rpb = _rpb_grad(dbias, rows)
    dqk_f, dv_f, dg_f = _mlstm_bwd(qk_act, rest, d_h, saved_f, T, False)
    dqk_b, dv_b, dg_b = _mlstm_bwd(qk_act, rest, d_h, saved_b, T, True)
    d_u, dconv = _conv_bwd(rest, qk_pre, conv_w_full, dqk_f, dqk_b, T)

    sections = [(0, 512, [(dq_na, 0)]), (512, 512, [(dk_na, 0)]), (1024, 512, [(dv_na, 0)]),
                (1536, 512, [(dz_na, 0)]), (2048, 512, [(d_u, 0)]), (2560, 512, [(d_u, 1)]),
                (3072, 512, [(dv_f, 0), (dv_b, 0)]), (3584, 512, [(d_mo, 0)]), (4096, 512, [(d_mz, 0)]),
                (4608, 128, [(dg_f, 0), (dg_b, 0)])]
    dw_pad, db_pad = _inproj_bwd_w(h_bf, sections, T)
    db_in = db_pad[0, :IN_W]

    dw_blocks = dw_pad[:, :IN_W].astype(_BF16).reshape(D_MODEL, N_DEV, n_in).transpose(1, 0, 2)
    dwo_blocks = dwo.astype(_BF16).reshape(N_DEV, n_wo, D_MODEL)
    started = _scatter_start([dw_blocks, dwo_blocks], "grads_start")
    grad_x, xvec = _inproj_bwd_x(x, dx1, scale1p + started[-1][0:1, 0:1], norm_w, w_in_bf, sections, T)
    (dw_blocks, dwo_blocks), (s_w_in, s_w_out) = _scatter_wait(started, xvec, "grads_wait")

    small = _pack({
        "b_ada": jnp.concatenate([xvec[0], xvec[1], pvec[1]]),
        "norm_w": xvec[2], "b_in": db_in, "conv_w": dconv[:CONV_W], "conv_b": dconv[CONV_W],
        "rpb": d_rpb, "ml_norm_w": pvec[2, :ML_W], "final_norm_w": pvec[0], "loss": pvec[3, :1]})
    (s_small,) = _exchange([small], [False], "exchange_small")

    own = lambda blocks: lax.dynamic_index_in_dim(blocks, me, axis=0, keepdims=False)
    g_w_in_s, d_w_in, nm_w_in, nv_w_in = _adamw_slots(
        w_in[0], m_w_in[0], v_w_in[0], s_w_in, 128, "adamw_w_in", own=own(dw_blocks))
    g_w_out_s, d_w_out, nm_w_out, nv_w_out = _adamw_slots(
        w_out[0], m_w_out[0], v_w_out[0], s_w_out, n_wo, "adamw_w_out", own=own(dwo_blocks))
    dmod_all = s_small[:, 0, :3 * D_MODEL]
    dmod_my = lax.dynamic_slice(dmod_all, (0, me * n_ada), (N_DEV, n_ada))
    g_w_ada, d_w_ada, nm_w_ada, nv_w_ada = _w_ada_update(c_all, dmod_my, w_ada[0], m_w_ada[0], v_w_ada[0])

    def embed(shard):
        return lax.dynamic_update_slice(jnp.zeros((CONV_W, N_DEV * n_cw), _F32), shard[0], (0, me * n_cw))

    zero1 = jnp.zeros((1,), _F32)
    packed = lambda b_a, n_w, b_i, c_w, c_b, rp, mn, fn: _pack({
        "b_ada": b_a, "norm_w": n_w, "b_in": b_i, "conv_w": embed(c_w), "conv_b": c_b, "rpb": rp,
        "ml_norm_w": mn, "final_norm_w": fn, "loss": zero1})
    pw = packed(b_ada, norm_w, b_in, conv_w, conv_b, rpb, ml_norm_w, final_norm_w)
    pm = packed(m_b_ada, m_norm_w, m_b_in, m_conv_w, m_conv_b, m_rpb, m_ml_norm_w, m_final_norm_w)
    pv = packed(v_b_ada, v_norm_w, v_b_in, v_conv_w, v_conv_b, v_rpb, v_ml_norm_w, v_final_norm_w)
    sg, sd, sm, sv = _adamw_slots(pw, pm, pv, s_small, 1, "adamw_small")

    def small_outs(vec):
        cw = lax.dynamic_slice(_unpack(vec, "conv_w", (CONV_W, N_DEV * n_cw)), (0, me * n_cw), (CONV_W, n_cw))
        return dict(b_ada=_unpack(vec, "b_ada", b_ada.shape), norm_w=_unpack(vec, "norm_w", norm_w.shape),
                    b_in=_unpack(vec, "b_in", b_in.shape), conv_w=cw[None],
                    conv_b=_unpack(vec, "conv_b", conv_b.shape), rpb=_unpack(vec, "rpb", rpb.shape),
                    ml_norm_w=_unpack(vec, "ml_norm_w", ml_norm_w.shape),
                    final_norm_w=_unpack(vec, "final_norm_w", final_norm_w.shape))

    loss = _unpack(sg, "loss", ())
    order = ("w_ada", "b_ada", "norm_w", "w_in", "b_in", "conv_w", "conv_b", "rpb", "ml_norm_w", "w_out",
             "final_norm_w")
    outs = []
    for vec, big in ((sg, (g_w_ada, g_w_in_s, g_w_out_s)), (sd, (d_w_ada, d_w_in, d_w_out)),
                     (sm, (nm_w_ada, nm_w_in, nm_w_out)), (sv, (nv_w_ada, nv_w_in, nv_w_out))):
        group = small_outs(vec)
        group.update(w_ada=big[0][None], w_in=big[1][None], w_out=big[2][None])
        outs.extend(group[name] for name in order)
    return (loss, grad_x, *outs)
```

```python
import functools

import numpy as np
import jax
import jax.numpy as jnp
from jax import lax
from jax.experimental import pallas as pl
from jax.experimental.pallas import tpu as pltpu

N_DEV = 8
D_MODEL = 1024
GRID_W = 64
NA_HEADS = 8
NA_HEAD_DIM = 64
NA_KH = 8
NA_KW = 16
NA_W = 512
ML_HEADS = 4
ML_HEAD_DIM = 128
ML_W = 512
ML_CHUNK = 512
CONV_W = 5
EPS = 1e-6
IN_W = 4624
IN_PAD = 4736
REST_W = IN_PAD - 3 * NA_W
GATE_COL = 3072
NEG = -1e30
NA_RB = 4
NA_WIN = 12
NA_SUB = 8
ML_CB = 1
ADAM_LR = 0.001
ADAM_B1 = 0.9
ADAM_B2 = 0.999
ADAM_EPS = 1e-08
ADAM_WD = 0.01
ADAM_STEP = 10
VMEM_LIMIT = 56 * 1024 * 1024

_F32 = jnp.float32
_BF16 = jnp.bfloat16
_HI = lax.Precision.HIGHEST


def _cparams(sem=None):
    return pltpu.CompilerParams(dimension_semantics=sem, vmem_limit_bytes=VMEM_LIMIT)


def _nt(a, b):
    return lax.dot_general(a, b, (((1,), (1,)), ((), ())), preferred_element_type=_F32)


def _tn(a, b):
    return lax.dot_general(a, b, (((0,), (0,)), ((), ())), preferred_element_type=_F32)


def _nn(a, b):
    return jnp.dot(a, b, preferred_element_type=_F32)


def _sigmoid(x):
    return 1.0 / (1.0 + jnp.exp(-x))


def _silu(x):
    return x * _sigmoid(x)


def _dsilu(x):
    s = _sigmoid(x)
    return s * (1.0 + x * (1.0 - s))


def _exchange(arrs, scatter, name):
    n = len(arrs)
    out_shape = []
    for a, sc in zip(arrs, scatter):
        blk = a.shape[1:] if sc else a.shape
        out_shape.append(jax.ShapeDtypeStruct((N_DEV,) + tuple(blk), a.dtype))

    def body(*refs):
        ins = refs[:n]
        outs = refs[n:2 * n]
        send_sems, recv_sems, local_sems = refs[2 * n:]
        x, y, c = lax.axis_index("x"), lax.axis_index("y"), lax.axis_index("c")
        me = 4 * x + 2 * y + c
        local, sends, recvs = [], [], []
        for a in range(n):
            own = ins[a].at[me] if scatter[a] else ins[a]
            cp = pltpu.make_async_copy(own, outs[a].at[me], local_sems.at[a])
            cp.start()
            local.append(cp)
            for k in range(1, N_DEV):
                px = 1 - x if k & 4 else x
                py = 1 - y if k & 2 else y
                pc = 1 - c if k & 1 else c
                p = 4 * px + 2 * py + pc
                src = ins[a].at[p] if scatter[a] else ins[a]
                snd = pltpu.make_async_remote_copy(
                    src_ref=src, dst_ref=outs[a].at[me],
                    send_sem=send_sems.at[a, k - 1], recv_sem=recv_sems.at[a, k - 1],
                    device_id=(px, py, pc), device_id_type=pl.DeviceIdType.MESH)
                snd.start()
                sends.append(snd)
                rcv = pltpu.make_async_remote_copy(
                    src_ref=src, dst_ref=outs[a].at[p],
                    send_sem=send_sems.at[a, k - 1], recv_sem=recv_sems.at[a, k - 1],
                    device_id=(px, py, pc), device_id_type=pl.DeviceIdType.MESH)
                recvs.append(rcv)
        for rcv in recvs:
            rcv.wait_recv()
        for snd in sends:
            snd.wait_send()
        for cp in local:
            cp.wait()

    any_spec = pl.BlockSpec(memory_space=pl.ANY)
    res = pl.pallas_call(
        body, name=name, out_shape=tuple(out_shape),
        in_specs=[any_spec] * n, out_specs=tuple([any_spec] * n),
        scratch_shapes=[pltpu.SemaphoreType.DMA((n, N_DEV - 1)),
                        pltpu.SemaphoreType.DMA((n, N_DEV - 1)),
                        pltpu.SemaphoreType.DMA((n,))],
    )(*arrs)
    return list(res)


def _peer(k):
    x, y, c = lax.axis_index("x"), lax.axis_index("y"), lax.axis_index("c")
    px = 1 - x if k & 4 else x
    py = 1 - y if k & 2 else y
    pc = 1 - c if k & 1 else c
    return (px, py, pc), 4 * px + 2 * py + pc, 4 * x + 2 * y + c


_ALL_PEERS = tuple(range(1, N_DEV))
_CHIP_PEERS = (2, 4, 6)


def _scatter_copy(srcs, lands, send_sems, recv_sems, a, k, receive, scatter, ks=_ALL_PEERS):
    dev, p, me = _peer(k)
    at = a * len(ks) + ks.index(k)
    return pltpu.make_async_remote_copy(
        src_ref=srcs[a].at[p] if scatter else srcs[a], dst_ref=lands[a].at[p if receive else me],
        send_sem=send_sems[at], recv_sem=recv_sems[at],
        device_id=dev, device_id_type=pl.DeviceIdType.MESH)


def _scatter_start(arrs, name, scatter=True, ks=_ALL_PEERS):
    n = len(arrs)
    ns = n * len(ks)
    hbm = pl.BlockSpec(memory_space=pltpu.HBM)
    sem = pl.BlockSpec(memory_space=pltpu.SEMAPHORE)

    def body(*refs):
        srcs, lands = refs[:n], refs[n:2 * n]
        send_sems, recv_sems = refs[2 * n:2 * n + ns], refs[2 * n + ns:2 * n + 2 * ns]
        token = refs[-1]
        for a in range(n):
            for k in ks:
                _scatter_copy(srcs, lands, send_sems, recv_sems, a, k, False, scatter, ks).start()
        token[...] = jnp.zeros_like(token)

    land_shapes = [a.shape if scatter else (N_DEV,) + a.shape for a in arrs]
    buffers = [pltpu.HBM(a.shape, a.dtype) for a in arrs]
    land_buffers = [pltpu.HBM(s, a.dtype) for s, a in zip(land_shapes, arrs)]
    sems = [pltpu.SemaphoreType.DMA(()) for _ in range(2 * ns)]
    res = pl.pallas_call(
        body, name=name,
        out_shape=(*sems, *buffers, *land_buffers, jax.ShapeDtypeStruct((8, 128), _F32)),
        in_specs=[hbm] * (2 * n),
        out_specs=(*([sem] * (2 * ns)), *([hbm] * (2 * n)), pl.BlockSpec(memory_space=pltpu.VMEM)),
        input_output_aliases={i: 2 * ns + i for i in range(2 * n)},
        compiler_params=pltpu.CompilerParams(has_side_effects=pltpu.SideEffectType.DATAFLOW_SIDE_EFFECTING),
    )(*[pltpu.with_memory_space_constraint(a, pltpu.HBM) for a in arrs],
      *[pltpu.with_memory_space_constraint(jnp.zeros(s, a.dtype), pltpu.HBM) for s, a in zip(land_shapes, arrs)])
    res = list(res)
    return (res[:ns], res[ns:2 * ns], res[2 * ns:2 * ns + n], res[2 * ns + n:2 * ns + 2 * n], res[-1])


def _scatter_wait(started, after, name, scatter=True, ks=_ALL_PEERS):
    send_sems, recv_sems, srcs, lands, _ = started
    n = len(srcs)
    ns = len(send_sems)
    hbm = pl.BlockSpec(memory_space=pltpu.HBM)
    sem = pl.BlockSpec(memory_space=pltpu.SEMAPHORE)

    def body(*refs):
        src_refs, land_refs = refs[:n], refs[n:2 * n]
        s_sems, r_sems = refs[2 * n:2 * n + ns], refs[2 * n + ns:2 * n + 2 * ns]
        for a in range(n):
            for k in ks:
                _scatter_copy(src_refs, land_refs, s_sems, r_sems, a, k, False, scatter, ks).wait_send()
                _scatter_copy(src_refs, land_refs, s_sems, r_sems, a, k, True, scatter, ks).wait_recv()

    buffers = [pltpu.HBM(a.shape, a.dtype) for a in list(srcs) + list(lands)]
    res = pl.pallas_call(
        body, name=name, out_shape=tuple(buffers),
        in_specs=[hbm] * (2 * n) + [sem] * (2 * ns) + [pl.BlockSpec(memory_space=pl.ANY)],
        out_specs=tuple([hbm] * (2 * n)),
        input_output_aliases={i: i for i in range(2 * n)},
        compiler_params=pltpu.CompilerParams(has_side_effects=pltpu.SideEffectType.DATAFLOW_SIDE_EFFECTING),
    )(*srcs, *lands, *send_sems, *recv_sems, after)
    return list(res[:n]), list(res[n:])


def _relay_copy(land, send_sems, recv_sems, j, receive):
    k = _CHIP_PEERS[j]
    sibling, _, _ = _peer(1)
    _, slot, _ = _peer(k | 1 if receive else k)
    return pltpu.make_async_remote_copy(
        src_ref=land.at[slot], dst_ref=land.at[slot], send_sem=send_sems[j], recv_sem=recv_sems[j],
        device_id=sibling, device_id_type=pl.DeviceIdType.MESH)


def _relay_start(land, name):
    ns = len(_CHIP_PEERS)
    hbm = pl.BlockSpec(memory_space=pltpu.HBM)
    sem = pl.BlockSpec(memory_space=pltpu.SEMAPHORE)

    def body(*refs):
        land_ref = refs[0]
        send_sems, recv_sems = refs[1:1 + ns], refs[1 + ns:1 + 2 * ns]
        for j in range(ns):
            _relay_copy(land_ref, send_sems, recv_sems, j, False).start()
        refs[-1][...] = jnp.zeros_like(refs[-1])

    sems = [pltpu.SemaphoreType.DMA(()) for _ in range(2 * ns)]
    res = pl.pallas_call(
        body, name=name,
        out_shape=(*sems, pltpu.HBM(land.shape, land.dtype), jax.ShapeDtypeStruct((8, 128), _F32)),
        in_specs=[hbm],
        out_specs=(*([sem] * (2 * ns)), hbm, pl.BlockSpec(memory_space=pltpu.VMEM)),
        input_output_aliases={0: 2 * ns},
        compiler_params=pltpu.CompilerParams(has_side_effects=pltpu.SideEffectType.DATAFLOW_SIDE_EFFECTING),
    )(land)
    res = list(res)
    return res[:ns], res[ns:2 * ns], res[2 * ns], res[-1]


def _relay_wait(started, name):
    send_sems, recv_sems, land, token = started
    ns = len(send_sems)
    hbm = pl.BlockSpec(memory_space=pltpu.HBM)
    sem = pl.BlockSpec(memory_space=pltpu.SEMAPHORE)

    def body(*refs):
        land_ref = refs[0]
        s_sems, r_sems = refs[1:1 + ns], refs[1 + ns:1 + 2 * ns]
        for j in range(ns):
            _relay_copy(land_ref, s_sems, r_sems, j, False).wait_send()
            _relay_copy(land_ref, s_sems, r_sems, j, True).wait_recv()

    return pl.pallas_call(
        body, name=name, out_shape=pltpu.HBM(land.shape, land.dtype),
        in_specs=[hbm] + [sem] * (2 * ns) + [pl.BlockSpec(memory_space=pl.ANY)],
        out_specs=hbm, input_output_aliases={0: 0},
        compiler_params=pltpu.CompilerParams(has_side_effects=pltpu.SideEffectType.DATAFLOW_SIDE_EFFECTING),
    )(land, *send_sems, *recv_sems, token)


def _mod_part(c_all, w_ada, b_my):
    def body(c_ref, w_ref, b_ref, o_ref):
        o_ref[...] = jnp.dot(_silu(c_ref[...]), w_ref[...], precision=_HI,
                             preferred_element_type=_F32) + b_ref[...]

    return pl.pallas_call(
        body, name="mod_part",
        out_shape=jax.ShapeDtypeStruct((N_DEV, w_ada.shape[1]), _F32),
        compiler_params=_cparams(),
    )(c_all, w_ada, b_my)


def _inproj_fwd(x, scale1p, shift, norm_w, w_in_bf, b_in_pad):
    T = x.shape[1]
    tm = 512
    n_q = 3 * NA_W

    def body(x_ref, sc_ref, sh_ref, nw_ref, w_ref, b_ref, qkv_ref, rest_ref, h_ref):
        xv = x_ref[...]
        r = lax.rsqrt(jnp.mean(xv * xv, axis=-1, keepdims=True) + EPS)
        h = xv * r * nw_ref[...] * sc_ref[...] + sh_ref[...]
        hb = h.astype(_BF16)
        h_ref[...] = h.T.astype(_BF16)
        for n0 in range(0, IN_PAD, 512):
            wd = min(512, IN_PAD - n0)
            acc = _nn(hb, w_ref[:, n0:n0 + wd]) + b_ref[:, n0:n0 + wd]
            if n0 == 0:
                acc = acc * (NA_HEAD_DIM ** -0.5)
            if n0 < n_q:
                qkv_ref[:, n0:n0 + wd] = acc.astype(_BF16)
            else:
                rest_ref[:, n0 - n_q:n0 - n_q + wd] = acc

    row = lambda w: pl.BlockSpec((1, w), lambda i: (0, 0))
    return pl.pallas_call(
        body, name="inproj_fwd", grid=(T // tm,),
        in_specs=[pl.BlockSpec((None, tm, D_MODEL), lambda i: (0, i, 0)), row(D_MODEL), row(D_MODEL), row(D_MODEL),
                  pl.BlockSpec((D_MODEL, IN_PAD), lambda i: (0, 0), pipeline_mode=pl.Buffered(1)), row(IN_PAD)],
        out_specs=(pl.BlockSpec((tm, n_q), lambda i: (i, 0)),
                   pl.BlockSpec((tm, REST_W), lambda i: (i, 0)),
                   pl.BlockSpec((D_MODEL, tm), lambda i: (0, i))),
        out_shape=(jax.ShapeDtypeStruct((T, n_q), _BF16),
                   jax.ShapeDtypeStruct((T, REST_W), _F32),
                   jax.ShapeDtypeStruct((D_MODEL, T), _BF16)),
        compiler_params=_cparams(("arbitrary",)),
    )(x, scale1p, shift, norm_w, w_in_bf, b_in_pad)


def _na_class_rows(rows):
    nb = rows // NA_RB
    out = []
    for rb in (0, min(1, nb - 1), nb - 1):
        ws = int(np.clip(NA_RB * rb - 4, 0, rows - NA_WIN))
        out.append((NA_RB * rb + np.arange(NA_RB), ws + np.arange(NA_WIN)))
    return out


def _na_pair_index(rows, qrows, krows):
    start = lambda r: np.clip(r - NA_KH // 2, 0, rows - NA_KH)
    col = np.arange(GRID_W)
    cstart = np.clip(col - NA_KW // 2, 0, GRID_W - NA_KW)
    dy = krows[None, :] - qrows[:, None] + NA_KH - 1
    vr = (krows[None, :] >= start(qrows)[:, None]) & (krows[None, :] < start(qrows)[:, None] + NA_KH)
    dx = np.clip(col[None, :] - col[:, None], -(NA_KW - 1), NA_KW - 1) + NA_KW - 1
    vc = (col[None, :] >= cstart[:, None]) & (col[None, :] < cstart[:, None] + NA_KW)
    nq, nk = len(qrows), len(krows)
    dy4 = np.broadcast_to(np.clip(dy, 0, 2 * NA_KH - 2)[:, None, :, None], (nq, GRID_W, nk, GRID_W))
    dx4 = np.broadcast_to(dx[None, :, None, :], (nq, GRID_W, nk, GRID_W))
    valid = vr[:, None, :, None] & vc[None, :, None, :]
    idx = (dy4 * (2 * NA_KW - 1) + dx4).reshape(nq * GRID_W, nk * GRID_W)
    return idx.astype(np.int32), valid.reshape(nq * GRID_W, nk * GRID_W), (dy, vr, dx, vc)


def _na_half_slabs(rpb):
    _, _, (_, _, dx, vc) = _na_pair_index(NA_WIN, np.arange(1), np.arange(1))
    qc, kc = np.meshgrid(np.arange(GRID_W), np.arange(GRID_W), indexing="ij")
    consts = []
    for right in (False, True):
        pos = (qc * 128 + (GRID_W if right else 0) + kc).reshape(-1)
        oh = np.zeros((32, GRID_W * 128), np.float32)
        oh[dx[qc, kc].reshape(-1), pos] = 1.0
        col_neg = np.zeros((1, GRID_W * 128), np.float32)
        col_neg[0, pos] = np.where(vc[qc, kc].reshape(-1), 0.0, NEG)
        half = np.zeros((1, GRID_W * 128), np.float32)
        half[0, pos] = 1.0
        consts += [jnp.asarray(oh), jnp.asarray(col_neg), jnp.asarray(half)]
    row_neg = np.where(np.arange(NA_HEADS * 16) % 16 == 15, NEG, 0.0).astype(np.float32).reshape(-1, 1)
    rp = jnp.pad(rpb, ((0, 0), (0, 1), (0, 1))).reshape(NA_HEADS * 16, 32)

    def body(*refs):
        r_ref, rn_ref = refs[0], refs[1]
        for t in range(2):
            oh_ref, cn_ref, half_ref = refs[2 + 3 * t:5 + 3 * t]
            refs[8 + t][...] = (jnp.dot(r_ref[...], oh_ref[...], precision=_HI, preferred_element_type=_F32)
                                + cn_ref[...] + rn_ref[...] * half_ref[...])

    outs = pl.pallas_call(
        body, name="na_half_slabs",
        out_shape=tuple(jax.ShapeDtypeStruct((NA_HEADS * 16, GRID_W * 128), _F32) for _ in range(2)),
        compiler_params=_cparams(),
    )(rp, jnp.asarray(row_neg), *consts)
    return [o.reshape(NA_HEADS, 16, GRID_W, 128) for o in outs]


def _na_bias_tables(rpb, rows):
    left, right = _na_half_slabs(rpb)
    didx = []
    for blk, win in _na_class_rows(rows):
        _, _, (dy, vr, _, _) = _na_pair_index(rows, blk, win)
        didx.append(np.where(vr, dy, 15))

    def body(l_ref, r_ref, b_ref):
        for ci, tab in enumerate(didx):
            for a in range(NA_RB):
                for j in range(NA_WIN // 2):
                    b_ref[ci, 0, a * GRID_W:(a + 1) * GRID_W, j * 128:(j + 1) * 128] = (
                        l_ref[0, int(tab[a, 2 * j])] + r_ref[0, int(tab[a, 2 * j + 1])])

    slab = pl.BlockSpec((1, 16, GRID_W, 128), lambda h: (h, 0, 0, 0))
    return pl.pallas_call(
        body, name="na_tables", grid=(NA_HEADS,),
        in_specs=[slab] * 2,
        out_specs=pl.BlockSpec((3, 1, NA_RB * GRID_W, NA_WIN * GRID_W), lambda h: (0, h, 0, 0)),
        out_shape=jax.ShapeDtypeStruct((3, NA_HEADS, NA_RB * GRID_W, NA_WIN * GRID_W), _F32),
        compiler_params=_cparams(("arbitrary",)),
    )(left, right)


def _stack_heads(x, first):
    zero = jnp.zeros_like(x)
    return jnp.concatenate([jnp.where(first, x, zero), jnp.where(first, zero, x)], axis=0)


def _na_sub(rb, u, rows):
    sb = NA_SUB * rb + u
    nb = rows // NA_RB
    cls = jnp.where(sb == 0, 0, jnp.where(sb == nb - 1, 2, 1))
    ws = pl.multiple_of(jnp.clip(NA_RB * sb - 4, 0, rows - NA_WIN) * GRID_W, 256)
    return cls, ws


def _na_fwd(qkv, bias, T):
    rows = T // GRID_W
    tq = NA_RB * GRID_W
    tw = NA_WIN * GRID_W
    ts = NA_SUB * tq

    def body(q_ref, k_ref, v_ref, b_ref, o_ref, l_ref):
        rb = pl.program_id(1)
        lane = lax.broadcasted_iota(jnp.int32, (1, 128), 1)
        first = lane < NA_HEAD_DIM
        for u in range(NA_SUB):
            cls, ws = _na_sub(rb, u, rows)
            kw = k_ref[pl.ds(ws, tw), :]
            vw = v_ref[pl.ds(ws, tw), :]
            q2 = _stack_heads(q_ref[u * tq:(u + 1) * tq, :], first)
            s = _nt(q2, kw) + b_ref[cls].reshape(2 * tq, tw)
            m = jnp.max(s, axis=1, keepdims=True)
            p = jnp.exp(s - m)
            l = jnp.sum(p, axis=1, keepdims=True)
            o2 = _nn(p.astype(_BF16), vw) / l
            lse2 = m + jnp.log(l)
            o_ref[u * tq:(u + 1) * tq, :] = jnp.where(first, o2[:tq], o2[tq:])
            l_ref[u * tq:(u + 1) * tq, :] = jnp.where(first, lse2[:tq], lse2[tq:])

    blk = lambda off: pl.BlockSpec((ts, 128), lambda hp, rb: (rb, off + hp))
    whole = lambda off: pl.BlockSpec((T, 128), lambda hp, rb: (0, off + hp))
    return pl.pallas_call(
        body, name="na_fwd", grid=(NA_HEADS // 2, T // ts),
        in_specs=[blk(0), whole(4), whole(8),
                  pl.BlockSpec((3, 2, tq, tw), lambda hp, rb: (0, hp, 0, 0))],
        out_specs=(blk(0), blk(0)),
        out_shape=(jax.ShapeDtypeStruct((T, NA_W), _F32), jax.ShapeDtypeStruct((T, NA_W), _F32)),
        compiler_params=_cparams(("arbitrary", "arbitrary")),
    )(qkv, qkv, qkv, bias)


def _na_bwd(qkv, bias, o, d_o, lse, T):
    rows = T // GRID_W
    tq = NA_RB * GRID_W
    tw = NA_WIN * GRID_W
    ts = NA_SUB * tq

    def body(q_ref, k_ref, v_ref, b_ref, o_ref, do_ref, l_ref, dq_ref, dk_ref, dv_ref, db_ref):
        rb = pl.program_id(1)
        lane = lax.broadcasted_iota(jnp.int32, (1, 128), 1)
        first = lane < NA_HEAD_DIM

        @pl.when(rb == 0)
        def _():
            db_ref[...] = jnp.zeros_like(db_ref)
            dk_ref[...] = jnp.zeros_like(dk_ref)
            dv_ref[...] = jnp.zeros_like(dv_ref)

        for u in range(NA_SUB):
            cls, ws = _na_sub(rb, u, rows)
            kw = k_ref[pl.ds(ws, tw), :]
            vw = v_ref[pl.ds(ws, tw), :]
            sl = slice(u * tq, (u + 1) * tq)
            q = q_ref[sl, :]
            d_ov = do_ref[sl, :]
            prod = d_ov.astype(_F32) * o_ref[sl, :]
            lse_v = l_ref[sl, :]
            dqs = []
            dk_win = jnp.zeros((tw, 128), _F32)
            dv_win = jnp.zeros((tw, 128), _F32)
            for hh in range(2):
                msk = first if hh == 0 else jnp.logical_not(first)
                c0 = hh * NA_HEAD_DIM
                qm = jnp.where(msk, q, jnp.zeros_like(q))
                dom = jnp.where(msk, d_ov, jnp.zeros_like(d_ov))
                s = _nt(qm, kw) + b_ref[cls, hh]
                p = jnp.exp(s - lse_v[:, c0:c0 + 1])
                dp = _nt(dom, vw)
                delta = jnp.sum(jnp.where(msk, prod, 0.0), axis=1, keepdims=True)
                ds = p * (dp - delta)
                db_ref[cls, hh] += ds
                dsb = ds.astype(_BF16)
                dqs.append(_nn(dsb, kw) * (NA_HEAD_DIM ** -0.5))
                dk_win = dk_win + _tn(dsb, qm)
                dv_win = dv_win + _tn(p.astype(_BF16), dom)
            dq_ref[sl, :] = jnp.where(first, dqs[0], dqs[1]).astype(_BF16)
            dk_ref[pl.ds(ws, tw), :] += dk_win
            dv_ref[pl.ds(ws, tw), :] += dv_win

    once = pl.Buffered(1)
    blk = lambda off: pl.BlockSpec((ts, 128), lambda hp, rb: (rb, off + hp))
    whole = lambda off: pl.BlockSpec((T, 128), lambda hp, rb: (0, off + hp), pipeline_mode=once)
    tab = pl.BlockSpec((3, 2, tq, tw), lambda hp, rb: (0, hp, 0, 0), pipeline_mode=once)
    return pl.pallas_call(
        body, name="na_bwd", grid=(NA_HEADS // 2, T // ts),
        in_specs=[blk(0), whole(4), whole(8), tab, blk(0), blk(0), blk(0)],
        out_specs=(blk(0), whole(0), whole(0), tab),
        out_shape=(jax.ShapeDtypeStruct((T, NA_W), _BF16), jax.ShapeDtypeStruct((T, NA_W), _F32),
                   jax.ShapeDtypeStruct((T, NA_W), _F32), jax.ShapeDtypeStruct(bias.shape, _F32)),
        compiler_params=_cparams(("arbitrary", "arbitrary")),
    )(qkv, qkv, qkv, bias, o, d_o, lse)


def _rpb_grad(dbias, rows):
    tw = NA_WIN * GRID_W
    lanes = 16 * GRID_W
    offs = [int(win[0] - blk[0] + NA_KH - 1) for blk, win in _na_class_rows(rows)]

    def body(x_ref, g_ref):
        sub = lax.broadcasted_iota(jnp.int32, (NA_RB, 1), 0)
        qc = lax.broadcasted_iota(jnp.int32, (NA_RB * GRID_W, 1), 0) % GRID_W
        tot = jnp.zeros((NA_RB, lanes), _F32)
        for ci in range(3):
            xv = x_ref[ci, 0]
            for bit in range(6):
                xv = jnp.where(((qc >> bit) & 1) == 1, pltpu.roll(xv, tw - (1 << bit), 1), xv)
            acc = pltpu.roll(jnp.sum(xv.reshape(NA_RB, GRID_W, tw), axis=1), NA_KW, 1)
            acc = jnp.concatenate([acc, jnp.zeros((NA_RB, lanes - tw), _F32)], axis=1)
            for a in range(NA_RB):
                tot = tot + jnp.where(sub == a, pltpu.roll(acc, (GRID_W * (offs[ci] - a)) % lanes, 1), 0.0)
        g_ref[0] = jnp.broadcast_to(jnp.sum(tot, axis=0, keepdims=True), (8, lanes))

    g = pl.pallas_call(
        body, name="rpb_grad", grid=(NA_HEADS,),
        in_specs=[pl.BlockSpec((3, 1) + dbias.shape[2:], lambda h: (0, h, 0, 0))],
        out_specs=pl.BlockSpec((1, 8, lanes), lambda h: (h, 0, 0)),
        out_shape=jax.ShapeDtypeStruct((NA_HEADS, 8, lanes), _F32),
        compiler_params=_cparams(("arbitrary",)),
    )(dbias)
    return g[:, 0].reshape(NA_HEADS, 16, GRID_W)[:, :2 * NA_KH - 1, 1:2 * NA_KW]


def _halo_specs(tm, width, col_of, T, order):
    hb = tm // 8
    last = T // 8 - 1
    if order == "ij":
        cur = pl.BlockSpec((tm, width), lambda i, j: (i, col_of(j)))
        prev = pl.BlockSpec((8, width), lambda i, j: (jnp.maximum(i * hb - 1, 0), col_of(j)))
        nxt = pl.BlockSpec((8, width), lambda i, j: (jnp.minimum((i + 1) * hb, last), col_of(j)))
    else:
        cur = pl.BlockSpec((tm, width), lambda j, i: (i, col_of(j)))
        prev = pl.BlockSpec((8, width), lambda j, i: (jnp.maximum(i * hb - 1, 0), col_of(j)))
        nxt = pl.BlockSpec((8, width), lambda j, i: (jnp.minimum((i + 1) * hb, last), col_of(j)))
    return [prev, cur, nxt]


def _extend(prev_ref, cur_ref, next_ref, i, n_i):
    prev = jnp.where(i > 0, prev_ref[...], 0.0)
    nxt = jnp.where(i < n_i - 1, next_ref[...], 0.0)
    return jnp.concatenate([prev, cur_ref[...], nxt], axis=0)


def _conv_fwd(rest, conv_w, conv_b, T):
    tm = 512
    n_i = T // tm
    n = tm + 16

    def body(p_ref, c_ref, n_ref, w_ref, b_ref, o_ref):
        i = pl.program_id(0)
        ext = _extend(p_ref, c_ref, n_ref, i, n_i)
        acc = jnp.zeros((tm, 512), _F32) + b_ref[...]
        for j in range(CONV_W):
            acc = acc + w_ref[j:j + 1, :] * pltpu.roll(ext, (2 - j) % n, 0)[8:8 + tm]
        o_ref[...] = _silu(acc)

    return pl.pallas_call(
        body, name="conv_fwd", grid=(n_i, 2),
        in_specs=_halo_specs(tm, 512, lambda j: 1 + j, T, "ij")
        + [pl.BlockSpec((8, 512), lambda i, j: (0, j)), pl.BlockSpec((1, 512), lambda i, j: (0, j))],
        out_specs=pl.BlockSpec((tm, 512), lambda i, j: (i, j)),
        out_shape=jax.ShapeDtypeStruct((T, 2 * ML_W), _F32),
        compiler_params=_cparams(("arbitrary", "arbitrary")),
    )(rest, rest, rest, conv_w, conv_b)


def _conv_bwd(rest, conv_w, conv_b, da_f, da_b, T):
    tm = 512
    n_i = T // tm
    n = tm + 16

    def body(up, uc, un, fp, fc, fn, bp, bc, bn, w_ref, b_ref, du_ref, dw_ref):
        i = pl.program_id(1)
        ext_u = _extend(up, uc, un, i, n_i)
        ext_da = _extend(fp, fc, fn, i, n_i) + _extend(bp, bc, bn, i, n_i)
        shifted = [pltpu.roll(ext_u, (2 - j) % n, 0) for j in range(CONV_W)]
        pre = jnp.zeros((n, 512), _F32) + b_ref[...]
        for j in range(CONV_W):
            pre = pre + w_ref[j:j + 1, :] * shifted[j]
        gidx = i * tm - 8 + lax.broadcasted_iota(jnp.int32, (n, 1), 0)
        dpre = jnp.where((gidx >= 0) & (gidx < T), ext_da * _dsilu(pre), 0.0)
        du = jnp.zeros((tm, 512), _F32)
        for j in range(CONV_W):
            du = du + w_ref[j:j + 1, :] * pltpu.roll(dpre, (j - 2) % n, 0)[8:8 + tm]
        du_ref[...] = du.astype(_BF16)
        dpc = dpre[8:8 + tm]
        parts = [jnp.sum(dpc * shifted[j][8:8 + tm], axis=0, keepdims=True) for j in range(CONV_W)]
        parts.append(jnp.sum(dpc, axis=0, keepdims=True))
        parts.append(jnp.zeros((2, 512), _F32))
        upd = jnp.concatenate(parts, axis=0)

        @pl.when(i == 0)
        def _():
            dw_ref[...] = upd

        @pl.when(i > 0)
        def _():
            dw_ref[...] += upd

    return pl.pallas_call(
        body, name="conv_bwd", grid=(2, n_i),
        in_specs=_halo_specs(tm, 512, lambda j: 1 + j, T, "ji")
        + _halo_specs(tm, 512, lambda j: j, T, "ji") + _halo_specs(tm, 512, lambda j: j, T, "ji")
        + [pl.BlockSpec((8, 512), lambda j, i: (0, j)), pl.BlockSpec((1, 512), lambda j, i: (0, j))],
        out_specs=(pl.BlockSpec((tm, 512), lambda j, i: (i, j)), pl.BlockSpec((8, 512), lambda j, i: (0, j))),
        out_shape=(jax.ShapeDtypeStruct((T, 2 * ML_W), _BF16), jax.ShapeDtypeStruct((8, 2 * ML_W), _F32)),
        compiler_params=_cparams(("arbitrary", "arbitrary")),
    )(rest, rest, rest, da_f, da_f, da_f, da_b, da_b, da_b, conv_w, conv_b)


def _scan_rows(x, suffix):
    L = x.shape[0]
    row = lax.broadcasted_iota(jnp.int32, (L, 1), 0)
    step = 1
    while step < L:
        if suffix:
            x = x + jnp.where(row < L - step, pltpu.roll(x, L - step, 0), 0.0)
        else:
            x = x + jnp.where(row >= step, pltpu.roll(x, step, 0), 0.0)
        step *= 2
    return x


def _ml_gates(gt, rev):
    L = gt.shape[0]
    ri = lax.broadcasted_iota(jnp.int32, (L, L), 0)
    ci = lax.broadcasted_iota(jnp.int32, (L, L), 1)
    mask = (ci >= ri) if rev else (ci <= ri)
    lf = jnp.minimum(gt, 0.0) - jnp.log(1.0 + jnp.exp(-jnp.abs(gt)))
    b = _scan_rows(lf, suffix=rev)
    return mask, b, b.T, gt.T


def _ml_head_gates(gt, gates, head, rev):
    _, b, b_t, gt_t = gates
    ci = (8 if rev else 0) + head
    cf = ci + ML_HEADS
    last = 0 if rev else gt.shape[0] - 1
    return dict(icol=gt[:, ci:ci + 1], b_col=b[:, cf:cf + 1], b_row=b_t[cf:cf + 1, :],
                i_row=gt_t[ci:ci + 1, :], bl=b[last:last + 1, cf:cf + 1])


def _ml_chunk(q, k, v, hg, mask, C, n, m, saved=None):
    icol, b_col, b_row, bl = hg["icol"], hg["b_col"], hg["b_row"], hg["bl"]
    if saved is None:
        dlog = jnp.where(mask, b_col - b_row + hg["i_row"], NEG)
        m_t = jnp.maximum(b_col + m, jnp.max(dlog, axis=1, keepdims=True))
        dm = jnp.exp(dlog - m_t)
    else:
        dm, m_t = saved[0].astype(_F32), saved[1]
    ks = k * (ML_HEAD_DIM ** -0.5)
    qb, kb, vb = q.astype(_BF16), ks.astype(_BF16), v.astype(_BF16)
    s = _nt(qb, kb) * dm
    g = jnp.exp(b_col + m - m_t)
    qc = _nt(qb, C.astype(_BF16))
    num = _nn(s.astype(_BF16), vb) + g * qc
    qn = jnp.sum(q * n, axis=1, keepdims=True)
    den = jnp.sum(s, axis=1, keepdims=True) + g * qn
    e_m = jnp.exp(-m_t)
    nrm = jnp.maximum(jnp.abs(den), e_m)
    h = num / nrm
    a_col = bl - b_col + icol
    m_new = jnp.maximum(bl + m, jnp.max(a_col, axis=0, keepdims=True))
    decay = jnp.exp(bl + m - m_new)
    w = jnp.exp(a_col - m_new)
    c_new = decay * C + _tn((w * v).astype(_BF16), kb)
    n_new = decay * n + jnp.sum(w * ks, axis=0, keepdims=True)
    aux = dict(dm=dm, m_t=m_t, ks=ks, qb=qb, kb=kb, vb=vb, s=s, g=g, qc=qc, qn=qn,
               den=den, e_m=e_m, nrm=nrm, decay=decay, w=w)
    return h, c_new, n_new, m_new, aux


def _mlstm_fwd(qk_act, rest, T, rev):
    tb = ML_CB * ML_CHUNK
    nblk = T // tb
    nc = T // ML_CHUNK
    bi = (lambda i: nblk - 1 - i) if rev else (lambda i: i)

    def body(q_ref, k_ref, v_ref, g_ref, h_ref, cs_ref, ns_ref, ms_ref, dm_ref, mt_ref, c_scr, n_scr, m_scr):
        @pl.when(pl.program_id(0) == 0)
        def _():
            c_scr[...] = jnp.zeros_like(c_scr)
            n_scr[...] = jnp.zeros_like(n_scr)
            m_scr[...] = jnp.zeros_like(m_scr)

        def step(j, carry):
            c = (ML_CB - 1 - j) if rev else j
            r0 = pl.multiple_of(c * ML_CHUNK, ML_CHUNK)
            gt = g_ref[pl.ds(r0, ML_CHUNK), :]
            gates = _ml_gates(gt, rev)
            lane = lax.broadcasted_iota(jnp.int32, (1, 128), 1)
            mt_tile = jnp.zeros((ML_CHUNK, 128), _F32)
            for hd in range(ML_HEADS):
                cols = slice(hd * ML_HEAD_DIM, (hd + 1) * ML_HEAD_DIM)
                C = c_scr[hd]
                n = n_scr[hd:hd + 1, :]
                mrow = m_scr[hd:hd + 1, :]
                cs_ref[c, hd] = C
                ns_ref[c, hd:hd + 1, :] = n
                ms_ref[c, hd:hd + 1, :] = mrow
                h, c_new, n_new, m_new, a = _ml_chunk(
                    q_ref[pl.ds(r0, ML_CHUNK), cols], k_ref[pl.ds(r0, ML_CHUNK), cols],
                    v_ref[pl.ds(r0, ML_CHUNK), cols], _ml_head_gates(gt, gates, hd, rev), gates[0],
                    C, n, mrow[:, 0:1])
                h_ref[pl.ds(r0, ML_CHUNK), cols] = h
                dm_ref[c, hd] = a["dm"].astype(_BF16)
                mt_tile = jnp.where(lane == hd, a["m_t"], mt_tile)
                c_scr[hd] = c_new
                n_scr[hd:hd + 1, :] = n_new
                m_scr[hd:hd + 1, :] = jnp.broadcast_to(m_new, (1, 128))
            mt_ref[c] = mt_tile
            return carry

        lax.fori_loop(0, ML_CB, step, 0)

    return pl.pallas_call(
        body, name="mlstm_fwd_rev" if rev else "mlstm_fwd", grid=(nblk,),
        in_specs=[pl.BlockSpec((tb, ML_W), lambda i: (bi(i), 0)),
                  pl.BlockSpec((tb, ML_W), lambda i: (bi(i), 1)),
                  pl.BlockSpec((tb, ML_W), lambda i: (bi(i), 3)),
                  pl.BlockSpec((tb, 128), lambda i: (bi(i), GATE_COL // 128))],
        out_specs=(pl.BlockSpec((tb, ML_W), lambda i: (bi(i), 0)),
                   pl.BlockSpec((ML_CB, ML_HEADS, 128, 128), lambda i: (bi(i), 0, 0, 0)),
                   pl.BlockSpec((ML_CB, ML_HEADS, 128), lambda i: (bi(i), 0, 0)),
                   pl.BlockSpec((ML_CB, ML_HEADS, 128), lambda i: (bi(i), 0, 0)),
                   pl.BlockSpec((ML_CB, ML_HEADS, ML_CHUNK, ML_CHUNK), lambda i: (bi(i), 0, 0, 0)),
                   pl.BlockSpec((ML_CB, ML_CHUNK, 128), lambda i: (bi(i), 0, 0))),
        out_shape=(jax.ShapeDtypeStruct((T, ML_W), _F32),
                   jax.ShapeDtypeStruct((nc, ML_HEADS, 128, 128), _F32),
                   jax.ShapeDtypeStruct((nc, ML_HEADS, 128), _F32),
                   jax.ShapeDtypeStruct((nc, ML_HEADS, 128), _F32),
                   jax.ShapeDtypeStruct((nc, ML_HEADS, ML_CHUNK, ML_CHUNK), _BF16),
                   jax.ShapeDtypeStruct((nc, ML_CHUNK, 128), _F32)),
        scratch_shapes=[pltpu.VMEM((ML_HEADS, 128, 128), _F32), pltpu.VMEM((8, 128), _F32),
                        pltpu.VMEM((8, 128), _F32)],
        compiler_params=_cparams(("arbitrary",)),
    )(qk_act, qk_act, rest, rest)


def _mlstm_bwd(qk_act, rest, d_h, saved, T, rev):
    tb = ML_CB * ML_CHUNK
    nblk = T // tb
    bi = (lambda i: i) if rev else (lambda i: nblk - 1 - i)

    def body(q_ref, k_ref, v_ref, g_ref, dh_ref, cs_ref, ns_ref, ms_ref, dm_ref, mt_ref,
             dqk_ref, dv_ref, dg_ref, dc_scr, dn_scr):
        @pl.when(pl.program_id(0) == 0)
        def _():
            dc_scr[...] = jnp.zeros_like(dc_scr)
            dn_scr[...] = jnp.zeros_like(dn_scr)

        def step(j, carry):
            c = j if rev else (ML_CB - 1 - j)
            r0 = pl.multiple_of(c * ML_CHUNK, ML_CHUNK)
            gt = g_ref[pl.ds(r0, ML_CHUNK), :]
            gates = _ml_gates(gt, rev)
            mask = gates[0]
            lane = lax.broadcasted_iota(jnp.int32, (1, 128), 1)
            sub = lax.broadcasted_iota(jnp.int32, (128, 1), 0)
            db_t = jnp.zeros((ML_CHUNK, 128), _F32)
            da_t = jnp.zeros((ML_CHUNK, 128), _F32)
            cs_rows = jnp.zeros((128, ML_CHUNK), _F32)
            dbl_t = jnp.zeros((1, 128), _F32)
            for hd in range(ML_HEADS):
                cols = slice(hd * ML_HEAD_DIM, (hd + 1) * ML_HEAD_DIM)
                ci = (8 if rev else 0) + hd
                cf = ci + ML_HEADS
                q = q_ref[pl.ds(r0, ML_CHUNK), cols]
                k = k_ref[pl.ds(r0, ML_CHUNK), cols]
                v = v_ref[pl.ds(r0, ML_CHUNK), cols]
                C = cs_ref[c, hd]
                n = ns_ref[c, hd:hd + 1, :]
                m = ms_ref[c, hd:hd + 1, :][:, 0:1]
                dcn = dc_scr[hd]
                dnn = dn_scr[hd:hd + 1, :]
                h, _, _, _, a = _ml_chunk(q, k, v, _ml_head_gates(gt, gates, hd, rev), mask, C, n, m,
                                          saved=(dm_ref[c, hd], mt_ref[c][:, hd:hd + 1]))
                d_hv = dh_ref[pl.ds(r0, ML_CHUNK), cols]
                g, s, w, ks = a["g"], a["s"], a["w"], a["ks"]
                qb, kb, vb = a["qb"], a["kb"], a["vb"]
                dnum = d_hv / a["nrm"]
                hdot = jnp.sum(d_hv * h, axis=1, keepdims=True)
                dden = jnp.where(jnp.abs(a["den"]) >= a["e_m"], -hdot / a["nrm"] * jnp.sign(a["den"]), 0.0)
                dnb = dnum.astype(_BF16)
                d_s = _nt(dnb, vb) + dden
                r = d_s * s
                dsqk = (d_s * a["dm"]).astype(_BF16)
                cb = C.astype(_BF16)
                dq = _nn(dsqk, kb) + g * _nn(dnb, cb) + (dden * g) * n
                dk = _tn(dsqk, qb)
                dv = _tn(s.astype(_BF16), dnb)
                dg = jnp.sum(dnum * a["qc"], axis=1, keepdims=True) + dden * a["qn"]
                db_col = jnp.sum(r, axis=1, keepdims=True) + dg * g
                cs_rows = cs_rows + jnp.where((sub == ci) | (sub == cf), jnp.sum(r, axis=0, keepdims=True), 0.0)
                dc_chunk = _tn((g * dnum).astype(_BF16), qb)
                dn_chunk = jnp.sum((dden * g) * q, axis=0, keepdims=True)
                dcb = dcn.astype(_BF16)
                vdc = _nn(vb, dcb)
                kdc = _nt(kb, dcb)
                dw = jnp.sum(vdc * ks, axis=1, keepdims=True) + jnp.sum(ks * dnn, axis=1, keepdims=True)
                dv = dv + w * kdc
                dk = dk + w * vdc + w * dnn
                da = dw * w
                ddecay = (jnp.sum(jnp.sum(dcn * C, axis=1, keepdims=True), axis=0, keepdims=True)
                          + jnp.sum(dnn * n, axis=1, keepdims=True))
                dbl = ddecay * a["decay"] + jnp.sum(da, axis=0, keepdims=True)
                db_t = db_t + jnp.where(lane == cf, db_col - da, 0.0)
                da_t = da_t + jnp.where(lane == ci, da, 0.0)
                dbl_t = dbl_t + jnp.where(lane == cf, dbl, 0.0)
                dc_scr[hd] = dc_chunk + a["decay"] * dcn
                dn_scr[hd:hd + 1, :] = dn_chunk + a["decay"] * dnn
                dqk_ref[pl.ds(r0, ML_CHUNK), cols] = dq
                dqk_ref[pl.ds(r0, ML_CHUNK), slice(ML_W + hd * 128, ML_W + (hd + 1) * 128)] = dk * (ML_HEAD_DIM ** -0.5)
                dv_ref[pl.ds(r0, ML_CHUNK), cols] = dv.astype(_BF16)
            lo = 8 if rev else 0
            is_i = (lane >= lo) & (lane < lo + ML_HEADS)
            is_f = (lane >= lo + ML_HEADS) & (lane < lo + 2 * ML_HEADS)
            cs_t = cs_rows.T
            db_all = db_t - jnp.where(is_f, cs_t, 0.0)
            dlf = _scan_rows(db_all, suffix=not rev) + dbl_t
            dg_ref[pl.ds(r0, ML_CHUNK), :] = (da_t + jnp.where(is_i, cs_t, 0.0)
                                               + jnp.where(is_f, dlf * _sigmoid(-gt), 0.0))
            return carry

        lax.fori_loop(0, ML_CB, step, 0)

    return pl.pallas_call(
        body, name="mlstm_bwd_rev" if rev else "mlstm_bwd", grid=(nblk,),
        in_specs=[pl.BlockSpec((tb, ML_W), lambda i: (bi(i), 0)),
                  pl.BlockSpec((tb, ML_W), lambda i: (bi(i), 1)),
                  pl.BlockSpec((tb, ML_W), lambda i: (bi(i), 3)),
                  pl.BlockSpec((tb, 128), lambda i: (bi(i), GATE_COL // 128)),
                  pl.BlockSpec((tb, ML_W), lambda i: (bi(i), 0)),
                  pl.BlockSpec((ML_CB, ML_HEADS, 128, 128), lambda i: (bi(i), 0, 0, 0)),
                  pl.BlockSpec((ML_CB, ML_HEADS, 128), lambda i: (bi(i), 0, 0)),
                  pl.BlockSpec((ML_CB, ML_HEADS, 128), lambda i: (bi(i), 0, 0)),
                  pl.BlockSpec((ML_CB, ML_HEADS, ML_CHUNK, ML_CHUNK), lambda i: (bi(i), 0, 0, 0)),
                  pl.BlockSpec((ML_CB, ML_CHUNK, 128), lambda i: (bi(i), 0, 0))],
        out_specs=(pl.BlockSpec((tb, 2 * ML_W), lambda i: (bi(i), 0)),
                   pl.BlockSpec((tb, ML_W), lambda i: (bi(i), 0)),
                   pl.BlockSpec((tb, 128), lambda i: (bi(i), 0))),
        out_shape=(jax.ShapeDtypeStruct((T, 2 * ML_W), _F32), jax.ShapeDtypeStruct((T, ML_W), _BF16),
                   jax.ShapeDtypeStruct((T, 128), _F32)),
        scratch_shapes=[pltpu.VMEM((ML_HEADS, 128, 128), _F32), pltpu.VMEM((8, 128), _F32)],
        compiler_params=_cparams(("arbitrary",)),
    )(qk_act, qk_act, rest, rest, d_h, *saved)


def _post(x, target, o_na, rest, h_f, h_b, gate, ml_norm_w, final_w, w_out_bf, T):
    tm = 256
    n_i = T // tm

    def body(x_ref, t_ref, o_ref, zna_ref, hf_ref, hb_ref, mo_ref, mz_ref, gate_ref, mw_ref, fw_ref, w_ref,
             dx1_ref, do_ref, dzna_ref, dh_ref, dmo_ref, dmz_ref, dwo_ref, vec_ref):
        i = pl.program_id(0)
        gate_v = gate_ref[...]
        fw = fw_ref[...]
        zna = zna_ref[...]
        o = o_ref[...]
        sig_zna = _sigmoid(zna)
        silu_zna = zna * sig_zna
        na_out = o * silu_zna
        hsum = hf_ref[...] + hb_ref[...]
        sg = _sigmoid(mo_ref[...])
        hm = hsum * sg
        mz = mz_ref[...]
        sig_mz = _sigmoid(mz)
        smz = mz * sig_mz
        dsilu_mz = sig_mz * (1.0 + mz * (1.0 - sig_mz))
        hn_l, rstd_l, ml_l = [], [], []
        for hd in range(ML_HEADS):
            cols = slice(hd * 128, (hd + 1) * 128)
            hh = hm[:, cols]
            mu = jnp.mean(hh, axis=-1, keepdims=True)
            var = jnp.mean(jnp.square(hh - mu), axis=-1, keepdims=True)
            rstd = lax.rsqrt(var + EPS)
            hn = (hh - mu) * rstd
            hn_l.append(hn)
            rstd_l.append(rstd)
            ml_l.append(hn * mw_ref[:, cols] * smz[:, cols])
        mix = jnp.concatenate([na_out] + ml_l, axis=1).astype(_BF16)
        y = _nn(mix, w_ref[...])
        x1 = x_ref[...] + gate_v * y
        r = lax.rsqrt(jnp.mean(x1 * x1, axis=-1, keepdims=True) + EPS)
        xhat = x1 * r
        out = xhat * fw
        err = out - t_ref[...]
        loss = 0.5 * jnp.sum(jnp.sum(err * err, axis=1, keepdims=True), axis=0, keepdims=True) / D_MODEL
        dout = err * (1.0 / D_MODEL)
        dfw = jnp.sum(dout * xhat, axis=0, keepdims=True)
        dxhat = dout * fw
        dx1 = r * (dxhat - xhat * jnp.mean(dxhat * xhat, axis=-1, keepdims=True))
        dx1_ref[...] = dx1
        dgate = jnp.sum(dx1 * y, axis=0, keepdims=True)
        dy = (dx1 * gate_v).astype(_BF16)
        dmix = _nt(dy, w_ref[...])
        dwo = _tn(mix, dy)
        dna = dmix[:, :NA_W]
        do_ref[...] = (dna * silu_zna).astype(_BF16)
        dzna_ref[...] = (dna * o * (sig_zna * (1.0 + zna * (1.0 - sig_zna)))).astype(_BF16)
        dmw_l = []
        for hd in range(ML_HEADS):
            cols = slice(hd * 128, (hd + 1) * 128)
            dml = dmix[:, NA_W + hd * 128:NA_W + (hd + 1) * 128]
            hn = hn_l[hd]
            mwv = mw_ref[:, cols]
            dmz_ref[:, cols] = (dml * hn * mwv * dsilu_mz[:, cols]).astype(_BF16)
            dhn = dml * mwv * smz[:, cols]
            dmw_l.append(jnp.sum(dml * hn * smz[:, cols], axis=0, keepdims=True))
            dhm = rstd_l[hd] * (dhn - jnp.mean(dhn, axis=-1, keepdims=True)
                                - hn * jnp.mean(dhn * hn, axis=-1, keepdims=True))
            sgc = sg[:, cols]
            dh_ref[:, cols] = dhm * sgc
            dmo_ref[:, cols] = (dhm * hsum[:, cols] * sgc * (1.0 - sgc)).astype(_BF16)
        dmw = jnp.concatenate(dmw_l + [jnp.zeros((1, D_MODEL - ML_W), _F32)], axis=1)
        lane = lax.broadcasted_iota(jnp.int32, (1, D_MODEL), 1)
        vec = jnp.concatenate([dfw, dgate, dmw, jnp.where(lane == 0, loss, 0.0),
                               jnp.zeros((4, D_MODEL), _F32)], axis=0)

        @pl.when(i == 0)
        def _():
            dwo_ref[...] = dwo
            vec_ref[...] = vec

        @pl.when(i > 0)
        def _():
            dwo_ref[...] += dwo
            vec_ref[...] += vec

    tok = lambda w, j: pl.BlockSpec((tm, w), lambda i: (i, j))
    tok3 = pl.BlockSpec((None, tm, D_MODEL), lambda i: (0, i, 0))
    row = lambda w: pl.BlockSpec((1, w), lambda i: (0, 0))
    f32 = lambda w: jax.ShapeDtypeStruct((T, w), _F32)
    bf16 = lambda w: jax.ShapeDtypeStruct((T, w), _BF16)
    return pl.pallas_call(
        body, name="post", grid=(n_i,),
        in_specs=[tok3, tok3, tok(NA_W, 0), tok(NA_W, 0), tok(ML_W, 0), tok(ML_W, 0),
                  tok(ML_W, 4), tok(ML_W, 5), row(D_MODEL), row(ML_W), row(D_MODEL),
                  pl.BlockSpec((D_MODEL, D_MODEL), lambda i: (0, 0))],
        out_specs=(tok(D_MODEL, 0), tok(NA_W, 0), tok(NA_W, 0), tok(ML_W, 0), tok(ML_W, 0), tok(ML_W, 0),
                   pl.BlockSpec((D_MODEL, D_MODEL), lambda i: (0, 0)),
                   pl.BlockSpec((8, D_MODEL), lambda i: (0, 0))),
        out_shape=(f32(D_MODEL), bf16(NA_W), bf16(NA_W), f32(ML_W), bf16(ML_W),
                   bf16(ML_W), jax.ShapeDtypeStruct((D_MODEL, D_MODEL), _F32),
                   jax.ShapeDtypeStruct((8, D_MODEL), _F32)),
        compiler_params=_cparams(("arbitrary",)),
    )(x, target, o_na, rest, h_f, h_b, rest, rest, gate, ml_norm_w, final_w, w_out_bf)


def _section_specs(sections, tm):
    specs, args = [], []
    for _, width, parts in sections:
        for arr, cb in parts:
            specs.append(pl.BlockSpec((tm, width), functools.partial(lambda i, cb: (i, cb), cb=cb)))
            args.append(arr)
    return specs, args


def _section_values(sections, refs, dtype):
    vals, at = [], 0
    for _, _, parts in sections:
        v = refs[at][...]
        for r in refs[at + 1:at + len(parts)]:
            v = v.astype(_F32) + r[...].astype(_F32)
        at += len(parts)
        vals.append(v.astype(dtype))
    return vals


def _inproj_bwd_x(x, dx1, scale1p, norm_w, w_in_bf, sections, T):
    tm = 512
    sspecs, sargs = _section_specs(sections, tm)
    ns = len(sargs)

    def body(*refs):
        x_ref, dx1_ref, sc_ref, nw_ref, w_ref = refs[:5]
        srefs = refs[5:5 + ns]
        gx_ref, vec_ref = refs[5 + ns:]
        i = pl.program_id(0)
        vals = _section_values(sections, srefs, _BF16)
        dh = jnp.zeros((tm, D_MODEL), _F32)
        for (c0, width, _), val in zip(sections, vals):
            dh = dh + _nt(val, w_ref[:, c0:c0 + width])
        xv = x_ref[...]
        r = lax.rsqrt(jnp.mean(xv * xv, axis=-1, keepdims=True) + EPS)
        xhat = xv * r
        nw = nw_ref[...]
        dshift = jnp.sum(dh, axis=0, keepdims=True)
        dscale = jnp.sum(dh * xhat * nw, axis=0, keepdims=True)
        dhpre = dh * sc_ref[...]
        dnw = jnp.sum(dhpre * xhat, axis=0, keepdims=True)
        dxhat = dhpre * nw
        gx_ref[...] = dx1_ref[...] + r * (dxhat - xhat * jnp.mean(dxhat * xhat, axis=-1, keepdims=True))
        vec = jnp.concatenate([dshift, dscale, dnw, jnp.zeros((5, D_MODEL), _F32)], axis=0)

        @pl.when(i == 0)
        def _():
            vec_ref[...] = vec

        @pl.when(i > 0)
        def _():
            vec_ref[...] += vec

    row = pl.BlockSpec((1, D_MODEL), lambda i: (0, 0))
    tok = pl.BlockSpec((tm, D_MODEL), lambda i: (i, 0))
    tok3 = pl.BlockSpec((None, tm, D_MODEL), lambda i: (0, i, 0))
    return pl.pallas_call(
        body, name="inproj_bwd_x", grid=(T // tm,),
        in_specs=[tok3, tok, row, row,
                  pl.BlockSpec((D_MODEL, IN_PAD), lambda i: (0, 0), pipeline_mode=pl.Buffered(1))] + sspecs,
        out_specs=(tok3, pl.BlockSpec((8, D_MODEL), lambda i: (0, 0))),
        out_shape=(jax.ShapeDtypeStruct((1, T, D_MODEL), _F32), jax.ShapeDtypeStruct((8, D_MODEL), _F32)),
        compiler_params=_cparams(("arbitrary",)),
    )(x, dx1, scale1p, norm_w, w_in_bf, *sargs)


def _inproj_bwd_w(h_t, sections, T):
    tm = 1024
    n_i = T // tm
    sspecs, sargs = _section_specs(sections, tm)
    ns = len(sargs)

    def body(*refs):
        h_ref = refs[0]
        srefs = refs[1:1 + ns]
        dw_ref, db_ref, acc, sem = refs[1 + ns:]
        i = pl.program_id(0)

        @pl.when(i == 0)
        def _():
            acc[...] = jnp.zeros_like(acc)
            db_ref[...] = jnp.zeros_like(db_ref)

        hv = h_ref[...]
        for (c0, width, _), v in zip(sections, _section_values(sections, srefs, _F32)):
            acc[:, c0:c0 + width] += _nn(hv, v.astype(_BF16))
            db_ref[0:1, c0:c0 + width] += jnp.sum(v, axis=0, keepdims=True)

        @pl.when(i == n_i - 1)
        def _():
            cp = pltpu.make_async_copy(acc, dw_ref, sem)
            cp.start()
            cp.wait()

    return pl.pallas_call(
        body, name="inproj_bwd_w", grid=(n_i,),
        in_specs=[pl.BlockSpec((D_MODEL, tm), lambda i: (0, i))] + sspecs,
        out_specs=(pl.BlockSpec(memory_space=pl.ANY), pl.BlockSpec((8, IN_PAD), lambda i: (0, 0))),
        out_shape=(jax.ShapeDtypeStruct((D_MODEL, IN_PAD), _F32), jax.ShapeDtypeStruct((8, IN_PAD), _F32)),
        scratch_shapes=[pltpu.VMEM((D_MODEL, IN_PAD), _F32), pltpu.SemaphoreType.DMA],
        compiler_params=_cparams(("arbitrary",)),
    )(h_t, *sargs)


def _adamw_math(w, g, m, v):
    m = ADAM_B1 * m + (1.0 - ADAM_B1) * g
    v = ADAM_B2 * v + (1.0 - ADAM_B2) * jnp.square(g)
    m_hat = m / (1.0 - ADAM_B1 ** ADAM_STEP)
    v_hat = v / (1.0 - ADAM_B2 ** ADAM_STEP)
    delta = -ADAM_LR * (m_hat / (jnp.sqrt(v_hat) + ADAM_EPS) + ADAM_WD * w)
    return delta, m, v


def _adamw_slots(w, m, v, slots, tr, name, own=None):
    R, C = w.shape
    extra = [] if own is None else [own]

    def body(w_ref, m_ref, v_ref, s_ref, *refs):
        g_ref, d_ref, nm_ref, nv_ref = refs[len(extra):]
        g = s_ref[0].astype(_F32)
        for k in range(1, N_DEV):
            g = g + s_ref[k].astype(_F32)
        if extra:
            g = g + refs[0][...].astype(_F32)
        g_ref[...] = g
        d_ref[...], nm_ref[...], nv_ref[...] = _adamw_math(w_ref[...], g, m_ref[...], v_ref[...])

    blk = pl.BlockSpec((tr, C), lambda i: (i, 0))
    return pl.pallas_call(
        body, name=name, grid=(R // tr,),
        in_specs=[blk, blk, blk, pl.BlockSpec((N_DEV, tr, C), lambda i: (0, i, 0))] + [blk] * len(extra),
        out_specs=(blk, blk, blk, blk),
        out_shape=tuple(jax.ShapeDtypeStruct((R, C), _F32) for _ in range(4)),
        compiler_params=_cparams(("arbitrary",)),
    )(w, m, v, slots, *extra)


def _w_ada_update(c_all, dmod_my, w, m, v):
    def body(c_ref, d_ref, w_ref, m_ref, v_ref, g_ref, dl_ref, nm_ref, nv_ref):
        g = lax.dot_general(_silu(c_ref[...]), d_ref[...], (((0,), (0,)), ((), ())),
                            precision=_HI, preferred_element_type=_F32)
        g_ref[...] = g
        dl_ref[...], nm_ref[...], nv_ref[...] = _adamw_math(w_ref[...], g, m_ref[...], v_ref[...])

    return pl.pallas_call(
        body, name="w_ada_update",
        out_shape=tuple(jax.ShapeDtypeStruct(w.shape, _F32) for _ in range(4)),
        compiler_params=_cparams(),
    )(c_all, dmod_my, w, m, v)


_PACK = (("b_ada", 3072, 3072), ("norm_w", 1024, 1024), ("b_in", IN_W, IN_PAD), ("conv_w", 5120, 5120),
         ("conv_b", 1024, 1024), ("rpb", 3720, 3840), ("ml_norm_w", 512, 512), ("final_norm_w", 1024, 1024),
         ("loss", 1, 128))
_PACK_OFF = {}
_off = 0
for _name, _len, _pad in _PACK:
    _PACK_OFF[_name] = (_off, _len)
    _off += _pad
_PACK_LEN = _off


def _pack(parts):
    cols = []
    for name, length, pad in _PACK:
        vec = parts[name].reshape(-1).astype(_F32)
        cols.append(jnp.pad(vec, (0, pad - length)))
    return jnp.concatenate(cols).reshape(1, _PACK_LEN)


def _unpack(vec, name, shape):
    off, length = _PACK_OFF[name]
    return vec.reshape(-1)[off:off + length].reshape(shape)


def kernel(x, c, w_ada, b_ada, norm_w, w_in, b_in, conv_w, conv_b, rpb, ml_norm_w, w_out, final_norm_w, loss_target, m_w_ada, m_b_ada, m_norm_w, m_w_in, m_b_in, m_conv_w, m_conv_b, m_rpb, m_ml_norm_w, m_w_out, m_final_norm_w, v_w_ada, v_b_ada, v_norm_w, v_w_in, v_b_in, v_conv_w, v_conv_b, v_rpb, v_ml_norm_w, v_w_out, v_final_norm_w):
    T = x.shape[1]
    rows = T // GRID_W
    me = 4 * lax.axis_index("x") + 2 * lax.axis_index("y") + lax.axis_index("c")
    n_in = w_in.shape[2]
    n_ada = w_ada.shape[2]
    n_cw = conv_w.shape[2]
    n_wo = w_out.shape[1]

    w_in_my, w_out_my = w_in[0].astype(_BF16), w_out[0].astype(_BF16)
    first_leg = (1,) + _CHIP_PEERS
    start_in = _scatter_start([w_in_my], "w_in_start", scatter=False, ks=first_leg)
    start_out = _scatter_start([w_out_my], "w_out_start", scatter=False)
    tokens = start_in[-1][0:1, 0:1] + start_out[-1][0:1, 0:1]
    g_conv_w, g_c = _exchange([conv_w[0], c + tokens], [False] * 2, "gather_small")
    b_in_pad = jnp.pad(b_in, ((0, 0), (0, IN_PAD - IN_W)))
    conv_w_full = jnp.pad(g_conv_w.transpose(1, 0, 2).reshape(CONV_W, N_DEV * n_cw), ((0, 3), (0, 0)))
    c_all = g_c.reshape(N_DEV, D_MODEL)

    b_ada_my = lax.dynamic_slice(b_ada, (0, me * n_ada), (1, n_ada))
    (mod_slots,) = _exchange([_mod_part(c_all, w_ada[0], b_ada_my)], [False], "gather_mod")
    mod = lax.dynamic_index_in_dim(mod_slots, me, axis=1, keepdims=False).reshape(1, 3 * D_MODEL)
    shift, scale, gate = mod[:, :D_MODEL], mod[:, D_MODEL:2 * D_MODEL], mod[:, 2 * D_MODEL:]
    scale1p = 1.0 + scale
    bias = _na_bias_tables(rpb[0], rows)

    def own_slot(land, own):
        return lax.dynamic_update_slice(land, own[None], (me,) + (0,) * own.ndim)

    def gathered(started, after, name):
        (own,), (land,) = _scatter_wait(started, after, name, scatter=False)
        return own_slot(land, own)

    (w_in_own,), (w_in_land,) = _scatter_wait(start_in, bias[0, 0, :8, :128] + scale1p[:, :128], "w_in_wait",
                                              scatter=False, ks=first_leg)
    g_w_in = own_slot(_relay_wait(_relay_start(w_in_land, "w_in_relay_start"), "w_in_relay_wait"), w_in_own)
    w_in_full = g_w_in.transpose(1, 0, 2).reshape(D_MODEL, N_DEV * n_in)
    w_in_bf = jnp.pad(w_in_full, ((0, 0), (0, IN_PAD - IN_W)))

    qkv, rest, h_bf = _inproj_fwd(x, scale1p, shift, norm_w, w_in_bf, b_in_pad)
    o_na, lse = _na_fwd(qkv, bias, T)
    qk_act = _conv_fwd(rest, conv_w_full, conv_b, T)
    h_f, *saved_f = _mlstm_fwd(qk_act, rest, T, False)
    h_b, *saved_b = _mlstm_fwd(qk_act, rest, T, True)

    w_out_bf = gathered(start_out, saved_b[2], "w_out_wait").reshape(N_DEV * n_wo, D_MODEL)
    dx1, d_o, dz_na, d_h, d_mo, d_mz, dwo, pvec = _post(
        x, loss_target, o_na, rest, h_f, h_b, gate, ml_norm_w, final_norm_w.reshape(1, D_MODEL), w_out_bf, T)

    dq_na, dk_na, dv_na, dbias = _na_bwd(qkv, bias, o_na, d_o, lse, T)
    d_rpb = _rpb_grad(dbias, rows)
    dqk_f, dv_f, dg_f = _mlstm_bwd(qk_act, rest, d_h, saved_f, T, False)
    dqk_b, dv_b, dg_b = _mlstm_bwd(qk_act, rest, d_h, saved_b, T, True)
    d_u, dconv = _conv_bwd(rest, conv_w_full, conv_b, dqk_f, dqk_b, T)

    sections = [(0, 512, [(dq_na, 0)]), (512, 512, [(dk_na, 0)]), (1024, 512, [(dv_na, 0)]),
                (1536, 512, [(dz_na, 0)]), (2048, 512, [(d_u, 0)]), (2560, 512, [(d_u, 1)]),
                (3072, 512, [(dv_f, 0), (dv_b, 0)]), (3584, 512, [(d_mo, 0)]), (4096, 512, [(d_mz, 0)]),
                (4608, 128, [(dg_f, 0), (dg_b, 0)])]
    dw_pad, db_pad = _inproj_bwd_w(h_bf, sections, T)
    db_in = db_pad[0, :IN_W]

    dw_blocks = dw_pad[:, :IN_W].astype(_BF16).reshape(D_MODEL, N_DEV, n_in).transpose(1, 0, 2)
    dwo_blocks = dwo.astype(_BF16).reshape(N_DEV, n_wo, D_MODEL)
    started = _scatter_start([dw_blocks, dwo_blocks], "grads_start")
    grad_x, xvec = _inproj_bwd_x(x, dx1, scale1p + started[-1][0:1, 0:1], norm_w, w_in_bf, sections, T)
    (dw_blocks, dwo_blocks), (s_w_in, s_w_out) = _scatter_wait(started, xvec, "grads_wait")

    small = _pack({
        "b_ada": jnp.concatenate([xvec[0], xvec[1], pvec[1]]),
        "norm_w": xvec[2], "b_in": db_in, "conv_w": dconv[:CONV_W], "conv_b": dconv[CONV_W],
        "rpb": d_rpb, "ml_norm_w": pvec[2, :ML_W], "final_norm_w": pvec[0], "loss": pvec[3, :1]})
    (s_small,) = _exchange([small], [False], "exchange_small")

    own = lambda blocks: lax.dynamic_index_in_dim(blocks, me, axis=0, keepdims=False)
    g_w_in_s, d_w_in, nm_w_in, nv_w_in = _adamw_slots(
        w_in[0], m_w_in[0], v_w_in[0], s_w_in, 128, "adamw_w_in", own=own(dw_blocks))
    g_w_out_s, d_w_out, nm_w_out, nv_w_out = _adamw_slots(
        w_out[0], m_w_out[0], v_w_out[0], s_w_out, n_wo, "adamw_w_out", own=own(dwo_blocks))
    dmod_all = s_small[:, 0, :3 * D_MODEL]
    dmod_my = lax.dynamic_slice(dmod_all, (0, me * n_ada), (N_DEV, n_ada))
    g_w_ada, d_w_ada, nm_w_ada, nv_w_ada = _w_ada_update(c_all, dmod_my, w_ada[0], m_w_ada[0], v_w_ada[0])

    def embed(shard):
        return lax.dynamic_update_slice(jnp.zeros((CONV_W, N_DEV * n_cw), _F32), shard[0], (0, me * n_cw))

    zero1 = jnp.zeros((1,), _F32)
    packed = lambda b_a, n_w, b_i, c_w, c_b, rp, mn, fn: _pack({
        "b_ada": b_a, "norm_w": n_w, "b_in": b_i, "conv_w": embed(c_w), "conv_b": c_b, "rpb": rp,
        "ml_norm_w": mn, "final_norm_w": fn, "loss": zero1})
    pw = packed(b_ada, norm_w, b_in, conv_w, conv_b, rpb, ml_norm_w, final_norm_w)
    pm = packed(m_b_ada, m_norm_w, m_b_in, m_conv_w, m_conv_b, m_rpb, m_ml_norm_w, m_final_norm_w)
    pv = packed(v_b_ada, v_norm_w, v_b_in, v_conv_w, v_conv_b, v_rpb, v_ml_norm_w, v_final_norm_w)
    sg, sd, sm, sv = _adamw_slots(pw, pm, pv, s_small, 1, "adamw_small")

    def small_outs(vec):
        cw = lax.dynamic_slice(_unpack(vec, "conv_w", (CONV_W, N_DEV * n_cw)), (0, me * n_cw), (CONV_W, n_cw))
        return dict(b_ada=_unpack(vec, "b_ada", b_ada.shape), norm_w=_unpack(vec, "norm_w", norm_w.shape),
                    b_in=_unpack(vec, "b_in", b_in.shape), conv_w=cw[None],
                    conv_b=_unpack(vec, "conv_b", conv_b.shape), rpb=_unpack(vec, "rpb", rpb.shape),
                    ml_norm_w=_unpack(vec, "ml_norm_w", ml_norm_w.shape),
                    final_norm_w=_unpack(vec, "final_norm_w", final_norm_w.shape))

    loss = _unpack(sg, "loss", ())
    order = ("w_ada", "b_ada", "norm_w", "w_in", "b_in", "conv_w", "conv_b", "rpb", "ml_norm_w", "w_out",
             "final_norm_w")
    outs = []
    for vec, big in ((sg, (g_w_ada, g_w_in_s, g_w_out_s)), (sd, (d_w_ada, d_w_in, d_w_out)),
                     (sm, (nm_w_ada, nm_w_in, nm_w_out)), (sv, (nv_w_ada, nv_w_in, nv_w_out))):
        group = small_outs(vec)
        group.update(w_ada=big[0][None], w_in=big[1][None], w_out=big[2][None])
        outs.extend(group[name] for name in order)
    return (loss, grad_x, *outs)
```

```python
import functools

import numpy as np
import jax
import jax.numpy as jnp
from jax import lax
from jax.experimental import pallas as pl
from jax.experimental.pallas import tpu as pltpu

N_DEV = 8
D_MODEL = 1024
GRID_W = 64
NA_HEADS = 8
NA_HEAD_DIM = 64
NA_KH = 8
NA_KW = 16
NA_W = 512
ML_HEADS = 4
ML_HEAD_DIM = 128
ML_W = 512
ML_CHUNK = 512
CONV_W = 5
EPS = 1e-6
IN_W = 4624
IN_PAD = 4736
REST_W = IN_PAD - 3 * NA_W
GATE_COL = 3072
NEG = -1e30
NA_RB = 4
NA_WIN = 12
NA_SUB = 16
ML_CB = 1
ADAM_LR = 0.001
ADAM_B1 = 0.9
ADAM_B2 = 0.999
ADAM_EPS = 1e-08
ADAM_WD = 0.01
ADAM_STEP = 10
VMEM_LIMIT = 56 * 1024 * 1024

_F32 = jnp.float32
_BF16 = jnp.bfloat16
_HI = lax.Precision.HIGHEST


def _cparams(sem=None):
    return pltpu.CompilerParams(dimension_semantics=sem, vmem_limit_bytes=VMEM_LIMIT)


def _nt(a, b):
    return lax.dot_general(a, b, (((1,), (1,)), ((), ())), preferred_element_type=_F32)


def _tn(a, b):
    return lax.dot_general(a, b, (((0,), (0,)), ((), ())), preferred_element_type=_F32)


def _nn(a, b):
    return jnp.dot(a, b, preferred_element_type=_F32)


def _sigmoid(x):
    return 1.0 / (1.0 + jnp.exp(-x))


def _silu(x):
    return x * _sigmoid(x)


def _dsilu(x):
    s = _sigmoid(x)
    return s * (1.0 + x * (1.0 - s))


def _exchange(arrs, scatter, name):
    n = len(arrs)
    out_shape = []
    for a, sc in zip(arrs, scatter):
        blk = a.shape[1:] if sc else a.shape
        out_shape.append(jax.ShapeDtypeStruct((N_DEV,) + tuple(blk), a.dtype))

    def body(*refs):
        ins = refs[:n]
        outs = refs[n:2 * n]
        send_sems, recv_sems, local_sems = refs[2 * n:]
        x, y, c = lax.axis_index("x"), lax.axis_index("y"), lax.axis_index("c")
        me = 4 * x + 2 * y + c
        local, sends, recvs = [], [], []
        for a in range(n):
            own = ins[a].at[me] if scatter[a] else ins[a]
            cp = pltpu.make_async_copy(own, outs[a].at[me], local_sems.at[a])
            cp.start()
            local.append(cp)
            for k in range(1, N_DEV):
                px = 1 - x if k & 4 else x
                py = 1 - y if k & 2 else y
                pc = 1 - c if k & 1 else c
                p = 4 * px + 2 * py + pc
                src = ins[a].at[p] if scatter[a] else ins[a]
                snd = pltpu.make_async_remote_copy(
                    src_ref=src, dst_ref=outs[a].at[me],
                    send_sem=send_sems.at[a, k - 1], recv_sem=recv_sems.at[a, k - 1],
                    device_id=(px, py, pc), device_id_type=pl.DeviceIdType.MESH)
                snd.start()
                sends.append(snd)
                rcv = pltpu.make_async_remote_copy(
                    src_ref=src, dst_ref=outs[a].at[p],
                    send_sem=send_sems.at[a, k - 1], recv_sem=recv_sems.at[a, k - 1],
                    device_id=(px, py, pc), device_id_type=pl.DeviceIdType.MESH)
                recvs.append(rcv)
        for rcv in recvs:
            rcv.wait_recv()
        for snd in sends:
            snd.wait_send()
        for cp in local:
            cp.wait()

    any_spec = pl.BlockSpec(memory_space=pl.ANY)
    res = pl.pallas_call(
        body, name=name, out_shape=tuple(out_shape),
        in_specs=[any_spec] * n, out_specs=tuple([any_spec] * n),
        scratch_shapes=[pltpu.SemaphoreType.DMA((n, N_DEV - 1)),
                        pltpu.SemaphoreType.DMA((n, N_DEV - 1)),
                        pltpu.SemaphoreType.DMA((n,))],
    )(*arrs)
    return list(res)


def _peer(k):
    x, y, c = lax.axis_index("x"), lax.axis_index("y"), lax.axis_index("c")
    px = 1 - x if k & 4 else x
    py = 1 - y if k & 2 else y
    pc = 1 - c if k & 1 else c
    return (px, py, pc), 4 * px + 2 * py + pc, 4 * x + 2 * y + c


_ALL_PEERS = tuple(range(1, N_DEV))
_CHIP_PEERS = (2, 4, 6)


def _scatter_copy(srcs, lands, send_sems, recv_sems, a, k, receive, scatter, ks=_ALL_PEERS):
    dev, p, me = _peer(k)
    at = a * len(ks) + ks.index(k)
    return pltpu.make_async_remote_copy(
        src_ref=srcs[a].at[p] if scatter else srcs[a], dst_ref=lands[a].at[p if receive else me],
        send_sem=send_sems[at], recv_sem=recv_sems[at],
        device_id=dev, device_id_type=pl.DeviceIdType.MESH)


def _scatter_start(arrs, name, scatter=True, ks=_ALL_PEERS):
    n = len(arrs)
    ns = n * len(ks)
    hbm = pl.BlockSpec(memory_space=pltpu.HBM)
    sem = pl.BlockSpec(memory_space=pltpu.SEMAPHORE)

    def body(*refs):
        srcs, lands = refs[:n], refs[n:2 * n]
        send_sems, recv_sems = refs[2 * n:2 * n + ns], refs[2 * n + ns:2 * n + 2 * ns]
        token = refs[-1]
        for a in range(n):
            for k in ks:
                _scatter_copy(srcs, lands, send_sems, recv_sems, a, k, False, scatter, ks).start()
        token[...] = jnp.zeros_like(token)

    land_shapes = [a.shape if scatter else (N_DEV,) + a.shape for a in arrs]
    buffers = [pltpu.HBM(a.shape, a.dtype) for a in arrs]
    land_buffers = [pltpu.HBM(s, a.dtype) for s, a in zip(land_shapes, arrs)]
    sems = [pltpu.SemaphoreType.DMA(()) for _ in range(2 * ns)]
    res = pl.pallas_call(
        body, name=name,
        out_shape=(*sems, *buffers, *land_buffers, jax.ShapeDtypeStruct((8, 128), _F32)),
        in_specs=[hbm] * (2 * n),
        out_specs=(*([sem] * (2 * ns)), *([hbm] * (2 * n)), pl.BlockSpec(memory_space=pltpu.VMEM)),
        input_output_aliases={i: 2 * ns + i for i in range(2 * n)},
        compiler_params=pltpu.CompilerParams(has_side_effects=pltpu.SideEffectType.DATAFLOW_SIDE_EFFECTING),
    )(*[pltpu.with_memory_space_constraint(a, pltpu.HBM) for a in arrs],
      *[pltpu.with_memory_space_constraint(jnp.zeros(s, a.dtype) if scatter else lax.empty(s, a.dtype), pltpu.HBM)
        for s, a in zip(land_shapes, arrs)])
    res = list(res)
    return (res[:ns], res[ns:2 * ns], res[2 * ns:2 * ns + n], res[2 * ns + n:2 * ns + 2 * n], res[-1])


def _scatter_wait(started, after, name, scatter=True, ks=_ALL_PEERS):
    send_sems, recv_sems, srcs, lands, _ = started
    n = len(srcs)
    ns = len(send_sems)
    hbm = pl.BlockSpec(memory_space=pltpu.HBM)
    sem = pl.BlockSpec(memory_space=pltpu.SEMAPHORE)

    def body(*refs):
        src_refs, land_refs = refs[:n], refs[n:2 * n]
        s_sems, r_sems = refs[2 * n:2 * n + ns], refs[2 * n + ns:2 * n + 2 * ns]
        for a in range(n):
            for k in ks:
                _scatter_copy(src_refs, land_refs, s_sems, r_sems, a, k, False, scatter, ks).wait_send()
                _scatter_copy(src_refs, land_refs, s_sems, r_sems, a, k, True, scatter, ks).wait_recv()

    buffers = [pltpu.HBM(a.shape, a.dtype) for a in list(srcs) + list(lands)]
    res = pl.pallas_call(
        body, name=name, out_shape=tuple(buffers),
        in_specs=[hbm] * (2 * n) + [sem] * (2 * ns) + [pl.BlockSpec(memory_space=pl.ANY)],
        out_specs=tuple([hbm] * (2 * n)),
        input_output_aliases={i: i for i in range(2 * n)},
        compiler_params=pltpu.CompilerParams(has_side_effects=pltpu.SideEffectType.DATAFLOW_SIDE_EFFECTING),
    )(*srcs, *lands, *send_sems, *recv_sems, after)
    return list(res[:n]), list(res[n:])


def _relay_copy(land, send_sems, recv_sems, j, receive):
    k = _CHIP_PEERS[j]
    sibling, _, _ = _peer(1)
    _, slot, _ = _peer(k | 1 if receive else k)
    return pltpu.make_async_remote_copy(
        src_ref=land.at[slot], dst_ref=land.at[slot], send_sem=send_sems[j], recv_sem=recv_sems[j],
        device_id=sibling, device_id_type=pl.DeviceIdType.MESH)


def _relay_start(land, name):
    ns = len(_CHIP_PEERS)
    hbm = pl.BlockSpec(memory_space=pltpu.HBM)
    sem = pl.BlockSpec(memory_space=pltpu.SEMAPHORE)

    def body(*refs):
        land_ref = refs[0]
        send_sems, recv_sems = refs[1:1 + ns], refs[1 + ns:1 + 2 * ns]
        for j in range(ns):
            _relay_copy(land_ref, send_sems, recv_sems, j, False).start()
        refs[-1][...] = jnp.zeros_like(refs[-1])

    sems = [pltpu.SemaphoreType.DMA(()) for _ in range(2 * ns)]
    res = pl.pallas_call(
        body, name=name,
        out_shape=(*sems, pltpu.HBM(land.shape, land.dtype), jax.ShapeDtypeStruct((8, 128), _F32)),
        in_specs=[hbm],
        out_specs=(*([sem] * (2 * ns)), hbm, pl.BlockSpec(memory_space=pltpu.VMEM)),
        input_output_aliases={0: 2 * ns},
        compiler_params=pltpu.CompilerParams(has_side_effects=pltpu.SideEffectType.DATAFLOW_SIDE_EFFECTING),
    )(land)
    res = list(res)
    return res[:ns], res[ns:2 * ns], res[2 * ns], res[-1]


def _relay_wait(started, name):
    send_sems, recv_sems, land, token = started
    ns = len(send_sems)
    hbm = pl.BlockSpec(memory_space=pltpu.HBM)
    sem = pl.BlockSpec(memory_space=pltpu.SEMAPHORE)

    def body(*refs):
        land_ref = refs[0]
        s_sems, r_sems = refs[1:1 + ns], refs[1 + ns:1 + 2 * ns]
        for j in range(ns):
            _relay_copy(land_ref, s_sems, r_sems, j, False).wait_send()
            _relay_copy(land_ref, s_sems, r_sems, j, True).wait_recv()

    return pl.pallas_call(
        body, name=name, out_shape=pltpu.HBM(land.shape, land.dtype),
        in_specs=[hbm] + [sem] * (2 * ns) + [pl.BlockSpec(memory_space=pl.ANY)],
        out_specs=hbm, input_output_aliases={0: 0},
        compiler_params=pltpu.CompilerParams(has_side_effects=pltpu.SideEffectType.DATAFLOW_SIDE_EFFECTING),
    )(land, *send_sems, *recv_sems, token)


def _mod_part(c_all, w_ada, b_my):
    def body(c_ref, w_ref, b_ref, o_ref):
        o_ref[...] = jnp.dot(_silu(c_ref[...]), w_ref[...], precision=_HI,
                             preferred_element_type=_F32) + b_ref[...]

    return pl.pallas_call(
        body, name="mod_part",
        out_shape=jax.ShapeDtypeStruct((N_DEV, w_ada.shape[1]), _F32),
        compiler_params=_cparams(),
    )(c_all, w_ada, b_my)


def _inproj_fwd(x, scale1p, shift, norm_w, w_in_bf, b_in_pad):
    T = x.shape[1]
    tm = 512
    n_q = 3 * NA_W

    def body(x_ref, sc_ref, sh_ref, nw_ref, w_ref, b_ref, qkv_ref, rest_ref, h_ref):
        xv = x_ref[...]
        r = lax.rsqrt(jnp.mean(xv * xv, axis=-1, keepdims=True) + EPS)
        h = xv * r * nw_ref[...] * sc_ref[...] + sh_ref[...]
        hb = h.astype(_BF16)
        h_ref[...] = h.T.astype(_BF16)
        for n0 in range(0, IN_PAD, 512):
            wd = min(512, IN_PAD - n0)
            acc = _nn(hb, w_ref[:, n0:n0 + wd]) + b_ref[:, n0:n0 + wd]
            if n0 == 0:
                acc = acc * (NA_HEAD_DIM ** -0.5)
            if n0 < n_q:
                qkv_ref[:, n0:n0 + wd] = acc.astype(_BF16)
            else:
                rest_ref[:, n0 - n_q:n0 - n_q + wd] = acc

    row = lambda w: pl.BlockSpec((1, w), lambda i: (0, 0))
    return pl.pallas_call(
        body, name="inproj_fwd", grid=(T // tm,),
        in_specs=[pl.BlockSpec((None, tm, D_MODEL), lambda i: (0, i, 0)), row(D_MODEL), row(D_MODEL), row(D_MODEL),
                  pl.BlockSpec((D_MODEL, IN_PAD), lambda i: (0, 0), pipeline_mode=pl.Buffered(1)), row(IN_PAD)],
        out_specs=(pl.BlockSpec((tm, n_q), lambda i: (i, 0)),
                   pl.BlockSpec((tm, REST_W), lambda i: (i, 0)),
                   pl.BlockSpec((D_MODEL, tm), lambda i: (0, i))),
        out_shape=(jax.ShapeDtypeStruct((T, n_q), _BF16),
                   jax.ShapeDtypeStruct((T, REST_W), _F32),
                   jax.ShapeDtypeStruct((D_MODEL, T), _BF16)),
        compiler_params=_cparams(("arbitrary",)),
    )(x, scale1p, shift, norm_w, w_in_bf, b_in_pad)


def _na_class_rows(rows):
    nb = rows // NA_RB
    out = []
    for rb in (0, min(1, nb - 1), nb - 1):
        ws = int(np.clip(NA_RB * rb - 4, 0, rows - NA_WIN))
        out.append((NA_RB * rb + np.arange(NA_RB), ws + np.arange(NA_WIN)))
    return out


def _na_pair_index(rows, qrows, krows):
    start = lambda r: np.clip(r - NA_KH // 2, 0, rows - NA_KH)
    col = np.arange(GRID_W)
    cstart = np.clip(col - NA_KW // 2, 0, GRID_W - NA_KW)
    dy = krows[None, :] - qrows[:, None] + NA_KH - 1
    vr = (krows[None, :] >= start(qrows)[:, None]) & (krows[None, :] < start(qrows)[:, None] + NA_KH)
    dx = np.clip(col[None, :] - col[:, None], -(NA_KW - 1), NA_KW - 1) + NA_KW - 1
    vc = (col[None, :] >= cstart[:, None]) & (col[None, :] < cstart[:, None] + NA_KW)
    nq, nk = len(qrows), len(krows)
    dy4 = np.broadcast_to(np.clip(dy, 0, 2 * NA_KH - 2)[:, None, :, None], (nq, GRID_W, nk, GRID_W))
    dx4 = np.broadcast_to(dx[None, :, None, :], (nq, GRID_W, nk, GRID_W))
    valid = vr[:, None, :, None] & vc[None, :, None, :]
    idx = (dy4 * (2 * NA_KW - 1) + dx4).reshape(nq * GRID_W, nk * GRID_W)
    return idx.astype(np.int32), valid.reshape(nq * GRID_W, nk * GRID_W), (dy, vr, dx, vc)


def _na_half_slabs(rpb):
    _, _, (_, _, dx, vc) = _na_pair_index(NA_WIN, np.arange(1), np.arange(1))
    qc, kc = np.meshgrid(np.arange(GRID_W), np.arange(GRID_W), indexing="ij")
    consts = []
    for right in (False, True):
        pos = (qc * 128 + (GRID_W if right else 0) + kc).reshape(-1)
        oh = np.zeros((32, GRID_W * 128), np.float32)
        oh[dx[qc, kc].reshape(-1), pos] = 1.0
        col_neg = np.zeros((1, GRID_W * 128), np.float32)
        col_neg[0, pos] = np.where(vc[qc, kc].reshape(-1), 0.0, NEG)
        half = np.zeros((1, GRID_W * 128), np.float32)
        half[0, pos] = 1.0
        consts += [jnp.asarray(oh), jnp.asarray(col_neg), jnp.asarray(half)]
    row_neg = np.where(np.arange(NA_HEADS * 16) % 16 == 15, NEG, 0.0).astype(np.float32).reshape(-1, 1)
    rp = jnp.pad(rpb, ((0, 0), (0, 1), (0, 1))).reshape(NA_HEADS * 16, 32)

    def body(*refs):
        r_ref, rn_ref = refs[0], refs[1]
        for t in range(2):
            oh_ref, cn_ref, half_ref = refs[2 + 3 * t:5 + 3 * t]
            refs[8 + t][...] = (jnp.dot(r_ref[...], oh_ref[...], precision=_HI, preferred_element_type=_F32)
                                + cn_ref[...] + rn_ref[...] * half_ref[...])

    outs = pl.pallas_call(
        body, name="na_half_slabs",
        out_shape=tuple(jax.ShapeDtypeStruct((NA_HEADS * 16, GRID_W * 128), _F32) for _ in range(2)),
        compiler_params=_cparams(),
    )(rp, jnp.asarray(row_neg), *consts)
    return [o.reshape(NA_HEADS, 16, GRID_W, 128) for o in outs]


def _na_bias_tables(rpb, rows):
    left, right = _na_half_slabs(rpb)
    didx = []
    for blk, win in _na_class_rows(rows):
        _, _, (dy, vr, _, _) = _na_pair_index(rows, blk, win)
        didx.append(np.where(vr, dy, 15))

    def body(l_ref, r_ref, b_ref):
        for ci, tab in enumerate(didx):
            for a in range(NA_RB):
                for j in range(NA_WIN // 2):
                    b_ref[ci, 0, a * GRID_W:(a + 1) * GRID_W, j * 128:(j + 1) * 128] = (
                        l_ref[0, int(tab[a, 2 * j])] + r_ref[0, int(tab[a, 2 * j + 1])])

    slab = pl.BlockSpec((1, 16, GRID_W, 128), lambda h: (h, 0, 0, 0))
    return pl.pallas_call(
        body, name="na_tables", grid=(NA_HEADS,),
        in_specs=[slab] * 2,
        out_specs=pl.BlockSpec((3, 1, NA_RB * GRID_W, NA_WIN * GRID_W), lambda h: (0, h, 0, 0)),
        out_shape=jax.ShapeDtypeStruct((3, NA_HEADS, NA_RB * GRID_W, NA_WIN * GRID_W), _F32),
        compiler_params=_cparams(("arbitrary",)),
    )(left, right)


def _stack_heads(x, first):
    zero = jnp.zeros_like(x)
    return jnp.concatenate([jnp.where(first, x, zero), jnp.where(first, zero, x)], axis=0)


def _na_sub(rb, u, rows):
    sb = NA_SUB * rb + u
    nb = rows // NA_RB
    cls = jnp.where(sb == 0, 0, jnp.where(sb == nb - 1, 2, 1))
    ws = pl.multiple_of(jnp.clip(NA_RB * sb - 4, 0, rows - NA_WIN) * GRID_W, 256)
    return cls, ws


def _na_fwd(qkv, bias, T):
    rows = T // GRID_W
    tq = NA_RB * GRID_W
    tw = NA_WIN * GRID_W
    ts = NA_SUB * tq

    def body(q_ref, k_ref, v_ref, b_ref, o_ref, l_ref):
        rb = pl.program_id(1)
        lane = lax.broadcasted_iota(jnp.int32, (1, 128), 1)
        first = lane < NA_HEAD_DIM
        for u in range(NA_SUB):
            cls, ws = _na_sub(rb, u, rows)
            kw = k_ref[pl.ds(ws, tw), :]
            vw = v_ref[pl.ds(ws, tw), :]
            q2 = _stack_heads(q_ref[u * tq:(u + 1) * tq, :], first)
            s = _nt(q2, kw) + b_ref[cls].reshape(2 * tq, tw)
            m = jnp.max(s, axis=1, keepdims=True)
            p = jnp.exp(s - m)
            l = jnp.sum(p, axis=1, keepdims=True)
            o2 = _nn(p.astype(_BF16), vw) / l
            lse2 = m + jnp.log(l)
            o_ref[u * tq:(u + 1) * tq, :] = jnp.where(first, o2[:tq], o2[tq:])
            l_ref[u * tq:(u + 1) * tq, :] = jnp.where(first, lse2[:tq], lse2[tq:])

    blk = lambda off: pl.BlockSpec((ts, 128), lambda hp, rb: (rb, off + hp))
    whole = lambda off: pl.BlockSpec((T, 128), lambda hp, rb: (0, off + hp))
    return pl.pallas_call(
        body, name="na_fwd", grid=(NA_HEADS // 2, T // ts),
        in_specs=[blk(0), whole(4), whole(8),
                  pl.BlockSpec((3, 2, tq, tw), lambda hp, rb: (0, hp, 0, 0))],
        out_specs=(blk(0), blk(0)),
        out_shape=(jax.ShapeDtypeStruct((T, NA_W), _F32), jax.ShapeDtypeStruct((T, NA_W), _F32)),
        compiler_params=_cparams(("arbitrary", "arbitrary")),
    )(qkv, qkv, qkv, bias)


def _na_bwd(qkv, bias, o, d_o, lse, T):
    rows = T // GRID_W
    tq = NA_RB * GRID_W
    tw = NA_WIN * GRID_W
    ts = NA_SUB * tq

    def body(q_ref, k_ref, v_ref, b_ref, o_ref, do_ref, l_ref, dq_ref, dk_ref, dv_ref, db_ref):
        rb = pl.program_id(1)
        lane = lax.broadcasted_iota(jnp.int32, (1, 128), 1)
        first = lane < NA_HEAD_DIM

        @pl.when(rb == 0)
        def _():
            db_ref[...] = jnp.zeros_like(db_ref)
            dk_ref[...] = jnp.zeros_like(dk_ref)
            dv_ref[...] = jnp.zeros_like(dv_ref)

        for u in range(NA_SUB):
            cls, ws = _na_sub(rb, u, rows)
            kw = k_ref[pl.ds(ws, tw), :]
            vw = v_ref[pl.ds(ws, tw), :]
            sl = slice(u * tq, (u + 1) * tq)
            q = q_ref[sl, :]
            d_ov = do_ref[sl, :]
            prod = d_ov.astype(_F32) * o_ref[sl, :]
            lse_v = l_ref[sl, :]
            dqs = []
            dk_win = jnp.zeros((tw, 128), _F32)
            dv_win = jnp.zeros((tw, 128), _F32)
            for hh in range(2):
                msk = first if hh == 0 else jnp.logical_not(first)
                c0 = hh * NA_HEAD_DIM
                qm = jnp.where(msk, q, jnp.zeros_like(q))
                dom = jnp.where(msk, d_ov, jnp.zeros_like(d_ov))
                s = _nt(qm, kw) + b_ref[cls, hh]
                p = jnp.exp(s - lse_v[:, c0:c0 + 1])
                dp = _nt(dom, vw)
                delta = jnp.sum(jnp.where(msk, prod, 0.0), axis=1, keepdims=True)
                ds = p * (dp - delta)
                db_ref[cls, hh] += ds
                dsb = ds.astype(_BF16)
                dqs.append(_nn(dsb, kw) * (NA_HEAD_DIM ** -0.5))
                dk_win = dk_win + _tn(dsb, qm)
                dv_win = dv_win + _tn(p.astype(_BF16), dom)
            dq_ref[sl, :] = jnp.where(first, dqs[0], dqs[1]).astype(_BF16)
            dk_ref[pl.ds(ws, tw), :] += dk_win
            dv_ref[pl.ds(ws, tw), :] += dv_win

    once = pl.Buffered(1)
    blk = lambda off: pl.BlockSpec((ts, 128), lambda hp, rb: (rb, off + hp))
    whole = lambda off: pl.BlockSpec((T, 128), lambda hp, rb: (0, off + hp), pipeline_mode=once)
    tab = pl.BlockSpec((3, 2, tq, tw), lambda hp, rb: (0, hp, 0, 0), pipeline_mode=once)
    return pl.pallas_call(
        body, name="na_bwd", grid=(NA_HEADS // 2, T // ts),
        in_specs=[blk(0), whole(4), whole(8), tab, blk(0), blk(0), blk(0)],
        out_specs=(blk(0), whole(0), whole(0), tab),
        out_shape=(jax.ShapeDtypeStruct((T, NA_W), _BF16), jax.ShapeDtypeStruct((T, NA_W), _F32),
                   jax.ShapeDtypeStruct((T, NA_W), _F32), jax.ShapeDtypeStruct(bias.shape, _F32)),
        compiler_params=_cparams(("arbitrary", "arbitrary")),
    )(qkv, qkv, qkv, bias, o, d_o, lse)


def _rpb_grad(dbias, rows):
    tw = NA_WIN * GRID_W
    lanes = 16 * GRID_W
    offs = [int(win[0] - blk[0] + NA_KH - 1) for blk, win in _na_class_rows(rows)]

    def body(x_ref, g_ref):
        sub = lax.broadcasted_iota(jnp.int32, (NA_RB, 1), 0)
        qc = lax.broadcasted_iota(jnp.int32, (NA_RB * GRID_W, 1), 0) % GRID_W
        tot = jnp.zeros((NA_RB, lanes), _F32)
        for ci in range(3):
            xv = x_ref[ci, 0]
            for bit in range(6):
                xv = jnp.where(((qc >> bit) & 1) == 1, pltpu.roll(xv, tw - (1 << bit), 1), xv)
            acc = pltpu.roll(jnp.sum(xv.reshape(NA_RB, GRID_W, tw), axis=1), NA_KW, 1)
            acc = jnp.concatenate([acc, jnp.zeros((NA_RB, lanes - tw), _F32)], axis=1)
            for a in range(NA_RB):
                tot = tot + jnp.where(sub == a, pltpu.roll(acc, (GRID_W * (offs[ci] - a)) % lanes, 1), 0.0)
        g_ref[0] = jnp.broadcast_to(jnp.sum(tot, axis=0, keepdims=True), (8, lanes))

    g = pl.pallas_call(
        body, name="rpb_grad", grid=(NA_HEADS,),
        in_specs=[pl.BlockSpec((3, 1) + dbias.shape[2:], lambda h: (0, h, 0, 0))],
        out_specs=pl.BlockSpec((1, 8, lanes), lambda h: (h, 0, 0)),
        out_shape=jax.ShapeDtypeStruct((NA_HEADS, 8, lanes), _F32),
        compiler_params=_cparams(("arbitrary",)),
    )(dbias)
    return g[:, 0].reshape(NA_HEADS, 16, GRID_W)[:, :2 * NA_KH - 1, 1:2 * NA_KW]


def _halo_specs(tm, width, col_of, T, order):
    hb = tm // 8
    last = T // 8 - 1
    if order == "ij":
        cur = pl.BlockSpec((tm, width), lambda i, j: (i, col_of(j)))
        prev = pl.BlockSpec((8, width), lambda i, j: (jnp.maximum(i * hb - 1, 0), col_of(j)))
        nxt = pl.BlockSpec((8, width), lambda i, j: (jnp.minimum((i + 1) * hb, last), col_of(j)))
    else:
        cur = pl.BlockSpec((tm, width), lambda j, i: (i, col_of(j)))
        prev = pl.BlockSpec((8, width), lambda j, i: (jnp.maximum(i * hb - 1, 0), col_of(j)))
        nxt = pl.BlockSpec((8, width), lambda j, i: (jnp.minimum((i + 1) * hb, last), col_of(j)))
    return [prev, cur, nxt]


def _extend(prev_ref, cur_ref, next_ref, i, n_i):
    prev = jnp.where(i > 0, prev_ref[...], 0.0)
    nxt = jnp.where(i < n_i - 1, next_ref[...], 0.0)
    return jnp.concatenate([prev, cur_ref[...], nxt], axis=0)


def _conv_fwd(rest, conv_w, conv_b, T):
    tm = 512
    n_i = T // tm
    n = tm + 16

    def body(p_ref, c_ref, n_ref, w_ref, b_ref, o_ref):
        i = pl.program_id(0)
        ext = _extend(p_ref, c_ref, n_ref, i, n_i)
        acc = jnp.zeros((tm, 512), _F32) + b_ref[...]
        for j in range(CONV_W):
            acc = acc + w_ref[j:j + 1, :] * pltpu.roll(ext, (2 - j) % n, 0)[8:8 + tm]
        o_ref[...] = _silu(acc)

    return pl.pallas_call(
        body, name="conv_fwd", grid=(n_i, 2),
        in_specs=_halo_specs(tm, 512, lambda j: 1 + j, T, "ij")
        + [pl.BlockSpec((8, 512), lambda i, j: (0, j)), pl.BlockSpec((1, 512), lambda i, j: (0, j))],
        out_specs=pl.BlockSpec((tm, 512), lambda i, j: (i, j)),
        out_shape=jax.ShapeDtypeStruct((T, 2 * ML_W), _F32),
        compiler_params=_cparams(("arbitrary", "arbitrary")),
    )(rest, rest, rest, conv_w, conv_b)


def _conv_bwd(rest, conv_w, conv_b, da_f, da_b, T):
    tm = 512
    n_i = T // tm
    n = tm + 16

    def body(up, uc, un, fp, fc, fn, bp, bc, bn, w_ref, b_ref, du_ref, dw_ref):
        i = pl.program_id(1)
        ext_u = _extend(up, uc, un, i, n_i)
        ext_da = _extend(fp, fc, fn, i, n_i) + _extend(bp, bc, bn, i, n_i)
        shifted = [pltpu.roll(ext_u, (2 - j) % n, 0) for j in range(CONV_W)]
        pre = jnp.zeros((n, 512), _F32) + b_ref[...]
        for j in range(CONV_W):
            pre = pre + w_ref[j:j + 1, :] * shifted[j]
        gidx = i * tm - 8 + lax.broadcasted_iota(jnp.int32, (n, 1), 0)
        dpre = jnp.where((gidx >= 0) & (gidx < T), ext_da * _dsilu(pre), 0.0)
        du = jnp.zeros((tm, 512), _F32)
        for j in range(CONV_W):
            du = du + w_ref[j:j + 1, :] * pltpu.roll(dpre, (j - 2) % n, 0)[8:8 + tm]
        du_ref[...] = du.astype(_BF16)
        dpc = dpre[8:8 + tm]
        parts = [jnp.sum(dpc * shifted[j][8:8 + tm], axis=0, keepdims=True) for j in range(CONV_W)]
        parts.append(jnp.sum(dpc, axis=0, keepdims=True))
        parts.append(jnp.zeros((2, 512), _F32))
        upd = jnp.concatenate(parts, axis=0)

        @pl.when(i == 0)
        def _():
            dw_ref[...] = upd

        @pl.when(i > 0)
        def _():
            dw_ref[...] += upd

    return pl.pallas_call(
        body, name="conv_bwd", grid=(2, n_i),
        in_specs=_halo_specs(tm, 512, lambda j: 1 + j, T, "ji")
        + _halo_specs(tm, 512, lambda j: j, T, "ji") + _halo_specs(tm, 512, lambda j: j, T, "ji")
        + [pl.BlockSpec((8, 512), lambda j, i: (0, j)), pl.BlockSpec((1, 512), lambda j, i: (0, j))],
        out_specs=(pl.BlockSpec((tm, 512), lambda j, i: (i, j)), pl.BlockSpec((8, 512), lambda j, i: (0, j))),
        out_shape=(jax.ShapeDtypeStruct((T, 2 * ML_W), _BF16), jax.ShapeDtypeStruct((8, 2 * ML_W), _F32)),
        compiler_params=_cparams(("arbitrary", "arbitrary")),
    )(rest, rest, rest, da_f, da_f, da_f, da_b, da_b, da_b, conv_w, conv_b)


def _scan_rows(x, suffix):
    L = x.shape[0]
    row = lax.broadcasted_iota(jnp.int32, (L, 1), 0)
    step = 1
    while step < L:
        if suffix:
            x = x + jnp.where(row < L - step, pltpu.roll(x, L - step, 0), 0.0)
        else:
            x = x + jnp.where(row >= step, pltpu.roll(x, step, 0), 0.0)
        step *= 2
    return x


def _ml_gates(gt, rev):
    L = gt.shape[0]
    ri = lax.broadcasted_iota(jnp.int32, (L, L), 0)
    ci = lax.broadcasted_iota(jnp.int32, (L, L), 1)
    mask = (ci >= ri) if rev else (ci <= ri)
    lf = jnp.minimum(gt, 0.0) - jnp.log(1.0 + jnp.exp(-jnp.abs(gt)))
    b = _scan_rows(lf, suffix=rev)
    return mask, b, b.T, gt.T


def _ml_head_gates(gt, gates, head, rev):
    _, b, b_t, gt_t = gates
    ci = (8 if rev else 0) + head
    cf = ci + ML_HEADS
    last = 0 if rev else gt.shape[0] - 1
    return dict(icol=gt[:, ci:ci + 1], b_col=b[:, cf:cf + 1], b_row=b_t[cf:cf + 1, :],
                i_row=gt_t[ci:ci + 1, :], bl=b[last:last + 1, cf:cf + 1])


def _ml_chunk(q, k, v, hg, mask, C, n, m, saved=None):
    icol, b_col, b_row, bl = hg["icol"], hg["b_col"], hg["b_row"], hg["bl"]
    if saved is None:
        dlog = jnp.where(mask, b_col - b_row + hg["i_row"], NEG)
        m_t = jnp.maximum(b_col + m, jnp.max(dlog, axis=1, keepdims=True))
        dm = jnp.exp(dlog - m_t)
    else:
        dm, m_t = saved[0].astype(_F32), saved[1]
    ks = k * (ML_HEAD_DIM ** -0.5)
    qb, kb, vb = q.astype(_BF16), ks.astype(_BF16), v.astype(_BF16)
    s = _nt(qb, kb) * dm
    g = jnp.exp(b_col + m - m_t)
    qc = _nt(qb, C.astype(_BF16))
    num = _nn(s.astype(_BF16), vb) + g * qc
    qn = jnp.sum(q * n, axis=1, keepdims=True)
    den = jnp.sum(s, axis=1, keepdims=True) + g * qn
    e_m = jnp.exp(-m_t)
    nrm = jnp.maximum(jnp.abs(den), e_m)
    h = num / nrm
    a_col = bl - b_col + icol
    m_new = jnp.maximum(bl + m, jnp.max(a_col, axis=0, keepdims=True))
    decay = jnp.exp(bl + m - m_new)
    w = jnp.exp(a_col - m_new)
    c_new = decay * C + _tn((w * v).astype(_BF16), kb)
    n_new = decay * n + jnp.sum(w * ks, axis=0, keepdims=True)
    aux = dict(dm=dm, m_t=m_t, ks=ks, qb=qb, kb=kb, vb=vb, s=s, g=g, qc=qc, qn=qn,
               den=den, e_m=e_m, nrm=nrm, decay=decay, w=w)
    return h, c_new, n_new, m_new, aux


def _mlstm_fwd(qk_act, rest, T, rev):
    tb = ML_CB * ML_CHUNK
    nblk = T // tb
    nc = T // ML_CHUNK
    bi = (lambda i: nblk - 1 - i) if rev else (lambda i: i)

    def body(q_ref, k_ref, v_ref, g_ref, h_ref, cs_ref, ns_ref, ms_ref, dm_ref, mt_ref, c_scr, n_scr, m_scr):
        @pl.when(pl.program_id(0) == 0)
        def _():
            c_scr[...] = jnp.zeros_like(c_scr)
            n_scr[...] = jnp.zeros_like(n_scr)
            m_scr[...] = jnp.zeros_like(m_scr)

        def step(j, carry):
            c = (ML_CB - 1 - j) if rev else j
            r0 = pl.multiple_of(c * ML_CHUNK, ML_CHUNK)
            gt = g_ref[pl.ds(r0, ML_CHUNK), :]
            gates = _ml_gates(gt, rev)
            lane = lax.broadcasted_iota(jnp.int32, (1, 128), 1)
            mt_tile = jnp.zeros((ML_CHUNK, 128), _F32)
            for hd in range(ML_HEADS):
                cols = slice(hd * ML_HEAD_DIM, (hd + 1) * ML_HEAD_DIM)
                C = c_scr[hd]
                n = n_scr[hd:hd + 1, :]
                mrow = m_scr[hd:hd + 1, :]
                cs_ref[c, hd] = C
                ns_ref[c, hd:hd + 1, :] = n
                ms_ref[c, hd:hd + 1, :] = mrow
                h, c_new, n_new, m_new, a = _ml_chunk(
                    q_ref[pl.ds(r0, ML_CHUNK), cols], k_ref[pl.ds(r0, ML_CHUNK), cols],
                    v_ref[pl.ds(r0, ML_CHUNK), cols], _ml_head_gates(gt, gates, hd, rev), gates[0],
                    C, n, mrow[:, 0:1])
                h_ref[pl.ds(r0, ML_CHUNK), cols] = h
                dm_ref[c, hd] = a["dm"].astype(_BF16)
                mt_tile = jnp.where(lane == hd, a["m_t"], mt_tile)
                c_scr[hd] = c_new
                n_scr[hd:hd + 1, :] = n_new
                m_scr[hd:hd + 1, :] = jnp.broadcast_to(m_new, (1, 128))
            mt_ref[c] = mt_tile
            return carry

        lax.fori_loop(0, ML_CB, step, 0)

    return pl.pallas_call(
        body, name="mlstm_fwd_rev" if rev else "mlstm_fwd", grid=(nblk,),
        in_specs=[pl.BlockSpec((tb, ML_W), lambda i: (bi(i), 0)),
                  pl.BlockSpec((tb, ML_W), lambda i: (bi(i), 1)),
                  pl.BlockSpec((tb, ML_W), lambda i: (bi(i), 3)),
                  pl.BlockSpec((tb, 128), lambda i: (bi(i), GATE_COL // 128))],
        out_specs=(pl.BlockSpec((tb, ML_W), lambda i: (bi(i), 0)),
                   pl.BlockSpec((ML_CB, ML_HEADS, 128, 128), lambda i: (bi(i), 0, 0, 0)),
                   pl.BlockSpec((ML_CB, ML_HEADS, 128), lambda i: (bi(i), 0, 0)),
                   pl.BlockSpec((ML_CB, ML_HEADS, 128), lambda i: (bi(i), 0, 0)),
                   pl.BlockSpec((ML_CB, ML_HEADS, ML_CHUNK, ML_CHUNK), lambda i: (bi(i), 0, 0, 0)),
                   pl.BlockSpec((ML_CB, ML_CHUNK, 128), lambda i: (bi(i), 0, 0))),
        out_shape=(jax.ShapeDtypeStruct((T, ML_W), _F32),
                   jax.ShapeDtypeStruct((nc, ML_HEADS, 128, 128), _F32),
                   jax.ShapeDtypeStruct((nc, ML_HEADS, 128), _F32),
                   jax.ShapeDtypeStruct((nc, ML_HEADS, 128), _F32),
                   jax.ShapeDtypeStruct((nc, ML_HEADS, ML_CHUNK, ML_CHUNK), _BF16),
                   jax.ShapeDtypeStruct((nc, ML_CHUNK, 128), _F32)),
        scratch_shapes=[pltpu.VMEM((ML_HEADS, 128, 128), _F32), pltpu.VMEM((8, 128), _F32),
                        pltpu.VMEM((8, 128), _F32)],
        compiler_params=_cparams(("arbitrary",)),
    )(qk_act, qk_act, rest, rest)


def _mlstm_bwd(qk_act, rest, d_h, saved, T, rev):
    tb = ML_CB * ML_CHUNK
    nblk = T // tb
    bi = (lambda i: i) if rev else (lambda i: nblk - 1 - i)

    def body(q_ref, k_ref, v_ref, g_ref, dh_ref, cs_ref, ns_ref, ms_ref, dm_ref, mt_ref,
             dqk_ref, dv_ref, dg_ref, dc_scr, dn_scr):
        @pl.when(pl.program_id(0) == 0)
        def _():
            dc_scr[...] = jnp.zeros_like(dc_scr)
            dn_scr[...] = jnp.zeros_like(dn_scr)

        def step(j, carry):
            c = j if rev else (ML_CB - 1 - j)
            r0 = pl.multiple_of(c * ML_CHUNK, ML_CHUNK)
            gt = g_ref[pl.ds(r0, ML_CHUNK), :]
            gates = _ml_gates(gt, rev)
            mask = gates[0]
            lane = lax.broadcasted_iota(jnp.int32, (1, 128), 1)
            sub = lax.broadcasted_iota(jnp.int32, (128, 1), 0)
            db_t = jnp.zeros((ML_CHUNK, 128), _F32)
            da_t = jnp.zeros((ML_CHUNK, 128), _F32)
            cs_rows = jnp.zeros((128, ML_CHUNK), _F32)
            dbl_t = jnp.zeros((1, 128), _F32)
            for hd in range(ML_HEADS):
                cols = slice(hd * ML_HEAD_DIM, (hd + 1) * ML_HEAD_DIM)
                ci = (8 if rev else 0) + hd
                cf = ci + ML_HEADS
                q = q_ref[pl.ds(r0, ML_CHUNK), cols]
                k = k_ref[pl.ds(r0, ML_CHUNK), cols]
                v = v_ref[pl.ds(r0, ML_CHUNK), cols]
                C = cs_ref[c, hd]
                n = ns_ref[c, hd:hd + 1, :]
                m = ms_ref[c, hd:hd + 1, :][:, 0:1]
                dcn = dc_scr[hd]
                dnn = dn_scr[hd:hd + 1, :]
                h, _, _, _, a = _ml_chunk(q, k, v, _ml_head_gates(gt, gates, hd, rev), mask, C, n, m,
                                          saved=(dm_ref[c, hd], mt_ref[c][:, hd:hd + 1]))
                d_hv = dh_ref[pl.ds(r0, ML_CHUNK), cols]
                g, s, w, ks = a["g"], a["s"], a["w"], a["ks"]
                qb, kb, vb = a["qb"], a["kb"], a["vb"]
                dnum = d_hv / a["nrm"]
                hdot = jnp.sum(d_hv * h, axis=1, keepdims=True)
                dden = jnp.where(jnp.abs(a["den"]) >= a["e_m"], -hdot / a["nrm"] * jnp.sign(a["den"]), 0.0)
                dnb = dnum.astype(_BF16)
                d_s = _nt(dnb, vb) + dden
                r = d_s * s
                dsqk = (d_s * a["dm"]).astype(_BF16)
                cb = C.astype(_BF16)
                dq = _nn(dsqk, kb) + g * _nn(dnb, cb) + (dden * g) * n
                dk = _tn(dsqk, qb)
                dv = _tn(s.astype(_BF16), dnb)
                dg = jnp.sum(dnum * a["qc"], axis=1, keepdims=True) + dden * a["qn"]
                db_col = jnp.sum(r, axis=1, keepdims=True) + dg * g
                cs_rows = cs_rows + jnp.where((sub == ci) | (sub == cf), jnp.sum(r, axis=0, keepdims=True), 0.0)
                dc_chunk = _tn((g * dnum).astype(_BF16), qb)
                dn_chunk = jnp.sum((dden * g) * q, axis=0, keepdims=True)
                dcb = dcn.astype(_BF16)
                vdc = _nn(vb, dcb)
                kdc = _nt(kb, dcb)
                dw = jnp.sum(vdc * ks, axis=1, keepdims=True) + jnp.sum(ks * dnn, axis=1, keepdims=True)
                dv = dv + w * kdc
                dk = dk + w * vdc + w * dnn
                da = dw * w
                ddecay = (jnp.sum(jnp.sum(dcn * C, axis=1, keepdims=True), axis=0, keepdims=True)
                          + jnp.sum(dnn * n, axis=1, keepdims=True))
                dbl = ddecay * a["decay"] + jnp.sum(da, axis=0, keepdims=True)
                db_t = db_t + jnp.where(lane == cf, db_col - da, 0.0)
                da_t = da_t + jnp.where(lane == ci, da, 0.0)
                dbl_t = dbl_t + jnp.where(lane == cf, dbl, 0.0)
                dc_scr[hd] = dc_chunk + a["decay"] * dcn
                dn_scr[hd:hd + 1, :] = dn_chunk + a["decay"] * dnn
                dqk_ref[pl.ds(r0, ML_CHUNK), cols] = dq
                dqk_ref[pl.ds(r0, ML_CHUNK), slice(ML_W + hd * 128, ML_W + (hd + 1) * 128)] = dk * (ML_HEAD_DIM ** -0.5)
                dv_ref[pl.ds(r0, ML_CHUNK), cols] = dv.astype(_BF16)
            lo = 8 if rev else 0
            is_i = (lane >= lo) & (lane < lo + ML_HEADS)
            is_f = (lane >= lo + ML_HEADS) & (lane < lo + 2 * ML_HEADS)
            cs_t = cs_rows.T
            db_all = db_t - jnp.where(is_f, cs_t, 0.0)
            dlf = _scan_rows(db_all, suffix=not rev) + dbl_t
            dg_ref[pl.ds(r0, ML_CHUNK), :] = (da_t + jnp.where(is_i, cs_t, 0.0)
                                               + jnp.where(is_f, dlf * _sigmoid(-gt), 0.0))
            return carry

        lax.fori_loop(0, ML_CB, step, 0)

    return pl.pallas_call(
        body, name="mlstm_bwd_rev" if rev else "mlstm_bwd", grid=(nblk,),
        in_specs=[pl.BlockSpec((tb, ML_W), lambda i: (bi(i), 0)),
                  pl.BlockSpec((tb, ML_W), lambda i: (bi(i), 1)),
                  pl.BlockSpec((tb, ML_W), lambda i: (bi(i), 3)),
                  pl.BlockSpec((tb, 128), lambda i: (bi(i), GATE_COL // 128)),
                  pl.BlockSpec((tb, ML_W), lambda i: (bi(i), 0)),
                  pl.BlockSpec((ML_CB, ML_HEADS, 128, 128), lambda i: (bi(i), 0, 0, 0)),
                  pl.BlockSpec((ML_CB, ML_HEADS, 128), lambda i: (bi(i), 0, 0)),
                  pl.BlockSpec((ML_CB, ML_HEADS, 128), lambda i: (bi(i), 0, 0)),
                  pl.BlockSpec((ML_CB, ML_HEADS, ML_CHUNK, ML_CHUNK), lambda i: (bi(i), 0, 0, 0)),
                  pl.BlockSpec((ML_CB, ML_CHUNK, 128), lambda i: (bi(i), 0, 0))],
        out_specs=(pl.BlockSpec((tb, 2 * ML_W), lambda i: (bi(i), 0)),
                   pl.BlockSpec((tb, ML_W), lambda i: (bi(i), 0)),
                   pl.BlockSpec((tb, 128), lambda i: (bi(i), 0))),
        out_shape=(jax.ShapeDtypeStruct((T, 2 * ML_W), _F32), jax.ShapeDtypeStruct((T, ML_W), _BF16),
                   jax.ShapeDtypeStruct((T, 128), _F32)),
        scratch_shapes=[pltpu.VMEM((ML_HEADS, 128, 128), _F32), pltpu.VMEM((8, 128), _F32)],
        compiler_params=_cparams(("arbitrary",)),
    )(qk_act, qk_act, rest, rest, d_h, *saved)


def _post(x, target, o_na, rest, h_f, h_b, gate, ml_norm_w, final_w, w_out_bf, T):
    tm = 256
    n_i = T // tm

    def body(x_ref, t_ref, o_ref, zna_ref, hf_ref, hb_ref, mo_ref, mz_ref, gate_ref, mw_ref, fw_ref, w_ref,
             dx1_ref, do_ref, dzna_ref, dh_ref, dmo_ref, dmz_ref, dwo_ref, vec_ref):
        i = pl.program_id(0)
        gate_v = gate_ref[...]
        fw = fw_ref[...]
        zna = zna_ref[...]
        o = o_ref[...]
        sig_zna = _sigmoid(zna)
        silu_zna = zna * sig_zna
        na_out = o * silu_zna
        hsum = hf_ref[...] + hb_ref[...]
        sg = _sigmoid(mo_ref[...])
        hm = hsum * sg
        mz = mz_ref[...]
        sig_mz = _sigmoid(mz)
        smz = mz * sig_mz
        dsilu_mz = sig_mz * (1.0 + mz * (1.0 - sig_mz))
        hn_l, rstd_l, ml_l = [], [], []
        for hd in range(ML_HEADS):
            cols = slice(hd * 128, (hd + 1) * 128)
            hh = hm[:, cols]
            mu = jnp.mean(hh, axis=-1, keepdims=True)
            var = jnp.mean(jnp.square(hh - mu), axis=-1, keepdims=True)
            rstd = lax.rsqrt(var + EPS)
            hn = (hh - mu) * rstd
            hn_l.append(hn)
            rstd_l.append(rstd)
            ml_l.append(hn * mw_ref[:, cols] * smz[:, cols])
        mix = jnp.concatenate([na_out] + ml_l, axis=1).astype(_BF16)
        y = _nn(mix, w_ref[...])
        x1 = x_ref[...] + gate_v * y
        r = lax.rsqrt(jnp.mean(x1 * x1, axis=-1, keepdims=True) + EPS)
        xhat = x1 * r
        out = xhat * fw
        err = out - t_ref[...]
        loss = 0.5 * jnp.sum(jnp.sum(err * err, axis=1, keepdims=True), axis=0, keepdims=True) / D_MODEL
        dout = err * (1.0 / D_MODEL)
        dfw = jnp.sum(dout * xhat, axis=0, keepdims=True)
        dxhat = dout * fw
        dx1 = r * (dxhat - xhat * jnp.mean(dxhat * xhat, axis=-1, keepdims=True))
        dx1_ref[...] = dx1
        dgate = jnp.sum(dx1 * y, axis=0, keepdims=True)
        dy = (dx1 * gate_v).astype(_BF16)
        dmix = _nt(dy, w_ref[...])
        dwo = _tn(mix, dy)
        dna = dmix[:, :NA_W]
        do_ref[...] = (dna * silu_zna).astype(_BF16)
        dzna_ref[...] = (dna * o * (sig_zna * (1.0 + zna * (1.0 - sig_zna)))).astype(_BF16)
        dmw_l = []
        for hd in range(ML_HEADS):
            cols = slice(hd * 128, (hd + 1) * 128)
            dml = dmix[:, NA_W + hd * 128:NA_W + (hd + 1) * 128]
            hn = hn_l[hd]
            mwv = mw_ref[:, cols]
            dmz_ref[:, cols] = (dml * hn * mwv * dsilu_mz[:, cols]).astype(_BF16)
            dhn = dml * mwv * smz[:, cols]
            dmw_l.append(jnp.sum(dml * hn * smz[:, cols], axis=0, keepdims=True))
            dhm = rstd_l[hd] * (dhn - jnp.mean(dhn, axis=-1, keepdims=True)
                                - hn * jnp.mean(dhn * hn, axis=-1, keepdims=True))
            sgc = sg[:, cols]
            dh_ref[:, cols] = dhm * sgc
            dmo_ref[:, cols] = (dhm * hsum[:, cols] * sgc * (1.0 - sgc)).astype(_BF16)
        dmw = jnp.concatenate(dmw_l + [jnp.zeros((1, D_MODEL - ML_W), _F32)], axis=1)
        lane = lax.broadcasted_iota(jnp.int32, (1, D_MODEL), 1)
        vec = jnp.concatenate([dfw, dgate, dmw, jnp.where(lane == 0, loss, 0.0),
                               jnp.zeros((4, D_MODEL), _F32)], axis=0)

        @pl.when(i == 0)
        def _():
            dwo_ref[...] = dwo
            vec_ref[...] = vec

        @pl.when(i > 0)
        def _():
            dwo_ref[...] += dwo
            vec_ref[...] += vec

    tok = lambda w, j: pl.BlockSpec((tm, w), lambda i: (i, j))
    tok3 = pl.BlockSpec((None, tm, D_MODEL), lambda i: (0, i, 0))
    row = lambda w: pl.BlockSpec((1, w), lambda i: (0, 0))
    f32 = lambda w: jax.ShapeDtypeStruct((T, w), _F32)
    bf16 = lambda w: jax.ShapeDtypeStruct((T, w), _BF16)
    return pl.pallas_call(
        body, name="post", grid=(n_i,),
        in_specs=[tok3, tok3, tok(NA_W, 0), tok(NA_W, 0), tok(ML_W, 0), tok(ML_W, 0),
                  tok(ML_W, 4), tok(ML_W, 5), row(D_MODEL), row(ML_W), row(D_MODEL),
                  pl.BlockSpec((D_MODEL, D_MODEL), lambda i: (0, 0))],
        out_specs=(tok(D_MODEL, 0), tok(NA_W, 0), tok(NA_W, 0), tok(ML_W, 0), tok(ML_W, 0), tok(ML_W, 0),
                   pl.BlockSpec((D_MODEL, D_MODEL), lambda i: (0, 0)),
                   pl.BlockSpec((8, D_MODEL), lambda i: (0, 0))),
        out_shape=(f32(D_MODEL), bf16(NA_W), bf16(NA_W), f32(ML_W), bf16(ML_W),
                   bf16(ML_W), jax.ShapeDtypeStruct((D_MODEL, D_MODEL), _F32),
                   jax.ShapeDtypeStruct((8, D_MODEL), _F32)),
        compiler_params=_cparams(("arbitrary",)),
    )(x, target, o_na, rest, h_f, h_b, rest, rest, gate, ml_norm_w, final_w, w_out_bf)


def _section_specs(sections, tm):
    specs, args = [], []
    for _, width, parts in sections:
        for arr, cb in parts:
            specs.append(pl.BlockSpec((tm, width), functools.partial(lambda i, cb: (i, cb), cb=cb)))
            args.append(arr)
    return specs, args


def _section_values(sections, refs, dtype):
    vals, at = [], 0
    for _, _, parts in sections:
        v = refs[at][...]
        for r in refs[at + 1:at + len(parts)]:
            v = v.astype(_F32) + r[...].astype(_F32)
        at += len(parts)
        vals.append(v.astype(dtype))
    return vals


def _inproj_bwd_x(x, dx1, scale1p, norm_w, w_in_bf, sections, T):
    tm = 512
    sspecs, sargs = _section_specs(sections, tm)
    ns = len(sargs)

    def body(*refs):
        x_ref, dx1_ref, sc_ref, nw_ref, w_ref = refs[:5]
        srefs = refs[5:5 + ns]
        gx_ref, vec_ref = refs[5 + ns:]
        i = pl.program_id(0)
        vals = _section_values(sections, srefs, _BF16)
        dh = jnp.zeros((tm, D_MODEL), _F32)
        for (c0, width, _), val in zip(sections, vals):
            dh = dh + _nt(val, w_ref[:, c0:c0 + width])
        xv = x_ref[...]
        r = lax.rsqrt(jnp.mean(xv * xv, axis=-1, keepdims=True) + EPS)
        xhat = xv * r
        nw = nw_ref[...]
        dshift = jnp.sum(dh, axis=0, keepdims=True)
        dscale = jnp.sum(dh * xhat * nw, axis=0, keepdims=True)
        dhpre = dh * sc_ref[...]
        dnw = jnp.sum(dhpre * xhat, axis=0, keepdims=True)
        dxhat = dhpre * nw
        gx_ref[...] = dx1_ref[...] + r * (dxhat - xhat * jnp.mean(dxhat * xhat, axis=-1, keepdims=True))
        vec = jnp.concatenate([dshift, dscale, dnw, jnp.zeros((5, D_MODEL), _F32)], axis=0)

        @pl.when(i == 0)
        def _():
            vec_ref[...] = vec

        @pl.when(i > 0)
        def _():
            vec_ref[...] += vec

    row = pl.BlockSpec((1, D_MODEL), lambda i: (0, 0))
    tok = pl.BlockSpec((tm, D_MODEL), lambda i: (i, 0))
    tok3 = pl.BlockSpec((None, tm, D_MODEL), lambda i: (0, i, 0))
    return pl.pallas_call(
        body, name="inproj_bwd_x", grid=(T // tm,),
        in_specs=[tok3, tok, row, row,
                  pl.BlockSpec((D_MODEL, IN_PAD), lambda i: (0, 0), pipeline_mode=pl.Buffered(1))] + sspecs,
        out_specs=(tok3, pl.BlockSpec((8, D_MODEL), lambda i: (0, 0))),
        out_shape=(jax.ShapeDtypeStruct((1, T, D_MODEL), _F32), jax.ShapeDtypeStruct((8, D_MODEL), _F32)),
        compiler_params=_cparams(("arbitrary",)),
    )(x, dx1, scale1p, norm_w, w_in_bf, *sargs)


def _inproj_bwd_w(h_t, sections, T):
    tm = 1024
    n_i = T // tm
    sspecs, sargs = _section_specs(sections, tm)
    ns = len(sargs)

    def body(*refs):
        h_ref = refs[0]
        srefs = refs[1:1 + ns]
        dw_ref, db_ref, acc, sem = refs[1 + ns:]
        i = pl.program_id(0)

        @pl.when(i == 0)
        def _():
            acc[...] = jnp.zeros_like(acc)
            db_ref[...] = jnp.zeros_like(db_ref)

        hv = h_ref[...]
        for (c0, width, _), v in zip(sections, _section_values(sections, srefs, _F32)):
            acc[:, c0:c0 + width] += _nn(hv, v.astype(_BF16))
            db_ref[0:1, c0:c0 + width] += jnp.sum(v, axis=0, keepdims=True)

        @pl.when(i == n_i - 1)
        def _():
            cp = pltpu.make_async_copy(acc, dw_ref, sem)
            cp.start()
            cp.wait()

    return pl.pallas_call(
        body, name="inproj_bwd_w", grid=(n_i,),
        in_specs=[pl.BlockSpec((D_MODEL, tm), lambda i: (0, i))] + sspecs,
        out_specs=(pl.BlockSpec(memory_space=pl.ANY), pl.BlockSpec((8, IN_PAD), lambda i: (0, 0))),
        out_shape=(jax.ShapeDtypeStruct((D_MODEL, IN_PAD), _F32), jax.ShapeDtypeStruct((8, IN_PAD), _F32)),
        scratch_shapes=[pltpu.VMEM((D_MODEL, IN_PAD), _F32), pltpu.SemaphoreType.DMA],
        compiler_params=_cparams(("arbitrary",)),
    )(h_t, *sargs)


def _adamw_math(w, g, m, v):
    m = ADAM_B1 * m + (1.0 - ADAM_B1) * g
    v = ADAM_B2 * v + (1.0 - ADAM_B2) * jnp.square(g)
    m_hat = m / (1.0 - ADAM_B1 ** ADAM_STEP)
    v_hat = v / (1.0 - ADAM_B2 ** ADAM_STEP)
    delta = -ADAM_LR * (m_hat / (jnp.sqrt(v_hat) + ADAM_EPS) + ADAM_WD * w)
    return delta, m, v


def _adamw_slots(w, m, v, slots, tr, name, own=None):
    R, C = w.shape
    extra = [] if own is None else [own]

    def body(w_ref, m_ref, v_ref, s_ref, *refs):
        g_ref, d_ref, nm_ref, nv_ref = refs[len(extra):]
        g = s_ref[0].astype(_F32)
        for k in range(1, N_DEV):
            g = g + s_ref[k].astype(_F32)
        if extra:
            g = g + refs[0][...].astype(_F32)
        g_ref[...] = g
        d_ref[...], nm_ref[...], nv_ref[...] = _adamw_math(w_ref[...], g, m_ref[...], v_ref[...])

    blk = pl.BlockSpec((tr, C), lambda i: (i, 0))
    return pl.pallas_call(
        body, name=name, grid=(R // tr,),
        in_specs=[blk, blk, blk, pl.BlockSpec((N_DEV, tr, C), lambda i: (0, i, 0))] + [blk] * len(extra),
        out_specs=(blk, blk, blk, blk),
        out_shape=tuple(jax.ShapeDtypeStruct((R, C), _F32) for _ in range(4)),
        compiler_params=_cparams(("arbitrary",)),
    )(w, m, v, slots, *extra)


def _w_ada_update(c_all, dmod_my, w, m, v):
    def body(c_ref, d_ref, w_ref, m_ref, v_ref, g_ref, dl_ref, nm_ref, nv_ref):
        g = lax.dot_general(_silu(c_ref[...]), d_ref[...], (((0,), (0,)), ((), ())),
                            precision=_HI, preferred_element_type=_F32)
        g_ref[...] = g
        dl_ref[...], nm_ref[...], nv_ref[...] = _adamw_math(w_ref[...], g, m_ref[...], v_ref[...])

    return pl.pallas_call(
        body, name="w_ada_update",
        out_shape=tuple(jax.ShapeDtypeStruct(w.shape, _F32) for _ in range(4)),
        compiler_params=_cparams(),
    )(c_all, dmod_my, w, m, v)


_PACK = (("b_ada", 3072, 3072), ("norm_w", 1024, 1024), ("b_in", IN_W, IN_PAD), ("conv_w", 5120, 5120),
         ("conv_b", 1024, 1024), ("rpb", 3720, 3840), ("ml_norm_w", 512, 512), ("final_norm_w", 1024, 1024),
         ("loss", 1, 128))
_PACK_OFF = {}
_off = 0
for _name, _len, _pad in _PACK:
    _PACK_OFF[_name] = (_off, _len)
    _off += _pad
_PACK_LEN = _off


def _pack(parts):
    cols = []
    for name, length, pad in _PACK:
        vec = parts[name].reshape(-1).astype(_F32)
        cols.append(jnp.pad(vec, (0, pad - length)))
    return jnp.concatenate(cols).reshape(1, _PACK_LEN)


def _unpack(vec, name, shape):
    off, length = _PACK_OFF[name]
    return vec.reshape(-1)[off:off + length].reshape(shape)


def kernel(x, c, w_ada, b_ada, norm_w, w_in, b_in, conv_w, conv_b, rpb, ml_norm_w, w_out, final_norm_w, loss_target, m_w_ada, m_b_ada, m_norm_w, m_w_in, m_b_in, m_conv_w, m_conv_b, m_rpb, m_ml_norm_w, m_w_out, m_final_norm_w, v_w_ada, v_b_ada, v_norm_w, v_w_in, v_b_in, v_conv_w, v_conv_b, v_rpb, v_ml_norm_w, v_w_out, v_final_norm_w):
    T = x.shape[1]
    rows = T // GRID_W
    me = 4 * lax.axis_index("x") + 2 * lax.axis_index("y") + lax.axis_index("c")
    n_in = w_in.shape[2]
    n_ada = w_ada.shape[2]
    n_cw = conv_w.shape[2]
    n_wo = w_out.shape[1]

    w_in_my, w_out_my = w_in[0].astype(_BF16), w_out[0].astype(_BF16)
    first_leg = (1,) + _CHIP_PEERS
    start_in = _scatter_start([w_in_my], "w_in_start", scatter=False, ks=first_leg)
    start_out = _scatter_start([w_out_my], "w_out_start", scatter=False)
    tokens = start_in[-1][0:1, 0:1] + start_out[-1][0:1, 0:1]
    g_conv_w, g_c = _exchange([conv_w[0], c + tokens], [False] * 2, "gather_small")
    b_in_pad = jnp.pad(b_in, ((0, 0), (0, IN_PAD - IN_W)))
    conv_w_full = jnp.pad(g_conv_w.transpose(1, 0, 2).reshape(CONV_W, N_DEV * n_cw), ((0, 3), (0, 0)))
    c_all = g_c.reshape(N_DEV, D_MODEL)

    b_ada_my = lax.dynamic_slice(b_ada, (0, me * n_ada), (1, n_ada))
    (mod_slots,) = _exchange([_mod_part(c_all, w_ada[0], b_ada_my)], [False], "gather_mod")
    mod = lax.dynamic_index_in_dim(mod_slots, me, axis=1, keepdims=False).reshape(1, 3 * D_MODEL)
    shift, scale, gate = mod[:, :D_MODEL], mod[:, D_MODEL:2 * D_MODEL], mod[:, 2 * D_MODEL:]
    scale1p = 1.0 + scale
    bias = _na_bias_tables(rpb[0], rows)

    def own_slot(land, own):
        return lax.dynamic_update_slice(land, own[None], (me,) + (0,) * own.ndim)

    def gathered(started, after, name):
        (own,), (land,) = _scatter_wait(started, after, name, scatter=False)
        return own_slot(land, own)

    (w_in_own,), (w_in_land,) = _scatter_wait(start_in, bias[0, 0, :8, :128] + scale1p[:, :128], "w_in_wait",
                                              scatter=False, ks=first_leg)
    g_w_in = own_slot(_relay_wait(_relay_start(w_in_land, "w_in_relay_start"), "w_in_relay_wait"), w_in_own)
    w_in_full = g_w_in.transpose(1, 0, 2).reshape(D_MODEL, N_DEV * n_in)
    w_in_bf = jnp.pad(w_in_full, ((0, 0), (0, IN_PAD - IN_W)))

    qkv, rest, h_bf = _inproj_fwd(x, scale1p, shift, norm_w, w_in_bf, b_in_pad)
    o_na, lse = _na_fwd(qkv, bias, T)
    qk_act = _conv_fwd(rest, conv_w_full, conv_b, T)
    h_f, *saved_f = _mlstm_fwd(qk_act, rest, T, False)
    h_b, *saved_b = _mlstm_fwd(qk_act, rest, T, True)

    w_out_bf = gathered(start_out, saved_b[2], "w_out_wait").reshape(N_DEV * n_wo, D_MODEL)
    dx1, d_o, dz_na, d_h, d_mo, d_mz, dwo, pvec = _post(
        x, loss_target, o_na, rest, h_f, h_b, gate, ml_norm_w, final_norm_w.reshape(1, D_MODEL), w_out_bf, T)

    dq_na, dk_na, dv_na, dbias = _na_bwd(qkv, bias, o_na, d_o, lse, T)
    d_rpb = _rpb_grad(dbias, rows)
    dqk_f, dv_f, dg_f = _mlstm_bwd(qk_act, rest, d_h, saved_f, T, False)
    dqk_b, dv_b, dg_b = _mlstm_bwd(qk_act, rest, d_h, saved_b, T, True)
    d_u, dconv = _conv_bwd(rest, conv_w_full, conv_b, dqk_f, dqk_b, T)

    sections = [(0, 512, [(dq_na, 0)]), (512, 512, [(dk_na, 0)]), (1024, 512, [(dv_na, 0)]),
                (1536, 512, [(dz_na, 0)]), (2048, 512, [(d_u, 0)]), (2560, 512, [(d_u, 1)]),
                (3072, 512, [(dv_f, 0), (dv_b, 0)]), (3584, 512, [(d_mo, 0)]), (4096, 512, [(d_mz, 0)]),
                (4608, 128, [(dg_f, 0), (dg_b, 0)])]
    dw_pad, db_pad = _inproj_bwd_w(h_bf, sections, T)
    db_in = db_pad[0, :IN_W]

    dw_blocks = dw_pad[:, :IN_W].astype(_BF16).reshape(D_MODEL, N_DEV, n_in).transpose(1, 0, 2)
    dwo_blocks = dwo.astype(_BF16).reshape(N_DEV, n_wo, D_MODEL)
    started = _scatter_start([dw_blocks, dwo_blocks], "grads_start")
    grad_x, xvec = _inproj_bwd_x(x, dx1, scale1p + started[-1][0:1, 0:1], norm_w, w_in_bf, sections, T)
    (dw_blocks, dwo_blocks), (s_w_in, s_w_out) = _scatter_wait(started, xvec, "grads_wait")

    small = _pack({
        "b_ada": jnp.concatenate([xvec[0], xvec[1], pvec[1]]),
        "norm_w": xvec[2], "b_in": db_in, "conv_w": dconv[:CONV_W], "conv_b": dconv[CONV_W],
        "rpb": d_rpb, "ml_norm_w": pvec[2, :ML_W], "final_norm_w": pvec[0], "loss": pvec[3, :1]})
    (s_small,) = _exchange([small], [False], "exchange_small")

    own = lambda blocks: lax.dynamic_index_in_dim(blocks, me, axis=0, keepdims=False)
    g_w_in_s, d_w_in, nm_w_in, nv_w_in = _adamw_slots(
        w_in[0], m_w_in[0], v_w_in[0], s_w_in, 128, "adamw_w_in", own=own(dw_blocks))
    g_w_out_s, d_w_out, nm_w_out, nv_w_out = _adamw_slots(
        w_out[0], m_w_out[0], v_w_out[0], s_w_out, n_wo, "adamw_w_out", own=own(dwo_blocks))
    dmod_all = s_small[:, 0, :3 * D_MODEL]
    dmod_my = lax.dynamic_slice(dmod_all, (0, me * n_ada), (N_DEV, n_ada))
    g_w_ada, d_w_ada, nm_w_ada, nv_w_ada = _w_ada_update(c_all, dmod_my, w_ada[0], m_w_ada[0], v_w_ada[0])

    def embed(shard):
        return lax.dynamic_update_slice(jnp.zeros((CONV_W, N_DEV * n_cw), _F32), shard[0], (0, me * n_cw))

    zero1 = jnp.zeros((1,), _F32)
    packed = lambda b_a, n_w, b_i, c_w, c_b, rp, mn, fn: _pack({
        "b_ada": b_a, "norm_w": n_w, "b_in": b_i, "conv_w": embed(c_w), "conv_b": c_b, "rpb": rp,
        "ml_norm_w": mn, "final_norm_w": fn, "loss": zero1})
    pw = packed(b_ada, norm_w, b_in, conv_w, conv_b, rpb, ml_norm_w, final_norm_w)
    pm = packed(m_b_ada, m_norm_w, m_b_in, m_conv_w, m_conv_b, m_rpb, m_ml_norm_w, m_final_norm_w)
    pv = packed(v_b_ada, v_norm_w, v_b_in, v_conv_w, v_conv_b, v_rpb, v_ml_norm_w, v_final_norm_w)
    sg, sd, sm, sv = _adamw_slots(pw, pm, pv, s_small, 1, "adamw_small")

    def small_outs(vec):
        cw = lax.dynamic_slice(_unpack(vec, "conv_w", (CONV_W, N_DEV * n_cw)), (0, me * n_cw), (CONV_W, n_cw))
        return dict(b_ada=_unpack(vec, "b_ada", b_ada.shape), norm_w=_unpack(vec, "norm_w", norm_w.shape),
                    b_in=_unpack(vec, "b_in", b_in.shape), conv_w=cw[None],
                    conv_b=_unpack(vec, "conv_b", conv_b.shape), rpb=_unpack(vec, "rpb", rpb.shape),
                    ml_norm_w=_unpack(vec, "ml_norm_w", ml_norm_w.shape),
                    final_norm_w=_unpack(vec, "final_norm_w", final_norm_w.shape))

    loss = _unpack(sg, "loss", ())
    order = ("w_ada", "b_ada", "norm_w", "w_in", "b_in", "conv_w", "conv_b", "rpb", "ml_norm_w", "w_out",
             "final_norm_w")
    outs = []
    for vec, big in ((sg, (g_w_ada, g_w_in_s, g_w_out_s)), (sd, (d_w_ada, d_w_in, d_w_out)),
                     (sm, (nm_w_ada, nm_w_in, nm_w_out)), (sv, (nv_w_ada, nv_w_in, nv_w_out))):
        group = small_outs(vec)
        group.update(w_ada=big[0][None], w_in=big[1][None], w_out=big[2][None])
        outs.extend(group[name] for name in order)
    return (loss, grad_x, *outs)
```

```python
import functools

import numpy as np
import jax
import jax.numpy as jnp
from jax import lax
from jax.experimental import pallas as pl
from jax.experimental.pallas import tpu as pltpu

N_DEV = 8
D_MODEL = 1024
GRID_W = 64
NA_HEADS = 8
NA_HEAD_DIM = 64
NA_KH = 8
NA_KW = 16
NA_W = 512
ML_HEADS = 4
ML_HEAD_DIM = 128
ML_W = 512
ML_CHUNK = 512
CONV_W = 5
EPS = 1e-6
IN_W = 4624
IN_PAD = 4736
REST_W = IN_PAD - 3 * NA_W
GATE_COL = 3072
NEG = -1e30
NA_RB = 4
NA_WIN = 12
NA_SUB = 8
ML_CB = 1
ADAM_LR = 0.001
ADAM_B1 = 0.9
ADAM_B2 = 0.999
ADAM_EPS = 1e-08
ADAM_WD = 0.01
ADAM_STEP = 10
VMEM_LIMIT = 56 * 1024 * 1024

_F32 = jnp.float32
_BF16 = jnp.bfloat16
_HI = lax.Precision.HIGHEST


def _cparams(sem=None):
    return pltpu.CompilerParams(dimension_semantics=sem, vmem_limit_bytes=VMEM_LIMIT)


def _nt(a, b):
    return lax.dot_general(a, b, (((1,), (1,)), ((), ())), preferred_element_type=_F32)


def _tn(a, b):
    return lax.dot_general(a, b, (((0,), (0,)), ((), ())), preferred_element_type=_F32)


def _nn(a, b):
    return jnp.dot(a, b, preferred_element_type=_F32)


def _sigmoid(x):
    return 1.0 / (1.0 + jnp.exp(-x))


def _silu(x):
    return x * _sigmoid(x)


def _dsilu(x):
    s = _sigmoid(x)
    return s * (1.0 + x * (1.0 - s))


def _exchange(arrs, scatter, name):
    n = len(arrs)
    out_shape = []
    for a, sc in zip(arrs, scatter):
        blk = a.shape[1:] if sc else a.shape
        out_shape.append(jax.ShapeDtypeStruct((N_DEV,) + tuple(blk), a.dtype))

    def body(*refs):
        ins = refs[:n]
        outs = refs[n:2 * n]
        send_sems, recv_sems, local_sems = refs[2 * n:]
        x, y, c = lax.axis_index("x"), lax.axis_index("y"), lax.axis_index("c")
        me = 4 * x + 2 * y + c
        local, sends, recvs = [], [], []
        for a in range(n):
            own = ins[a].at[me] if scatter[a] else ins[a]
            cp = pltpu.make_async_copy(own, outs[a].at[me], local_sems.at[a])
            cp.start()
            local.append(cp)
            for k in range(1, N_DEV):
                px = 1 - x if k & 4 else x
                py = 1 - y if k & 2 else y
                pc = 1 - c if k & 1 else c
                p = 4 * px + 2 * py + pc
                src = ins[a].at[p] if scatter[a] else ins[a]
                snd = pltpu.make_async_remote_copy(
                    src_ref=src, dst_ref=outs[a].at[me],
                    send_sem=send_sems.at[a, k - 1], recv_sem=recv_sems.at[a, k - 1],
                    device_id=(px, py, pc), device_id_type=pl.DeviceIdType.MESH)
                snd.start()
                sends.append(snd)
                rcv = pltpu.make_async_remote_copy(
                    src_ref=src, dst_ref=outs[a].at[p],
                    send_sem=send_sems.at[a, k - 1], recv_sem=recv_sems.at[a, k - 1],
                    device_id=(px, py, pc), device_id_type=pl.DeviceIdType.MESH)
                recvs.append(rcv)
        for rcv in recvs:
            rcv.wait_recv()
        for snd in sends:
            snd.wait_send()
        for cp in local:
            cp.wait()

    any_spec = pl.BlockSpec(memory_space=pl.ANY)
    res = pl.pallas_call(
        body, name=name, out_shape=tuple(out_shape),
        in_specs=[any_spec] * n, out_specs=tuple([any_spec] * n),
        scratch_shapes=[pltpu.SemaphoreType.DMA((n, N_DEV - 1)),
                        pltpu.SemaphoreType.DMA((n, N_DEV - 1)),
                        pltpu.SemaphoreType.DMA((n,))],
    )(*arrs)
    return list(res)


def _peer(k):
    x, y, c = lax.axis_index("x"), lax.axis_index("y"), lax.axis_index("c")
    px = 1 - x if k & 4 else x
    py = 1 - y if k & 2 else y
    pc = 1 - c if k & 1 else c
    return (px, py, pc), 4 * px + 2 * py + pc, 4 * x + 2 * y + c


_ALL_PEERS = tuple(range(1, N_DEV))
_CHIP_PEERS = (2, 4, 6)


def _scatter_copy(srcs, lands, send_sems, recv_sems, a, k, receive, scatter, ks=_ALL_PEERS):
    dev, p, me = _peer(k)
    at = a * len(ks) + ks.index(k)
    return pltpu.make_async_remote_copy(
        src_ref=srcs[a].at[p] if scatter else srcs[a], dst_ref=lands[a].at[p if receive else me],
        send_sem=send_sems[at], recv_sem=recv_sems[at],
        device_id=dev, device_id_type=pl.DeviceIdType.MESH)


def _scatter_start(arrs, name, scatter=True, ks=_ALL_PEERS):
    n = len(arrs)
    ns = n * len(ks)
    hbm = pl.BlockSpec(memory_space=pltpu.HBM)
    sem = pl.BlockSpec(memory_space=pltpu.SEMAPHORE)

    def body(*refs):
        srcs, lands = refs[:n], refs[n:2 * n]
        send_sems, recv_sems = refs[2 * n:2 * n + ns], refs[2 * n + ns:2 * n + 2 * ns]
        token = refs[-1]
        for a in range(n):
            for k in ks:
                _scatter_copy(srcs, lands, send_sems, recv_sems, a, k, False, scatter, ks).start()
        token[...] = jnp.zeros_like(token)

    land_shapes = [a.shape if scatter else (N_DEV,) + a.shape for a in arrs]
    buffers = [pltpu.HBM(a.shape, a.dtype) for a in arrs]
    land_buffers = [pltpu.HBM(s, a.dtype) for s, a in zip(land_shapes, arrs)]
    sems = [pltpu.SemaphoreType.DMA(()) for _ in range(2 * ns)]
    res = pl.pallas_call(
        body, name=name,
        out_shape=(*sems, *buffers, *land_buffers, jax.ShapeDtypeStruct((8, 128), _F32)),
        in_specs=[hbm] * (2 * n),
        out_specs=(*([sem] * (2 * ns)), *([hbm] * (2 * n)), pl.BlockSpec(memory_space=pltpu.VMEM)),
        input_output_aliases={i: 2 * ns + i for i in range(2 * n)},
        compiler_params=pltpu.CompilerParams(has_side_effects=pltpu.SideEffectType.DATAFLOW_SIDE_EFFECTING),
    )(*[pltpu.with_memory_space_constraint(a, pltpu.HBM) for a in arrs],
      *[pltpu.with_memory_space_constraint(jnp.zeros(s, a.dtype), pltpu.HBM) for s, a in zip(land_shapes, arrs)])
    res = list(res)
    return (res[:ns], res[ns:2 * ns], res[2 * ns:2 * ns + n], res[2 * ns + n:2 * ns + 2 * n], res[-1])


def _scatter_wait(started, after, name, scatter=True, ks=_ALL_PEERS):
    send_sems, recv_sems, srcs, lands, _ = started
    n = len(srcs)
    ns = len(send_sems)
    hbm = pl.BlockSpec(memory_space=pltpu.HBM)
    sem = pl.BlockSpec(memory_space=pltpu.SEMAPHORE)

    def body(*refs):
        src_refs, land_refs = refs[:n], refs[n:2 * n]
        s_sems, r_sems = refs[2 * n:2 * n + ns], refs[2 * n + ns:2 * n + 2 * ns]
        for a in range(n):
            for k in ks:
                _scatter_copy(src_refs, land_refs, s_sems, r_sems, a, k, False, scatter, ks).wait_send()
                _scatter_copy(src_refs, land_refs, s_sems, r_sems, a, k, True, scatter, ks).wait_recv()

    buffers = [pltpu.HBM(a.shape, a.dtype) for a in list(srcs) + list(lands)]
    res = pl.pallas_call(
        body, name=name, out_shape=tuple(buffers),
        in_specs=[hbm] * (2 * n) + [sem] * (2 * ns) + [pl.BlockSpec(memory_space=pl.ANY)],
        out_specs=tuple([hbm] * (2 * n)),
        input_output_aliases={i: i for i in range(2 * n)},
        compiler_params=pltpu.CompilerParams(has_side_effects=pltpu.SideEffectType.DATAFLOW_SIDE_EFFECTING),
    )(*srcs, *lands, *send_sems, *recv_sems, after)
    return list(res[:n]), list(res[n:])


def _relay_copy(land, send_sems, recv_sems, j, receive):
    k = _CHIP_PEERS[j]
    sibling, _, _ = _peer(1)
    _, slot, _ = _peer(k | 1 if receive else k)
    return pltpu.make_async_remote_copy(
        src_ref=land.at[slot], dst_ref=land.at[slot], send_sem=send_sems[j], recv_sem=recv_sems[j],
        device_id=sibling, device_id_type=pl.DeviceIdType.MESH)


def _relay_start(land, name):
    ns = len(_CHIP_PEERS)
    hbm = pl.BlockSpec(memory_space=pltpu.HBM)
    sem = pl.BlockSpec(memory_space=pltpu.SEMAPHORE)

    def body(*refs):
        land_ref = refs[0]
        send_sems, recv_sems = refs[1:1 + ns], refs[1 + ns:1 + 2 * ns]
        for j in range(ns):
            _relay_copy(land_ref, send_sems, recv_sems, j, False).start()
        refs[-1][...] = jnp.zeros_like(refs[-1])

    sems = [pltpu.SemaphoreType.DMA(()) for _ in range(2 * ns)]
    res = pl.pallas_call(
        body, name=name,
        out_shape=(*sems, pltpu.HBM(land.shape, land.dtype), jax.ShapeDtypeStruct((8, 128), _F32)),
        in_specs=[hbm],
        out_specs=(*([sem] * (2 * ns)), hbm, pl.BlockSpec(memory_space=pltpu.VMEM)),
        input_output_aliases={0: 2 * ns},
        compiler_params=pltpu.CompilerParams(has_side_effects=pltpu.SideEffectType.DATAFLOW_SIDE_EFFECTING),
    )(land)
    res = list(res)
    return res[:ns], res[ns:2 * ns], res[2 * ns], res[-1]


def _relay_wait(started, name):
    send_sems, recv_sems, land, token = started
    ns = len(send_sems)
    hbm = pl.BlockSpec(memory_space=pltpu.HBM)
    sem = pl.BlockSpec(memory_space=pltpu.SEMAPHORE)

    def body(*refs):
        land_ref = refs[0]
        s_sems, r_sems = refs[1:1 + ns], refs[1 + ns:1 + 2 * ns]
        for j in range(ns):
            _relay_copy(land_ref, s_sems, r_sems, j, False).wait_send()
            _relay_copy(land_ref, s_sems, r_sems, j, True).wait_recv()

    return pl.pallas_call(
        body, name=name, out_shape=pltpu.HBM(land.shape, land.dtype),
        in_specs=[hbm] + [sem] * (2 * ns) + [pl.BlockSpec(memory_space=pl.ANY)],
        out_specs=hbm, input_output_aliases={0: 0},
        compiler_params=pltpu.CompilerParams(has_side_effects=pltpu.SideEffectType.DATAFLOW_SIDE_EFFECTING),
    )(land, *send_sems, *recv_sems, token)


def _mod_part(c_all, w_ada, b_my):
    def body(c_ref, w_ref, b_ref, o_ref):
        o_ref[...] = jnp.dot(_silu(c_ref[...]), w_ref[...], precision=_HI,
                             preferred_element_type=_F32) + b_ref[...]

    return pl.pallas_call(
        body, name="mod_part",
        out_shape=jax.ShapeDtypeStruct((N_DEV, w_ada.shape[1]), _F32),
        compiler_params=_cparams(),
    )(c_all, w_ada, b_my)


def _inproj_fwd(x, scale1p, shift, norm_w, w_in_bf, b_in_pad):
    T = x.shape[1]
    tm = 512
    n_q = 3 * NA_W

    def body(x_ref, sc_ref, sh_ref, nw_ref, w_ref, b_ref, qkv_ref, rest_ref, h_ref):
        xv = x_ref[...]
        r = lax.rsqrt(jnp.mean(xv * xv, axis=-1, keepdims=True) + EPS)
        h = xv * r * nw_ref[...] * sc_ref[...] + sh_ref[...]
        hb = h.astype(_BF16)
        h_ref[...] = h.T.astype(_BF16)
        for n0 in range(0, IN_PAD, 512):
            wd = min(512, IN_PAD - n0)
            acc = _nn(hb, w_ref[:, n0:n0 + wd]) + b_ref[:, n0:n0 + wd]
            if n0 == 0:
                acc = acc * (NA_HEAD_DIM ** -0.5)
            if n0 < n_q:
                qkv_ref[:, n0:n0 + wd] = acc.astype(_BF16)
            else:
                rest_ref[:, n0 - n_q:n0 - n_q + wd] = acc

    row = lambda w: pl.BlockSpec((1, w), lambda i: (0, 0))
    return pl.pallas_call(
        body, name="inproj_fwd", grid=(T // tm,),
        in_specs=[pl.BlockSpec((None, tm, D_MODEL), lambda i: (0, i, 0)), row(D_MODEL), row(D_MODEL), row(D_MODEL),
                  pl.BlockSpec((D_MODEL, IN_PAD), lambda i: (0, 0), pipeline_mode=pl.Buffered(1)), row(IN_PAD)],
        out_specs=(pl.BlockSpec((tm, n_q), lambda i: (i, 0)),
                   pl.BlockSpec((tm, REST_W), lambda i: (i, 0)),
                   pl.BlockSpec((D_MODEL, tm), lambda i: (0, i))),
        out_shape=(jax.ShapeDtypeStruct((T, n_q), _BF16),
                   jax.ShapeDtypeStruct((T, REST_W), _F32),
                   jax.ShapeDtypeStruct((D_MODEL, T), _BF16)),
        compiler_params=_cparams(("arbitrary",)),
    )(x, scale1p, shift, norm_w, w_in_bf, b_in_pad)


def _na_class_rows(rows):
    nb = rows // NA_RB
    out = []
    for rb in (0, min(1, nb - 1), nb - 1):
        ws = int(np.clip(NA_RB * rb - 4, 0, rows - NA_WIN))
        out.append((NA_RB * rb + np.arange(NA_RB), ws + np.arange(NA_WIN)))
    return out


def _na_pair_index(rows, qrows, krows):
    start = lambda r: np.clip(r - NA_KH // 2, 0, rows - NA_KH)
    col = np.arange(GRID_W)
    cstart = np.clip(col - NA_KW // 2, 0, GRID_W - NA_KW)
    dy = krows[None, :] - qrows[:, None] + NA_KH - 1
    vr = (krows[None, :] >= start(qrows)[:, None]) & (krows[None, :] < start(qrows)[:, None] + NA_KH)
    dx = np.clip(col[None, :] - col[:, None], -(NA_KW - 1), NA_KW - 1) + NA_KW - 1
    vc = (col[None, :] >= cstart[:, None]) & (col[None, :] < cstart[:, None] + NA_KW)
    nq, nk = len(qrows), len(krows)
    dy4 = np.broadcast_to(np.clip(dy, 0, 2 * NA_KH - 2)[:, None, :, None], (nq, GRID_W, nk, GRID_W))
    dx4 = np.broadcast_to(dx[None, :, None, :], (nq, GRID_W, nk, GRID_W))
    valid = vr[:, None, :, None] & vc[None, :, None, :]
    idx = (dy4 * (2 * NA_KW - 1) + dx4).reshape(nq * GRID_W, nk * GRID_W)
    return idx.astype(np.int32), valid.reshape(nq * GRID_W, nk * GRID_W), (dy, vr, dx, vc)


def _na_half_slabs(rpb):
    _, _, (_, _, dx, vc) = _na_pair_index(NA_WIN, np.arange(1), np.arange(1))
    qc, kc = np.meshgrid(np.arange(GRID_W), np.arange(GRID_W), indexing="ij")
    consts = []
    for right in (False, True):
        pos = (qc * 128 + (GRID_W if right else 0) + kc).reshape(-1)
        oh = np.zeros((32, GRID_W * 128), np.float32)
        oh[dx[qc, kc].reshape(-1), pos] = 1.0
        col_neg = np.zeros((1, GRID_W * 128), np.float32)
        col_neg[0, pos] = np.where(vc[qc, kc].reshape(-1), 0.0, NEG)
        half = np.zeros((1, GRID_W * 128), np.float32)
        half[0, pos] = 1.0
        consts += [jnp.asarray(oh), jnp.asarray(col_neg), jnp.asarray(half)]
    row_neg = np.where(np.arange(NA_HEADS * 16) % 16 == 15, NEG, 0.0).astype(np.float32).reshape(-1, 1)
    rp = jnp.pad(rpb, ((0, 0), (0, 1), (0, 1))).reshape(NA_HEADS * 16, 32)

    def body(*refs):
        r_ref, rn_ref = refs[0], refs[1]
        for t in range(2):
            oh_ref, cn_ref, half_ref = refs[2 + 3 * t:5 + 3 * t]
            refs[8 + t][...] = (jnp.dot(r_ref[...], oh_ref[...], precision=_HI, preferred_element_type=_F32)
                                + cn_ref[...] + rn_ref[...] * half_ref[...])

    outs = pl.pallas_call(
        body, name="na_half_slabs",
        out_shape=tuple(jax.ShapeDtypeStruct((NA_HEADS * 16, GRID_W * 128), _F32) for _ in range(2)),
        compiler_params=_cparams(),
    )(rp, jnp.asarray(row_neg), *consts)
    return [o.reshape(NA_HEADS, 16, GRID_W, 128) for o in outs]


def _na_bias_tables(rpb, rows):
    left, right = _na_half_slabs(rpb)
    didx = []
    for blk, win in _na_class_rows(rows):
        _, _, (dy, vr, _, _) = _na_pair_index(rows, blk, win)
        didx.append(np.where(vr, dy, 15))

    def body(l_ref, r_ref, b_ref):
        for ci, tab in enumerate(didx):
            for a in range(NA_RB):
                for j in range(NA_WIN // 2):
                    b_ref[ci, 0, a * GRID_W:(a + 1) * GRID_W, j * 128:(j + 1) * 128] = (
                        l_ref[0, int(tab[a, 2 * j])] + r_ref[0, int(tab[a, 2 * j + 1])])

    slab = pl.BlockSpec((1, 16, GRID_W, 128), lambda h: (h, 0, 0, 0))
    return pl.pallas_call(
        body, name="na_tables", grid=(NA_HEADS,),
        in_specs=[slab] * 2,
        out_specs=pl.BlockSpec((3, 1, NA_RB * GRID_W, NA_WIN * GRID_W), lambda h: (0, h, 0, 0)),
        out_shape=jax.ShapeDtypeStruct((3, NA_HEADS, NA_RB * GRID_W, NA_WIN * GRID_W), _F32),
        compiler_params=_cparams(("arbitrary",)),
    )(left, right)


def _stack_heads(x, first):
    zero = jnp.zeros_like(x)
    return jnp.concatenate([jnp.where(first, x, zero), jnp.where(first, zero, x)], axis=0)


def _na_sub(rb, u, rows):
    sb = NA_SUB * rb + u
    nb = rows // NA_RB
    cls = jnp.where(sb == 0, 0, jnp.where(sb == nb - 1, 2, 1))
    ws = pl.multiple_of(jnp.clip(NA_RB * sb - 4, 0, rows - NA_WIN) * GRID_W, 256)
    return cls, ws


def _na_fwd(qkv, bias, T):
    rows = T // GRID_W
    tq = NA_RB * GRID_W
    tw = NA_WIN * GRID_W
    ts = NA_SUB * tq

    def body(q_ref, k_ref, v_ref, b_ref, o_ref, l_ref):
        rb = pl.program_id(1)
        lane = lax.broadcasted_iota(jnp.int32, (1, 128), 1)
        first = lane < NA_HEAD_DIM
        for u in range(NA_SUB):
            cls, ws = _na_sub(rb, u, rows)
            kw = k_ref[pl.ds(ws, tw), :]
            vw = v_ref[pl.ds(ws, tw), :]
            q2 = _stack_heads(q_ref[u * tq:(u + 1) * tq, :], first)
            s = _nt(q2, kw) + b_ref[cls].reshape(2 * tq, tw)
            m = jnp.max(s, axis=1, keepdims=True)
            p = jnp.exp(s - m)
            l = jnp.sum(p, axis=1, keepdims=True)
            o2 = _nn(p.astype(_BF16), vw) / l
            lse2 = m + jnp.log(l)
            o_ref[u * tq:(u + 1) * tq, :] = jnp.where(first, o2[:tq], o2[tq:])
            l_ref[u * tq:(u + 1) * tq, :] = jnp.where(first, lse2[:tq], lse2[tq:])

    blk = lambda off: pl.BlockSpec((ts, 128), lambda hp, rb: (rb, off + hp))
    whole = lambda off: pl.BlockSpec((T, 128), lambda hp, rb: (0, off + hp))
    return pl.pallas_call(
        body, name="na_fwd", grid=(NA_HEADS // 2, T // ts),
        in_specs=[blk(0), whole(4), whole(8),
                  pl.BlockSpec((3, 2, tq, tw), lambda hp, rb: (0, hp, 0, 0))],
        out_specs=(blk(0), blk(0)),
        out_shape=(jax.ShapeDtypeStruct((T, NA_W), _F32), jax.ShapeDtypeStruct((T, NA_W), _F32)),
        compiler_params=_cparams(("arbitrary", "arbitrary")),
    )(qkv, qkv, qkv, bias)


def _na_bwd(qkv, bias, o, d_o, lse, T):
    rows = T // GRID_W
    tq = NA_RB * GRID_W
    tw = NA_WIN * GRID_W
    ts = NA_SUB * tq

    def body(q_ref, k_ref, v_ref, b_ref, o_ref, do_ref, l_ref, dq_ref, dk_ref, dv_ref, db_ref):
        rb = pl.program_id(1)
        lane = lax.broadcasted_iota(jnp.int32, (1, 128), 1)
        first = lane < NA_HEAD_DIM

        @pl.when(rb == 0)
        def _():
            db_ref[...] = jnp.zeros_like(db_ref)
            dk_ref[...] = jnp.zeros_like(dk_ref)
            dv_ref[...] = jnp.zeros_like(dv_ref)

        for u in range(NA_SUB):
            cls, ws = _na_sub(rb, u, rows)
            kw = k_ref[pl.ds(ws, tw), :]
            vw = v_ref[pl.ds(ws, tw), :]
            sl = slice(u * tq, (u + 1) * tq)
            q = q_ref[sl, :]
            d_ov = do_ref[sl, :]
            prod = d_ov.astype(_F32) * o_ref[sl, :]
            lse_v = l_ref[sl, :]
            dqs = []
            dk_win = jnp.zeros((tw, 128), _F32)
            dv_win = jnp.zeros((tw, 128), _F32)
            for hh in range(2):
                msk = first if hh == 0 else jnp.logical_not(first)
                c0 = hh * NA_HEAD_DIM
                qm = jnp.where(msk, q, jnp.zeros_like(q))
                dom = jnp.where(msk, d_ov, jnp.zeros_like(d_ov))
                s = _nt(qm, kw) + b_ref[cls, hh]
                p = jnp.exp(s - lse_v[:, c0:c0 + 1])
                dp = _nt(dom, vw)
                delta = jnp.sum(jnp.where(msk, prod, 0.0), axis=1, keepdims=True)
                ds = p * (dp - delta)
                db_ref[cls, hh] += ds
                dsb = ds.astype(_BF16)
                dqs.append(_nn(dsb, kw) * (NA_HEAD_DIM ** -0.5))
                dk_win = dk_win + _tn(dsb, qm)
                dv_win = dv_win + _tn(p.astype(_BF16), dom)
            dq_ref[sl, :] = jnp.where(first, dqs[0], dqs[1]).astype(_BF16)
            dk_ref[pl.ds(ws, tw), :] += dk_win
            dv_ref[pl.ds(ws, tw), :] += dv_win

    once = pl.Buffered(1)
    blk = lambda off: pl.BlockSpec((ts, 128), lambda hp, rb: (rb, off + hp))
    whole = lambda off, **kw: pl.BlockSpec((T, 128), lambda hp, rb: (0, off + hp), **kw)
    tab = lambda **kw: pl.BlockSpec((3, 2, tq, tw), lambda hp, rb: (0, hp, 0, 0), **kw)
    return pl.pallas_call(
        body, name="na_bwd", grid=(NA_HEADS // 2, T // ts),
        in_specs=[blk(0), whole(4), whole(8), tab(), blk(0), blk(0), blk(0)],
        out_specs=(blk(0), whole(0, pipeline_mode=once), whole(0, pipeline_mode=once), tab(pipeline_mode=once)),
        out_shape=(jax.ShapeDtypeStruct((T, NA_W), _BF16), jax.ShapeDtypeStruct((T, NA_W), _F32),
                   jax.ShapeDtypeStruct((T, NA_W), _F32), jax.ShapeDtypeStruct(bias.shape, _F32)),
        compiler_params=_cparams(("arbitrary", "arbitrary")),
    )(qkv, qkv, qkv, bias, o, d_o, lse)


def _rpb_grad(dbias, rows):
    tw = NA_WIN * GRID_W
    lanes = 16 * GRID_W
    offs = [int(win[0] - blk[0] + NA_KH - 1) for blk, win in _na_class_rows(rows)]

    def body(x_ref, g_ref):
        sub = lax.broadcasted_iota(jnp.int32, (NA_RB, 1), 0)
        qc = lax.broadcasted_iota(jnp.int32, (NA_RB * GRID_W, 1), 0) % GRID_W
        tot = jnp.zeros((NA_RB, lanes), _F32)
        for ci in range(3):
            xv = x_ref[ci, 0]
            for bit in range(6):
                xv = jnp.where(((qc >> bit) & 1) == 1, pltpu.roll(xv, tw - (1 << bit), 1), xv)
            acc = pltpu.roll(jnp.sum(xv.reshape(NA_RB, GRID_W, tw), axis=1), NA_KW, 1)
            acc = jnp.concatenate([acc, jnp.zeros((NA_RB, lanes - tw), _F32)], axis=1)
            for a in range(NA_RB):
                tot = tot + jnp.where(sub == a, pltpu.roll(acc, (GRID_W * (offs[ci] - a)) % lanes, 1), 0.0)
        g_ref[0] = jnp.broadcast_to(jnp.sum(tot, axis=0, keepdims=True), (8, lanes))

    g = pl.pallas_call(
        body, name="rpb_grad", grid=(NA_HEADS,),
        in_specs=[pl.BlockSpec((3, 1) + dbias.shape[2:], lambda h: (0, h, 0, 0))],
        out_specs=pl.BlockSpec((1, 8, lanes), lambda h: (h, 0, 0)),
        out_shape=jax.ShapeDtypeStruct((NA_HEADS, 8, lanes), _F32),
        compiler_params=_cparams(("arbitrary",)),
    )(dbias)
    return g[:, 0].reshape(NA_HEADS, 16, GRID_W)[:, :2 * NA_KH - 1, 1:2 * NA_KW]


def _halo_specs(tm, width, col_of, T, order):
    hb = tm // 8
    last = T // 8 - 1
    if order == "ij":
        cur = pl.BlockSpec((tm, width), lambda i, j: (i, col_of(j)))
        prev = pl.BlockSpec((8, width), lambda i, j: (jnp.maximum(i * hb - 1, 0), col_of(j)))
        nxt = pl.BlockSpec((8, width), lambda i, j: (jnp.minimum((i + 1) * hb, last), col_of(j)))
    else:
        cur = pl.BlockSpec((tm, width), lambda j, i: (i, col_of(j)))
        prev = pl.BlockSpec((8, width), lambda j, i: (jnp.maximum(i * hb - 1, 0), col_of(j)))
        nxt = pl.BlockSpec((8, width), lambda j, i: (jnp.minimum((i + 1) * hb, last), col_of(j)))
    return [prev, cur, nxt]


def _extend(prev_ref, cur_ref, next_ref, i, n_i):
    prev = jnp.where(i > 0, prev_ref[...], 0.0)
    nxt = jnp.where(i < n_i - 1, next_ref[...], 0.0)
    return jnp.concatenate([prev, cur_ref[...], nxt], axis=0)


def _conv_fwd(rest, conv_w, conv_b, T):
    tm = 512
    n_i = T // tm
    n = tm + 16

    def body(p_ref, c_ref, n_ref, w_ref, b_ref, o_ref):
        i = pl.program_id(0)
        ext = _extend(p_ref, c_ref, n_ref, i, n_i)
        acc = jnp.zeros((tm, 512), _F32) + b_ref[...]
        for j in range(CONV_W):
            acc = acc + w_ref[j:j + 1, :] * pltpu.roll(ext, (2 - j) % n, 0)[8:8 + tm]
        o_ref[...] = _silu(acc)

    return pl.pallas_call(
        body, name="conv_fwd", grid=(n_i, 2),
        in_specs=_halo_specs(tm, 512, lambda j: 1 + j, T, "ij")
        + [pl.BlockSpec((8, 512), lambda i, j: (0, j)), pl.BlockSpec((1, 512), lambda i, j: (0, j))],
        out_specs=pl.BlockSpec((tm, 512), lambda i, j: (i, j)),
        out_shape=jax.ShapeDtypeStruct((T, 2 * ML_W), _F32),
        compiler_params=_cparams(("arbitrary", "arbitrary")),
    )(rest, rest, rest, conv_w, conv_b)


def _conv_bwd(rest, conv_w, conv_b, da_f, da_b, T):
    tm = 512
    n_i = T // tm
    n = tm + 16

    def body(up, uc, un, fp, fc, fn, bp, bc, bn, w_ref, b_ref, du_ref, dw_ref):
        i = pl.program_id(1)
        ext_u = _extend(up, uc, un, i, n_i)
        ext_da = _extend(fp, fc, fn, i, n_i) + _extend(bp, bc, bn, i, n_i)
        shifted = [pltpu.roll(ext_u, (2 - j) % n, 0) for j in range(CONV_W)]
        pre = jnp.zeros((n, 512), _F32) + b_ref[...]
        for j in range(CONV_W):
            pre = pre + w_ref[j:j + 1, :] * shifted[j]
        gidx = i * tm - 8 + lax.broadcasted_iota(jnp.int32, (n, 1), 0)
        dpre = jnp.where((gidx >= 0) & (gidx < T), ext_da * _dsilu(pre), 0.0)
        du = jnp.zeros((tm, 512), _F32)
        for j in range(CONV_W):
            du = du + w_ref[j:j + 1, :] * pltpu.roll(dpre, (j - 2) % n, 0)[8:8 + tm]
        du_ref[...] = du.astype(_BF16)
        dpc = dpre[8:8 + tm]
        parts = [jnp.sum(dpc * shifted[j][8:8 + tm], axis=0, keepdims=True) for j in range(CONV_W)]
        parts.append(jnp.sum(dpc, axis=0, keepdims=True))
        parts.append(jnp.zeros((2, 512), _F32))
        upd = jnp.concatenate(parts, axis=0)

        @pl.when(i == 0)
        def _():
            dw_ref[...] = upd

        @pl.when(i > 0)
        def _():
            dw_ref[...] += upd

    return pl.pallas_call(
        body, name="conv_bwd", grid=(2, n_i),
        in_specs=_halo_specs(tm, 512, lambda j: 1 + j, T, "ji")
        + _halo_specs(tm, 512, lambda j: j, T, "ji") + _halo_specs(tm, 512, lambda j: j, T, "ji")
        + [pl.BlockSpec((8, 512), lambda j, i: (0, j)), pl.BlockSpec((1, 512), lambda j, i: (0, j))],
        out_specs=(pl.BlockSpec((tm, 512), lambda j, i: (i, j)), pl.BlockSpec((8, 512), lambda j, i: (0, j))),
        out_shape=(jax.ShapeDtypeStruct((T, 2 * ML_W), _BF16), jax.ShapeDtypeStruct((8, 2 * ML_W), _F32)),
        compiler_params=_cparams(("arbitrary", "arbitrary")),
    )(rest, rest, rest, da_f, da_f, da_f, da_b, da_b, da_b, conv_w, conv_b)


def _scan_rows(x, suffix):
    L = x.shape[0]
    row = lax.broadcasted_iota(jnp.int32, (L, 1), 0)
    step = 1
    while step < L:
        if suffix:
            x = x + jnp.where(row < L - step, pltpu.roll(x, L - step, 0), 0.0)
        else:
            x = x + jnp.where(row >= step, pltpu.roll(x, step, 0), 0.0)
        step *= 2
    return x


def _ml_gates(gt, rev):
    L = gt.shape[0]
    ri = lax.broadcasted_iota(jnp.int32, (L, L), 0)
    ci = lax.broadcasted_iota(jnp.int32, (L, L), 1)
    mask = (ci >= ri) if rev else (ci <= ri)
    lf = jnp.minimum(gt, 0.0) - jnp.log(1.0 + jnp.exp(-jnp.abs(gt)))
    b = _scan_rows(lf, suffix=rev)
    return mask, b, b.T, gt.T


def _ml_head_gates(gt, gates, head, rev):
    _, b, b_t, gt_t = gates
    ci = (8 if rev else 0) + head
    cf = ci + ML_HEADS
    last = 0 if rev else gt.shape[0] - 1
    return dict(icol=gt[:, ci:ci + 1], b_col=b[:, cf:cf + 1], b_row=b_t[cf:cf + 1, :],
                i_row=gt_t[ci:ci + 1, :], bl=b[last:last + 1, cf:cf + 1])


def _ml_chunk(q, k, v, hg, mask, C, n, m, saved=None):
    icol, b_col, b_row, bl = hg["icol"], hg["b_col"], hg["b_row"], hg["bl"]
    if saved is None:
        dlog = jnp.where(mask, b_col - b_row + hg["i_row"], NEG)
        m_t = jnp.maximum(b_col + m, jnp.max(dlog, axis=1, keepdims=True))
        dm = jnp.exp(dlog - m_t)
    else:
        dm, m_t = saved[0].astype(_F32), saved[1]
    ks = k * (ML_HEAD_DIM ** -0.5)
    qb, kb, vb = q.astype(_BF16), ks.astype(_BF16), v.astype(_BF16)
    s = _nt(qb, kb) * dm
    g = jnp.exp(b_col + m - m_t)
    qc = _nt(qb, C.astype(_BF16))
    num = _nn(s.astype(_BF16), vb) + g * qc
    qn = jnp.sum(q * n, axis=1, keepdims=True)
    den = jnp.sum(s, axis=1, keepdims=True) + g * qn
    e_m = jnp.exp(-m_t)
    nrm = jnp.maximum(jnp.abs(den), e_m)
    h = num / nrm
    a_col = bl - b_col + icol
    m_new = jnp.maximum(bl + m, jnp.max(a_col, axis=0, keepdims=True))
    decay = jnp.exp(bl + m - m_new)
    w = jnp.exp(a_col - m_new)
    c_new = decay * C + _tn((w * v).astype(_BF16), kb)
    n_new = decay * n + jnp.sum(w * ks, axis=0, keepdims=True)
    aux = dict(dm=dm, m_t=m_t, ks=ks, qb=qb, kb=kb, vb=vb, s=s, g=g, qc=qc, qn=qn,
               den=den, e_m=e_m, nrm=nrm, decay=decay, w=w)
    return h, c_new, n_new, m_new, aux


def _mlstm_fwd(qk_act, rest, T, rev):
    tb = ML_CB * ML_CHUNK
    nblk = T // tb
    nc = T // ML_CHUNK
    bi = (lambda i: nblk - 1 - i) if rev else (lambda i: i)

    def body(q_ref, k_ref, v_ref, g_ref, h_ref, cs_ref, ns_ref, ms_ref, dm_ref, mt_ref, c_scr, n_scr, m_scr):
        @pl.when(pl.program_id(0) == 0)
        def _():
            c_scr[...] = jnp.zeros_like(c_scr)
            n_scr[...] = jnp.zeros_like(n_scr)
            m_scr[...] = jnp.zeros_like(m_scr)

        def step(j, carry):
            c = (ML_CB - 1 - j) if rev else j
            r0 = pl.multiple_of(c * ML_CHUNK, ML_CHUNK)
            gt = g_ref[pl.ds(r0, ML_CHUNK), :]
            gates = _ml_gates(gt, rev)
            lane = lax.broadcasted_iota(jnp.int32, (1, 128), 1)
            mt_tile = jnp.zeros((ML_CHUNK, 128), _F32)
            for hd in range(ML_HEADS):
                cols = slice(hd * ML_HEAD_DIM, (hd + 1) * ML_HEAD_DIM)
                C = c_scr[hd]
                n = n_scr[hd:hd + 1, :]
                mrow = m_scr[hd:hd + 1, :]
                cs_ref[c, hd] = C
                ns_ref[c, hd:hd + 1, :] = n
                ms_ref[c, hd:hd + 1, :] = mrow
                h, c_new, n_new, m_new, a = _ml_chunk(
                    q_ref[pl.ds(r0, ML_CHUNK), cols], k_ref[pl.ds(r0, ML_CHUNK), cols],
                    v_ref[pl.ds(r0, ML_CHUNK), cols], _ml_head_gates(gt, gates, hd, rev), gates[0],
                    C, n, mrow[:, 0:1])
                h_ref[pl.ds(r0, ML_CHUNK), cols] = h
                dm_ref[c, hd] = a["dm"].astype(_BF16)
                mt_tile = jnp.where(lane == hd, a["m_t"], mt_tile)
                c_scr[hd] = c_new
                n_scr[hd:hd + 1, :] = n_new
                m_scr[hd:hd + 1, :] = jnp.broadcast_to(m_new, (1, 128))
            mt_ref[c] = mt_tile
            return carry

        lax.fori_loop(0, ML_CB, step, 0)

    return pl.pallas_call(
        body, name="mlstm_fwd_rev" if rev else "mlstm_fwd", grid=(nblk,),
        in_specs=[pl.BlockSpec((tb, ML_W), lambda i: (bi(i), 0)),
                  pl.BlockSpec((tb, ML_W), lambda i: (bi(i), 1)),
                  pl.BlockSpec((tb, ML_W), lambda i: (bi(i), 3)),
                  pl.BlockSpec((tb, 128), lambda i: (bi(i), GATE_COL // 128))],
        out_specs=(pl.BlockSpec((tb, ML_W), lambda i: (bi(i), 0)),
                   pl.BlockSpec((ML_CB, ML_HEADS, 128, 128), lambda i: (bi(i), 0, 0, 0)),
                   pl.BlockSpec((ML_CB, ML_HEADS, 128), lambda i: (bi(i), 0, 0)),
                   pl.BlockSpec((ML_CB, ML_HEADS, 128), lambda i: (bi(i), 0, 0)),
                   pl.BlockSpec((ML_CB, ML_HEADS, ML_CHUNK, ML_CHUNK), lambda i: (bi(i), 0, 0, 0)),
                   pl.BlockSpec((ML_CB, ML_CHUNK, 128), lambda i: (bi(i), 0, 0))),
        out_shape=(jax.ShapeDtypeStruct((T, ML_W), _F32),
                   jax.ShapeDtypeStruct((nc, ML_HEADS, 128, 128), _F32),
                   jax.ShapeDtypeStruct((nc, ML_HEADS, 128), _F32),
                   jax.ShapeDtypeStruct((nc, ML_HEADS, 128), _F32),
                   jax.ShapeDtypeStruct((nc, ML_HEADS, ML_CHUNK, ML_CHUNK), _BF16),
                   jax.ShapeDtypeStruct((nc, ML_CHUNK, 128), _F32)),
        scratch_shapes=[pltpu.VMEM((ML_HEADS, 128, 128), _F32), pltpu.VMEM((8, 128), _F32),
                        pltpu.VMEM((8, 128), _F32)],
        compiler_params=_cparams(("arbitrary",)),
    )(qk_act, qk_act, rest, rest)


def _mlstm_bwd(qk_act, rest, d_h, saved, T, rev):
    tb = ML_CB * ML_CHUNK
    nblk = T // tb
    bi = (lambda i: i) if rev else (lambda i: nblk - 1 - i)

    def body(q_ref, k_ref, v_ref, g_ref, dh_ref, cs_ref, ns_ref, ms_ref, dm_ref, mt_ref,
             dqk_ref, dv_ref, dg_ref, dc_scr, dn_scr):
        @pl.when(pl.program_id(0) == 0)
        def _():
            dc_scr[...] = jnp.zeros_like(dc_scr)
            dn_scr[...] = jnp.zeros_like(dn_scr)

        def step(j, carry):
            c = j if rev else (ML_CB - 1 - j)
            r0 = pl.multiple_of(c * ML_CHUNK, ML_CHUNK)
            gt = g_ref[pl.ds(r0, ML_CHUNK), :]
            gates = _ml_gates(gt, rev)
            mask = gates[0]
            lane = lax.broadcasted_iota(jnp.int32, (1, 128), 1)
            sub = lax.broadcasted_iota(jnp.int32, (128, 1), 0)
            db_t = jnp.zeros((ML_CHUNK, 128), _F32)
            da_t = jnp.zeros((ML_CHUNK, 128), _F32)
            cs_rows = jnp.zeros((128, ML_CHUNK), _F32)
            dbl_t = jnp.zeros((1, 128), _F32)
            for hd in range(ML_HEADS):
                cols = slice(hd * ML_HEAD_DIM, (hd + 1) * ML_HEAD_DIM)
                ci = (8 if rev else 0) + hd
                cf = ci + ML_HEADS
                q = q_ref[pl.ds(r0, ML_CHUNK), cols]
                k = k_ref[pl.ds(r0, ML_CHUNK), cols]
                v = v_ref[pl.ds(r0, ML_CHUNK), cols]
                C = cs_ref[c, hd]
                n = ns_ref[c, hd:hd + 1, :]
                m = ms_ref[c, hd:hd + 1, :][:, 0:1]
                dcn = dc_scr[hd]
                dnn = dn_scr[hd:hd + 1, :]
                h, _, _, _, a = _ml_chunk(q, k, v, _ml_head_gates(gt, gates, hd, rev), mask, C, n, m,
                                          saved=(dm_ref[c, hd], mt_ref[c][:, hd:hd + 1]))
                d_hv = dh_ref[pl.ds(r0, ML_CHUNK), cols]
                g, s, w, ks = a["g"], a["s"], a["w"], a["ks"]
                qb, kb, vb = a["qb"], a["kb"], a["vb"]
                dnum = d_hv / a["nrm"]
                hdot = jnp.sum(d_hv * h, axis=1, keepdims=True)
                dden = jnp.where(jnp.abs(a["den"]) >= a["e_m"], -hdot / a["nrm"] * jnp.sign(a["den"]), 0.0)
                dnb = dnum.astype(_BF16)
                d_s = _nt(dnb, vb) + dden
                r = d_s * s
                dsqk = (d_s * a["dm"]).astype(_BF16)
                cb = C.astype(_BF16)
                dq = _nn(dsqk, kb) + g * _nn(dnb, cb) + (dden * g) * n
                dk = _tn(dsqk, qb)
                dv = _tn(s.astype(_BF16), dnb)
                dg = jnp.sum(dnum * a["qc"], axis=1, keepdims=True) + dden * a["qn"]
                db_col = jnp.sum(r, axis=1, keepdims=True) + dg * g
                cs_rows = cs_rows + jnp.where((sub == ci) | (sub == cf), jnp.sum(r, axis=0, keepdims=True), 0.0)
                dc_chunk = _tn((g * dnum).astype(_BF16), qb)
                dn_chunk = jnp.sum((dden * g) * q, axis=0, keepdims=True)
                dcb = dcn.astype(_BF16)
                vdc = _nn(vb, dcb)
                kdc = _nt(kb, dcb)
                dw = jnp.sum(vdc * ks, axis=1, keepdims=True) + jnp.sum(ks * dnn, axis=1, keepdims=True)
                dv = dv + w * kdc
                dk = dk + w * vdc + w * dnn
                da = dw * w
                ddecay = (jnp.sum(jnp.sum(dcn * C, axis=1, keepdims=True), axis=0, keepdims=True)
                          + jnp.sum(dnn * n, axis=1, keepdims=True))
                dbl = ddecay * a["decay"] + jnp.sum(da, axis=0, keepdims=True)
                db_t = db_t + jnp.where(lane == cf, db_col - da, 0.0)
                da_t = da_t + jnp.where(lane == ci, da, 0.0)
                dbl_t = dbl_t + jnp.where(lane == cf, dbl, 0.0)
                dc_scr[hd] = dc_chunk + a["decay"] * dcn
                dn_scr[hd:hd + 1, :] = dn_chunk + a["decay"] * dnn
                dqk_ref[pl.ds(r0, ML_CHUNK), cols] = dq
                dqk_ref[pl.ds(r0, ML_CHUNK), slice(ML_W + hd * 128, ML_W + (hd + 1) * 128)] = dk * (ML_HEAD_DIM ** -0.5)
                dv_ref[pl.ds(r0, ML_CHUNK), cols] = dv.astype(_BF16)
            lo = 8 if rev else 0
            is_i = (lane >= lo) & (lane < lo + ML_HEADS)
            is_f = (lane >= lo + ML_HEADS) & (lane < lo + 2 * ML_HEADS)
            cs_t = cs_rows.T
            db_all = db_t - jnp.where(is_f, cs_t, 0.0)
            dlf = _scan_rows(db_all, suffix=not rev) + dbl_t
            dg_ref[pl.ds(r0, ML_CHUNK), :] = (da_t + jnp.where(is_i, cs_t, 0.0)
                                               + jnp.where(is_f, dlf * _sigmoid(-gt), 0.0))
            return carry

        lax.fori_loop(0, ML_CB, step, 0)

    return pl.pallas_call(
        body, name="mlstm_bwd_rev" if rev else "mlstm_bwd", grid=(nblk,),
        in_specs=[pl.BlockSpec((tb, ML_W), lambda i: (bi(i), 0)),
                  pl.BlockSpec((tb, ML_W), lambda i: (bi(i), 1)),
                  pl.BlockSpec((tb, ML_W), lambda i: (bi(i), 3)),
                  pl.BlockSpec((tb, 128), lambda i: (bi(i), GATE_COL // 128)),
                  pl.BlockSpec((tb, ML_W), lambda i: (bi(i), 0)),
                  pl.BlockSpec((ML_CB, ML_HEADS, 128, 128), lambda i: (bi(i), 0, 0, 0)),
                  pl.BlockSpec((ML_CB, ML_HEADS, 128), lambda i: (bi(i), 0, 0)),
                  pl.BlockSpec((ML_CB, ML_HEADS, 128), lambda i: (bi(i), 0, 0)),
                  pl.BlockSpec((ML_CB, ML_HEADS, ML_CHUNK, ML_CHUNK), lambda i: (bi(i), 0, 0, 0)),
                  pl.BlockSpec((ML_CB, ML_CHUNK, 128), lambda i: (bi(i), 0, 0))],
        out_specs=(pl.BlockSpec((tb, 2 * ML_W), lambda i: (bi(i), 0)),
                   pl.BlockSpec((tb, ML_W), lambda i: (bi(i), 0)),
                   pl.BlockSpec((tb, 128), lambda i: (bi(i), 0))),
        out_shape=(jax.ShapeDtypeStruct((T, 2 * ML_W), _F32), jax.ShapeDtypeStruct((T, ML_W), _BF16),
                   jax.ShapeDtypeStruct((T, 128), _F32)),
        scratch_shapes=[pltpu.VMEM((ML_HEADS, 128, 128), _F32), pltpu.VMEM((8, 128), _F32)],
        compiler_params=_cparams(("arbitrary",)),
    )(qk_act, qk_act, rest, rest, d_h, *saved)


def _post(x, target, o_na, rest, h_f, h_b, gate, ml_norm_w, final_w, w_out_bf, T):
    tm = 256
    n_i = T // tm

    def body(x_ref, t_ref, o_ref, zna_ref, hf_ref, hb_ref, mo_ref, mz_ref, gate_ref, mw_ref, fw_ref, w_ref,
             dx1_ref, do_ref, dzna_ref, dh_ref, dmo_ref, dmz_ref, dwo_ref, vec_ref):
        i = pl.program_id(0)
        gate_v = gate_ref[...]
        fw = fw_ref[...]
        zna = zna_ref[...]
        o = o_ref[...]
        sig_zna = _sigmoid(zna)
        silu_zna = zna * sig_zna
        na_out = o * silu_zna
        hsum = hf_ref[...] + hb_ref[...]
        sg = _sigmoid(mo_ref[...])
        hm = hsum * sg
        mz = mz_ref[...]
        sig_mz = _sigmoid(mz)
        smz = mz * sig_mz
        dsilu_mz = sig_mz * (1.0 + mz * (1.0 - sig_mz))
        hn_l, rstd_l, ml_l = [], [], []
        for hd in range(ML_HEADS):
            cols = slice(hd * 128, (hd + 1) * 128)
            hh = hm[:, cols]
            mu = jnp.mean(hh, axis=-1, keepdims=True)
            var = jnp.mean(jnp.square(hh - mu), axis=-1, keepdims=True)
            rstd = lax.rsqrt(var + EPS)
            hn = (hh - mu) * rstd
            hn_l.append(hn)
            rstd_l.append(rstd)
            ml_l.append(hn * mw_ref[:, cols] * smz[:, cols])
        mix = jnp.concatenate([na_out] + ml_l, axis=1).astype(_BF16)
        y = _nn(mix, w_ref[...])
        x1 = x_ref[...] + gate_v * y
        r = lax.rsqrt(jnp.mean(x1 * x1, axis=-1, keepdims=True) + EPS)
        xhat = x1 * r
        out = xhat * fw
        err = out - t_ref[...]
        loss = 0.5 * jnp.sum(jnp.sum(err * err, axis=1, keepdims=True), axis=0, keepdims=True) / D_MODEL
        dout = err * (1.0 / D_MODEL)
        dfw = jnp.sum(dout * xhat, axis=0, keepdims=True)
        dxhat = dout * fw
        dx1 = r * (dxhat - xhat * jnp.mean(dxhat * xhat, axis=-1, keepdims=True))
        dx1_ref[...] = dx1
        dgate = jnp.sum(dx1 * y, axis=0, keepdims=True)
        dy = (dx1 * gate_v).astype(_BF16)
        dmix = _nt(dy, w_ref[...])
        dwo = _tn(mix, dy)
        dna = dmix[:, :NA_W]
        do_ref[...] = (dna * silu_zna).astype(_BF16)
        dzna_ref[...] = (dna * o * (sig_zna * (1.0 + zna * (1.0 - sig_zna)))).astype(_BF16)
        dmw_l = []
        for hd in range(ML_HEADS):
            cols = slice(hd * 128, (hd + 1) * 128)
            dml = dmix[:, NA_W + hd * 128:NA_W + (hd + 1) * 128]
            hn = hn_l[hd]
            mwv = mw_ref[:, cols]
            dmz_ref[:, cols] = (dml * hn * mwv * dsilu_mz[:, cols]).astype(_BF16)
            dhn = dml * mwv * smz[:, cols]
            dmw_l.append(jnp.sum(dml * hn * smz[:, cols], axis=0, keepdims=True))
            dhm = rstd_l[hd] * (dhn - jnp.mean(dhn, axis=-1, keepdims=True)
                                - hn * jnp.mean(dhn * hn, axis=-1, keepdims=True))
            sgc = sg[:, cols]
            dh_ref[:, cols] = dhm * sgc
            dmo_ref[:, cols] = (dhm * hsum[:, cols] * sgc * (1.0 - sgc)).astype(_BF16)
        dmw = jnp.concatenate(dmw_l + [jnp.zeros((1, D_MODEL - ML_W), _F32)], axis=1)
        lane = lax.broadcasted_iota(jnp.int32, (1, D_MODEL), 1)
        vec = jnp.concatenate([dfw, dgate, dmw, jnp.where(lane == 0, loss, 0.0),
                               jnp.zeros((4, D_MODEL), _F32)], axis=0)

        @pl.when(i == 0)
        def _():
            dwo_ref[...] = dwo
            vec_ref[...] = vec

        @pl.when(i > 0)
        def _():
            dwo_ref[...] += dwo
            vec_ref[...] += vec

    tok = lambda w, j: pl.BlockSpec((tm, w), lambda i: (i, j))
    tok3 = pl.BlockSpec((None, tm, D_MODEL), lambda i: (0, i, 0))
    row = lambda w: pl.BlockSpec((1, w), lambda i: (0, 0))
    f32 = lambda w: jax.ShapeDtypeStruct((T, w), _F32)
    bf16 = lambda w: jax.ShapeDtypeStruct((T, w), _BF16)
    return pl.pallas_call(
        body, name="post", grid=(n_i,),
        in_specs=[tok3, tok3, tok(NA_W, 0), tok(NA_W, 0), tok(ML_W, 0), tok(ML_W, 0),
                  tok(ML_W, 4), tok(ML_W, 5), row(D_MODEL), row(ML_W), row(D_MODEL),
                  pl.BlockSpec((D_MODEL, D_MODEL), lambda i: (0, 0))],
        out_specs=(tok(D_MODEL, 0), tok(NA_W, 0), tok(NA_W, 0), tok(ML_W, 0), tok(ML_W, 0), tok(ML_W, 0),
                   pl.BlockSpec((D_MODEL, D_MODEL), lambda i: (0, 0)),
                   pl.BlockSpec((8, D_MODEL), lambda i: (0, 0))),
        out_shape=(f32(D_MODEL), bf16(NA_W), bf16(NA_W), f32(ML_W), bf16(ML_W),
                   bf16(ML_W), jax.ShapeDtypeStruct((D_MODEL, D_MODEL), _F32),
                   jax.ShapeDtypeStruct((8, D_MODEL), _F32)),
        compiler_params=_cparams(("arbitrary",)),
    )(x, target, o_na, rest, h_f, h_b, rest, rest, gate, ml_norm_w, final_w, w_out_bf)


def _section_specs(sections, tm):
    specs, args = [], []
    for _, width, parts in sections:
        for arr, cb in parts:
            specs.append(pl.BlockSpec((tm, width), functools.partial(lambda i, cb: (i, cb), cb=cb)))
            args.append(arr)
    return specs, args


def _section_values(sections, refs, dtype):
    vals, at = [], 0
    for _, _, parts in sections:
        v = refs[at][...]
        for r in refs[at + 1:at + len(parts)]:
            v = v.astype(_F32) + r[...].astype(_F32)
        at += len(parts)
        vals.append(v.astype(dtype))
    return vals


def _inproj_bwd_x(x, dx1, scale1p, norm_w, w_in_bf, sections, T):
    tm = 512
    sspecs, sargs = _section_specs(sections, tm)
    ns = len(sargs)

    def body(*refs):
        x_ref, dx1_ref, sc_ref, nw_ref, w_ref = refs[:5]
        srefs = refs[5:5 + ns]
        gx_ref, vec_ref = refs[5 + ns:]
        i = pl.program_id(0)
        vals = _section_values(sections, srefs, _BF16)
        dh = jnp.zeros((tm, D_MODEL), _F32)
        for (c0, width, _), val in zip(sections, vals):
            dh = dh + _nt(val, w_ref[:, c0:c0 + width])
        xv = x_ref[...]
        r = lax.rsqrt(jnp.mean(xv * xv, axis=-1, keepdims=True) + EPS)
        xhat = xv * r
        nw = nw_ref[...]
        dshift = jnp.sum(dh, axis=0, keepdims=True)
        dscale = jnp.sum(dh * xhat * nw, axis=0, keepdims=True)
        dhpre = dh * sc_ref[...]
        dnw = jnp.sum(dhpre * xhat, axis=0, keepdims=True)
        dxhat = dhpre * nw
        gx_ref[...] = dx1_ref[...] + r * (dxhat - xhat * jnp.mean(dxhat * xhat, axis=-1, keepdims=True))
        vec = jnp.concatenate([dshift, dscale, dnw, jnp.zeros((5, D_MODEL), _F32)], axis=0)

        @pl.when(i == 0)
        def _():
            vec_ref[...] = vec

        @pl.when(i > 0)
        def _():
            vec_ref[...] += vec

    row = pl.BlockSpec((1, D_MODEL), lambda i: (0, 0))
    tok = pl.BlockSpec((tm, D_MODEL), lambda i: (i, 0))
    tok3 = pl.BlockSpec((None, tm, D_MODEL), lambda i: (0, i, 0))
    return pl.pallas_call(
        body, name="inproj_bwd_x", grid=(T // tm,),
        in_specs=[tok3, tok, row, row,
                  pl.BlockSpec((D_MODEL, IN_PAD), lambda i: (0, 0), pipeline_mode=pl.Buffered(1))] + sspecs,
        out_specs=(tok3, pl.BlockSpec((8, D_MODEL), lambda i: (0, 0))),
        out_shape=(jax.ShapeDtypeStruct((1, T, D_MODEL), _F32), jax.ShapeDtypeStruct((8, D_MODEL), _F32)),
        compiler_params=_cparams(("arbitrary",)),
    )(x, dx1, scale1p, norm_w, w_in_bf, *sargs)


def _inproj_bwd_w(h_t, sections, T):
    tm = 1024
    n_i = T // tm
    sspecs, sargs = _section_specs(sections, tm)
    ns = len(sargs)

    def body(*refs):
        h_ref = refs[0]
        srefs = refs[1:1 + ns]
        dw_ref, db_ref, acc, sem = refs[1 + ns:]
        i = pl.program_id(0)

        @pl.when(i == 0)
        def _():
            acc[...] = jnp.zeros_like(acc)
            db_ref[...] = jnp.zeros_like(db_ref)

        hv = h_ref[...]
        for (c0, width, _), v in zip(sections, _section_values(sections, srefs, _F32)):
            acc[:, c0:c0 + width] += _nn(hv, v.astype(_BF16))
            db_ref[0:1, c0:c0 + width] += jnp.sum(v, axis=0, keepdims=True)

        @pl.when(i == n_i - 1)
        def _():
            cp = pltpu.make_async_copy(acc, dw_ref, sem)
            cp.start()
            cp.wait()

    return pl.pallas_call(
        body, name="inproj_bwd_w", grid=(n_i,),
        in_specs=[pl.BlockSpec((D_MODEL, tm), lambda i: (0, i))] + sspecs,
        out_specs=(pl.BlockSpec(memory_space=pl.ANY), pl.BlockSpec((8, IN_PAD), lambda i: (0, 0))),
        out_shape=(jax.ShapeDtypeStruct((D_MODEL, IN_PAD), _F32), jax.ShapeDtypeStruct((8, IN_PAD), _F32)),
        scratch_shapes=[pltpu.VMEM((D_MODEL, IN_PAD), _F32), pltpu.SemaphoreType.DMA],
        compiler_params=_cparams(("arbitrary",)),
    )(h_t, *sargs)


def _adamw_math(w, g, m, v):
    m = ADAM_B1 * m + (1.0 - ADAM_B1) * g
    v = ADAM_B2 * v + (1.0 - ADAM_B2) * jnp.square(g)
    m_hat = m / (1.0 - ADAM_B1 ** ADAM_STEP)
    v_hat = v / (1.0 - ADAM_B2 ** ADAM_STEP)
    delta = -ADAM_LR * (m_hat / (jnp.sqrt(v_hat) + ADAM_EPS) + ADAM_WD * w)
    return delta, m, v


def _adamw_slots(w, m, v, slots, tr, name, own=None):
    R, C = w.shape
    extra = [] if own is None else [own]

    def body(w_ref, m_ref, v_ref, s_ref, *refs):
        g_ref, d_ref, nm_ref, nv_ref = refs[len(extra):]
        g = s_ref[0].astype(_F32)
        for k in range(1, N_DEV):
            g = g + s_ref[k].astype(_F32)
        if extra:
            g = g + refs[0][...].astype(_F32)
        g_ref[...] = g
        d_ref[...], nm_ref[...], nv_ref[...] = _adamw_math(w_ref[...], g, m_ref[...], v_ref[...])

    blk = pl.BlockSpec((tr, C), lambda i: (i, 0))
    return pl.pallas_call(
        body, name=name, grid=(R // tr,),
        in_specs=[blk, blk, blk, pl.BlockSpec((N_DEV, tr, C), lambda i: (0, i, 0))] + [blk] * len(extra),
        out_specs=(blk, blk, blk, blk),
        out_shape=tuple(jax.ShapeDtypeStruct((R, C), _F32) for _ in range(4)),
        compiler_params=_cparams(("arbitrary",)),
    )(w, m, v, slots, *extra)


def _w_ada_update(c_all, dmod_my, w, m, v):
    def body(c_ref, d_ref, w_ref, m_ref, v_ref, g_ref, dl_ref, nm_ref, nv_ref):
        g = lax.dot_general(_silu(c_ref[...]), d_ref[...], (((0,), (0,)), ((), ())),
                            precision=_HI, preferred_element_type=_F32)
        g_ref[...] = g
        dl_ref[...], nm_ref[...], nv_ref[...] = _adamw_math(w_ref[...], g, m_ref[...], v_ref[...])

    return pl.pallas_call(
        body, name="w_ada_update",
        out_shape=tuple(jax.ShapeDtypeStruct(w.shape, _F32) for _ in range(4)),
        compiler_params=_cparams(),
    )(c_all, dmod_my, w, m, v)


_PACK = (("b_ada", 3072, 3072), ("norm_w", 1024, 1024), ("b_in", IN_W, IN_PAD), ("conv_w", 5120, 5120),
         ("conv_b", 1024, 1024), ("rpb", 3720, 3840), ("ml_norm_w", 512, 512), ("final_norm_w", 1024, 1024),
         ("loss", 1, 128))
_PACK_OFF = {}
_off = 0
for _name, _len, _pad in _PACK:
    _PACK_OFF[_name] = (_off, _len)
    _off += _pad
_PACK_LEN = _off


def _pack(parts):
    cols = []
    for name, length, pad in _PACK:
        vec = parts[name].reshape(-1).astype(_F32)
        cols.append(jnp.pad(vec, (0, pad - length)))
    return jnp.concatenate(cols).reshape(1, _PACK_LEN)


def _unpack(vec, name, shape):
    off, length = _PACK_OFF[name]
    return vec.reshape(-1)[off:off + length].reshape(shape)


def kernel(x, c, w_ada, b_ada, norm_w, w_in, b_in, conv_w, conv_b, rpb, ml_norm_w, w_out, final_norm_w, loss_target, m_w_ada, m_b_ada, m_norm_w, m_w_in, m_b_in, m_conv_w, m_conv_b, m_rpb, m_ml_norm_w, m_w_out, m_final_norm_w, v_w_ada, v_b_ada, v_norm_w, v_w_in, v_b_in, v_conv_w, v_conv_b, v_rpb, v_ml_norm_w, v_w_out, v_final_norm_w):
    T = x.shape[1]
    rows = T // GRID_W
    me = 4 * lax.axis_index("x") + 2 * lax.axis_index("y") + lax.axis_index("c")
    n_in = w_in.shape[2]
    n_ada = w_ada.shape[2]
    n_cw = conv_w.shape[2]
    n_wo = w_out.shape[1]

    w_in_my, w_out_my = w_in[0].astype(_BF16), w_out[0].astype(_BF16)
    first_leg = (1,) + _CHIP_PEERS
    start_in = _scatter_start([w_in_my], "w_in_start", scatter=False, ks=first_leg)
    start_out = _scatter_start([w_out_my], "w_out_start", scatter=False)
    tokens = start_in[-1][0:1, 0:1] + start_out[-1][0:1, 0:1]
    g_conv_w, g_c = _exchange([conv_w[0], c + tokens], [False] * 2, "gather_small")
    b_in_pad = jnp.pad(b_in, ((0, 0), (0, IN_PAD - IN_W)))
    conv_w_full = jnp.pad(g_conv_w.transpose(1, 0, 2).reshape(CONV_W, N_DEV * n_cw), ((0, 3), (0, 0)))
    c_all = g_c.reshape(N_DEV, D_MODEL)

    b_ada_my = lax.dynamic_slice(b_ada, (0, me * n_ada), (1, n_ada))
    (mod_slots,) = _exchange([_mod_part(c_all, w_ada[0], b_ada_my)], [False], "gather_mod")
    mod = lax.dynamic_index_in_dim(mod_slots, me, axis=1, keepdims=False).reshape(1, 3 * D_MODEL)
    shift, scale, gate = mod[:, :D_MODEL], mod[:, D_MODEL:2 * D_MODEL], mod[:, 2 * D_MODEL:]
    scale1p = 1.0 + scale
    bias = _na_bias_tables(rpb[0], rows)

    def own_slot(land, own):
        return lax.dynamic_update_slice(land, own[None], (me,) + (0,) * own.ndim)

    def gathered(started, after, name):
        (own,), (land,) = _scatter_wait(started, after, name, scatter=False)
        return own_slot(land, own)

    (w_in_own,), (w_in_land,) = _scatter_wait(start_in, bias[0, 0, :8, :128] + scale1p[:, :128], "w_in_wait",
                                              scatter=False, ks=first_leg)
    g_w_in = own_slot(_relay_wait(_relay_start(w_in_land, "w_in_relay_start"), "w_in_relay_wait"), w_in_own)
    w_in_full = g_w_in.transpose(1, 0, 2).reshape(D_MODEL, N_DEV * n_in)
    w_in_bf = jnp.pad(w_in_full, ((0, 0), (0, IN_PAD - IN_W)))

    qkv, rest, h_bf = _inproj_fwd(x, scale1p, shift, norm_w, w_in_bf, b_in_pad)
    o_na, lse = _na_fwd(qkv, bias, T)
    qk_act = _conv_fwd(rest, conv_w_full, conv_b, T)
    h_f, *saved_f = _mlstm_fwd(qk_act, rest, T, False)
    h_b, *saved_b = _mlstm_fwd(qk_act, rest, T, True)

    w_out_bf = gathered(start_out, saved_b[2], "w_out_wait").reshape(N_DEV * n_wo, D_MODEL)
    dx1, d_o, dz_na, d_h, d_mo, d_mz, dwo, pvec = _post(
        x, loss_target, o_na, rest, h_f, h_b, gate, ml_norm_w, final_norm_w.reshape(1, D_MODEL), w_out_bf, T)

    dq_na, dk_na, dv_na, dbias = _na_bwd(qkv, bias, o_na, d_o, lse, T)
    d_rpb = _rpb_grad(dbias, rows)
    dqk_f, dv_f, dg_f = _mlstm_bwd(qk_act, rest, d_h, saved_f, T, False)
    dqk_b, dv_b, dg_b = _mlstm_bwd(qk_act, rest, d_h, saved_b, T, True)
    d_u, dconv = _conv_bwd(rest, conv_w_full, conv_b, dqk_f, dqk_b, T)

    sections = [(0, 512, [(dq_na, 0)]), (512, 512, [(dk_na, 0)]), (1024, 512, [(dv_na, 0)]),
                (1536, 512, [(dz_na, 0)]), (2048, 512, [(d_u, 0)]), (2560, 512, [(d_u, 1)]),
                (3072, 512, [(dv_f, 0), (dv_b, 0)]), (3584, 512, [(d_mo, 0)]), (4096, 512, [(d_mz, 0)]),
                (4608, 128, [(dg_f, 0), (dg_b, 0)])]
    dw_pad, db_pad = _inproj_bwd_w(h_bf, sections, T)
    db_in = db_pad[0, :IN_W]

    dw_blocks = dw_pad[:, :IN_W].astype(_BF16).reshape(D_MODEL, N_DEV, n_in).transpose(1, 0, 2)
    dwo_blocks = dwo.astype(_BF16).reshape(N_DEV, n_wo, D_MODEL)
    started = _scatter_start([dw_blocks, dwo_blocks], "grads_start")
    grad_x, xvec = _inproj_bwd_x(x, dx1, scale1p + started[-1][0:1, 0:1], norm_w, w_in_bf, sections, T)
    (dw_blocks, dwo_blocks), (s_w_in, s_w_out) = _scatter_wait(started, xvec, "grads_wait")

    small = _pack({
        "b_ada": jnp.concatenate([xvec[0], xvec[1], pvec[1]]),
        "norm_w": xvec[2], "b_in": db_in, "conv_w": dconv[:CONV_W], "conv_b": dconv[CONV_W],
        "rpb": d_rpb, "ml_norm_w": pvec[2, :ML_W], "final_norm_w": pvec[0], "loss": pvec[3, :1]})
    (s_small,) = _exchange([small], [False], "exchange_small")

    own = lambda blocks: lax.dynamic_index_in_dim(blocks, me, axis=0, keepdims=False)
    g_w_in_s, d_w_in, nm_w_in, nv_w_in = _adamw_slots(
        w_in[0], m_w_in[0], v_w_in[0], s_w_in, 128, "adamw_w_in", own=own(dw_blocks))
    g_w_out_s, d_w_out, nm_w_out, nv_w_out = _adamw_slots(
        w_out[0], m_w_out[0], v_w_out[0], s_w_out, n_wo, "adamw_w_out", own=own(dwo_blocks))
    dmod_all = s_small[:, 0, :3 * D_MODEL]
    dmod_my = lax.dynamic_slice(dmod_all, (0, me * n_ada), (N_DEV, n_ada))
    g_w_ada, d_w_ada, nm_w_ada, nv_w_ada = _w_ada_update(c_all, dmod_my, w_ada[0], m_w_ada[0], v_w_ada[0])

    def embed(shard):
        return lax.dynamic_update_slice(jnp.zeros((CONV_W, N_DEV * n_cw), _F32), shard[0], (0, me * n_cw))

    zero1 = jnp.zeros((1,), _F32)
    packed = lambda b_a, n_w, b_i, c_w, c_b, rp, mn, fn: _pack({
        "b_ada": b_a, "norm_w": n_w, "b_in": b_i, "conv_w": embed(c_w), "conv_b": c_b, "rpb": rp,
        "ml_norm_w": mn, "final_norm_w": fn, "loss": zero1})
    pw = packed(b_ada, norm_w, b_in, conv_w, conv_b, rpb, ml_norm_w, final_norm_w)
    pm = packed(m_b_ada, m_norm_w, m_b_in, m_conv_w, m_conv_b, m_rpb, m_ml_norm_w, m_final_norm_w)
    pv = packed(v_b_ada, v_norm_w, v_b_in, v_conv_w, v_conv_b, v_rpb, v_ml_norm_w, v_final_norm_w)
    sg, sd, sm, sv = _adamw_slots(pw, pm, pv, s_small, 1, "adamw_small")

    def small_outs(vec):
        cw = lax.dynamic_slice(_unpack(vec, "conv_w", (CONV_W, N_DEV * n_cw)), (0, me * n_cw), (CONV_W, n_cw))
        return dict(b_ada=_unpack(vec, "b_ada", b_ada.shape), norm_w=_unpack(vec, "norm_w", norm_w.shape),
                    b_in=_unpack(vec, "b_in", b_in.shape), conv_w=cw[None],
                    conv_b=_unpack(vec, "conv_b", conv_b.shape), rpb=_unpack(vec, "rpb", rpb.shape),
                    ml_norm_w=_unpack(vec, "ml_norm_w", ml_norm_w.shape),
                    final_norm_w=_unpack(vec, "final_norm_w", final_norm_w.shape))

    loss = _unpack(sg, "loss", ())
    order = ("w_ada", "b_ada", "norm_w", "w_in", "b_in", "conv_w", "conv_b", "rpb", "ml_norm_w", "w_out",
             "final_norm_w")
    outs = []
    for vec, big in ((sg, (g_w_ada, g_w_in_s, g_w_out_s)), (sd, (d_w_ada, d_w_in, d_w_out)),
                     (sm, (nm_w_ada, nm_w_in, nm_w_out)), (sv, (nv_w_ada, nv_w_in, nv_w_out))):
        group = small_outs(vec)
        group.update(w_ada=big[0][None], w_in=big[1][None], w_out=big[2][None])
        outs.extend(group[name] for name in order)
    return (loss, grad_x, *outs)
```

```python
import functools

import numpy as np
import jax
import jax.numpy as jnp
from jax import lax
from jax.experimental import pallas as pl
from jax.experimental.pallas import tpu as pltpu

N_DEV = 8
D_MODEL = 1024
GRID_W = 64
NA_HEADS = 8
NA_HEAD_DIM = 64
NA_KH = 8
NA_KW = 16
NA_W = 512
ML_HEADS = 4
ML_HEAD_DIM = 128
ML_W = 512
ML_CHUNK = 512
CONV_W = 5
EPS = 1e-6
IN_W = 4624
IN_PAD = 4736
REST_W = IN_PAD - 3 * NA_W
GATE_COL = 3072
NEG = -1e30
NA_RB = 4
NA_WIN = 12
NA_SUB = 8
ML_CB = 1
ADAM_LR = 0.001
ADAM_B1 = 0.9
ADAM_B2 = 0.999
ADAM_EPS = 1e-08
ADAM_WD = 0.01
ADAM_STEP = 10
VMEM_LIMIT = 56 * 1024 * 1024

_F32 = jnp.float32
_BF16 = jnp.bfloat16
_HI = lax.Precision.HIGHEST


def _cparams(sem=None):
    return pltpu.CompilerParams(dimension_semantics=sem, vmem_limit_bytes=VMEM_LIMIT)


def _nt(a, b):
    return lax.dot_general(a, b, (((1,), (1,)), ((), ())), preferred_element_type=_F32)


def _tn(a, b):
    return lax.dot_general(a, b, (((0,), (0,)), ((), ())), preferred_element_type=_F32)


def _nn(a, b):
    return jnp.dot(a, b, preferred_element_type=_F32)


def _sigmoid(x):
    return 1.0 / (1.0 + jnp.exp(-x))


def _silu(x):
    return x * _sigmoid(x)


def _dsilu(x):
    s = _sigmoid(x)
    return s * (1.0 + x * (1.0 - s))


def _exchange(arrs, scatter, name):
    n = len(arrs)
    out_shape = []
    for a, sc in zip(arrs, scatter):
        blk = a.shape[1:] if sc else a.shape
        out_shape.append(jax.ShapeDtypeStruct((N_DEV,) + tuple(blk), a.dtype))

    def body(*refs):
        ins = refs[:n]
        outs = refs[n:2 * n]
        send_sems, recv_sems, local_sems = refs[2 * n:]
        x, y, c = lax.axis_index("x"), lax.axis_index("y"), lax.axis_index("c")
        me = 4 * x + 2 * y + c
        local, sends, recvs = [], [], []
        for a in range(n):
            own = ins[a].at[me] if scatter[a] else ins[a]
            cp = pltpu.make_async_copy(own, outs[a].at[me], local_sems.at[a])
            cp.start()
            local.append(cp)
            for k in range(1, N_DEV):
                px = 1 - x if k & 4 else x
                py = 1 - y if k & 2 else y
                pc = 1 - c if k & 1 else c
                p = 4 * px + 2 * py + pc
                src = ins[a].at[p] if scatter[a] else ins[a]
                snd = pltpu.make_async_remote_copy(
                    src_ref=src, dst_ref=outs[a].at[me],
                    send_sem=send_sems.at[a, k - 1], recv_sem=recv_sems.at[a, k - 1],
                    device_id=(px, py, pc), device_id_type=pl.DeviceIdType.MESH)
                snd.start()
                sends.append(snd)
                rcv = pltpu.make_async_remote_copy(
                    src_ref=src, dst_ref=outs[a].at[p],
                    send_sem=send_sems.at[a, k - 1], recv_sem=recv_sems.at[a, k - 1],
                    device_id=(px, py, pc), device_id_type=pl.DeviceIdType.MESH)
                recvs.append(rcv)
        for rcv in recvs:
            rcv.wait_recv()
        for snd in sends:
            snd.wait_send()
        for cp in local:
            cp.wait()

    any_spec = pl.BlockSpec(memory_space=pl.ANY)
    res = pl.pallas_call(
        body, name=name, out_shape=tuple(out_shape),
        in_specs=[any_spec] * n, out_specs=tuple([any_spec] * n),
        scratch_shapes=[pltpu.SemaphoreType.DMA((n, N_DEV - 1)),
                        pltpu.SemaphoreType.DMA((n, N_DEV - 1)),
                        pltpu.SemaphoreType.DMA((n,))],
    )(*arrs)
    return list(res)


def _peer(k):
    x, y, c = lax.axis_index("x"), lax.axis_index("y"), lax.axis_index("c")
    px = 1 - x if k & 4 else x
    py = 1 - y if k & 2 else y
    pc = 1 - c if k & 1 else c
    return (px, py, pc), 4 * px + 2 * py + pc, 4 * x + 2 * y + c


_ALL_PEERS = tuple(range(1, N_DEV))
_CHIP_PEERS = (2, 4, 6)


def _scatter_copy(srcs, lands, send_sems, recv_sems, a, k, receive, scatter, ks=_ALL_PEERS):
    dev, p, me = _peer(k)
    at = a * len(ks) + ks.index(k)
    return pltpu.make_async_remote_copy(
        src_ref=srcs[a].at[p] if scatter else srcs[a], dst_ref=lands[a].at[p if receive else me],
        send_sem=send_sems[at], recv_sem=recv_sems[at],
        device_id=dev, device_id_type=pl.DeviceIdType.MESH)


def _scatter_start(arrs, name, scatter=True, ks=_ALL_PEERS):
    n = len(arrs)
    ns = n * len(ks)
    hbm = pl.BlockSpec(memory_space=pltpu.HBM)
    sem = pl.BlockSpec(memory_space=pltpu.SEMAPHORE)

    def body(*refs):
        srcs, lands = refs[:n], refs[n:2 * n]
        send_sems, recv_sems = refs[2 * n:2 * n + ns], refs[2 * n + ns:2 * n + 2 * ns]
        token = refs[-1]
        for a in range(n):
            for k in ks:
                _scatter_copy(srcs, lands, send_sems, recv_sems, a, k, False, scatter, ks).start()
        token[...] = jnp.zeros_like(token)

    land_shapes = [a.shape if scatter else (N_DEV,) + a.shape for a in arrs]
    buffers = [pltpu.HBM(a.shape, a.dtype) for a in arrs]
    land_buffers = [pltpu.HBM(s, a.dtype) for s, a in zip(land_shapes, arrs)]
    sems = [pltpu.SemaphoreType.DMA(()) for _ in range(2 * ns)]
    res = pl.pallas_call(
        body, name=name,
        out_shape=(*sems, *buffers, *land_buffers, jax.ShapeDtypeStruct((8, 128), _F32)),
        in_specs=[hbm] * (2 * n),
        out_specs=(*([sem] * (2 * ns)), *([hbm] * (2 * n)), pl.BlockSpec(memory_space=pltpu.VMEM)),
        input_output_aliases={i: 2 * ns + i for i in range(2 * n)},
        compiler_params=pltpu.CompilerParams(has_side_effects=pltpu.SideEffectType.DATAFLOW_SIDE_EFFECTING),
    )(*[pltpu.with_memory_space_constraint(a, pltpu.HBM) for a in arrs],
      *[pltpu.with_memory_space_constraint(jnp.zeros(s, a.dtype), pltpu.HBM) for s, a in zip(land_shapes, arrs)])
    res = list(res)
    return (res[:ns], res[ns:2 * ns], res[2 * ns:2 * ns + n], res[2 * ns + n:2 * ns + 2 * n], res[-1])


def _scatter_wait(started, after, name, scatter=True, ks=_ALL_PEERS):
    send_sems, recv_sems, srcs, lands, _ = started
    n = len(srcs)
    ns = len(send_sems)
    hbm = pl.BlockSpec(memory_space=pltpu.HBM)
    sem = pl.BlockSpec(memory_space=pltpu.SEMAPHORE)

    def body(*refs):
        src_refs, land_refs = refs[:n], refs[n:2 * n]
        s_sems, r_sems = refs[2 * n:2 * n + ns], refs[2 * n + ns:2 * n + 2 * ns]
        for a in range(n):
            for k in ks:
                _scatter_copy(src_refs, land_refs, s_sems, r_sems, a, k, False, scatter, ks).wait_send()
                _scatter_copy(src_refs, land_refs, s_sems, r_sems, a, k, True, scatter, ks).wait_recv()

    buffers = [pltpu.HBM(a.shape, a.dtype) for a in list(srcs) + list(lands)]
    res = pl.pallas_call(
        body, name=name, out_shape=tuple(buffers),
        in_specs=[hbm] * (2 * n) + [sem] * (2 * ns) + [pl.BlockSpec(memory_space=pl.ANY)],
        out_specs=tuple([hbm] * (2 * n)),
        input_output_aliases={i: i for i in range(2 * n)},
        compiler_params=pltpu.CompilerParams(has_side_effects=pltpu.SideEffectType.DATAFLOW_SIDE_EFFECTING),
    )(*srcs, *lands, *send_sems, *recv_sems, after)
    return list(res[:n]), list(res[n:])


def _relay_copy(land, send_sems, recv_sems, j, receive):
    k = _CHIP_PEERS[j]
    sibling, _, _ = _peer(1)
    _, slot, _ = _peer(k | 1 if receive else k)
    return pltpu.make_async_remote_copy(
        src_ref=land.at[slot], dst_ref=land.at[slot], send_sem=send_sems[j], recv_sem=recv_sems[j],
        device_id=sibling, device_id_type=pl.DeviceIdType.MESH)


def _relay_start(land, name):
    ns = len(_CHIP_PEERS)
    hbm = pl.BlockSpec(memory_space=pltpu.HBM)
    sem = pl.BlockSpec(memory_space=pltpu.SEMAPHORE)

    def body(*refs):
        land_ref = refs[0]
        send_sems, recv_sems = refs[1:1 + ns], refs[1 + ns:1 + 2 * ns]
        for j in range(ns):
            _relay_copy(land_ref, send_sems, recv_sems, j, False).start()
        refs[-1][...] = jnp.zeros_like(refs[-1])

    sems = [pltpu.SemaphoreType.DMA(()) for _ in range(2 * ns)]
    res = pl.pallas_call(
        body, name=name,
        out_shape=(*sems, pltpu.HBM(land.shape, land.dtype), jax.ShapeDtypeStruct((8, 128), _F32)),
        in_specs=[hbm],
        out_specs=(*([sem] * (2 * ns)), hbm, pl.BlockSpec(memory_space=pltpu.VMEM)),
        input_output_aliases={0: 2 * ns},
        compiler_params=pltpu.CompilerParams(has_side_effects=pltpu.SideEffectType.DATAFLOW_SIDE_EFFECTING),
    )(land)
    res = list(res)
    return res[:ns], res[ns:2 * ns], res[2 * ns], res[-1]


def _relay_wait(started, name):
    send_sems, recv_sems, land, token = started
    ns = len(send_sems)
    hbm = pl.BlockSpec(memory_space=pltpu.HBM)
    sem = pl.BlockSpec(memory_space=pltpu.SEMAPHORE)

    def body(*refs):
        land_ref = refs[0]
        s_sems, r_sems = refs[1:1 + ns], refs[1 + ns:1 + 2 * ns]
        for j in range(ns):
            _relay_copy(land_ref, s_sems, r_sems, j, False).wait_send()
            _relay_copy(land_ref, s_sems, r_sems, j, True).wait_recv()

    return pl.pallas_call(
        body, name=name, out_shape=pltpu.HBM(land.shape, land.dtype),
        in_specs=[hbm] + [sem] * (2 * ns) + [pl.BlockSpec(memory_space=pl.ANY)],
        out_specs=hbm, input_output_aliases={0: 0},
        compiler_params=pltpu.CompilerParams(has_side_effects=pltpu.SideEffectType.DATAFLOW_SIDE_EFFECTING),
    )(land, *send_sems, *recv_sems, token)


def _mod_part(c_all, w_ada, b_my):
    def body(c_ref, w_ref, b_ref, o_ref):
        o_ref[...] = jnp.dot(_silu(c_ref[...]), w_ref[...], precision=_HI,
                             preferred_element_type=_F32) + b_ref[...]

    return pl.pallas_call(
        body, name="mod_part",
        out_shape=jax.ShapeDtypeStruct((N_DEV, w_ada.shape[1]), _F32),
        compiler_params=_cparams(),
    )(c_all, w_ada, b_my)


def _inproj_fwd(x, scale1p, shift, norm_w, w_in_bf, b_in_pad):
    T = x.shape[1]
    tm = 512
    n_q = 3 * NA_W

    def body(x_ref, sc_ref, sh_ref, nw_ref, w_ref, b_ref, qkv_ref, rest_ref, h_ref):
        xv = x_ref[...]
        r = lax.rsqrt(jnp.mean(xv * xv, axis=-1, keepdims=True) + EPS)
        h = xv * r * nw_ref[...] * sc_ref[...] + sh_ref[...]
        hb = h.astype(_BF16)
        h_ref[...] = h.T.astype(_BF16)
        for n0 in range(0, IN_PAD, 512):
            wd = min(512, IN_PAD - n0)
            acc = _nn(hb, w_ref[:, n0:n0 + wd]) + b_ref[:, n0:n0 + wd]
            if n0 == 0:
                acc = acc * (NA_HEAD_DIM ** -0.5)
            if n0 < n_q:
                qkv_ref[:, n0:n0 + wd] = acc.astype(_BF16)
            else:
                rest_ref[:, n0 - n_q:n0 - n_q + wd] = acc

    row = lambda w: pl.BlockSpec((1, w), lambda i: (0, 0))
    return pl.pallas_call(
        body, name="inproj_fwd", grid=(T // tm,),
        in_specs=[pl.BlockSpec((None, tm, D_MODEL), lambda i: (0, i, 0)), row(D_MODEL), row(D_MODEL), row(D_MODEL),
                  pl.BlockSpec((D_MODEL, IN_PAD), lambda i: (0, 0), pipeline_mode=pl.Buffered(1)), row(IN_PAD)],
        out_specs=(pl.BlockSpec((tm, n_q), lambda i: (i, 0)),
                   pl.BlockSpec((tm, REST_W), lambda i: (i, 0)),
                   pl.BlockSpec((D_MODEL, tm), lambda i: (0, i))),
        out_shape=(jax.ShapeDtypeStruct((T, n_q), _BF16),
                   jax.ShapeDtypeStruct((T, REST_W), _F32),
                   jax.ShapeDtypeStruct((D_MODEL, T), _BF16)),
        compiler_params=_cparams(("arbitrary",)),
    )(x, scale1p, shift, norm_w, w_in_bf, b_in_pad)


def _na_class_rows(rows):
    nb = rows // NA_RB
    out = []
    for rb in (0, min(1, nb - 1), nb - 1):
        ws = int(np.clip(NA_RB * rb - 4, 0, rows - NA_WIN))
        out.append((NA_RB * rb + np.arange(NA_RB), ws + np.arange(NA_WIN)))
    return out


def _na_pair_index(rows, qrows, krows):
    start = lambda r: np.clip(r - NA_KH // 2, 0, rows - NA_KH)
    col = np.arange(GRID_W)
    cstart = np.clip(col - NA_KW // 2, 0, GRID_W - NA_KW)
    dy = krows[None, :] - qrows[:, None] + NA_KH - 1
    vr = (krows[None, :] >= start(qrows)[:, None]) & (krows[None, :] < start(qrows)[:, None] + NA_KH)
    dx = np.clip(col[None, :] - col[:, None], -(NA_KW - 1), NA_KW - 1) + NA_KW - 1
    vc = (col[None, :] >= cstart[:, None]) & (col[None, :] < cstart[:, None] + NA_KW)
    nq, nk = len(qrows), len(krows)
    dy4 = np.broadcast_to(np.clip(dy, 0, 2 * NA_KH - 2)[:, None, :, None], (nq, GRID_W, nk, GRID_W))
    dx4 = np.broadcast_to(dx[None, :, None, :], (nq, GRID_W, nk, GRID_W))
    valid = vr[:, None, :, None] & vc[None, :, None, :]
    idx = (dy4 * (2 * NA_KW - 1) + dx4).reshape(nq * GRID_W, nk * GRID_W)
    return idx.astype(np.int32), valid.reshape(nq * GRID_W, nk * GRID_W), (dy, vr, dx, vc)


def _na_half_slabs(rpb):
    _, _, (_, _, dx, vc) = _na_pair_index(NA_WIN, np.arange(1), np.arange(1))
    qc, kc = np.meshgrid(np.arange(GRID_W), np.arange(GRID_W), indexing="ij")
    consts = []
    for right in (False, True):
        pos = (qc * 128 + (GRID_W if right else 0) + kc).reshape(-1)
        oh = np.zeros((32, GRID_W * 128), np.float32)
        oh[dx[qc, kc].reshape(-1), pos] = 1.0
        col_neg = np.zeros((1, GRID_W * 128), np.float32)
        col_neg[0, pos] = np.where(vc[qc, kc].reshape(-1), 0.0, NEG)
        half = np.zeros((1, GRID_W * 128), np.float32)
        half[0, pos] = 1.0
        consts += [jnp.asarray(oh), jnp.asarray(col_neg), jnp.asarray(half)]
    row_neg = np.where(np.arange(NA_HEADS * 16) % 16 == 15, NEG, 0.0).astype(np.float32).reshape(-1, 1)
    rp = jnp.pad(rpb, ((0, 0), (0, 1), (0, 1))).reshape(NA_HEADS * 16, 32)

    def body(*refs):
        r_ref, rn_ref = refs[0], refs[1]
        for t in range(2):
            oh_ref, cn_ref, half_ref = refs[2 + 3 * t:5 + 3 * t]
            refs[8 + t][...] = (jnp.dot(r_ref[...], oh_ref[...], precision=_HI, preferred_element_type=_F32)
                                + cn_ref[...] + rn_ref[...] * half_ref[...])

    outs = pl.pallas_call(
        body, name="na_half_slabs",
        out_shape=tuple(jax.ShapeDtypeStruct((NA_HEADS * 16, GRID_W * 128), _F32) for _ in range(2)),
        compiler_params=_cparams(),
    )(rp, jnp.asarray(row_neg), *consts)
    return [o.reshape(NA_HEADS, 16, GRID_W, 128) for o in outs]


def _na_bias_tables(rpb, rows):
    left, right = _na_half_slabs(rpb)
    didx = []
    for blk, win in _na_class_rows(rows):
        _, _, (dy, vr, _, _) = _na_pair_index(rows, blk, win)
        didx.append(np.where(vr, dy, 15))

    def body(l_ref, r_ref, b_ref):
        for ci, tab in enumerate(didx):
            for a in range(NA_RB):
                for j in range(NA_WIN // 2):
                    b_ref[ci, 0, a * GRID_W:(a + 1) * GRID_W, j * 128:(j + 1) * 128] = (
                        l_ref[0, int(tab[a, 2 * j])] + r_ref[0, int(tab[a, 2 * j + 1])])

    slab = pl.BlockSpec((1, 16, GRID_W, 128), lambda h: (h, 0, 0, 0))
    return pl.pallas_call(
        body, name="na_tables", grid=(NA_HEADS,),
        in_specs=[slab] * 2,
        out_specs=pl.BlockSpec((3, 1, NA_RB * GRID_W, NA_WIN * GRID_W), lambda h: (0, h, 0, 0)),
        out_shape=jax.ShapeDtypeStruct((3, NA_HEADS, NA_RB * GRID_W, NA_WIN * GRID_W), _F32),
        compiler_params=_cparams(("arbitrary",)),
    )(left, right)


def _stack_heads(x, first):
    zero = jnp.zeros_like(x)
    return jnp.concatenate([jnp.where(first, x, zero), jnp.where(first, zero, x)], axis=0)


def _na_sub(rb, u, rows):
    sb = NA_SUB * rb + u
    nb = rows // NA_RB
    cls = jnp.where(sb == 0, 0, jnp.where(sb == nb - 1, 2, 1))
    ws = pl.multiple_of(jnp.clip(NA_RB * sb - 4, 0, rows - NA_WIN) * GRID_W, 256)
    return cls, ws


def _na_fwd(qkv, bias, T):
    rows = T // GRID_W
    tq = NA_RB * GRID_W
    tw = NA_WIN * GRID_W
    ts = NA_SUB * tq

    def body(q_ref, k_ref, v_ref, b_ref, o_ref, l_ref):
        rb = pl.program_id(1)
        lane = lax.broadcasted_iota(jnp.int32, (1, 128), 1)
        first = lane < NA_HEAD_DIM
        for u in range(NA_SUB):
            cls, ws = _na_sub(rb, u, rows)
            kw = k_ref[pl.ds(ws, tw), :]
            vw = v_ref[pl.ds(ws, tw), :]
            q2 = _stack_heads(q_ref[u * tq:(u + 1) * tq, :], first)
            s = _nt(q2, kw) + b_ref[cls].reshape(2 * tq, tw)
            m = jnp.max(s, axis=1, keepdims=True)
            p = jnp.exp(s - m)
            l = jnp.sum(p, axis=1, keepdims=True)
            o2 = _nn(p.astype(_BF16), vw) / l
            lse2 = m + jnp.log(l)
            o_ref[u * tq:(u + 1) * tq, :] = jnp.where(first, o2[:tq], o2[tq:])
            l_ref[u * tq:(u + 1) * tq, :] = jnp.where(first, lse2[:tq], lse2[tq:])

    blk = lambda off: pl.BlockSpec((ts, 128), lambda hp, rb: (rb, off + hp))
    whole = lambda off: pl.BlockSpec((T, 128), lambda hp, rb: (0, off + hp))
    return pl.pallas_call(
        body, name="na_fwd", grid=(NA_HEADS // 2, T // ts),
        in_specs=[blk(0), whole(4), whole(8),
                  pl.BlockSpec((3, 2, tq, tw), lambda hp, rb: (0, hp, 0, 0))],
        out_specs=(blk(0), blk(0)),
        out_shape=(jax.ShapeDtypeStruct((T, NA_W), _F32), jax.ShapeDtypeStruct((T, NA_W), _F32)),
        compiler_params=_cparams(("arbitrary", "arbitrary")),
    )(qkv, qkv, qkv, bias)


def _na_bwd(qkv, bias, o, d_o, lse, T):
    rows = T // GRID_W
    tq = NA_RB * GRID_W
    tw = NA_WIN * GRID_W
    ts = NA_SUB * tq

    def body(q_ref, k_ref, v_ref, b_ref, o_ref, do_ref, l_ref, dq_ref, dk_ref, dv_ref, db_ref):
        rb = pl.program_id(1)
        lane = lax.broadcasted_iota(jnp.int32, (1, 128), 1)
        first = lane < NA_HEAD_DIM

        @pl.when(rb == 0)
        def _():
            db_ref[...] = jnp.zeros_like(db_ref)
            dk_ref[...] = jnp.zeros_like(dk_ref)
            dv_ref[...] = jnp.zeros_like(dv_ref)

        for u in range(NA_SUB):
            cls, ws = _na_sub(rb, u, rows)
            kw = k_ref[pl.ds(ws, tw), :]
            vw = v_ref[pl.ds(ws, tw), :]
            sl = slice(u * tq, (u + 1) * tq)
            q = q_ref[sl, :]
            d_ov = do_ref[sl, :]
            prod = d_ov.astype(_F32) * o_ref[sl, :]
            lse_v = l_ref[sl, :]
            dqs = []
            dk_win = jnp.zeros((tw, 128), _F32)
            dv_win = jnp.zeros((tw, 128), _F32)
            for hh in range(2):
                msk = first if hh == 0 else jnp.logical_not(first)
                c0 = hh * NA_HEAD_DIM
                qm = jnp.where(msk, q, jnp.zeros_like(q))
                dom = jnp.where(msk, d_ov, jnp.zeros_like(d_ov))
                s = _nt(qm, kw) + b_ref[cls, hh]
                p = jnp.exp(s - lse_v[:, c0:c0 + 1])
                dp = _nt(dom, vw)
                delta = jnp.sum(jnp.where(msk, prod, 0.0), axis=1, keepdims=True)
                ds = p * (dp - delta)
                db_ref[cls, hh] += ds
                dsb = ds.astype(_BF16)
                dqs.append(_nn(dsb, kw) * (NA_HEAD_DIM ** -0.5))
                dk_win = dk_win + _tn(dsb, qm)
                dv_win = dv_win + _tn(p.astype(_BF16), dom)
            dq_ref[sl, :] = jnp.where(first, dqs[0], dqs[1]).astype(_BF16)
            dk_ref[pl.ds(ws, tw), :] += dk_win
            dv_ref[pl.ds(ws, tw), :] += dv_win

    once = pl.Buffered(1)
    blk = lambda off: pl.BlockSpec((ts, 128), lambda hp, rb: (rb, off + hp))
    whole = lambda off, **kw: pl.BlockSpec((T, 128), lambda hp, rb: (0, off + hp), **kw)
    tab = lambda **kw: pl.BlockSpec((3, 2, tq, tw), lambda hp, rb: (0, hp, 0, 0), **kw)
    return pl.pallas_call(
        body, name="na_bwd", grid=(NA_HEADS // 2, T // ts),
        in_specs=[blk(0), whole(4), whole(8), tab(), blk(0), blk(0), blk(0)],
        out_specs=(blk(0), whole(0, pipeline_mode=once), whole(0, pipeline_mode=once), tab(pipeline_mode=once)),
        out_shape=(jax.ShapeDtypeStruct((T, NA_W), _BF16), jax.ShapeDtypeStruct((T, NA_W), _F32),
                   jax.ShapeDtypeStruct((T, NA_W), _F32), jax.ShapeDtypeStruct(bias.shape, _F32)),
        compiler_params=_cparams(("arbitrary", "arbitrary")),
    )(qkv, qkv, qkv, bias, o, d_o, lse)


def _rpb_grad(dbias, rows):
    tw = NA_WIN * GRID_W
    lanes = 16 * GRID_W
    offs = [int(win[0] - blk[0] + NA_KH - 1) for blk, win in _na_class_rows(rows)]

    def body(x_ref, g_ref):
        sub = lax.broadcasted_iota(jnp.int32, (NA_RB, 1), 0)
        qc = lax.broadcasted_iota(jnp.int32, (NA_RB * GRID_W, 1), 0) % GRID_W
        tot = jnp.zeros((NA_RB, lanes), _F32)
        for ci in range(3):
            xv = x_ref[ci, 0]
            for bit in range(6):
                xv = jnp.where(((qc >> bit) & 1) == 1, pltpu.roll(xv, tw - (1 << bit), 1), xv)
            acc = pltpu.roll(jnp.sum(xv.reshape(NA_RB, GRID_W, tw), axis=1), NA_KW, 1)
            acc = jnp.concatenate([acc, jnp.zeros((NA_RB, lanes - tw), _F32)], axis=1)
            for a in range(NA_RB):
                tot = tot + jnp.where(sub == a, pltpu.roll(acc, (GRID_W * (offs[ci] - a)) % lanes, 1), 0.0)
        g_ref[0] = jnp.broadcast_to(jnp.sum(tot, axis=0, keepdims=True), (8, lanes))

    g = pl.pallas_call(
        body, name="rpb_grad", grid=(NA_HEADS,),
        in_specs=[pl.BlockSpec((3, 1) + dbias.shape[2:], lambda h: (0, h, 0, 0))],
        out_specs=pl.BlockSpec((1, 8, lanes), lambda h: (h, 0, 0)),
        out_shape=jax.ShapeDtypeStruct((NA_HEADS, 8, lanes), _F32),
        compiler_params=_cparams(("arbitrary",)),
    )(dbias)
    return g[:, 0].reshape(NA_HEADS, 16, GRID_W)[:, :2 * NA_KH - 1, 1:2 * NA_KW]


def _halo_specs(tm, width, col_of, T, order):
    hb = tm // 8
    last = T // 8 - 1
    if order == "ij":
        cur = pl.BlockSpec((tm, width), lambda i, j: (i, col_of(j)))
        prev = pl.BlockSpec((8, width), lambda i, j: (jnp.maximum(i * hb - 1, 0), col_of(j)))
        nxt = pl.BlockSpec((8, width), lambda i, j: (jnp.minimum((i + 1) * hb, last), col_of(j)))
    else:
        cur = pl.BlockSpec((tm, width), lambda j, i: (i, col_of(j)))
        prev = pl.BlockSpec((8, width), lambda j, i: (jnp.maximum(i * hb - 1, 0), col_of(j)))
        nxt = pl.BlockSpec((8, width), lambda j, i: (jnp.minimum((i + 1) * hb, last), col_of(j)))
    return [prev, cur, nxt]


def _extend(prev_ref, cur_ref, next_ref, i, n_i):
    prev = jnp.where(i > 0, prev_ref[...], 0.0)
    nxt = jnp.where(i < n_i - 1, next_ref[...], 0.0)
    return jnp.concatenate([prev, cur_ref[...], nxt], axis=0)


def _conv_fwd(rest, conv_w, conv_b, T):
    tm = 512
    n_i = T // tm
    n = tm + 16

    def body(p_ref, c_ref, n_ref, w_ref, b_ref, o_ref):
        i = pl.program_id(0)
        ext = _extend(p_ref, c_ref, n_ref, i, n_i)
        acc = jnp.zeros((tm, 512), _F32) + b_ref[...]
        for j in range(CONV_W):
            acc = acc + w_ref[j:j + 1, :] * pltpu.roll(ext, (2 - j) % n, 0)[8:8 + tm]
        o_ref[...] = _silu(acc)

    return pl.pallas_call(
        body, name="conv_fwd", grid=(n_i, 2),
        in_specs=_halo_specs(tm, 512, lambda j: 1 + j, T, "ij")
        + [pl.BlockSpec((8, 512), lambda i, j: (0, j)), pl.BlockSpec((1, 512), lambda i, j: (0, j))],
        out_specs=pl.BlockSpec((tm, 512), lambda i, j: (i, j)),
        out_shape=jax.ShapeDtypeStruct((T, 2 * ML_W), _F32),
        compiler_params=_cparams(("arbitrary", "arbitrary")),
    )(rest, rest, rest, conv_w, conv_b)


def _conv_bwd(rest, conv_w, conv_b, da_f, da_b, T):
    tm = 512
    n_i = T // tm
    n = tm + 16

    def body(up, uc, un, fp, fc, fn, bp, bc, bn, w_ref, b_ref, du_ref, dw_ref):
        i = pl.program_id(1)
        ext_u = _extend(up, uc, un, i, n_i)
        ext_da = _extend(fp, fc, fn, i, n_i) + _extend(bp, bc, bn, i, n_i)
        shifted = [pltpu.roll(ext_u, (2 - j) % n, 0) for j in range(CONV_W)]
        pre = jnp.zeros((n, 512), _F32) + b_ref[...]
        for j in range(CONV_W):
            pre = pre + w_ref[j:j + 1, :] * shifted[j]
        gidx = i * tm - 8 + lax.broadcasted_iota(jnp.int32, (n, 1), 0)
        dpre = jnp.where((gidx >= 0) & (gidx < T), ext_da * _dsilu(pre), 0.0)
        du = jnp.zeros((tm, 512), _F32)
        for j in range(CONV_W):
            du = du + w_ref[j:j + 1, :] * pltpu.roll(dpre, (j - 2) % n, 0)[8:8 + tm]
        du_ref[...] = du.astype(_BF16)
        dpc = dpre[8:8 + tm]
        parts = [jnp.sum(dpc * shifted[j][8:8 + tm], axis=0, keepdims=True) for j in range(CONV_W)]
        parts.append(jnp.sum(dpc, axis=0, keepdims=True))
        parts.append(jnp.zeros((2, 512), _F32))
        upd = jnp.concatenate(parts, axis=0)

        @pl.when(i == 0)
        def _():
            dw_ref[...] = upd

        @pl.when(i > 0)
        def _():
            dw_ref[...] += upd

    return pl.pallas_call(
        body, name="conv_bwd", grid=(2, n_i),
        in_specs=_halo_specs(tm, 512, lambda j: 1 + j, T, "ji")
        + _halo_specs(tm, 512, lambda j: j, T, "ji") + _halo_specs(tm, 512, lambda j: j, T, "ji")
        + [pl.BlockSpec((8, 512), lambda j, i: (0, j)), pl.BlockSpec((1, 512), lambda j, i: (0, j))],
        out_specs=(pl.BlockSpec((tm, 512), lambda j, i: (i, j)), pl.BlockSpec((8, 512), lambda j, i: (0, j))),
        out_shape=(jax.ShapeDtypeStruct((T, 2 * ML_W), _BF16), jax.ShapeDtypeStruct((8, 2 * ML_W), _F32)),
        compiler_params=_cparams(("arbitrary", "arbitrary")),
    )(rest, rest, rest, da_f, da_f, da_f, da_b, da_b, da_b, conv_w, conv_b)


def _scan_rows(x, suffix):
    L = x.shape[0]
    row = lax.broadcasted_iota(jnp.int32, (L, 1), 0)
    step = 1
    while step < L:
        if suffix:
            x = x + jnp.where(row < L - step, pltpu.roll(x, L - step, 0), 0.0)
        else:
            x = x + jnp.where(row >= step, pltpu.roll(x, step, 0), 0.0)
        step *= 2
    return x


def _ml_gates(gt, rev):
    L = gt.shape[0]
    ri = lax.broadcasted_iota(jnp.int32, (L, L), 0)
    ci = lax.broadcasted_iota(jnp.int32, (L, L), 1)
    mask = (ci >= ri) if rev else (ci <= ri)
    lf = jnp.minimum(gt, 0.0) - jnp.log(1.0 + jnp.exp(-jnp.abs(gt)))
    b = _scan_rows(lf, suffix=rev)
    return mask, b, b.T, gt.T


def _ml_head_gates(gt, gates, head, rev):
    _, b, b_t, gt_t = gates
    ci = (8 if rev else 0) + head
    cf = ci + ML_HEADS
    last = 0 if rev else gt.shape[0] - 1
    return dict(icol=gt[:, ci:ci + 1], b_col=b[:, cf:cf + 1], b_row=b_t[cf:cf + 1, :],
                i_row=gt_t[ci:ci + 1, :], bl=b[last:last + 1, cf:cf + 1])


def _ml_chunk(q, k, v, hg, mask, C, n, m, saved=None):
    icol, b_col, b_row, bl = hg["icol"], hg["b_col"], hg["b_row"], hg["bl"]
    if saved is None:
        dlog = jnp.where(mask, b_col - b_row + hg["i_row"], NEG)
        m_t = jnp.maximum(b_col + m, jnp.max(dlog, axis=1, keepdims=True))
        dm = jnp.exp(dlog - m_t)
    else:
        dm, m_t = saved[0].astype(_F32), saved[1]
    ks = k * (ML_HEAD_DIM ** -0.5)
    qb, kb, vb = q.astype(_BF16), ks.astype(_BF16), v.astype(_BF16)
    s = _nt(qb, kb) * dm
    g = jnp.exp(b_col + m - m_t)
    qc = _nt(qb, C.astype(_BF16))
    num = _nn(s.astype(_BF16), vb) + g * qc
    qn = jnp.sum(q * n, axis=1, keepdims=True)
    den = jnp.sum(s, axis=1, keepdims=True) + g * qn
    e_m = jnp.exp(-m_t)
    nrm = jnp.maximum(jnp.abs(den), e_m)
    h = num / nrm
    a_col = bl - b_col + icol
    m_new = jnp.maximum(bl + m, jnp.max(a_col, axis=0, keepdims=True))
    decay = jnp.exp(bl + m - m_new)
    w = jnp.exp(a_col - m_new)
    c_new = decay * C + _tn((w * v).astype(_BF16), kb)
    n_new = decay * n + jnp.sum(w * ks, axis=0, keepdims=True)
    aux = dict(dm=dm, m_t=m_t, ks=ks, qb=qb, kb=kb, vb=vb, s=s, g=g, qc=qc, qn=qn,
               den=den, e_m=e_m, nrm=nrm, decay=decay, w=w)
    return h, c_new, n_new, m_new, aux


def _mlstm_fwd(qk_act, rest, T, rev):
    tb = ML_CB * ML_CHUNK
    nblk = T // tb
    nc = T // ML_CHUNK
    bi = (lambda i: nblk - 1 - i) if rev else (lambda i: i)

    def body(q_ref, k_ref, v_ref, g_ref, h_ref, cs_ref, ns_ref, ms_ref, dm_ref, mt_ref, c_scr, n_scr, m_scr):
        @pl.when(pl.program_id(0) == 0)
        def _():
            c_scr[...] = jnp.zeros_like(c_scr)
            n_scr[...] = jnp.zeros_like(n_scr)
            m_scr[...] = jnp.zeros_like(m_scr)

        def step(j, carry):
            c = (ML_CB - 1 - j) if rev else j
            r0 = pl.multiple_of(c * ML_CHUNK, ML_CHUNK)
            gt = g_ref[pl.ds(r0, ML_CHUNK), :]
            gates = _ml_gates(gt, rev)
            lane = lax.broadcasted_iota(jnp.int32, (1, 128), 1)
            mt_tile = jnp.zeros((ML_CHUNK, 128), _F32)
            for hd in range(ML_HEADS):
                cols = slice(hd * ML_HEAD_DIM, (hd + 1) * ML_HEAD_DIM)
                C = c_scr[hd]
                n = n_scr[hd:hd + 1, :]
                mrow = m_scr[hd:hd + 1, :]
                cs_ref[c, hd] = C
                ns_ref[c, hd:hd + 1, :] = n
                ms_ref[c, hd:hd + 1, :] = mrow
                h, c_new, n_new, m_new, a = _ml_chunk(
                    q_ref[pl.ds(r0, ML_CHUNK), cols], k_ref[pl.ds(r0, ML_CHUNK), cols],
                    v_ref[pl.ds(r0, ML_CHUNK), cols], _ml_head_gates(gt, gates, hd, rev), gates[0],
                    C, n, mrow[:, 0:1])
                h_ref[pl.ds(r0, ML_CHUNK), cols] = h
                dm_ref[c, hd] = a["dm"].astype(_BF16)
                mt_tile = jnp.where(lane == hd, a["m_t"], mt_tile)
                c_scr[hd] = c_new
                n_scr[hd:hd + 1, :] = n_new
                m_scr[hd:hd + 1, :] = jnp.broadcast_to(m_new, (1, 128))
            mt_ref[c] = mt_tile
            return carry

        lax.fori_loop(0, ML_CB, step, 0)

    return pl.pallas_call(
        body, name="mlstm_fwd_rev" if rev else "mlstm_fwd", grid=(nblk,),
        in_specs=[pl.BlockSpec((tb, ML_W), lambda i: (bi(i), 0)),
                  pl.BlockSpec((tb, ML_W), lambda i: (bi(i), 1)),
                  pl.BlockSpec((tb, ML_W), lambda i: (bi(i), 3)),
                  pl.BlockSpec((tb, 128), lambda i: (bi(i), GATE_COL // 128))],
        out_specs=(pl.BlockSpec((tb, ML_W), lambda i: (bi(i), 0)),
                   pl.BlockSpec((ML_CB, ML_HEADS, 128, 128), lambda i: (bi(i), 0, 0, 0)),
                   pl.BlockSpec((ML_CB, ML_HEADS, 128), lambda i: (bi(i), 0, 0)),
                   pl.BlockSpec((ML_CB, ML_HEADS, 128), lambda i: (bi(i), 0, 0)),
                   pl.BlockSpec((ML_CB, ML_HEADS, ML_CHUNK, ML_CHUNK), lambda i: (bi(i), 0, 0, 0)),
                   pl.BlockSpec((ML_CB, ML_CHUNK, 128), lambda i: (bi(i), 0, 0))),
        out_shape=(jax.ShapeDtypeStruct((T, ML_W), _F32),
                   jax.ShapeDtypeStruct((nc, ML_HEADS, 128, 128), _F32),
                   jax.ShapeDtypeStruct((nc, ML_HEADS, 128), _F32),
                   jax.ShapeDtypeStruct((nc, ML_HEADS, 128), _F32),
                   jax.ShapeDtypeStruct((nc, ML_HEADS, ML_CHUNK, ML_CHUNK), _BF16),
                   jax.ShapeDtypeStruct((nc, ML_CHUNK, 128), _F32)),
        scratch_shapes=[pltpu.VMEM((ML_HEADS, 128, 128), _F32), pltpu.VMEM((8, 128), _F32),
                        pltpu.VMEM((8, 128), _F32)],
        compiler_params=_cparams(("arbitrary",)),
    )(qk_act, qk_act, rest, rest)


def _mlstm_bwd(qk_act, rest, d_h, saved, T, rev):
    tb = ML_CB * ML_CHUNK
    nblk = T // tb
    bi = (lambda i: i) if rev else (lambda i: nblk - 1 - i)

    def body(q_ref, k_ref, v_ref, g_ref, dh_ref, cs_ref, ns_ref, ms_ref, dm_ref, mt_ref,
             dqk_ref, dv_ref, dg_ref, dc_scr, dn_scr):
        @pl.when(pl.program_id(0) == 0)
        def _():
            dc_scr[...] = jnp.zeros_like(dc_scr)
            dn_scr[...] = jnp.zeros_like(dn_scr)

        def step(j, carry):
            c = j if rev else (ML_CB - 1 - j)
            r0 = pl.multiple_of(c * ML_CHUNK, ML_CHUNK)
            gt = g_ref[pl.ds(r0, ML_CHUNK), :]
            gates = _ml_gates(gt, rev)
            mask = gates[0]
            lane = lax.broadcasted_iota(jnp.int32, (1, 128), 1)
            sub = lax.broadcasted_iota(jnp.int32, (128, 1), 0)
            db_t = jnp.zeros((ML_CHUNK, 128), _F32)
            da_t = jnp.zeros((ML_CHUNK, 128), _F32)
            cs_rows = jnp.zeros((128, ML_CHUNK), _F32)
            dbl_t = jnp.zeros((1, 128), _F32)
            for hd in range(ML_HEADS):
                cols = slice(hd * ML_HEAD_DIM, (hd + 1) * ML_HEAD_DIM)
                ci = (8 if rev else 0) + hd
                cf = ci + ML_HEADS
                q = q_ref[pl.ds(r0, ML_CHUNK), cols]
                k = k_ref[pl.ds(r0, ML_CHUNK), cols]
                v = v_ref[pl.ds(r0, ML_CHUNK), cols]
                C = cs_ref[c, hd]
                n = ns_ref[c, hd:hd + 1, :]
                m = ms_ref[c, hd:hd + 1, :][:, 0:1]
                dcn = dc_scr[hd]
                dnn = dn_scr[hd:hd + 1, :]
                h, _, _, _, a = _ml_chunk(q, k, v, _ml_head_gates(gt, gates, hd, rev), mask, C, n, m,
                                          saved=(dm_ref[c, hd], mt_ref[c][:, hd:hd + 1]))
                d_hv = dh_ref[pl.ds(r0, ML_CHUNK), cols]
                g, s, w, ks = a["g"], a["s"], a["w"], a["ks"]
                qb, kb, vb = a["qb"], a["kb"], a["vb"]
                dnum = d_hv / a["nrm"]
                hdot = jnp.sum(d_hv * h, axis=1, keepdims=True)
                dden = jnp.where(jnp.abs(a["den"]) >= a["e_m"], -hdot / a["nrm"] * jnp.sign(a["den"]), 0.0)
                dnb = dnum.astype(_BF16)
                d_s = _nt(dnb, vb) + dden
                r = d_s * s
                dsqk = (d_s * a["dm"]).astype(_BF16)
                cb = C.astype(_BF16)
                dq = _nn(dsqk, kb) + g * _nn(dnb, cb) + (dden * g) * n
                dk = _tn(dsqk, qb)
                dv = _tn(s.astype(_BF16), dnb)
                dg = jnp.sum(dnum * a["qc"], axis=1, keepdims=True) + dden * a["qn"]
                db_col = jnp.sum(r, axis=1, keepdims=True) + dg * g
                cs_rows = cs_rows + jnp.where((sub == ci) | (sub == cf), jnp.sum(r, axis=0, keepdims=True), 0.0)
                dc_chunk = _tn((g * dnum).astype(_BF16), qb)
                dn_chunk = jnp.sum((dden * g) * q, axis=0, keepdims=True)
                dcb = dcn.astype(_BF16)
                vdc = _nn(vb, dcb)
                kdc = _nt(kb, dcb)
                dw = jnp.sum(vdc * ks, axis=1, keepdims=True) + jnp.sum(ks * dnn, axis=1, keepdims=True)
                dv = dv + w * kdc
                dk = dk + w * vdc + w * dnn
                da = dw * w
                ddecay = (jnp.sum(jnp.sum(dcn * C, axis=1, keepdims=True), axis=0, keepdims=True)
                          + jnp.sum(dnn * n, axis=1, keepdims=True))
                dbl = ddecay * a["decay"] + jnp.sum(da, axis=0, keepdims=True)
                db_t = db_t + jnp.where(lane == cf, db_col - da, 0.0)
                da_t = da_t + jnp.where(lane == ci, da, 0.0)
                dbl_t = dbl_t + jnp.where(lane == cf, dbl, 0.0)
                dc_scr[hd] = dc_chunk + a["decay"] * dcn
                dn_scr[hd:hd + 1, :] = dn_chunk + a["decay"] * dnn
                dqk_ref[pl.ds(r0, ML_CHUNK), cols] = dq
                dqk_ref[pl.ds(r0, ML_CHUNK), slice(ML_W + hd * 128, ML_W + (hd + 1) * 128)] = dk * (ML_HEAD_DIM ** -0.5)
                dv_ref[pl.ds(r0, ML_CHUNK), cols] = dv.astype(_BF16)
            lo = 8 if rev else 0
            is_i = (lane >= lo) & (lane < lo + ML_HEADS)
            is_f = (lane >= lo + ML_HEADS) & (lane < lo + 2 * ML_HEADS)
            cs_t = cs_rows.T
            db_all = db_t - jnp.where(is_f, cs_t, 0.0)
            dlf = _scan_rows(db_all, suffix=not rev) + dbl_t
            dg_ref[pl.ds(r0, ML_CHUNK), :] = (da_t + jnp.where(is_i, cs_t, 0.0)
                                               + jnp.where(is_f, dlf * _sigmoid(-gt), 0.0))
            return carry

        lax.fori_loop(0, ML_CB, step, 0)

    return pl.pallas_call(
        body, name="mlstm_bwd_rev" if rev else "mlstm_bwd", grid=(nblk,),
        in_specs=[pl.BlockSpec((tb, ML_W), lambda i: (bi(i), 0)),
                  pl.BlockSpec((tb, ML_W), lambda i: (bi(i), 1)),
                  pl.BlockSpec((tb, ML_W), lambda i: (bi(i), 3)),
                  pl.BlockSpec((tb, 128), lambda i: (bi(i), GATE_COL // 128)),
                  pl.BlockSpec((tb, ML_W), lambda i: (bi(i), 0)),
                  pl.BlockSpec((ML_CB, ML_HEADS, 128, 128), lambda i: (bi(i), 0, 0, 0)),
                  pl.BlockSpec((ML_CB, ML_HEADS, 128), lambda i: (bi(i), 0, 0)),
                  pl.BlockSpec((ML_CB, ML_HEADS, 128), lambda i: (bi(i), 0, 0)),
                  pl.BlockSpec((ML_CB, ML_HEADS, ML_CHUNK, ML_CHUNK), lambda i: (bi(i), 0, 0, 0)),
                  pl.BlockSpec((ML_CB, ML_CHUNK, 128), lambda i: (bi(i), 0, 0))],
        out_specs=(pl.BlockSpec((tb, 2 * ML_W), lambda i: (bi(i), 0)),
                   pl.BlockSpec((tb, ML_W), lambda i: (bi(i), 0)),
                   pl.BlockSpec((tb, 128), lambda i: (bi(i), 0))),
        out_shape=(jax.ShapeDtypeStruct((T, 2 * ML_W), _F32), jax.ShapeDtypeStruct((T, ML_W), _BF16),
                   jax.ShapeDtypeStruct((T, 128), _F32)),
        scratch_shapes=[pltpu.VMEM((ML_HEADS, 128, 128), _F32), pltpu.VMEM((8, 128), _F32)],
        compiler_params=_cparams(("arbitrary",)),
    )(qk_act, qk_act, rest, rest, d_h, *saved)


def _post(x, target, o_na, rest, h_f, h_b, gate, ml_norm_w, final_w, w_out_bf, T):
    tm = 256
    n_i = T // tm

    def body(x_ref, t_ref, o_ref, zna_ref, hf_ref, hb_ref, mo_ref, mz_ref, gate_ref, mw_ref, fw_ref, w_ref,
             dx1_ref, do_ref, dzna_ref, dh_ref, dmo_ref, dmz_ref, dwo_ref, vec_ref):
        i = pl.program_id(0)
        gate_v = gate_ref[...]
        fw = fw_ref[...]
        zna = zna_ref[...]
        o = o_ref[...]
        sig_zna = _sigmoid(zna)
        silu_zna = zna * sig_zna
        na_out = o * silu_zna
        hsum = hf_ref[...] + hb_ref[...]
        sg = _sigmoid(mo_ref[...])
        hm = hsum * sg
        mz = mz_ref[...]
        sig_mz = _sigmoid(mz)
        smz = mz * sig_mz
        dsilu_mz = sig_mz * (1.0 + mz * (1.0 - sig_mz))
        hn_l, rstd_l, ml_l = [], [], []
        for hd in range(ML_HEADS):
            cols = slice(hd * 128, (hd + 1) * 128)
            hh = hm[:, cols]
            mu = jnp.mean(hh, axis=-1, keepdims=True)
            var = jnp.mean(jnp.square(hh - mu), axis=-1, keepdims=True)
            rstd = lax.rsqrt(var + EPS)
            hn = (hh - mu) * rstd
            hn_l.append(hn)
            rstd_l.append(rstd)
            ml_l.append(hn * mw_ref[:, cols] * smz[:, cols])
        mix = jnp.concatenate([na_out] + ml_l, axis=1).astype(_BF16)
        y = _nn(mix, w_ref[...])
        x1 = x_ref[...] + gate_v * y
        r = lax.rsqrt(jnp.mean(x1 * x1, axis=-1, keepdims=True) + EPS)
        xhat = x1 * r
        out = xhat * fw
        err = out - t_ref[...]
        loss = 0.5 * jnp.sum(jnp.sum(err * err, axis=1, keepdims=True), axis=0, keepdims=True) / D_MODEL
        dout = err * (1.0 / D_MODEL)
        dfw = jnp.sum(dout * xhat, axis=0, keepdims=True)
        dxhat = dout * fw
        dx1 = r * (dxhat - xhat * jnp.mean(dxhat * xhat, axis=-1, keepdims=True))
        dx1_ref[...] = dx1
        dgate = jnp.sum(dx1 * y, axis=0, keepdims=True)
        dy = (dx1 * gate_v).astype(_BF16)
        dmix = _nt(dy, w_ref[...])
        dwo = _tn(mix, dy)
        dna = dmix[:, :NA_W]
        do_ref[...] = (dna * silu_zna).astype(_BF16)
        dzna_ref[...] = (dna * o * (sig_zna * (1.0 + zna * (1.0 - sig_zna)))).astype(_BF16)
        dmw_l = []
        for hd in range(ML_HEADS):
            cols = slice(hd * 128, (hd + 1) * 128)
            dml = dmix[:, NA_W + hd * 128:NA_W + (hd + 1) * 128]
            hn = hn_l[hd]
            mwv = mw_ref[:, cols]
            dmz_ref[:, cols] = (dml * hn * mwv * dsilu_mz[:, cols]).astype(_BF16)
            dhn = dml * mwv * smz[:, cols]
            dmw_l.append(jnp.sum(dml * hn * smz[:, cols], axis=0, keepdims=True))
            dhm = rstd_l[hd] * (dhn - jnp.mean(dhn, axis=-1, keepdims=True)
                                - hn * jnp.mean(dhn * hn, axis=-1, keepdims=True))
            sgc = sg[:, cols]
            dh_ref[:, cols] = dhm * sgc
            dmo_ref[:, cols] = (dhm * hsum[:, cols] * sgc * (1.0 - sgc)).astype(_BF16)
        dmw = jnp.concatenate(dmw_l + [jnp.zeros((1, D_MODEL - ML_W), _F32)], axis=1)
        lane = lax.broadcasted_iota(jnp.int32, (1, D_MODEL), 1)
        vec = jnp.concatenate([dfw, dgate, dmw, jnp.where(lane == 0, loss, 0.0),
                               jnp.zeros((4, D_MODEL), _F32)], axis=0)

        @pl.when(i == 0)
        def _():
            dwo_ref[...] = dwo
            vec_ref[...] = vec

        @pl.when(i > 0)
        def _():
            dwo_ref[...] += dwo
            vec_ref[...] += vec

    tok = lambda w, j: pl.BlockSpec((tm, w), lambda i: (i, j))
    tok3 = pl.BlockSpec((None, tm, D_MODEL), lambda i: (0, i, 0))
    row = lambda w: pl.BlockSpec((1, w), lambda i: (0, 0))
    f32 = lambda w: jax.ShapeDtypeStruct((T, w), _F32)
    bf16 = lambda w: jax.ShapeDtypeStruct((T, w), _BF16)
    return pl.pallas_call(
        body, name="post", grid=(n_i,),
        in_specs=[tok3, tok3, tok(NA_W, 0), tok(NA_W, 0), tok(ML_W, 0), tok(ML_W, 0),
                  tok(ML_W, 4), tok(ML_W, 5), row(D_MODEL), row(ML_W), row(D_MODEL),
                  pl.BlockSpec((D_MODEL, D_MODEL), lambda i: (0, 0))],
        out_specs=(tok(D_MODEL, 0), tok(NA_W, 0), tok(NA_W, 0), tok(ML_W, 0), tok(ML_W, 0), tok(ML_W, 0),
                   pl.BlockSpec((D_MODEL, D_MODEL), lambda i: (0, 0)),
                   pl.BlockSpec((8, D_MODEL), lambda i: (0, 0))),
        out_shape=(f32(D_MODEL), bf16(NA_W), bf16(NA_W), f32(ML_W), bf16(ML_W),
                   bf16(ML_W), jax.ShapeDtypeStruct((D_MODEL, D_MODEL), _F32),
                   jax.ShapeDtypeStruct((8, D_MODEL), _F32)),
        compiler_params=_cparams(("arbitrary",)),
    )(x, target, o_na, rest, h_f, h_b, rest, rest, gate, ml_norm_w, final_w, w_out_bf)


def _section_specs(sections, tm):
    specs, args = [], []
    for _, width, parts in sections:
        for arr, cb in parts:
            specs.append(pl.BlockSpec((tm, width), functools.partial(lambda i, cb: (i, cb), cb=cb)))
            args.append(arr)
    return specs, args


def _section_values(sections, refs, dtype):
    vals, at = [], 0
    for _, _, parts in sections:
        v = refs[at][...]
        for r in refs[at + 1:at + len(parts)]:
            v = v.astype(_F32) + r[...].astype(_F32)
        at += len(parts)
        vals.append(v.astype(dtype))
    return vals


def _inproj_bwd_x(x, dx1, scale1p, norm_w, w_in_bf, sections, T):
    tm = 512
    sspecs, sargs = _section_specs(sections, tm)
    ns = len(sargs)

    def body(*refs):
        x_ref, dx1_ref, sc_ref, nw_ref, w_ref = refs[:5]
        srefs = refs[5:5 + ns]
        gx_ref, vec_ref = refs[5 + ns:]
        i = pl.program_id(0)
        vals = _section_values(sections, srefs, _BF16)
        dh = jnp.zeros((tm, D_MODEL), _F32)
        for (c0, width, _), val in zip(sections, vals):
            dh = dh + _nt(val, w_ref[:, c0:c0 + width])
        xv = x_ref[...]
        r = lax.rsqrt(jnp.mean(xv * xv, axis=-1, keepdims=True) + EPS)
        xhat = xv * r
        nw = nw_ref[...]
        dshift = jnp.sum(dh, axis=0, keepdims=True)
        dscale = jnp.sum(dh * xhat * nw, axis=0, keepdims=True)
        dhpre = dh * sc_ref[...]
        dnw = jnp.sum(dhpre * xhat, axis=0, keepdims=True)
        dxhat = dhpre * nw
        gx_ref[...] = dx1_ref[...] + r * (dxhat - xhat * jnp.mean(dxhat * xhat, axis=-1, keepdims=True))
        vec = jnp.concatenate([dshift, dscale, dnw, jnp.zeros((5, D_MODEL), _F32)], axis=0)

        @pl.when(i == 0)
        def _():
            vec_ref[...] = vec

        @pl.when(i > 0)
        def _():
            vec_ref[...] += vec

    row = pl.BlockSpec((1, D_MODEL), lambda i: (0, 0))
    tok = pl.BlockSpec((tm, D_MODEL), lambda i: (i, 0))
    tok3 = pl.BlockSpec((None, tm, D_MODEL), lambda i: (0, i, 0))
    return pl.pallas_call(
        body, name="inproj_bwd_x", grid=(T // tm,),
        in_specs=[tok3, tok, row, row,
                  pl.BlockSpec((D_MODEL, IN_PAD), lambda i: (0, 0), pipeline_mode=pl.Buffered(1))] + sspecs,
        out_specs=(tok3, pl.BlockSpec((8, D_MODEL), lambda i: (0, 0))),
        out_shape=(jax.ShapeDtypeStruct((1, T, D_MODEL), _F32), jax.ShapeDtypeStruct((8, D_MODEL), _F32)),
        compiler_params=_cparams(("arbitrary",)),
    )(x, dx1, scale1p, norm_w, w_in_bf, *sargs)


def _inproj_bwd_w(h_t, sections, T):
    tm = 1024
    n_i = T // tm
    sspecs, sargs = _section_specs(sections, tm)
    ns = len(sargs)

    def body(*refs):
        h_ref = refs[0]
        srefs = refs[1:1 + ns]
        dw_ref, db_ref, acc, sem = refs[1 + ns:]
        i = pl.program_id(0)

        @pl.when(i == 0)
        def _():
            acc[...] = jnp.zeros_like(acc)
            db_ref[...] = jnp.zeros_like(db_ref)

        hv = h_ref[...]
        for (c0, width, _), v in zip(sections, _section_values(sections, srefs, _F32)):
            acc[:, c0:c0 + width] += _nn(hv, v.astype(_BF16))
            db_ref[0:1, c0:c0 + width] += jnp.sum(v, axis=0, keepdims=True)

        @pl.when(i == n_i - 1)
        def _():
            cp = pltpu.make_async_copy(acc, dw_ref, sem)
            cp.start()
            cp.wait()

    return pl.pallas_call(
        body, name="inproj_bwd_w", grid=(n_i,),
        in_specs=[pl.BlockSpec((D_MODEL, tm), lambda i: (0, i))] + sspecs,
        out_specs=(pl.BlockSpec(memory_space=pl.ANY), pl.BlockSpec((8, IN_PAD), lambda i: (0, 0))),
        out_shape=(jax.ShapeDtypeStruct((D_MODEL, IN_PAD), _F32), jax.ShapeDtypeStruct((8, IN_PAD), _F32)),
        scratch_shapes=[pltpu.VMEM((D_MODEL, IN_PAD), _F32), pltpu.SemaphoreType.DMA],
        compiler_params=_cparams(("arbitrary",)),
    )(h_t, *sargs)


def _adamw_math(w, g, m, v):
    m = ADAM_B1 * m + (1.0 - ADAM_B1) * g
    v = ADAM_B2 * v + (1.0 - ADAM_B2) * jnp.square(g)
    m_hat = m / (1.0 - ADAM_B1 ** ADAM_STEP)
    v_hat = v / (1.0 - ADAM_B2 ** ADAM_STEP)
    delta = -ADAM_LR * (m_hat / (jnp.sqrt(v_hat) + ADAM_EPS) + ADAM_WD * w)
    return delta, m, v


def _adamw_slots(w, m, v, slots, tr, name, own=None):
    R, C = w.shape
    extra = [] if own is None else [own]

    def body(w_ref, m_ref, v_ref, s_ref, *refs):
        g_ref, d_ref, nm_ref, nv_ref = refs[len(extra):]
        g = s_ref[0].astype(_F32)
        for k in range(1, N_DEV):
            g = g + s_ref[k].astype(_F32)
        if extra:
            g = g + refs[0][...].astype(_F32)
        g_ref[...] = g
        d_ref[...], nm_ref[...], nv_ref[...] = _adamw_math(w_ref[...], g, m_ref[...], v_ref[...])

    blk = pl.BlockSpec((tr, C), lambda i: (i, 0))
    return pl.pallas_call(
        body, name=name, grid=(R // tr,),
        in_specs=[blk, blk, blk, pl.BlockSpec((N_DEV, tr, C), lambda i: (0, i, 0))] + [blk] * len(extra),
        out_specs=(blk, blk, blk, blk),
        out_shape=tuple(jax.ShapeDtypeStruct((R, C), _F32) for _ in range(4)),
        compiler_params=_cparams(("arbitrary",)),
    )(w, m, v, slots, *extra)


def _w_ada_update(c_all, dmod_my, w, m, v):
    def body(c_ref, d_ref, w_ref, m_ref, v_ref, g_ref, dl_ref, nm_ref, nv_ref):
        g = lax.dot_general(_silu(c_ref[...]), d_ref[...], (((0,), (0,)), ((), ())),
                            precision=_HI, preferred_element_type=_F32)
        g_ref[...] = g
        dl_ref[...], nm_ref[...], nv_ref[...] = _adamw_math(w_ref[...], g, m_ref[...], v_ref[...])

    return pl.pallas_call(
        body, name="w_ada_update",
        out_shape=tuple(jax.ShapeDtypeStruct(w.shape, _F32) for _ in range(4)),
        compiler_params=_cparams(),
    )(c_all, dmod_my, w, m, v)


_PACK = (("b_ada", 3072, 3072), ("norm_w", 1024, 1024), ("b_in", IN_W, IN_PAD), ("conv_w", 5120, 5120),
         ("conv_b", 1024, 1024), ("rpb", 3720, 3840), ("ml_norm_w", 512, 512), ("final_norm_w", 1024, 1024),
         ("loss", 1, 128))
_PACK_OFF = {}
_off = 0
for _name, _len, _pad in _PACK:
    _PACK_OFF[_name] = (_off, _len)
    _off += _pad
_PACK_LEN = _off


def _pack(parts):
    cols = []
    for name, length, pad in _PACK:
        vec = parts[name].reshape(-1).astype(_F32)
        cols.append(jnp.pad(vec, (0, pad - length)))
    return jnp.concatenate(cols).reshape(1, _PACK_LEN)


def _unpack(vec, name, shape):
    off, length = _PACK_OFF[name]
    return vec.reshape(-1)[off:off + length].reshape(shape)


def kernel(x, c, w_ada, b_ada, norm_w, w_in, b_in, conv_w, conv_b, rpb, ml_norm_w, w_out, final_norm_w, loss_target, m_w_ada, m_b_ada, m_norm_w, m_w_in, m_b_in, m_conv_w, m_conv_b, m_rpb, m_ml_norm_w, m_w_out, m_final_norm_w, v_w_ada, v_b_ada, v_norm_w, v_w_in, v_b_in, v_conv_w, v_conv_b, v_rpb, v_ml_norm_w, v_w_out, v_final_norm_w):
    T = x.shape[1]
    rows = T // GRID_W
    me = 4 * lax.axis_index("x") + 2 * lax.axis_index("y") + lax.axis_index("c")
    n_in = w_in.shape[2]
    n_ada = w_ada.shape[2]
    n_cw = conv_w.shape[2]
    n_wo = w_out.shape[1]

    w_in_my, w_out_my = w_in[0].astype(_BF16), w_out[0].astype(_BF16)
    first_leg = (1,) + _CHIP_PEERS
    start_in = _scatter_start([w_in_my], "w_in_start", scatter=False, ks=first_leg)
    g_conv_w, g_c = _exchange([conv_w[0], c + start_in[-1][0:1, 0:1]], [False] * 2, "gather_small")
    b_in_pad = jnp.pad(b_in, ((0, 0), (0, IN_PAD - IN_W)))
    conv_w_full = jnp.pad(g_conv_w.transpose(1, 0, 2).reshape(CONV_W, N_DEV * n_cw), ((0, 3), (0, 0)))
    c_all = g_c.reshape(N_DEV, D_MODEL)

    b_ada_my = lax.dynamic_slice(b_ada, (0, me * n_ada), (1, n_ada))
    (mod_slots,) = _exchange([_mod_part(c_all, w_ada[0], b_ada_my)], [False], "gather_mod")
    mod = lax.dynamic_index_in_dim(mod_slots, me, axis=1, keepdims=False).reshape(1, 3 * D_MODEL)
    shift, scale, gate = mod[:, :D_MODEL], mod[:, D_MODEL:2 * D_MODEL], mod[:, 2 * D_MODEL:]
    scale1p = 1.0 + scale
    bias = _na_bias_tables(rpb[0], rows)

    def own_slot(land, own):
        return lax.dynamic_update_slice(land, own[None], (me,) + (0,) * own.ndim)

    def gathered(started, after, name):
        (own,), (land,) = _scatter_wait(started, after, name, scatter=False)
        return own_slot(land, own)

    (w_in_own,), (w_in_land,) = _scatter_wait(start_in, bias[0, 0, :8, :128] + scale1p[:, :128], "w_in_wait",
                                              scatter=False, ks=first_leg)
    g_w_in = own_slot(_relay_wait(_relay_start(w_in_land, "w_in_relay_start"), "w_in_relay_wait"), w_in_own)
    w_in_full = g_w_in.transpose(1, 0, 2).reshape(D_MODEL, N_DEV * n_in)
    w_in_bf = jnp.pad(w_in_full, ((0, 0), (0, IN_PAD - IN_W)))
    arrived = (g_w_in[0, 0:1, 0:1] != g_w_in[0, 0:1, 0:1]).astype(_BF16)
    start_out = _scatter_start([w_out_my + arrived], "w_out_start", scatter=False)

    qkv, rest, h_bf = _inproj_fwd(x, scale1p, shift, norm_w, w_in_bf, b_in_pad)
    o_na, lse = _na_fwd(qkv, bias, T)
    qk_act = _conv_fwd(rest, conv_w_full, conv_b, T)
    h_f, *saved_f = _mlstm_fwd(qk_act, rest, T, False)
    h_b, *saved_b = _mlstm_fwd(qk_act, rest, T, True)

    w_out_bf = gathered(start_out, saved_b[2], "w_out_wait").reshape(N_DEV * n_wo, D_MODEL)
    dx1, d_o, dz_na, d_h, d_mo, d_mz, dwo, pvec = _post(
        x, loss_target, o_na, rest, h_f, h_b, gate, ml_norm_w, final_norm_w.reshape(1, D_MODEL), w_out_bf, T)

    dq_na, dk_na, dv_na, dbias = _na_bwd(qkv, bias, o_na, d_o, lse, T)
    d_rpb = _rpb_grad(dbias, rows)
    dqk_f, dv_f, dg_f = _mlstm_bwd(qk_act, rest, d_h, saved_f, T, False)
    dqk_b, dv_b, dg_b = _mlstm_bwd(qk_act, rest, d_h, saved_b, T, True)
    d_u, dconv = _conv_bwd(rest, conv_w_full, conv_b, dqk_f, dqk_b, T)

    sections = [(0, 512, [(dq_na, 0)]), (512, 512, [(dk_na, 0)]), (1024, 512, [(dv_na, 0)]),
                (1536, 512, [(dz_na, 0)]), (2048, 512, [(d_u, 0)]), (2560, 512, [(d_u, 1)]),
                (3072, 512, [(dv_f, 0), (dv_b, 0)]), (3584, 512, [(d_mo, 0)]), (4096, 512, [(d_mz, 0)]),
                (4608, 128, [(dg_f, 0), (dg_b, 0)])]
    dw_pad, db_pad = _inproj_bwd_w(h_bf, sections, T)
    db_in = db_pad[0, :IN_W]

    dw_blocks = dw_pad[:, :IN_W].astype(_BF16).reshape(D_MODEL, N_DEV, n_in).transpose(1, 0, 2)
    dwo_blocks = dwo.astype(_BF16).reshape(N_DEV, n_wo, D_MODEL)
    started = _scatter_start([dw_blocks, dwo_blocks], "grads_start")
    grad_x, xvec = _inproj_bwd_x(x, dx1, scale1p + started[-1][0:1, 0:1], norm_w, w_in_bf, sections, T)
    (dw_blocks, dwo_blocks), (s_w_in, s_w_out) = _scatter_wait(started, xvec, "grads_wait")

    small = _pack({
        "b_ada": jnp.concatenate([xvec[0], xvec[1], pvec[1]]),
        "norm_w": xvec[2], "b_in": db_in, "conv_w": dconv[:CONV_W], "conv_b": dconv[CONV_W],
        "rpb": d_rpb, "ml_norm_w": pvec[2, :ML_W], "final_norm_w": pvec[0], "loss": pvec[3, :1]})
    (s_small,) = _exchange([small], [False], "exchange_small")

    own = lambda blocks: lax.dynamic_index_in_dim(blocks, me, axis=0, keepdims=False)
    g_w_in_s, d_w_in, nm_w_in, nv_w_in = _adamw_slots(
        w_in[0], m_w_in[0], v_w_in[0], s_w_in, 128, "adamw_w_in", own=own(dw_blocks))
    g_w_out_s, d_w_out, nm_w_out, nv_w_out = _adamw_slots(
        w_out[0], m_w_out[0], v_w_out[0], s_w_out, n_wo, "adamw_w_out", own=own(dwo_blocks))
    dmod_all = s_small[:, 0, :3 * D_MODEL]
    dmod_my = lax.dynamic_slice(dmod_all, (0, me * n_ada), (N_DEV, n_ada))
    g_w_ada, d_w_ada, nm_w_ada, nv_w_ada = _w_ada_update(c_all, dmod_my, w_ada[0], m_w_ada[0], v_w_ada[0])

    def embed(shard):
        return lax.dynamic_update_slice(jnp.zeros((CONV_W, N_DEV * n_cw), _F32), shard[0], (0, me * n_cw))

    zero1 = jnp.zeros((1,), _F32)
    packed = lambda b_a, n_w, b_i, c_w, c_b, rp, mn, fn: _pack({
        "b_ada": b_a, "norm_w": n_w, "b_in": b_i, "conv_w": embed(c_w), "conv_b": c_b, "rpb": rp,
        "ml_norm_w": mn, "final_norm_w": fn, "loss": zero1})
    pw = packed(b_ada, norm_w, b_in, conv_w, conv_b, rpb, ml_norm_w, final_norm_w)
    pm = packed(m_b_ada, m_norm_w, m_b_in, m_conv_w, m_conv_b, m_rpb, m_ml_norm_w, m_final_norm_w)
    pv = packed(v_b_ada, v_norm_w, v_b_in, v_conv_w, v_conv_b, v_rpb, v_ml_norm_w, v_final_norm_w)
    sg, sd, sm, sv = _adamw_slots(pw, pm, pv, s_small, 1, "adamw_small")

    def small_outs(vec):
        cw = lax.dynamic_slice(_unpack(vec, "conv_w", (CONV_W, N_DEV * n_cw)), (0, me * n_cw), (CONV_W, n_cw))
        return dict(b_ada=_unpack(vec, "b_ada", b_ada.shape), norm_w=_unpack(vec, "norm_w", norm_w.shape),
                    b_in=_unpack(vec, "b_in", b_in.shape), conv_w=cw[None],
                    conv_b=_unpack(vec, "conv_b", conv_b.shape), rpb=_unpack(vec, "rpb", rpb.shape),
                    ml_norm_w=_unpack(vec, "ml_norm_w", ml_norm_w.shape),
                    final_norm_w=_unpack(vec, "final_norm_w", final_norm_w.shape))

    loss = _unpack(sg, "loss", ())
    order = ("w_ada", "b_ada", "norm_w", "w_in", "b_in", "conv_w", "conv_b", "rpb", "ml_norm_w", "w_out",
             "final_norm_w")
    outs = []
    for vec, big in ((sg, (g_w_ada, g_w_in_s, g_w_out_s)), (sd, (d_w_ada, d_w_in, d_w_out)),
                     (sm, (nm_w_ada, nm_w_in, nm_w_out)), (sv, (nv_w_ada, nv_w_in, nv_w_out))):
        group = small_outs(vec)
        group.update(w_ada=big[0][None], w_in=big[1][None], w_out=big[2][None])
        outs.extend(group[name] for name in order)
    return (loss, grad_x, *outs)
```

```python
import functools

import numpy as np
import jax
import jax.numpy as jnp
from jax import lax
from jax.experimental import pallas as pl
from jax.experimental.pallas import tpu as pltpu

N_DEV = 8
D_MODEL = 1024
GRID_W = 64
NA_HEADS = 8
NA_HEAD_DIM = 64
NA_KH = 8
NA_KW = 16
NA_W = 512
ML_HEADS = 4
ML_HEAD_DIM = 128
ML_W = 512
ML_CHUNK = 512
CONV_W = 5
EPS = 1e-6
IN_W = 4624
IN_PAD = 4736
REST_W = IN_PAD - 3 * NA_W
GATE_COL = 3072
NEG = -1e30
NA_RB = 4
NA_WIN = 12
NA_SUB = 8
ML_CB = 1
ADAM_LR = 0.001
ADAM_B1 = 0.9
ADAM_B2 = 0.999
ADAM_EPS = 1e-08
ADAM_WD = 0.01
ADAM_STEP = 10
VMEM_LIMIT = 56 * 1024 * 1024

_F32 = jnp.float32
_BF16 = jnp.bfloat16
_HI = lax.Precision.HIGHEST


def _cparams(sem=None):
    return pltpu.CompilerParams(dimension_semantics=sem, vmem_limit_bytes=VMEM_LIMIT)


def _nt(a, b):
    return lax.dot_general(a, b, (((1,), (1,)), ((), ())), preferred_element_type=_F32)


def _tn(a, b):
    return lax.dot_general(a, b, (((0,), (0,)), ((), ())), preferred_element_type=_F32)


def _nn(a, b):
    return jnp.dot(a, b, preferred_element_type=_F32)


def _sigmoid(x):
    return 1.0 / (1.0 + jnp.exp(-x))


def _silu(x):
    return x * _sigmoid(x)


def _dsilu(x):
    s = _sigmoid(x)
    return s * (1.0 + x * (1.0 - s))


def _exchange(arrs, scatter, name):
    n = len(arrs)
    out_shape = []
    for a, sc in zip(arrs, scatter):
        blk = a.shape[1:] if sc else a.shape
        out_shape.append(jax.ShapeDtypeStruct((N_DEV,) + tuple(blk), a.dtype))

    def body(*refs):
        ins = refs[:n]
        outs = refs[n:2 * n]
        send_sems, recv_sems, local_sems = refs[2 * n:]
        x, y, c = lax.axis_index("x"), lax.axis_index("y"), lax.axis_index("c")
        me = 4 * x + 2 * y + c
        local, sends, recvs = [], [], []
        for a in range(n):
            own = ins[a].at[me] if scatter[a] else ins[a]
            cp = pltpu.make_async_copy(own, outs[a].at[me], local_sems.at[a])
            cp.start()
            local.append(cp)
            for k in range(1, N_DEV):
                px = 1 - x if k & 4 else x
                py = 1 - y if k & 2 else y
                pc = 1 - c if k & 1 else c
                p = 4 * px + 2 * py + pc
                src = ins[a].at[p] if scatter[a] else ins[a]
                snd = pltpu.make_async_remote_copy(
                    src_ref=src, dst_ref=outs[a].at[me],
                    send_sem=send_sems.at[a, k - 1], recv_sem=recv_sems.at[a, k - 1],
                    device_id=(px, py, pc), device_id_type=pl.DeviceIdType.MESH)
                snd.start()
                sends.append(snd)
                rcv = pltpu.make_async_remote_copy(
                    src_ref=src, dst_ref=outs[a].at[p],
                    send_sem=send_sems.at[a, k - 1], recv_sem=recv_sems.at[a, k - 1],
                    device_id=(px, py, pc), device_id_type=pl.DeviceIdType.MESH)
                recvs.append(rcv)
        for rcv in recvs:
            rcv.wait_recv()
        for snd in sends:
            snd.wait_send()
        for cp in local:
            cp.wait()

    any_spec = pl.BlockSpec(memory_space=pl.ANY)
    res = pl.pallas_call(
        body, name=name, out_shape=tuple(out_shape),
        in_specs=[any_spec] * n, out_specs=tuple([any_spec] * n),
        scratch_shapes=[pltpu.SemaphoreType.DMA((n, N_DEV - 1)),
                        pltpu.SemaphoreType.DMA((n, N_DEV - 1)),
                        pltpu.SemaphoreType.DMA((n,))],
    )(*arrs)
    return list(res)


def _peer(k):
    x, y, c = lax.axis_index("x"), lax.axis_index("y"), lax.axis_index("c")
    px = 1 - x if k & 4 else x
    py = 1 - y if k & 2 else y
    pc = 1 - c if k & 1 else c
    return (px, py, pc), 4 * px + 2 * py + pc, 4 * x + 2 * y + c


_ALL_PEERS = tuple(range(1, N_DEV))
_CHIP_PEERS = (2, 4, 6)


def _scatter_copy(srcs, lands, send_sems, recv_sems, a, k, receive, scatter, ks=_ALL_PEERS):
    dev, p, me = _peer(k)
    at = a * len(ks) + ks.index(k)
    return pltpu.make_async_remote_copy(
        src_ref=srcs[a].at[p] if scatter else srcs[a], dst_ref=lands[a].at[p if receive else me],
        send_sem=send_sems[at], recv_sem=recv_sems[at],
        device_id=dev, device_id_type=pl.DeviceIdType.MESH)


def _scatter_start(arrs, name, scatter=True, ks=_ALL_PEERS):
    n = len(arrs)
    ns = n * len(ks)
    hbm = pl.BlockSpec(memory_space=pltpu.HBM)
    sem = pl.BlockSpec(memory_space=pltpu.SEMAPHORE)

    def body(*refs):
        srcs, lands = refs[:n], refs[n:2 * n]
        send_sems, recv_sems = refs[2 * n:2 * n + ns], refs[2 * n + ns:2 * n + 2 * ns]
        token = refs[-1]
        for a in range(n):
            for k in ks:
                _scatter_copy(srcs, lands, send_sems, recv_sems, a, k, False, scatter, ks).start()
        token[...] = jnp.zeros_like(token)

    land_shapes = [a.shape if scatter else (N_DEV,) + a.shape for a in arrs]
    buffers = [pltpu.HBM(a.shape, a.dtype) for a in arrs]
    land_buffers = [pltpu.HBM(s, a.dtype) for s, a in zip(land_shapes, arrs)]
    sems = [pltpu.SemaphoreType.DMA(()) for _ in range(2 * ns)]
    res = pl.pallas_call(
        body, name=name,
        out_shape=(*sems, *buffers, *land_buffers, jax.ShapeDtypeStruct((8, 128), _F32)),
        in_specs=[hbm] * (2 * n),
        out_specs=(*([sem] * (2 * ns)), *([hbm] * (2 * n)), pl.BlockSpec(memory_space=pltpu.VMEM)),
        input_output_aliases={i: 2 * ns + i for i in range(2 * n)},
        compiler_params=pltpu.CompilerParams(has_side_effects=pltpu.SideEffectType.DATAFLOW_SIDE_EFFECTING),
    )(*[pltpu.with_memory_space_constraint(a, pltpu.HBM) for a in arrs],
      *[pltpu.with_memory_space_constraint(jnp.zeros(s, a.dtype), pltpu.HBM) for s, a in zip(land_shapes, arrs)])
    res = list(res)
    return (res[:ns], res[ns:2 * ns], res[2 * ns:2 * ns + n], res[2 * ns + n:2 * ns + 2 * n], res[-1])


def _scatter_wait(started, after, name, scatter=True, ks=_ALL_PEERS):
    send_sems, recv_sems, srcs, lands, _ = started
    n = len(srcs)
    ns = len(send_sems)
    hbm = pl.BlockSpec(memory_space=pltpu.HBM)
    sem = pl.BlockSpec(memory_space=pltpu.SEMAPHORE)

    def body(*refs):
        src_refs, land_refs = refs[:n], refs[n:2 * n]
        s_sems, r_sems = refs[2 * n:2 * n + ns], refs[2 * n + ns:2 * n + 2 * ns]
        for a in range(n):
            for k in ks:
                _scatter_copy(src_refs, land_refs, s_sems, r_sems, a, k, False, scatter, ks).wait_send()
                _scatter_copy(src_refs, land_refs, s_sems, r_sems, a, k, True, scatter, ks).wait_recv()

    buffers = [pltpu.HBM(a.shape, a.dtype) for a in list(srcs) + list(lands)]
    res = pl.pallas_call(
        body, name=name, out_shape=tuple(buffers),
        in_specs=[hbm] * (2 * n) + [sem] * (2 * ns) + [pl.BlockSpec(memory_space=pl.ANY)],
        out_specs=tuple([hbm] * (2 * n)),
        input_output_aliases={i: i for i in range(2 * n)},
        compiler_params=pltpu.CompilerParams(has_side_effects=pltpu.SideEffectType.DATAFLOW_SIDE_EFFECTING),
    )(*srcs, *lands, *send_sems, *recv_sems, after)
    return list(res[:n]), list(res[n:])


def _relay_copy(land, send_sems, recv_sems, j, receive):
    k = _CHIP_PEERS[j]
    sibling, _, _ = _peer(1)
    _, slot, _ = _peer(k | 1 if receive else k)
    return pltpu.make_async_remote_copy(
        src_ref=land.at[slot], dst_ref=land.at[slot], send_sem=send_sems[j], recv_sem=recv_sems[j],
        device_id=sibling, device_id_type=pl.DeviceIdType.MESH)


def _relay_start(land, name):
    ns = len(_CHIP_PEERS)
    hbm = pl.BlockSpec(memory_space=pltpu.HBM)
    sem = pl.BlockSpec(memory_space=pltpu.SEMAPHORE)

    def body(*refs):
        land_ref = refs[0]
        send_sems, recv_sems = refs[1:1 + ns], refs[1 + ns:1 + 2 * ns]
        for j in range(ns):
            _relay_copy(land_ref, send_sems, recv_sems, j, False).start()
        refs[-1][...] = jnp.zeros_like(refs[-1])

    sems = [pltpu.SemaphoreType.DMA(()) for _ in range(2 * ns)]
    res = pl.pallas_call(
        body, name=name,
        out_shape=(*sems, pltpu.HBM(land.shape, land.dtype), jax.ShapeDtypeStruct((8, 128), _F32)),
        in_specs=[hbm],
        out_specs=(*([sem] * (2 * ns)), hbm, pl.BlockSpec(memory_space=pltpu.VMEM)),
        input_output_aliases={0: 2 * ns},
        compiler_params=pltpu.CompilerParams(has_side_effects=pltpu.SideEffectType.DATAFLOW_SIDE_EFFECTING),
    )(land)
    res = list(res)
    return res[:ns], res[ns:2 * ns], res[2 * ns], res[-1]


def _relay_wait(started, name):
    send_sems, recv_sems, land, token = started
    ns = len(send_sems)
    hbm = pl.BlockSpec(memory_space=pltpu.HBM)
    sem = pl.BlockSpec(memory_space=pltpu.SEMAPHORE)

    def body(*refs):
        land_ref = refs[0]
        s_sems, r_sems = refs[1:1 + ns], refs[1 + ns:1 + 2 * ns]
        for j in range(ns):
            _relay_copy(land_ref, s_sems, r_sems, j, False).wait_send()
            _relay_copy(land_ref, s_sems, r_sems, j, True).wait_recv()

    return pl.pallas_call(
        body, name=name, out_shape=pltpu.HBM(land.shape, land.dtype),
        in_specs=[hbm] + [sem] * (2 * ns) + [pl.BlockSpec(memory_space=pl.ANY)],
        out_specs=hbm, input_output_aliases={0: 0},
        compiler_params=pltpu.CompilerParams(has_side_effects=pltpu.SideEffectType.DATAFLOW_SIDE_EFFECTING),
    )(land, *send_sems, *recv_sems, token)


def _mod_part(c_all, w_ada, b_my):
    def body(c_ref, w_ref, b_ref, o_ref):
        o_ref[...] = jnp.dot(_silu(c_ref[...]), w_ref[...], precision=_HI,
                             preferred_element_type=_F32) + b_ref[...]

    return pl.pallas_call(
        body, name="mod_part",
        out_shape=jax.ShapeDtypeStruct((N_DEV, w_ada.shape[1]), _F32),
        compiler_params=_cparams(),
    )(c_all, w_ada, b_my)


def _prenorm(x, scale1p, shift, norm_w):
    T = x.shape[1]
    tm = 512

    def body(x_ref, sc_ref, sh_ref, nw_ref, h_ref, ht_ref):
        xv = x_ref[...]
        r = lax.rsqrt(jnp.mean(xv * xv, axis=-1, keepdims=True) + EPS)
        h = xv * r * nw_ref[...] * sc_ref[...] + sh_ref[...]
        h_ref[...] = h.astype(_BF16)
        ht_ref[...] = h.T.astype(_BF16)

    row = pl.BlockSpec((1, D_MODEL), lambda i: (0, 0))
    return pl.pallas_call(
        body, name="prenorm", grid=(T // tm,),
        in_specs=[pl.BlockSpec((None, tm, D_MODEL), lambda i: (0, i, 0)), row, row, row],
        out_specs=(pl.BlockSpec((tm, D_MODEL), lambda i: (i, 0)), pl.BlockSpec((D_MODEL, tm), lambda i: (0, i))),
        out_shape=(jax.ShapeDtypeStruct((T, D_MODEL), _BF16), jax.ShapeDtypeStruct((D_MODEL, T), _BF16)),
        compiler_params=_cparams(("arbitrary",)),
    )(x, scale1p, shift, norm_w)


def _inproj_fwd(h, w_in_bf, b_in_pad):
    T = h.shape[0]
    tm = 512
    n_q = 3 * NA_W

    def body(h_ref, w_ref, b_ref, qkv_ref, rest_ref):
        hb = h_ref[...]
        for n0 in range(0, IN_PAD, 512):
            wd = min(512, IN_PAD - n0)
            acc = _nn(hb, w_ref[:, n0:n0 + wd]) + b_ref[:, n0:n0 + wd]
            if n0 == 0:
                acc = acc * (NA_HEAD_DIM ** -0.5)
            if n0 < n_q:
                qkv_ref[:, n0:n0 + wd] = acc.astype(_BF16)
            else:
                rest_ref[:, n0 - n_q:n0 - n_q + wd] = acc

    row = lambda w: pl.BlockSpec((1, w), lambda i: (0, 0))
    return pl.pallas_call(
        body, name="inproj_fwd", grid=(T // tm,),
        in_specs=[pl.BlockSpec((tm, D_MODEL), lambda i: (i, 0)),
                  pl.BlockSpec((D_MODEL, IN_PAD), lambda i: (0, 0), pipeline_mode=pl.Buffered(1)), row(IN_PAD)],
        out_specs=(pl.BlockSpec((tm, n_q), lambda i: (i, 0)),
                   pl.BlockSpec((tm, REST_W), lambda i: (i, 0))),
        out_shape=(jax.ShapeDtypeStruct((T, n_q), _BF16),
                   jax.ShapeDtypeStruct((T, REST_W), _F32)),
        compiler_params=_cparams(("arbitrary",)),
    )(h, w_in_bf, b_in_pad)


def _na_class_rows(rows):
    nb = rows // NA_RB
    out = []
    for rb in (0, min(1, nb - 1), nb - 1):
        ws = int(np.clip(NA_RB * rb - 4, 0, rows - NA_WIN))
        out.append((NA_RB * rb + np.arange(NA_RB), ws + np.arange(NA_WIN)))
    return out


def _na_pair_index(rows, qrows, krows):
    start = lambda r: np.clip(r - NA_KH // 2, 0, rows - NA_KH)
    col = np.arange(GRID_W)
    cstart = np.clip(col - NA_KW // 2, 0, GRID_W - NA_KW)
    dy = krows[None, :] - qrows[:, None] + NA_KH - 1
    vr = (krows[None, :] >= start(qrows)[:, None]) & (krows[None, :] < start(qrows)[:, None] + NA_KH)
    dx = np.clip(col[None, :] - col[:, None], -(NA_KW - 1), NA_KW - 1) + NA_KW - 1
    vc = (col[None, :] >= cstart[:, None]) & (col[None, :] < cstart[:, None] + NA_KW)
    nq, nk = len(qrows), len(krows)
    dy4 = np.broadcast_to(np.clip(dy, 0, 2 * NA_KH - 2)[:, None, :, None], (nq, GRID_W, nk, GRID_W))
    dx4 = np.broadcast_to(dx[None, :, None, :], (nq, GRID_W, nk, GRID_W))
    valid = vr[:, None, :, None] & vc[None, :, None, :]
    idx = (dy4 * (2 * NA_KW - 1) + dx4).reshape(nq * GRID_W, nk * GRID_W)
    return idx.astype(np.int32), valid.reshape(nq * GRID_W, nk * GRID_W), (dy, vr, dx, vc)


def _na_half_slabs(rpb):
    _, _, (_, _, dx, vc) = _na_pair_index(NA_WIN, np.arange(1), np.arange(1))
    qc, kc = np.meshgrid(np.arange(GRID_W), np.arange(GRID_W), indexing="ij")
    consts = []
    for right in (False, True):
        pos = (qc * 128 + (GRID_W if right else 0) + kc).reshape(-1)
        oh = np.zeros((32, GRID_W * 128), np.float32)
        oh[dx[qc, kc].reshape(-1), pos] = 1.0
        col_neg = np.zeros((1, GRID_W * 128), np.float32)
        col_neg[0, pos] = np.where(vc[qc, kc].reshape(-1), 0.0, NEG)
        half = np.zeros((1, GRID_W * 128), np.float32)
        half[0, pos] = 1.0
        consts += [jnp.asarray(oh), jnp.asarray(col_neg), jnp.asarray(half)]
    row_neg = np.where(np.arange(NA_HEADS * 16) % 16 == 15, NEG, 0.0).astype(np.float32).reshape(-1, 1)
    rp = jnp.pad(rpb, ((0, 0), (0, 1), (0, 1))).reshape(NA_HEADS * 16, 32)

    def body(*refs):
        r_ref, rn_ref = refs[0], refs[1]
        for t in range(2):
            oh_ref, cn_ref, half_ref = refs[2 + 3 * t:5 + 3 * t]
            refs[8 + t][...] = (jnp.dot(r_ref[...], oh_ref[...], precision=_HI, preferred_element_type=_F32)
                                + cn_ref[...] + rn_ref[...] * half_ref[...])

    outs = pl.pallas_call(
        body, name="na_half_slabs",
        out_shape=tuple(jax.ShapeDtypeStruct((NA_HEADS * 16, GRID_W * 128), _F32) for _ in range(2)),
        compiler_params=_cparams(),
    )(rp, jnp.asarray(row_neg), *consts)
    return [o.reshape(NA_HEADS, 16, GRID_W, 128) for o in outs]


def _na_bias_tables(rpb, rows):
    left, right = _na_half_slabs(rpb)
    didx = []
    for blk, win in _na_class_rows(rows):
        _, _, (dy, vr, _, _) = _na_pair_index(rows, blk, win)
        didx.append(np.where(vr, dy, 15))

    def body(l_ref, r_ref, b_ref):
        for ci, tab in enumerate(didx):
            for a in range(NA_RB):
                for j in range(NA_WIN // 2):
                    b_ref[ci, 0, a * GRID_W:(a + 1) * GRID_W, j * 128:(j + 1) * 128] = (
                        l_ref[0, int(tab[a, 2 * j])] + r_ref[0, int(tab[a, 2 * j + 1])])

    slab = pl.BlockSpec((1, 16, GRID_W, 128), lambda h: (h, 0, 0, 0))
    return pl.pallas_call(
        body, name="na_tables", grid=(NA_HEADS,),
        in_specs=[slab] * 2,
        out_specs=pl.BlockSpec((3, 1, NA_RB * GRID_W, NA_WIN * GRID_W), lambda h: (0, h, 0, 0)),
        out_shape=jax.ShapeDtypeStruct((3, NA_HEADS, NA_RB * GRID_W, NA_WIN * GRID_W), _F32),
        compiler_params=_cparams(("arbitrary",)),
    )(left, right)


def _stack_heads(x, first):
    zero = jnp.zeros_like(x)
    return jnp.concatenate([jnp.where(first, x, zero), jnp.where(first, zero, x)], axis=0)


def _na_sub(rb, u, rows):
    sb = NA_SUB * rb + u
    nb = rows // NA_RB
    cls = jnp.where(sb == 0, 0, jnp.where(sb == nb - 1, 2, 1))
    ws = pl.multiple_of(jnp.clip(NA_RB * sb - 4, 0, rows - NA_WIN) * GRID_W, 256)
    return cls, ws


def _na_fwd(qkv, bias, T):
    rows = T // GRID_W
    tq = NA_RB * GRID_W
    tw = NA_WIN * GRID_W
    ts = NA_SUB * tq

    def body(q_ref, k_ref, v_ref, b_ref, o_ref, l_ref):
        rb = pl.program_id(1)
        lane = lax.broadcasted_iota(jnp.int32, (1, 128), 1)
        first = lane < NA_HEAD_DIM
        for u in range(NA_SUB):
            cls, ws = _na_sub(rb, u, rows)
            kw = k_ref[pl.ds(ws, tw), :]
            vw = v_ref[pl.ds(ws, tw), :]
            q2 = _stack_heads(q_ref[u * tq:(u + 1) * tq, :], first)
            s = _nt(q2, kw) + b_ref[cls].reshape(2 * tq, tw)
            m = jnp.max(s, axis=1, keepdims=True)
            p = jnp.exp(s - m)
            l = jnp.sum(p, axis=1, keepdims=True)
            o2 = _nn(p.astype(_BF16), vw) / l
            lse2 = m + jnp.log(l)
            o_ref[u * tq:(u + 1) * tq, :] = jnp.where(first, o2[:tq], o2[tq:])
            l_ref[u * tq:(u + 1) * tq, :] = jnp.where(first, lse2[:tq], lse2[tq:])

    blk = lambda off: pl.BlockSpec((ts, 128), lambda hp, rb: (rb, off + hp))
    whole = lambda off: pl.BlockSpec((T, 128), lambda hp, rb: (0, off + hp))
    return pl.pallas_call(
        body, name="na_fwd", grid=(NA_HEADS // 2, T // ts),
        in_specs=[blk(0), whole(4), whole(8),
                  pl.BlockSpec((3, 2, tq, tw), lambda hp, rb: (0, hp, 0, 0))],
        out_specs=(blk(0), blk(0)),
        out_shape=(jax.ShapeDtypeStruct((T, NA_W), _F32), jax.ShapeDtypeStruct((T, NA_W), _F32)),
        compiler_params=_cparams(("arbitrary", "arbitrary")),
    )(qkv, qkv, qkv, bias)


def _na_bwd(qkv, bias, o, d_o, lse, T):
    rows = T // GRID_W
    tq = NA_RB * GRID_W
    tw = NA_WIN * GRID_W
    ts = NA_SUB * tq

    def body(q_ref, k_ref, v_ref, b_ref, o_ref, do_ref, l_ref, dq_ref, dk_ref, dv_ref, db_ref):
        rb = pl.program_id(1)
        lane = lax.broadcasted_iota(jnp.int32, (1, 128), 1)
        first = lane < NA_HEAD_DIM

        @pl.when(rb == 0)
        def _():
            db_ref[...] = jnp.zeros_like(db_ref)
            dk_ref[...] = jnp.zeros_like(dk_ref)
            dv_ref[...] = jnp.zeros_like(dv_ref)

        for u in range(NA_SUB):
            cls, ws = _na_sub(rb, u, rows)
            kw = k_ref[pl.ds(ws, tw), :]
            vw = v_ref[pl.ds(ws, tw), :]
            sl = slice(u * tq, (u + 1) * tq)
            q = q_ref[sl, :]
            d_ov = do_ref[sl, :]
            prod = d_ov.astype(_F32) * o_ref[sl, :]
            lse_v = l_ref[sl, :]
            dqs = []
            dk_win = jnp.zeros((tw, 128), _F32)
            dv_win = jnp.zeros((tw, 128), _F32)
            for hh in range(2):
                msk = first if hh == 0 else jnp.logical_not(first)
                c0 = hh * NA_HEAD_DIM
                qm = jnp.where(msk, q, jnp.zeros_like(q))
                dom = jnp.where(msk, d_ov, jnp.zeros_like(d_ov))
                s = _nt(qm, kw) + b_ref[cls, hh]
                p = jnp.exp(s - lse_v[:, c0:c0 + 1])
                dp = _nt(dom, vw)
                delta = jnp.sum(jnp.where(msk, prod, 0.0), axis=1, keepdims=True)
                ds = p * (dp - delta)
                db_ref[cls, hh] += ds
                dsb = ds.astype(_BF16)
                dqs.append(_nn(dsb, kw) * (NA_HEAD_DIM ** -0.5))
                dk_win = dk_win + _tn(dsb, qm)
                dv_win = dv_win + _tn(p.astype(_BF16), dom)
            dq_ref[sl, :] = jnp.where(first, dqs[0], dqs[1]).astype(_BF16)
            dk_ref[pl.ds(ws, tw), :] += dk_win
            dv_ref[pl.ds(ws, tw), :] += dv_win

    once = pl.Buffered(1)
    blk = lambda off: pl.BlockSpec((ts, 128), lambda hp, rb: (rb, off + hp))
    whole = lambda off, **kw: pl.BlockSpec((T, 128), lambda hp, rb: (0, off + hp), **kw)
    tab = lambda **kw: pl.BlockSpec((3, 2, tq, tw), lambda hp, rb: (0, hp, 0, 0), **kw)
    return pl.pallas_call(
        body, name="na_bwd", grid=(NA_HEADS // 2, T // ts),
        in_specs=[blk(0), whole(4), whole(8), tab(), blk(0), blk(0), blk(0)],
        out_specs=(blk(0), whole(0, pipeline_mode=once), whole(0, pipeline_mode=once), tab(pipeline_mode=once)),
        out_shape=(jax.ShapeDtypeStruct((T, NA_W), _BF16), jax.ShapeDtypeStruct((T, NA_W), _F32),
                   jax.ShapeDtypeStruct((T, NA_W), _F32), jax.ShapeDtypeStruct(bias.shape, _F32)),
        compiler_params=_cparams(("arbitrary", "arbitrary")),
    )(qkv, qkv, qkv, bias, o, d_o, lse)


def _rpb_grad(dbias, rows):
    tw = NA_WIN * GRID_W
    lanes = 16 * GRID_W
    offs = [int(win[0] - blk[0] + NA_KH - 1) for blk, win in _na_class_rows(rows)]

    def body(x_ref, g_ref):
        sub = lax.broadcasted_iota(jnp.int32, (NA_RB, 1), 0)
        qc = lax.broadcasted_iota(jnp.int32, (NA_RB * GRID_W, 1), 0) % GRID_W
        tot = jnp.zeros((NA_RB, lanes), _F32)
        for ci in range(3):
            xv = x_ref[ci, 0]
            for bit in range(6):
                xv = jnp.where(((qc >> bit) & 1) == 1, pltpu.roll(xv, tw - (1 << bit), 1), xv)
            acc = pltpu.roll(jnp.sum(xv.reshape(NA_RB, GRID_W, tw), axis=1), NA_KW, 1)
            acc = jnp.concatenate([acc, jnp.zeros((NA_RB, lanes - tw), _F32)], axis=1)
            for a in range(NA_RB):
                tot = tot + jnp.where(sub == a, pltpu.roll(acc, (GRID_W * (offs[ci] - a)) % lanes, 1), 0.0)
        g_ref[0] = jnp.broadcast_to(jnp.sum(tot, axis=0, keepdims=True), (8, lanes))

    g = pl.pallas_call(
        body, name="rpb_grad", grid=(NA_HEADS,),
        in_specs=[pl.BlockSpec((3, 1) + dbias.shape[2:], lambda h: (0, h, 0, 0))],
        out_specs=pl.BlockSpec((1, 8, lanes), lambda h: (h, 0, 0)),
        out_shape=jax.ShapeDtypeStruct((NA_HEADS, 8, lanes), _F32),
        compiler_params=_cparams(("arbitrary",)),
    )(dbias)
    return g[:, 0].reshape(NA_HEADS, 16, GRID_W)[:, :2 * NA_KH - 1, 1:2 * NA_KW]


def _halo_specs(tm, width, col_of, T, order):
    hb = tm // 8
    last = T // 8 - 1
    if order == "ij":
        cur = pl.BlockSpec((tm, width), lambda i, j: (i, col_of(j)))
        prev = pl.BlockSpec((8, width), lambda i, j: (jnp.maximum(i * hb - 1, 0), col_of(j)))
        nxt = pl.BlockSpec((8, width), lambda i, j: (jnp.minimum((i + 1) * hb, last), col_of(j)))
    else:
        cur = pl.BlockSpec((tm, width), lambda j, i: (i, col_of(j)))
        prev = pl.BlockSpec((8, width), lambda j, i: (jnp.maximum(i * hb - 1, 0), col_of(j)))
        nxt = pl.BlockSpec((8, width), lambda j, i: (jnp.minimum((i + 1) * hb, last), col_of(j)))
    return [prev, cur, nxt]


def _extend(prev_ref, cur_ref, next_ref, i, n_i):
    prev = jnp.where(i > 0, prev_ref[...], 0.0)
    nxt = jnp.where(i < n_i - 1, next_ref[...], 0.0)
    return jnp.concatenate([prev, cur_ref[...], nxt], axis=0)


def _conv_fwd(rest, conv_w, conv_b, T):
    tm = 512
    n_i = T // tm
    n = tm + 16

    def body(p_ref, c_ref, n_ref, w_ref, b_ref, o_ref):
        i = pl.program_id(0)
        ext = _extend(p_ref, c_ref, n_ref, i, n_i)
        acc = jnp.zeros((tm, 512), _F32) + b_ref[...]
        for j in range(CONV_W):
            acc = acc + w_ref[j:j + 1, :] * pltpu.roll(ext, (2 - j) % n, 0)[8:8 + tm]
        o_ref[...] = _silu(acc)

    return pl.pallas_call(
        body, name="conv_fwd", grid=(n_i, 2),
        in_specs=_halo_specs(tm, 512, lambda j: 1 + j, T, "ij")
        + [pl.BlockSpec((8, 512), lambda i, j: (0, j)), pl.BlockSpec((1, 512), lambda i, j: (0, j))],
        out_specs=pl.BlockSpec((tm, 512), lambda i, j: (i, j)),
        out_shape=jax.ShapeDtypeStruct((T, 2 * ML_W), _F32),
        compiler_params=_cparams(("arbitrary", "arbitrary")),
    )(rest, rest, rest, conv_w, conv_b)


def _conv_bwd(rest, conv_w, conv_b, da_f, da_b, T):
    tm = 512
    n_i = T // tm
    n = tm + 16

    def body(up, uc, un, fp, fc, fn, bp, bc, bn, w_ref, b_ref, du_ref, dw_ref):
        i = pl.program_id(1)
        ext_u = _extend(up, uc, un, i, n_i)
        ext_da = _extend(fp, fc, fn, i, n_i) + _extend(bp, bc, bn, i, n_i)
        shifted = [pltpu.roll(ext_u, (2 - j) % n, 0) for j in range(CONV_W)]
        pre = jnp.zeros((n, 512), _F32) + b_ref[...]
        for j in range(CONV_W):
            pre = pre + w_ref[j:j + 1, :] * shifted[j]
        gidx = i * tm - 8 + lax.broadcasted_iota(jnp.int32, (n, 1), 0)
        dpre = jnp.where((gidx >= 0) & (gidx < T), ext_da * _dsilu(pre), 0.0)
        du = jnp.zeros((tm, 512), _F32)
        for j in range(CONV_W):
            du = du + w_ref[j:j + 1, :] * pltpu.roll(dpre, (j - 2) % n, 0)[8:8 + tm]
        du_ref[...] = du.astype(_BF16)
        dpc = dpre[8:8 + tm]
        parts = [jnp.sum(dpc * shifted[j][8:8 + tm], axis=0, keepdims=True) for j in range(CONV_W)]
        parts.append(jnp.sum(dpc, axis=0, keepdims=True))
        parts.append(jnp.zeros((2, 512), _F32))
        upd = jnp.concatenate(parts, axis=0)

        @pl.when(i == 0)
        def _():
            dw_ref[...] = upd

        @pl.when(i > 0)
        def _():
            dw_ref[...] += upd

    return pl.pallas_call(
        body, name="conv_bwd", grid=(2, n_i),
        in_specs=_halo_specs(tm, 512, lambda j: 1 + j, T, "ji")
        + _halo_specs(tm, 512, lambda j: j, T, "ji") + _halo_specs(tm, 512, lambda j: j, T, "ji")
        + [pl.BlockSpec((8, 512), lambda j, i: (0, j)), pl.BlockSpec((1, 512), lambda j, i: (0, j))],
        out_specs=(pl.BlockSpec((tm, 512), lambda j, i: (i, j)), pl.BlockSpec((8, 512), lambda j, i: (0, j))),
        out_shape=(jax.ShapeDtypeStruct((T, 2 * ML_W), _BF16), jax.ShapeDtypeStruct((8, 2 * ML_W), _F32)),
        compiler_params=_cparams(("arbitrary", "arbitrary")),
    )(rest, rest, rest, da_f, da_f, da_f, da_b, da_b, da_b, conv_w, conv_b)


def _scan_rows(x, suffix):
    L = x.shape[0]
    row = lax.broadcasted_iota(jnp.int32, (L, 1), 0)
    step = 1
    while step < L:
        if suffix:
            x = x + jnp.where(row < L - step, pltpu.roll(x, L - step, 0), 0.0)
        else:
            x = x + jnp.where(row >= step, pltpu.roll(x, step, 0), 0.0)
        step *= 2
    return x


def _ml_gates(gt, rev):
    L = gt.shape[0]
    ri = lax.broadcasted_iota(jnp.int32, (L, L), 0)
    ci = lax.broadcasted_iota(jnp.int32, (L, L), 1)
    mask = (ci >= ri) if rev else (ci <= ri)
    lf = jnp.minimum(gt, 0.0) - jnp.log(1.0 + jnp.exp(-jnp.abs(gt)))
    b = _scan_rows(lf, suffix=rev)
    return mask, b, b.T, gt.T


def _ml_head_gates(gt, gates, head, rev):
    _, b, b_t, gt_t = gates
    ci = (8 if rev else 0) + head
    cf = ci + ML_HEADS
    last = 0 if rev else gt.shape[0] - 1
    return dict(icol=gt[:, ci:ci + 1], b_col=b[:, cf:cf + 1], b_row=b_t[cf:cf + 1, :],
                i_row=gt_t[ci:ci + 1, :], bl=b[last:last + 1, cf:cf + 1])


def _ml_chunk(q, k, v, hg, mask, C, n, m, saved=None):
    icol, b_col, b_row, bl = hg["icol"], hg["b_col"], hg["b_row"], hg["bl"]
    if saved is None:
        dlog = jnp.where(mask, b_col - b_row + hg["i_row"], NEG)
        m_t = jnp.maximum(b_col + m, jnp.max(dlog, axis=1, keepdims=True))
        dm = jnp.exp(dlog - m_t)
    else:
        dm, m_t = saved[0].astype(_F32), saved[1]
    ks = k * (ML_HEAD_DIM ** -0.5)
    qb, kb, vb = q.astype(_BF16), ks.astype(_BF16), v.astype(_BF16)
    s = _nt(qb, kb) * dm
    g = jnp.exp(b_col + m - m_t)
    qc = _nt(qb, C.astype(_BF16))
    num = _nn(s.astype(_BF16), vb) + g * qc
    qn = jnp.sum(q * n, axis=1, keepdims=True)
    den = jnp.sum(s, axis=1, keepdims=True) + g * qn
    e_m = jnp.exp(-m_t)
    nrm = jnp.maximum(jnp.abs(den), e_m)
    h = num / nrm
    a_col = bl - b_col + icol
    m_new = jnp.maximum(bl + m, jnp.max(a_col, axis=0, keepdims=True))
    decay = jnp.exp(bl + m - m_new)
    w = jnp.exp(a_col - m_new)
    c_new = decay * C + _tn((w * v).astype(_BF16), kb)
    n_new = decay * n + jnp.sum(w * ks, axis=0, keepdims=True)
    aux = dict(dm=dm, m_t=m_t, ks=ks, qb=qb, kb=kb, vb=vb, s=s, g=g, qc=qc, qn=qn,
               den=den, e_m=e_m, nrm=nrm, decay=decay, w=w)
    return h, c_new, n_new, m_new, aux


def _mlstm_fwd(qk_act, rest, T, rev):
    tb = ML_CB * ML_CHUNK
    nblk = T // tb
    nc = T // ML_CHUNK
    bi = (lambda i: nblk - 1 - i) if rev else (lambda i: i)

    def body(q_ref, k_ref, v_ref, g_ref, h_ref, cs_ref, ns_ref, ms_ref, dm_ref, mt_ref, c_scr, n_scr, m_scr):
        @pl.when(pl.program_id(0) == 0)
        def _():
            c_scr[...] = jnp.zeros_like(c_scr)
            n_scr[...] = jnp.zeros_like(n_scr)
            m_scr[...] = jnp.zeros_like(m_scr)

        def step(j, carry):
            c = (ML_CB - 1 - j) if rev else j
            r0 = pl.multiple_of(c * ML_CHUNK, ML_CHUNK)
            gt = g_ref[pl.ds(r0, ML_CHUNK), :]
            gates = _ml_gates(gt, rev)
            lane = lax.broadcasted_iota(jnp.int32, (1, 128), 1)
            mt_tile = jnp.zeros((ML_CHUNK, 128), _F32)
            for hd in range(ML_HEADS):
                cols = slice(hd * ML_HEAD_DIM, (hd + 1) * ML_HEAD_DIM)
                C = c_scr[hd]
                n = n_scr[hd:hd + 1, :]
                mrow = m_scr[hd:hd + 1, :]
                cs_ref[c, hd] = C
                ns_ref[c, hd:hd + 1, :] = n
                ms_ref[c, hd:hd + 1, :] = mrow
                h, c_new, n_new, m_new, a = _ml_chunk(
                    q_ref[pl.ds(r0, ML_CHUNK), cols], k_ref[pl.ds(r0, ML_CHUNK), cols],
                    v_ref[pl.ds(r0, ML_CHUNK), cols], _ml_head_gates(gt, gates, hd, rev), gates[0],
                    C, n, mrow[:, 0:1])
                h_ref[pl.ds(r0, ML_CHUNK), cols] = h
                dm_ref[c, hd] = a["dm"].astype(_BF16)
                mt_tile = jnp.where(lane == hd, a["m_t"], mt_tile)
                c_scr[hd] = c_new
                n_scr[hd:hd + 1, :] = n_new
                m_scr[hd:hd + 1, :] = jnp.broadcast_to(m_new, (1, 128))
            mt_ref[c] = mt_tile
            return carry

        lax.fori_loop(0, ML_CB, step, 0)

    return pl.pallas_call(
        body, name="mlstm_fwd_rev" if rev else "mlstm_fwd", grid=(nblk,),
        in_specs=[pl.BlockSpec((tb, ML_W), lambda i: (bi(i), 0)),
                  pl.BlockSpec((tb, ML_W), lambda i: (bi(i), 1)),
                  pl.BlockSpec((tb, ML_W), lambda i: (bi(i), 3)),
                  pl.BlockSpec((tb, 128), lambda i: (bi(i), GATE_COL // 128))],
        out_specs=(pl.BlockSpec((tb, ML_W), lambda i: (bi(i), 0)),
                   pl.BlockSpec((ML_CB, ML_HEADS, 128, 128), lambda i: (bi(i), 0, 0, 0)),
                   pl.BlockSpec((ML_CB, ML_HEADS, 128), lambda i: (bi(i), 0, 0)),
                   pl.BlockSpec((ML_CB, ML_HEADS, 128), lambda i: (bi(i), 0, 0)),
                   pl.BlockSpec((ML_CB, ML_HEADS, ML_CHUNK, ML_CHUNK), lambda i: (bi(i), 0, 0, 0)),
                   pl.BlockSpec((ML_CB, ML_CHUNK, 128), lambda i: (bi(i), 0, 0))),
        out_shape=(jax.ShapeDtypeStruct((T, ML_W), _F32),
                   jax.ShapeDtypeStruct((nc, ML_HEADS, 128, 128), _F32),
                   jax.ShapeDtypeStruct((nc, ML_HEADS, 128), _F32),
                   jax.ShapeDtypeStruct((nc, ML_HEADS, 128), _F32),
                   jax.ShapeDtypeStruct((nc, ML_HEADS, ML_CHUNK, ML_CHUNK), _BF16),
                   jax.ShapeDtypeStruct((nc, ML_CHUNK, 128), _F32)),
        scratch_shapes=[pltpu.VMEM((ML_HEADS, 128, 128), _F32), pltpu.VMEM((8, 128), _F32),
                        pltpu.VMEM((8, 128), _F32)],
        compiler_params=_cparams(("arbitrary",)),
    )(qk_act, qk_act, rest, rest)


def _mlstm_bwd(qk_act, rest, d_h, saved, T, rev):
    tb = ML_CB * ML_CHUNK
    nblk = T // tb
    bi = (lambda i: i) if rev else (lambda i: nblk - 1 - i)

    def body(q_ref, k_ref, v_ref, g_ref, dh_ref, cs_ref, ns_ref, ms_ref, dm_ref, mt_ref,
             dqk_ref, dv_ref, dg_ref, dc_scr, dn_scr):
        @pl.when(pl.program_id(0) == 0)
        def _():
            dc_scr[...] = jnp.zeros_like(dc_scr)
            dn_scr[...] = jnp.zeros_like(dn_scr)

        def step(j, carry):
            c = j if rev else (ML_CB - 1 - j)
            r0 = pl.multiple_of(c * ML_CHUNK, ML_CHUNK)
            gt = g_ref[pl.ds(r0, ML_CHUNK), :]
            gates = _ml_gates(gt, rev)
            mask = gates[0]
            lane = lax.broadcasted_iota(jnp.int32, (1, 128), 1)
            sub = lax.broadcasted_iota(jnp.int32, (128, 1), 0)
            db_t = jnp.zeros((ML_CHUNK, 128), _F32)
            da_t = jnp.zeros((ML_CHUNK, 128), _F32)
            cs_rows = jnp.zeros((128, ML_CHUNK), _F32)
            dbl_t = jnp.zeros((1, 128), _F32)
            for hd in range(ML_HEADS):
                cols = slice(hd * ML_HEAD_DIM, (hd + 1) * ML_HEAD_DIM)
                ci = (8 if rev else 0) + hd
                cf = ci + ML_HEADS
                q = q_ref[pl.ds(r0, ML_CHUNK), cols]
                k = k_ref[pl.ds(r0, ML_CHUNK), cols]
                v = v_ref[pl.ds(r0, ML_CHUNK), cols]
                C = cs_ref[c, hd]
                n = ns_ref[c, hd:hd + 1, :]
                m = ms_ref[c, hd:hd + 1, :][:, 0:1]
                dcn = dc_scr[hd]
                dnn = dn_scr[hd:hd + 1, :]
                h, _, _, _, a = _ml_chunk(q, k, v, _ml_head_gates(gt, gates, hd, rev), mask, C, n, m,
                                          saved=(dm_ref[c, hd], mt_ref[c][:, hd:hd + 1]))
                d_hv = dh_ref[pl.ds(r0, ML_CHUNK), cols]
                g, s, w, ks = a["g"], a["s"], a["w"], a["ks"]
                qb, kb, vb = a["qb"], a["kb"], a["vb"]
                dnum = d_hv / a["nrm"]
                hdot = jnp.sum(d_hv * h, axis=1, keepdims=True)
                dden = jnp.where(jnp.abs(a["den"]) >= a["e_m"], -hdot / a["nrm"] * jnp.sign(a["den"]), 0.0)
                dnb = dnum.astype(_BF16)
                d_s = _nt(dnb, vb) + dden
                r = d_s * s
                dsqk = (d_s * a["dm"]).astype(_BF16)
                cb = C.astype(_BF16)
                dq = _nn(dsqk, kb) + g * _nn(dnb, cb) + (dden * g) * n
                dk = _tn(dsqk, qb)
                dv = _tn(s.astype(_BF16), dnb)
                dg = jnp.sum(dnum * a["qc"], axis=1, keepdims=True) + dden * a["qn"]
                db_col = jnp.sum(r, axis=1, keepdims=True) + dg * g
                cs_rows = cs_rows + jnp.where((sub == ci) | (sub == cf), jnp.sum(r, axis=0, keepdims=True), 0.0)
                dc_chunk = _tn((g * dnum).astype(_BF16), qb)
                dn_chunk = jnp.sum((dden * g) * q, axis=0, keepdims=True)
                dcb = dcn.astype(_BF16)
                vdc = _nn(vb, dcb)
                kdc = _nt(kb, dcb)
                dw = jnp.sum(vdc * ks, axis=1, keepdims=True) + jnp.sum(ks * dnn, axis=1, keepdims=True)
                dv = dv + w * kdc
                dk = dk + w * vdc + w * dnn
                da = dw * w
                ddecay = (jnp.sum(jnp.sum(dcn * C, axis=1, keepdims=True), axis=0, keepdims=True)
                          + jnp.sum(dnn * n, axis=1, keepdims=True))
                dbl = ddecay * a["decay"] + jnp.sum(da, axis=0, keepdims=True)
                db_t = db_t + jnp.where(lane == cf, db_col - da, 0.0)
                da_t = da_t + jnp.where(lane == ci, da, 0.0)
                dbl_t = dbl_t + jnp.where(lane == cf, dbl, 0.0)
                dc_scr[hd] = dc_chunk + a["decay"] * dcn
                dn_scr[hd:hd + 1, :] = dn_chunk + a["decay"] * dnn
                dqk_ref[pl.ds(r0, ML_CHUNK), cols] = dq
                dqk_ref[pl.ds(r0, ML_CHUNK), slice(ML_W + hd * 128, ML_W + (hd + 1) * 128)] = dk * (ML_HEAD_DIM ** -0.5)
                dv_ref[pl.ds(r0, ML_CHUNK), cols] = dv.astype(_BF16)
            lo = 8 if rev else 0
            is_i = (lane >= lo) & (lane < lo + ML_HEADS)
            is_f = (lane >= lo + ML_HEADS) & (lane < lo + 2 * ML_HEADS)
            cs_t = cs_rows.T
            db_all = db_t - jnp.where(is_f, cs_t, 0.0)
            dlf = _scan_rows(db_all, suffix=not rev) + dbl_t
            dg_ref[pl.ds(r0, ML_CHUNK), :] = (da_t + jnp.where(is_i, cs_t, 0.0)
                                               + jnp.where(is_f, dlf * _sigmoid(-gt), 0.0))
            return carry

        lax.fori_loop(0, ML_CB, step, 0)

    return pl.pallas_call(
        body, name="mlstm_bwd_rev" if rev else "mlstm_bwd", grid=(nblk,),
        in_specs=[pl.BlockSpec((tb, ML_W), lambda i: (bi(i), 0)),
                  pl.BlockSpec((tb, ML_W), lambda i: (bi(i), 1)),
                  pl.BlockSpec((tb, ML_W), lambda i: (bi(i), 3)),
                  pl.BlockSpec((tb, 128), lambda i: (bi(i), GATE_COL // 128)),
                  pl.BlockSpec((tb, ML_W), lambda i: (bi(i), 0)),
                  pl.BlockSpec((ML_CB, ML_HEADS, 128, 128), lambda i: (bi(i), 0, 0, 0)),
                  pl.BlockSpec((ML_CB, ML_HEADS, 128), lambda i: (bi(i), 0, 0)),
                  pl.BlockSpec((ML_CB, ML_HEADS, 128), lambda i: (bi(i), 0, 0)),
                  pl.BlockSpec((ML_CB, ML_HEADS, ML_CHUNK, ML_CHUNK), lambda i: (bi(i), 0, 0, 0)),
                  pl.BlockSpec((ML_CB, ML_CHUNK, 128), lambda i: (bi(i), 0, 0))],
        out_specs=(pl.BlockSpec((tb, 2 * ML_W), lambda i: (bi(i), 0)),
                   pl.BlockSpec((tb, ML_W), lambda i: (bi(i), 0)),
                   pl.BlockSpec((tb, 128), lambda i: (bi(i), 0))),
        out_shape=(jax.ShapeDtypeStruct((T, 2 * ML_W), _F32), jax.ShapeDtypeStruct((T, ML_W), _BF16),
                   jax.ShapeDtypeStruct((T, 128), _F32)),
        scratch_shapes=[pltpu.VMEM((ML_HEADS, 128, 128), _F32), pltpu.VMEM((8, 128), _F32)],
        compiler_params=_cparams(("arbitrary",)),
    )(qk_act, qk_act, rest, rest, d_h, *saved)


def _post(x, target, o_na, rest, h_f, h_b, gate, ml_norm_w, final_w, w_out_bf, T):
    tm = 256
    n_i = T // tm

    def body(x_ref, t_ref, o_ref, zna_ref, hf_ref, hb_ref, mo_ref, mz_ref, gate_ref, mw_ref, fw_ref, w_ref,
             dx1_ref, do_ref, dzna_ref, dh_ref, dmo_ref, dmz_ref, dwo_ref, vec_ref):
        i = pl.program_id(0)
        gate_v = gate_ref[...]
        fw = fw_ref[...]
        zna = zna_ref[...]
        o = o_ref[...]
        sig_zna = _sigmoid(zna)
        silu_zna = zna * sig_zna
        na_out = o * silu_zna
        hsum = hf_ref[...] + hb_ref[...]
        sg = _sigmoid(mo_ref[...])
        hm = hsum * sg
        mz = mz_ref[...]
        sig_mz = _sigmoid(mz)
        smz = mz * sig_mz
        dsilu_mz = sig_mz * (1.0 + mz * (1.0 - sig_mz))
        hn_l, rstd_l, ml_l = [], [], []
        for hd in range(ML_HEADS):
            cols = slice(hd * 128, (hd + 1) * 128)
            hh = hm[:, cols]
            mu = jnp.mean(hh, axis=-1, keepdims=True)
            var = jnp.mean(jnp.square(hh - mu), axis=-1, keepdims=True)
            rstd = lax.rsqrt(var + EPS)
            hn = (hh - mu) * rstd
            hn_l.append(hn)
            rstd_l.append(rstd)
            ml_l.append(hn * mw_ref[:, cols] * smz[:, cols])
        mix = jnp.concatenate([na_out] + ml_l, axis=1).astype(_BF16)
        y = _nn(mix, w_ref[...])
        x1 = x_ref[...] + gate_v * y
        r = lax.rsqrt(jnp.mean(x1 * x1, axis=-1, keepdims=True) + EPS)
        xhat = x1 * r
        out = xhat * fw
        err = out - t_ref[...]
        loss = 0.5 * jnp.sum(jnp.sum(err * err, axis=1, keepdims=True), axis=0, keepdims=True) / D_MODEL
        dout = err * (1.0 / D_MODEL)
        dfw = jnp.sum(dout * xhat, axis=0, keepdims=True)
        dxhat = dout * fw
        dx1 = r * (dxhat - xhat * jnp.mean(dxhat * xhat, axis=-1, keepdims=True))
        dx1_ref[...] = dx1
        dgate = jnp.sum(dx1 * y, axis=0, keepdims=True)
        dy = (dx1 * gate_v).astype(_BF16)
        dmix = _nt(dy, w_ref[...])
        dwo = _tn(mix, dy)
        dna = dmix[:, :NA_W]
        do_ref[...] = (dna * silu_zna).astype(_BF16)
        dzna_ref[...] = (dna * o * (sig_zna * (1.0 + zna * (1.0 - sig_zna)))).astype(_BF16)
        dmw_l = []
        for hd in range(ML_HEADS):
            cols = slice(hd * 128, (hd + 1) * 128)
            dml = dmix[:, NA_W + hd * 128:NA_W + (hd + 1) * 128]
            hn = hn_l[hd]
            mwv = mw_ref[:, cols]
            dmz_ref[:, cols] = (dml * hn * mwv * dsilu_mz[:, cols]).astype(_BF16)
            dhn = dml * mwv * smz[:, cols]
            dmw_l.append(jnp.sum(dml * hn * smz[:, cols], axis=0, keepdims=True))
            dhm = rstd_l[hd] * (dhn - jnp.mean(dhn, axis=-1, keepdims=True)
                                - hn * jnp.mean(dhn * hn, axis=-1, keepdims=True))
            sgc = sg[:, cols]
            dh_ref[:, cols] = dhm * sgc
            dmo_ref[:, cols] = (dhm * hsum[:, cols] * sgc * (1.0 - sgc)).astype(_BF16)
        dmw = jnp.concatenate(dmw_l + [jnp.zeros((1, D_MODEL - ML_W), _F32)], axis=1)
        lane = lax.broadcasted_iota(jnp.int32, (1, D_MODEL), 1)
        vec = jnp.concatenate([dfw, dgate, dmw, jnp.where(lane == 0, loss, 0.0),
                               jnp.zeros((4, D_MODEL), _F32)], axis=0)

        @pl.when(i == 0)
        def _():
            dwo_ref[...] = dwo
            vec_ref[...] = vec

        @pl.when(i > 0)
        def _():
            dwo_ref[...] += dwo
            vec_ref[...] += vec

    tok = lambda w, j: pl.BlockSpec((tm, w), lambda i: (i, j))
    tok3 = pl.BlockSpec((None, tm, D_MODEL), lambda i: (0, i, 0))
    row = lambda w: pl.BlockSpec((1, w), lambda i: (0, 0))
    f32 = lambda w: jax.ShapeDtypeStruct((T, w), _F32)
    bf16 = lambda w: jax.ShapeDtypeStruct((T, w), _BF16)
    return pl.pallas_call(
        body, name="post", grid=(n_i,),
        in_specs=[tok3, tok3, tok(NA_W, 0), tok(NA_W, 0), tok(ML_W, 0), tok(ML_W, 0),
                  tok(ML_W, 4), tok(ML_W, 5), row(D_MODEL), row(ML_W), row(D_MODEL),
                  pl.BlockSpec((D_MODEL, D_MODEL), lambda i: (0, 0))],
        out_specs=(tok(D_MODEL, 0), tok(NA_W, 0), tok(NA_W, 0), tok(ML_W, 0), tok(ML_W, 0), tok(ML_W, 0),
                   pl.BlockSpec((D_MODEL, D_MODEL), lambda i: (0, 0)),
                   pl.BlockSpec((8, D_MODEL), lambda i: (0, 0))),
        out_shape=(f32(D_MODEL), bf16(NA_W), bf16(NA_W), f32(ML_W), bf16(ML_W),
                   bf16(ML_W), jax.ShapeDtypeStruct((D_MODEL, D_MODEL), _F32),
                   jax.ShapeDtypeStruct((8, D_MODEL), _F32)),
        compiler_params=_cparams(("arbitrary",)),
    )(x, target, o_na, rest, h_f, h_b, rest, rest, gate, ml_norm_w, final_w, w_out_bf)


def _section_specs(sections, tm):
    specs, args = [], []
    for _, width, parts in sections:
        for arr, cb in parts:
            specs.append(pl.BlockSpec((tm, width), functools.partial(lambda i, cb: (i, cb), cb=cb)))
            args.append(arr)
    return specs, args


def _section_values(sections, refs, dtype):
    vals, at = [], 0
    for _, _, parts in sections:
        v = refs[at][...]
        for r in refs[at + 1:at + len(parts)]:
            v = v.astype(_F32) + r[...].astype(_F32)
        at += len(parts)
        vals.append(v.astype(dtype))
    return vals


def _inproj_bwd_x(x, dx1, scale1p, norm_w, w_in_bf, sections, T):
    tm = 512
    sspecs, sargs = _section_specs(sections, tm)
    ns = len(sargs)

    def body(*refs):
        x_ref, dx1_ref, sc_ref, nw_ref, w_ref = refs[:5]
        srefs = refs[5:5 + ns]
        gx_ref, vec_ref = refs[5 + ns:]
        i = pl.program_id(0)
        vals = _section_values(sections, srefs, _BF16)
        dh = jnp.zeros((tm, D_MODEL), _F32)
        for (c0, width, _), val in zip(sections, vals):
            dh = dh + _nt(val, w_ref[:, c0:c0 + width])
        xv = x_ref[...]
        r = lax.rsqrt(jnp.mean(xv * xv, axis=-1, keepdims=True) + EPS)
        xhat = xv * r
        nw = nw_ref[...]
        dshift = jnp.sum(dh, axis=0, keepdims=True)
        dscale = jnp.sum(dh * xhat * nw, axis=0, keepdims=True)
        dhpre = dh * sc_ref[...]
        dnw = jnp.sum(dhpre * xhat, axis=0, keepdims=True)
        dxhat = dhpre * nw
        gx_ref[...] = dx1_ref[...] + r * (dxhat - xhat * jnp.mean(dxhat * xhat, axis=-1, keepdims=True))
        vec = jnp.concatenate([dshift, dscale, dnw, jnp.zeros((5, D_MODEL), _F32)], axis=0)

        @pl.when(i == 0)
        def _():
            vec_ref[...] = vec

        @pl.when(i > 0)
        def _():
            vec_ref[...] += vec

    row = pl.BlockSpec((1, D_MODEL), lambda i: (0, 0))
    tok = pl.BlockSpec((tm, D_MODEL), lambda i: (i, 0))
    tok3 = pl.BlockSpec((None, tm, D_MODEL), lambda i: (0, i, 0))
    return pl.pallas_call(
        body, name="inproj_bwd_x", grid=(T // tm,),
        in_specs=[tok3, tok, row, row,
                  pl.BlockSpec((D_MODEL, IN_PAD), lambda i: (0, 0), pipeline_mode=pl.Buffered(1))] + sspecs,
        out_specs=(tok3, pl.BlockSpec((8, D_MODEL), lambda i: (0, 0))),
        out_shape=(jax.ShapeDtypeStruct((1, T, D_MODEL), _F32), jax.ShapeDtypeStruct((8, D_MODEL), _F32)),
        compiler_params=_cparams(("arbitrary",)),
    )(x, dx1, scale1p, norm_w, w_in_bf, *sargs)


def _inproj_bwd_w(h_t, sections, T):
    tm = 1024
    n_i = T // tm
    sspecs, sargs = _section_specs(sections, tm)
    ns = len(sargs)

    def body(*refs):
        h_ref = refs[0]
        srefs = refs[1:1 + ns]
        dw_ref, db_ref, acc, sem = refs[1 + ns:]
        i = pl.program_id(0)

        @pl.when(i == 0)
        def _():
            acc[...] = jnp.zeros_like(acc)
            db_ref[...] = jnp.zeros_like(db_ref)

        hv = h_ref[...]
        for (c0, width, _), v in zip(sections, _section_values(sections, srefs, _F32)):
            acc[:, c0:c0 + width] += _nn(hv, v.astype(_BF16))
            db_ref[0:1, c0:c0 + width] += jnp.sum(v, axis=0, keepdims=True)

        @pl.when(i == n_i - 1)
        def _():
            cp = pltpu.make_async_copy(acc, dw_ref, sem)
            cp.start()
            cp.wait()

    return pl.pallas_call(
        body, name="inproj_bwd_w", grid=(n_i,),
        in_specs=[pl.BlockSpec((D_MODEL, tm), lambda i: (0, i))] + sspecs,
        out_specs=(pl.BlockSpec(memory_space=pl.ANY), pl.BlockSpec((8, IN_PAD), lambda i: (0, 0))),
        out_shape=(jax.ShapeDtypeStruct((D_MODEL, IN_PAD), _F32), jax.ShapeDtypeStruct((8, IN_PAD), _F32)),
        scratch_shapes=[pltpu.VMEM((D_MODEL, IN_PAD), _F32), pltpu.SemaphoreType.DMA],
        compiler_params=_cparams(("arbitrary",)),
    )(h_t, *sargs)


def _adamw_math(w, g, m, v):
    m = ADAM_B1 * m + (1.0 - ADAM_B1) * g
    v = ADAM_B2 * v + (1.0 - ADAM_B2) * jnp.square(g)
    m_hat = m / (1.0 - ADAM_B1 ** ADAM_STEP)
    v_hat = v / (1.0 - ADAM_B2 ** ADAM_STEP)
    delta = -ADAM_LR * (m_hat / (jnp.sqrt(v_hat) + ADAM_EPS) + ADAM_WD * w)
    return delta, m, v


def _adamw_slots(w, m, v, slots, tr, name, own=None):
    R, C = w.shape
    extra = [] if own is None else [own]

    def body(w_ref, m_ref, v_ref, s_ref, *refs):
        g_ref, d_ref, nm_ref, nv_ref = refs[len(extra):]
        g = s_ref[0].astype(_F32)
        for k in range(1, N_DEV):
            g = g + s_ref[k].astype(_F32)
        if extra:
            g = g + refs[0][...].astype(_F32)
        g_ref[...] = g
        d_ref[...], nm_ref[...], nv_ref[...] = _adamw_math(w_ref[...], g, m_ref[...], v_ref[...])

    blk = pl.BlockSpec((tr, C), lambda i: (i, 0))
    return pl.pallas_call(
        body, name=name, grid=(R // tr,),
        in_specs=[blk, blk, blk, pl.BlockSpec((N_DEV, tr, C), lambda i: (0, i, 0))] + [blk] * len(extra),
        out_specs=(blk, blk, blk, blk),
        out_shape=tuple(jax.ShapeDtypeStruct((R, C), _F32) for _ in range(4)),
        compiler_params=_cparams(("arbitrary",)),
    )(w, m, v, slots, *extra)


def _w_ada_update(c_all, dmod_my, w, m, v):
    def body(c_ref, d_ref, w_ref, m_ref, v_ref, g_ref, dl_ref, nm_ref, nv_ref):
        g = lax.dot_general(_silu(c_ref[...]), d_ref[...], (((0,), (0,)), ((), ())),
                            precision=_HI, preferred_element_type=_F32)
        g_ref[...] = g
        dl_ref[...], nm_ref[...], nv_ref[...] = _adamw_math(w_ref[...], g, m_ref[...], v_ref[...])

    return pl.pallas_call(
        body, name="w_ada_update",
        out_shape=tuple(jax.ShapeDtypeStruct(w.shape, _F32) for _ in range(4)),
        compiler_params=_cparams(),
    )(c_all, dmod_my, w, m, v)


_PACK = (("b_ada", 3072, 3072), ("norm_w", 1024, 1024), ("b_in", IN_W, IN_PAD), ("conv_w", 5120, 5120),
         ("conv_b", 1024, 1024), ("rpb", 3720, 3840), ("ml_norm_w", 512, 512), ("final_norm_w", 1024, 1024),
         ("loss", 1, 128))
_PACK_OFF = {}
_off = 0
for _name, _len, _pad in _PACK:
    _PACK_OFF[_name] = (_off, _len)
    _off += _pad
_PACK_LEN = _off


def _pack(parts):
    cols = []
    for name, length, pad in _PACK:
        vec = parts[name].reshape(-1).astype(_F32)
        cols.append(jnp.pad(vec, (0, pad - length)))
    return jnp.concatenate(cols).reshape(1, _PACK_LEN)


def _unpack(vec, name, shape):
    off, length = _PACK_OFF[name]
    return vec.reshape(-1)[off:off + length].reshape(shape)


def kernel(x, c, w_ada, b_ada, norm_w, w_in, b_in, conv_w, conv_b, rpb, ml_norm_w, w_out, final_norm_w, loss_target, m_w_ada, m_b_ada, m_norm_w, m_w_in, m_b_in, m_conv_w, m_conv_b, m_rpb, m_ml_norm_w, m_w_out, m_final_norm_w, v_w_ada, v_b_ada, v_norm_w, v_w_in, v_b_in, v_conv_w, v_conv_b, v_rpb, v_ml_norm_w, v_w_out, v_final_norm_w):
    T = x.shape[1]
    rows = T // GRID_W
    me = 4 * lax.axis_index("x") + 2 * lax.axis_index("y") + lax.axis_index("c")
    n_in = w_in.shape[2]
    n_ada = w_ada.shape[2]
    n_cw = conv_w.shape[2]
    n_wo = w_out.shape[1]

    w_in_my, w_out_my = w_in[0].astype(_BF16), w_out[0].astype(_BF16)
    first_leg = (1,) + _CHIP_PEERS
    g_conv_w, g_c = _exchange([conv_w[0], c], [False] * 2, "gather_small")
    b_in_pad = jnp.pad(b_in, ((0, 0), (0, IN_PAD - IN_W)))
    conv_w_full = jnp.pad(g_conv_w.transpose(1, 0, 2).reshape(CONV_W, N_DEV * n_cw), ((0, 3), (0, 0)))
    c_all = g_c.reshape(N_DEV, D_MODEL)

    b_ada_my = lax.dynamic_slice(b_ada, (0, me * n_ada), (1, n_ada))
    (mod_slots,) = _exchange([_mod_part(c_all, w_ada[0], b_ada_my)], [False], "gather_mod")
    mod = lax.dynamic_index_in_dim(mod_slots, me, axis=1, keepdims=False).reshape(1, 3 * D_MODEL)
    shift, scale, gate = mod[:, :D_MODEL], mod[:, D_MODEL:2 * D_MODEL], mod[:, 2 * D_MODEL:]
    idle = (mod_slots[0, 0:1, 0:1] != mod_slots[0, 0:1, 0:1]).astype(_BF16)
    start_in = _scatter_start([w_in_my + idle], "w_in_start", scatter=False, ks=first_leg)
    scale1p = 1.0 + scale + start_in[-1][0:1, 0:1]
    h_tok, h_bf = _prenorm(x, scale1p, shift, norm_w)
    bias = _na_bias_tables(rpb[0], rows)

    def own_slot(land, own):
        return lax.dynamic_update_slice(land, own[None], (me,) + (0,) * own.ndim)

    def gathered(started, after, name):
        (own,), (land,) = _scatter_wait(started, after, name, scatter=False)
        return own_slot(land, own)

    (w_in_own,), (w_in_land,) = _scatter_wait(start_in, bias[0, 0, :8, :128] + h_tok[:8, :128].astype(_F32),
                                              "w_in_wait", scatter=False, ks=first_leg)
    g_w_in = own_slot(_relay_wait(_relay_start(w_in_land, "w_in_relay_start"), "w_in_relay_wait"), w_in_own)
    w_in_full = g_w_in.transpose(1, 0, 2).reshape(D_MODEL, N_DEV * n_in)
    w_in_bf = jnp.pad(w_in_full, ((0, 0), (0, IN_PAD - IN_W)))
    arrived = (g_w_in[0, 0:1, 0:1] != g_w_in[0, 0:1, 0:1]).astype(_BF16)
    start_out = _scatter_start([w_out_my + arrived], "w_out_start", scatter=False)

    qkv, rest = _inproj_fwd(h_tok, w_in_bf, b_in_pad)
    o_na, lse = _na_fwd(qkv, bias, T)
    qk_act = _conv_fwd(rest, conv_w_full, conv_b, T)
    h_f, *saved_f = _mlstm_fwd(qk_act, rest, T, False)
    h_b, *saved_b = _mlstm_fwd(qk_act, rest, T, True)

    w_out_bf = gathered(start_out, saved_b[2], "w_out_wait").reshape(N_DEV * n_wo, D_MODEL)
    dx1, d_o, dz_na, d_h, d_mo, d_mz, dwo, pvec = _post(
        x, loss_target, o_na, rest, h_f, h_b, gate, ml_norm_w, final_norm_w.reshape(1, D_MODEL), w_out_bf, T)

    dq_na, dk_na, dv_na, dbias = _na_bwd(qkv, bias, o_na, d_o, lse, T)
    d_rpb = _rpb_grad(dbias, rows)
    dqk_f, dv_f, dg_f = _mlstm_bwd(qk_act, rest, d_h, saved_f, T, False)
    dqk_b, dv_b, dg_b = _mlstm_bwd(qk_act, rest, d_h, saved_b, T, True)
    d_u, dconv = _conv_bwd(rest, conv_w_full, conv_b, dqk_f, dqk_b, T)

    sections = [(0, 512, [(dq_na, 0)]), (512, 512, [(dk_na, 0)]), (1024, 512, [(dv_na, 0)]),
                (1536, 512, [(dz_na, 0)]), (2048, 512, [(d_u, 0)]), (2560, 512, [(d_u, 1)]),
                (3072, 512, [(dv_f, 0), (dv_b, 0)]), (3584, 512, [(d_mo, 0)]), (4096, 512, [(d_mz, 0)]),
                (4608, 128, [(dg_f, 0), (dg_b, 0)])]
    dw_pad, db_pad = _inproj_bwd_w(h_bf, sections, T)
    db_in = db_pad[0, :IN_W]

    dw_blocks = dw_pad[:, :IN_W].astype(_BF16).reshape(D_MODEL, N_DEV, n_in).transpose(1, 0, 2)
    dwo_blocks = dwo.astype(_BF16).reshape(N_DEV, n_wo, D_MODEL)
    started = _scatter_start([dw_blocks, dwo_blocks], "grads_start")
    grad_x, xvec = _inproj_bwd_x(x, dx1, scale1p + started[-1][0:1, 0:1], norm_w, w_in_bf, sections, T)
    (dw_blocks, dwo_blocks), (s_w_in, s_w_out) = _scatter_wait(started, xvec, "grads_wait")

    small = _pack({
        "b_ada": jnp.concatenate([xvec[0], xvec[1], pvec[1]]),
        "norm_w": xvec[2], "b_in": db_in, "conv_w": dconv[:CONV_W], "conv_b": dconv[CONV_W],
        "rpb": d_rpb, "ml_norm_w": pvec[2, :ML_W], "final_norm_w": pvec[0], "loss": pvec[3, :1]})
    (s_small,) = _exchange([small], [False], "exchange_small")

    own = lambda blocks: lax.dynamic_index_in_dim(blocks, me, axis=0, keepdims=False)
    g_w_in_s, d_w_in, nm_w_in, nv_w_in = _adamw_slots(
        w_in[0], m_w_in[0], v_w_in[0], s_w_in, 128, "adamw_w_in", own=own(dw_blocks))
    g_w_out_s, d_w_out, nm_w_out, nv_w_out = _adamw_slots(
        w_out[0], m_w_out[0], v_w_out[0], s_w_out, n_wo, "adamw_w_out", own=own(dwo_blocks))
    dmod_all = s_small[:, 0, :3 * D_MODEL]
    dmod_my = lax.dynamic_slice(dmod_all, (0, me * n_ada), (N_DEV, n_ada))
    g_w_ada, d_w_ada, nm_w_ada, nv_w_ada = _w_ada_update(c_all, dmod_my, w_ada[0], m_w_ada[0], v_w_ada[0])

    def embed(shard):
        return lax.dynamic_update_slice(jnp.zeros((CONV_W, N_DEV * n_cw), _F32), shard[0], (0, me * n_cw))

    zero1 = jnp.zeros((1,), _F32)
    packed = lambda b_a, n_w, b_i, c_w, c_b, rp, mn, fn: _pack({
        "b_ada": b_a, "norm_w": n_w, "b_in": b_i, "conv_w": embed(c_w), "conv_b": c_b, "rpb": rp,
        "ml_norm_w": mn, "final_norm_w": fn, "loss": zero1})
    pw = packed(b_ada, norm_w, b_in, conv_w, conv_b, rpb, ml_norm_w, final_norm_w)
    pm = packed(m_b_ada, m_norm_w, m_b_in, m_conv_w, m_conv_b, m_rpb, m_ml_norm_w, m_final_norm_w)
    pv = packed(v_b_ada, v_norm_w, v_b_in, v_conv_w, v_conv_b, v_rpb, v_ml_norm_w, v_final_norm_w)
    sg, sd, sm, sv = _adamw_slots(pw, pm, pv, s_small, 1, "adamw_small")

    def small_outs(vec):
        cw = lax.dynamic_slice(_unpack(vec, "conv_w", (CONV_W, N_DEV * n_cw)), (0, me * n_cw), (CONV_W, n_cw))
        return dict(b_ada=_unpack(vec, "b_ada", b_ada.shape), norm_w=_unpack(vec, "norm_w", norm_w.shape),
                    b_in=_unpack(vec, "b_in", b_in.shape), conv_w=cw[None],
                    conv_b=_unpack(vec, "conv_b", conv_b.shape), rpb=_unpack(vec, "rpb", rpb.shape),
                    ml_norm_w=_unpack(vec, "ml_norm_w", ml_norm_w.shape),
                    final_norm_w=_unpack(vec, "final_norm_w", final_norm_w.shape))

    loss = _unpack(sg, "loss", ())
    order = ("w_ada", "b_ada", "norm_w", "w_in", "b_in", "conv_w", "conv_b", "rpb", "ml_norm_w", "w_out",
             "final_norm_w")
    outs = []
    for vec, big in ((sg, (g_w_ada, g_w_in_s, g_w_out_s)), (sd, (d_w_ada, d_w_in, d_w_out)),
                     (sm, (nm_w_ada, nm_w_in, nm_w_out)), (sv, (nv_w_ada, nv_w_in, nv_w_out))):
        group = small_outs(vec)
        group.update(w_ada=big[0][None], w_in=big[1][None], w_out=big[2][None])
        outs.extend(group[name] for name in order)
    return (loss, grad_x, *outs)
```
